```python
import math
import jax, jax.numpy as jnp
from jax import lax
import numpy as np

D_MODEL = 2048
BATCH = 8
SEQ = 4096
DEPTH = 1

CHUNK = 64
D_CONV = 1024
CONV_HEADS = 16
CONV_WIDTH = 3
D_SSM = 1024
SSM_GROUP = 16
SSM_GROUPS = D_SSM // SSM_GROUP
SSM_STATE = 64
D_MIX = D_CONV + D_SSM
N_IN = 4 * D_CONV + 2 * D_SSM
DT_MIN = 1e-3
DT_MAX = 1e-1
EPS = 1e-6

kernel_name = "hybrid_shortconv_s5_block"


def rms_norm(x, g):
    xf = x.astype(jnp.float32)
    y = xf * lax.rsqrt(jnp.mean(xf * xf, axis=-1, keepdims=True) + EPS)
    return (y * g.astype(jnp.float32)).astype(x.dtype)


def causal_dwconv(h, w, b):
    L = h.shape[1]
    hp = jnp.pad(h, ((0, 0), (CONV_WIDTH - 1, 0), (0, 0)))
    out = b[None, None, :]
    for k in range(CONV_WIDTH):
        out = out + w[k][None, None, :] * hp[:, k:k + L, :]
    return out


def _scan_combine(e1, e2):
    a1r, a1i, b1r, b1i = e1
    a2r, a2i, b2r, b2i = e2
    ar = a2r * a1r - a2i * a1i
    ai = a2r * a1i + a2i * a1r
    br = a2r * b1r - a2i * b1i + b2r
    bi = a2r * b1i + a2i * b1r + b2i
    return (ar, ai, br, bi)


def s5_branch(u, a_re, a_im, log_dt, b_re, b_im, c_re, c_im, d_skip, w_glu, b_glu):
    f32 = jnp.float32
    uf = u.astype(f32)
    Bsz, L, _ = uf.shape
    ug = uf.reshape(Bsz, L, SSM_GROUPS, SSM_GROUP)
    lr, li = a_re.astype(f32), a_im.astype(f32)
    dt = jnp.exp(log_dt.astype(f32))[:, None]
    mag = jnp.exp(lr * dt)
    lbr, lbi = mag * jnp.cos(li * dt), mag * jnp.sin(li * dt)
    nr, ni = lbr - 1.0, lbi
    den = lr * lr + li * li
    qr = (nr * lr + ni * li) / den
    qi = (ni * lr - nr * li) / den
    br_, bi_ = b_re.astype(f32), b_im.astype(f32)
    bbr = qr[..., None] * br_ - qi[..., None] * bi_
    bbi = qr[..., None] * bi_ + qi[..., None] * br_
    bu_r = jnp.einsum('blgh,gph->blgp', ug, bbr)
    bu_i = jnp.einsum('blgh,gph->blgp', ug, bbi)
    ar = jnp.broadcast_to(lbr[None, None], bu_r.shape)
    ai = jnp.broadcast_to(lbi[None, None], bu_i.shape)
    _, _, s_r, s_i = lax.associative_scan(_scan_combine, (ar, ai, bu_r, bu_i), axis=1)
    y = (jnp.einsum('ghp,blgp->blgh', c_re.astype(f32), s_r)
         - jnp.einsum('ghp,blgp->blgh', c_im.astype(f32), s_i))
    y = y.reshape(Bsz, L, D_SSM) + d_skip.astype(f32)[None, None, :] * uf
    y = jax.nn.gelu(y)
    y = y * jax.nn.sigmoid(y @ w_glu.astype(f32) + b_glu.astype(f32))
    return y.astype(u.dtype)


def _fwd_setup_inputs(seed: int = 0) -> dict:
    key = jax.random.key(seed)
    ks = jax.random.split(key, 20)
    f32 = jnp.float32
    x = jax.random.normal(ks[0], (BATCH, SEQ, D_MODEL), f32)
    norm_pre_g = 1.0 + 0.05 * jax.random.normal(ks[1], (D_MODEL,), f32)
    w_in = jax.random.normal(ks[2], (D_MODEL, N_IN), f32) * D_MODEL ** -0.5
    conv_w = jax.random.normal(ks[3], (CONV_WIDTH, D_CONV), f32) * CONV_WIDTH ** -0.5
    conv_b = 0.02 * jax.random.normal(ks[4], (D_CONV,), f32)
    n = jnp.arange(SSM_STATE, dtype=f32)[None, :]
    ssm_a_re = -0.5 + 0.01 * jax.random.normal(ks[5], (SSM_GROUPS, SSM_STATE), f32)
    ssm_a_im = math.pi * n + 0.01 * jax.random.normal(ks[6], (SSM_GROUPS, SSM_STATE), f32)
    ssm_log_dt = jax.random.uniform(ks[7], (SSM_GROUPS,), f32, math.log(DT_MIN), math.log(DT_MAX))
    bs = (2.0 * SSM_GROUP) ** -0.5
    ssm_b_re = jax.random.normal(ks[8], (SSM_GROUPS, SSM_STATE, SSM_GROUP), f32) * bs
    ssm_b_im = jax.random.normal(ks[9], (SSM_GROUPS, SSM_STATE, SSM_GROUP), f32) * bs
    cs = (2.0 * SSM_STATE) ** -0.5
    ssm_c_re = jax.random.normal(ks[10], (SSM_GROUPS, SSM_GROUP, SSM_STATE), f32) * cs
    ssm_c_im = jax.random.normal(ks[11], (SSM_GROUPS, SSM_GROUP, SSM_STATE), f32) * cs
    ssm_d = jax.random.normal(ks[12], (D_SSM,), f32)
    w_glu = jax.random.normal(ks[13], (D_SSM, D_SSM), f32) * D_SSM ** -0.5
    b_glu = 0.02 * jax.random.normal(ks[14], (D_SSM,), f32)
    w_out = jax.random.normal(ks[15], (D_MIX, D_MODEL), f32) * D_MIX ** -0.5
    norm_post_g = 1.0 + 0.05 * jax.random.normal(ks[16], (D_MODEL,), f32)
    return {"x": x, "norm_pre_g": norm_pre_g, "w_in": w_in, "conv_w": conv_w, "conv_b": conv_b,
            "ssm_a_re": ssm_a_re, "ssm_a_im": ssm_a_im, "ssm_log_dt": ssm_log_dt,
            "ssm_b_re": ssm_b_re, "ssm_b_im": ssm_b_im, "ssm_c_re": ssm_c_re, "ssm_c_im": ssm_c_im,
            "ssm_d": ssm_d, "w_glu": w_glu, "b_glu": b_glu, "w_out": w_out, "norm_post_g": norm_post_g}


def _fwd_reference(x, norm_pre_g, w_in, conv_w, conv_b, ssm_a_re, ssm_a_im, ssm_log_dt,
              ssm_b_re, ssm_b_im, ssm_c_re, ssm_c_im, ssm_d, w_glu, b_glu, w_out, norm_post_g):
    for _ in range(DEPTH):
        h = rms_norm(x, norm_pre_g)
        proj = h @ w_in
        b_gate, c_gate, v, z_conv = jnp.split(proj[..., :4 * D_CONV], 4, axis=-1)
        u, z_ssm = jnp.split(proj[..., 4 * D_CONV:], 2, axis=-1)
        y_conv = b_gate * causal_dwconv(c_gate * v, conv_w, conv_b)
        y_conv = y_conv * jax.nn.silu(z_conv)
        y_ssm = s5_branch(u, ssm_a_re, ssm_a_im, ssm_log_dt, ssm_b_re, ssm_b_im,
                          ssm_c_re, ssm_c_im, ssm_d, w_glu, b_glu)
        y_ssm = y_ssm * jax.nn.silu(z_ssm)
        mix = jnp.concatenate([y_conv, y_ssm], axis=-1)
        x = x + rms_norm(mix @ w_out, norm_post_g)
    return x


import jax as _jax
import jax.numpy as _jnp

TWIN_FORMAT = 'train_step'
FWD_PARAMS = ['x', 'norm_pre_g', 'w_in', 'conv_w', 'conv_b', 'ssm_a_re', 'ssm_a_im', 'ssm_log_dt', 'ssm_b_re', 'ssm_b_im', 'ssm_c_re', 'ssm_c_im', 'ssm_d', 'w_glu', 'b_glu', 'w_out', 'norm_post_g']
TWIN_WEIGHTS = ['norm_pre_g', 'w_in', 'conv_w', 'conv_b', 'ssm_a_re', 'ssm_a_im', 'ssm_log_dt', 'ssm_b_re', 'ssm_b_im', 'ssm_c_re', 'ssm_c_im', 'ssm_d', 'w_glu', 'b_glu', 'w_out', 'norm_post_g']
TWIN_DIFF_INPUT = 'x'
TWIN_INPUTS = ['x', 'norm_pre_g', 'w_in', 'conv_w', 'conv_b', 'ssm_a_re', 'ssm_a_im', 'ssm_log_dt', 'ssm_b_re', 'ssm_b_im', 'ssm_c_re', 'ssm_c_im', 'ssm_d', 'w_glu', 'b_glu', 'w_out', 'norm_post_g', 'loss_target', 'm_norm_pre_g', 'm_w_in', 'm_conv_w', 'm_conv_b', 'm_ssm_a_re', 'm_ssm_a_im', 'm_ssm_log_dt', 'm_ssm_b_re', 'm_ssm_b_im', 'm_ssm_c_re', 'm_ssm_c_im', 'm_ssm_d', 'm_w_glu', 'm_b_glu', 'm_w_out', 'm_norm_post_g', 'v_norm_pre_g', 'v_w_in', 'v_conv_w', 'v_conv_b', 'v_ssm_a_re', 'v_ssm_a_im', 'v_ssm_log_dt', 'v_ssm_b_re', 'v_ssm_b_im', 'v_ssm_c_re', 'v_ssm_c_im', 'v_ssm_d', 'v_w_glu', 'v_b_glu', 'v_w_out', 'v_norm_post_g']
TWIN_OUTPUTS = ['loss', 'grad_x', 'grad_norm_pre_g', 'grad_w_in', 'grad_conv_w', 'grad_conv_b', 'grad_ssm_a_re', 'grad_ssm_a_im', 'grad_ssm_log_dt', 'grad_ssm_b_re', 'grad_ssm_b_im', 'grad_ssm_c_re', 'grad_ssm_c_im', 'grad_ssm_d', 'grad_w_glu', 'grad_b_glu', 'grad_w_out', 'grad_norm_post_g', 'delta_norm_pre_g', 'delta_w_in', 'delta_conv_w', 'delta_conv_b', 'delta_ssm_a_re', 'delta_ssm_a_im', 'delta_ssm_log_dt', 'delta_ssm_b_re', 'delta_ssm_b_im', 'delta_ssm_c_re', 'delta_ssm_c_im', 'delta_ssm_d', 'delta_w_glu', 'delta_b_glu', 'delta_w_out', 'delta_norm_post_g', 'new_m_norm_pre_g', 'new_m_w_in', 'new_m_conv_w', 'new_m_conv_b', 'new_m_ssm_a_re', 'new_m_ssm_a_im', 'new_m_ssm_log_dt', 'new_m_ssm_b_re', 'new_m_ssm_b_im', 'new_m_ssm_c_re', 'new_m_ssm_c_im', 'new_m_ssm_d', 'new_m_w_glu', 'new_m_b_glu', 'new_m_w_out', 'new_m_norm_post_g', 'new_v_norm_pre_g', 'new_v_w_in', 'new_v_conv_w', 'new_v_conv_b', 'new_v_ssm_a_re', 'new_v_ssm_a_im', 'new_v_ssm_log_dt', 'new_v_ssm_b_re', 'new_v_ssm_b_im', 'new_v_ssm_c_re', 'new_v_ssm_c_im', 'new_v_ssm_d', 'new_v_w_glu', 'new_v_b_glu', 'new_v_w_out', 'new_v_norm_post_g']
TWIN_LEAF_KINDS = {'loss': 'loss', 'grad_x': 'grad_x', 'grad_norm_pre_g': 'grad_w', 'grad_w_in': 'grad_w', 'grad_conv_w': 'grad_w', 'grad_conv_b': 'grad_w', 'grad_ssm_a_re': 'grad_w', 'grad_ssm_a_im': 'grad_w', 'grad_ssm_log_dt': 'grad_w', 'grad_ssm_b_re': 'grad_w', 'grad_ssm_b_im': 'grad_w', 'grad_ssm_c_re': 'grad_w', 'grad_ssm_c_im': 'grad_w', 'grad_ssm_d': 'grad_w', 'grad_w_glu': 'grad_w', 'grad_b_glu': 'grad_w', 'grad_w_out': 'grad_w', 'grad_norm_post_g': 'grad_w', 'delta_norm_pre_g': 'delta_w', 'delta_w_in': 'delta_w', 'delta_conv_w': 'delta_w', 'delta_conv_b': 'delta_w', 'delta_ssm_a_re': 'delta_w', 'delta_ssm_a_im': 'delta_w', 'delta_ssm_log_dt': 'delta_w', 'delta_ssm_b_re': 'delta_w', 'delta_ssm_b_im': 'delta_w', 'delta_ssm_c_re': 'delta_w', 'delta_ssm_c_im': 'delta_w', 'delta_ssm_d': 'delta_w', 'delta_w_glu': 'delta_w', 'delta_b_glu': 'delta_w', 'delta_w_out': 'delta_w', 'delta_norm_post_g': 'delta_w', 'new_m_norm_pre_g': 'new_m', 'new_m_w_in': 'new_m', 'new_m_conv_w': 'new_m', 'new_m_conv_b': 'new_m', 'new_m_ssm_a_re': 'new_m', 'new_m_ssm_a_im': 'new_m', 'new_m_ssm_log_dt': 'new_m', 'new_m_ssm_b_re': 'new_m', 'new_m_ssm_b_im': 'new_m', 'new_m_ssm_c_re': 'new_m', 'new_m_ssm_c_im': 'new_m', 'new_m_ssm_d': 'new_m', 'new_m_w_glu': 'new_m', 'new_m_b_glu': 'new_m', 'new_m_w_out': 'new_m', 'new_m_norm_post_g': 'new_m', 'new_v_norm_pre_g': 'new_v', 'new_v_w_in': 'new_v', 'new_v_conv_w': 'new_v', 'new_v_conv_b': 'new_v', 'new_v_ssm_a_re': 'new_v', 'new_v_ssm_a_im': 'new_v', 'new_v_ssm_log_dt': 'new_v', 'new_v_ssm_b_re': 'new_v', 'new_v_ssm_b_im': 'new_v', 'new_v_ssm_c_re': 'new_v', 'new_v_ssm_c_im': 'new_v', 'new_v_ssm_d': 'new_v', 'new_v_w_glu': 'new_v', 'new_v_b_glu': 'new_v', 'new_v_w_out': 'new_v', 'new_v_norm_post_g': 'new_v'}


def _forward(args):
    return _fwd_reference(*[args[k] for k in FWD_PARAMS])


def _output_shape():
    def fwd():
        inp = _fwd_setup_inputs(0)
        return _fwd_reference(*[inp[k] for k in FWD_PARAMS])
    out = _jax.eval_shape(fwd)
    return out.shape, out.dtype

N_MICROBATCH = 1
ADAM_LR = 0.001
ADAM_B1 = 0.9
ADAM_B2 = 0.999
ADAM_EPS = 1e-08
ADAM_WD = 0.01
ADAM_STEP = 10
PER_EXAMPLE_BATCH_AXIS = {'x': 0, 'loss_target': 0}
SHARED_INPUTS = []
_WEIGHT_DTYPES = {'norm_pre_g': _jnp.float32, 'w_in': _jnp.float32, 'conv_w': _jnp.float32, 'conv_b': _jnp.float32, 'ssm_a_re': _jnp.float32, 'ssm_a_im': _jnp.float32, 'ssm_log_dt': _jnp.float32, 'ssm_b_re': _jnp.float32, 'ssm_b_im': _jnp.float32, 'ssm_c_re': _jnp.float32, 'ssm_c_im': _jnp.float32, 'ssm_d': _jnp.float32, 'w_glu': _jnp.float32, 'b_glu': _jnp.float32, 'w_out': _jnp.float32, 'norm_post_g': _jnp.float32}
MOMENT_SCALE = {'norm_pre_g': 2.513296e-01, 'w_in': 1.455768e-01, 'conv_w': 1.739486e-01, 'conv_b': 1.890664e-01, 'ssm_a_re': 3.337045e-03, 'ssm_a_im': 3.109423e-03, 'ssm_log_dt': 2.639121e+00, 'ssm_b_re': 2.100433e-03, 'ssm_b_im': 2.070481e-03, 'ssm_c_re': 4.165049e-03, 'ssm_c_im': 4.233144e-03, 'ssm_d': 8.037199e-02, 'w_glu': 1.936344e-02, 'b_glu': 3.431312e-02, 'w_out': 1.361078e-01, 'norm_post_g': 1.604467e+01}


def _to_microbatches(a, axis):
    t = _jnp.moveaxis(a, axis, 0)
    t = t.reshape((N_MICROBATCH, t.shape[0] // N_MICROBATCH) + t.shape[1:])
    return _jnp.moveaxis(t, 1, axis + 1)


def setup_inputs(seed: int = 0) -> dict:
    inp = _fwd_setup_inputs(seed)
    key = _jax.random.fold_in(_jax.random.key(seed), 7919)
    shape, _ = _output_shape()
    out = dict(inp)
    out["loss_target"] = _jax.random.normal(_jax.random.fold_in(key, 0), shape, _jnp.float32)
    for i, name in enumerate(TWIN_WEIGHTS):
        w = inp[name].astype(_jnp.float32)
        if MOMENT_SCALE is None:
            s = _jnp.sqrt(_jnp.mean(_jnp.square(w)) + 1e-30)
        else:
            s = MOMENT_SCALE[name]
        km, kv = _jax.random.split(_jax.random.fold_in(key, i + 1))
        out[name] = w
        out["m_" + name] = s * _jax.random.normal(km, w.shape, _jnp.float32)
        out["v_" + name] = (s * s) * _jax.random.uniform(kv, w.shape, _jnp.float32, 0.5, 1.5)
    if N_MICROBATCH > 1:
        for name, axis in PER_EXAMPLE_BATCH_AXIS.items():
            out[name] = _to_microbatches(out[name], axis)
    return {'x': out['x'], 'norm_pre_g': out['norm_pre_g'], 'w_in': out['w_in'], 'conv_w': out['conv_w'], 'conv_b': out['conv_b'], 'ssm_a_re': out['ssm_a_re'], 'ssm_a_im': out['ssm_a_im'], 'ssm_log_dt': out['ssm_log_dt'], 'ssm_b_re': out['ssm_b_re'], 'ssm_b_im': out['ssm_b_im'], 'ssm_c_re': out['ssm_c_re'], 'ssm_c_im': out['ssm_c_im'], 'ssm_d': out['ssm_d'], 'w_glu': out['w_glu'], 'b_glu': out['b_glu'], 'w_out': out['w_out'], 'norm_post_g': out['norm_post_g'], 'loss_target': out['loss_target'], 'm_norm_pre_g': out['m_norm_pre_g'], 'm_w_in': out['m_w_in'], 'm_conv_w': out['m_conv_w'], 'm_conv_b': out['m_conv_b'], 'm_ssm_a_re': out['m_ssm_a_re'], 'm_ssm_a_im': out['m_ssm_a_im'], 'm_ssm_log_dt': out['m_ssm_log_dt'], 'm_ssm_b_re': out['m_ssm_b_re'], 'm_ssm_b_im': out['m_ssm_b_im'], 'm_ssm_c_re': out['m_ssm_c_re'], 'm_ssm_c_im': out['m_ssm_c_im'], 'm_ssm_d': out['m_ssm_d'], 'm_w_glu': out['m_w_glu'], 'm_b_glu': out['m_b_glu'], 'm_w_out': out['m_w_out'], 'm_norm_post_g': out['m_norm_post_g'], 'v_norm_pre_g': out['v_norm_pre_g'], 'v_w_in': out['v_w_in'], 'v_conv_w': out['v_conv_w'], 'v_conv_b': out['v_conv_b'], 'v_ssm_a_re': out['v_ssm_a_re'], 'v_ssm_a_im': out['v_ssm_a_im'], 'v_ssm_log_dt': out['v_ssm_log_dt'], 'v_ssm_b_re': out['v_ssm_b_re'], 'v_ssm_b_im': out['v_ssm_b_im'], 'v_ssm_c_re': out['v_ssm_c_re'], 'v_ssm_c_im': out['v_ssm_c_im'], 'v_ssm_d': out['v_ssm_d'], 'v_w_glu': out['v_w_glu'], 'v_b_glu': out['v_b_glu'], 'v_w_out': out['v_w_out'], 'v_norm_post_g': out['v_norm_post_g']}


def _loss(weights, diff, rest, loss_target):
    with _jax.named_scope("forward"):
        args = {**rest, TWIN_DIFF_INPUT: diff, **{k: w.astype(_WEIGHT_DTYPES[k]) for k, w in weights.items()}}
        y = _forward(args)
    with _jax.named_scope("loss_head"):
        err = _jnp.square(y.astype(_jnp.float32) - loss_target)
        return 0.5 * _jnp.sum(_jnp.mean(err, axis=-1)) if err.ndim else 0.5 * err


def _adamw(w, g, m, v):
    m = ADAM_B1 * m + (1.0 - ADAM_B1) * g
    v = ADAM_B2 * v + (1.0 - ADAM_B2) * _jnp.square(g)
    m_hat = m / (1.0 - ADAM_B1 ** ADAM_STEP)
    v_hat = v / (1.0 - ADAM_B2 ** ADAM_STEP)
    delta = -ADAM_LR * (m_hat / (_jnp.sqrt(v_hat) + ADAM_EPS) + ADAM_WD * w)
    return delta, m, v


def reference(x, norm_pre_g, w_in, conv_w, conv_b, ssm_a_re, ssm_a_im, ssm_log_dt, ssm_b_re, ssm_b_im, ssm_c_re, ssm_c_im, ssm_d, w_glu, b_glu, w_out, norm_post_g, loss_target, m_norm_pre_g, m_w_in, m_conv_w, m_conv_b, m_ssm_a_re, m_ssm_a_im, m_ssm_log_dt, m_ssm_b_re, m_ssm_b_im, m_ssm_c_re, m_ssm_c_im, m_ssm_d, m_w_glu, m_b_glu, m_w_out, m_norm_post_g, v_norm_pre_g, v_w_in, v_conv_w, v_conv_b, v_ssm_a_re, v_ssm_a_im, v_ssm_log_dt, v_ssm_b_re, v_ssm_b_im, v_ssm_c_re, v_ssm_c_im, v_ssm_d, v_w_glu, v_b_glu, v_w_out, v_norm_post_g):
    given = dict(x=x, norm_pre_g=norm_pre_g, w_in=w_in, conv_w=conv_w, conv_b=conv_b, ssm_a_re=ssm_a_re, ssm_a_im=ssm_a_im, ssm_log_dt=ssm_log_dt, ssm_b_re=ssm_b_re, ssm_b_im=ssm_b_im, ssm_c_re=ssm_c_re, ssm_c_im=ssm_c_im, ssm_d=ssm_d, w_glu=w_glu, b_glu=b_glu, w_out=w_out, norm_post_g=norm_post_g, loss_target=loss_target, m_norm_pre_g=m_norm_pre_g, m_w_in=m_w_in, m_conv_w=m_conv_w, m_conv_b=m_conv_b, m_ssm_a_re=m_ssm_a_re, m_ssm_a_im=m_ssm_a_im, m_ssm_log_dt=m_ssm_log_dt, m_ssm_b_re=m_ssm_b_re, m_ssm_b_im=m_ssm_b_im, m_ssm_c_re=m_ssm_c_re, m_ssm_c_im=m_ssm_c_im, m_ssm_d=m_ssm_d, m_w_glu=m_w_glu, m_b_glu=m_b_glu, m_w_out=m_w_out, m_norm_post_g=m_norm_post_g, v_norm_pre_g=v_norm_pre_g, v_w_in=v_w_in, v_conv_w=v_conv_w, v_conv_b=v_conv_b, v_ssm_a_re=v_ssm_a_re, v_ssm_a_im=v_ssm_a_im, v_ssm_log_dt=v_ssm_log_dt, v_ssm_b_re=v_ssm_b_re, v_ssm_b_im=v_ssm_b_im, v_ssm_c_re=v_ssm_c_re, v_ssm_c_im=v_ssm_c_im, v_ssm_d=v_ssm_d, v_w_glu=v_w_glu, v_b_glu=v_b_glu, v_w_out=v_w_out, v_norm_post_g=v_norm_post_g)
    weights = {n: given[n] for n in TWIN_WEIGHTS}
    shared = {n: given[n] for n in SHARED_INPUTS}
    per_example = {n: given[n] for n in ['x']}
    grad_fn = _jax.value_and_grad(_loss, argnums=(0, 1))

    def one_microbatch(ex, loss_target):
        ex = dict(ex)
        diff = ex.pop(TWIN_DIFF_INPUT)
        return grad_fn(weights, diff, {**shared, **ex}, loss_target)

    if N_MICROBATCH == 1:
        loss, (grad_w, grad_x) = one_microbatch(per_example, given["loss_target"])
    else:
        def body(carry, xs):
            loss_sum, grad_sum = carry
            l_k, (gw_k, gx_k) = one_microbatch(xs[0], xs[1])
            with _jax.named_scope("update"):
                return (loss_sum + l_k, _jax.tree.map(_jnp.add, grad_sum, gw_k)), gx_k

        init = (_jnp.zeros((), _jnp.float32), _jax.tree.map(_jnp.zeros_like, weights))
        (loss, grad_w), grad_x = _jax.lax.scan(body, init, (per_example, given["loss_target"]))
    with _jax.named_scope("update"):
        delta_w, new_m, new_v = {}, {}, {}
        for n in TWIN_WEIGHTS:
            delta_w[n], new_m[n], new_v[n] = _adamw(weights[n], grad_w[n], given["m_" + n], given["v_" + n])
    return (loss, grad_x, *[grad_w[n] for n in TWIN_WEIGHTS], *[delta_w[n] for n in TWIN_WEIGHTS],
            *[new_m[n] for n in TWIN_WEIGHTS], *[new_v[n] for n in TWIN_WEIGHTS])
```

```python
import functools
import math

import jax
import jax.numpy as jnp
from jax import lax
from jax.experimental import pallas as pl
from jax.experimental.pallas import tpu as pltpu

F32 = jnp.float32
MXU_DTYPE = jnp.bfloat16
AXES = ("x", "y", "c")
N_DEV = 8
N_CHUNK = 8
LANES = 128
SSM_GROUP = 16
SSM_STATE = 64
HALF_CH = 64
HALF_G = HALF_CH // SSM_GROUP
HALF_W = HALF_G * SSM_STATE
EPS = 1e-6
ADAM_LR, ADAM_B1, ADAM_B2, ADAM_EPS, ADAM_WD, ADAM_STEP = 0.001, 0.9, 0.999, 1e-08, 0.01, 10
GELU_C = math.sqrt(2.0 / math.pi)
GELU_K = 0.044715
VMEM_LIMIT = 56 * 1024 * 1024


def _params(sem=None):
    return pltpu.CompilerParams(dimension_semantics=sem, vmem_limit_bytes=VMEM_LIMIT)


def _dot(a, b):
    return jnp.dot(a, b, preferred_element_type=F32)


def _dot_nt(a, b):
    return lax.dot_general(a, b, (((1,), (1,)), ((), ())), preferred_element_type=F32)


def _dot_tn(a, b):
    return lax.dot_general(a, b, (((0,), (0,)), ((), ())), preferred_element_type=F32)


def _sigmoid(z):
    return 1.0 / (1.0 + jnp.exp(-z))


def _exchange(name, gathers, scatters):
    n_g = len(gathers)
    ins = list(gathers) + list(scatters)
    n = len(ins)

    def body(*refs):
        in_refs, out_refs = refs[:n], refs[n:2 * n]
        send_sems, recv_sems, local_sems = refs[2 * n:]
        x, y, c = (lax.axis_index(a) for a in AXES)
        me = 4 * x + 2 * y + c

        def src(t, dev):
            return in_refs[t] if t < n_g else in_refs[t].at[dev]

        local, sends = [], []
        for t in range(n):
            cp = pltpu.make_async_copy(src(t, me), out_refs[t].at[me], local_sems.at[t])
            cp.start()
            local.append(cp)
        peers = []
        for m in range(1, N_DEV):
            px = 1 - x if (m >> 2) & 1 else x
            py = 1 - y if (m >> 1) & 1 else y
            pc = 1 - c if m & 1 else c
            peers.append((px, py, pc, 4 * px + 2 * py + pc))
        for t in range(n):
            for m, (px, py, pc, peer) in enumerate(peers):
                cp = pltpu.make_async_remote_copy(
                    src_ref=src(t, peer), dst_ref=out_refs[t].at[me],
                    send_sem=send_sems.at[t, m], recv_sem=recv_sems.at[t, m],
                    device_id=(px, py, pc), device_id_type=pl.DeviceIdType.MESH)
                cp.start()
                sends.append(cp)
        for t in range(n):
            for m, (px, py, pc, peer) in enumerate(peers):
                pltpu.make_async_remote_copy(
                    src_ref=src(t, peer), dst_ref=out_refs[t].at[peer],
                    send_sem=send_sems.at[t, m], recv_sem=recv_sems.at[t, m],
                    device_id=(px, py, pc), device_id_type=pl.DeviceIdType.MESH).wait_recv()
        for cp in sends:
            cp.wait_send()
        for cp in local:
            cp.wait()

    out_shape = [jax.ShapeDtypeStruct((N_DEV,) + a.shape, a.dtype) for a in gathers]
    out_shape += [jax.ShapeDtypeStruct(p.shape, p.dtype) for p in scatters]
    return pl.pallas_call(
        body, name=name, out_shape=out_shape,
        in_specs=[pl.BlockSpec(memory_space=pl.ANY)] * n,
        out_specs=[pl.BlockSpec(memory_space=pl.ANY)] * n,
        scratch_shapes=[pltpu.SemaphoreType.DMA((n, N_DEV - 1)), pltpu.SemaphoreType.DMA((n, N_DEV - 1)),
                        pltpu.SemaphoreType.DMA((n,))],
    )(*ins)


def _ssm_prep(a_re, a_im, log_dt, bt_re, bt_im):
    def body(lr_ref, li_ref, ldt_ref, br_ref, bi_ref, lbr_ref, lbi_ref, qr_ref, qi_ref, bbr_ref, bbi_ref):
        lr, li = lr_ref[...], li_ref[...]
        dt = jnp.exp(ldt_ref[...])
        mag = jnp.exp(lr * dt)
        lbr, lbi = mag * jnp.cos(li * dt), mag * jnp.sin(li * dt)
        nr, ni = lbr - 1.0, lbi
        den = lr * lr + li * li
        qr = (nr * lr + ni * li) / den
        qi = (ni * lr - nr * li) / den
        br, bi = br_ref[...], bi_ref[...]
        lbr_ref[...], lbi_ref[...], qr_ref[...], qi_ref[...] = lbr, lbi, qr, qi
        bbr_ref[...] = qr * br - qi * bi
        bbi_ref[...] = qr * bi + qi * br

    s2 = jax.ShapeDtypeStruct(a_re.shape, F32)
    s3 = jax.ShapeDtypeStruct(bt_re.shape, F32)
    return pl.pallas_call(body, name="ssm_prep", out_shape=[s2, s2, s2, s2, s3, s3],
                          compiler_params=_params())(a_re, a_im, log_dt, bt_re, bt_im)


def _adam(w, g, m, v):
    m2 = ADAM_B1 * m + (1.0 - ADAM_B1) * g
    v2 = ADAM_B2 * v + (1.0 - ADAM_B2) * (g * g)
    m_hat = m2 / (1.0 - ADAM_B1 ** ADAM_STEP)
    v_hat = v2 / (1.0 - ADAM_B2 ** ADAM_STEP)
    delta = -ADAM_LR * (m_hat / (jnp.sqrt(v_hat) + ADAM_EPS) + ADAM_WD * w)
    return delta, m2, v2


def _small_update(direct, ssm):
    n_direct = len(direct)
    flat = [a for quad in direct for a in quad]
    names = ["da_r", "da_i", "dbb_r", "dbb_i", "lr", "li", "ldt", "bt_r", "bt_i", "lbr", "lbi", "qr", "qi"]
    flat += [ssm[k] for k in names]
    chain = ["a_re", "a_im", "log_dt", "bt_re", "bt_im"]
    for k in chain:
        flat += [ssm["w_" + k], ssm["m_" + k], ssm["v_" + k]]
    n_in = len(flat)

    def body(*refs):
        ins, outs = refs[:n_in], refs[n_in:]
        for p in range(n_direct):
            g, w, m, v = (r[...] for r in ins[4 * p:4 * p + 4])
            d, m2, v2 = _adam(w, g, m, v)
            outs[4 * p][...], outs[4 * p + 1][...], outs[4 * p + 2][...], outs[4 * p + 3][...] = g, d, m2, v2
        o = 4 * n_direct
        da_r, da_i, dbb_r, dbb_i, lr, li, ldt, bt_r, bt_i, lbr, lbi, qr, qi = (r[...] for r in ins[o:o + 13])
        dt = jnp.exp(ldt)
        g_br = qr * dbb_r + qi * dbb_i
        g_bi = qr * dbb_i - qi * dbb_r
        dq_r = jnp.sum(bt_r * dbb_r + bt_i * dbb_i, axis=1, keepdims=True)
        dq_i = jnp.sum(bt_r * dbb_i - bt_i * dbb_r, axis=1, keepdims=True)
        den = lr * lr + li * li
        cr, ci = lr / den, li / den
        gl_r = da_r + (cr * dq_r - ci * dq_i)
        gl_i = da_i + (cr * dq_i + ci * dq_r)
        w_r = qr * cr + qi * ci
        w_i = qi * cr - qr * ci
        g_lr = dt * (lbr * gl_r + lbi * gl_i) + (-w_r * dq_r - w_i * dq_i)
        g_li = dt * (lbr * gl_i - lbi * gl_r) + (-w_r * dq_i + w_i * dq_r)
        m_r = lr * lbr - li * lbi
        m_i = lr * lbi + li * lbr
        g_ldt = jnp.sum(m_r * gl_r + m_i * gl_i, axis=2, keepdims=True) * dt
        grads = [g_lr, g_li, g_ldt, g_br, g_bi]
        base_in, base_out = o + 13, 4 * n_direct
        for p, g in enumerate(grads):
            w, m, v = (r[...] for r in ins[base_in + 3 * p:base_in + 3 * p + 3])
            d, m2, v2 = _adam(w, g, m, v)
            q = base_out + 4 * p
            outs[q][...], outs[q + 1][...], outs[q + 2][...], outs[q + 3][...] = g, d, m2, v2

    out_shape = []
    for quad in direct:
        out_shape += [jax.ShapeDtypeStruct(quad[1].shape, F32)] * 4
    for k in chain:
        out_shape += [jax.ShapeDtypeStruct(ssm["w_" + k].shape, F32)] * 4
    res = pl.pallas_call(body, name="small_update", out_shape=out_shape, compiler_params=_params())(*flat)
    return [tuple(res[4 * p:4 * p + 4]) for p in range(n_direct + len(chain))]


def _sum_slots(pack):
    def body(p_ref, o_ref):
        acc = p_ref[0]
        for k in range(1, N_DEV):
            acc = acc + p_ref[k]
        o_ref[...] = acc

    return pl.pallas_call(body, name="sum_slots", out_shape=jax.ShapeDtypeStruct(pack.shape[1:], F32),
                          compiler_params=_params())(pack)


def _adam_big(name, recv, w, m, v, tr):
    _, rows, cols = recv.shape

    def body(r_ref, w_ref, m_ref, v_ref, g_ref, d_ref, m2_ref, v2_ref):
        g = r_ref[0]
        for k in range(1, N_DEV):
            g = g + r_ref[k]
        d, m2, v2 = _adam(w_ref[...], g, m_ref[...], v_ref[...])
        g_ref[...], d_ref[...], m2_ref[...], v2_ref[...] = g, d, m2, v2

    blk = pl.BlockSpec((tr, cols), lambda i: (i, 0))
    shp = jax.ShapeDtypeStruct((rows, cols), F32)
    return pl.pallas_call(
        body, name=name, grid=(rows // tr,),
        in_specs=[pl.BlockSpec((N_DEV, tr, cols), lambda i: (0, i, 0)), blk, blk, blk],
        out_specs=[blk] * 4, out_shape=[shp] * 4, compiler_params=_params(("parallel",)),
    )(recv, w, m, v)


def _fwd_in(xp, g_pre, win_g, tm):
    seq, d_model = xp.shape
    nb, _, nc = win_g.shape

    def body(x_ref, g_ref, w_ref, proj_ref, h_ref):
        @pl.when(pl.program_id(1) == 0)
        def _():
            x = x_ref[...]
            r = lax.rsqrt(jnp.mean(x * x, axis=-1, keepdims=True) + EPS)
            h_ref[...] = (x * r * g_ref[...]).astype(h_ref.dtype)

        proj_ref[...] = _dot(h_ref[...], w_ref[...])

    return pl.pallas_call(
        body, name="fwd_in", grid=(seq // tm, nb),
        in_specs=[pl.BlockSpec((tm, d_model), lambda i, k: (i, 0)), pl.BlockSpec((1, d_model), lambda i, k: (0, 0)),
                  pl.BlockSpec((None, d_model, nc), lambda i, k: (k, 0, 0))],
        out_specs=[pl.BlockSpec((tm, nc), lambda i, k: (i, k)), pl.BlockSpec((tm, d_model), lambda i, k: (i, 0))],
        out_shape=[jax.ShapeDtypeStruct((seq, nb * nc), F32), jax.ShapeDtypeStruct((seq, d_model), MXU_DTYPE)],
        compiler_params=_params(("parallel", "arbitrary")),
    )(xp, g_pre, win_g)


def _shift_prev(a):
    n = a.shape[0]
    last = a[n - N_CHUNK:, :]
    row = lax.broadcasted_iota(jnp.int32, last.shape, 0)
    wrap = jnp.where(row == 0, 0.0, pltpu.roll(last, 1, axis=0))
    return jnp.concatenate([wrap, a[:n - N_CHUNK, :]], axis=0)


def _shift_next(a):
    first = a[:N_CHUNK, :]
    row = lax.broadcasted_iota(jnp.int32, first.shape, 0)
    wrap = jnp.where(row == N_CHUNK - 1, 0.0, pltpu.roll(first, N_CHUNK - 1, axis=0))
    return jnp.concatenate([a[N_CHUNK:, :], wrap], axis=0)


def _conv_specs(seq, d_conv):
    nblk = d_conv // LANES
    return [pl.BlockSpec((seq, LANES), functools.partial(lambda i, o: (0, o + i), o=q * nblk)) for q in range(4)]


def _conv_fwd(proj, conv_w8, conv_b, d_conv):
    seq = proj.shape[0]

    def body(bg_ref, cg_ref, v_ref, zc_ref, w_ref, b_ref, y_ref):
        cv = cg_ref[...] * v_ref[...]
        s1 = _shift_prev(cv)
        s2 = _shift_prev(s1)
        conv = b_ref[...] + w_ref[0:1, :] * s2 + w_ref[1:2, :] * s1 + w_ref[2:3, :] * cv
        z = zc_ref[...]
        y_ref[...] = bg_ref[...] * conv * (z * _sigmoid(z))

    col = pl.BlockSpec((seq, LANES), lambda i: (0, i))
    return pl.pallas_call(
        body, name="conv_fwd", grid=(d_conv // LANES,),
        in_specs=_conv_specs(seq, d_conv) + [pl.BlockSpec((8, LANES), lambda i: (0, i)), pl.BlockSpec((1, LANES), lambda i: (0, i))],
        out_specs=col, out_shape=jax.ShapeDtypeStruct((seq, d_conv), F32),
        compiler_params=_params(("parallel",)),
    )(proj, proj, proj, proj, conv_w8, conv_b)


def _conv_bwd(proj, dyc, conv_w8, conv_b, d_conv):
    seq = proj.shape[0]

    def body(bg_ref, cg_ref, v_ref, zc_ref, dy_ref, w_ref, b_ref, dbg_ref, dcg_ref, dv_ref, dzc_ref, dcb_ref, dcw_ref):
        bg, cg, v, z = bg_ref[...], cg_ref[...], v_ref[...], zc_ref[...]
        w0, w1, w2 = w_ref[0:1, :], w_ref[1:2, :], w_ref[2:3, :]
        cv = cg * v
        s1 = _shift_prev(cv)
        s2 = _shift_prev(s1)
        conv = b_ref[...] + w0 * s2 + w1 * s1 + w2 * cv
        sig = _sigmoid(z)
        dy = dy_ref[...]
        g1 = dy * (z * sig)
        d_conv_ = g1 * bg
        dbg_ref[...] = (g1 * conv).astype(dbg_ref.dtype)
        dzc_ref[...] = (dy * bg * conv * (sig * (1.0 + z * (1.0 - sig)))).astype(dzc_ref.dtype)
        n1 = _shift_next(d_conv_)
        n2 = _shift_next(n1)
        d_cv = w2 * d_conv_ + w1 * n1 + w0 * n2
        dcg_ref[...] = (d_cv * v).astype(dcg_ref.dtype)
        dv_ref[...] = (d_cv * cg).astype(dv_ref.dtype)
        dcb_ref[...] = jnp.sum(d_conv_, axis=0, keepdims=True)
        rows = [jnp.sum(d_conv_ * s, axis=0, keepdims=True) for s in (s2, s1, cv)]
        dcw_ref[...] = jnp.concatenate(rows + [jnp.zeros((5, LANES), F32)], axis=0)

    col = pl.BlockSpec((seq, LANES), lambda i: (0, i))
    big = jax.ShapeDtypeStruct((seq, d_conv), MXU_DTYPE)
    return pl.pallas_call(
        body, name="conv_bwd", grid=(d_conv // LANES,),
        in_specs=_conv_specs(seq, d_conv) + [col, pl.BlockSpec((8, LANES), lambda i: (0, i)), pl.BlockSpec((1, LANES), lambda i: (0, i))],
        out_specs=[col] * 4 + [pl.BlockSpec((1, LANES), lambda i: (0, i)), pl.BlockSpec((8, LANES), lambda i: (0, i))],
        out_shape=[big] * 4 + [jax.ShapeDtypeStruct((1, d_conv), F32), jax.ShapeDtypeStruct((8, d_conv), F32)],
        compiler_params=_params(("parallel",)),
    )(proj, proj, proj, proj, dyc, conv_w8, conv_b)


def _cmul(ar, ai, br, bi):
    return ar * br - ai * bi, ar * bi + ai * br


def _cpow(ar, ai, n):
    rr, ri = jnp.ones_like(ar), jnp.zeros_like(ai)
    while n:
        if n & 1:
            rr, ri = _cmul(rr, ri, ar, ai)
        n >>= 1
        if n:
            ar, ai = _cmul(ar, ai, ar, ai)
    return rr, ri


def _down(v, k):
    row = lax.broadcasted_iota(jnp.int32, v.shape, 0)
    return jnp.where(row >= k, pltpu.roll(v, k, axis=0), 0.0)


def _up(v, k):
    row = lax.broadcasted_iota(jnp.int32, v.shape, 0)
    return jnp.where(row < N_CHUNK - k, pltpu.roll(v, N_CHUNK - k, axis=0), 0.0)


def _chunk_carry(fr, fi, mr, mi, shift):
    vr, vi = shift(fr, 1), shift(fi, 1)
    for k in (1, 2, 4):
        pr, pi = _cmul(mr, mi, shift(vr, k), shift(vi, k))
        vr, vi = vr + pr, vi + pi
        mr, mi = _cmul(mr, mi, mr, mi)
    return vr, vi


def _scan(s_ref, ar, ai, steps, width, reverse):
    def step(q, carry):
        sr, si = carry
        j = steps - 1 - q if reverse else q
        r = pl.multiple_of(j * N_CHUNK, N_CHUNK)
        nr = ar * sr - ai * si + s_ref[pl.ds(r, N_CHUNK), 0:width]
        ni = ar * si + ai * sr + s_ref[pl.ds(r, N_CHUNK), width:2 * width]
        s_ref[pl.ds(r, N_CHUNK), 0:width] = nr
        s_ref[pl.ds(r, N_CHUNK), width:2 * width] = ni
        return nr, ni

    z = jnp.zeros((N_CHUNK, width), F32)
    return lax.fori_loop(0, steps, step, (z, z))


def _patch(s_ref, ar, ai, cr, ci, steps, width, reverse):
    def step(q, pw):
        pr, pi = pw
        j = steps - 1 - q if reverse else q
        r = pl.multiple_of(j * N_CHUNK, N_CHUNK)
        fr, fi = _cmul(pr, pi, cr, ci)
        s_ref[pl.ds(r, N_CHUNK), 0:width] = s_ref[pl.ds(r, N_CHUNK), 0:width] + fr
        s_ref[pl.ds(r, N_CHUNK), width:2 * width] = s_ref[pl.ds(r, N_CHUNK), width:2 * width] + fi
        return _cmul(pr, pi, ar, ai)

    lax.fori_loop(0, steps, step, (ar, ai))


def _local_states(s_ref, u_half, bb, lam_ref, hh, steps, width):
    s_ref[...] = _dot(u_half.astype(MXU_DTYPE), bb)
    ar = jnp.broadcast_to(lam_ref[hh, 0:1, :], (N_CHUNK, width))
    ai = jnp.broadcast_to(lam_ref[hh, 1:2, :], (N_CHUNK, width))
    fr, fi = _scan(s_ref, ar, ai, steps, width, reverse=False)
    pr, pi = _cpow(ar, ai, steps)
    cr, ci = _chunk_carry(fr, fi, pr, pi, _down)
    return ar, ai, cr, ci


def _ssm_specs(seq, col0):
    return dict(
        col=pl.BlockSpec((seq, LANES), lambda i: (0, col0 + i)),
        lam=pl.BlockSpec((2, 2, HALF_W), lambda i: (i, 0, 0)),
        bb=pl.BlockSpec((2, HALF_CH, 2 * HALF_W), lambda i: (i, 0, 0)),
        cc=pl.BlockSpec((2, 2 * HALF_W, HALF_CH), lambda i: (i, 0, 0)),
        vec=pl.BlockSpec((1, LANES), lambda i: (0, i)),
        out=pl.BlockSpec((seq, LANES), lambda i: (0, i)),
    )


def _ssm_fwd(proj, lam, bbcat, cccat, d_skip, d_ssm, u_col0):
    seq = proj.shape[0]
    steps = seq // N_CHUNK

    def body(u_ref, lam_ref, bb_ref, cc_ref, d_ref, yp_ref, s_ref):
        for hh in range(2):
            lanes = slice(HALF_CH * hh, HALF_CH * (hh + 1))
            u_half = u_ref[:, lanes]
            ar, ai, cr, ci = _local_states(s_ref, u_half, bb_ref[hh], lam_ref, hh, steps, HALF_W)
            _patch(s_ref, ar, ai, cr, ci, steps, HALF_W, reverse=False)
            y = _dot(s_ref[...].astype(MXU_DTYPE), cc_ref[hh])
            yp_ref[:, lanes] = y + d_ref[:, lanes] * u_half

    sp = _ssm_specs(seq, u_col0 // LANES)
    return pl.pallas_call(
        body, name="ssm_fwd", grid=(d_ssm // LANES,),
        in_specs=[sp["col"], sp["lam"], sp["bb"], sp["cc"], sp["vec"]], out_specs=sp["out"],
        out_shape=jax.ShapeDtypeStruct((seq, d_ssm), F32),
        scratch_shapes=[pltpu.VMEM((seq, 2 * HALF_W), F32)],
        compiler_params=_params(("parallel",)),
    )(proj, lam, bbcat, cccat, d_skip)


def _ssm_bwd(proj, dyp, lam, bbcat, cccat, d_skip, d_ssm, u_col0):
    seq = proj.shape[0]
    steps = seq // N_CHUNK
    n_half = 2 * d_ssm // LANES
    width = HALF_W

    def body(u_ref, dyp_ref, lam_ref, bb_ref, cc_ref, d_ref, du_ref, dbb_ref, dcc_ref, da_ref, dd_ref, s_ref, g_ref):
        for hh in range(2):
            lanes = slice(HALF_CH * hh, HALF_CH * (hh + 1))
            u_half, dy_half = u_ref[:, lanes], dyp_ref[:, lanes]
            dy_mx = dy_half.astype(MXU_DTYPE)
            ar, ai, cr, ci = _local_states(s_ref, u_half, bb_ref[hh], lam_ref, hh, steps, width)
            g_ref[...] = _dot_nt(dy_mx, cc_ref[hh])
            nai = -ai
            lr_, li_ = _scan(g_ref, ar, nai, steps, width, reverse=True)
            pr, pi = _cpow(ar, nai, steps)
            gr, gi = _chunk_carry(lr_, li_, pr, pi, _up)
            _patch(g_ref, ar, nai, gr, gi, steps, width, reverse=True)

            def step(j, carry):
                pwr, pwi, sr, si, accr, acci = carry
                r = pl.multiple_of(j * N_CHUNK, N_CHUNK)
                fr, fi = _cmul(pwr, pwi, cr, ci)
                nr = s_ref[pl.ds(r, N_CHUNK), 0:width] + fr
                ni = s_ref[pl.ds(r, N_CHUNK), width:2 * width] + fi
                s_ref[pl.ds(r, N_CHUNK), 0:width] = nr
                s_ref[pl.ds(r, N_CHUNK), width:2 * width] = ni
                qr = g_ref[pl.ds(r, N_CHUNK), 0:width]
                qi = g_ref[pl.ds(r, N_CHUNK), width:2 * width]
                accr = accr + (sr * qr + si * qi)
                acci = acci + (sr * qi - si * qr)
                pwr, pwi = _cmul(pwr, pwi, ar, ai)
                return pwr, pwi, nr, ni, accr, acci

            z = jnp.zeros((N_CHUNK, width), F32)
            _, _, _, _, accr, acci = lax.fori_loop(0, steps, step, (ar, ai, cr, ci, z, z))
            da_ref[hh, :, 0:width] = jnp.sum(accr, axis=0, keepdims=True)
            da_ref[hh, :, width:2 * width] = jnp.sum(acci, axis=0, keepdims=True)

            g_mx = g_ref[...].astype(MXU_DTYPE)
            dcc_ref[hh] = _dot_tn(s_ref[...].astype(MXU_DTYPE), dy_mx)
            dbb_ref[hh] = _dot_tn(u_half.astype(MXU_DTYPE), g_mx)
            du = _dot_nt(g_mx, bb_ref[hh]) + d_ref[:, lanes] * dy_half
            du_ref[:, lanes] = du.astype(du_ref.dtype)
            dd_ref[:, lanes] = jnp.sum(dy_half * u_half, axis=0, keepdims=True)

    sp = _ssm_specs(seq, u_col0 // LANES)
    return pl.pallas_call(
        body, name="ssm_bwd", grid=(d_ssm // LANES,),
        in_specs=[sp["col"], sp["out"], sp["lam"], sp["bb"], sp["cc"], sp["vec"]],
        out_specs=[sp["out"], sp["bb"], sp["cc"], pl.BlockSpec((2, 1, 2 * width), lambda i: (i, 0, 0)), sp["vec"]],
        out_shape=[jax.ShapeDtypeStruct((seq, d_ssm), MXU_DTYPE),
                   jax.ShapeDtypeStruct((n_half, HALF_CH, 2 * width), F32),
                   jax.ShapeDtypeStruct((n_half, 2 * width, HALF_CH), F32),
                   jax.ShapeDtypeStruct((n_half, 1, 2 * width), F32),
                   jax.ShapeDtypeStruct((1, d_ssm), F32)],
        scratch_shapes=[pltpu.VMEM((seq, 2 * width), F32), pltpu.VMEM((seq, 2 * width), F32)],
        compiler_params=_params(("parallel",)),
    )(proj, dyp, lam, bbcat, cccat, d_skip)


def _tail(xp, tp, proj, yconv, yp, w_glu, b_glu, w_out, g_post, zs_col0, tm):
    seq, d_model = xp.shape
    d_conv, d_ssm = yconv.shape[1], yp.shape[1]
    d_mix = d_conv + d_ssm
    assert zs_col0 % d_ssm == 0

    def body(x_ref, t_ref, zs_ref, yc_ref, yp_ref, wglu_hbm, bglu_ref, wout_hbm, gpost_ref,
             dy_ref, do_ref, mix_ref, dyc_ref, dyp_ref, dzs_ref, yg_ref, dq_ref, loss_ref, dgpost_ref, dbglu_ref,
             wglu, wout):
        @pl.when(pl.program_id(0) == 0)
        def _():
            pltpu.sync_copy(wglu_hbm, wglu)
            pltpu.sync_copy(wout_hbm, wout)
            loss_ref[...] = jnp.zeros_like(loss_ref)
            dgpost_ref[...] = jnp.zeros_like(dgpost_ref)
            dbglu_ref[...] = jnp.zeros_like(dbglu_ref)

        a = yp_ref[...]
        th = jnp.tanh(GELU_C * (a + GELU_K * (a * a * a)))
        yg = a * (0.5 * (1.0 + th))
        dgelu = 0.5 * (1.0 + th) + 0.5 * a * (1.0 - th * th) * (GELU_C * (1.0 + 3.0 * GELU_K * a * a))
        yg_mx = yg.astype(MXU_DTYPE)
        sq = _sigmoid(_dot(yg_mx, wglu[...]) + bglu_ref[...])
        y2 = yg * sq
        zs = zs_ref[...]
        sz = _sigmoid(zs)
        silz = zs * sz
        mix = jnp.concatenate([yc_ref[...], y2 * silz], axis=1).astype(MXU_DTYPE)
        mix_ref[...] = mix
        o = _dot(mix, wout[...])
        r2 = lax.rsqrt(jnp.mean(o * o, axis=-1, keepdims=True) + EPS)
        on = o * r2
        gpost = gpost_ref[...]
        err = (x_ref[...] + on * gpost) - t_ref[...]
        loss_ref[...] += 0.5 * jnp.sum(jnp.mean(err * err, axis=-1, keepdims=True), axis=0, keepdims=True)
        dy = err * (1.0 / d_model)
        dy_ref[...] = dy
        dgpost_ref[...] += jnp.sum(dy * on, axis=0, keepdims=True)
        d_on = dy * gpost
        d_o = r2 * (d_on - on * jnp.mean(d_on * on, axis=-1, keepdims=True))
        do_mx = d_o.astype(MXU_DTYPE)
        do_ref[...] = do_mx
        d_mix_ = _dot_nt(do_mx, wout[...])
        dyc_ref[...] = d_mix_[:, :d_conv]
        d_yssm = d_mix_[:, d_conv:]
        d_y2 = d_yssm * silz
        dzs_ref[...] = (d_yssm * y2 * (sz * (1.0 + zs * (1.0 - sz)))).astype(dzs_ref.dtype)
        d_q = d_y2 * yg * (sq * (1.0 - sq))
        dq_mx = d_q.astype(MXU_DTYPE)
        dq_ref[...] = dq_mx
        yg_ref[...] = yg_mx
        dbglu_ref[...] += jnp.sum(d_q, axis=0, keepdims=True)
        d_yg = d_y2 * sq + _dot_nt(dq_mx, wglu[...])
        dyp_ref[...] = d_yg * dgelu

    def rows(width, col=0):
        return pl.BlockSpec((tm, width), lambda i: (i, col))

    def fixed(width):
        return pl.BlockSpec((1, width), lambda i: (0, 0))

    any_ = pl.BlockSpec(memory_space=pl.ANY)
    return pl.pallas_call(
        body, name="tail", grid=(seq // tm,),
        in_specs=[rows(d_model), rows(d_model), rows(d_ssm, zs_col0 // d_ssm), rows(d_conv), rows(d_ssm),
                  any_, fixed(d_ssm), any_, fixed(d_model)],
        out_specs=[rows(d_model), rows(d_model), rows(d_mix), rows(d_conv), rows(d_ssm), rows(d_ssm), rows(d_ssm),
                   rows(d_ssm), fixed(LANES), fixed(d_model), fixed(d_ssm)],
        out_shape=[jax.ShapeDtypeStruct((seq, d_model), F32), jax.ShapeDtypeStruct((seq, d_model), MXU_DTYPE),
                   jax.ShapeDtypeStruct((seq, d_mix), MXU_DTYPE), jax.ShapeDtypeStruct((seq, d_conv), F32),
                   jax.ShapeDtypeStruct((seq, d_ssm), F32), jax.ShapeDtypeStruct((seq, d_ssm), MXU_DTYPE),
                   jax.ShapeDtypeStruct((seq, d_ssm), MXU_DTYPE), jax.ShapeDtypeStruct((seq, d_ssm), MXU_DTYPE),
                   jax.ShapeDtypeStruct((1, LANES), F32), jax.ShapeDtypeStruct((1, d_model), F32),
                   jax.ShapeDtypeStruct((1, d_ssm), F32)],
        scratch_shapes=[pltpu.VMEM(w_glu.shape, MXU_DTYPE), pltpu.VMEM(w_out.shape, MXU_DTYPE)],
        compiler_params=_params(("arbitrary",)),
    )(xp, tp, proj, yconv, yp, w_glu, b_glu, w_out, g_post)


def _bwd_in(dproj, win_g, xp, dy, g_pre, tm):
    seq, d_model = xp.shape
    nb, _, nc = win_g.shape

    def body(dp_ref, w_ref, x_ref, dy_ref, g_ref, gx_ref, dg_ref, acc):
        i, k = pl.program_id(0), pl.program_id(1)

        @pl.when(k == 0)
        def _():
            acc[...] = jnp.zeros_like(acc)

        @pl.when((i == 0) & (k == 0))
        def _():
            dg_ref[...] = jnp.zeros_like(dg_ref)

        acc[...] += _dot_nt(dp_ref[...], w_ref[...])

        @pl.when(k == nb - 1)
        def _():
            x = x_ref[...]
            r = lax.rsqrt(jnp.mean(x * x, axis=-1, keepdims=True) + EPS)
            xn = x * r
            dh = acc[...]
            dg_ref[...] += jnp.sum(dh * xn, axis=0, keepdims=True)
            dxn = dh * g_ref[...]
            gx_ref[...] = r * (dxn - xn * jnp.mean(dxn * xn, axis=-1, keepdims=True)) + dy_ref[...]

    row = pl.BlockSpec((tm, d_model), lambda i, k: (i, 0))
    vec = pl.BlockSpec((1, d_model), lambda i, k: (0, 0))
    return pl.pallas_call(
        body, name="bwd_in", grid=(seq // tm, nb),
        in_specs=[pl.BlockSpec((tm, nc), lambda i, k: (i, k)), pl.BlockSpec((None, d_model, nc), lambda i, k: (k, 0, 0)),
                  row, row, vec],
        out_specs=[row, vec],
        out_shape=[jax.ShapeDtypeStruct((seq, d_model), F32), jax.ShapeDtypeStruct((1, d_model), F32)],
        scratch_shapes=[pltpu.VMEM((tm, d_model), F32)],
        compiler_params=_params(("arbitrary", "arbitrary")),
    )(dproj, win_g, xp, dy, g_pre)


def _tn_matmul(name, a, b, tm, tn, tk, out_shape, out_block, out_index):
    seq, m = a.shape
    n = b.shape[1]

    def body(a_ref, b_ref, o_ref):
        @pl.when(pl.program_id(2) == 0)
        def _():
            o_ref[...] = jnp.zeros_like(o_ref)

        o_ref[...] += _dot_tn(a_ref[...], b_ref[...])

    return pl.pallas_call(
        body, name=name, grid=(m // tm, n // tn, seq // tk),
        in_specs=[pl.BlockSpec((tk, tm), lambda i, j, k: (k, i)), pl.BlockSpec((tk, tn), lambda i, j, k: (k, j))],
        out_specs=pl.BlockSpec(out_block, out_index),
        out_shape=jax.ShapeDtypeStruct(out_shape, F32),
        compiler_params=_params(("parallel", "parallel", "arbitrary")),
    )(a, b)


def _eye_g():
    return jnp.eye(HALF_G, dtype=F32)


def _bb_blockdiag(bbt_r, bbt_i):
    n_half = bbt_r.shape[0] // HALF_G

    def one(t):
        t = t.reshape(n_half, HALF_G, SSM_GROUP, SSM_STATE)
        t = t[:, :, :, None, :] * _eye_g()[None, :, None, :, None]
        return t.reshape(n_half, HALF_CH, HALF_W)

    return jnp.concatenate([one(bbt_r), one(bbt_i)], axis=-1)


def _cc_blockdiag(c_re, c_im):
    n_half = c_re.shape[0] // HALF_G

    def one(t):
        t = t.reshape(n_half, HALF_G, SSM_GROUP, SSM_STATE)
        t = jnp.transpose(t, (0, 3, 1, 2))
        t = t[:, None, :, :, :] * _eye_g()[None, :, None, :, None]
        return t.reshape(n_half, HALF_W, HALF_CH)

    return jnp.concatenate([one(c_re), one(-c_im)], axis=1)


def _bb_diag(dbb):
    n_half = dbb.shape[0]
    t = dbb.reshape(n_half, HALF_G, SSM_GROUP, 2, HALF_G, SSM_STATE)
    t = jnp.sum(t * _eye_g()[None, :, None, None, :, None], axis=4)
    t = jnp.transpose(t, (3, 0, 1, 2, 4))
    return t.reshape(2, n_half * HALF_G, SSM_GROUP, SSM_STATE)


def _cc_diag(dcc):
    n_half = dcc.shape[0]
    t = dcc.reshape(n_half, 2, HALF_G, SSM_STATE, HALF_G, SSM_GROUP)
    t = jnp.sum(t * _eye_g()[None, None, :, None, :, None], axis=2)
    t = jnp.transpose(t, (1, 0, 3, 4, 2))
    return t.reshape(2, n_half * HALF_G, SSM_GROUP, SSM_STATE)


def _permute_rows(a):
    seq, d = a.shape
    return a.reshape(N_CHUNK, seq // N_CHUNK, d).transpose(1, 0, 2).reshape(seq, d)


def _unpermute_rows(a):
    seq, d = a.shape
    return a.reshape(seq // N_CHUNK, N_CHUNK, d).transpose(1, 0, 2).reshape(seq, d)


def _pack(parts):
    flat = jnp.concatenate([p.reshape(-1) for p in parts])
    pad = (-flat.shape[0]) % (8 * LANES)
    return jnp.pad(flat, (0, pad)).reshape(-1, LANES)


def _unpack(packed, shapes):
    flat, out, o = packed.reshape(-1), [], 0
    for s in shapes:
        n = math.prod(s)
        out.append(flat[o:o + n].reshape(s))
        o += n
    return out


def kernel(x, norm_pre_g, w_in, conv_w, conv_b, ssm_a_re, ssm_a_im, ssm_log_dt, ssm_b_re, ssm_b_im, ssm_c_re, ssm_c_im, ssm_d, w_glu, b_glu, w_out, norm_post_g, loss_target, m_norm_pre_g, m_w_in, m_conv_w, m_conv_b, m_ssm_a_re, m_ssm_a_im, m_ssm_log_dt, m_ssm_b_re, m_ssm_b_im, m_ssm_c_re, m_ssm_c_im, m_ssm_d, m_w_glu, m_b_glu, m_w_out, m_norm_post_g, v_norm_pre_g, v_w_in, v_conv_w, v_conv_b, v_ssm_a_re, v_ssm_a_im, v_ssm_log_dt, v_ssm_b_re, v_ssm_b_im, v_ssm_c_re, v_ssm_c_im, v_ssm_d, v_w_glu, v_b_glu, v_w_out, v_norm_post_g):
    seq, d_model = x.shape[1], x.shape[2]
    d_conv, d_ssm = conv_b.shape[0], ssm_d.shape[0]
    groups, states = ssm_a_re.shape
    assert x.shape[0] == 1 and seq % (8 * N_CHUNK) == 0 and d_conv == d_ssm
    assert (groups, states) == (d_ssm // SSM_GROUP, SSM_STATE) and d_ssm % LANES == 0
    me = 4 * lax.axis_index("x") + 2 * lax.axis_index("y") + lax.axis_index("c")
    tm = min(512, seq)

    xp = _permute_rows(x[0])
    tp = _permute_rows(loss_target[0])
    row = lambda a: a.reshape(1, -1)
    conv_w8 = jnp.pad(conv_w, ((0, 8 - conv_w.shape[0]), (0, 0)))

    win_g, wout_g, wglu_g, convw_g = _exchange(
        "gather_weights", [w_in.astype(MXU_DTYPE), w_out.astype(MXU_DTYPE), w_glu.astype(MXU_DTYPE), conv_w8], [])
    w_out_full = wout_g.reshape(-1, d_model)
    w_glu_full = wglu_g.reshape(-1, d_ssm)
    conv_w_full = jnp.transpose(convw_g, (1, 0, 2)).reshape(8, d_conv)

    g3 = lambda a: a.reshape(groups, 1, -1)
    bt_re, bt_im = jnp.transpose(ssm_b_re, (0, 2, 1)), jnp.transpose(ssm_b_im, (0, 2, 1))
    lbr, lbi, qr, qi, bbt_r, bbt_i = _ssm_prep(g3(ssm_a_re), g3(ssm_a_im), g3(ssm_log_dt), bt_re, bt_im)
    n_half = groups // HALF_G
    lam = jnp.stack([lbr.reshape(n_half, HALF_W), lbi.reshape(n_half, HALF_W)], axis=1)
    bbcat = _bb_blockdiag(bbt_r, bbt_i).astype(MXU_DTYPE)
    cccat = _cc_blockdiag(ssm_c_re, ssm_c_im).astype(MXU_DTYPE)

    proj, h = _fwd_in(xp, row(norm_pre_g), win_g, tm)
    u_col0, zs_col0 = 4 * d_conv, 4 * d_conv + d_ssm
    yconv = _conv_fwd(proj, conv_w_full, row(conv_b), d_conv)
    yp = _ssm_fwd(proj, lam, bbcat, cccat, row(ssm_d), d_ssm, u_col0)
    (dy, d_o, mix, dyc, dyp, dzs, yg, dq, loss_part, dgpost, dbglu) = _tail(
        xp, tp, proj, yconv, yp, w_glu_full, row(b_glu), w_out_full, row(norm_post_g), zs_col0, min(256, seq))

    dbg, dcg, dv, dzc, dconvb, dconvw = _conv_bwd(proj, dyc, conv_w_full, row(conv_b), d_conv)
    du, dbb, dcc, da, dd = _ssm_bwd(proj, dyp, lam, bbcat, cccat, row(ssm_d), d_ssm, u_col0)
    dproj = jnp.concatenate([dbg, dcg, dv, dzc, du, dzs], axis=1)
    gx_p, dgpre = _bwd_in(dproj, win_g, xp, dy, row(norm_pre_g), tm)
    nb, nc = N_DEV, w_in.shape[1]
    tk = min(512, seq)
    dwin_p = _tn_matmul("dw_in", h, dproj, min(512, d_model), nc, tk, (nb, d_model, nc),
                        (None, min(512, d_model), nc), lambda i, j, k: (j, i, 0))
    r_out = w_out.shape[0]
    dwout_p = _tn_matmul("dw_out", mix, d_o, r_out, d_model, tk, (N_DEV, r_out, d_model),
                         (None, r_out, d_model), lambda i, j, k: (i, 0, 0))
    r_glu = w_glu.shape[0]
    dwglu_p = _tn_matmul("dw_glu", yg, dq, r_glu, d_ssm, tk, (N_DEV, r_glu, d_ssm),
                         (None, r_glu, d_ssm), lambda i, j, k: (i, 0, 0))

    da_n = jnp.transpose(da.reshape(n_half, 2, HALF_G, SSM_STATE), (1, 0, 2, 3)).reshape(2, groups, 1, states)
    parts = [dgpre, dgpost, dconvb, dd, dbglu, dconvw[:3], da_n, _bb_diag(dbb), _cc_diag(dcc)]
    shapes = [p.shape for p in parts]
    pack_g, recv_in, recv_out, recv_glu = _exchange("reduce_grads", [_pack(parts)], [dwin_p, dwout_p, dwglu_p])
    (g_gpre, g_gpost, g_convb, g_d, g_bglu, g_convw, g_da, g_dbb, g_dcc) = _unpack(_sum_slots(pack_g), shapes)
    g_convw = lax.dynamic_slice(g_convw, (0, me * conv_w.shape[1]), conv_w.shape)

    tr = lambda a: jnp.transpose(a, (0, 2, 1))
    direct = [(g_gpre, row(norm_pre_g), row(m_norm_pre_g), row(v_norm_pre_g)),
              (g_convb, row(conv_b), row(m_conv_b), row(v_conv_b)),
              (g_d, row(ssm_d), row(m_ssm_d), row(v_ssm_d)),
              (g_bglu, row(b_glu), row(m_b_glu), row(v_b_glu)),
              (g_gpost, row(norm_post_g), row(m_norm_post_g), row(v_norm_post_g)),
              (g_convw, conv_w, m_conv_w, v_conv_w),
              (g_dcc[0], ssm_c_re, m_ssm_c_re, v_ssm_c_re),
              (-g_dcc[1], ssm_c_im, m_ssm_c_im, v_ssm_c_im)]
    ssm = dict(da_r=g_da[0], da_i=g_da[1], dbb_r=g_dbb[0], dbb_i=g_dbb[1], lr=g3(ssm_a_re), li=g3(ssm_a_im),
               ldt=g3(ssm_log_dt), bt_r=bt_re, bt_i=bt_im, lbr=lbr, lbi=lbi, qr=qr, qi=qi,
               w_a_re=g3(ssm_a_re), m_a_re=g3(m_ssm_a_re), v_a_re=g3(v_ssm_a_re),
               w_a_im=g3(ssm_a_im), m_a_im=g3(m_ssm_a_im), v_a_im=g3(v_ssm_a_im),
               w_log_dt=g3(ssm_log_dt), m_log_dt=g3(m_ssm_log_dt), v_log_dt=g3(v_ssm_log_dt),
               w_bt_re=bt_re, m_bt_re=tr(m_ssm_b_re), v_bt_re=tr(v_ssm_b_re),
               w_bt_im=bt_im, m_bt_im=tr(m_ssm_b_im), v_bt_im=tr(v_ssm_b_im))
    small = _small_update(direct, ssm)
    res = {}
    for name, quad, shape in zip(["norm_pre_g", "conv_b", "ssm_d", "b_glu", "norm_post_g", "conv_w", "ssm_c_re", "ssm_c_im"],
                                 small[:8], [norm_pre_g.shape, conv_b.shape, ssm_d.shape, b_glu.shape,
                                             norm_post_g.shape, conv_w.shape, ssm_c_re.shape, ssm_c_im.shape]):
        res[name] = tuple(a.reshape(shape) for a in quad)
    res["ssm_a_re"] = tuple(a.reshape(ssm_a_re.shape) for a in small[8])
    res["ssm_a_im"] = tuple(a.reshape(ssm_a_im.shape) for a in small[9])
    res["ssm_log_dt"] = tuple(a.reshape(ssm_log_dt.shape) for a in small[10])
    res["ssm_b_re"] = tuple(tr(a) for a in small[11])
    res["ssm_b_im"] = tuple(tr(a) for a in small[12])
    res["w_in"] = tuple(_adam_big("adam_w_in", recv_in, w_in, m_w_in, v_w_in, min(256, d_model)))
    res["w_out"] = tuple(_adam_big("adam_w_out", recv_out, w_out, m_w_out, v_w_out, min(128, r_out)))
    res["w_glu"] = tuple(_adam_big("adam_w_glu", recv_glu, w_glu, m_w_glu, v_w_glu, r_glu))

    order = ["norm_pre_g", "w_in", "conv_w", "conv_b", "ssm_a_re", "ssm_a_im", "ssm_log_dt", "ssm_b_re", "ssm_b_im",
             "ssm_c_re", "ssm_c_im", "ssm_d", "w_glu", "b_glu", "w_out", "norm_post_g"]
    loss = lax.psum(loss_part[0, 0], AXES)
    grad_x = _unpermute_rows(gx_p)[None]
    return (loss, grad_x, *[res[n][0] for n in order], *[res[n][1] for n in order],
            *[res[n][2] for n in order], *[res[n][3] for n in order])
```

```python
import functools
import math

import jax
import jax.numpy as jnp
from jax import lax
from jax.experimental import pallas as pl
from jax.experimental.pallas import tpu as pltpu

F32 = jnp.float32
MXU_DTYPE = jnp.bfloat16
AXES = ("x", "y", "c")
N_DEV = 8
N_CHUNK = 8
LANES = 128
SSM_GROUP = 16
SSM_STATE = 64
HALF_CH = 64
HALF_G = HALF_CH // SSM_GROUP
HALF_W = HALF_G * SSM_STATE
EPS = 1e-6
ADAM_LR, ADAM_B1, ADAM_B2, ADAM_EPS, ADAM_WD, ADAM_STEP = 0.001, 0.9, 0.999, 1e-08, 0.01, 10
GELU_C = math.sqrt(2.0 / math.pi)
GELU_K = 0.044715
VMEM_LIMIT = 56 * 1024 * 1024


def _params(sem=None):
    return pltpu.CompilerParams(dimension_semantics=sem, vmem_limit_bytes=VMEM_LIMIT)


def _dot(a, b):
    return jnp.dot(a, b, preferred_element_type=F32)


def _dot_nt(a, b):
    return lax.dot_general(a, b, (((1,), (1,)), ((), ())), preferred_element_type=F32)


def _dot_tn(a, b):
    return lax.dot_general(a, b, (((0,), (0,)), ((), ())), preferred_element_type=F32)


def _sigmoid(z):
    return 1.0 / (1.0 + jnp.exp(-z))


def _flip(v, bit):
    return 1 - v if bit else v


def _peers():
    x, y, c = (lax.axis_index(a) for a in AXES)
    out = []
    for m in range(1, N_DEV):
        px, py, pc = _flip(x, (m >> 2) & 1), _flip(y, (m >> 1) & 1), _flip(c, m & 1)
        out.append((px, py, pc, 4 * px + 2 * py + pc))
    return out


class _Comm:
    def __init__(self, gathers=(), scatters=()):
        self.n_g = len(gathers)
        self.operands = list(gathers) + list(scatters)
        self.n = len(self.operands)

    def out_shape(self):
        return [jax.ShapeDtypeStruct((N_DEV,) + a.shape if t < self.n_g else a.shape, a.dtype)
                for t, a in enumerate(self.operands)]

    def scratch(self):
        if not self.n:
            return []
        return [pltpu.SemaphoreType.DMA((self.n, N_DEV - 1)), pltpu.SemaphoreType.DMA((self.n, N_DEV - 1)),
                pltpu.SemaphoreType.DMA((self.n,))]

    def _copies(self, in_refs, out_refs, sems, arrivals):
        send_sems, recv_sems, local_sems = sems
        x, y, c = (lax.axis_index(a) for a in AXES)
        me = 4 * x + 2 * y + c

        def src(t, dev):
            return in_refs[t] if t < self.n_g else in_refs[t].at[dev]

        local = [pltpu.make_async_copy(src(t, me), out_refs[t].at[me], local_sems.at[t]) for t in range(self.n)]
        sends, recvs = [], []
        for t in range(self.n):
            for m, (px, py, pc, peer) in enumerate(_peers()):
                kw = dict(send_sem=send_sems.at[t, m], recv_sem=recv_sems.at[t, m],
                          device_id=(px, py, pc), device_id_type=pl.DeviceIdType.MESH)
                sends.append(pltpu.make_async_remote_copy(src_ref=src(t, peer), dst_ref=out_refs[t].at[me], **kw))
                if arrivals:
                    recvs.append(pltpu.make_async_remote_copy(src_ref=src(t, peer), dst_ref=out_refs[t].at[peer], **kw))
        return local, sends, recvs

    def start(self, in_refs, out_refs, sems):
        local, sends, _ = self._copies(in_refs, out_refs, sems, arrivals=False)
        for cp in local + sends:
            cp.start()

    def finish(self, in_refs, out_refs, sems):
        local, sends, recvs = self._copies(in_refs, out_refs, sems, arrivals=True)
        for cp in recvs:
            cp.wait_recv()
        for cp in sends:
            cp.wait_send()
        for cp in local:
            cp.wait()


def _call(body, comm, *, name, grid, in_specs, out_specs, out_shape, operands, scratch_shapes=()):
    comm = comm or _Comm()
    n_in, n_out, n_scr, cn = len(in_specs), len(out_specs), len(scratch_shapes), comm.n

    def wrapped(*refs):
        parts, o = [], 0
        for k in (n_in, cn, n_out, cn, n_scr):
            parts.append(refs[o:o + k])
            o += k
        h_in, c_in, h_out, c_out, h_scr = parts
        sems = refs[o:]
        if cn:
            first = functools.reduce(jnp.logical_and, [pl.program_id(d) == 0 for d in range(len(grid))])

            @pl.when(first)
            def _():
                comm.start(c_in, c_out, sems)

        body(*h_in, *h_out, *h_scr)
        if cn:
            last = functools.reduce(jnp.logical_and, [pl.program_id(d) == grid[d] - 1 for d in range(len(grid))])

            @pl.when(last)
            def _():
                comm.finish(c_in, c_out, sems)

    any_ = pl.BlockSpec(memory_space=pl.ANY)
    res = pl.pallas_call(
        wrapped, name=name, grid=grid, in_specs=list(in_specs) + [any_] * cn, out_specs=list(out_specs) + [any_] * cn,
        out_shape=list(out_shape) + comm.out_shape(), scratch_shapes=list(scratch_shapes) + comm.scratch(),
        compiler_params=_params(("arbitrary",) * len(grid)),
    )(*operands, *comm.operands)
    return list(res[:n_out]), list(res[n_out:])


def _exchange(name, gathers, scatters):
    def body(tok_ref):
        tok_ref[...] = jnp.zeros_like(tok_ref)

    return _call(body, _Comm(gathers, scatters), name=name, grid=(1,), in_specs=[],
                 out_specs=[pl.BlockSpec((8, LANES), lambda i: (0, 0))],
                 out_shape=[jax.ShapeDtypeStruct((8, LANES), F32)], operands=[])[1]


def _ssm_prep(a_re, a_im, log_dt, bt_re, bt_im):
    def body(lr_ref, li_ref, ldt_ref, br_ref, bi_ref, lbr_ref, lbi_ref, qr_ref, qi_ref, bbr_ref, bbi_ref):
        lr, li = lr_ref[...], li_ref[...]
        dt = jnp.exp(ldt_ref[...])
        mag = jnp.exp(lr * dt)
        lbr, lbi = mag * jnp.cos(li * dt), mag * jnp.sin(li * dt)
        nr, ni = lbr - 1.0, lbi
        den = lr * lr + li * li
        qr = (nr * lr + ni * li) / den
        qi = (ni * lr - nr * li) / den
        br, bi = br_ref[...], bi_ref[...]
        lbr_ref[...], lbi_ref[...], qr_ref[...], qi_ref[...] = lbr, lbi, qr, qi
        bbr_ref[...] = qr * br - qi * bi
        bbi_ref[...] = qr * bi + qi * br

    s2 = jax.ShapeDtypeStruct(a_re.shape, F32)
    s3 = jax.ShapeDtypeStruct(bt_re.shape, F32)
    return pl.pallas_call(body, name="ssm_prep", out_shape=[s2, s2, s2, s2, s3, s3],
                          compiler_params=_params())(a_re, a_im, log_dt, bt_re, bt_im)


def _adam(w, g, m, v):
    m2 = ADAM_B1 * m + (1.0 - ADAM_B1) * g
    v2 = ADAM_B2 * v + (1.0 - ADAM_B2) * (g * g)
    m_hat = m2 / (1.0 - ADAM_B1 ** ADAM_STEP)
    v_hat = v2 / (1.0 - ADAM_B2 ** ADAM_STEP)
    delta = -ADAM_LR * (m_hat / (jnp.sqrt(v_hat) + ADAM_EPS) + ADAM_WD * w)
    return delta, m2, v2


def _small_update(direct, ssm):
    n_direct = len(direct)
    flat = [a for quad in direct for a in quad]
    names = ["da_r", "da_i", "dbb_r", "dbb_i", "lr", "li", "ldt", "bt_r", "bt_i", "lbr", "lbi", "qr", "qi"]
    flat += [ssm[k] for k in names]
    chain = ["a_re", "a_im", "log_dt", "bt_re", "bt_im"]
    for k in chain:
        flat += [ssm["w_" + k], ssm["m_" + k], ssm["v_" + k]]
    n_in = len(flat)

    def body(*refs):
        ins, outs = refs[:n_in], refs[n_in:]
        for p in range(n_direct):
            g, w, m, v = (r[...] for r in ins[4 * p:4 * p + 4])
            d, m2, v2 = _adam(w, g, m, v)
            outs[4 * p][...], outs[4 * p + 1][...], outs[4 * p + 2][...], outs[4 * p + 3][...] = g, d, m2, v2
        o = 4 * n_direct
        da_r, da_i, dbb_r, dbb_i, lr, li, ldt, bt_r, bt_i, lbr, lbi, qr, qi = (r[...] for r in ins[o:o + 13])
        dt = jnp.exp(ldt)
        g_br = qr * dbb_r + qi * dbb_i
        g_bi = qr * dbb_i - qi * dbb_r
        dq_r = jnp.sum(bt_r * dbb_r + bt_i * dbb_i, axis=1, keepdims=True)
        dq_i = jnp.sum(bt_r * dbb_i - bt_i * dbb_r, axis=1, keepdims=True)
        den = lr * lr + li * li
        cr, ci = lr / den, li / den
        gl_r = da_r + (cr * dq_r - ci * dq_i)
        gl_i = da_i + (cr * dq_i + ci * dq_r)
        w_r = qr * cr + qi * ci
        w_i = qi * cr - qr * ci
        g_lr = dt * (lbr * gl_r + lbi * gl_i) + (-w_r * dq_r - w_i * dq_i)
        g_li = dt * (lbr * gl_i - lbi * gl_r) + (-w_r * dq_i + w_i * dq_r)
        m_r = lr * lbr - li * lbi
        m_i = lr * lbi + li * lbr
        g_ldt = jnp.sum(m_r * gl_r + m_i * gl_i, axis=2, keepdims=True) * dt
        grads = [g_lr, g_li, g_ldt, g_br, g_bi]
        base_in, base_out = o + 13, 4 * n_direct
        for p, g in enumerate(grads):
            w, m, v = (r[...] for r in ins[base_in + 3 * p:base_in + 3 * p + 3])
            d, m2, v2 = _adam(w, g, m, v)
            q = base_out + 4 * p
            outs[q][...], outs[q + 1][...], outs[q + 2][...], outs[q + 3][...] = g, d, m2, v2

    out_shape = []
    for quad in direct:
        out_shape += [jax.ShapeDtypeStruct(quad[1].shape, F32)] * 4
    for k in chain:
        out_shape += [jax.ShapeDtypeStruct(ssm["w_" + k].shape, F32)] * 4
    res = pl.pallas_call(body, name="small_update", out_shape=out_shape, compiler_params=_params())(*flat)
    return [tuple(res[4 * p:4 * p + 4]) for p in range(n_direct + len(chain))]


def _sum_slots(name, pack):
    def body(p_ref, o_ref):
        acc = p_ref[0]
        for k in range(1, N_DEV):
            acc = acc + p_ref[k]
        o_ref[...] = acc

    return pl.pallas_call(body, name=name, out_shape=jax.ShapeDtypeStruct(pack.shape[1:], F32),
                          compiler_params=_params())(pack)


def _adam_big(name, recv, w, m, v, tr):
    _, rows, cols = recv.shape

    def body(r_ref, w_ref, m_ref, v_ref, g_ref, d_ref, m2_ref, v2_ref):
        g = r_ref[0].astype(F32)
        for k in range(1, N_DEV):
            g = g + r_ref[k].astype(F32)
        d, m2, v2 = _adam(w_ref[...], g, m_ref[...], v_ref[...])
        g_ref[...], d_ref[...], m2_ref[...], v2_ref[...] = g, d, m2, v2

    blk = pl.BlockSpec((tr, cols), lambda i: (i, 0))
    shp = jax.ShapeDtypeStruct((rows, cols), F32)
    return pl.pallas_call(
        body, name=name, grid=(rows // tr,),
        in_specs=[pl.BlockSpec((N_DEV, tr, cols), lambda i: (0, i, 0)), blk, blk, blk],
        out_specs=[blk] * 4, out_shape=[shp] * 4, compiler_params=_params(("parallel",)),
    )(recv, w, m, v)


def _norm_in(xp, g_pre, tm):
    seq, d_model = xp.shape

    def body(x_ref, g_ref, h_ref, ht_ref):
        x = x_ref[...]
        r = lax.rsqrt(jnp.mean(x * x, axis=-1, keepdims=True) + EPS)
        h = x * r * g_ref[...]
        h_ref[...] = h.astype(h_ref.dtype)
        ht_ref[...] = h.T.astype(ht_ref.dtype)

    return pl.pallas_call(
        body, name="norm_in", grid=(seq // tm,),
        in_specs=[pl.BlockSpec((tm, d_model), lambda i: (i, 0)), pl.BlockSpec((1, d_model), lambda i: (0, 0))],
        out_specs=[pl.BlockSpec((tm, d_model), lambda i: (i, 0)), pl.BlockSpec((d_model, tm), lambda i: (0, i))],
        out_shape=[jax.ShapeDtypeStruct((seq, d_model), MXU_DTYPE), jax.ShapeDtypeStruct((d_model, seq), MXU_DTYPE)],
        compiler_params=_params(("parallel",)),
    )(xp, g_pre)


GATHER_ORDER = (0, 1, 4, 2, 6, 5, 3, 7)


def _fwd_in(h, w_shard, order, comm, tm):
    seq, d_model = h.shape
    nc = w_shard.shape[1]
    n_i = seq // tm
    cn = comm.n

    def body(order_ref, h_ref, w_hbm, *rest):
        c_in, rest = rest[:cn], rest[cn:]
        proj_ref, wing = rest[0], rest[1]
        c_out, rest = rest[2:2 + cn], rest[2 + cn:]
        wbuf, send_sems, recv_sems, own_sem, load_sem = rest[:5]
        c_sems = rest[5:]
        k, i = pl.program_id(0), pl.program_id(1)
        x, y, c = (lax.axis_index(a) for a in AXES)
        me = 4 * x + 2 * y + c

        def dev(rel):
            return _flip(x, (rel >> 2) & 1), _flip(y, (rel >> 1) & 1), _flip(c, rel & 1)

        def slot(rel):
            px, py, pc = dev(rel)
            return 4 * px + 2 * py + pc

        def remote(src, block, to_rel, sem):
            return pltpu.make_async_remote_copy(
                src_ref=src, dst_ref=wing.at[block], send_sem=send_sems.at[sem], recv_sem=recv_sems.at[sem],
                device_id=dev(to_rel), device_id_type=pl.DeviceIdType.MESH)

        own = pltpu.make_async_copy(w_hbm, wing.at[me], own_sem)
        first_hand = [remote(w_hbm, me, GATHER_ORDER[p], p - 1) for p in range(1, 5)]
        passed_on = [remote(wing.at[slot(GATHER_ORDER[p])], slot(GATHER_ORDER[p]), 1, p + 2) for p in range(2, 5)]

        @pl.when((k == 0) & (i == 0))
        def _():
            own.start()
            for cp in first_hand:
                cp.start()
            comm.start(c_in, c_out, c_sems)

        @pl.when(i == 0)
        def _():
            for p in range(N_DEV):
                @pl.when(k == p)
                def _():
                    if p == 0:
                        own.wait()
                    else:
                        remote(w_hbm, slot(GATHER_ORDER[p]), GATHER_ORDER[p], p - 1).wait_recv()
                    if 2 <= p <= 4:
                        passed_on[p - 2].start()
            load = pltpu.make_async_copy(wing.at[order_ref[k]], wbuf, load_sem)
            load.start()
            load.wait()

        proj_ref[...] = _dot(h_ref[...], wbuf[...])

        @pl.when((k == N_DEV - 1) & (i == n_i - 1))
        def _():
            for cp in first_hand + passed_on:
                cp.wait_send()
            comm.finish(c_in, c_out, c_sems)

    any_ = pl.BlockSpec(memory_space=pl.ANY)
    grid_spec = pltpu.PrefetchScalarGridSpec(
        num_scalar_prefetch=1, grid=(N_DEV, n_i),
        in_specs=[pl.BlockSpec((tm, d_model), lambda k, i, o: (i, 0)), any_] + [any_] * cn,
        out_specs=[pl.BlockSpec((tm, nc), lambda k, i, o: (i, o[k])), any_] + [any_] * cn,
        scratch_shapes=[pltpu.VMEM((d_model, nc), w_shard.dtype), pltpu.SemaphoreType.DMA((N_DEV - 1,)),
                        pltpu.SemaphoreType.DMA((N_DEV - 1,)), pltpu.SemaphoreType.DMA, pltpu.SemaphoreType.DMA]
        + comm.scratch())
    res = pl.pallas_call(
        body, name="fwd_in", grid_spec=grid_spec,
        out_shape=[jax.ShapeDtypeStruct((seq, N_DEV * nc), F32),
                   jax.ShapeDtypeStruct((N_DEV, d_model, nc), w_shard.dtype)] + comm.out_shape(),
        compiler_params=_params(("arbitrary", "arbitrary")),
    )(order, h, w_shard, *comm.operands)
    return res[0], res[1], list(res[2:])


def _shift_prev(a):
    n = a.shape[0]
    last = a[n - N_CHUNK:, :]
    row = lax.broadcasted_iota(jnp.int32, last.shape, 0)
    wrap = jnp.where(row == 0, 0.0, pltpu.roll(last, 1, axis=0))
    return jnp.concatenate([wrap, a[:n - N_CHUNK, :]], axis=0)


def _shift_next(a):
    first = a[:N_CHUNK, :]
    row = lax.broadcasted_iota(jnp.int32, first.shape, 0)
    wrap = jnp.where(row == N_CHUNK - 1, 0.0, pltpu.roll(first, N_CHUNK - 1, axis=0))
    return jnp.concatenate([a[N_CHUNK:, :], wrap], axis=0)


def _conv_specs(seq, d_conv):
    nblk = d_conv // LANES
    return [pl.BlockSpec((seq, LANES), functools.partial(lambda i, o: (0, o + i), o=q * nblk)) for q in range(4)]


def _conv_fwd(proj, conv_w8, conv_b, d_conv):
    seq = proj.shape[0]

    def body(bg_ref, cg_ref, v_ref, zc_ref, w_ref, b_ref, y_ref):
        cv = cg_ref[...] * v_ref[...]
        s1 = _shift_prev(cv)
        s2 = _shift_prev(s1)
        conv = b_ref[...] + w_ref[0:1, :] * s2 + w_ref[1:2, :] * s1 + w_ref[2:3, :] * cv
        z = zc_ref[...]
        y_ref[...] = bg_ref[...] * conv * (z * _sigmoid(z))

    col = pl.BlockSpec((seq, LANES), lambda i: (0, i))
    return pl.pallas_call(
        body, name="conv_fwd", grid=(d_conv // LANES,),
        in_specs=_conv_specs(seq, d_conv) + [pl.BlockSpec((8, LANES), lambda i: (0, i)), pl.BlockSpec((1, LANES), lambda i: (0, i))],
        out_specs=col, out_shape=jax.ShapeDtypeStruct((seq, d_conv), F32),
        compiler_params=_params(("parallel",)),
    )(proj, proj, proj, proj, conv_w8, conv_b)


def _conv_bwd(proj, dyc, conv_w8, conv_b, d_conv):
    seq = proj.shape[0]

    def body(bg_ref, cg_ref, v_ref, zc_ref, dy_ref, w_ref, b_ref, dbg_ref, dcg_ref, dv_ref, dzc_ref, dcb_ref, dcw_ref):
        bg, cg, v, z = bg_ref[...], cg_ref[...], v_ref[...], zc_ref[...]
        w0, w1, w2 = w_ref[0:1, :], w_ref[1:2, :], w_ref[2:3, :]
        cv = cg * v
        s1 = _shift_prev(cv)
        s2 = _shift_prev(s1)
        conv = b_ref[...] + w0 * s2 + w1 * s1 + w2 * cv
        sig = _sigmoid(z)
        dy = dy_ref[...]
        g1 = dy * (z * sig)
        d_conv_ = g1 * bg
        dbg_ref[...] = (g1 * conv).astype(dbg_ref.dtype)
        dzc_ref[...] = (dy * bg * conv * (sig * (1.0 + z * (1.0 - sig)))).astype(dzc_ref.dtype)
        n1 = _shift_next(d_conv_)
        n2 = _shift_next(n1)
        d_cv = w2 * d_conv_ + w1 * n1 + w0 * n2
        dcg_ref[...] = (d_cv * v).astype(dcg_ref.dtype)
        dv_ref[...] = (d_cv * cg).astype(dv_ref.dtype)
        dcb_ref[...] = jnp.sum(d_conv_, axis=0, keepdims=True)
        rows = [jnp.sum(d_conv_ * s, axis=0, keepdims=True) for s in (s2, s1, cv)]
        dcw_ref[...] = jnp.concatenate(rows + [jnp.zeros((5, LANES), F32)], axis=0)

    col = pl.BlockSpec((seq, LANES), lambda i: (0, i))
    big = jax.ShapeDtypeStruct((seq, d_conv), MXU_DTYPE)
    return pl.pallas_call(
        body, name="conv_bwd", grid=(d_conv // LANES,),
        in_specs=_conv_specs(seq, d_conv) + [col, pl.BlockSpec((8, LANES), lambda i: (0, i)), pl.BlockSpec((1, LANES), lambda i: (0, i))],
        out_specs=[col] * 4 + [pl.BlockSpec((1, LANES), lambda i: (0, i)), pl.BlockSpec((8, LANES), lambda i: (0, i))],
        out_shape=[big] * 4 + [jax.ShapeDtypeStruct((1, d_conv), F32), jax.ShapeDtypeStruct((8, d_conv), F32)],
        compiler_params=_params(("parallel",)),
    )(proj, proj, proj, proj, dyc, conv_w8, conv_b)


def _cmul(ar, ai, br, bi):
    return ar * br - ai * bi, ar * bi + ai * br


def _cpow(ar, ai, n):
    rr, ri = jnp.ones_like(ar), jnp.zeros_like(ai)
    while n:
        if n & 1:
            rr, ri = _cmul(rr, ri, ar, ai)
        n >>= 1
        if n:
            ar, ai = _cmul(ar, ai, ar, ai)
    return rr, ri


def _down(v, k):
    row = lax.broadcasted_iota(jnp.int32, v.shape, 0)
    return jnp.where(row >= k, pltpu.roll(v, k, axis=0), 0.0)


def _up(v, k):
    row = lax.broadcasted_iota(jnp.int32, v.shape, 0)
    return jnp.where(row < N_CHUNK - k, pltpu.roll(v, N_CHUNK - k, axis=0), 0.0)


def _chunk_carry(fr, fi, mr, mi, shift):
    vr, vi = shift(fr, 1), shift(fi, 1)
    for k in (1, 2, 4):
        pr, pi = _cmul(mr, mi, shift(vr, k), shift(vi, k))
        vr, vi = vr + pr, vi + pi
        mr, mi = _cmul(mr, mi, mr, mi)
    return vr, vi


def _scan(s_ref, ar, ai, steps, width, reverse):
    def step(q, carry):
        sr, si = carry
        j = steps - 1 - q if reverse else q
        r = pl.multiple_of(j * N_CHUNK, N_CHUNK)
        nr = ar * sr - ai * si + s_ref[pl.ds(r, N_CHUNK), 0:width]
        ni = ar * si + ai * sr + s_ref[pl.ds(r, N_CHUNK), width:2 * width]
        s_ref[pl.ds(r, N_CHUNK), 0:width] = nr
        s_ref[pl.ds(r, N_CHUNK), width:2 * width] = ni
        return nr, ni

    z = jnp.zeros((N_CHUNK, width), F32)
    return lax.fori_loop(0, steps, step, (z, z))


def _patch(s_ref, ar, ai, cr, ci, steps, width, reverse):
    def step(q, pw):
        pr, pi = pw
        j = steps - 1 - q if reverse else q
        r = pl.multiple_of(j * N_CHUNK, N_CHUNK)
        fr, fi = _cmul(pr, pi, cr, ci)
        s_ref[pl.ds(r, N_CHUNK), 0:width] = s_ref[pl.ds(r, N_CHUNK), 0:width] + fr
        s_ref[pl.ds(r, N_CHUNK), width:2 * width] = s_ref[pl.ds(r, N_CHUNK), width:2 * width] + fi
        return _cmul(pr, pi, ar, ai)

    lax.fori_loop(0, steps, step, (ar, ai))


def _local_states(s_ref, u_half, bb, lam_ref, hh, steps, width):
    s_ref[...] = _dot(u_half.astype(MXU_DTYPE), bb)
    ar = jnp.broadcast_to(lam_ref[hh, 0:1, :], (N_CHUNK, width))
    ai = jnp.broadcast_to(lam_ref[hh, 1:2, :], (N_CHUNK, width))
    fr, fi = _scan(s_ref, ar, ai, steps, width, reverse=False)
    pr, pi = _cpow(ar, ai, steps)
    cr, ci = _chunk_carry(fr, fi, pr, pi, _down)
    return ar, ai, cr, ci


def _ssm_specs(seq, col0):
    return dict(
        col=pl.BlockSpec((seq, LANES), lambda i: (0, col0 + i)),
        lam=pl.BlockSpec((2, 2, HALF_W), lambda i: (i, 0, 0)),
        bb=pl.BlockSpec((2, HALF_CH, 2 * HALF_W), lambda i: (i, 0, 0)),
        cc=pl.BlockSpec((2, 2 * HALF_W, HALF_CH), lambda i: (i, 0, 0)),
        vec=pl.BlockSpec((1, LANES), lambda i: (0, i)),
        out=pl.BlockSpec((seq, LANES), lambda i: (0, i)),
    )


def _ssm_fwd(proj, lam, bbcat, cccat, d_skip, d_ssm, u_col0, comm=None):
    seq = proj.shape[0]
    steps = seq // N_CHUNK

    def body(u_ref, lam_ref, bb_ref, cc_ref, d_ref, yp_ref, s_ref):
        for hh in range(2):
            lanes = slice(HALF_CH * hh, HALF_CH * (hh + 1))
            u_half = u_ref[:, lanes]
            ar, ai, cr, ci = _local_states(s_ref, u_half, bb_ref[hh], lam_ref, hh, steps, HALF_W)
            _patch(s_ref, ar, ai, cr, ci, steps, HALF_W, reverse=False)
            y = _dot(s_ref[...].astype(MXU_DTYPE), cc_ref[hh])
            yp_ref[:, lanes] = y + d_ref[:, lanes] * u_half

    sp = _ssm_specs(seq, u_col0 // LANES)
    return _call(
        body, comm, name="ssm_fwd", grid=(d_ssm // LANES,),
        in_specs=[sp["col"], sp["lam"], sp["bb"], sp["cc"], sp["vec"]], out_specs=[sp["out"]],
        out_shape=[jax.ShapeDtypeStruct((seq, d_ssm), F32)],
        scratch_shapes=[pltpu.VMEM((seq, 2 * HALF_W), F32)],
        operands=[proj, lam, bbcat, cccat, d_skip])


def _ssm_bwd(proj, dyp, lam, bbcat, cccat, d_skip, d_ssm, u_col0, comm=None):
    seq = proj.shape[0]
    steps = seq // N_CHUNK
    n_half = 2 * d_ssm // LANES
    width = HALF_W

    def body(u_ref, dyp_ref, lam_ref, bb_ref, cc_ref, d_ref, du_ref, dbb_ref, dcc_ref, da_ref, dd_ref, s_ref, g_ref):
        for hh in range(2):
            lanes = slice(HALF_CH * hh, HALF_CH * (hh + 1))
            u_half, dy_half = u_ref[:, lanes], dyp_ref[:, lanes]
            dy_mx = dy_half.astype(MXU_DTYPE)
            ar, ai, cr, ci = _local_states(s_ref, u_half, bb_ref[hh], lam_ref, hh, steps, width)
            g_ref[...] = _dot_nt(dy_mx, cc_ref[hh])
            nai = -ai
            lr_, li_ = _scan(g_ref, ar, nai, steps, width, reverse=True)
            pr, pi = _cpow(ar, nai, steps)
            gr, gi = _chunk_carry(lr_, li_, pr, pi, _up)
            _patch(g_ref, ar, nai, gr, gi, steps, width, reverse=True)

            def step(j, carry):
                pwr, pwi, sr, si, accr, acci = carry
                r = pl.multiple_of(j * N_CHUNK, N_CHUNK)
                fr, fi = _cmul(pwr, pwi, cr, ci)
                nr = s_ref[pl.ds(r, N_CHUNK), 0:width] + fr
                ni = s_ref[pl.ds(r, N_CHUNK), width:2 * width] + fi
                s_ref[pl.ds(r, N_CHUNK), 0:width] = nr
                s_ref[pl.ds(r, N_CHUNK), width:2 * width] = ni
                qr = g_ref[pl.ds(r, N_CHUNK), 0:width]
                qi = g_ref[pl.ds(r, N_CHUNK), width:2 * width]
                accr = accr + (sr * qr + si * qi)
                acci = acci + (sr * qi - si * qr)
                pwr, pwi = _cmul(pwr, pwi, ar, ai)
                return pwr, pwi, nr, ni, accr, acci

            z = jnp.zeros((N_CHUNK, width), F32)
            _, _, _, _, accr, acci = lax.fori_loop(0, steps, step, (ar, ai, cr, ci, z, z))
            da_ref[hh, :, 0:width] = jnp.sum(accr, axis=0, keepdims=True)
            da_ref[hh, :, width:2 * width] = jnp.sum(acci, axis=0, keepdims=True)

            g_mx = g_ref[...].astype(MXU_DTYPE)
            dcc_ref[hh] = _dot_tn(s_ref[...].astype(MXU_DTYPE), dy_mx)
            dbb_ref[hh] = _dot_tn(u_half.astype(MXU_DTYPE), g_mx)
            du = _dot_nt(g_mx, bb_ref[hh]) + d_ref[:, lanes] * dy_half
            du_ref[:, lanes] = du.astype(du_ref.dtype)
            dd_ref[:, lanes] = jnp.sum(dy_half * u_half, axis=0, keepdims=True)

    sp = _ssm_specs(seq, u_col0 // LANES)
    return _call(
        body, comm, name="ssm_bwd", grid=(d_ssm // LANES,),
        in_specs=[sp["col"], sp["out"], sp["lam"], sp["bb"], sp["cc"], sp["vec"]],
        out_specs=[sp["out"], sp["bb"], sp["cc"], pl.BlockSpec((2, 1, 2 * width), lambda i: (i, 0, 0)), sp["vec"]],
        out_shape=[jax.ShapeDtypeStruct((seq, d_ssm), MXU_DTYPE),
                   jax.ShapeDtypeStruct((n_half, HALF_CH, 2 * width), F32),
                   jax.ShapeDtypeStruct((n_half, 2 * width, HALF_CH), F32),
                   jax.ShapeDtypeStruct((n_half, 1, 2 * width), F32),
                   jax.ShapeDtypeStruct((1, d_ssm), F32)],
        scratch_shapes=[pltpu.VMEM((seq, 2 * width), F32), pltpu.VMEM((seq, 2 * width), F32)],
        operands=[proj, dyp, lam, bbcat, cccat, d_skip])


def _tail(xp, tp, proj, yconv, yp, w_glu, b_glu, w_out, g_post, zs_col0, tm):
    seq, d_model = xp.shape
    d_conv, d_ssm = yconv.shape[1], yp.shape[1]
    d_mix = d_conv + d_ssm
    assert zs_col0 % d_ssm == 0

    def body(x_ref, t_ref, zs_ref, yc_ref, yp_ref, wglu_hbm, bglu_ref, wout_hbm, gpost_ref,
             dy_ref, do_ref, mixt_ref, dyc_ref, dyp_ref, dzs_ref, ygt_ref, dq_ref, loss_ref, dgpost_ref, dbglu_ref,
             wglu, wout):
        @pl.when(pl.program_id(0) == 0)
        def _():
            pltpu.sync_copy(wglu_hbm, wglu)
            pltpu.sync_copy(wout_hbm, wout)
            loss_ref[...] = jnp.zeros_like(loss_ref)
            dgpost_ref[...] = jnp.zeros_like(dgpost_ref)
            dbglu_ref[...] = jnp.zeros_like(dbglu_ref)

        a = yp_ref[...]
        th = jnp.tanh(GELU_C * (a + GELU_K * (a * a * a)))
        yg = a * (0.5 * (1.0 + th))
        dgelu = 0.5 * (1.0 + th) + 0.5 * a * (1.0 - th * th) * (GELU_C * (1.0 + 3.0 * GELU_K * a * a))
        yg_mx = yg.astype(MXU_DTYPE)
        sq = _sigmoid(_dot(yg_mx, wglu[...]) + bglu_ref[...])
        y2 = yg * sq
        zs = zs_ref[...]
        sz = _sigmoid(zs)
        silz = zs * sz
        yc, ys = yc_ref[...], y2 * silz
        mix = jnp.concatenate([yc, ys], axis=1).astype(MXU_DTYPE)
        mixt_ref[0:d_conv, :] = yc.T.astype(MXU_DTYPE)
        mixt_ref[d_conv:, :] = ys.T.astype(MXU_DTYPE)
        o = _dot(mix, wout[...])
        r2 = lax.rsqrt(jnp.mean(o * o, axis=-1, keepdims=True) + EPS)
        on = o * r2
        gpost = gpost_ref[...]
        err = (x_ref[...] + on * gpost) - t_ref[...]
        loss_ref[...] += 0.5 * jnp.sum(jnp.mean(err * err, axis=-1, keepdims=True), axis=0, keepdims=True)
        dy = err * (1.0 / d_model)
        dy_ref[...] = dy
        dgpost_ref[...] += jnp.sum(dy * on, axis=0, keepdims=True)
        d_on = dy * gpost
        d_o = r2 * (d_on - on * jnp.mean(d_on * on, axis=-1, keepdims=True))
        do_mx = d_o.astype(MXU_DTYPE)
        do_ref[...] = do_mx
        d_mix_ = _dot_nt(do_mx, wout[...])
        dyc_ref[...] = d_mix_[:, :d_conv]
        d_yssm = d_mix_[:, d_conv:]
        d_y2 = d_yssm * silz
        dzs_ref[...] = (d_yssm * y2 * (sz * (1.0 + zs * (1.0 - sz)))).astype(dzs_ref.dtype)
        d_q = d_y2 * yg * (sq * (1.0 - sq))
        dq_mx = d_q.astype(MXU_DTYPE)
        dq_ref[...] = dq_mx
        ygt_ref[...] = yg.T.astype(MXU_DTYPE)
        dbglu_ref[...] += jnp.sum(d_q, axis=0, keepdims=True)
        d_yg = d_y2 * sq + _dot_nt(dq_mx, wglu[...])
        dyp_ref[...] = d_yg * dgelu

    def rows(width, col=0):
        return pl.BlockSpec((tm, width), lambda i: (i, col))

    def fixed(width):
        return pl.BlockSpec((1, width), lambda i: (0, 0))

    def cols(height):
        return pl.BlockSpec((height, tm), lambda i: (0, i))

    any_ = pl.BlockSpec(memory_space=pl.ANY)
    return pl.pallas_call(
        body, name="tail", grid=(seq // tm,),
        in_specs=[rows(d_model), rows(d_model), rows(d_ssm, zs_col0 // d_ssm), rows(d_conv), rows(d_ssm),
                  any_, fixed(d_ssm), any_, fixed(d_model)],
        out_specs=[rows(d_model), rows(d_model), cols(d_mix), rows(d_conv), rows(d_ssm), rows(d_ssm), cols(d_ssm),
                   rows(d_ssm), fixed(LANES), fixed(d_model), fixed(d_ssm)],
        out_shape=[jax.ShapeDtypeStruct((seq, d_model), F32), jax.ShapeDtypeStruct((seq, d_model), MXU_DTYPE),
                   jax.ShapeDtypeStruct((d_mix, seq), MXU_DTYPE), jax.ShapeDtypeStruct((seq, d_conv), F32),
                   jax.ShapeDtypeStruct((seq, d_ssm), F32), jax.ShapeDtypeStruct((seq, d_ssm), MXU_DTYPE),
                   jax.ShapeDtypeStruct((d_ssm, seq), MXU_DTYPE), jax.ShapeDtypeStruct((seq, d_ssm), MXU_DTYPE),
                   jax.ShapeDtypeStruct((1, LANES), F32), jax.ShapeDtypeStruct((1, d_model), F32),
                   jax.ShapeDtypeStruct((1, d_ssm), F32)],
        scratch_shapes=[pltpu.VMEM(w_glu.shape, MXU_DTYPE), pltpu.VMEM(w_out.shape, MXU_DTYPE)],
        compiler_params=_params(("arbitrary",)),
    )(xp, tp, proj, yconv, yp, w_glu, b_glu, w_out, g_post)


def _bwd_in(dproj, win_g, xp, dy, g_pre, comm, tm):
    seq, d_model = xp.shape
    nb, _, nc = win_g.shape

    def body(dp_ref, w_hbm, x_ref, dy_ref, g_ref, gx_ref, dg_ref, acc, w_all):
        i, k = pl.program_id(0), pl.program_id(1)

        @pl.when(k == 0)
        def _():
            acc[...] = jnp.zeros_like(acc)

        @pl.when((i == 0) & (k == 0))
        def _():
            dg_ref[...] = jnp.zeros_like(dg_ref)
            pltpu.sync_copy(w_hbm, w_all)

        acc[...] += _dot_nt(dp_ref[...], w_all[k])

        @pl.when(k == nb - 1)
        def _():
            x = x_ref[...]
            r = lax.rsqrt(jnp.mean(x * x, axis=-1, keepdims=True) + EPS)
            xn = x * r
            dh = acc[...]
            dg_ref[...] += jnp.sum(dh * xn, axis=0, keepdims=True)
            dxn = dh * g_ref[...]
            gx_ref[...] = r * (dxn - xn * jnp.mean(dxn * xn, axis=-1, keepdims=True)) + dy_ref[...]

    row = pl.BlockSpec((tm, d_model), lambda i, k: (i, 0))
    vec = pl.BlockSpec((1, d_model), lambda i, k: (0, 0))
    return _call(
        body, comm, name="bwd_in", grid=(seq // tm, nb),
        in_specs=[pl.BlockSpec((tm, nc), lambda i, k: (i, k)), pl.BlockSpec(memory_space=pl.ANY), row, row, vec],
        out_specs=[row, vec],
        out_shape=[jax.ShapeDtypeStruct((seq, d_model), F32), jax.ShapeDtypeStruct((1, d_model), F32)],
        scratch_shapes=[pltpu.VMEM((tm, d_model), F32), pltpu.VMEM(win_g.shape, win_g.dtype)],
        operands=[dproj, win_g, xp, dy, g_pre])


def _wgrad(name, at, b, tm, tn, out_shape, out_block, out_index, comm=None):
    m, seq = at.shape
    n = b.shape[1]

    def body(a_ref, b_ref, o_ref):
        o_ref[...] = _dot(a_ref[...], b_ref[...]).astype(o_ref.dtype)

    return _call(
        body, comm, name=name, grid=(n // tn, m // tm),
        in_specs=[pl.BlockSpec((tm, seq), lambda j, i: (i, 0)), pl.BlockSpec((seq, tn), lambda j, i: (0, j))],
        out_specs=[pl.BlockSpec(out_block, lambda j, i: out_index(i, j))],
        out_shape=[jax.ShapeDtypeStruct(out_shape, MXU_DTYPE)],
        operands=[at, b])


def _eye_g():
    return jnp.eye(HALF_G, dtype=F32)


def _bb_blockdiag(bbt_r, bbt_i):
    n_half = bbt_r.shape[0] // HALF_G

    def one(t):
        t = t.reshape(n_half, HALF_G, SSM_GROUP, SSM_STATE)
        t = t[:, :, :, None, :] * _eye_g()[None, :, None, :, None]
        return t.reshape(n_half, HALF_CH, HALF_W)

    return jnp.concatenate([one(bbt_r), one(bbt_i)], axis=-1)


def _cc_blockdiag(c_re, c_im):
    n_half = c_re.shape[0] // HALF_G

    def one(t):
        t = t.reshape(n_half, HALF_G, SSM_GROUP, SSM_STATE)
        t = jnp.transpose(t, (0, 3, 1, 2))
        t = t[:, None, :, :, :] * _eye_g()[None, :, None, :, None]
        return t.reshape(n_half, HALF_W, HALF_CH)

    return jnp.concatenate([one(c_re), one(-c_im)], axis=1)


def _bb_diag(dbb):
    n_half = dbb.shape[0]
    t = dbb.reshape(n_half, HALF_G, SSM_GROUP, 2, HALF_G, SSM_STATE)
    t = jnp.sum(t * _eye_g()[None, :, None, None, :, None], axis=4)
    t = jnp.transpose(t, (3, 0, 1, 2, 4))
    return t.reshape(2, n_half * HALF_G, SSM_GROUP, SSM_STATE)


def _cc_diag(dcc):
    n_half = dcc.shape[0]
    t = dcc.reshape(n_half, 2, HALF_G, SSM_STATE, HALF_G, SSM_GROUP)
    t = jnp.sum(t * _eye_g()[None, None, :, None, :, None], axis=2)
    t = jnp.transpose(t, (1, 0, 3, 4, 2))
    return t.reshape(2, n_half * HALF_G, SSM_GROUP, SSM_STATE)


def _permute_rows(a):
    seq, d = a.shape
    return a.reshape(N_CHUNK, seq // N_CHUNK, d).transpose(1, 0, 2).reshape(seq, d)


def _unpermute_rows(a):
    seq, d = a.shape
    return a.reshape(seq // N_CHUNK, N_CHUNK, d).transpose(1, 0, 2).reshape(seq, d)


def _pack(parts):
    flat = jnp.concatenate([p.reshape(-1) for p in parts])
    pad = (-flat.shape[0]) % (8 * LANES)
    return jnp.pad(flat, (0, pad)).reshape(-1, LANES)


def _unpack(packed, shapes):
    flat, out, o = packed.reshape(-1), [], 0
    for s in shapes:
        n = math.prod(s)
        out.append(flat[o:o + n].reshape(s))
        o += n
    return out


def kernel(x, norm_pre_g, w_in, conv_w, conv_b, ssm_a_re, ssm_a_im, ssm_log_dt, ssm_b_re, ssm_b_im, ssm_c_re, ssm_c_im, ssm_d, w_glu, b_glu, w_out, norm_post_g, loss_target, m_norm_pre_g, m_w_in, m_conv_w, m_conv_b, m_ssm_a_re, m_ssm_a_im, m_ssm_log_dt, m_ssm_b_re, m_ssm_b_im, m_ssm_c_re, m_ssm_c_im, m_ssm_d, m_w_glu, m_b_glu, m_w_out, m_norm_post_g, v_norm_pre_g, v_w_in, v_conv_w, v_conv_b, v_ssm_a_re, v_ssm_a_im, v_ssm_log_dt, v_ssm_b_re, v_ssm_b_im, v_ssm_c_re, v_ssm_c_im, v_ssm_d, v_w_glu, v_b_glu, v_w_out, v_norm_post_g):
    seq, d_model = x.shape[1], x.shape[2]
    d_conv, d_ssm = conv_b.shape[0], ssm_d.shape[0]
    groups, states = ssm_a_re.shape
    assert x.shape[0] == 1 and seq % (8 * N_CHUNK) == 0 and d_conv == d_ssm
    assert (groups, states) == (d_ssm // SSM_GROUP, SSM_STATE) and d_ssm % LANES == 0
    me = 4 * lax.axis_index("x") + 2 * lax.axis_index("y") + lax.axis_index("c")
    tm = min(512, seq)

    xp = _permute_rows(x[0])
    tp = _permute_rows(loss_target[0])
    row = lambda a: a.reshape(1, -1)
    conv_w8 = jnp.pad(conv_w, ((0, 8 - conv_w.shape[0]), (0, 0)))

    g3 = lambda a: a.reshape(groups, 1, -1)
    bt_re, bt_im = jnp.transpose(ssm_b_re, (0, 2, 1)), jnp.transpose(ssm_b_im, (0, 2, 1))
    lbr, lbi, qr, qi, bbt_r, bbt_i = _ssm_prep(g3(ssm_a_re), g3(ssm_a_im), g3(ssm_log_dt), bt_re, bt_im)
    n_half = groups // HALF_G
    lam = jnp.stack([lbr.reshape(n_half, HALF_W), lbi.reshape(n_half, HALF_W)], axis=1)
    bbcat = _bb_blockdiag(bbt_r, bbt_i).astype(MXU_DTYPE)
    cccat = _cc_blockdiag(ssm_c_re, ssm_c_im).astype(MXU_DTYPE)

    h, ht = _norm_in(xp, row(norm_pre_g), tm)
    order = jnp.stack([jnp.bitwise_xor(me, r) for r in GATHER_ORDER]).astype(jnp.int32)
    proj, win_g, (convw_g,) = _fwd_in(h, w_in.astype(MXU_DTYPE), order, _Comm([conv_w8]), tm)
    conv_w_full = jnp.transpose(convw_g, (1, 0, 2)).reshape(8, d_conv)
    u_col0, zs_col0 = 4 * d_conv, 4 * d_conv + d_ssm
    yconv = _conv_fwd(proj, conv_w_full, row(conv_b), d_conv)
    (yp,), (wout_g, wglu_g) = _ssm_fwd(proj, lam, bbcat, cccat, row(ssm_d), d_ssm, u_col0,
                                       _Comm([w_out.astype(MXU_DTYPE), w_glu.astype(MXU_DTYPE)]))
    w_out_full = wout_g.reshape(-1, d_model)
    w_glu_full = wglu_g.reshape(-1, d_ssm)
    (dy, d_o, mixt, dyc, dyp, dzs, ygt, dq, loss_part, dgpost, dbglu) = _tail(
        xp, tp, proj, yconv, yp, w_glu_full, row(b_glu), w_out_full, row(norm_post_g), zs_col0, min(256, seq))

    r_out, r_glu, nc = w_out.shape[0], w_glu.shape[0], w_in.shape[1]
    (dwout_p,), _ = _wgrad("dw_out", mixt, d_o, r_out, min(1024, d_model), (N_DEV, r_out, d_model),
                           (None, r_out, min(1024, d_model)), lambda i, j: (i, 0, j))
    (dwglu_p,), _ = _wgrad("dw_glu", ygt, dq, r_glu, d_ssm, (N_DEV, r_glu, d_ssm),
                           (None, r_glu, d_ssm), lambda i, j: (i, 0, 0))
    dbg, dcg, dv, dzc, dconvb, dconvw = _conv_bwd(proj, dyc, conv_w_full, row(conv_b), d_conv)
    (du, dbb, dcc, da, dd), (recv_out, recv_glu) = _ssm_bwd(
        proj, dyp, lam, bbcat, cccat, row(ssm_d), d_ssm, u_col0, _Comm([], [dwout_p, dwglu_p]))
    dproj = jnp.concatenate([dbg, dcg, dv, dzc, du, dzs], axis=1)
    da_n = jnp.transpose(da.reshape(n_half, 2, HALF_G, SSM_STATE), (1, 0, 2, 3)).reshape(2, groups, 1, states)
    parts = [dgpost, dconvb, dd, dbglu, dconvw[:3], da_n, _bb_diag(dbb), _cc_diag(dcc)]
    shapes = [p.shape for p in parts]
    tmw = min(1024, d_model)
    (dwin_p,), (pack_g,) = _wgrad("dw_in", ht, dproj, tmw, nc, (N_DEV, d_model, nc),
                                  (None, tmw, nc), lambda i, j: (j, i, 0), _Comm([_pack(parts)]))
    (gx_p, dgpre), (recv_in,) = _bwd_in(dproj, win_g, xp, dy, row(norm_pre_g), _Comm([], [dwin_p]), min(256, seq))
    (last_g,) = _exchange("reduce_last", [_pack([dgpre])], [])
    (g_gpost, g_convb, g_d, g_bglu, g_convw, g_da, g_dbb, g_dcc) = _unpack(_sum_slots("sum_pack", pack_g), shapes)
    (g_gpre,) = _unpack(_sum_slots("sum_last", last_g), [dgpre.shape])
    g_convw = lax.dynamic_slice(g_convw, (0, me * conv_w.shape[1]), conv_w.shape)

    tr = lambda a: jnp.transpose(a, (0, 2, 1))
    direct = [(g_gpre, row(norm_pre_g), row(m_norm_pre_g), row(v_norm_pre_g)),
              (g_convb, row(conv_b), row(m_conv_b), row(v_conv_b)),
              (g_d, row(ssm_d), row(m_ssm_d), row(v_ssm_d)),
              (g_bglu, row(b_glu), row(m_b_glu), row(v_b_glu)),
              (g_gpost, row(norm_post_g), row(m_norm_post_g), row(v_norm_post_g)),
              (g_convw, conv_w, m_conv_w, v_conv_w),
              (g_dcc[0], ssm_c_re, m_ssm_c_re, v_ssm_c_re),
              (-g_dcc[1], ssm_c_im, m_ssm_c_im, v_ssm_c_im)]
    ssm = dict(da_r=g_da[0], da_i=g_da[1], dbb_r=g_dbb[0], dbb_i=g_dbb[1], lr=g3(ssm_a_re), li=g3(ssm_a_im),
               ldt=g3(ssm_log_dt), bt_r=bt_re, bt_i=bt_im, lbr=lbr, lbi=lbi, qr=qr, qi=qi,
               w_a_re=g3(ssm_a_re), m_a_re=g3(m_ssm_a_re), v_a_re=g3(v_ssm_a_re),
               w_a_im=g3(ssm_a_im), m_a_im=g3(m_ssm_a_im), v_a_im=g3(v_ssm_a_im),
               w_log_dt=g3(ssm_log_dt), m_log_dt=g3(m_ssm_log_dt), v_log_dt=g3(v_ssm_log_dt),
               w_bt_re=bt_re, m_bt_re=tr(m_ssm_b_re), v_bt_re=tr(v_ssm_b_re),
               w_bt_im=bt_im, m_bt_im=tr(m_ssm_b_im), v_bt_im=tr(v_ssm_b_im))
    small = _small_update(direct, ssm)
    res = {}
    for name, quad, shape in zip(["norm_pre_g", "conv_b", "ssm_d", "b_glu", "norm_post_g", "conv_w", "ssm_c_re", "ssm_c_im"],
                                 small[:8], [norm_pre_g.shape, conv_b.shape, ssm_d.shape, b_glu.shape,
                                             norm_post_g.shape, conv_w.shape, ssm_c_re.shape, ssm_c_im.shape]):
        res[name] = tuple(a.reshape(shape) for a in quad)
    res["ssm_a_re"] = tuple(a.reshape(ssm_a_re.shape) for a in small[8])
    res["ssm_a_im"] = tuple(a.reshape(ssm_a_im.shape) for a in small[9])
    res["ssm_log_dt"] = tuple(a.reshape(ssm_log_dt.shape) for a in small[10])
    res["ssm_b_re"] = tuple(tr(a) for a in small[11])
    res["ssm_b_im"] = tuple(tr(a) for a in small[12])
    res["w_in"] = tuple(_adam_big("adam_w_in", recv_in, w_in, m_w_in, v_w_in, min(256, d_model)))
    res["w_out"] = tuple(_adam_big("adam_w_out", recv_out, w_out, m_w_out, v_w_out, min(128, r_out)))
    res["w_glu"] = tuple(_adam_big("adam_w_glu", recv_glu, w_glu, m_w_glu, v_w_glu, r_glu))

    order = ["norm_pre_g", "w_in", "conv_w", "conv_b", "ssm_a_re", "ssm_a_im", "ssm_log_dt", "ssm_b_re", "ssm_b_im",
             "ssm_c_re", "ssm_c_im", "ssm_d", "w_glu", "b_glu", "w_out", "norm_post_g"]
    loss = lax.psum(loss_part[0, 0], AXES)
    grad_x = _unpermute_rows(gx_p)[None]
    return (loss, grad_x, *[res[n][0] for n in order], *[res[n][1] for n in order],
            *[res[n][2] for n in order], *[res[n][3] for n in order])
```

```python
import functools
import math

import jax
import jax.numpy as jnp
from jax import lax
from jax.experimental import pallas as pl
from jax.experimental.pallas import tpu as pltpu

F32 = jnp.float32
MXU_DTYPE = jnp.bfloat16
AXES = ("x", "y", "c")
N_DEV = 8
N_CHUNK = 8
LANES = 128
SSM_GROUP = 16
SSM_STATE = 64
HALF_CH = 64
HALF_G = HALF_CH // SSM_GROUP
HALF_W = HALF_G * SSM_STATE
EPS = 1e-6
ADAM_LR, ADAM_B1, ADAM_B2, ADAM_EPS, ADAM_WD, ADAM_STEP = 0.001, 0.9, 0.999, 1e-08, 0.01, 10
GELU_C = math.sqrt(2.0 / math.pi)
GELU_K = 0.044715
VMEM_LIMIT = 56 * 1024 * 1024


def _params(sem=None):
    return pltpu.CompilerParams(dimension_semantics=sem, vmem_limit_bytes=VMEM_LIMIT)


def _dot(a, b):
    return jnp.dot(a, b, preferred_element_type=F32)


def _dot_nt(a, b):
    return lax.dot_general(a, b, (((1,), (1,)), ((), ())), preferred_element_type=F32)


def _dot_tn(a, b):
    return lax.dot_general(a, b, (((0,), (0,)), ((), ())), preferred_element_type=F32)


def _sigmoid(z):
    return 1.0 / (1.0 + jnp.exp(-z))


def _flip(v, bit):
    return 1 - v if bit else v


def _peers():
    x, y, c = (lax.axis_index(a) for a in AXES)
    out = []
    for m in range(1, N_DEV):
        px, py, pc = _flip(x, (m >> 2) & 1), _flip(y, (m >> 1) & 1), _flip(c, m & 1)
        out.append((px, py, pc, 4 * px + 2 * py + pc))
    return out


class _Comm:
    def __init__(self, gathers=(), scatters=(), dests=None, into=None):
        self.n_g = len(gathers)
        self.operands = list(gathers) + list(scatters)
        self.n = len(self.operands)
        self.dests = dests or {}
        self.into = into or {}

    def out_shape(self):
        return [jax.ShapeDtypeStruct((N_DEV,) + a.shape if t < self.n_g else a.shape, a.dtype)
                for t, a in enumerate(self.operands)]

    def scratch(self):
        if not self.n:
            return []
        return [pltpu.SemaphoreType.DMA((self.n, N_DEV - 1)), pltpu.SemaphoreType.DMA((self.n, N_DEV - 1)),
                pltpu.SemaphoreType.DMA((self.n,))]

    def _copies(self, in_refs, out_refs, sems, arrivals):
        send_sems, recv_sems, local_sems = sems
        x, y, c = (lax.axis_index(a) for a in AXES)
        me = 4 * x + 2 * y + c

        def src(t, dev):
            return in_refs[t] if t < self.n_g else in_refs[t].at[dev]

        def member(t, dev):
            if t not in self.dests:
                return None
            return functools.reduce(jnp.logical_or, [dev == d for d in self.dests[t]])

        local = [(member(t, me), pltpu.make_async_copy(src(t, me), out_refs[t].at[me], local_sems.at[t]))
                 for t in range(self.n)]
        sends, recvs = [], []
        for t in range(self.n):
            for m, (px, py, pc, peer) in enumerate(_peers()):
                kw = dict(send_sem=send_sems.at[t, m], recv_sem=recv_sems.at[t, m],
                          device_id=(px, py, pc), device_id_type=pl.DeviceIdType.MESH)
                sends.append((member(t, peer), pltpu.make_async_remote_copy(
                    src_ref=src(t, peer), dst_ref=out_refs[t].at[me], **kw)))
                if arrivals:
                    recvs.append((member(t, me), pltpu.make_async_remote_copy(
                        src_ref=src(t, peer), dst_ref=out_refs[t].at[peer], **kw)))
        return local, sends, recvs

    @staticmethod
    def _do(cond, action):
        if cond is None:
            action()
        else:
            pl.when(cond)(action)

    def start(self, in_refs, out_refs, sems):
        local, sends, _ = self._copies(in_refs, out_refs, sems, arrivals=False)
        for cond, cp in local + sends:
            self._do(cond, cp.start)

    def finish(self, in_refs, out_refs, sems):
        local, sends, recvs = self._copies(in_refs, out_refs, sems, arrivals=True)
        for cond, cp in recvs:
            self._do(cond, cp.wait_recv)
        for cond, cp in sends:
            self._do(cond, cp.wait_send)
        for cond, cp in local:
            self._do(cond, cp.wait)


def _call(body, comm, *, name, grid, in_specs, out_specs, out_shape, operands, scratch_shapes=()):
    comm = comm or _Comm()
    n_in, n_out, n_scr, cn = len(in_specs), len(out_specs), len(scratch_shapes), comm.n
    landing = sorted(comm.into)
    aliases = {n_in + cn + q: n_out + t for q, t in enumerate(landing)}

    def wrapped(*refs):
        parts, o = [], 0
        for k in (n_in, cn, len(landing), n_out, cn, n_scr):
            parts.append(refs[o:o + k])
            o += k
        h_in, c_in, _, h_out, c_out, h_scr = parts
        sems = refs[o:]
        if cn:
            first = functools.reduce(jnp.logical_and, [pl.program_id(d) == 0 for d in range(len(grid))])

            @pl.when(first)
            def _():
                comm.start(c_in, c_out, sems)

        body(*h_in, *h_out, *h_scr)
        if cn:
            last = functools.reduce(jnp.logical_and, [pl.program_id(d) == grid[d] - 1 for d in range(len(grid))])

            @pl.when(last)
            def _():
                comm.finish(c_in, c_out, sems)

    any_ = pl.BlockSpec(memory_space=pl.ANY)
    res = pl.pallas_call(
        wrapped, name=name, grid=grid, in_specs=list(in_specs) + [any_] * (cn + len(landing)),
        out_specs=list(out_specs) + [any_] * cn,
        out_shape=list(out_shape) + comm.out_shape(), scratch_shapes=list(scratch_shapes) + comm.scratch(),
        input_output_aliases=aliases, compiler_params=_params(("arbitrary",) * len(grid)),
    )(*operands, *comm.operands, *[comm.into[t] for t in landing])
    return list(res[:n_out]), list(res[n_out:])


def _exchange(name, gathers, scatters):
    def body(tok_ref):
        tok_ref[...] = jnp.zeros_like(tok_ref)

    return _call(body, _Comm(gathers, scatters), name=name, grid=(1,), in_specs=[],
                 out_specs=[pl.BlockSpec((8, LANES), lambda i: (0, 0))],
                 out_shape=[jax.ShapeDtypeStruct((8, LANES), F32)], operands=[])[1]


def _ssm_prep(a_re, a_im, log_dt, bt_re, bt_im):
    def body(lr_ref, li_ref, ldt_ref, br_ref, bi_ref, lbr_ref, lbi_ref, qr_ref, qi_ref, bbr_ref, bbi_ref):
        lr, li = lr_ref[...], li_ref[...]
        dt = jnp.exp(ldt_ref[...])
        mag = jnp.exp(lr * dt)
        lbr, lbi = mag * jnp.cos(li * dt), mag * jnp.sin(li * dt)
        nr, ni = lbr - 1.0, lbi
        den = lr * lr + li * li
        qr = (nr * lr + ni * li) / den
        qi = (ni * lr - nr * li) / den
        br, bi = br_ref[...], bi_ref[...]
        lbr_ref[...], lbi_ref[...], qr_ref[...], qi_ref[...] = lbr, lbi, qr, qi
        bbr_ref[...] = qr * br - qi * bi
        bbi_ref[...] = qr * bi + qi * br

    s2 = jax.ShapeDtypeStruct(a_re.shape, F32)
    s3 = jax.ShapeDtypeStruct(bt_re.shape, F32)
    return pl.pallas_call(body, name="ssm_prep", out_shape=[s2, s2, s2, s2, s3, s3],
                          compiler_params=_params())(a_re, a_im, log_dt, bt_re, bt_im)


def _adam(w, g, m, v):
    m2 = ADAM_B1 * m + (1.0 - ADAM_B1) * g
    v2 = ADAM_B2 * v + (1.0 - ADAM_B2) * (g * g)
    m_hat = m2 / (1.0 - ADAM_B1 ** ADAM_STEP)
    v_hat = v2 / (1.0 - ADAM_B2 ** ADAM_STEP)
    delta = -ADAM_LR * (m_hat / (jnp.sqrt(v_hat) + ADAM_EPS) + ADAM_WD * w)
    return delta, m2, v2


def _small_update(direct, ssm):
    n_direct = len(direct)
    flat = [a for quad in direct for a in quad]
    names = ["da_r", "da_i", "dbb_r", "dbb_i", "lr", "li", "ldt", "bt_r", "bt_i", "lbr", "lbi", "qr", "qi"]
    flat += [ssm[k] for k in names]
    chain = ["a_re", "a_im", "log_dt", "bt_re", "bt_im"]
    for k in chain:
        flat += [ssm["w_" + k], ssm["m_" + k], ssm["v_" + k]]
    n_in = len(flat)

    def body(*refs):
        ins, outs = refs[:n_in], refs[n_in:]
        for p in range(n_direct):
            g, w, m, v = (r[...] for r in ins[4 * p:4 * p + 4])
            d, m2, v2 = _adam(w, g, m, v)
            outs[4 * p][...], outs[4 * p + 1][...], outs[4 * p + 2][...], outs[4 * p + 3][...] = g, d, m2, v2
        o = 4 * n_direct
        da_r, da_i, dbb_r, dbb_i, lr, li, ldt, bt_r, bt_i, lbr, lbi, qr, qi = (r[...] for r in ins[o:o + 13])
        dt = jnp.exp(ldt)
        g_br = qr * dbb_r + qi * dbb_i
        g_bi = qr * dbb_i - qi * dbb_r
        dq_r = jnp.sum(bt_r * dbb_r + bt_i * dbb_i, axis=1, keepdims=True)
        dq_i = jnp.sum(bt_r * dbb_i - bt_i * dbb_r, axis=1, keepdims=True)
        den = lr * lr + li * li
        cr, ci = lr / den, li / den
        gl_r = da_r + (cr * dq_r - ci * dq_i)
        gl_i = da_i + (cr * dq_i + ci * dq_r)
        w_r = qr * cr + qi * ci
        w_i = qi * cr - qr * ci
        g_lr = dt * (lbr * gl_r + lbi * gl_i) + (-w_r * dq_r - w_i * dq_i)
        g_li = dt * (lbr * gl_i - lbi * gl_r) + (-w_r * dq_i + w_i * dq_r)
        m_r = lr * lbr - li * lbi
        m_i = lr * lbi + li * lbr
        g_ldt = jnp.sum(m_r * gl_r + m_i * gl_i, axis=2, keepdims=True) * dt
        grads = [g_lr, g_li, g_ldt, g_br, g_bi]
        base_in, base_out = o + 13, 4 * n_direct
        for p, g in enumerate(grads):
            w, m, v = (r[...] for r in ins[base_in + 3 * p:base_in + 3 * p + 3])
            d, m2, v2 = _adam(w, g, m, v)
            q = base_out + 4 * p
            outs[q][...], outs[q + 1][...], outs[q + 2][...], outs[q + 3][...] = g, d, m2, v2

    out_shape = []
    for quad in direct:
        out_shape += [jax.ShapeDtypeStruct(quad[1].shape, F32)] * 4
    for k in chain:
        out_shape += [jax.ShapeDtypeStruct(ssm["w_" + k].shape, F32)] * 4
    res = pl.pallas_call(body, name="small_update", out_shape=out_shape, compiler_params=_params())(*flat)
    return [tuple(res[4 * p:4 * p + 4]) for p in range(n_direct + len(chain))]


def _sum_slots(name, pack):
    def body(p_ref, o_ref):
        acc = p_ref[0].astype(F32)
        for k in range(1, N_DEV):
            acc = acc + p_ref[k].astype(F32)
        o_ref[...] = acc

    return pl.pallas_call(body, name=name, out_shape=jax.ShapeDtypeStruct(pack.shape[1:], F32),
                          compiler_params=_params())(pack)


def _adam_big(name, recv, w, m, v, tr):
    _, rows, cols = recv.shape

    def body(r_ref, w_ref, m_ref, v_ref, g_ref, d_ref, m2_ref, v2_ref):
        g = r_ref[0].astype(F32)
        for k in range(1, N_DEV):
            g = g + r_ref[k].astype(F32)
        d, m2, v2 = _adam(w_ref[...], g, m_ref[...], v_ref[...])
        g_ref[...], d_ref[...], m2_ref[...], v2_ref[...] = g, d, m2, v2

    blk = pl.BlockSpec((tr, cols), lambda i: (i, 0))
    shp = jax.ShapeDtypeStruct((rows, cols), F32)
    return pl.pallas_call(
        body, name=name, grid=(rows // tr,),
        in_specs=[pl.BlockSpec((N_DEV, tr, cols), lambda i: (0, i, 0)), blk, blk, blk],
        out_specs=[blk] * 4, out_shape=[shp] * 4, compiler_params=_params(("parallel",)),
    )(recv, w, m, v)


def _norm_in(xp, g_pre, tm):
    seq, d_model = xp.shape

    def body(x_ref, g_ref, h_ref, ht_ref):
        x = x_ref[...]
        r = lax.rsqrt(jnp.mean(x * x, axis=-1, keepdims=True) + EPS)
        h = x * r * g_ref[...]
        h_ref[...] = h.astype(h_ref.dtype)
        ht_ref[...] = h.T.astype(ht_ref.dtype)

    return pl.pallas_call(
        body, name="norm_in", grid=(seq // tm,),
        in_specs=[pl.BlockSpec((tm, d_model), lambda i: (i, 0)), pl.BlockSpec((1, d_model), lambda i: (0, 0))],
        out_specs=[pl.BlockSpec((tm, d_model), lambda i: (i, 0)), pl.BlockSpec((d_model, tm), lambda i: (0, i))],
        out_shape=[jax.ShapeDtypeStruct((seq, d_model), MXU_DTYPE), jax.ShapeDtypeStruct((d_model, seq), MXU_DTYPE)],
        compiler_params=_params(("parallel",)),
    )(xp, g_pre)


GATHER_ORDER = (0, 1, 4, 2, 6, 5, 3, 7)


def _fwd_in(h, w_shard, order, comm, tm):
    seq, d_model = h.shape
    nc = w_shard.shape[1]
    n_i = seq // tm
    cn = comm.n

    def body(order_ref, h_ref, w_hbm, *rest):
        c_in, rest = rest[:cn], rest[cn:]
        proj_ref, wing = rest[0], rest[1]
        c_out, rest = rest[2:2 + cn], rest[2 + cn:]
        wbuf, send_sems, recv_sems, own_sem, load_sem = rest[:5]
        c_sems = rest[5:]
        k, i = pl.program_id(0), pl.program_id(1)
        x, y, c = (lax.axis_index(a) for a in AXES)
        me = 4 * x + 2 * y + c

        def dev(rel):
            return _flip(x, (rel >> 2) & 1), _flip(y, (rel >> 1) & 1), _flip(c, rel & 1)

        def slot(rel):
            px, py, pc = dev(rel)
            return 4 * px + 2 * py + pc

        def remote(src, block, to_rel, sem):
            return pltpu.make_async_remote_copy(
                src_ref=src, dst_ref=wing.at[block], send_sem=send_sems.at[sem], recv_sem=recv_sems.at[sem],
                device_id=dev(to_rel), device_id_type=pl.DeviceIdType.MESH)

        own = pltpu.make_async_copy(w_hbm, wing.at[me], own_sem)
        first_hand = [remote(w_hbm, me, GATHER_ORDER[p], p - 1) for p in range(1, 5)]
        passed_on = [remote(wing.at[slot(GATHER_ORDER[p])], slot(GATHER_ORDER[p]), 1, p + 2) for p in range(2, 5)]

        @pl.when((k == 0) & (i == 0))
        def _():
            own.start()
            for cp in first_hand:
                cp.start()
            comm.start(c_in, c_out, c_sems)

        @pl.when(i == 0)
        def _():
            for p in range(N_DEV):
                @pl.when(k == p)
                def _():
                    if p == 0:
                        own.wait()
                    else:
                        remote(w_hbm, slot(GATHER_ORDER[p]), GATHER_ORDER[p], p - 1).wait_recv()
                    if 2 <= p <= 4:
                        passed_on[p - 2].start()
            load = pltpu.make_async_copy(wing.at[order_ref[k]], wbuf, load_sem)
            load.start()
            load.wait()

        proj_ref[...] = _dot(h_ref[...], wbuf[...])

        @pl.when((k == N_DEV - 1) & (i == n_i - 1))
        def _():
            for cp in first_hand + passed_on:
                cp.wait_send()
            comm.finish(c_in, c_out, c_sems)

    any_ = pl.BlockSpec(memory_space=pl.ANY)
    grid_spec = pltpu.PrefetchScalarGridSpec(
        num_scalar_prefetch=1, grid=(N_DEV, n_i),
        in_specs=[pl.BlockSpec((tm, d_model), lambda k, i, o: (i, 0)), any_] + [any_] * cn,
        out_specs=[pl.BlockSpec((tm, nc), lambda k, i, o: (i, o[k])), any_] + [any_] * cn,
        scratch_shapes=[pltpu.VMEM((d_model, nc), w_shard.dtype), pltpu.SemaphoreType.DMA((N_DEV - 1,)),
                        pltpu.SemaphoreType.DMA((N_DEV - 1,)), pltpu.SemaphoreType.DMA, pltpu.SemaphoreType.DMA]
        + comm.scratch())
    res = pl.pallas_call(
        body, name="fwd_in", grid_spec=grid_spec,
        out_shape=[jax.ShapeDtypeStruct((seq, N_DEV * nc), F32),
                   jax.ShapeDtypeStruct((N_DEV, d_model, nc), w_shard.dtype)] + comm.out_shape(),
        compiler_params=_params(("arbitrary", "arbitrary")),
    )(order, h, w_shard, *comm.operands)
    return res[0], res[1], list(res[2:])


def _shift_prev(a):
    n = a.shape[0]
    last = a[n - N_CHUNK:, :]
    row = lax.broadcasted_iota(jnp.int32, last.shape, 0)
    wrap = jnp.where(row == 0, 0.0, pltpu.roll(last, 1, axis=0))
    return jnp.concatenate([wrap, a[:n - N_CHUNK, :]], axis=0)


def _shift_next(a):
    first = a[:N_CHUNK, :]
    row = lax.broadcasted_iota(jnp.int32, first.shape, 0)
    wrap = jnp.where(row == N_CHUNK - 1, 0.0, pltpu.roll(first, N_CHUNK - 1, axis=0))
    return jnp.concatenate([a[N_CHUNK:, :], wrap], axis=0)


def _conv_specs(seq, d_conv):
    nblk = d_conv // LANES
    return [pl.BlockSpec((seq, LANES), functools.partial(lambda i, o: (0, o + i), o=q * nblk)) for q in range(4)]


def _conv_fwd(proj, conv_w8, conv_b, d_conv):
    seq = proj.shape[0]

    def body(bg_ref, cg_ref, v_ref, zc_ref, w_ref, b_ref, y_ref):
        cv = cg_ref[...] * v_ref[...]
        s1 = _shift_prev(cv)
        s2 = _shift_prev(s1)
        conv = b_ref[...] + w_ref[0:1, :] * s2 + w_ref[1:2, :] * s1 + w_ref[2:3, :] * cv
        z = zc_ref[...]
        y_ref[...] = bg_ref[...] * conv * (z * _sigmoid(z))

    col = pl.BlockSpec((seq, LANES), lambda i: (0, i))
    return pl.pallas_call(
        body, name="conv_fwd", grid=(d_conv // LANES,),
        in_specs=_conv_specs(seq, d_conv) + [pl.BlockSpec((8, LANES), lambda i: (0, i)), pl.BlockSpec((1, LANES), lambda i: (0, i))],
        out_specs=col, out_shape=jax.ShapeDtypeStruct((seq, d_conv), F32),
        compiler_params=_params(("parallel",)),
    )(proj, proj, proj, proj, conv_w8, conv_b)


def _conv_bwd(proj, dyc, conv_w8, conv_b, d_conv):
    seq = proj.shape[0]

    def body(bg_ref, cg_ref, v_ref, zc_ref, dy_ref, w_ref, b_ref, dbg_ref, dcg_ref, dv_ref, dzc_ref, dcb_ref, dcw_ref):
        bg, cg, v, z = bg_ref[...], cg_ref[...], v_ref[...], zc_ref[...]
        w0, w1, w2 = w_ref[0:1, :], w_ref[1:2, :], w_ref[2:3, :]
        cv = cg * v
        s1 = _shift_prev(cv)
        s2 = _shift_prev(s1)
        conv = b_ref[...] + w0 * s2 + w1 * s1 + w2 * cv
        sig = _sigmoid(z)
        dy = dy_ref[...]
        g1 = dy * (z * sig)
        d_conv_ = g1 * bg
        dbg_ref[...] = (g1 * conv).astype(dbg_ref.dtype)
        dzc_ref[...] = (dy * bg * conv * (sig * (1.0 + z * (1.0 - sig)))).astype(dzc_ref.dtype)
        n1 = _shift_next(d_conv_)
        n2 = _shift_next(n1)
        d_cv = w2 * d_conv_ + w1 * n1 + w0 * n2
        dcg_ref[...] = (d_cv * v).astype(dcg_ref.dtype)
        dv_ref[...] = (d_cv * cg).astype(dv_ref.dtype)
        dcb_ref[...] = jnp.sum(d_conv_, axis=0, keepdims=True)
        rows = [jnp.sum(d_conv_ * s, axis=0, keepdims=True) for s in (s2, s1, cv)]
        dcw_ref[...] = jnp.concatenate(rows + [jnp.zeros((5, LANES), F32)], axis=0)

    col = pl.BlockSpec((seq, LANES), lambda i: (0, i))
    big = jax.ShapeDtypeStruct((seq, d_conv), MXU_DTYPE)
    return pl.pallas_call(
        body, name="conv_bwd", grid=(d_conv // LANES,),
        in_specs=_conv_specs(seq, d_conv) + [col, pl.BlockSpec((8, LANES), lambda i: (0, i)), pl.BlockSpec((1, LANES), lambda i: (0, i))],
        out_specs=[col] * 4 + [pl.BlockSpec((1, LANES), lambda i: (0, i)), pl.BlockSpec((8, LANES), lambda i: (0, i))],
        out_shape=[big] * 4 + [jax.ShapeDtypeStruct((1, d_conv), F32), jax.ShapeDtypeStruct((8, d_conv), F32)],
        compiler_params=_params(("parallel",)),
    )(proj, proj, proj, proj, dyc, conv_w8, conv_b)


def _cmul(ar, ai, br, bi):
    return ar * br - ai * bi, ar * bi + ai * br


def _cpow(ar, ai, n):
    rr, ri = jnp.ones_like(ar), jnp.zeros_like(ai)
    while n:
        if n & 1:
            rr, ri = _cmul(rr, ri, ar, ai)
        n >>= 1
        if n:
            ar, ai = _cmul(ar, ai, ar, ai)
    return rr, ri


def _down(v, k):
    row = lax.broadcasted_iota(jnp.int32, v.shape, 0)
    return jnp.where(row >= k, pltpu.roll(v, k, axis=0), 0.0)


def _up(v, k):
    row = lax.broadcasted_iota(jnp.int32, v.shape, 0)
    return jnp.where(row < N_CHUNK - k, pltpu.roll(v, N_CHUNK - k, axis=0), 0.0)


def _chunk_carry(fr, fi, mr, mi, shift):
    vr, vi = shift(fr, 1), shift(fi, 1)
    for k in (1, 2, 4):
        pr, pi = _cmul(mr, mi, shift(vr, k), shift(vi, k))
        vr, vi = vr + pr, vi + pi
        mr, mi = _cmul(mr, mi, mr, mi)
    return vr, vi


def _tile(ref, j, width, part):
    return ref.at[pl.ds(pl.multiple_of(j * N_CHUNK, N_CHUNK), N_CHUNK), pl.ds(part * width, width)]


def _row(t, k):
    return jnp.broadcast_to(t[k:k + 1, :], t.shape)


def _power_table(tab_ref, ar, ai, steps, width):
    e = lax.broadcasted_iota(jnp.int32, ar.shape, 0) + 1
    rr, ri = jnp.ones_like(ar), jnp.zeros_like(ai)
    br, bi = ar, ai
    for bit in range(4):
        mr, mi = _cmul(rr, ri, br, bi)
        take = ((e >> bit) & 1) == 1
        rr, ri = jnp.where(take, mr, rr), jnp.where(take, mi, ri)
        if bit < 3:
            br, bi = _cmul(br, bi, br, bi)
    _tile(tab_ref, 0, width, 0)[...] = rr
    _tile(tab_ref, 0, width, 1)[...] = ri

    def step(m, carry):
        tr, ti = _cmul(carry[0], carry[1], br, bi)
        _tile(tab_ref, m, width, 0)[...] = tr
        _tile(tab_ref, m, width, 1)[...] = ti
        return tr, ti

    lax.fori_loop(1, steps // N_CHUNK, step, (rr, ri))


def _last_power(tab_ref, steps, width):
    shape = (N_CHUNK, width)
    return (jnp.broadcast_to(tab_ref[steps - 1:steps, 0:width], shape),
            jnp.broadcast_to(tab_ref[steps - 1:steps, width:2 * width], shape))


def _scan_fwd(s_ref, ar, ai, steps, width):
    def step(j, carry):
        sr, si = carry
        nr = ar * sr - ai * si + _tile(s_ref, j, width, 0)[...]
        ni = ar * si + ai * sr + _tile(s_ref, j, width, 1)[...]
        _tile(s_ref, j, width, 0)[...] = nr
        _tile(s_ref, j, width, 1)[...] = ni
        return nr, ni

    z = jnp.zeros((N_CHUNK, width), F32)
    return lax.fori_loop(0, steps, step, (z, z), unroll=4)


def _scan_both(s_ref, g_ref, ar, ai, steps, width):
    def step(q, carry):
        sr, si, gr, gi = carry
        j, jb = q, steps - 1 - q
        nsr = ar * sr - ai * si + _tile(s_ref, j, width, 0)[...]
        nsi = ar * si + ai * sr + _tile(s_ref, j, width, 1)[...]
        ngr = ar * gr + ai * gi + _tile(g_ref, jb, width, 0)[...]
        ngi = ar * gi - ai * gr + _tile(g_ref, jb, width, 1)[...]
        _tile(s_ref, j, width, 0)[...] = nsr
        _tile(s_ref, j, width, 1)[...] = nsi
        _tile(g_ref, jb, width, 0)[...] = ngr
        _tile(g_ref, jb, width, 1)[...] = ngi
        return nsr, nsi, ngr, ngi

    z = jnp.zeros((N_CHUNK, width), F32)
    return lax.fori_loop(0, steps, step, (z, z, z, z), unroll=2)


def _patch_fwd(s_ref, tab_ref, cr, ci, steps, width):
    def tile(m, _):
        tr, ti = _tile(tab_ref, m, width, 0)[...], _tile(tab_ref, m, width, 1)[...]
        for k in range(N_CHUNK):
            fr, fi = _cmul(_row(tr, k), _row(ti, k), cr, ci)
            j = m * N_CHUNK + k
            _tile(s_ref, j, width, 0)[...] += fr
            _tile(s_ref, j, width, 1)[...] += fi
        return 0

    lax.fori_loop(0, steps // N_CHUNK, tile, 0)


def _lam_rows(lam_ref, hh, width):
    return (jnp.broadcast_to(lam_ref[hh, 0:1, :], (N_CHUNK, width)),
            jnp.broadcast_to(lam_ref[hh, 1:2, :], (N_CHUNK, width)))


def _ssm_specs(seq, col0):
    return dict(
        col=pl.BlockSpec((seq, LANES), lambda i: (0, col0 + i)),
        lam=pl.BlockSpec((2, 2, HALF_W), lambda i: (i, 0, 0)),
        bb=pl.BlockSpec((2, HALF_CH, 2 * HALF_W), lambda i: (i, 0, 0)),
        cc=pl.BlockSpec((2, 2 * HALF_W, HALF_CH), lambda i: (i, 0, 0)),
        vec=pl.BlockSpec((1, LANES), lambda i: (0, i)),
        out=pl.BlockSpec((seq, LANES), lambda i: (0, i)),
    )


def _ssm_fwd(proj, lam, bbcat, cccat, d_skip, d_ssm, u_col0, comm=None):
    seq = proj.shape[0]
    steps = seq // N_CHUNK

    def body(u_ref, lam_ref, bb_ref, cc_ref, d_ref, yp_ref, s_ref, tab_ref):
        for hh in range(2):
            lanes = slice(HALF_CH * hh, HALF_CH * (hh + 1))
            u_half = u_ref[:, lanes]
            ar, ai = _lam_rows(lam_ref, hh, HALF_W)
            _power_table(tab_ref, ar, ai, steps, HALF_W)
            s_ref[...] = _dot(u_half.astype(MXU_DTYPE), bb_ref[hh])
            fr, fi = _scan_fwd(s_ref, ar, ai, steps, HALF_W)
            pr, pi = _last_power(tab_ref, steps, HALF_W)
            cr, ci = _chunk_carry(fr, fi, pr, pi, _down)
            _patch_fwd(s_ref, tab_ref, cr, ci, steps, HALF_W)
            y = _dot(s_ref[...].astype(MXU_DTYPE), cc_ref[hh])
            yp_ref[:, lanes] = y + d_ref[:, lanes] * u_half

    sp = _ssm_specs(seq, u_col0 // LANES)
    return _call(
        body, comm, name="ssm_fwd", grid=(d_ssm // LANES,),
        in_specs=[sp["col"], sp["lam"], sp["bb"], sp["cc"], sp["vec"]], out_specs=[sp["out"]],
        out_shape=[jax.ShapeDtypeStruct((seq, d_ssm), F32)],
        scratch_shapes=[pltpu.VMEM((seq, 2 * HALF_W), F32), pltpu.VMEM((steps, 2 * HALF_W), F32)],
        operands=[proj, lam, bbcat, cccat, d_skip])


def _ssm_bwd(proj, dyp, lam, bbcat, cccat, d_skip, d_ssm, u_col0, comm=None):
    seq = proj.shape[0]
    steps = seq // N_CHUNK
    n_half = 2 * d_ssm // LANES
    width = HALF_W

    def body(u_ref, dyp_ref, lam_ref, bb_ref, cc_ref, d_ref, du_ref, dbb_ref, dcc_ref, da_ref, dd_ref,
             s_ref, g_ref, tab_ref):
        n_tiles = steps // N_CHUNK
        for hh in range(2):
            lanes = slice(HALF_CH * hh, HALF_CH * (hh + 1))
            u_half, dy_half = u_ref[:, lanes], dyp_ref[:, lanes]
            dy_mx = dy_half.astype(MXU_DTYPE)
            ar, ai = _lam_rows(lam_ref, hh, width)
            _power_table(tab_ref, ar, ai, steps, width)
            s_ref[...] = _dot(u_half.astype(MXU_DTYPE), bb_ref[hh])
            g_ref[...] = _dot_nt(dy_mx, cc_ref[hh])
            fr, fi, lr_, li_ = _scan_both(s_ref, g_ref, ar, ai, steps, width)
            pr, pi = _last_power(tab_ref, steps, width)
            cr, ci = _chunk_carry(fr, fi, pr, pi, _down)
            gr, gi = _chunk_carry(lr_, li_, pr, -pi, _up)

            def tile(m, carry):
                sr, si, accr, acci = carry
                t1r, t1i = _tile(tab_ref, m, width, 0)[...], _tile(tab_ref, m, width, 1)[...]
                mb = n_tiles - 1 - m
                t2r, t2i = _tile(tab_ref, mb, width, 0)[...], _tile(tab_ref, mb, width, 1)[...]
                for k in range(N_CHUNK):
                    j = m * N_CHUNK + k
                    xr, xi = _cmul(_row(t1r, k), _row(t1i, k), cr, ci)
                    nsr = _tile(s_ref, j, width, 0)[...] + xr
                    nsi = _tile(s_ref, j, width, 1)[...] + xi
                    _tile(s_ref, j, width, 0)[...] = nsr
                    _tile(s_ref, j, width, 1)[...] = nsi
                    qr, qi = _row(t2r, N_CHUNK - 1 - k), _row(t2i, N_CHUNK - 1 - k)
                    ngr = _tile(g_ref, j, width, 0)[...] + (qr * gr + qi * gi)
                    ngi = _tile(g_ref, j, width, 1)[...] + (qr * gi - qi * gr)
                    _tile(g_ref, j, width, 0)[...] = ngr
                    _tile(g_ref, j, width, 1)[...] = ngi
                    accr = accr + (sr * ngr + si * ngi)
                    acci = acci + (sr * ngi - si * ngr)
                    sr, si = nsr, nsi
                return sr, si, accr, acci

            z = jnp.zeros((N_CHUNK, width), F32)
            _, _, accr, acci = lax.fori_loop(0, n_tiles, tile, (cr, ci, z, z))
            da_ref[hh, :, 0:width] = jnp.sum(accr, axis=0, keepdims=True)
            da_ref[hh, :, width:2 * width] = jnp.sum(acci, axis=0, keepdims=True)

            g_mx = g_ref[...].astype(MXU_DTYPE)
            dcc_ref[hh] = _dot_tn(dy_mx, s_ref[...].astype(MXU_DTYPE)).T
            dbb_ref[hh] = _dot_tn(u_half.astype(MXU_DTYPE), g_mx)
            du = _dot_nt(g_mx, bb_ref[hh]) + d_ref[:, lanes] * dy_half
            du_ref[:, lanes] = du.astype(du_ref.dtype)
            dd_ref[:, lanes] = jnp.sum(dy_half * u_half, axis=0, keepdims=True)

    sp = _ssm_specs(seq, u_col0 // LANES)
    return _call(
        body, comm, name="ssm_bwd", grid=(d_ssm // LANES,),
        in_specs=[sp["col"], sp["out"], sp["lam"], sp["bb"], sp["cc"], sp["vec"]],
        out_specs=[sp["out"], sp["bb"], sp["cc"], pl.BlockSpec((2, 1, 2 * width), lambda i: (i, 0, 0)), sp["vec"]],
        out_shape=[jax.ShapeDtypeStruct((seq, d_ssm), MXU_DTYPE),
                   jax.ShapeDtypeStruct((n_half, HALF_CH, 2 * width), F32),
                   jax.ShapeDtypeStruct((n_half, 2 * width, HALF_CH), F32),
                   jax.ShapeDtypeStruct((n_half, 1, 2 * width), F32),
                   jax.ShapeDtypeStruct((1, d_ssm), F32)],
        scratch_shapes=[pltpu.VMEM((seq, 2 * width), F32), pltpu.VMEM((seq, 2 * width), F32),
                        pltpu.VMEM((steps, 2 * width), F32)],
        operands=[proj, dyp, lam, bbcat, cccat, d_skip])


def _tail(xp, tp, proj, yconv, yp, w_glu, b_glu, w_out, g_post, zs_col0, tm):
    seq, d_model = xp.shape
    d_conv, d_ssm = yconv.shape[1], yp.shape[1]
    d_mix = d_conv + d_ssm
    assert zs_col0 % d_ssm == 0

    def body(x_ref, t_ref, zs_ref, yc_ref, yp_ref, wglu_hbm, bglu_ref, wout_hbm, gpost_ref,
             dy_ref, do_ref, mixt_ref, dyc_ref, dyp_ref, dzs_ref, ygt_ref, dq_ref, loss_ref, dgpost_ref, dbglu_ref,
             wglu, wout):
        @pl.when(pl.program_id(0) == 0)
        def _():
            pltpu.sync_copy(wglu_hbm, wglu)
            pltpu.sync_copy(wout_hbm, wout)
            loss_ref[...] = jnp.zeros_like(loss_ref)
            dgpost_ref[...] = jnp.zeros_like(dgpost_ref)
            dbglu_ref[...] = jnp.zeros_like(dbglu_ref)

        a = yp_ref[...]
        th = jnp.tanh(GELU_C * (a + GELU_K * (a * a * a)))
        yg = a * (0.5 * (1.0 + th))
        dgelu = 0.5 * (1.0 + th) + 0.5 * a * (1.0 - th * th) * (GELU_C * (1.0 + 3.0 * GELU_K * a * a))
        yg_mx = yg.astype(MXU_DTYPE)
        sq = _sigmoid(_dot(yg_mx, wglu[...]) + bglu_ref[...])
        y2 = yg * sq
        zs = zs_ref[...]
        sz = _sigmoid(zs)
        silz = zs * sz
        yc, ys = yc_ref[...], y2 * silz
        mix = jnp.concatenate([yc, ys], axis=1).astype(MXU_DTYPE)
        mixt_ref[0:d_conv, :] = yc.T.astype(MXU_DTYPE)
        mixt_ref[d_conv:, :] = ys.T.astype(MXU_DTYPE)
        o = _dot(mix, wout[...])
        r2 = lax.rsqrt(jnp.mean(o * o, axis=-1, keepdims=True) + EPS)
        on = o * r2
        gpost = gpost_ref[...]
        err = (x_ref[...] + on * gpost) - t_ref[...]
        loss_ref[...] += 0.5 * jnp.sum(jnp.mean(err * err, axis=-1, keepdims=True), axis=0, keepdims=True)
        dy = err * (1.0 / d_model)
        dy_ref[...] = dy
        dgpost_ref[...] += jnp.sum(dy * on, axis=0, keepdims=True)
        d_on = dy * gpost
        d_o = r2 * (d_on - on * jnp.mean(d_on * on, axis=-1, keepdims=True))
        do_mx = d_o.astype(MXU_DTYPE)
        do_ref[...] = do_mx
        d_mix_ = _dot_nt(do_mx, wout[...])
        dyc_ref[...] = d_mix_[:, :d_conv]
        d_yssm = d_mix_[:, d_conv:]
        d_y2 = d_yssm * silz
        dzs_ref[...] = (d_yssm * y2 * (sz * (1.0 + zs * (1.0 - sz)))).astype(dzs_ref.dtype)
        d_q = d_y2 * yg * (sq * (1.0 - sq))
        dq_mx = d_q.astype(MXU_DTYPE)
        dq_ref[...] = dq_mx
        ygt_ref[...] = yg.T.astype(MXU_DTYPE)
        dbglu_ref[...] += jnp.sum(d_q, axis=0, keepdims=True)
        d_yg = d_y2 * sq + _dot_nt(dq_mx, wglu[...])
        dyp_ref[...] = d_yg * dgelu

    def rows(width, col=0):
        return pl.BlockSpec((tm, width), lambda i: (i, col))

    def fixed(width):
        return pl.BlockSpec((1, width), lambda i: (0, 0))

    def cols(height):
        return pl.BlockSpec((height, tm), lambda i: (0, i))

    any_ = pl.BlockSpec(memory_space=pl.ANY)
    return pl.pallas_call(
        body, name="tail", grid=(seq // tm,),
        in_specs=[rows(d_model), rows(d_model), rows(d_ssm, zs_col0 // d_ssm), rows(d_conv), rows(d_ssm),
                  any_, fixed(d_ssm), any_, fixed(d_model)],
        out_specs=[rows(d_model), rows(d_model), cols(d_mix), rows(d_conv), rows(d_ssm), rows(d_ssm), cols(d_ssm),
                   rows(d_ssm), fixed(LANES), fixed(d_model), fixed(d_ssm)],
        out_shape=[jax.ShapeDtypeStruct((seq, d_model), F32), jax.ShapeDtypeStruct((seq, d_model), MXU_DTYPE),
                   jax.ShapeDtypeStruct((d_mix, seq), MXU_DTYPE), jax.ShapeDtypeStruct((seq, d_conv), F32),
                   jax.ShapeDtypeStruct((seq, d_ssm), F32), jax.ShapeDtypeStruct((seq, d_ssm), MXU_DTYPE),
                   jax.ShapeDtypeStruct((d_ssm, seq), MXU_DTYPE), jax.ShapeDtypeStruct((seq, d_ssm), MXU_DTYPE),
                   jax.ShapeDtypeStruct((1, LANES), F32), jax.ShapeDtypeStruct((1, d_model), F32),
                   jax.ShapeDtypeStruct((1, d_ssm), F32)],
        scratch_shapes=[pltpu.VMEM(w_glu.shape, MXU_DTYPE), pltpu.VMEM(w_out.shape, MXU_DTYPE)],
        compiler_params=_params(("arbitrary",)),
    )(xp, tp, proj, yconv, yp, w_glu, b_glu, w_out, g_post)


def _bwd_in(dp_early, dp_late, late0, win_g, xp, dy, g_pre, comm, tm):
    seq, d_model = xp.shape
    nb, _, nc = win_g.shape
    n_late = dp_late.shape[1] // nc

    def body(de_ref, dl_ref, w_hbm, x_ref, dy_ref, g_ref, gx_ref, dg_ref, acc, w_all):
        i, k = pl.program_id(0), pl.program_id(1)

        @pl.when(k == 0)
        def _():
            acc[...] = jnp.zeros_like(acc)

        @pl.when((i == 0) & (k == 0))
        def _():
            dg_ref[...] = jnp.zeros_like(dg_ref)
            pltpu.sync_copy(w_hbm, w_all)

        is_late = (k >= late0) & (k < late0 + n_late)

        @pl.when(is_late)
        def _():
            acc[...] += _dot_nt(dl_ref[...], w_all[k])

        @pl.when(jnp.logical_not(is_late))
        def _():
            acc[...] += _dot_nt(de_ref[...], w_all[k])

        @pl.when(k == nb - 1)
        def _():
            x = x_ref[...]
            r = lax.rsqrt(jnp.mean(x * x, axis=-1, keepdims=True) + EPS)
            xn = x * r
            dh = acc[...]
            dg_ref[...] += jnp.sum(dh * xn, axis=0, keepdims=True)
            dxn = dh * g_ref[...]
            gx_ref[...] = r * (dxn - xn * jnp.mean(dxn * xn, axis=-1, keepdims=True)) + dy_ref[...]

    row = pl.BlockSpec((tm, d_model), lambda i, k: (i, 0))
    vec = pl.BlockSpec((1, d_model), lambda i, k: (0, 0))
    def early_block(i, k):
        return i, jnp.where(k < late0, k, jnp.where(k < late0 + n_late, late0 - 1, k - n_late))

    def late_block(i, k):
        return i, jnp.clip(k - late0, 0, n_late - 1)

    return _call(
        body, comm, name="bwd_in", grid=(seq // tm, nb),
        in_specs=[pl.BlockSpec((tm, nc), early_block), pl.BlockSpec((tm, nc), late_block),
                  pl.BlockSpec(memory_space=pl.ANY), row, row, vec],
        out_specs=[row, vec],
        out_shape=[jax.ShapeDtypeStruct((seq, d_model), F32), jax.ShapeDtypeStruct((1, d_model), F32)],
        scratch_shapes=[pltpu.VMEM((tm, d_model), F32), pltpu.VMEM(win_g.shape, win_g.dtype)],
        operands=[dp_early, dp_late, win_g, xp, dy, g_pre])


def _wgrad(name, at, b, tm, tn, out_shape, out_block, out_index, comm=None):
    m, seq = at.shape
    n = b.shape[1]

    def body(a_ref, b_ref, o_ref):
        o_ref[...] = _dot(a_ref[...], b_ref[...]).astype(o_ref.dtype)

    return _call(
        body, comm, name=name, grid=(n // tn, m // tm),
        in_specs=[pl.BlockSpec((tm, seq), lambda j, i: (i, 0)), pl.BlockSpec((seq, tn), lambda j, i: (0, j))],
        out_specs=[pl.BlockSpec(out_block, lambda j, i: out_index(i, j))],
        out_shape=[jax.ShapeDtypeStruct(out_shape, MXU_DTYPE)],
        operands=[at, b])


def _eye_g():
    return jnp.eye(HALF_G, dtype=F32)


def _bb_blockdiag(bbt_r, bbt_i):
    n_half = bbt_r.shape[0] // HALF_G

    def one(t):
        t = t.reshape(n_half, HALF_G, SSM_GROUP, SSM_STATE)
        t = t[:, :, :, None, :] * _eye_g()[None, :, None, :, None]
        return t.reshape(n_half, HALF_CH, HALF_W)

    return jnp.concatenate([one(bbt_r), one(bbt_i)], axis=-1)


def _cc_blockdiag(c_re, c_im):
    n_half = c_re.shape[0] // HALF_G

    def one(t):
        t = t.reshape(n_half, HALF_G, SSM_GROUP, SSM_STATE)
        t = jnp.transpose(t, (0, 3, 1, 2))
        t = t[:, None, :, :, :] * _eye_g()[None, :, None, :, None]
        return t.reshape(n_half, HALF_W, HALF_CH)

    return jnp.concatenate([one(c_re), one(-c_im)], axis=1)


def _bb_diag(dbb):
    n_half = dbb.shape[0]
    t = dbb.reshape(n_half, HALF_G, SSM_GROUP, 2, HALF_G, SSM_STATE)
    t = jnp.sum(t * _eye_g()[None, :, None, None, :, None], axis=4)
    t = jnp.transpose(t, (3, 0, 1, 2, 4))
    return t.reshape(2, n_half * HALF_G, SSM_GROUP, SSM_STATE)


def _cc_diag(dcc):
    n_half = dcc.shape[0]
    t = dcc.reshape(n_half, 2, HALF_G, SSM_STATE, HALF_G, SSM_GROUP)
    t = jnp.sum(t * _eye_g()[None, None, :, None, :, None], axis=2)
    t = jnp.transpose(t, (1, 0, 3, 4, 2))
    return t.reshape(2, n_half * HALF_G, SSM_GROUP, SSM_STATE)


def _permute_rows(a):
    seq, d = a.shape
    return a.reshape(N_CHUNK, seq // N_CHUNK, d).transpose(1, 0, 2).reshape(seq, d)


def _unpermute_rows(a):
    seq, d = a.shape
    return a.reshape(seq // N_CHUNK, N_CHUNK, d).transpose(1, 0, 2).reshape(seq, d)


def _pack_rows(shape):
    return -(-math.prod(shape) // (8 * LANES)) * 8


def _pack(parts, dtype=F32):
    rows = []
    for p in parts:
        flat = p.reshape(-1).astype(dtype)
        rows.append(jnp.pad(flat, (0, _pack_rows(p.shape) * LANES - flat.shape[0])).reshape(-1, LANES))
    return jnp.concatenate(rows, axis=0)


def _unpack(packed, shapes):
    out, o = [], 0
    for s in shapes:
        n = _pack_rows(s)
        out.append(packed[o:o + n].reshape(-1)[:math.prod(s)].reshape(s))
        o += n
    return out


def kernel(x, norm_pre_g, w_in, conv_w, conv_b, ssm_a_re, ssm_a_im, ssm_log_dt, ssm_b_re, ssm_b_im, ssm_c_re, ssm_c_im, ssm_d, w_glu, b_glu, w_out, norm_post_g, loss_target, m_norm_pre_g, m_w_in, m_conv_w, m_conv_b, m_ssm_a_re, m_ssm_a_im, m_ssm_log_dt, m_ssm_b_re, m_ssm_b_im, m_ssm_c_re, m_ssm_c_im, m_ssm_d, m_w_glu, m_b_glu, m_w_out, m_norm_post_g, v_norm_pre_g, v_w_in, v_conv_w, v_conv_b, v_ssm_a_re, v_ssm_a_im, v_ssm_log_dt, v_ssm_b_re, v_ssm_b_im, v_ssm_c_re, v_ssm_c_im, v_ssm_d, v_w_glu, v_b_glu, v_w_out, v_norm_post_g):
    seq, d_model = x.shape[1], x.shape[2]
    d_conv, d_ssm = conv_b.shape[0], ssm_d.shape[0]
    groups, states = ssm_a_re.shape
    assert x.shape[0] == 1 and seq % (8 * N_CHUNK) == 0 and d_conv == d_ssm
    assert (groups, states) == (d_ssm // SSM_GROUP, SSM_STATE) and d_ssm % LANES == 0
    me = 4 * lax.axis_index("x") + 2 * lax.axis_index("y") + lax.axis_index("c")
    tm = min(512, seq)

    xp = _permute_rows(x[0])
    tp = _permute_rows(loss_target[0])
    row = lambda a: a.reshape(1, -1)
    conv_w8 = jnp.pad(conv_w, ((0, 8 - conv_w.shape[0]), (0, 0)))

    g3 = lambda a: a.reshape(groups, 1, -1)
    bt_re, bt_im = jnp.transpose(ssm_b_re, (0, 2, 1)), jnp.transpose(ssm_b_im, (0, 2, 1))
    lbr, lbi, qr, qi, bbt_r, bbt_i = _ssm_prep(g3(ssm_a_re), g3(ssm_a_im), g3(ssm_log_dt), bt_re, bt_im)
    n_half = groups // HALF_G
    lam = jnp.stack([lbr.reshape(n_half, HALF_W), lbi.reshape(n_half, HALF_W)], axis=1)
    bbcat = _bb_blockdiag(bbt_r, bbt_i).astype(MXU_DTYPE)
    cccat = _cc_blockdiag(ssm_c_re, ssm_c_im).astype(MXU_DTYPE)

    h, ht = _norm_in(xp, row(norm_pre_g), tm)
    order = jnp.stack([jnp.bitwise_xor(me, r) for r in GATHER_ORDER]).astype(jnp.int32)
    proj, win_g, (convw_g,) = _fwd_in(h, w_in.astype(MXU_DTYPE), order, _Comm([conv_w8]), tm)
    conv_w_full = jnp.transpose(convw_g, (1, 0, 2)).reshape(8, d_conv)
    u_col0, zs_col0 = 4 * d_conv, 4 * d_conv + d_ssm
    yconv = _conv_fwd(proj, conv_w_full, row(conv_b), d_conv)
    (yp,), (wout_g, wglu_g) = _ssm_fwd(proj, lam, bbcat, cccat, row(ssm_d), d_ssm, u_col0,
                                       _Comm([w_out.astype(MXU_DTYPE), w_glu.astype(MXU_DTYPE)]))
    w_out_full = wout_g.reshape(-1, d_model)
    w_glu_full = wglu_g.reshape(-1, d_ssm)
    (dy, d_o, mixt, dyc, dyp, dzs, ygt, dq, loss_part, dgpost, dbglu) = _tail(
        xp, tp, proj, yconv, yp, w_glu_full, row(b_glu), w_out_full, row(norm_post_g), zs_col0, min(256, seq))

    r_out, r_glu, nc = w_out.shape[0], w_glu.shape[0], w_in.shape[1]
    (dwout_p,), _ = _wgrad("dw_out", mixt, d_o, r_out, min(1024, d_model), (N_DEV, r_out, d_model),
                           (None, r_out, min(1024, d_model)), lambda i, j: (i, 0, j))
    (dwglu_p,), _ = _wgrad("dw_glu", ygt, dq, r_glu, d_ssm, (N_DEV, r_glu, d_ssm),
                           (None, r_glu, d_ssm), lambda i, j: (i, 0, 0))
    dbg, dcg, dv, dzc, dconvb, dconvw = _conv_bwd(proj, dyc, conv_w_full, row(conv_b), d_conv)
    late = [k for k in range(N_DEV) if k * nc < u_col0 + d_ssm and (k + 1) * nc > u_col0]
    early = [k for k in range(N_DEV) if k not in late]
    late0, cut_a, cut_b = late[0], late[0] * nc - 3 * d_conv, (late[-1] + 1) * nc - zs_col0
    assert late == list(range(late0, late0 + len(late))) and 0 <= cut_a <= d_conv and 0 <= cut_b <= d_ssm
    dp_early = jnp.concatenate([dbg, dcg, dv, dzc[:, :cut_a], dzs[:, cut_b:]], axis=1)
    tmw = min(1024, d_model)
    (dwin_e,), (recv_out, recv_glu) = _wgrad(
        "dw_in_early", ht, dp_early, tmw, nc, (N_DEV, d_model, nc), (None, tmw, nc),
        lambda i, j: (jnp.where(j < late0, j, j + len(late)), i, 0), _Comm([], [dwout_p, dwglu_p]))
    (du, dbb, dcc, da, dd), (recv_in,) = _ssm_bwd(
        proj, dyp, lam, bbcat, cccat, row(ssm_d), d_ssm, u_col0, _Comm([], [dwin_e], dests={0: early}))
    dp_late = jnp.concatenate([dzc[:, cut_a:], du, dzs[:, :cut_b]], axis=1)
    (dwin_l,), _ = _wgrad("dw_in_late", ht, dp_late, tmw, nc, (N_DEV, d_model, nc), (None, tmw, nc),
                          lambda i, j: (late0 + j, i, 0))
    da_n = jnp.transpose(da.reshape(n_half, 2, HALF_G, SSM_STATE), (1, 0, 2, 3)).reshape(2, groups, 1, states)
    parts = [dgpost, dconvb, dd, dbglu, dconvw[:3], da_n]
    parts_mx = [_bb_diag(dbb), _cc_diag(dcc)]
    shapes, shapes_mx = [p.shape for p in parts], [p.shape for p in parts_mx]
    (gx_p, dgpre), (pack_g, pack_mx_g, recv_in) = _bwd_in(
        dp_early, dp_late, late0, win_g, xp, dy, row(norm_pre_g),
        _Comm([_pack(parts), _pack(parts_mx, MXU_DTYPE)], [dwin_l], dests={2: late}, into={2: recv_in}),
        min(256, seq))
    (last_g,) = _exchange("reduce_last", [_pack([dgpre])], [])
    (g_gpost, g_convb, g_d, g_bglu, g_convw, g_da) = _unpack(_sum_slots("sum_pack", pack_g), shapes)
    (g_dbb, g_dcc) = _unpack(_sum_slots("sum_pack_mx", pack_mx_g), shapes_mx)
    (g_gpre,) = _unpack(_sum_slots("sum_last", last_g), [dgpre.shape])
    g_convw = lax.dynamic_slice(g_convw, (0, me * conv_w.shape[1]), conv_w.shape)

    tr = lambda a: jnp.transpose(a, (0, 2, 1))
    direct = [(g_gpre, row(norm_pre_g), row(m_norm_pre_g), row(v_norm_pre_g)),
              (g_convb, row(conv_b), row(m_conv_b), row(v_conv_b)),
              (g_d, row(ssm_d), row(m_ssm_d), row(v_ssm_d)),
              (g_bglu, row(b_glu), row(m_b_glu), row(v_b_glu)),
              (g_gpost, row(norm_post_g), row(m_norm_post_g), row(v_norm_post_g)),
              (g_convw, conv_w, m_conv_w, v_conv_w),
              (g_dcc[0], ssm_c_re, m_ssm_c_re, v_ssm_c_re),
              (-g_dcc[1], ssm_c_im, m_ssm_c_im, v_ssm_c_im)]
    ssm = dict(da_r=g_da[0], da_i=g_da[1], dbb_r=g_dbb[0], dbb_i=g_dbb[1], lr=g3(ssm_a_re), li=g3(ssm_a_im),
               ldt=g3(ssm_log_dt), bt_r=bt_re, bt_i=bt_im, lbr=lbr, lbi=lbi, qr=qr, qi=qi,
               w_a_re=g3(ssm_a_re), m_a_re=g3(m_ssm_a_re), v_a_re=g3(v_ssm_a_re),
               w_a_im=g3(ssm_a_im), m_a_im=g3(m_ssm_a_im), v_a_im=g3(v_ssm_a_im),
               w_log_dt=g3(ssm_log_dt), m_log_dt=g3(m_ssm_log_dt), v_log_dt=g3(v_ssm_log_dt),
               w_bt_re=bt_re, m_bt_re=tr(m_ssm_b_re), v_bt_re=tr(v_ssm_b_re),
               w_bt_im=bt_im, m_bt_im=tr(m_ssm_b_im), v_bt_im=tr(v_ssm_b_im))
    small = _small_update(direct, ssm)
    res = {}
    for name, quad, shape in zip(["norm_pre_g", "conv_b", "ssm_d", "b_glu", "norm_post_g", "conv_w", "ssm_c_re", "ssm_c_im"],
                                 small[:8], [norm_pre_g.shape, conv_b.shape, ssm_d.shape, b_glu.shape,
                                             norm_post_g.shape, conv_w.shape, ssm_c_re.shape, ssm_c_im.shape]):
        res[name] = tuple(a.reshape(shape) for a in quad)
    res["ssm_a_re"] = tuple(a.reshape(ssm_a_re.shape) for a in small[8])
    res["ssm_a_im"] = tuple(a.reshape(ssm_a_im.shape) for a in small[9])
    res["ssm_log_dt"] = tuple(a.reshape(ssm_log_dt.shape) for a in small[10])
    res["ssm_b_re"] = tuple(tr(a) for a in small[11])
    res["ssm_b_im"] = tuple(tr(a) for a in small[12])
    res["w_in"] = tuple(_adam_big("adam_w_in", recv_in, w_in, m_w_in, v_w_in, min(256, d_model)))
    res["w_out"] = tuple(_adam_big("adam_w_out", recv_out, w_out, m_w_out, v_w_out, min(128, r_out)))
    res["w_glu"] = tuple(_adam_big("adam_w_glu", recv_glu, w_glu, m_w_glu, v_w_glu, r_glu))

    order = ["norm_pre_g", "w_in", "conv_w", "conv_b", "ssm_a_re", "ssm_a_im", "ssm_log_dt", "ssm_b_re", "ssm_b_im",
             "ssm_c_re", "ssm_c_im", "ssm_d", "w_glu", "b_glu", "w_out", "norm_post_g"]
    loss = lax.psum(loss_part[0, 0], AXES)
    grad_x = _unpermute_rows(gx_p)[None]
    return (loss, grad_x, *[res[n][0] for n in order], *[res[n][1] for n in order],
            *[res[n][2] for n in order], *[res[n][3] for n in order])
```

```python
import functools
import math

import jax
import jax.numpy as jnp
from jax import lax
from jax.experimental import pallas as pl
from jax.experimental.pallas import tpu as pltpu

F32 = jnp.float32
MXU_DTYPE = jnp.bfloat16
AXES = ("x", "y", "c")
N_DEV = 8
N_CHUNK = 8
LANES = 128
SSM_GROUP = 16
SSM_STATE = 64
HALF_CH = 64
HALF_G = HALF_CH // SSM_GROUP
HALF_W = HALF_G * SSM_STATE
EPS = 1e-6
ADAM_LR, ADAM_B1, ADAM_B2, ADAM_EPS, ADAM_WD, ADAM_STEP = 0.001, 0.9, 0.999, 1e-08, 0.01, 10
GELU_C = math.sqrt(2.0 / math.pi)
GELU_K = 0.044715
VMEM_LIMIT = 56 * 1024 * 1024


def _params(sem=None):
    return pltpu.CompilerParams(dimension_semantics=sem, vmem_limit_bytes=VMEM_LIMIT)


def _dot(a, b):
    return jnp.dot(a, b, preferred_element_type=F32)


def _dot_nt(a, b):
    return lax.dot_general(a, b, (((1,), (1,)), ((), ())), preferred_element_type=F32)


def _dot_tn(a, b):
    return lax.dot_general(a, b, (((0,), (0,)), ((), ())), preferred_element_type=F32)


def _sigmoid(z):
    return 1.0 / (1.0 + jnp.exp(-z))


def _flip(v, bit):
    return 1 - v if bit else v


def _peers():
    x, y, c = (lax.axis_index(a) for a in AXES)
    out = []
    for m in range(1, N_DEV):
        px, py, pc = _flip(x, (m >> 2) & 1), _flip(y, (m >> 1) & 1), _flip(c, m & 1)
        out.append((px, py, pc, 4 * px + 2 * py + pc))
    return out


class _Comm:
    def __init__(self, gathers=(), scatters=(), dests=None, into=None):
        self.n_g = len(gathers)
        self.operands = list(gathers) + list(scatters)
        self.n = len(self.operands)
        self.dests = dests or {}
        self.into = into or {}

    def out_shape(self):
        return [jax.ShapeDtypeStruct((N_DEV,) + a.shape if t < self.n_g else a.shape, a.dtype)
                for t, a in enumerate(self.operands)]

    def scratch(self):
        if not self.n:
            return []
        return [pltpu.SemaphoreType.DMA((self.n, N_DEV - 1)), pltpu.SemaphoreType.DMA((self.n, N_DEV - 1)),
                pltpu.SemaphoreType.DMA((self.n,))]

    def _copies(self, in_refs, out_refs, sems, arrivals):
        send_sems, recv_sems, local_sems = sems
        x, y, c = (lax.axis_index(a) for a in AXES)
        me = 4 * x + 2 * y + c

        def src(t, dev):
            return in_refs[t] if t < self.n_g else in_refs[t].at[dev]

        def member(t, dev):
            if t not in self.dests:
                return None
            return functools.reduce(jnp.logical_or, [dev == d for d in self.dests[t]])

        local = [(member(t, me), pltpu.make_async_copy(src(t, me), out_refs[t].at[me], local_sems.at[t]))
                 for t in range(self.n)]
        sends, recvs = [], []
        for t in range(self.n):
            for m, (px, py, pc, peer) in enumerate(_peers()):
                kw = dict(send_sem=send_sems.at[t, m], recv_sem=recv_sems.at[t, m],
                          device_id=(px, py, pc), device_id_type=pl.DeviceIdType.MESH)
                sends.append((member(t, peer), pltpu.make_async_remote_copy(
                    src_ref=src(t, peer), dst_ref=out_refs[t].at[me], **kw)))
                if arrivals:
                    recvs.append((member(t, me), pltpu.make_async_remote_copy(
                        src_ref=src(t, peer), dst_ref=out_refs[t].at[peer], **kw)))
        return local, sends, recvs

    @staticmethod
    def _do(cond, action):
        if cond is None:
            action()
        else:
            pl.when(cond)(action)

    def start(self, in_refs, out_refs, sems):
        local, sends, _ = self._copies(in_refs, out_refs, sems, arrivals=False)
        for cond, cp in local + sends:
            self._do(cond, cp.start)

    def finish(self, in_refs, out_refs, sems):
        local, sends, recvs = self._copies(in_refs, out_refs, sems, arrivals=True)
        for cond, cp in recvs:
            self._do(cond, cp.wait_recv)
        for cond, cp in sends:
            self._do(cond, cp.wait_send)
        for cond, cp in local:
            self._do(cond, cp.wait)


def _call(body, comm, *, name, grid, in_specs, out_specs, out_shape, operands, scratch_shapes=()):
    comm = comm or _Comm()
    n_in, n_out, n_scr, cn = len(in_specs), len(out_specs), len(scratch_shapes), comm.n
    landing = sorted(comm.into)
    aliases = {n_in + cn + q: n_out + t for q, t in enumerate(landing)}

    def wrapped(*refs):
        parts, o = [], 0
        for k in (n_in, cn, len(landing), n_out, cn, n_scr):
            parts.append(refs[o:o + k])
            o += k
        h_in, c_in, _, h_out, c_out, h_scr = parts
        sems = refs[o:]
        if cn:
            first = functools.reduce(jnp.logical_and, [pl.program_id(d) == 0 for d in range(len(grid))])

            @pl.when(first)
            def _():
                comm.start(c_in, c_out, sems)

        body(*h_in, *h_out, *h_scr)
        if cn:
            last = functools.reduce(jnp.logical_and, [pl.program_id(d) == grid[d] - 1 for d in range(len(grid))])

            @pl.when(last)
            def _():
                comm.finish(c_in, c_out, sems)

    any_ = pl.BlockSpec(memory_space=pl.ANY)
    res = pl.pallas_call(
        wrapped, name=name, grid=grid, in_specs=list(in_specs) + [any_] * (cn + len(landing)),
        out_specs=list(out_specs) + [any_] * cn,
        out_shape=list(out_shape) + comm.out_shape(), scratch_shapes=list(scratch_shapes) + comm.scratch(),
        input_output_aliases=aliases, compiler_params=_params(("arbitrary",) * len(grid)),
    )(*operands, *comm.operands, *[comm.into[t] for t in landing])
    return list(res[:n_out]), list(res[n_out:])


def _exchange(name, gathers, scatters):
    def body(tok_ref):
        tok_ref[...] = jnp.zeros_like(tok_ref)

    return _call(body, _Comm(gathers, scatters), name=name, grid=(1,), in_specs=[],
                 out_specs=[pl.BlockSpec((8, LANES), lambda i: (0, 0))],
                 out_shape=[jax.ShapeDtypeStruct((8, LANES), F32)], operands=[])[1]


def _ssm_prep(a_re, a_im, log_dt, bt_re, bt_im):
    def body(lr_ref, li_ref, ldt_ref, br_ref, bi_ref, lbr_ref, lbi_ref, qr_ref, qi_ref, bbr_ref, bbi_ref):
        lr, li = lr_ref[...], li_ref[...]
        dt = jnp.exp(ldt_ref[...])
        mag = jnp.exp(lr * dt)
        lbr, lbi = mag * jnp.cos(li * dt), mag * jnp.sin(li * dt)
        nr, ni = lbr - 1.0, lbi
        den = lr * lr + li * li
        qr = (nr * lr + ni * li) / den
        qi = (ni * lr - nr * li) / den
        br, bi = br_ref[...], bi_ref[...]
        lbr_ref[...], lbi_ref[...], qr_ref[...], qi_ref[...] = lbr, lbi, qr, qi
        bbr_ref[...] = qr * br - qi * bi
        bbi_ref[...] = qr * bi + qi * br

    s2 = jax.ShapeDtypeStruct(a_re.shape, F32)
    s3 = jax.ShapeDtypeStruct(bt_re.shape, F32)
    return pl.pallas_call(body, name="ssm_prep", out_shape=[s2, s2, s2, s2, s3, s3],
                          compiler_params=_params())(a_re, a_im, log_dt, bt_re, bt_im)


def _adam(w, g, m, v):
    m2 = ADAM_B1 * m + (1.0 - ADAM_B1) * g
    v2 = ADAM_B2 * v + (1.0 - ADAM_B2) * (g * g)
    m_hat = m2 / (1.0 - ADAM_B1 ** ADAM_STEP)
    v_hat = v2 / (1.0 - ADAM_B2 ** ADAM_STEP)
    delta = -ADAM_LR * (m_hat / (jnp.sqrt(v_hat) + ADAM_EPS) + ADAM_WD * w)
    return delta, m2, v2


def _small_update(direct, ssm):
    n_direct = len(direct)
    flat = [a for quad in direct for a in quad]
    names = ["da_r", "da_i", "dbb_r", "dbb_i", "lr", "li", "ldt", "bt_r", "bt_i", "lbr", "lbi", "qr", "qi"]
    flat += [ssm[k] for k in names]
    chain = ["a_re", "a_im", "log_dt", "bt_re", "bt_im"]
    for k in chain:
        flat += [ssm["w_" + k], ssm["m_" + k], ssm["v_" + k]]
    n_in = len(flat)

    def body(*refs):
        ins, outs = refs[:n_in], refs[n_in:]
        for p in range(n_direct):
            g, w, m, v = (r[...] for r in ins[4 * p:4 * p + 4])
            d, m2, v2 = _adam(w, g, m, v)
            outs[4 * p][...], outs[4 * p + 1][...], outs[4 * p + 2][...], outs[4 * p + 3][...] = g, d, m2, v2
        o = 4 * n_direct
        da_r, da_i, dbb_r, dbb_i, lr, li, ldt, bt_r, bt_i, lbr, lbi, qr, qi = (r[...] for r in ins[o:o + 13])
        dt = jnp.exp(ldt)
        g_br = qr * dbb_r + qi * dbb_i
        g_bi = qr * dbb_i - qi * dbb_r
        dq_r = jnp.sum(bt_r * dbb_r + bt_i * dbb_i, axis=1, keepdims=True)
        dq_i = jnp.sum(bt_r * dbb_i - bt_i * dbb_r, axis=1, keepdims=True)
        den = lr * lr + li * li
        cr, ci = lr / den, li / den
        gl_r = da_r + (cr * dq_r - ci * dq_i)
        gl_i = da_i + (cr * dq_i + ci * dq_r)
        w_r = qr * cr + qi * ci
        w_i = qi * cr - qr * ci
        g_lr = dt * (lbr * gl_r + lbi * gl_i) + (-w_r * dq_r - w_i * dq_i)
        g_li = dt * (lbr * gl_i - lbi * gl_r) + (-w_r * dq_i + w_i * dq_r)
        m_r = lr * lbr - li * lbi
        m_i = lr * lbi + li * lbr
        g_ldt = jnp.sum(m_r * gl_r + m_i * gl_i, axis=2, keepdims=True) * dt
        grads = [g_lr, g_li, g_ldt, g_br, g_bi]
        base_in, base_out = o + 13, 4 * n_direct
        for p, g in enumerate(grads):
            w, m, v = (r[...] for r in ins[base_in + 3 * p:base_in + 3 * p + 3])
            d, m2, v2 = _adam(w, g, m, v)
            q = base_out + 4 * p
            outs[q][...], outs[q + 1][...], outs[q + 2][...], outs[q + 3][...] = g, d, m2, v2

    out_shape = []
    for quad in direct:
        out_shape += [jax.ShapeDtypeStruct(quad[1].shape, F32)] * 4
    for k in chain:
        out_shape += [jax.ShapeDtypeStruct(ssm["w_" + k].shape, F32)] * 4
    res = pl.pallas_call(body, name="small_update", out_shape=out_shape, compiler_params=_params())(*flat)
    return [tuple(res[4 * p:4 * p + 4]) for p in range(n_direct + len(chain))]


def _sum_slots(name, pack):
    def body(p_ref, o_ref):
        acc = p_ref[0].astype(F32)
        for k in range(1, N_DEV):
            acc = acc + p_ref[k].astype(F32)
        o_ref[...] = acc

    return pl.pallas_call(body, name=name, out_shape=jax.ShapeDtypeStruct(pack.shape[1:], F32),
                          compiler_params=_params())(pack)


def _adam_big(name, recv, w, m, v, tr):
    _, rows, cols = recv.shape

    def body(r_ref, w_ref, m_ref, v_ref, g_ref, d_ref, m2_ref, v2_ref):
        g = r_ref[0].astype(F32)
        for k in range(1, N_DEV):
            g = g + r_ref[k].astype(F32)
        d, m2, v2 = _adam(w_ref[...], g, m_ref[...], v_ref[...])
        g_ref[...], d_ref[...], m2_ref[...], v2_ref[...] = g, d, m2, v2

    blk = pl.BlockSpec((tr, cols), lambda i: (i, 0))
    shp = jax.ShapeDtypeStruct((rows, cols), F32)
    return pl.pallas_call(
        body, name=name, grid=(rows // tr,),
        in_specs=[pl.BlockSpec((N_DEV, tr, cols), lambda i: (0, i, 0)), blk, blk, blk],
        out_specs=[blk] * 4, out_shape=[shp] * 4, compiler_params=_params(("parallel",)),
    )(recv, w, m, v)


def _norm_in(xp, g_pre, tm):
    seq, d_model = xp.shape

    def body(x_ref, g_ref, h_ref, ht_ref):
        x = x_ref[...]
        r = lax.rsqrt(jnp.mean(x * x, axis=-1, keepdims=True) + EPS)
        h = x * r * g_ref[...]
        h_ref[...] = h.astype(h_ref.dtype)
        ht_ref[...] = h.T.astype(ht_ref.dtype)

    return pl.pallas_call(
        body, name="norm_in", grid=(seq // tm,),
        in_specs=[pl.BlockSpec((tm, d_model), lambda i: (i, 0)), pl.BlockSpec((1, d_model), lambda i: (0, 0))],
        out_specs=[pl.BlockSpec((tm, d_model), lambda i: (i, 0)), pl.BlockSpec((d_model, tm), lambda i: (0, i))],
        out_shape=[jax.ShapeDtypeStruct((seq, d_model), MXU_DTYPE), jax.ShapeDtypeStruct((d_model, seq), MXU_DTYPE)],
        compiler_params=_params(("parallel",)),
    )(xp, g_pre)


GATHER_ORDER = (0, 1, 4, 2, 6, 5, 3, 7)
CONSUME_ORDER = (0, 1, 4, 2, 5, 3, 6, 7)


def _fwd_in(h, w_shard, order, comm, tm):
    seq, d_model = h.shape
    nc = w_shard.shape[1]
    n_i = seq // tm
    cn = comm.n

    def body(order_ref, h_ref, w_hbm, *rest):
        c_in, rest = rest[:cn], rest[cn:]
        proj_ref, wing = rest[0], rest[1]
        c_out, rest = rest[2:2 + cn], rest[2 + cn:]
        wbuf, send_sems, recv_sems, own_sem, load_sems = rest[:5]
        c_sems = rest[5:]
        k, i = pl.program_id(0), pl.program_id(1)
        x, y, c = (lax.axis_index(a) for a in AXES)
        me = 4 * x + 2 * y + c

        def dev(rel):
            return _flip(x, (rel >> 2) & 1), _flip(y, (rel >> 1) & 1), _flip(c, rel & 1)

        def slot(rel):
            px, py, pc = dev(rel)
            return 4 * px + 2 * py + pc

        def remote(src, block, to_rel, sem):
            return pltpu.make_async_remote_copy(
                src_ref=src, dst_ref=wing.at[block], send_sem=send_sems.at[sem], recv_sem=recv_sems.at[sem],
                device_id=dev(to_rel), device_id_type=pl.DeviceIdType.MESH)

        own = pltpu.make_async_copy(w_hbm, wing.at[me], own_sem)
        first_hand = [remote(w_hbm, me, GATHER_ORDER[p], p - 1) for p in range(1, 5)]
        passed_on = [remote(wing.at[slot(GATHER_ORDER[p])], slot(GATHER_ORDER[p]), 1, p + 2) for p in range(2, 5)]

        def load(q):
            return pltpu.make_async_copy(wing.at[slot(CONSUME_ORDER[q])], wbuf.at[q % 2], load_sems.at[q % 2])

        def take(q):
            p = GATHER_ORDER.index(CONSUME_ORDER[q])
            if p == 0:
                own.wait()
            else:
                remote(w_hbm, slot(GATHER_ORDER[p]), GATHER_ORDER[p], p - 1).wait_recv()
            if 2 <= p <= 4:
                passed_on[p - 2].start()
            load(q).start()

        @pl.when((k == 0) & (i == 0))
        def _():
            own.start()
            for cp in first_hand:
                cp.start()
            comm.start(c_in, c_out, c_sems)
            take(0)

        for q in range(N_DEV):
            @pl.when((k == q) & (i == 0))
            def _():
                load(q).wait()

            if q + 1 < N_DEV:
                @pl.when((k == q) & (i == n_i - 1))
                def _():
                    take(q + 1)

        proj_ref[...] = _dot(h_ref[...], wbuf[k % 2])

        @pl.when((k == N_DEV - 1) & (i == n_i - 1))
        def _():
            for cp in first_hand + passed_on:
                cp.wait_send()
            comm.finish(c_in, c_out, c_sems)

    any_ = pl.BlockSpec(memory_space=pl.ANY)
    grid_spec = pltpu.PrefetchScalarGridSpec(
        num_scalar_prefetch=1, grid=(N_DEV, n_i),
        in_specs=[pl.BlockSpec((tm, d_model), lambda k, i, o: (i, 0)), any_] + [any_] * cn,
        out_specs=[pl.BlockSpec((tm, nc), lambda k, i, o: (i, o[k])), any_] + [any_] * cn,
        scratch_shapes=[pltpu.VMEM((2, d_model, nc), w_shard.dtype), pltpu.SemaphoreType.DMA((N_DEV - 1,)),
                        pltpu.SemaphoreType.DMA((N_DEV - 1,)), pltpu.SemaphoreType.DMA, pltpu.SemaphoreType.DMA((2,))]
        + comm.scratch())
    res = pl.pallas_call(
        body, name="fwd_in", grid_spec=grid_spec,
        out_shape=[jax.ShapeDtypeStruct((seq, N_DEV * nc), F32),
                   jax.ShapeDtypeStruct((N_DEV, d_model, nc), w_shard.dtype)] + comm.out_shape(),
        compiler_params=_params(("arbitrary", "arbitrary")),
    )(order, h, w_shard, *comm.operands)
    return res[0], res[1], list(res[2:])


def _shift_prev(a):
    n = a.shape[0]
    last = a[n - N_CHUNK:, :]
    row = lax.broadcasted_iota(jnp.int32, last.shape, 0)
    wrap = jnp.where(row == 0, 0.0, pltpu.roll(last, 1, axis=0))
    return jnp.concatenate([wrap, a[:n - N_CHUNK, :]], axis=0)


def _shift_next(a):
    first = a[:N_CHUNK, :]
    row = lax.broadcasted_iota(jnp.int32, first.shape, 0)
    wrap = jnp.where(row == N_CHUNK - 1, 0.0, pltpu.roll(first, N_CHUNK - 1, axis=0))
    return jnp.concatenate([a[N_CHUNK:, :], wrap], axis=0)


def _conv_specs(seq, d_conv):
    nblk = d_conv // LANES
    return [pl.BlockSpec((seq, LANES), functools.partial(lambda i, o: (0, o + i), o=q * nblk)) for q in range(4)]


def _conv_fwd(proj, conv_w8, conv_b, d_conv):
    seq = proj.shape[0]

    def body(bg_ref, cg_ref, v_ref, zc_ref, w_ref, b_ref, y_ref):
        cv = cg_ref[...] * v_ref[...]
        s1 = _shift_prev(cv)
        s2 = _shift_prev(s1)
        conv = b_ref[...] + w_ref[0:1, :] * s2 + w_ref[1:2, :] * s1 + w_ref[2:3, :] * cv
        z = zc_ref[...]
        y_ref[...] = bg_ref[...] * conv * (z * _sigmoid(z))

    col = pl.BlockSpec((seq, LANES), lambda i: (0, i))
    return pl.pallas_call(
        body, name="conv_fwd", grid=(d_conv // LANES,),
        in_specs=_conv_specs(seq, d_conv) + [pl.BlockSpec((8, LANES), lambda i: (0, i)), pl.BlockSpec((1, LANES), lambda i: (0, i))],
        out_specs=col, out_shape=jax.ShapeDtypeStruct((seq, d_conv), F32),
        compiler_params=_params(("parallel",)),
    )(proj, proj, proj, proj, conv_w8, conv_b)


def _conv_bwd(proj, dyc, conv_w8, conv_b, d_conv):
    seq = proj.shape[0]

    def body(bg_ref, cg_ref, v_ref, zc_ref, dy_ref, w_ref, b_ref, d4_ref, dcb_ref, dcw_ref):
        bg, cg, v, z = bg_ref[...], cg_ref[...], v_ref[...], zc_ref[...]
        w0, w1, w2 = w_ref[0:1, :], w_ref[1:2, :], w_ref[2:3, :]
        cv = cg * v
        s1 = _shift_prev(cv)
        s2 = _shift_prev(s1)
        conv = b_ref[...] + w0 * s2 + w1 * s1 + w2 * cv
        sig = _sigmoid(z)
        dy = dy_ref[...]
        g1 = dy * (z * sig)
        d_conv_ = g1 * bg
        d4_ref[0] = (g1 * conv).astype(d4_ref.dtype)
        d4_ref[3] = (dy * bg * conv * (sig * (1.0 + z * (1.0 - sig)))).astype(d4_ref.dtype)
        n1 = _shift_next(d_conv_)
        n2 = _shift_next(n1)
        d_cv = w2 * d_conv_ + w1 * n1 + w0 * n2
        d4_ref[1] = (d_cv * v).astype(d4_ref.dtype)
        d4_ref[2] = (d_cv * cg).astype(d4_ref.dtype)
        dcb_ref[...] = jnp.sum(d_conv_, axis=0, keepdims=True)
        rows = [jnp.sum(d_conv_ * s, axis=0, keepdims=True) for s in (s2, s1, cv)]
        dcw_ref[...] = jnp.concatenate(rows + [jnp.zeros((5, LANES), F32)], axis=0)

    col = pl.BlockSpec((seq, LANES), lambda i: (0, i))
    return pl.pallas_call(
        body, name="conv_bwd", grid=(d_conv // LANES,),
        in_specs=_conv_specs(seq, d_conv) + [col, pl.BlockSpec((8, LANES), lambda i: (0, i)), pl.BlockSpec((1, LANES), lambda i: (0, i))],
        out_specs=[pl.BlockSpec((4, seq, LANES), lambda i: (0, 0, i)), pl.BlockSpec((1, LANES), lambda i: (0, i)),
                   pl.BlockSpec((8, LANES), lambda i: (0, i))],
        out_shape=[jax.ShapeDtypeStruct((4, seq, d_conv), MXU_DTYPE), jax.ShapeDtypeStruct((1, d_conv), F32),
                   jax.ShapeDtypeStruct((8, d_conv), F32)],
        compiler_params=_params(("parallel",)),
    )(proj, proj, proj, proj, dyc, conv_w8, conv_b)


def _cmul(ar, ai, br, bi):
    return ar * br - ai * bi, ar * bi + ai * br


def _cpow(ar, ai, n):
    rr, ri = jnp.ones_like(ar), jnp.zeros_like(ai)
    while n:
        if n & 1:
            rr, ri = _cmul(rr, ri, ar, ai)
        n >>= 1
        if n:
            ar, ai = _cmul(ar, ai, ar, ai)
    return rr, ri


def _down(v, k):
    row = lax.broadcasted_iota(jnp.int32, v.shape, 0)
    return jnp.where(row >= k, pltpu.roll(v, k, axis=0), 0.0)


def _up(v, k):
    row = lax.broadcasted_iota(jnp.int32, v.shape, 0)
    return jnp.where(row < N_CHUNK - k, pltpu.roll(v, N_CHUNK - k, axis=0), 0.0)


def _chunk_carry(fr, fi, mr, mi, shift):
    vr, vi = shift(fr, 1), shift(fi, 1)
    for k in (1, 2, 4):
        pr, pi = _cmul(mr, mi, shift(vr, k), shift(vi, k))
        vr, vi = vr + pr, vi + pi
        mr, mi = _cmul(mr, mi, mr, mi)
    return vr, vi


def _tile(ref, j, width, part):
    return ref.at[pl.ds(pl.multiple_of(j * N_CHUNK, N_CHUNK), N_CHUNK), pl.ds(part * width, width)]


def _row(t, k):
    return jnp.broadcast_to(t[k:k + 1, :], t.shape)


def _power_table(tab_ref, ar, ai, steps, width):
    e = lax.broadcasted_iota(jnp.int32, ar.shape, 0) + 1
    rr, ri = jnp.ones_like(ar), jnp.zeros_like(ai)
    br, bi = ar, ai
    for bit in range(4):
        mr, mi = _cmul(rr, ri, br, bi)
        take = ((e >> bit) & 1) == 1
        rr, ri = jnp.where(take, mr, rr), jnp.where(take, mi, ri)
        if bit < 3:
            br, bi = _cmul(br, bi, br, bi)
    _tile(tab_ref, 0, width, 0)[...] = rr
    _tile(tab_ref, 0, width, 1)[...] = ri

    def step(m, carry):
        tr, ti = _cmul(carry[0], carry[1], br, bi)
        _tile(tab_ref, m, width, 0)[...] = tr
        _tile(tab_ref, m, width, 1)[...] = ti
        return tr, ti

    lax.fori_loop(1, steps // N_CHUNK, step, (rr, ri))


def _last_power(tab_ref, steps, width):
    shape = (N_CHUNK, width)
    return (jnp.broadcast_to(tab_ref[steps - 1:steps, 0:width], shape),
            jnp.broadcast_to(tab_ref[steps - 1:steps, width:2 * width], shape))


def _scan_fwd(s_ref, ar, ai, steps, width):
    def step(j, carry):
        sr, si = carry
        nr = ar * sr - ai * si + _tile(s_ref, j, width, 0)[...]
        ni = ar * si + ai * sr + _tile(s_ref, j, width, 1)[...]
        _tile(s_ref, j, width, 0)[...] = nr
        _tile(s_ref, j, width, 1)[...] = ni
        return nr, ni

    z = jnp.zeros((N_CHUNK, width), F32)
    return lax.fori_loop(0, steps, step, (z, z), unroll=4)


def _scan_both(s_ref, g_ref, ar, ai, steps, width):
    def step(q, carry):
        sr, si, gr, gi = carry
        j, jb = q, steps - 1 - q
        nsr = ar * sr - ai * si + _tile(s_ref, j, width, 0)[...]
        nsi = ar * si + ai * sr + _tile(s_ref, j, width, 1)[...]
        ngr = ar * gr + ai * gi + _tile(g_ref, jb, width, 0)[...]
        ngi = ar * gi - ai * gr + _tile(g_ref, jb, width, 1)[...]
        _tile(s_ref, j, width, 0)[...] = nsr
        _tile(s_ref, j, width, 1)[...] = nsi
        _tile(g_ref, jb, width, 0)[...] = ngr
        _tile(g_ref, jb, width, 1)[...] = ngi
        return nsr, nsi, ngr, ngi

    z = jnp.zeros((N_CHUNK, width), F32)
    return lax.fori_loop(0, steps, step, (z, z, z, z), unroll=2)


def _patch_fwd(s_ref, tab_ref, cr, ci, steps, width):
    def tile(m, _):
        tr, ti = _tile(tab_ref, m, width, 0)[...], _tile(tab_ref, m, width, 1)[...]
        for k in range(N_CHUNK):
            fr, fi = _cmul(_row(tr, k), _row(ti, k), cr, ci)
            j = m * N_CHUNK + k
            _tile(s_ref, j, width, 0)[...] += fr
            _tile(s_ref, j, width, 1)[...] += fi
        return 0

    lax.fori_loop(0, steps // N_CHUNK, tile, 0)


def _lam_rows(lam_ref, hh, width):
    return (jnp.broadcast_to(lam_ref[hh, 0:1, :], (N_CHUNK, width)),
            jnp.broadcast_to(lam_ref[hh, 1:2, :], (N_CHUNK, width)))


def _ssm_specs(seq, col0):
    return dict(
        col=pl.BlockSpec((seq, LANES), lambda i: (0, col0 + i)),
        lam=pl.BlockSpec((2, 2, HALF_W), lambda i: (i, 0, 0)),
        bb=pl.BlockSpec((2, HALF_CH, 2 * HALF_W), lambda i: (i, 0, 0)),
        cc=pl.BlockSpec((2, 2 * HALF_W, HALF_CH), lambda i: (i, 0, 0)),
        vec=pl.BlockSpec((1, LANES), lambda i: (0, i)),
        out=pl.BlockSpec((seq, LANES), lambda i: (0, i)),
    )


def _ssm_fwd(proj, lam, bbcat, cccat, d_skip, d_ssm, u_col0, comm=None):
    seq = proj.shape[0]
    steps = seq // N_CHUNK

    def body(u_ref, lam_ref, bb_ref, cc_ref, d_ref, yp_ref, s_ref, tab_ref):
        for hh in range(2):
            lanes = slice(HALF_CH * hh, HALF_CH * (hh + 1))
            u_half = u_ref[:, lanes]
            ar, ai = _lam_rows(lam_ref, hh, HALF_W)
            _power_table(tab_ref, ar, ai, steps, HALF_W)
            s_ref[...] = _dot(u_half.astype(MXU_DTYPE), bb_ref[hh])
            fr, fi = _scan_fwd(s_ref, ar, ai, steps, HALF_W)
            pr, pi = _last_power(tab_ref, steps, HALF_W)
            cr, ci = _chunk_carry(fr, fi, pr, pi, _down)
            _patch_fwd(s_ref, tab_ref, cr, ci, steps, HALF_W)
            y = _dot(s_ref[...].astype(MXU_DTYPE), cc_ref[hh])
            yp_ref[:, lanes] = y + d_ref[:, lanes] * u_half

    sp = _ssm_specs(seq, u_col0 // LANES)
    return _call(
        body, comm, name="ssm_fwd", grid=(d_ssm // LANES,),
        in_specs=[sp["col"], sp["lam"], sp["bb"], sp["cc"], sp["vec"]], out_specs=[sp["out"]],
        out_shape=[jax.ShapeDtypeStruct((seq, d_ssm), F32)],
        scratch_shapes=[pltpu.VMEM((seq, 2 * HALF_W), F32), pltpu.VMEM((steps, 2 * HALF_W), F32)],
        operands=[proj, lam, bbcat, cccat, d_skip])


def _ssm_bwd(proj, dyp, lam, bbcat, cccat, d_skip, d_ssm, u_col0, comm=None):
    seq = proj.shape[0]
    steps = seq // N_CHUNK
    n_half = 2 * d_ssm // LANES
    width = HALF_W

    def body(u_ref, dyp_ref, lam_ref, bb_ref, cc_ref, d_ref, du_ref, dbb_ref, dcc_ref, da_ref, dd_ref,
             s_ref, g_ref, tab_ref):
        n_tiles = steps // N_CHUNK
        for hh in range(2):
            lanes = slice(HALF_CH * hh, HALF_CH * (hh + 1))
            u_half, dy_half = u_ref[:, lanes], dyp_ref[:, lanes]
            dy_mx = dy_half.astype(MXU_DTYPE)
            ar, ai = _lam_rows(lam_ref, hh, width)
            _power_table(tab_ref, ar, ai, steps, width)
            s_ref[...] = _dot(u_half.astype(MXU_DTYPE), bb_ref[hh])
            g_ref[...] = _dot_nt(dy_mx, cc_ref[hh])
            fr, fi, lr_, li_ = _scan_both(s_ref, g_ref, ar, ai, steps, width)
            pr, pi = _last_power(tab_ref, steps, width)
            cr, ci = _chunk_carry(fr, fi, pr, pi, _down)
            gr, gi = _chunk_carry(lr_, li_, pr, -pi, _up)

            def tile(m, carry):
                sr, si, accr, acci = carry
                t1r, t1i = _tile(tab_ref, m, width, 0)[...], _tile(tab_ref, m, width, 1)[...]
                mb = n_tiles - 1 - m
                t2r, t2i = _tile(tab_ref, mb, width, 0)[...], _tile(tab_ref, mb, width, 1)[...]
                for k in range(N_CHUNK):
                    j = m * N_CHUNK + k
                    xr, xi = _cmul(_row(t1r, k), _row(t1i, k), cr, ci)
                    nsr = _tile(s_ref, j, width, 0)[...] + xr
                    nsi = _tile(s_ref, j, width, 1)[...] + xi
                    _tile(s_ref, j, width, 0)[...] = nsr
                    _tile(s_ref, j, width, 1)[...] = nsi
                    qr, qi = _row(t2r, N_CHUNK - 1 - k), _row(t2i, N_CHUNK - 1 - k)
                    ngr = _tile(g_ref, j, width, 0)[...] + (qr * gr + qi * gi)
                    ngi = _tile(g_ref, j, width, 1)[...] + (qr * gi - qi * gr)
                    _tile(g_ref, j, width, 0)[...] = ngr
                    _tile(g_ref, j, width, 1)[...] = ngi
                    accr = accr + (sr * ngr + si * ngi)
                    acci = acci + (sr * ngi - si * ngr)
                    sr, si = nsr, nsi
                return sr, si, accr, acci

            z = jnp.zeros((N_CHUNK, width), F32)
            _, _, accr, acci = lax.fori_loop(0, n_tiles, tile, (cr, ci, z, z))
            da_ref[hh, :, 0:width] = jnp.sum(accr, axis=0, keepdims=True)
            da_ref[hh, :, width:2 * width] = jnp.sum(acci, axis=0, keepdims=True)

            g_mx = g_ref[...].astype(MXU_DTYPE)
            dcc_ref[hh] = _dot_tn(dy_mx, s_ref[...].astype(MXU_DTYPE)).T
            dbb_ref[hh] = _dot_tn(u_half.astype(MXU_DTYPE), g_mx)
            du = _dot_nt(g_mx, bb_ref[hh]) + d_ref[:, lanes] * dy_half
            du_ref[:, lanes] = du.astype(du_ref.dtype)
            dd_ref[:, lanes] = jnp.sum(dy_half * u_half, axis=0, keepdims=True)

    sp = _ssm_specs(seq, u_col0 // LANES)
    return _call(
        body, comm, name="ssm_bwd", grid=(d_ssm // LANES,),
        in_specs=[sp["col"], sp["out"], sp["lam"], sp["bb"], sp["cc"], sp["vec"]],
        out_specs=[sp["out"], sp["bb"], sp["cc"], pl.BlockSpec((2, 1, 2 * width), lambda i: (i, 0, 0)), sp["vec"]],
        out_shape=[jax.ShapeDtypeStruct((seq, d_ssm), MXU_DTYPE),
                   jax.ShapeDtypeStruct((n_half, HALF_CH, 2 * width), F32),
                   jax.ShapeDtypeStruct((n_half, 2 * width, HALF_CH), F32),
                   jax.ShapeDtypeStruct((n_half, 1, 2 * width), F32),
                   jax.ShapeDtypeStruct((1, d_ssm), F32)],
        scratch_shapes=[pltpu.VMEM((seq, 2 * width), F32), pltpu.VMEM((seq, 2 * width), F32),
                        pltpu.VMEM((steps, 2 * width), F32)],
        operands=[proj, dyp, lam, bbcat, cccat, d_skip])


def _tail(xp, tp, proj, yconv, yp, w_glu, b_glu, w_out, g_post, zs_col0, tm):
    seq, d_model = xp.shape
    d_conv, d_ssm = yconv.shape[1], yp.shape[1]
    d_mix = d_conv + d_ssm
    assert zs_col0 % d_ssm == 0

    def body(x_ref, t_ref, zs_ref, yc_ref, yp_ref, wglu_hbm, bglu_ref, wout_hbm, gpost_ref,
             dy_ref, do_ref, mixt_ref, dyc_ref, dyp_ref, dzs_ref, ygt_ref, dq_ref, loss_ref, dgpost_ref, dbglu_ref,
             wglu, wout, w_sems):
        first = pl.program_id(0) == 0
        load_glu = pltpu.make_async_copy(wglu_hbm, wglu, w_sems.at[0])
        load_out = pltpu.make_async_copy(wout_hbm, wout, w_sems.at[1])

        @pl.when(first)
        def _():
            load_glu.start()
            load_out.start()
            loss_ref[...] = jnp.zeros_like(loss_ref)
            dgpost_ref[...] = jnp.zeros_like(dgpost_ref)
            dbglu_ref[...] = jnp.zeros_like(dbglu_ref)

        a = yp_ref[...]
        th = jnp.tanh(GELU_C * (a + GELU_K * (a * a * a)))
        yg = a * (0.5 * (1.0 + th))
        dgelu = 0.5 * (1.0 + th) + 0.5 * a * (1.0 - th * th) * (GELU_C * (1.0 + 3.0 * GELU_K * a * a))
        yg_mx = yg.astype(MXU_DTYPE)
        pl.when(first)(load_glu.wait)
        sq = _sigmoid(_dot(yg_mx, wglu[...]) + bglu_ref[...])
        y2 = yg * sq
        zs = zs_ref[...]
        sz = _sigmoid(zs)
        silz = zs * sz
        yc, ys = yc_ref[...], y2 * silz
        mix = jnp.concatenate([yc, ys], axis=1).astype(MXU_DTYPE)
        mixt_ref[0:d_conv, :] = yc.T.astype(MXU_DTYPE)
        mixt_ref[d_conv:, :] = ys.T.astype(MXU_DTYPE)
        pl.when(first)(load_out.wait)
        o = _dot(mix, wout[...])
        r2 = lax.rsqrt(jnp.mean(o * o, axis=-1, keepdims=True) + EPS)
        on = o * r2
        gpost = gpost_ref[...]
        err = (x_ref[...] + on * gpost) - t_ref[...]
        loss_ref[...] += 0.5 * jnp.sum(jnp.mean(err * err, axis=-1, keepdims=True), axis=0, keepdims=True)
        dy = err * (1.0 / d_model)
        dy_ref[...] = dy
        dgpost_ref[...] += jnp.sum(dy * on, axis=0, keepdims=True)
        d_on = dy * gpost
        d_o = r2 * (d_on - on * jnp.mean(d_on * on, axis=-1, keepdims=True))
        do_mx = d_o.astype(MXU_DTYPE)
        do_ref[...] = do_mx
        d_mix_ = _dot_nt(do_mx, wout[...])
        dyc_ref[...] = d_mix_[:, :d_conv]
        d_yssm = d_mix_[:, d_conv:]
        d_y2 = d_yssm * silz
        dzs_ref[...] = (d_yssm * y2 * (sz * (1.0 + zs * (1.0 - sz)))).astype(dzs_ref.dtype)
        d_q = d_y2 * yg * (sq * (1.0 - sq))
        dq_mx = d_q.astype(MXU_DTYPE)
        dq_ref[...] = dq_mx
        ygt_ref[...] = yg.T.astype(MXU_DTYPE)
        dbglu_ref[...] += jnp.sum(d_q, axis=0, keepdims=True)
        d_yg = d_y2 * sq + _dot_nt(dq_mx, wglu[...])
        dyp_ref[...] = d_yg * dgelu

    def rows(width, col=0):
        return pl.BlockSpec((tm, width), lambda i: (i, col))

    def fixed(width):
        return pl.BlockSpec((1, width), lambda i: (0, 0))

    def cols(height):
        return pl.BlockSpec((height, tm), lambda i: (0, i))

    any_ = pl.BlockSpec(memory_space=pl.ANY)
    return pl.pallas_call(
        body, name="tail", grid=(seq // tm,),
        in_specs=[rows(d_model), rows(d_model), rows(d_ssm, zs_col0 // d_ssm), rows(d_conv), rows(d_ssm),
                  any_, fixed(d_ssm), any_, fixed(d_model)],
        out_specs=[rows(d_model), rows(d_model), cols(d_mix), rows(d_conv), rows(d_ssm), rows(d_ssm), cols(d_ssm),
                   rows(d_ssm), fixed(LANES), fixed(d_model), fixed(d_ssm)],
        out_shape=[jax.ShapeDtypeStruct((seq, d_model), F32), jax.ShapeDtypeStruct((seq, d_model), MXU_DTYPE),
                   jax.ShapeDtypeStruct((d_mix, seq), MXU_DTYPE), jax.ShapeDtypeStruct((seq, d_conv), F32),
                   jax.ShapeDtypeStruct((seq, d_ssm), F32), jax.ShapeDtypeStruct((seq, d_ssm), MXU_DTYPE),
                   jax.ShapeDtypeStruct((d_ssm, seq), MXU_DTYPE), jax.ShapeDtypeStruct((seq, d_ssm), MXU_DTYPE),
                   jax.ShapeDtypeStruct((1, LANES), F32), jax.ShapeDtypeStruct((1, d_model), F32),
                   jax.ShapeDtypeStruct((1, d_ssm), F32)],
        scratch_shapes=[pltpu.VMEM(w_glu.shape, MXU_DTYPE), pltpu.VMEM(w_out.shape, MXU_DTYPE),
                        pltpu.SemaphoreType.DMA((2,))],
        compiler_params=_params(("arbitrary",)),
    )(xp, tp, proj, yconv, yp, w_glu, b_glu, w_out, g_post)


def _bwd_in(d4, du, dzs, gr, win_g, xp, dy, g_pre, comm, tm):
    seq, d_model = xp.shape
    nb, _, nc = win_g.shape
    per = d4.shape[2] // gr

    def body(d4_ref, du_ref, dzs_ref, w_hbm, x_ref, dy_ref, g_ref, gx_ref, dg_ref, w_all, w_sems):
        def granule(g):
            p, cols = g // per, slice(g % per * gr, (g % per + 1) * gr)
            if p < 4:
                return d4_ref[p, :, cols]
            return du_ref[:, cols] if p == 4 else dzs_ref[:, cols]

        i = pl.program_id(0)
        loads = [pltpu.make_async_copy(w_hbm.at[k], w_all.at[k], w_sems.at[k]) for k in range(nb)]

        @pl.when(i == 0)
        def _():
            dg_ref[...] = jnp.zeros_like(dg_ref)
            for cp in loads:
                cp.start()

        dh = None
        for k in range(nb):
            @pl.when(i == 0)
            def _():
                loads[k].wait()

            dp = jnp.concatenate([granule(g) for g in range(k * nc // gr, (k + 1) * nc // gr)], axis=1)
            part = _dot_nt(dp, w_all[k])
            dh = part if dh is None else dh + part

        x = x_ref[...]
        r = lax.rsqrt(jnp.mean(x * x, axis=-1, keepdims=True) + EPS)
        xn = x * r
        dg_ref[...] += jnp.sum(dh * xn, axis=0, keepdims=True)
        dxn = dh * g_ref[...]
        gx_ref[...] = r * (dxn - xn * jnp.mean(dxn * xn, axis=-1, keepdims=True)) + dy_ref[...]

    row = pl.BlockSpec((tm, d_model), lambda i: (i, 0))
    vec = pl.BlockSpec((1, d_model), lambda i: (0, 0))
    return _call(
        body, comm, name="bwd_in", grid=(seq // tm,),
        in_specs=[pl.BlockSpec((4, tm, d4.shape[2]), lambda i: (0, i, 0)),
                  pl.BlockSpec((tm, du.shape[1]), lambda i: (i, 0)), pl.BlockSpec((tm, dzs.shape[1]), lambda i: (i, 0)),
                  pl.BlockSpec(memory_space=pl.ANY), row, row, vec],
        out_specs=[row, vec],
        out_shape=[jax.ShapeDtypeStruct((seq, d_model), F32), jax.ShapeDtypeStruct((1, d_model), F32)],
        scratch_shapes=[pltpu.VMEM(win_g.shape, win_g.dtype), pltpu.SemaphoreType.DMA((nb,))],
        operands=[d4, du, dzs, win_g, xp, dy, g_pre])


def _lookup(g, table):
    out = jnp.int32(table[0])
    for gi in range(1, len(table)):
        if table[gi] != table[gi - 1]:
            out = jnp.where(g >= gi, jnp.int32(table[gi]), out)
    return out


def _held(values, used):
    cur = next(v for v, u in zip(values, used) if u)
    out = []
    for v, u in zip(values, used):
        cur = v if u else cur
        out.append(cur)
    return out


def _dw_in(name, ht, d4, du, dzs, granules, gr, nc, tm, comm=None):
    d_model, seq = ht.shape
    per = d4.shape[2] // gr
    piece, col = [g // per for g in granules], [g % per for g in granules]
    sources = [(d4, [p < 4 for p in piece]), (du, [p == 4 for p in piece]), (dzs, [p == 5 for p in piece])]
    sources = [(a, used) for a, used in sources if any(used)]
    select = [next(s for s, (_, used) in enumerate(sources) if used[q]) for q in range(len(granules))]
    owner, place = [g * gr // nc for g in granules], [g * gr % nc // gr for g in granules]

    def body(a_ref, *refs):
        src_refs, o_ref = refs[:-1], refs[-1]
        j = pl.program_id(0)
        for s, ref in enumerate(src_refs):
            @pl.when(_lookup(j, select) == s)
            def _():
                o_ref[...] = _dot(a_ref[...], ref[...]).astype(o_ref.dtype)

    in_specs = [pl.BlockSpec((tm, seq), lambda j, i: (i, 0))]
    for a, used in sources:
        cols = _held(col, used)
        if a.ndim == 3:
            rows = _held(piece, used)
            in_specs.append(pl.BlockSpec((None, seq, gr), functools.partial(
                lambda j, i, rows, cols: (_lookup(j, rows), 0, _lookup(j, cols)), rows=rows, cols=cols)))
        else:
            in_specs.append(pl.BlockSpec((seq, gr), functools.partial(
                lambda j, i, cols: (0, _lookup(j, cols)), cols=cols)))
    return _call(
        body, comm, name=name, grid=(len(granules), d_model // tm), in_specs=in_specs,
        out_specs=[pl.BlockSpec((None, tm, gr), lambda j, i: (_lookup(j, owner), i, _lookup(j, place)))],
        out_shape=[jax.ShapeDtypeStruct((N_DEV, d_model, nc), MXU_DTYPE)],
        operands=[ht] + [a for a, _ in sources])


def _wgrad(name, at, b, tm, tn, out_shape, out_block, out_index, comm=None):
    m, seq = at.shape
    n = b.shape[1]

    def body(a_ref, b_ref, o_ref):
        o_ref[...] = _dot(a_ref[...], b_ref[...]).astype(o_ref.dtype)

    return _call(
        body, comm, name=name, grid=(n // tn, m // tm),
        in_specs=[pl.BlockSpec((tm, seq), lambda j, i: (i, 0)), pl.BlockSpec((seq, tn), lambda j, i: (0, j))],
        out_specs=[pl.BlockSpec(out_block, lambda j, i: out_index(i, j))],
        out_shape=[jax.ShapeDtypeStruct(out_shape, MXU_DTYPE)],
        operands=[at, b])


def _eye_g():
    return jnp.eye(HALF_G, dtype=F32)


def _bb_blockdiag(bbt_r, bbt_i):
    n_half = bbt_r.shape[0] // HALF_G

    def one(t):
        t = t.reshape(n_half, HALF_G, SSM_GROUP, SSM_STATE)
        t = t[:, :, :, None, :] * _eye_g()[None, :, None, :, None]
        return t.reshape(n_half, HALF_CH, HALF_W)

    return jnp.concatenate([one(bbt_r), one(bbt_i)], axis=-1)


def _cc_blockdiag(c_re, c_im):
    n_half = c_re.shape[0] // HALF_G

    def one(t):
        t = t.reshape(n_half, HALF_G, SSM_GROUP, SSM_STATE)
        t = jnp.transpose(t, (0, 3, 1, 2))
        t = t[:, None, :, :, :] * _eye_g()[None, :, None, :, None]
        return t.reshape(n_half, HALF_W, HALF_CH)

    return jnp.concatenate([one(c_re), one(-c_im)], axis=1)


def _bb_diag(dbb):
    n_half = dbb.shape[0]
    t = dbb.reshape(n_half, HALF_G, SSM_GROUP, 2, HALF_G, SSM_STATE)
    t = jnp.sum(t * _eye_g()[None, :, None, None, :, None], axis=4)
    t = jnp.transpose(t, (3, 0, 1, 2, 4))
    return t.reshape(2, n_half * HALF_G, SSM_GROUP, SSM_STATE)


def _cc_diag(dcc):
    n_half = dcc.shape[0]
    t = dcc.reshape(n_half, 2, HALF_G, SSM_STATE, HALF_G, SSM_GROUP)
    t = jnp.sum(t * _eye_g()[None, None, :, None, :, None], axis=2)
    t = jnp.transpose(t, (1, 0, 3, 4, 2))
    return t.reshape(2, n_half * HALF_G, SSM_GROUP, SSM_STATE)


def _permute_rows(a):
    seq, d = a.shape
    return a.reshape(N_CHUNK, seq // N_CHUNK, d).transpose(1, 0, 2).reshape(seq, d)


def _unpermute_rows(a):
    seq, d = a.shape
    return a.reshape(seq // N_CHUNK, N_CHUNK, d).transpose(1, 0, 2).reshape(seq, d)


def _pack_rows(shape):
    return -(-math.prod(shape) // (8 * LANES)) * 8


def _pack(parts, dtype=F32):
    rows = []
    for p in parts:
        flat = p.reshape(-1).astype(dtype)
        rows.append(jnp.pad(flat, (0, _pack_rows(p.shape) * LANES - flat.shape[0])).reshape(-1, LANES))
    return jnp.concatenate(rows, axis=0)


def _unpack(packed, shapes):
    out, o = [], 0
    for s in shapes:
        n = _pack_rows(s)
        out.append(packed[o:o + n].reshape(-1)[:math.prod(s)].reshape(s))
        o += n
    return out


def kernel(x, norm_pre_g, w_in, conv_w, conv_b, ssm_a_re, ssm_a_im, ssm_log_dt, ssm_b_re, ssm_b_im, ssm_c_re, ssm_c_im, ssm_d, w_glu, b_glu, w_out, norm_post_g, loss_target, m_norm_pre_g, m_w_in, m_conv_w, m_conv_b, m_ssm_a_re, m_ssm_a_im, m_ssm_log_dt, m_ssm_b_re, m_ssm_b_im, m_ssm_c_re, m_ssm_c_im, m_ssm_d, m_w_glu, m_b_glu, m_w_out, m_norm_post_g, v_norm_pre_g, v_w_in, v_conv_w, v_conv_b, v_ssm_a_re, v_ssm_a_im, v_ssm_log_dt, v_ssm_b_re, v_ssm_b_im, v_ssm_c_re, v_ssm_c_im, v_ssm_d, v_w_glu, v_b_glu, v_w_out, v_norm_post_g):
    seq, d_model = x.shape[1], x.shape[2]
    d_conv, d_ssm = conv_b.shape[0], ssm_d.shape[0]
    groups, states = ssm_a_re.shape
    assert x.shape[0] == 1 and seq % (8 * N_CHUNK) == 0 and d_conv == d_ssm
    assert (groups, states) == (d_ssm // SSM_GROUP, SSM_STATE) and d_ssm % LANES == 0
    me = 4 * lax.axis_index("x") + 2 * lax.axis_index("y") + lax.axis_index("c")
    tm = min(512, seq)

    xp = _permute_rows(x[0])
    tp = _permute_rows(loss_target[0])
    row = lambda a: a.reshape(1, -1)
    conv_w8 = jnp.pad(conv_w, ((0, 8 - conv_w.shape[0]), (0, 0)))

    g3 = lambda a: a.reshape(groups, 1, -1)
    bt_re, bt_im = jnp.transpose(ssm_b_re, (0, 2, 1)), jnp.transpose(ssm_b_im, (0, 2, 1))
    lbr, lbi, qr, qi, bbt_r, bbt_i = _ssm_prep(g3(ssm_a_re), g3(ssm_a_im), g3(ssm_log_dt), bt_re, bt_im)
    n_half = groups // HALF_G
    lam = jnp.stack([lbr.reshape(n_half, HALF_W), lbi.reshape(n_half, HALF_W)], axis=1)
    bbcat = _bb_blockdiag(bbt_r, bbt_i).astype(MXU_DTYPE)
    cccat = _cc_blockdiag(ssm_c_re, ssm_c_im).astype(MXU_DTYPE)

    h, ht = _norm_in(xp, row(norm_pre_g), tm)
    order = jnp.stack([jnp.bitwise_xor(me, r) for r in CONSUME_ORDER]).astype(jnp.int32)
    proj, win_g, (convw_g,) = _fwd_in(h, w_in.astype(MXU_DTYPE), order, _Comm([conv_w8]), tm)
    conv_w_full = jnp.transpose(convw_g, (1, 0, 2)).reshape(8, d_conv)
    u_col0, zs_col0 = 4 * d_conv, 4 * d_conv + d_ssm
    yconv = _conv_fwd(proj, conv_w_full, row(conv_b), d_conv)
    (yp,), (wout_g, wglu_g) = _ssm_fwd(proj, lam, bbcat, cccat, row(ssm_d), d_ssm, u_col0,
                                       _Comm([w_out.astype(MXU_DTYPE), w_glu.astype(MXU_DTYPE)]))
    w_out_full = wout_g.reshape(-1, d_model)
    w_glu_full = wglu_g.reshape(-1, d_ssm)
    (dy, d_o, mixt, dyc, dyp, dzs, ygt, dq, loss_part, dgpost, dbglu) = _tail(
        xp, tp, proj, yconv, yp, w_glu_full, row(b_glu), w_out_full, row(norm_post_g), zs_col0, min(256, seq))

    r_out, r_glu, nc = w_out.shape[0], w_glu.shape[0], w_in.shape[1]
    (dwout_p,), _ = _wgrad("dw_out", mixt, d_o, r_out, min(1024, d_model), (N_DEV, r_out, d_model),
                           (None, r_out, min(1024, d_model)), lambda i, j: (i, 0, j))
    (dwglu_p,), _ = _wgrad("dw_glu", ygt, dq, r_glu, d_ssm, (N_DEV, r_glu, d_ssm),
                           (None, r_glu, d_ssm), lambda i, j: (i, 0, 0))
    d4, dconvb, dconvw = _conv_bwd(proj, dyc, conv_w_full, row(conv_b), d_conv)
    late = [k for k in range(N_DEV) if k * nc < u_col0 + d_ssm and (k + 1) * nc > u_col0]
    early = [k for k in range(N_DEV) if k not in late]
    gr = math.gcd(nc, d_conv)
    granules = lambda blocks: [g for k in blocks for g in range(k * nc // gr, (k + 1) * nc // gr)]
    tmw = min(1024, d_model)
    (dwin_e,), (recv_out, recv_glu) = _dw_in("dw_in_early", ht, d4, None, dzs, granules(early), gr, nc, tmw,
                                             _Comm([], [dwout_p, dwglu_p]))
    (du, dbb, dcc, da, dd), (recv_in,) = _ssm_bwd(
        proj, dyp, lam, bbcat, cccat, row(ssm_d), d_ssm, u_col0, _Comm([], [dwin_e], dests={0: early}))
    (dwin_l,), _ = _dw_in("dw_in_late", ht, d4, du, dzs, granules(late), gr, nc, tmw)
    da_n = jnp.transpose(da.reshape(n_half, 2, HALF_G, SSM_STATE), (1, 0, 2, 3)).reshape(2, groups, 1, states)
    parts = [dgpost, dconvb, dd, dbglu, dconvw[:3], da_n, loss_part]
    parts_mx = [_bb_diag(dbb), _cc_diag(dcc)]
    shapes, shapes_mx = [p.shape for p in parts], [p.shape for p in parts_mx]
    (gx_p, dgpre), (pack_g, pack_mx_g, recv_in) = _bwd_in(
        d4, du, dzs, gr, win_g, xp, dy, row(norm_pre_g),
        _Comm([_pack(parts), _pack(parts_mx, MXU_DTYPE)], [dwin_l], dests={2: late}, into={2: recv_in}),
        min(256, seq))
    (last_g,) = _exchange("reduce_last", [_pack([dgpre])], [])
    (g_gpost, g_convb, g_d, g_bglu, g_convw, g_da, loss_sum) = _unpack(_sum_slots("sum_pack", pack_g), shapes)
    (g_dbb, g_dcc) = _unpack(_sum_slots("sum_pack_mx", pack_mx_g), shapes_mx)
    (g_gpre,) = _unpack(_sum_slots("sum_last", last_g), [dgpre.shape])
    g_convw = lax.dynamic_slice(g_convw, (0, me * conv_w.shape[1]), conv_w.shape)

    tr = lambda a: jnp.transpose(a, (0, 2, 1))
    direct = [(g_gpre, row(norm_pre_g), row(m_norm_pre_g), row(v_norm_pre_g)),
              (g_convb, row(conv_b), row(m_conv_b), row(v_conv_b)),
              (g_d, row(ssm_d), row(m_ssm_d), row(v_ssm_d)),
              (g_bglu, row(b_glu), row(m_b_glu), row(v_b_glu)),
              (g_gpost, row(norm_post_g), row(m_norm_post_g), row(v_norm_post_g)),
              (g_convw, conv_w, m_conv_w, v_conv_w),
              (g_dcc[0], ssm_c_re, m_ssm_c_re, v_ssm_c_re),
              (-g_dcc[1], ssm_c_im, m_ssm_c_im, v_ssm_c_im)]
    ssm = dict(da_r=g_da[0], da_i=g_da[1], dbb_r=g_dbb[0], dbb_i=g_dbb[1], lr=g3(ssm_a_re), li=g3(ssm_a_im),
               ldt=g3(ssm_log_dt), bt_r=bt_re, bt_i=bt_im, lbr=lbr, lbi=lbi, qr=qr, qi=qi,
               w_a_re=g3(ssm_a_re), m_a_re=g3(m_ssm_a_re), v_a_re=g3(v_ssm_a_re),
               w_a_im=g3(ssm_a_im), m_a_im=g3(m_ssm_a_im), v_a_im=g3(v_ssm_a_im),
               w_log_dt=g3(ssm_log_dt), m_log_dt=g3(m_ssm_log_dt), v_log_dt=g3(v_ssm_log_dt),
               w_bt_re=bt_re, m_bt_re=tr(m_ssm_b_re), v_bt_re=tr(v_ssm_b_re),
               w_bt_im=bt_im, m_bt_im=tr(m_ssm_b_im), v_bt_im=tr(v_ssm_b_im))
    small = _small_update(direct, ssm)
    res = {}
    for name, quad, shape in zip(["norm_pre_g", "conv_b", "ssm_d", "b_glu", "norm_post_g", "conv_w", "ssm_c_re", "ssm_c_im"],
                                 small[:8], [norm_pre_g.shape, conv_b.shape, ssm_d.shape, b_glu.shape,
                                             norm_post_g.shape, conv_w.shape, ssm_c_re.shape, ssm_c_im.shape]):
        res[name] = tuple(a.reshape(shape) for a in quad)
    res["ssm_a_re"] = tuple(a.reshape(ssm_a_re.shape) for a in small[8])
    res["ssm_a_im"] = tuple(a.reshape(ssm_a_im.shape) for a in small[9])
    res["ssm_log_dt"] = tuple(a.reshape(ssm_log_dt.shape) for a in small[10])
    res["ssm_b_re"] = tuple(tr(a) for a in small[11])
    res["ssm_b_im"] = tuple(tr(a) for a in small[12])
    res["w_in"] = tuple(_adam_big("adam_w_in", recv_in, w_in, m_w_in, v_w_in, min(256, d_model)))
    res["w_out"] = tuple(_adam_big("adam_w_out", recv_out, w_out, m_w_out, v_w_out, min(128, r_out)))
    res["w_glu"] = tuple(_adam_big("adam_w_glu", recv_glu, w_glu, m_w_glu, v_w_glu, r_glu))

    order = ["norm_pre_g", "w_in", "conv_w", "conv_b", "ssm_a_re", "ssm_a_im", "ssm_log_dt", "ssm_b_re", "ssm_b_im",
             "ssm_c_re", "ssm_c_im", "ssm_d", "w_glu", "b_glu", "w_out", "norm_post_g"]
    loss = loss_sum[0, 0]
    grad_x = _unpermute_rows(gx_p)[None]
    return (loss, grad_x, *[res[n][0] for n in order], *[res[n][1] for n in order],
            *[res[n][2] for n in order], *[res[n][3] for n in order])
```

```python
import functools
import math

import jax
import jax.numpy as jnp
from jax import lax
from jax.experimental import pallas as pl
from jax.experimental.pallas import tpu as pltpu

F32 = jnp.float32
MXU_DTYPE = jnp.bfloat16
AXES = ("x", "y", "c")
N_DEV = 8
N_CHUNK = 8
LANES = 128
SSM_GROUP = 16
SSM_STATE = 64
HALF_CH = 64
HALF_G = HALF_CH // SSM_GROUP
HALF_W = HALF_G * SSM_STATE
EPS = 1e-6
ADAM_LR, ADAM_B1, ADAM_B2, ADAM_EPS, ADAM_WD, ADAM_STEP = 0.001, 0.9, 0.999, 1e-08, 0.01, 10
GELU_C = math.sqrt(2.0 / math.pi)
GELU_K = 0.044715
VMEM_LIMIT = 56 * 1024 * 1024


def _params(sem=None):
    return pltpu.CompilerParams(dimension_semantics=sem, vmem_limit_bytes=VMEM_LIMIT)


def _dot(a, b):
    return jnp.dot(a, b, preferred_element_type=F32)


def _dot_nt(a, b):
    return lax.dot_general(a, b, (((1,), (1,)), ((), ())), preferred_element_type=F32)


def _dot_tn(a, b):
    return lax.dot_general(a, b, (((0,), (0,)), ((), ())), preferred_element_type=F32)


def _sigmoid(z):
    return 1.0 / (1.0 + jnp.exp(-z))


def _flip(v, bit):
    return 1 - v if bit else v


def _peers():
    x, y, c = (lax.axis_index(a) for a in AXES)
    out = []
    for m in range(1, N_DEV):
        px, py, pc = _flip(x, (m >> 2) & 1), _flip(y, (m >> 1) & 1), _flip(c, m & 1)
        out.append((px, py, pc, 4 * px + 2 * py + pc))
    return out


class _Comm:
    def __init__(self, gathers=(), scatters=(), dests=None, into=None):
        self.n_g = len(gathers)
        self.operands = list(gathers) + list(scatters)
        self.n = len(self.operands)
        self.dests = dests or {}
        self.into = into or {}

    def out_shape(self):
        return [jax.ShapeDtypeStruct((N_DEV,) + a.shape if t < self.n_g else a.shape, a.dtype)
                for t, a in enumerate(self.operands)]

    def scratch(self):
        if not self.n:
            return []
        return [pltpu.SemaphoreType.DMA((self.n, N_DEV - 1)), pltpu.SemaphoreType.DMA((self.n, N_DEV - 1)),
                pltpu.SemaphoreType.DMA((self.n,))]

    def _copies(self, in_refs, out_refs, sems, arrivals):
        send_sems, recv_sems, local_sems = sems
        x, y, c = (lax.axis_index(a) for a in AXES)
        me = 4 * x + 2 * y + c

        def src(t, dev):
            return in_refs[t] if t < self.n_g else in_refs[t].at[dev]

        def member(t, dev):
            if t not in self.dests:
                return None
            return functools.reduce(jnp.logical_or, [dev == d for d in self.dests[t]])

        local = [(member(t, me), pltpu.make_async_copy(src(t, me), out_refs[t].at[me], local_sems.at[t]))
                 for t in range(self.n)]
        sends, recvs = [], []
        for t in range(self.n):
            for m, (px, py, pc, peer) in enumerate(_peers()):
                kw = dict(send_sem=send_sems.at[t, m], recv_sem=recv_sems.at[t, m],
                          device_id=(px, py, pc), device_id_type=pl.DeviceIdType.MESH)
                sends.append((member(t, peer), pltpu.make_async_remote_copy(
                    src_ref=src(t, peer), dst_ref=out_refs[t].at[me], **kw)))
                if arrivals:
                    recvs.append((member(t, me), pltpu.make_async_remote_copy(
                        src_ref=src(t, peer), dst_ref=out_refs[t].at[peer], **kw)))
        return local, sends, recvs

    @staticmethod
    def _do(cond, action):
        if cond is None:
            action()
        else:
            pl.when(cond)(action)

    def start(self, in_refs, out_refs, sems):
        local, sends, _ = self._copies(in_refs, out_refs, sems, arrivals=False)
        for cond, cp in local + sends:
            self._do(cond, cp.start)

    def finish(self, in_refs, out_refs, sems):
        local, sends, recvs = self._copies(in_refs, out_refs, sems, arrivals=True)
        for cond, cp in recvs:
            self._do(cond, cp.wait_recv)
        for cond, cp in sends:
            self._do(cond, cp.wait_send)
        for cond, cp in local:
            self._do(cond, cp.wait)


def _call(body, comm, *, name, grid, in_specs, out_specs, out_shape, operands, scratch_shapes=()):
    comm = comm or _Comm()
    n_in, n_out, n_scr, cn = len(in_specs), len(out_specs), len(scratch_shapes), comm.n
    landing = sorted(comm.into)
    aliases = {n_in + cn + q: n_out + t for q, t in enumerate(landing)}

    def wrapped(*refs):
        parts, o = [], 0
        for k in (n_in, cn, len(landing), n_out, cn, n_scr):
            parts.append(refs[o:o + k])
            o += k
        h_in, c_in, _, h_out, c_out, h_scr = parts
        sems = refs[o:]
        if cn:
            first = functools.reduce(jnp.logical_and, [pl.program_id(d) == 0 for d in range(len(grid))])

            @pl.when(first)
            def _():
                comm.start(c_in, c_out, sems)

        body(*h_in, *h_out, *h_scr)
        if cn:
            last = functools.reduce(jnp.logical_and, [pl.program_id(d) == grid[d] - 1 for d in range(len(grid))])

            @pl.when(last)
            def _():
                comm.finish(c_in, c_out, sems)

    any_ = pl.BlockSpec(memory_space=pl.ANY)
    res = pl.pallas_call(
        wrapped, name=name, grid=grid, in_specs=list(in_specs) + [any_] * (cn + len(landing)),
        out_specs=list(out_specs) + [any_] * cn,
        out_shape=list(out_shape) + comm.out_shape(), scratch_shapes=list(scratch_shapes) + comm.scratch(),
        input_output_aliases=aliases, compiler_params=_params(("arbitrary",) * len(grid)),
    )(*operands, *comm.operands, *[comm.into[t] for t in landing])
    return list(res[:n_out]), list(res[n_out:])


def _exchange(name, gathers, scatters):
    def body(tok_ref):
        tok_ref[...] = jnp.zeros_like(tok_ref)

    return _call(body, _Comm(gathers, scatters), name=name, grid=(1,), in_specs=[],
                 out_specs=[pl.BlockSpec((8, LANES), lambda i: (0, 0))],
                 out_shape=[jax.ShapeDtypeStruct((8, LANES), F32)], operands=[])[1]


def _ssm_prep(a_re, a_im, log_dt, bt_re, bt_im):
    def body(lr_ref, li_ref, ldt_ref, br_ref, bi_ref, lbr_ref, lbi_ref, qr_ref, qi_ref, bbr_ref, bbi_ref):
        lr, li = lr_ref[...], li_ref[...]
        dt = jnp.exp(ldt_ref[...])
        mag = jnp.exp(lr * dt)
        lbr, lbi = mag * jnp.cos(li * dt), mag * jnp.sin(li * dt)
        nr, ni = lbr - 1.0, lbi
        den = lr * lr + li * li
        qr = (nr * lr + ni * li) / den
        qi = (ni * lr - nr * li) / den
        br, bi = br_ref[...], bi_ref[...]
        lbr_ref[...], lbi_ref[...], qr_ref[...], qi_ref[...] = lbr, lbi, qr, qi
        bbr_ref[...] = qr * br - qi * bi
        bbi_ref[...] = qr * bi + qi * br

    s2 = jax.ShapeDtypeStruct(a_re.shape, F32)
    s3 = jax.ShapeDtypeStruct(bt_re.shape, F32)
    return pl.pallas_call(body, name="ssm_prep", out_shape=[s2, s2, s2, s2, s3, s3],
                          compiler_params=_params())(a_re, a_im, log_dt, bt_re, bt_im)


def _adam(w, g, m, v):
    m2 = ADAM_B1 * m + (1.0 - ADAM_B1) * g
    v2 = ADAM_B2 * v + (1.0 - ADAM_B2) * (g * g)
    m_hat = m2 / (1.0 - ADAM_B1 ** ADAM_STEP)
    v_hat = v2 / (1.0 - ADAM_B2 ** ADAM_STEP)
    delta = -ADAM_LR * (m_hat / (jnp.sqrt(v_hat) + ADAM_EPS) + ADAM_WD * w)
    return delta, m2, v2


def _small_update(direct, ssm):
    n_direct = len(direct)
    flat = [a for quad in direct for a in quad]
    names = ["da_r", "da_i", "dbb_r", "dbb_i", "lr", "li", "ldt", "bt_r", "bt_i", "lbr", "lbi", "qr", "qi"]
    flat += [ssm[k] for k in names]
    chain = ["a_re", "a_im", "log_dt", "bt_re", "bt_im"]
    for k in chain:
        flat += [ssm["w_" + k], ssm["m_" + k], ssm["v_" + k]]
    n_in = len(flat)

    def body(*refs):
        ins, outs = refs[:n_in], refs[n_in:]
        for p in range(n_direct):
            g, w, m, v = (r[...] for r in ins[4 * p:4 * p + 4])
            d, m2, v2 = _adam(w, g, m, v)
            outs[4 * p][...], outs[4 * p + 1][...], outs[4 * p + 2][...], outs[4 * p + 3][...] = g, d, m2, v2
        o = 4 * n_direct
        da_r, da_i, dbb_r, dbb_i, lr, li, ldt, bt_r, bt_i, lbr, lbi, qr, qi = (r[...] for r in ins[o:o + 13])
        dt = jnp.exp(ldt)
        g_br = qr * dbb_r + qi * dbb_i
        g_bi = qr * dbb_i - qi * dbb_r
        dq_r = jnp.sum(bt_r * dbb_r + bt_i * dbb_i, axis=1, keepdims=True)
        dq_i = jnp.sum(bt_r * dbb_i - bt_i * dbb_r, axis=1, keepdims=True)
        den = lr * lr + li * li
        cr, ci = lr / den, li / den
        gl_r = da_r + (cr * dq_r - ci * dq_i)
        gl_i = da_i + (cr * dq_i + ci * dq_r)
        w_r = qr * cr + qi * ci
        w_i = qi * cr - qr * ci
        g_lr = dt * (lbr * gl_r + lbi * gl_i) + (-w_r * dq_r - w_i * dq_i)
        g_li = dt * (lbr * gl_i - lbi * gl_r) + (-w_r * dq_i + w_i * dq_r)
        m_r = lr * lbr - li * lbi
        m_i = lr * lbi + li * lbr
        g_ldt = jnp.sum(m_r * gl_r + m_i * gl_i, axis=2, keepdims=True) * dt
        grads = [g_lr, g_li, g_ldt, g_br, g_bi]
        base_in, base_out = o + 13, 4 * n_direct
        for p, g in enumerate(grads):
            w, m, v = (r[...] for r in ins[base_in + 3 * p:base_in + 3 * p + 3])
            d, m2, v2 = _adam(w, g, m, v)
            q = base_out + 4 * p
            outs[q][...], outs[q + 1][...], outs[q + 2][...], outs[q + 3][...] = g, d, m2, v2

    out_shape = []
    for quad in direct:
        out_shape += [jax.ShapeDtypeStruct(quad[1].shape, F32)] * 4
    for k in chain:
        out_shape += [jax.ShapeDtypeStruct(ssm["w_" + k].shape, F32)] * 4
    res = pl.pallas_call(body, name="small_update", out_shape=out_shape, compiler_params=_params())(*flat)
    return [tuple(res[4 * p:4 * p + 4]) for p in range(n_direct + len(chain))]


def _sum_slots(name, pack):
    def body(p_ref, o_ref):
        acc = p_ref[0].astype(F32)
        for k in range(1, N_DEV):
            acc = acc + p_ref[k].astype(F32)
        o_ref[...] = acc

    return pl.pallas_call(body, name=name, out_shape=jax.ShapeDtypeStruct(pack.shape[1:], F32),
                          compiler_params=_params())(pack)


def _adam_big(name, recv, w, m, v, tr):
    _, rows, cols = recv.shape

    def body(r_ref, w_ref, m_ref, v_ref, g_ref, d_ref, m2_ref, v2_ref):
        g = r_ref[0].astype(F32)
        for k in range(1, N_DEV):
            g = g + r_ref[k].astype(F32)
        d, m2, v2 = _adam(w_ref[...], g, m_ref[...], v_ref[...])
        g_ref[...], d_ref[...], m2_ref[...], v2_ref[...] = g, d, m2, v2

    blk = pl.BlockSpec((tr, cols), lambda i: (i, 0))
    shp = jax.ShapeDtypeStruct((rows, cols), F32)
    return pl.pallas_call(
        body, name=name, grid=(rows // tr,),
        in_specs=[pl.BlockSpec((N_DEV, tr, cols), lambda i: (0, i, 0)), blk, blk, blk],
        out_specs=[blk] * 4, out_shape=[shp] * 4, compiler_params=_params(("parallel",)),
    )(recv, w, m, v)


def _chunk_block(tm, d):
    return pl.BlockSpec((N_CHUNK, tm // N_CHUNK, d), lambda i: (0, i, 0))


def _interleave(block):
    c, n, d = block.shape
    return pltpu.einshape("cjd->jcd", block).reshape(n * c, d)


def _norm_in(x3, g_pre, tm):
    _, steps, d_model = x3.shape
    seq = steps * N_CHUNK

    def body(x_ref, g_ref, xp_ref, h_ref, ht_ref):
        x = _interleave(x_ref[...])
        xp_ref[...] = x
        r = lax.rsqrt(jnp.mean(x * x, axis=-1, keepdims=True) + EPS)
        h = x * r * g_ref[...]
        h_ref[...] = h.astype(h_ref.dtype)
        ht_ref[...] = h.T.astype(ht_ref.dtype)

    rows = pl.BlockSpec((tm, d_model), lambda i: (i, 0))
    return pl.pallas_call(
        body, name="norm_in", grid=(seq // tm,),
        in_specs=[_chunk_block(tm, d_model), pl.BlockSpec((1, d_model), lambda i: (0, 0))],
        out_specs=[rows, rows, pl.BlockSpec((d_model, tm), lambda i: (0, i))],
        out_shape=[jax.ShapeDtypeStruct((seq, d_model), F32), jax.ShapeDtypeStruct((seq, d_model), MXU_DTYPE),
                   jax.ShapeDtypeStruct((d_model, seq), MXU_DTYPE)],
        compiler_params=_params(("parallel",)),
    )(x3, g_pre)


GATHER_ORDER = (0, 1, 4, 2, 6, 5, 3, 7)
CONSUME_ORDER = (0, 1, 4, 2, 5, 3, 6, 7)


def _fwd_in(h, w_shard, order, comm, tm):
    seq, d_model = h.shape
    nc = w_shard.shape[1]
    n_i = seq // tm
    cn = comm.n

    def body(order_ref, h_ref, w_hbm, *rest):
        c_in, rest = rest[:cn], rest[cn:]
        proj_ref, wing = rest[0], rest[1]
        c_out, rest = rest[2:2 + cn], rest[2 + cn:]
        wbuf, send_sems, recv_sems, own_sem, load_sems = rest[:5]
        c_sems = rest[5:]
        k, i = pl.program_id(0), pl.program_id(1)
        x, y, c = (lax.axis_index(a) for a in AXES)
        me = 4 * x + 2 * y + c

        def dev(rel):
            return _flip(x, (rel >> 2) & 1), _flip(y, (rel >> 1) & 1), _flip(c, rel & 1)

        def slot(rel):
            px, py, pc = dev(rel)
            return 4 * px + 2 * py + pc

        def remote(src, block, to_rel, sem):
            return pltpu.make_async_remote_copy(
                src_ref=src, dst_ref=wing.at[block], send_sem=send_sems.at[sem], recv_sem=recv_sems.at[sem],
                device_id=dev(to_rel), device_id_type=pl.DeviceIdType.MESH)

        own = pltpu.make_async_copy(w_hbm, wing.at[me], own_sem)
        first_hand = [remote(w_hbm, me, GATHER_ORDER[p], p - 1) for p in range(1, 5)]
        passed_on = [remote(wing.at[slot(GATHER_ORDER[p])], slot(GATHER_ORDER[p]), 1, p + 2) for p in range(2, 5)]

        def load(q):
            return pltpu.make_async_copy(wing.at[slot(CONSUME_ORDER[q])], wbuf.at[q % 2], load_sems.at[q % 2])

        def take(q):
            p = GATHER_ORDER.index(CONSUME_ORDER[q])
            if p == 0:
                own.wait()
            else:
                remote(w_hbm, slot(GATHER_ORDER[p]), GATHER_ORDER[p], p - 1).wait_recv()
            if 2 <= p <= 4:
                passed_on[p - 2].start()
            load(q).start()

        @pl.when((k == 0) & (i == 0))
        def _():
            own.start()
            for cp in first_hand:
                cp.start()
            comm.start(c_in, c_out, c_sems)
            take(0)

        for q in range(N_DEV):
            @pl.when((k == q) & (i == 0))
            def _():
                load(q).wait()

            if q + 1 < N_DEV:
                @pl.when((k == q) & (i == n_i - 1))
                def _():
                    take(q + 1)

        proj_ref[...] = _dot(h_ref[...], wbuf[k % 2])

        @pl.when((k == N_DEV - 1) & (i == n_i - 1))
        def _():
            for cp in first_hand + passed_on:
                cp.wait_send()
            comm.finish(c_in, c_out, c_sems)

    any_ = pl.BlockSpec(memory_space=pl.ANY)
    grid_spec = pltpu.PrefetchScalarGridSpec(
        num_scalar_prefetch=1, grid=(N_DEV, n_i),
        in_specs=[pl.BlockSpec((tm, d_model), lambda k, i, o: (i, 0)), any_] + [any_] * cn,
        out_specs=[pl.BlockSpec((tm, nc), lambda k, i, o: (i, o[k])), any_] + [any_] * cn,
        scratch_shapes=[pltpu.VMEM((2, d_model, nc), w_shard.dtype), pltpu.SemaphoreType.DMA((N_DEV - 1,)),
                        pltpu.SemaphoreType.DMA((N_DEV - 1,)), pltpu.SemaphoreType.DMA, pltpu.SemaphoreType.DMA((2,))]
        + comm.scratch())
    res = pl.pallas_call(
        body, name="fwd_in", grid_spec=grid_spec,
        out_shape=[jax.ShapeDtypeStruct((seq, N_DEV * nc), F32),
                   jax.ShapeDtypeStruct((N_DEV, d_model, nc), w_shard.dtype)] + comm.out_shape(),
        compiler_params=_params(("arbitrary", "arbitrary")),
    )(order, h, w_shard, *comm.operands)
    return res[0], res[1], list(res[2:])


def _shift_prev(a):
    n = a.shape[0]
    last = a[n - N_CHUNK:, :]
    row = lax.broadcasted_iota(jnp.int32, last.shape, 0)
    wrap = jnp.where(row == 0, 0.0, pltpu.roll(last, 1, axis=0))
    return jnp.concatenate([wrap, a[:n - N_CHUNK, :]], axis=0)


def _shift_next(a):
    first = a[:N_CHUNK, :]
    row = lax.broadcasted_iota(jnp.int32, first.shape, 0)
    wrap = jnp.where(row == N_CHUNK - 1, 0.0, pltpu.roll(first, N_CHUNK - 1, axis=0))
    return jnp.concatenate([a[N_CHUNK:, :], wrap], axis=0)


def _conv_specs(seq, d_conv):
    nblk = d_conv // LANES
    return [pl.BlockSpec((seq, LANES), functools.partial(lambda i, o: (0, o + i), o=q * nblk)) for q in range(4)]


def _conv_fwd(proj, conv_w8, conv_b, d_conv):
    seq = proj.shape[0]

    def body(bg_ref, cg_ref, v_ref, zc_ref, w_ref, b_ref, y_ref):
        cv = cg_ref[...] * v_ref[...]
        s1 = _shift_prev(cv)
        s2 = _shift_prev(s1)
        conv = b_ref[...] + w_ref[0:1, :] * s2 + w_ref[1:2, :] * s1 + w_ref[2:3, :] * cv
        z = zc_ref[...]
        y_ref[...] = bg_ref[...] * conv * (z * _sigmoid(z))

    col = pl.BlockSpec((seq, LANES), lambda i: (0, i))
    return pl.pallas_call(
        body, name="conv_fwd", grid=(d_conv // LANES,),
        in_specs=_conv_specs(seq, d_conv) + [pl.BlockSpec((8, LANES), lambda i: (0, i)), pl.BlockSpec((1, LANES), lambda i: (0, i))],
        out_specs=col, out_shape=jax.ShapeDtypeStruct((seq, d_conv), F32),
        compiler_params=_params(("parallel",)),
    )(proj, proj, proj, proj, conv_w8, conv_b)


def _conv_bwd(proj, dyc, conv_w8, conv_b, d_conv):
    seq = proj.shape[0]

    def body(bg_ref, cg_ref, v_ref, zc_ref, dy_ref, w_ref, b_ref, d4_ref, dcb_ref, dcw_ref):
        bg, cg, v, z = bg_ref[...], cg_ref[...], v_ref[...], zc_ref[...]
        w0, w1, w2 = w_ref[0:1, :], w_ref[1:2, :], w_ref[2:3, :]
        cv = cg * v
        s1 = _shift_prev(cv)
        s2 = _shift_prev(s1)
        conv = b_ref[...] + w0 * s2 + w1 * s1 + w2 * cv
        sig = _sigmoid(z)
        dy = dy_ref[...]
        g1 = dy * (z * sig)
        d_conv_ = g1 * bg
        d4_ref[0] = (g1 * conv).astype(d4_ref.dtype)
        d4_ref[3] = (dy * bg * conv * (sig * (1.0 + z * (1.0 - sig)))).astype(d4_ref.dtype)
        n1 = _shift_next(d_conv_)
        n2 = _shift_next(n1)
        d_cv = w2 * d_conv_ + w1 * n1 + w0 * n2
        d4_ref[1] = (d_cv * v).astype(d4_ref.dtype)
        d4_ref[2] = (d_cv * cg).astype(d4_ref.dtype)
        dcb_ref[...] = jnp.sum(d_conv_, axis=0, keepdims=True)
        rows = [jnp.sum(d_conv_ * s, axis=0, keepdims=True) for s in (s2, s1, cv)]
        dcw_ref[...] = jnp.concatenate(rows + [jnp.zeros((5, LANES), F32)], axis=0)

    col = pl.BlockSpec((seq, LANES), lambda i: (0, i))
    return pl.pallas_call(
        body, name="conv_bwd", grid=(d_conv // LANES,),
        in_specs=_conv_specs(seq, d_conv) + [col, pl.BlockSpec((8, LANES), lambda i: (0, i)), pl.BlockSpec((1, LANES), lambda i: (0, i))],
        out_specs=[pl.BlockSpec((4, seq, LANES), lambda i: (0, 0, i)), pl.BlockSpec((1, LANES), lambda i: (0, i)),
                   pl.BlockSpec((8, LANES), lambda i: (0, i))],
        out_shape=[jax.ShapeDtypeStruct((4, seq, d_conv), MXU_DTYPE), jax.ShapeDtypeStruct((1, d_conv), F32),
                   jax.ShapeDtypeStruct((8, d_conv), F32)],
        compiler_params=_params(("parallel",)),
    )(proj, proj, proj, proj, dyc, conv_w8, conv_b)


def _cmul(ar, ai, br, bi):
    return ar * br - ai * bi, ar * bi + ai * br


def _cpow(ar, ai, n):
    rr, ri = jnp.ones_like(ar), jnp.zeros_like(ai)
    while n:
        if n & 1:
            rr, ri = _cmul(rr, ri, ar, ai)
        n >>= 1
        if n:
            ar, ai = _cmul(ar, ai, ar, ai)
    return rr, ri


def _down(v, k):
    row = lax.broadcasted_iota(jnp.int32, v.shape, 0)
    return jnp.where(row >= k, pltpu.roll(v, k, axis=0), 0.0)


def _up(v, k):
    row = lax.broadcasted_iota(jnp.int32, v.shape, 0)
    return jnp.where(row < N_CHUNK - k, pltpu.roll(v, N_CHUNK - k, axis=0), 0.0)


def _chunk_carry(fr, fi, mr, mi, shift):
    vr, vi = shift(fr, 1), shift(fi, 1)
    for k in (1, 2, 4):
        pr, pi = _cmul(mr, mi, shift(vr, k), shift(vi, k))
        vr, vi = vr + pr, vi + pi
        mr, mi = _cmul(mr, mi, mr, mi)
    return vr, vi


def _tile(ref, j, width, part):
    return ref.at[pl.ds(pl.multiple_of(j * N_CHUNK, N_CHUNK), N_CHUNK), pl.ds(part * width, width)]


def _row(t, k):
    return jnp.broadcast_to(t[k:k + 1, :], t.shape)


def _power_table(tab_ref, ar, ai, steps, width):
    e = lax.broadcasted_iota(jnp.int32, ar.shape, 0) + 1
    rr, ri = jnp.ones_like(ar), jnp.zeros_like(ai)
    br, bi = ar, ai
    for bit in range(4):
        mr, mi = _cmul(rr, ri, br, bi)
        take = ((e >> bit) & 1) == 1
        rr, ri = jnp.where(take, mr, rr), jnp.where(take, mi, ri)
        if bit < 3:
            br, bi = _cmul(br, bi, br, bi)
    _tile(tab_ref, 0, width, 0)[...] = rr
    _tile(tab_ref, 0, width, 1)[...] = ri

    def step(m, carry):
        tr, ti = _cmul(carry[0], carry[1], br, bi)
        _tile(tab_ref, m, width, 0)[...] = tr
        _tile(tab_ref, m, width, 1)[...] = ti
        return tr, ti

    lax.fori_loop(1, steps // N_CHUNK, step, (rr, ri))


def _last_power(tab_ref, steps, width):
    shape = (N_CHUNK, width)
    return (jnp.broadcast_to(tab_ref[steps - 1:steps, 0:width], shape),
            jnp.broadcast_to(tab_ref[steps - 1:steps, width:2 * width], shape))


def _scan_fwd(s_ref, ar, ai, steps, width):
    def step(j, carry):
        sr, si = carry
        nr = ar * sr - ai * si + _tile(s_ref, j, width, 0)[...]
        ni = ar * si + ai * sr + _tile(s_ref, j, width, 1)[...]
        _tile(s_ref, j, width, 0)[...] = nr
        _tile(s_ref, j, width, 1)[...] = ni
        return nr, ni

    z = jnp.zeros((N_CHUNK, width), F32)
    return lax.fori_loop(0, steps, step, (z, z), unroll=4)


def _scan_both(s_ref, g_ref, ar, ai, steps, width):
    def step(q, carry):
        sr, si, gr, gi = carry
        j, jb = q, steps - 1 - q
        nsr = ar * sr - ai * si + _tile(s_ref, j, width, 0)[...]
        nsi = ar * si + ai * sr + _tile(s_ref, j, width, 1)[...]
        ngr = ar * gr + ai * gi + _tile(g_ref, jb, width, 0)[...]
        ngi = ar * gi - ai * gr + _tile(g_ref, jb, width, 1)[...]
        _tile(s_ref, j, width, 0)[...] = nsr
        _tile(s_ref, j, width, 1)[...] = nsi
        _tile(g_ref, jb, width, 0)[...] = ngr
        _tile(g_ref, jb, width, 1)[...] = ngi
        return nsr, nsi, ngr, ngi

    z = jnp.zeros((N_CHUNK, width), F32)
    return lax.fori_loop(0, steps, step, (z, z, z, z), unroll=2)


def _patch_fwd(s_ref, tab_ref, cr, ci, steps, width):
    def tile(m, _):
        tr, ti = _tile(tab_ref, m, width, 0)[...], _tile(tab_ref, m, width, 1)[...]
        for k in range(N_CHUNK):
            fr, fi = _cmul(_row(tr, k), _row(ti, k), cr, ci)
            j = m * N_CHUNK + k
            _tile(s_ref, j, width, 0)[...] += fr
            _tile(s_ref, j, width, 1)[...] += fi
        return 0

    lax.fori_loop(0, steps // N_CHUNK, tile, 0)


def _lam_rows(lam_ref, hh, width):
    return (jnp.broadcast_to(lam_ref[hh, 0:1, :], (N_CHUNK, width)),
            jnp.broadcast_to(lam_ref[hh, 1:2, :], (N_CHUNK, width)))


def _ssm_specs(seq, col0):
    return dict(
        col=pl.BlockSpec((seq, LANES), lambda i: (0, col0 + i)),
        lam=pl.BlockSpec((2, 2, HALF_W), lambda i: (i, 0, 0)),
        bb=pl.BlockSpec((2, HALF_CH, 2 * HALF_W), lambda i: (i, 0, 0)),
        cc=pl.BlockSpec((2, 2 * HALF_W, HALF_CH), lambda i: (i, 0, 0)),
        vec=pl.BlockSpec((1, LANES), lambda i: (0, i)),
        out=pl.BlockSpec((seq, LANES), lambda i: (0, i)),
    )


def _ssm_fwd(proj, lam, bbcat, cccat, d_skip, d_ssm, u_col0, comm=None):
    seq = proj.shape[0]
    steps = seq // N_CHUNK

    def body(u_ref, lam_ref, bb_ref, cc_ref, d_ref, yp_ref, s_ref, tab_ref):
        for hh in range(2):
            lanes = slice(HALF_CH * hh, HALF_CH * (hh + 1))
            u_half = u_ref[:, lanes]
            ar, ai = _lam_rows(lam_ref, hh, HALF_W)
            _power_table(tab_ref, ar, ai, steps, HALF_W)
            s_ref[...] = _dot(u_half.astype(MXU_DTYPE), bb_ref[hh])
            fr, fi = _scan_fwd(s_ref, ar, ai, steps, HALF_W)
            pr, pi = _last_power(tab_ref, steps, HALF_W)
            cr, ci = _chunk_carry(fr, fi, pr, pi, _down)
            _patch_fwd(s_ref, tab_ref, cr, ci, steps, HALF_W)
            y = _dot(s_ref[...].astype(MXU_DTYPE), cc_ref[hh])
            yp_ref[:, lanes] = y + d_ref[:, lanes] * u_half

    sp = _ssm_specs(seq, u_col0 // LANES)
    return _call(
        body, comm, name="ssm_fwd", grid=(d_ssm // LANES,),
        in_specs=[sp["col"], sp["lam"], sp["bb"], sp["cc"], sp["vec"]], out_specs=[sp["out"]],
        out_shape=[jax.ShapeDtypeStruct((seq, d_ssm), F32)],
        scratch_shapes=[pltpu.VMEM((seq, 2 * HALF_W), F32), pltpu.VMEM((steps, 2 * HALF_W), F32)],
        operands=[proj, lam, bbcat, cccat, d_skip])


def _ssm_bwd(proj, dyp, lam, bbcat, cccat, d_skip, d_ssm, u_col0, comm=None):
    seq = proj.shape[0]
    steps = seq // N_CHUNK
    n_half = 2 * d_ssm // LANES
    width = HALF_W

    def body(u_ref, dyp_ref, lam_ref, bb_ref, cc_ref, d_ref, du_ref, dbb_ref, dcc_ref, da_ref, dd_ref,
             s_ref, g_ref, tab_ref):
        n_tiles = steps // N_CHUNK
        for hh in range(2):
            lanes = slice(HALF_CH * hh, HALF_CH * (hh + 1))
            u_half, dy_half = u_ref[:, lanes], dyp_ref[:, lanes]
            dy_mx = dy_half.astype(MXU_DTYPE)
            ar, ai = _lam_rows(lam_ref, hh, width)
            _power_table(tab_ref, ar, ai, steps, width)
            s_ref[...] = _dot(u_half.astype(MXU_DTYPE), bb_ref[hh])
            g_ref[...] = _dot_nt(dy_mx, cc_ref[hh])
            fr, fi, lr_, li_ = _scan_both(s_ref, g_ref, ar, ai, steps, width)
            pr, pi = _last_power(tab_ref, steps, width)
            cr, ci = _chunk_carry(fr, fi, pr, pi, _down)
            gr, gi = _chunk_carry(lr_, li_, pr, -pi, _up)

            def tile(m, carry):
                sr, si, accr, acci = carry
                t1r, t1i = _tile(tab_ref, m, width, 0)[...], _tile(tab_ref, m, width, 1)[...]
                mb = n_tiles - 1 - m
                t2r, t2i = _tile(tab_ref, mb, width, 0)[...], _tile(tab_ref, mb, width, 1)[...]
                for k in range(N_CHUNK):
                    j = m * N_CHUNK + k
                    xr, xi = _cmul(_row(t1r, k), _row(t1i, k), cr, ci)
                    nsr = _tile(s_ref, j, width, 0)[...] + xr
                    nsi = _tile(s_ref, j, width, 1)[...] + xi
                    _tile(s_ref, j, width, 0)[...] = nsr
                    _tile(s_ref, j, width, 1)[...] = nsi
                    qr, qi = _row(t2r, N_CHUNK - 1 - k), _row(t2i, N_CHUNK - 1 - k)
                    ngr = _tile(g_ref, j, width, 0)[...] + (qr * gr + qi * gi)
                    ngi = _tile(g_ref, j, width, 1)[...] + (qr * gi - qi * gr)
                    _tile(g_ref, j, width, 0)[...] = ngr
                    _tile(g_ref, j, width, 1)[...] = ngi
                    accr = accr + (sr * ngr + si * ngi)
                    acci = acci + (sr * ngi - si * ngr)
                    sr, si = nsr, nsi
                return sr, si, accr, acci

            z = jnp.zeros((N_CHUNK, width), F32)
            _, _, accr, acci = lax.fori_loop(0, n_tiles, tile, (cr, ci, z, z))
            da_ref[hh, :, 0:width] = jnp.sum(accr, axis=0, keepdims=True)
            da_ref[hh, :, width:2 * width] = jnp.sum(acci, axis=0, keepdims=True)

            g_mx = g_ref[...].astype(MXU_DTYPE)
            dcc_ref[hh] = _dot_tn(dy_mx, s_ref[...].astype(MXU_DTYPE)).T
            dbb_ref[hh] = _dot_tn(u_half.astype(MXU_DTYPE), g_mx)
            du = _dot_nt(g_mx, bb_ref[hh]) + d_ref[:, lanes] * dy_half
            du_ref[:, lanes] = du.astype(du_ref.dtype)
            dd_ref[:, lanes] = jnp.sum(dy_half * u_half, axis=0, keepdims=True)

    sp = _ssm_specs(seq, u_col0 // LANES)
    return _call(
        body, comm, name="ssm_bwd", grid=(d_ssm // LANES,),
        in_specs=[sp["col"], sp["out"], sp["lam"], sp["bb"], sp["cc"], sp["vec"]],
        out_specs=[sp["out"], sp["bb"], sp["cc"], pl.BlockSpec((2, 1, 2 * width), lambda i: (i, 0, 0)), sp["vec"]],
        out_shape=[jax.ShapeDtypeStruct((seq, d_ssm), MXU_DTYPE),
                   jax.ShapeDtypeStruct((n_half, HALF_CH, 2 * width), F32),
                   jax.ShapeDtypeStruct((n_half, 2 * width, HALF_CH), F32),
                   jax.ShapeDtypeStruct((n_half, 1, 2 * width), F32),
                   jax.ShapeDtypeStruct((1, d_ssm), F32)],
        scratch_shapes=[pltpu.VMEM((seq, 2 * width), F32), pltpu.VMEM((seq, 2 * width), F32),
                        pltpu.VMEM((steps, 2 * width), F32)],
        operands=[proj, dyp, lam, bbcat, cccat, d_skip])


def _tail(xp, t3, proj, yconv, yp, w_glu, b_glu, w_out, g_post, zs_col0, tm):
    seq, d_model = xp.shape
    d_conv, d_ssm = yconv.shape[1], yp.shape[1]
    d_mix = d_conv + d_ssm
    assert zs_col0 % d_ssm == 0

    def body(x_ref, t_ref, zs_ref, yc_ref, yp_ref, wglu_hbm, bglu_ref, wout_hbm, gpost_ref,
             dy_ref, do_ref, mixt_ref, dyc_ref, dyp_ref, dzs_ref, ygt_ref, dq_ref, loss_ref, dgpost_ref, dbglu_ref,
             wglu, wout):
        @pl.when(pl.program_id(0) == 0)
        def _():
            pltpu.sync_copy(wglu_hbm, wglu)
            pltpu.sync_copy(wout_hbm, wout)
            loss_ref[...] = jnp.zeros_like(loss_ref)
            dgpost_ref[...] = jnp.zeros_like(dgpost_ref)
            dbglu_ref[...] = jnp.zeros_like(dbglu_ref)

        a = yp_ref[...]
        th = jnp.tanh(GELU_C * (a + GELU_K * (a * a * a)))
        yg = a * (0.5 * (1.0 + th))
        dgelu = 0.5 * (1.0 + th) + 0.5 * a * (1.0 - th * th) * (GELU_C * (1.0 + 3.0 * GELU_K * a * a))
        yg_mx = yg.astype(MXU_DTYPE)
        sq = _sigmoid(_dot(yg_mx, wglu[...]) + bglu_ref[...])
        y2 = yg * sq
        zs = zs_ref[...]
        sz = _sigmoid(zs)
        silz = zs * sz
        yc, ys = yc_ref[...], y2 * silz
        mix = jnp.concatenate([yc, ys], axis=1).astype(MXU_DTYPE)
        mixt_ref[0:d_conv, :] = yc.T.astype(MXU_DTYPE)
        mixt_ref[d_conv:, :] = ys.T.astype(MXU_DTYPE)
        o = _dot(mix, wout[...])
        r2 = lax.rsqrt(jnp.mean(o * o, axis=-1, keepdims=True) + EPS)
        on = o * r2
        gpost = gpost_ref[...]
        err = (x_ref[...] + on * gpost) - _interleave(t_ref[...])
        loss_ref[...] += 0.5 * jnp.sum(jnp.mean(err * err, axis=-1, keepdims=True), axis=0, keepdims=True)
        dy = err * (1.0 / d_model)
        dy_ref[...] = dy
        dgpost_ref[...] += jnp.sum(dy * on, axis=0, keepdims=True)
        d_on = dy * gpost
        d_o = r2 * (d_on - on * jnp.mean(d_on * on, axis=-1, keepdims=True))
        do_mx = d_o.astype(MXU_DTYPE)
        do_ref[...] = do_mx
        d_mix_ = _dot_nt(do_mx, wout[...])
        dyc_ref[...] = d_mix_[:, :d_conv]
        d_yssm = d_mix_[:, d_conv:]
        d_y2 = d_yssm * silz
        dzs_ref[...] = (d_yssm * y2 * (sz * (1.0 + zs * (1.0 - sz)))).astype(dzs_ref.dtype)
        d_q = d_y2 * yg * (sq * (1.0 - sq))
        dq_mx = d_q.astype(MXU_DTYPE)
        dq_ref[...] = dq_mx
        ygt_ref[...] = yg.T.astype(MXU_DTYPE)
        dbglu_ref[...] += jnp.sum(d_q, axis=0, keepdims=True)
        d_yg = d_y2 * sq + _dot_nt(dq_mx, wglu[...])
        dyp_ref[...] = d_yg * dgelu

    def rows(width, col=0):
        return pl.BlockSpec((tm, width), lambda i: (i, col))

    def fixed(width):
        return pl.BlockSpec((1, width), lambda i: (0, 0))

    def cols(height):
        return pl.BlockSpec((height, tm), lambda i: (0, i))

    any_ = pl.BlockSpec(memory_space=pl.ANY)
    return pl.pallas_call(
        body, name="tail", grid=(seq // tm,),
        in_specs=[rows(d_model), _chunk_block(tm, d_model), rows(d_ssm, zs_col0 // d_ssm), rows(d_conv), rows(d_ssm),
                  any_, fixed(d_ssm), any_, fixed(d_model)],
        out_specs=[rows(d_model), rows(d_model), cols(d_mix), rows(d_conv), rows(d_ssm), rows(d_ssm), cols(d_ssm),
                   rows(d_ssm), fixed(LANES), fixed(d_model), fixed(d_ssm)],
        out_shape=[jax.ShapeDtypeStruct((seq, d_model), F32), jax.ShapeDtypeStruct((seq, d_model), MXU_DTYPE),
                   jax.ShapeDtypeStruct((d_mix, seq), MXU_DTYPE), jax.ShapeDtypeStruct((seq, d_conv), F32),
                   jax.ShapeDtypeStruct((seq, d_ssm), F32), jax.ShapeDtypeStruct((seq, d_ssm), MXU_DTYPE),
                   jax.ShapeDtypeStruct((d_ssm, seq), MXU_DTYPE), jax.ShapeDtypeStruct((seq, d_ssm), MXU_DTYPE),
                   jax.ShapeDtypeStruct((1, LANES), F32), jax.ShapeDtypeStruct((1, d_model), F32),
                   jax.ShapeDtypeStruct((1, d_ssm), F32)],
        scratch_shapes=[pltpu.VMEM(w_glu.shape, MXU_DTYPE), pltpu.VMEM(w_out.shape, MXU_DTYPE)],
        compiler_params=_params(("arbitrary",)),
    )(xp, t3, proj, yconv, yp, w_glu, b_glu, w_out, g_post)


def _bwd_in(d4, du, dzs, gr, win_g, xp, dy, g_pre, comm, tm):
    seq, d_model = xp.shape
    nb, _, nc = win_g.shape
    per = d4.shape[2] // gr

    def body(d4_ref, du_ref, dzs_ref, w_hbm, x_ref, dy_ref, g_ref, gx_ref, dg_ref, w_all, w_sems):
        def granule(g):
            p, cols = g // per, slice(g % per * gr, (g % per + 1) * gr)
            if p < 4:
                return d4_ref[p, :, cols]
            return du_ref[:, cols] if p == 4 else dzs_ref[:, cols]

        i = pl.program_id(0)
        loads = [pltpu.make_async_copy(w_hbm.at[k], w_all.at[k], w_sems.at[k]) for k in range(nb)]

        @pl.when(i == 0)
        def _():
            dg_ref[...] = jnp.zeros_like(dg_ref)
            for cp in loads:
                cp.start()

        dh = None
        for k in range(nb):
            @pl.when(i == 0)
            def _():
                loads[k].wait()

            dp = jnp.concatenate([granule(g) for g in range(k * nc // gr, (k + 1) * nc // gr)], axis=1)
            part = _dot_nt(dp, w_all[k])
            dh = part if dh is None else dh + part

        x = x_ref[...]
        r = lax.rsqrt(jnp.mean(x * x, axis=-1, keepdims=True) + EPS)
        xn = x * r
        dg_ref[...] += jnp.sum(dh * xn, axis=0, keepdims=True)
        dxn = dh * g_ref[...]
        gx_ref[...] = r * (dxn - xn * jnp.mean(dxn * xn, axis=-1, keepdims=True)) + dy_ref[...]

    row = pl.BlockSpec((tm, d_model), lambda i: (i, 0))
    vec = pl.BlockSpec((1, d_model), lambda i: (0, 0))
    return _call(
        body, comm, name="bwd_in", grid=(seq // tm,),
        in_specs=[pl.BlockSpec((4, tm, d4.shape[2]), lambda i: (0, i, 0)),
                  pl.BlockSpec((tm, du.shape[1]), lambda i: (i, 0)), pl.BlockSpec((tm, dzs.shape[1]), lambda i: (i, 0)),
                  pl.BlockSpec(memory_space=pl.ANY), row, row, vec],
        out_specs=[row, vec],
        out_shape=[jax.ShapeDtypeStruct((seq, d_model), F32), jax.ShapeDtypeStruct((1, d_model), F32)],
        scratch_shapes=[pltpu.VMEM(win_g.shape, win_g.dtype), pltpu.SemaphoreType.DMA((nb,))],
        operands=[d4, du, dzs, win_g, xp, dy, g_pre])


def _lookup(g, table):
    out = jnp.int32(table[0])
    for gi in range(1, len(table)):
        if table[gi] != table[gi - 1]:
            out = jnp.where(g >= gi, jnp.int32(table[gi]), out)
    return out


def _held(values, used):
    cur = next(v for v, u in zip(values, used) if u)
    out = []
    for v, u in zip(values, used):
        cur = v if u else cur
        out.append(cur)
    return out


def _dw_in(name, ht, d4, du, dzs, granules, gr, nc, tm, comm=None):
    d_model, seq = ht.shape
    per = d4.shape[2] // gr
    piece, col = [g // per for g in granules], [g % per for g in granules]
    sources = [(d4, [p < 4 for p in piece]), (du, [p == 4 for p in piece]), (dzs, [p == 5 for p in piece])]
    sources = [(a, used) for a, used in sources if any(used)]
    select = [next(s for s, (_, used) in enumerate(sources) if used[q]) for q in range(len(granules))]
    owner, place = [g * gr // nc for g in granules], [g * gr % nc // gr for g in granules]

    def body(a_ref, *refs):
        src_refs, o_ref = refs[:-1], refs[-1]
        j = pl.program_id(0)
        for s, ref in enumerate(src_refs):
            @pl.when(_lookup(j, select) == s)
            def _():
                o_ref[...] = _dot(a_ref[...], ref[...]).astype(o_ref.dtype)

    in_specs = [pl.BlockSpec((tm, seq), lambda j, i: (i, 0))]
    for a, used in sources:
        cols = _held(col, used)
        if a.ndim == 3:
            rows = _held(piece, used)
            in_specs.append(pl.BlockSpec((None, seq, gr), functools.partial(
                lambda j, i, rows, cols: (_lookup(j, rows), 0, _lookup(j, cols)), rows=rows, cols=cols)))
        else:
            in_specs.append(pl.BlockSpec((seq, gr), functools.partial(
                lambda j, i, cols: (0, _lookup(j, cols)), cols=cols)))
    return _call(
        body, comm, name=name, grid=(len(granules), d_model // tm), in_specs=in_specs,
        out_specs=[pl.BlockSpec((None, tm, gr), lambda j, i: (_lookup(j, owner), i, _lookup(j, place)))],
        out_shape=[jax.ShapeDtypeStruct((N_DEV, d_model, nc), MXU_DTYPE)],
        operands=[ht] + [a for a, _ in sources])


def _wgrad(name, at, b, tm, tn, out_shape, out_block, out_index, comm=None):
    m, seq = at.shape
    n = b.shape[1]

    def body(a_ref, b_ref, o_ref):
        o_ref[...] = _dot(a_ref[...], b_ref[...]).astype(o_ref.dtype)

    return _call(
        body, comm, name=name, grid=(n // tn, m // tm),
        in_specs=[pl.BlockSpec((tm, seq), lambda j, i: (i, 0)), pl.BlockSpec((seq, tn), lambda j, i: (0, j))],
        out_specs=[pl.BlockSpec(out_block, lambda j, i: out_index(i, j))],
        out_shape=[jax.ShapeDtypeStruct(out_shape, MXU_DTYPE)],
        operands=[at, b])


def _eye_g():
    return jnp.eye(HALF_G, dtype=F32)


def _bb_blockdiag(bbt_r, bbt_i):
    n_half = bbt_r.shape[0] // HALF_G

    def one(t):
        t = t.reshape(n_half, HALF_G, SSM_GROUP, SSM_STATE)
        t = t[:, :, :, None, :] * _eye_g()[None, :, None, :, None]
        return t.reshape(n_half, HALF_CH, HALF_W)

    return jnp.concatenate([one(bbt_r), one(bbt_i)], axis=-1)


def _cc_blockdiag(c_re, c_im):
    n_half = c_re.shape[0] // HALF_G

    def one(t):
        t = t.reshape(n_half, HALF_G, SSM_GROUP, SSM_STATE)
        t = jnp.transpose(t, (0, 3, 1, 2))
        t = t[:, None, :, :, :] * _eye_g()[None, :, None, :, None]
        return t.reshape(n_half, HALF_W, HALF_CH)

    return jnp.concatenate([one(c_re), one(-c_im)], axis=1)


def _bb_diag(dbb):
    n_half = dbb.shape[0]
    t = dbb.reshape(n_half, HALF_G, SSM_GROUP, 2, HALF_G, SSM_STATE)
    t = jnp.sum(t * _eye_g()[None, :, None, None, :, None], axis=4)
    t = jnp.transpose(t, (3, 0, 1, 2, 4))
    return t.reshape(2, n_half * HALF_G, SSM_GROUP, SSM_STATE)


def _cc_diag(dcc):
    n_half = dcc.shape[0]
    t = dcc.reshape(n_half, 2, HALF_G, SSM_STATE, HALF_G, SSM_GROUP)
    t = jnp.sum(t * _eye_g()[None, None, :, None, :, None], axis=2)
    t = jnp.transpose(t, (1, 0, 3, 4, 2))
    return t.reshape(2, n_half * HALF_G, SSM_GROUP, SSM_STATE)


def _permute_rows(a):
    seq, d = a.shape
    return a.reshape(N_CHUNK, seq // N_CHUNK, d).transpose(1, 0, 2).reshape(seq, d)


def _unpermute_rows(a):
    seq, d = a.shape
    return a.reshape(seq // N_CHUNK, N_CHUNK, d).transpose(1, 0, 2).reshape(seq, d)


def _pack_rows(shape):
    return -(-math.prod(shape) // (8 * LANES)) * 8


def _pack(parts, dtype=F32):
    rows = []
    for p in parts:
        flat = p.reshape(-1).astype(dtype)
        rows.append(jnp.pad(flat, (0, _pack_rows(p.shape) * LANES - flat.shape[0])).reshape(-1, LANES))
    return jnp.concatenate(rows, axis=0)


def _unpack(packed, shapes):
    out, o = [], 0
    for s in shapes:
        n = _pack_rows(s)
        out.append(packed[o:o + n].reshape(-1)[:math.prod(s)].reshape(s))
        o += n
    return out


def kernel(x, norm_pre_g, w_in, conv_w, conv_b, ssm_a_re, ssm_a_im, ssm_log_dt, ssm_b_re, ssm_b_im, ssm_c_re, ssm_c_im, ssm_d, w_glu, b_glu, w_out, norm_post_g, loss_target, m_norm_pre_g, m_w_in, m_conv_w, m_conv_b, m_ssm_a_re, m_ssm_a_im, m_ssm_log_dt, m_ssm_b_re, m_ssm_b_im, m_ssm_c_re, m_ssm_c_im, m_ssm_d, m_w_glu, m_b_glu, m_w_out, m_norm_post_g, v_norm_pre_g, v_w_in, v_conv_w, v_conv_b, v_ssm_a_re, v_ssm_a_im, v_ssm_log_dt, v_ssm_b_re, v_ssm_b_im, v_ssm_c_re, v_ssm_c_im, v_ssm_d, v_w_glu, v_b_glu, v_w_out, v_norm_post_g):
    seq, d_model = x.shape[1], x.shape[2]
    d_conv, d_ssm = conv_b.shape[0], ssm_d.shape[0]
    groups, states = ssm_a_re.shape
    assert x.shape[0] == 1 and seq % (8 * N_CHUNK) == 0 and d_conv == d_ssm
    assert (groups, states) == (d_ssm // SSM_GROUP, SSM_STATE) and d_ssm % LANES == 0
    me = 4 * lax.axis_index("x") + 2 * lax.axis_index("y") + lax.axis_index("c")
    tm = min(512, seq)

    x3 = x[0].reshape(N_CHUNK, seq // N_CHUNK, d_model)
    t3 = loss_target[0].reshape(N_CHUNK, seq // N_CHUNK, d_model)
    row = lambda a: a.reshape(1, -1)
    conv_w8 = jnp.pad(conv_w, ((0, 8 - conv_w.shape[0]), (0, 0)))

    g3 = lambda a: a.reshape(groups, 1, -1)
    bt_re, bt_im = jnp.transpose(ssm_b_re, (0, 2, 1)), jnp.transpose(ssm_b_im, (0, 2, 1))
    lbr, lbi, qr, qi, bbt_r, bbt_i = _ssm_prep(g3(ssm_a_re), g3(ssm_a_im), g3(ssm_log_dt), bt_re, bt_im)
    n_half = groups // HALF_G
    lam = jnp.stack([lbr.reshape(n_half, HALF_W), lbi.reshape(n_half, HALF_W)], axis=1)
    bbcat = _bb_blockdiag(bbt_r, bbt_i).astype(MXU_DTYPE)
    cccat = _cc_blockdiag(ssm_c_re, ssm_c_im).astype(MXU_DTYPE)

    xp, h, ht = _norm_in(x3, row(norm_pre_g), tm)
    order = jnp.stack([jnp.bitwise_xor(me, r) for r in CONSUME_ORDER]).astype(jnp.int32)
    proj, win_g, (convw_g,) = _fwd_in(h, w_in.astype(MXU_DTYPE), order, _Comm([conv_w8]), tm)
    conv_w_full = jnp.transpose(convw_g, (1, 0, 2)).reshape(8, d_conv)
    u_col0, zs_col0 = 4 * d_conv, 4 * d_conv + d_ssm
    yconv = _conv_fwd(proj, conv_w_full, row(conv_b), d_conv)
    (yp,), (wout_g, wglu_g) = _ssm_fwd(proj, lam, bbcat, cccat, row(ssm_d), d_ssm, u_col0,
                                       _Comm([w_out.astype(MXU_DTYPE), w_glu.astype(MXU_DTYPE)]))
    w_out_full = wout_g.reshape(-1, d_model)
    w_glu_full = wglu_g.reshape(-1, d_ssm)
    (dy, d_o, mixt, dyc, dyp, dzs, ygt, dq, loss_part, dgpost, dbglu) = _tail(
        xp, t3, proj, yconv, yp, w_glu_full, row(b_glu), w_out_full, row(norm_post_g), zs_col0, min(256, seq))

    r_out, r_glu, nc = w_out.shape[0], w_glu.shape[0], w_in.shape[1]
    (dwout_p,), _ = _wgrad("dw_out", mixt, d_o, r_out, min(1024, d_model), (N_DEV, r_out, d_model),
                           (None, r_out, min(1024, d_model)), lambda i, j: (i, 0, j))
    (dwglu_p,), _ = _wgrad("dw_glu", ygt, dq, r_glu, d_ssm, (N_DEV, r_glu, d_ssm),
                           (None, r_glu, d_ssm), lambda i, j: (i, 0, 0))
    d4, dconvb, dconvw = _conv_bwd(proj, dyc, conv_w_full, row(conv_b), d_conv)
    late = [k for k in range(N_DEV) if k * nc < u_col0 + d_ssm and (k + 1) * nc > u_col0]
    early = [k for k in range(N_DEV) if k not in late]
    gr = math.gcd(nc, d_conv)
    granules = lambda blocks: [g for k in blocks for g in range(k * nc // gr, (k + 1) * nc // gr)]
    tmw = min(1024, d_model)
    (dwin_e,), (recv_out, recv_glu) = _dw_in("dw_in_early", ht, d4, None, dzs, granules(early), gr, nc, tmw,
                                             _Comm([], [dwout_p, dwglu_p]))
    (du, dbb, dcc, da, dd), (recv_in,) = _ssm_bwd(
        proj, dyp, lam, bbcat, cccat, row(ssm_d), d_ssm, u_col0, _Comm([], [dwin_e], dests={0: early}))
    (dwin_l,), _ = _dw_in("dw_in_late", ht, d4, du, dzs, granules(late), gr, nc, tmw)
    da_n = jnp.transpose(da.reshape(n_half, 2, HALF_G, SSM_STATE), (1, 0, 2, 3)).reshape(2, groups, 1, states)
    parts = [dgpost, dconvb, dd, dbglu, dconvw[:3], da_n, loss_part]
    parts_mx = [_bb_diag(dbb), _cc_diag(dcc)]
    shapes, shapes_mx = [p.shape for p in parts], [p.shape for p in parts_mx]
    (gx_p, dgpre), (pack_g, pack_mx_g, recv_in) = _bwd_in(
        d4, du, dzs, gr, win_g, xp, dy, row(norm_pre_g),
        _Comm([_pack(parts), _pack(parts_mx, MXU_DTYPE)], [dwin_l], dests={2: late}, into={2: recv_in}),
        min(256, seq))
    (last_g,) = _exchange("reduce_last", [_pack([dgpre])], [])
    (g_gpost, g_convb, g_d, g_bglu, g_convw, g_da, loss_sum) = _unpack(_sum_slots("sum_pack", pack_g), shapes)
    (g_dbb, g_dcc) = _unpack(_sum_slots("sum_pack_mx", pack_mx_g), shapes_mx)
    (g_gpre,) = _unpack(_sum_slots("sum_last", last_g), [dgpre.shape])
    g_convw = lax.dynamic_slice(g_convw, (0, me * conv_w.shape[1]), conv_w.shape)

    tr = lambda a: jnp.transpose(a, (0, 2, 1))
    direct = [(g_gpre, row(norm_pre_g), row(m_norm_pre_g), row(v_norm_pre_g)),
              (g_convb, row(conv_b), row(m_conv_b), row(v_conv_b)),
              (g_d, row(ssm_d), row(m_ssm_d), row(v_ssm_d)),
              (g_bglu, row(b_glu), row(m_b_glu), row(v_b_glu)),
              (g_gpost, row(norm_post_g), row(m_norm_post_g), row(v_norm_post_g)),
              (g_convw, conv_w, m_conv_w, v_conv_w),
              (g_dcc[0], ssm_c_re, m_ssm_c_re, v_ssm_c_re),
              (-g_dcc[1], ssm_c_im, m_ssm_c_im, v_ssm_c_im)]
    ssm = dict(da_r=g_da[0], da_i=g_da[1], dbb_r=g_dbb[0], dbb_i=g_dbb[1], lr=g3(ssm_a_re), li=g3(ssm_a_im),
               ldt=g3(ssm_log_dt), bt_r=bt_re, bt_i=bt_im, lbr=lbr, lbi=lbi, qr=qr, qi=qi,
               w_a_re=g3(ssm_a_re), m_a_re=g3(m_ssm_a_re), v_a_re=g3(v_ssm_a_re),
               w_a_im=g3(ssm_a_im), m_a_im=g3(m_ssm_a_im), v_a_im=g3(v_ssm_a_im),
               w_log_dt=g3(ssm_log_dt), m_log_dt=g3(m_ssm_log_dt), v_log_dt=g3(v_ssm_log_dt),
               w_bt_re=bt_re, m_bt_re=tr(m_ssm_b_re), v_bt_re=tr(v_ssm_b_re),
               w_bt_im=bt_im, m_bt_im=tr(m_ssm_b_im), v_bt_im=tr(v_ssm_b_im))
    small = _small_update(direct, ssm)
    res = {}
    for name, quad, shape in zip(["norm_pre_g", "conv_b", "ssm_d", "b_glu", "norm_post_g", "conv_w", "ssm_c_re", "ssm_c_im"],
                                 small[:8], [norm_pre_g.shape, conv_b.shape, ssm_d.shape, b_glu.shape,
                                             norm_post_g.shape, conv_w.shape, ssm_c_re.shape, ssm_c_im.shape]):
        res[name] = tuple(a.reshape(shape) for a in quad)
    res["ssm_a_re"] = tuple(a.reshape(ssm_a_re.shape) for a in small[8])
    res["ssm_a_im"] = tuple(a.reshape(ssm_a_im.shape) for a in small[9])
    res["ssm_log_dt"] = tuple(a.reshape(ssm_log_dt.shape) for a in small[10])
    res["ssm_b_re"] = tuple(tr(a) for a in small[11])
    res["ssm_b_im"] = tuple(tr(a) for a in small[12])
    res["w_in"] = tuple(_adam_big("adam_w_in", recv_in, w_in, m_w_in, v_w_in, min(256, d_model)))
    res["w_out"] = tuple(_adam_big("adam_w_out", recv_out, w_out, m_w_out, v_w_out, min(128, r_out)))
    res["w_glu"] = tuple(_adam_big("adam_w_glu", recv_glu, w_glu, m_w_glu, v_w_glu, r_glu))

    order = ["norm_pre_g", "w_in", "conv_w", "conv_b", "ssm_a_re", "ssm_a_im", "ssm_log_dt", "ssm_b_re", "ssm_b_im",
             "ssm_c_re", "ssm_c_im", "ssm_d", "w_glu", "b_glu", "w_out", "norm_post_g"]
    loss = loss_sum[0, 0]
    grad_x = _unpermute_rows(gx_p)[None]
    return (loss, grad_x, *[res[n][0] for n in order], *[res[n][1] for n in order],
            *[res[n][2] for n in order], *[res[n][3] for n in order])
```

```python
import functools
import math

import jax
import jax.numpy as jnp
from jax import lax
from jax.experimental import pallas as pl
from jax.experimental.pallas import tpu as pltpu

F32 = jnp.float32
MXU_DTYPE = jnp.bfloat16
AXES = ("x", "y", "c")
N_DEV = 8
N_CHUNK = 8
LANES = 128
SSM_GROUP = 16
SSM_STATE = 64
HALF_CH = 64
HALF_G = HALF_CH // SSM_GROUP
HALF_W = HALF_G * SSM_STATE
EPS = 1e-6
ADAM_LR, ADAM_B1, ADAM_B2, ADAM_EPS, ADAM_WD, ADAM_STEP = 0.001, 0.9, 0.999, 1e-08, 0.01, 10
GELU_C = math.sqrt(2.0 / math.pi)
GELU_K = 0.044715
VMEM_LIMIT = 56 * 1024 * 1024


def _params(sem=None):
    return pltpu.CompilerParams(dimension_semantics=sem, vmem_limit_bytes=VMEM_LIMIT)


def _dot(a, b):
    return jnp.dot(a, b, preferred_element_type=F32)


def _dot_nt(a, b):
    return lax.dot_general(a, b, (((1,), (1,)), ((), ())), preferred_element_type=F32)


def _dot_tn(a, b):
    return lax.dot_general(a, b, (((0,), (0,)), ((), ())), preferred_element_type=F32)


def _sigmoid(z):
    return 1.0 / (1.0 + jnp.exp(-z))


def _flip(v, bit):
    return 1 - v if bit else v


def _peers():
    x, y, c = (lax.axis_index(a) for a in AXES)
    out = []
    for m in range(1, N_DEV):
        px, py, pc = _flip(x, (m >> 2) & 1), _flip(y, (m >> 1) & 1), _flip(c, m & 1)
        out.append((px, py, pc, 4 * px + 2 * py + pc))
    return out


class _Comm:
    def __init__(self, gathers=(), scatters=(), dests=None, into=None):
        self.n_g = len(gathers)
        self.operands = list(gathers) + list(scatters)
        self.n = len(self.operands)
        self.dests = dests or {}
        self.into = into or {}

    def out_shape(self):
        return [jax.ShapeDtypeStruct((N_DEV,) + a.shape if t < self.n_g else a.shape, a.dtype)
                for t, a in enumerate(self.operands)]

    def scratch(self):
        if not self.n:
            return []
        return [pltpu.SemaphoreType.DMA((self.n, N_DEV - 1)), pltpu.SemaphoreType.DMA((self.n, N_DEV - 1)),
                pltpu.SemaphoreType.DMA((self.n,))]

    def _copies(self, in_refs, out_refs, sems, arrivals):
        send_sems, recv_sems, local_sems = sems
        x, y, c = (lax.axis_index(a) for a in AXES)
        me = 4 * x + 2 * y + c

        def src(t, dev):
            return in_refs[t] if t < self.n_g else in_refs[t].at[dev]

        def member(t, dev):
            if t not in self.dests:
                return None
            return functools.reduce(jnp.logical_or, [dev == d for d in self.dests[t]])

        local = [(member(t, me), pltpu.make_async_copy(src(t, me), out_refs[t].at[me], local_sems.at[t]))
                 for t in range(self.n)]
        sends, recvs = [], []
        for t in range(self.n):
            for m, (px, py, pc, peer) in enumerate(_peers()):
                kw = dict(send_sem=send_sems.at[t, m], recv_sem=recv_sems.at[t, m],
                          device_id=(px, py, pc), device_id_type=pl.DeviceIdType.MESH)
                sends.append((member(t, peer), pltpu.make_async_remote_copy(
                    src_ref=src(t, peer), dst_ref=out_refs[t].at[me], **kw)))
                if arrivals:
                    recvs.append((member(t, me), pltpu.make_async_remote_copy(
                        src_ref=src(t, peer), dst_ref=out_refs[t].at[peer], **kw)))
        return local, sends, recvs

    @staticmethod
    def _do(cond, action):
        if cond is None:
            action()
        else:
            pl.when(cond)(action)

    def start(self, in_refs, out_refs, sems):
        local, sends, _ = self._copies(in_refs, out_refs, sems, arrivals=False)
        for cond, cp in local + sends:
            self._do(cond, cp.start)

    def finish(self, in_refs, out_refs, sems):
        local, sends, recvs = self._copies(in_refs, out_refs, sems, arrivals=True)
        for cond, cp in recvs:
            self._do(cond, cp.wait_recv)
        for cond, cp in sends:
            self._do(cond, cp.wait_send)
        for cond, cp in local:
            self._do(cond, cp.wait)


def _call(body, comm, *, name, grid, in_specs, out_specs, out_shape, operands, scratch_shapes=()):
    comm = comm or _Comm()
    n_in, n_out, n_scr, cn = len(in_specs), len(out_specs), len(scratch_shapes), comm.n
    landing = sorted(comm.into)
    aliases = {n_in + cn + q: n_out + t for q, t in enumerate(landing)}

    def wrapped(*refs):
        parts, o = [], 0
        for k in (n_in, cn, len(landing), n_out, cn, n_scr):
            parts.append(refs[o:o + k])
            o += k
        h_in, c_in, _, h_out, c_out, h_scr = parts
        sems = refs[o:]
        if cn:
            first = functools.reduce(jnp.logical_and, [pl.program_id(d) == 0 for d in range(len(grid))])

            @pl.when(first)
            def _():
                comm.start(c_in, c_out, sems)

        body(*h_in, *h_out, *h_scr)
        if cn:
            last = functools.reduce(jnp.logical_and, [pl.program_id(d) == grid[d] - 1 for d in range(len(grid))])

            @pl.when(last)
            def _():
                comm.finish(c_in, c_out, sems)

    any_ = pl.BlockSpec(memory_space=pl.ANY)
    res = pl.pallas_call(
        wrapped, name=name, grid=grid, in_specs=list(in_specs) + [any_] * (cn + len(landing)),
        out_specs=list(out_specs) + [any_] * cn,
        out_shape=list(out_shape) + comm.out_shape(), scratch_shapes=list(scratch_shapes) + comm.scratch(),
        input_output_aliases=aliases, compiler_params=_params(("arbitrary",) * len(grid)),
    )(*operands, *comm.operands, *[comm.into[t] for t in landing])
    return list(res[:n_out]), list(res[n_out:])


def _exchange(name, gathers, scatters):
    def body(tok_ref):
        tok_ref[...] = jnp.zeros_like(tok_ref)

    return _call(body, _Comm(gathers, scatters), name=name, grid=(1,), in_specs=[],
                 out_specs=[pl.BlockSpec((8, LANES), lambda i: (0, 0))],
                 out_shape=[jax.ShapeDtypeStruct((8, LANES), F32)], operands=[])[1]


def _ssm_prep(a_re, a_im, log_dt, bt_re, bt_im):
    def body(lr_ref, li_ref, ldt_ref, br_ref, bi_ref, lbr_ref, lbi_ref, qr_ref, qi_ref, bbr_ref, bbi_ref):
        lr, li = lr_ref[...], li_ref[...]
        dt = jnp.exp(ldt_ref[...])
        mag = jnp.exp(lr * dt)
        lbr, lbi = mag * jnp.cos(li * dt), mag * jnp.sin(li * dt)
        nr, ni = lbr - 1.0, lbi
        den = lr * lr + li * li
        qr = (nr * lr + ni * li) / den
        qi = (ni * lr - nr * li) / den
        br, bi = br_ref[...], bi_ref[...]
        lbr_ref[...], lbi_ref[...], qr_ref[...], qi_ref[...] = lbr, lbi, qr, qi
        bbr_ref[...] = qr * br - qi * bi
        bbi_ref[...] = qr * bi + qi * br

    s2 = jax.ShapeDtypeStruct(a_re.shape, F32)
    s3 = jax.ShapeDtypeStruct(bt_re.shape, F32)
    return pl.pallas_call(body, name="ssm_prep", out_shape=[s2, s2, s2, s2, s3, s3],
                          compiler_params=_params())(a_re, a_im, log_dt, bt_re, bt_im)


def _adam(w, g, m, v):
    m2 = ADAM_B1 * m + (1.0 - ADAM_B1) * g
    v2 = ADAM_B2 * v + (1.0 - ADAM_B2) * (g * g)
    m_hat = m2 / (1.0 - ADAM_B1 ** ADAM_STEP)
    v_hat = v2 / (1.0 - ADAM_B2 ** ADAM_STEP)
    delta = -ADAM_LR * (m_hat / (jnp.sqrt(v_hat) + ADAM_EPS) + ADAM_WD * w)
    return delta, m2, v2


def _small_update(direct, ssm):
    n_direct = len(direct)
    flat = [a for quad in direct for a in quad]
    names = ["da_r", "da_i", "dbb_r", "dbb_i", "lr", "li", "ldt", "bt_r", "bt_i", "lbr", "lbi", "qr", "qi"]
    flat += [ssm[k] for k in names]
    chain = ["a_re", "a_im", "log_dt", "bt_re", "bt_im"]
    for k in chain:
        flat += [ssm["w_" + k], ssm["m_" + k], ssm["v_" + k]]
    n_in = len(flat)

    def body(*refs):
        ins, outs = refs[:n_in], refs[n_in:]
        for p in range(n_direct):
            g, w, m, v = (r[...] for r in ins[4 * p:4 * p + 4])
            d, m2, v2 = _adam(w, g, m, v)
            outs[4 * p][...], outs[4 * p + 1][...], outs[4 * p + 2][...], outs[4 * p + 3][...] = g, d, m2, v2
        o = 4 * n_direct
        da_r, da_i, dbb_r, dbb_i, lr, li, ldt, bt_r, bt_i, lbr, lbi, qr, qi = (r[...] for r in ins[o:o + 13])
        dt = jnp.exp(ldt)
        g_br = qr * dbb_r + qi * dbb_i
        g_bi = qr * dbb_i - qi * dbb_r
        dq_r = jnp.sum(bt_r * dbb_r + bt_i * dbb_i, axis=1, keepdims=True)
        dq_i = jnp.sum(bt_r * dbb_i - bt_i * dbb_r, axis=1, keepdims=True)
        den = lr * lr + li * li
        cr, ci = lr / den, li / den
        gl_r = da_r + (cr * dq_r - ci * dq_i)
        gl_i = da_i + (cr * dq_i + ci * dq_r)
        w_r = qr * cr + qi * ci
        w_i = qi * cr - qr * ci
        g_lr = dt * (lbr * gl_r + lbi * gl_i) + (-w_r * dq_r - w_i * dq_i)
        g_li = dt * (lbr * gl_i - lbi * gl_r) + (-w_r * dq_i + w_i * dq_r)
        m_r = lr * lbr - li * lbi
        m_i = lr * lbi + li * lbr
        g_ldt = jnp.sum(m_r * gl_r + m_i * gl_i, axis=2, keepdims=True) * dt
        grads = [g_lr, g_li, g_ldt, g_br, g_bi]
        base_in, base_out = o + 13, 4 * n_direct
        for p, g in enumerate(grads):
            w, m, v = (r[...] for r in ins[base_in + 3 * p:base_in + 3 * p + 3])
            d, m2, v2 = _adam(w, g, m, v)
            q = base_out + 4 * p
            outs[q][...], outs[q + 1][...], outs[q + 2][...], outs[q + 3][...] = g, d, m2, v2

    out_shape = []
    for quad in direct:
        out_shape += [jax.ShapeDtypeStruct(quad[1].shape, F32)] * 4
    for k in chain:
        out_shape += [jax.ShapeDtypeStruct(ssm["w_" + k].shape, F32)] * 4
    res = pl.pallas_call(body, name="small_update", out_shape=out_shape, compiler_params=_params())(*flat)
    return [tuple(res[4 * p:4 * p + 4]) for p in range(n_direct + len(chain))]


def _sum_slots(name, pack):
    def body(p_ref, o_ref):
        acc = p_ref[0].astype(F32)
        for k in range(1, N_DEV):
            acc = acc + p_ref[k].astype(F32)
        o_ref[...] = acc

    return pl.pallas_call(body, name=name, out_shape=jax.ShapeDtypeStruct(pack.shape[1:], F32),
                          compiler_params=_params())(pack)


def _adam_big(name, recv, w, m, v, tr):
    _, rows, cols = recv.shape

    def body(r_ref, w_ref, m_ref, v_ref, g_ref, d_ref, m2_ref, v2_ref):
        g = r_ref[0].astype(F32)
        for k in range(1, N_DEV):
            g = g + r_ref[k].astype(F32)
        d, m2, v2 = _adam(w_ref[...], g, m_ref[...], v_ref[...])
        g_ref[...], d_ref[...], m2_ref[...], v2_ref[...] = g, d, m2, v2

    blk = pl.BlockSpec((tr, cols), lambda i: (i, 0))
    shp = jax.ShapeDtypeStruct((rows, cols), F32)
    return pl.pallas_call(
        body, name=name, grid=(rows // tr,),
        in_specs=[pl.BlockSpec((N_DEV, tr, cols), lambda i: (0, i, 0)), blk, blk, blk],
        out_specs=[blk] * 4, out_shape=[shp] * 4, compiler_params=_params(("parallel",)),
    )(recv, w, m, v)


def _chunk_block(tm, d):
    return pl.BlockSpec((N_CHUNK, tm // N_CHUNK, d), lambda i: (0, i, 0))


def _interleave(block):
    c, n, d = block.shape
    return pltpu.einshape("cjd->jcd", block).reshape(n * c, d)


def _norm_in(x3, g_pre, tm):
    _, steps, d_model = x3.shape
    seq = steps * N_CHUNK

    def body(x_ref, g_ref, xp_ref, h_ref, ht_ref):
        x = _interleave(x_ref[...])
        xp_ref[...] = x
        r = lax.rsqrt(jnp.mean(x * x, axis=-1, keepdims=True) + EPS)
        h = x * r * g_ref[...]
        h_ref[...] = h.astype(h_ref.dtype)
        ht_ref[...] = h.T.astype(ht_ref.dtype)

    rows = pl.BlockSpec((tm, d_model), lambda i: (i, 0))
    return pl.pallas_call(
        body, name="norm_in", grid=(seq // tm,),
        in_specs=[_chunk_block(tm, d_model), pl.BlockSpec((1, d_model), lambda i: (0, 0))],
        out_specs=[rows, rows, pl.BlockSpec((d_model, tm), lambda i: (0, i))],
        out_shape=[jax.ShapeDtypeStruct((seq, d_model), F32), jax.ShapeDtypeStruct((seq, d_model), MXU_DTYPE),
                   jax.ShapeDtypeStruct((d_model, seq), MXU_DTYPE)],
        compiler_params=_params(("parallel",)),
    )(x3, g_pre)


GATHER_ORDER = (0, 1, 4, 2, 6, 5, 3, 7)
CONSUME_ORDER = (0, 1, 4, 2, 5, 3, 6, 7)


def _fwd_in(h, w_shard, order, comm, tm):
    seq, d_model = h.shape
    nc = w_shard.shape[1]
    n_i = seq // tm
    cn = comm.n

    def body(order_ref, h_ref, w_hbm, *rest):
        c_in, rest = rest[:cn], rest[cn:]
        proj_ref, wing = rest[0], rest[1]
        c_out, rest = rest[2:2 + cn], rest[2 + cn:]
        wbuf, send_sems, recv_sems, own_sem, load_sems = rest[:5]
        c_sems = rest[5:]
        k, i = pl.program_id(0), pl.program_id(1)
        x, y, c = (lax.axis_index(a) for a in AXES)
        me = 4 * x + 2 * y + c

        def dev(rel):
            return _flip(x, (rel >> 2) & 1), _flip(y, (rel >> 1) & 1), _flip(c, rel & 1)

        def slot(rel):
            px, py, pc = dev(rel)
            return 4 * px + 2 * py + pc

        def remote(src, block, to_rel, sem):
            return pltpu.make_async_remote_copy(
                src_ref=src, dst_ref=wing.at[block], send_sem=send_sems.at[sem], recv_sem=recv_sems.at[sem],
                device_id=dev(to_rel), device_id_type=pl.DeviceIdType.MESH)

        own = pltpu.make_async_copy(w_hbm, wing.at[me], own_sem)
        first_hand = [remote(w_hbm, me, GATHER_ORDER[p], p - 1) for p in range(1, 5)]
        passed_on = [remote(wing.at[slot(GATHER_ORDER[p])], slot(GATHER_ORDER[p]), 1, p + 2) for p in range(2, 5)]

        def load(q):
            return pltpu.make_async_copy(wing.at[slot(CONSUME_ORDER[q])], wbuf.at[q % 2], load_sems.at[q % 2])

        def take(q):
            p = GATHER_ORDER.index(CONSUME_ORDER[q])
            if p == 0:
                own.wait()
            else:
                remote(w_hbm, slot(GATHER_ORDER[p]), GATHER_ORDER[p], p - 1).wait_recv()
            if 2 <= p <= 4:
                passed_on[p - 2].start()
            load(q).start()

        @pl.when((k == 0) & (i == 0))
        def _():
            own.start()
            for cp in first_hand:
                cp.start()
            comm.start(c_in, c_out, c_sems)
            take(0)

        for q in range(N_DEV):
            @pl.when((k == q) & (i == 0))
            def _():
                load(q).wait()

            if q + 1 < N_DEV:
                @pl.when((k == q) & (i == n_i - 1))
                def _():
                    take(q + 1)

        proj_ref[...] = _dot(h_ref[...], wbuf[k % 2])

        @pl.when((k == N_DEV - 1) & (i == n_i - 1))
        def _():
            for cp in first_hand + passed_on:
                cp.wait_send()
            comm.finish(c_in, c_out, c_sems)

    any_ = pl.BlockSpec(memory_space=pl.ANY)
    grid_spec = pltpu.PrefetchScalarGridSpec(
        num_scalar_prefetch=1, grid=(N_DEV, n_i),
        in_specs=[pl.BlockSpec((tm, d_model), lambda k, i, o: (i, 0)), any_] + [any_] * cn,
        out_specs=[pl.BlockSpec((tm, nc), lambda k, i, o: (i, o[k])), any_] + [any_] * cn,
        scratch_shapes=[pltpu.VMEM((2, d_model, nc), w_shard.dtype), pltpu.SemaphoreType.DMA((N_DEV - 1,)),
                        pltpu.SemaphoreType.DMA((N_DEV - 1,)), pltpu.SemaphoreType.DMA, pltpu.SemaphoreType.DMA((2,))]
        + comm.scratch())
    res = pl.pallas_call(
        body, name="fwd_in", grid_spec=grid_spec,
        out_shape=[jax.ShapeDtypeStruct((seq, N_DEV * nc), F32),
                   jax.ShapeDtypeStruct((N_DEV, d_model, nc), w_shard.dtype)] + comm.out_shape(),
        compiler_params=_params(("arbitrary", "arbitrary")),
    )(order, h, w_shard, *comm.operands)
    return res[0], res[1], list(res[2:])


def _shift_prev(a):
    n = a.shape[0]
    last = a[n - N_CHUNK:, :]
    row = lax.broadcasted_iota(jnp.int32, last.shape, 0)
    wrap = jnp.where(row == 0, 0.0, pltpu.roll(last, 1, axis=0))
    return jnp.concatenate([wrap, a[:n - N_CHUNK, :]], axis=0)


def _shift_next(a):
    first = a[:N_CHUNK, :]
    row = lax.broadcasted_iota(jnp.int32, first.shape, 0)
    wrap = jnp.where(row == N_CHUNK - 1, 0.0, pltpu.roll(first, N_CHUNK - 1, axis=0))
    return jnp.concatenate([a[N_CHUNK:, :], wrap], axis=0)


def _conv_specs(seq, d_conv):
    nblk = d_conv // LANES
    return [pl.BlockSpec((seq, LANES), functools.partial(lambda i, o: (0, o + i), o=q * nblk)) for q in range(4)]


def _conv_fwd(proj, conv_w8, conv_b, d_conv):
    seq = proj.shape[0]

    def body(bg_ref, cg_ref, v_ref, zc_ref, w_ref, b_ref, y_ref):
        cv = cg_ref[...] * v_ref[...]
        s1 = _shift_prev(cv)
        s2 = _shift_prev(s1)
        conv = b_ref[...] + w_ref[0:1, :] * s2 + w_ref[1:2, :] * s1 + w_ref[2:3, :] * cv
        z = zc_ref[...]
        y_ref[...] = bg_ref[...] * conv * (z * _sigmoid(z))

    col = pl.BlockSpec((seq, LANES), lambda i: (0, i))
    return pl.pallas_call(
        body, name="conv_fwd", grid=(d_conv // LANES,),
        in_specs=_conv_specs(seq, d_conv) + [pl.BlockSpec((8, LANES), lambda i: (0, i)), pl.BlockSpec((1, LANES), lambda i: (0, i))],
        out_specs=col, out_shape=jax.ShapeDtypeStruct((seq, d_conv), F32),
        compiler_params=_params(("parallel",)),
    )(proj, proj, proj, proj, conv_w8, conv_b)


def _conv_bwd(proj, dyc, conv_w8, conv_b, d_conv, comm=None):
    seq = proj.shape[0]

    def body(bg_ref, cg_ref, v_ref, zc_ref, dy_ref, w_ref, b_ref, d4_ref, dcb_ref, dcw_ref):
        bg, cg, v, z = bg_ref[...], cg_ref[...], v_ref[...], zc_ref[...]
        w0, w1, w2 = w_ref[0:1, :], w_ref[1:2, :], w_ref[2:3, :]
        cv = cg * v
        s1 = _shift_prev(cv)
        s2 = _shift_prev(s1)
        conv = b_ref[...] + w0 * s2 + w1 * s1 + w2 * cv
        sig = _sigmoid(z)
        dy = dy_ref[...]
        g1 = dy * (z * sig)
        d_conv_ = g1 * bg
        d4_ref[0] = (g1 * conv).astype(d4_ref.dtype)
        d4_ref[3] = (dy * bg * conv * (sig * (1.0 + z * (1.0 - sig)))).astype(d4_ref.dtype)
        n1 = _shift_next(d_conv_)
        n2 = _shift_next(n1)
        d_cv = w2 * d_conv_ + w1 * n1 + w0 * n2
        d4_ref[1] = (d_cv * v).astype(d4_ref.dtype)
        d4_ref[2] = (d_cv * cg).astype(d4_ref.dtype)
        dcb_ref[...] = jnp.sum(d_conv_, axis=0, keepdims=True)
        rows = [jnp.sum(d_conv_ * s, axis=0, keepdims=True) for s in (s2, s1, cv)]
        dcw_ref[...] = jnp.concatenate(rows + [jnp.zeros((5, LANES), F32)], axis=0)

    col = pl.BlockSpec((seq, LANES), lambda i: (0, i))
    return _call(
        body, comm, name="conv_bwd", grid=(d_conv // LANES,),
        in_specs=_conv_specs(seq, d_conv) + [col, pl.BlockSpec((8, LANES), lambda i: (0, i)), pl.BlockSpec((1, LANES), lambda i: (0, i))],
        out_specs=[pl.BlockSpec((4, seq, LANES), lambda i: (0, 0, i)), pl.BlockSpec((1, LANES), lambda i: (0, i)),
                   pl.BlockSpec((8, LANES), lambda i: (0, i))],
        out_shape=[jax.ShapeDtypeStruct((4, seq, d_conv), MXU_DTYPE), jax.ShapeDtypeStruct((1, d_conv), F32),
                   jax.ShapeDtypeStruct((8, d_conv), F32)],
        operands=[proj, proj, proj, proj, dyc, conv_w8, conv_b])


def _cmul(ar, ai, br, bi):
    return ar * br - ai * bi, ar * bi + ai * br


def _cpow(ar, ai, n):
    rr, ri = jnp.ones_like(ar), jnp.zeros_like(ai)
    while n:
        if n & 1:
            rr, ri = _cmul(rr, ri, ar, ai)
        n >>= 1
        if n:
            ar, ai = _cmul(ar, ai, ar, ai)
    return rr, ri


def _down(v, k):
    row = lax.broadcasted_iota(jnp.int32, v.shape, 0)
    return jnp.where(row >= k, pltpu.roll(v, k, axis=0), 0.0)


def _up(v, k):
    row = lax.broadcasted_iota(jnp.int32, v.shape, 0)
    return jnp.where(row < N_CHUNK - k, pltpu.roll(v, N_CHUNK - k, axis=0), 0.0)


def _chunk_carry(fr, fi, mr, mi, shift):
    vr, vi = shift(fr, 1), shift(fi, 1)
    for k in (1, 2, 4):
        pr, pi = _cmul(mr, mi, shift(vr, k), shift(vi, k))
        vr, vi = vr + pr, vi + pi
        mr, mi = _cmul(mr, mi, mr, mi)
    return vr, vi


def _tile(ref, j, width, part):
    return ref.at[pl.ds(pl.multiple_of(j * N_CHUNK, N_CHUNK), N_CHUNK), pl.ds(part * width, width)]


def _row(t, k):
    return jnp.broadcast_to(t[k:k + 1, :], t.shape)


def _power_table(tab_ref, ar, ai, steps, width):
    e = lax.broadcasted_iota(jnp.int32, ar.shape, 0) + 1
    rr, ri = jnp.ones_like(ar), jnp.zeros_like(ai)
    br, bi = ar, ai
    for bit in range(4):
        mr, mi = _cmul(rr, ri, br, bi)
        take = ((e >> bit) & 1) == 1
        rr, ri = jnp.where(take, mr, rr), jnp.where(take, mi, ri)
        if bit < 3:
            br, bi = _cmul(br, bi, br, bi)
    _tile(tab_ref, 0, width, 0)[...] = rr
    _tile(tab_ref, 0, width, 1)[...] = ri

    def step(m, carry):
        tr, ti = _cmul(carry[0], carry[1], br, bi)
        _tile(tab_ref, m, width, 0)[...] = tr
        _tile(tab_ref, m, width, 1)[...] = ti
        return tr, ti

    lax.fori_loop(1, steps // N_CHUNK, step, (rr, ri))


def _last_power(tab_ref, steps, width):
    shape = (N_CHUNK, width)
    return (jnp.broadcast_to(tab_ref[steps - 1:steps, 0:width], shape),
            jnp.broadcast_to(tab_ref[steps - 1:steps, width:2 * width], shape))


def _scan_fwd(s_ref, ar, ai, steps, width):
    def step(j, carry):
        sr, si = carry
        nr = ar * sr - ai * si + _tile(s_ref, j, width, 0)[...]
        ni = ar * si + ai * sr + _tile(s_ref, j, width, 1)[...]
        _tile(s_ref, j, width, 0)[...] = nr
        _tile(s_ref, j, width, 1)[...] = ni
        return nr, ni

    z = jnp.zeros((N_CHUNK, width), F32)
    return lax.fori_loop(0, steps, step, (z, z), unroll=4)


def _scan_both(s_ref, g_ref, ar, ai, steps, width):
    def step(q, carry):
        sr, si, gr, gi = carry
        j, jb = q, steps - 1 - q
        nsr = ar * sr - ai * si + _tile(s_ref, j, width, 0)[...]
        nsi = ar * si + ai * sr + _tile(s_ref, j, width, 1)[...]
        ngr = ar * gr + ai * gi + _tile(g_ref, jb, width, 0)[...]
        ngi = ar * gi - ai * gr + _tile(g_ref, jb, width, 1)[...]
        _tile(s_ref, j, width, 0)[...] = nsr
        _tile(s_ref, j, width, 1)[...] = nsi
        _tile(g_ref, jb, width, 0)[...] = ngr
        _tile(g_ref, jb, width, 1)[...] = ngi
        return nsr, nsi, ngr, ngi

    z = jnp.zeros((N_CHUNK, width), F32)
    return lax.fori_loop(0, steps, step, (z, z, z, z), unroll=2)


def _patch_fwd(s_ref, tab_ref, cr, ci, steps, width):
    def tile(m, _):
        tr, ti = _tile(tab_ref, m, width, 0)[...], _tile(tab_ref, m, width, 1)[...]
        for k in range(N_CHUNK):
            fr, fi = _cmul(_row(tr, k), _row(ti, k), cr, ci)
            j = m * N_CHUNK + k
            _tile(s_ref, j, width, 0)[...] += fr
            _tile(s_ref, j, width, 1)[...] += fi
        return 0

    lax.fori_loop(0, steps // N_CHUNK, tile, 0)


def _lam_rows(lam_ref, hh, width):
    return (jnp.broadcast_to(lam_ref[hh, 0:1, :], (N_CHUNK, width)),
            jnp.broadcast_to(lam_ref[hh, 1:2, :], (N_CHUNK, width)))


def _ssm_specs(seq, col0):
    return dict(
        col=pl.BlockSpec((seq, LANES), lambda i: (0, col0 + i)),
        lam=pl.BlockSpec((2, 2, HALF_W), lambda i: (i, 0, 0)),
        bb=pl.BlockSpec((2, HALF_CH, 2 * HALF_W), lambda i: (i, 0, 0)),
        cc=pl.BlockSpec((2, 2 * HALF_W, HALF_CH), lambda i: (i, 0, 0)),
        vec=pl.BlockSpec((1, LANES), lambda i: (0, i)),
        out=pl.BlockSpec((seq, LANES), lambda i: (0, i)),
    )


def _ssm_fwd(proj, lam, bbcat, cccat, d_skip, d_ssm, u_col0, comm=None):
    seq = proj.shape[0]
    steps = seq // N_CHUNK

    def body(u_ref, lam_ref, bb_ref, cc_ref, d_ref, yp_ref, s_ref, tab_ref):
        for hh in range(2):
            lanes = slice(HALF_CH * hh, HALF_CH * (hh + 1))
            u_half = u_ref[:, lanes]
            ar, ai = _lam_rows(lam_ref, hh, HALF_W)
            _power_table(tab_ref, ar, ai, steps, HALF_W)
            s_ref[...] = _dot(u_half.astype(MXU_DTYPE), bb_ref[hh])
            fr, fi = _scan_fwd(s_ref, ar, ai, steps, HALF_W)
            pr, pi = _last_power(tab_ref, steps, HALF_W)
            cr, ci = _chunk_carry(fr, fi, pr, pi, _down)
            _patch_fwd(s_ref, tab_ref, cr, ci, steps, HALF_W)
            y = _dot(s_ref[...].astype(MXU_DTYPE), cc_ref[hh])
            yp_ref[:, lanes] = y + d_ref[:, lanes] * u_half

    sp = _ssm_specs(seq, u_col0 // LANES)
    return _call(
        body, comm, name="ssm_fwd", grid=(d_ssm // LANES,),
        in_specs=[sp["col"], sp["lam"], sp["bb"], sp["cc"], sp["vec"]], out_specs=[sp["out"]],
        out_shape=[jax.ShapeDtypeStruct((seq, d_ssm), F32)],
        scratch_shapes=[pltpu.VMEM((seq, 2 * HALF_W), F32), pltpu.VMEM((steps, 2 * HALF_W), F32)],
        operands=[proj, lam, bbcat, cccat, d_skip])


def _ssm_bwd(proj, dyp, lam, bbcat, cccat, d_skip, d_ssm, u_col0, comm=None):
    seq = proj.shape[0]
    steps = seq // N_CHUNK
    n_half = 2 * d_ssm // LANES
    width = HALF_W

    def body(u_ref, dyp_ref, lam_ref, bb_ref, cc_ref, d_ref, du_ref, dbb_ref, dcc_ref, da_ref, dd_ref,
             s_ref, g_ref, tab_ref):
        n_tiles = steps // N_CHUNK
        for hh in range(2):
            lanes = slice(HALF_CH * hh, HALF_CH * (hh + 1))
            u_half, dy_half = u_ref[:, lanes], dyp_ref[:, lanes]
            dy_mx = dy_half.astype(MXU_DTYPE)
            ar, ai = _lam_rows(lam_ref, hh, width)
            _power_table(tab_ref, ar, ai, steps, width)
            s_ref[...] = _dot(u_half.astype(MXU_DTYPE), bb_ref[hh])
            g_ref[...] = _dot_nt(dy_mx, cc_ref[hh])
            fr, fi, lr_, li_ = _scan_both(s_ref, g_ref, ar, ai, steps, width)
            pr, pi = _last_power(tab_ref, steps, width)
            cr, ci = _chunk_carry(fr, fi, pr, pi, _down)
            gr, gi = _chunk_carry(lr_, li_, pr, -pi, _up)

            def tile(m, carry):
                sr, si, accr, acci = carry
                t1r, t1i = _tile(tab_ref, m, width, 0)[...], _tile(tab_ref, m, width, 1)[...]
                mb = n_tiles - 1 - m
                t2r, t2i = _tile(tab_ref, mb, width, 0)[...], _tile(tab_ref, mb, width, 1)[...]
                for k in range(N_CHUNK):
                    j = m * N_CHUNK + k
                    xr, xi = _cmul(_row(t1r, k), _row(t1i, k), cr, ci)
                    nsr = _tile(s_ref, j, width, 0)[...] + xr
                    nsi = _tile(s_ref, j, width, 1)[...] + xi
                    _tile(s_ref, j, width, 0)[...] = nsr
                    _tile(s_ref, j, width, 1)[...] = nsi
                    qr, qi = _row(t2r, N_CHUNK - 1 - k), _row(t2i, N_CHUNK - 1 - k)
                    ngr = _tile(g_ref, j, width, 0)[...] + (qr * gr + qi * gi)
                    ngi = _tile(g_ref, j, width, 1)[...] + (qr * gi - qi * gr)
                    _tile(g_ref, j, width, 0)[...] = ngr
                    _tile(g_ref, j, width, 1)[...] = ngi
                    accr = accr + (sr * ngr + si * ngi)
                    acci = acci + (sr * ngi - si * ngr)
                    sr, si = nsr, nsi
                return sr, si, accr, acci

            z = jnp.zeros((N_CHUNK, width), F32)
            _, _, accr, acci = lax.fori_loop(0, n_tiles, tile, (cr, ci, z, z))
            da_ref[hh, :, 0:width] = jnp.sum(accr, axis=0, keepdims=True)
            da_ref[hh, :, width:2 * width] = jnp.sum(acci, axis=0, keepdims=True)

            g_mx = g_ref[...].astype(MXU_DTYPE)
            dcc_ref[hh] = _dot_tn(dy_mx, s_ref[...].astype(MXU_DTYPE)).T
            dbb_ref[hh] = _dot_tn(u_half.astype(MXU_DTYPE), g_mx)
            du = _dot_nt(g_mx, bb_ref[hh]) + d_ref[:, lanes] * dy_half
            du_ref[:, lanes] = du.astype(du_ref.dtype)
            dd_ref[:, lanes] = jnp.sum(dy_half * u_half, axis=0, keepdims=True)

    sp = _ssm_specs(seq, u_col0 // LANES)
    return _call(
        body, comm, name="ssm_bwd", grid=(d_ssm // LANES,),
        in_specs=[sp["col"], sp["out"], sp["lam"], sp["bb"], sp["cc"], sp["vec"]],
        out_specs=[sp["out"], sp["bb"], sp["cc"], pl.BlockSpec((2, 1, 2 * width), lambda i: (i, 0, 0)), sp["vec"]],
        out_shape=[jax.ShapeDtypeStruct((seq, d_ssm), MXU_DTYPE),
                   jax.ShapeDtypeStruct((n_half, HALF_CH, 2 * width), F32),
                   jax.ShapeDtypeStruct((n_half, 2 * width, HALF_CH), F32),
                   jax.ShapeDtypeStruct((n_half, 1, 2 * width), F32),
                   jax.ShapeDtypeStruct((1, d_ssm), F32)],
        scratch_shapes=[pltpu.VMEM((seq, 2 * width), F32), pltpu.VMEM((seq, 2 * width), F32),
                        pltpu.VMEM((steps, 2 * width), F32)],
        operands=[proj, dyp, lam, bbcat, cccat, d_skip])


def _tail(xp, t3, proj, yconv, yp, w_glu, b_glu, w_out, g_post, zs_col0, tm):
    seq, d_model = xp.shape
    d_conv, d_ssm = yconv.shape[1], yp.shape[1]
    d_mix = d_conv + d_ssm
    assert zs_col0 % d_ssm == 0

    def body(x_ref, t_ref, zs_ref, yc_ref, yp_ref, wglu_hbm, bglu_ref, wout_hbm, gpost_ref,
             dy_ref, do_ref, mixt_ref, dyc_ref, dyp_ref, dzs_ref, ygt_ref, dq_ref, loss_ref, dgpost_ref, dbglu_ref,
             wglu, wout):
        @pl.when(pl.program_id(0) == 0)
        def _():
            pltpu.sync_copy(wglu_hbm, wglu)
            pltpu.sync_copy(wout_hbm, wout)
            loss_ref[...] = jnp.zeros_like(loss_ref)
            dgpost_ref[...] = jnp.zeros_like(dgpost_ref)
            dbglu_ref[...] = jnp.zeros_like(dbglu_ref)

        a = yp_ref[...]
        th = jnp.tanh(GELU_C * (a + GELU_K * (a * a * a)))
        yg = a * (0.5 * (1.0 + th))
        dgelu = 0.5 * (1.0 + th) + 0.5 * a * (1.0 - th * th) * (GELU_C * (1.0 + 3.0 * GELU_K * a * a))
        yg_mx = yg.astype(MXU_DTYPE)
        sq = _sigmoid(_dot(yg_mx, wglu[...]) + bglu_ref[...])
        y2 = yg * sq
        zs = zs_ref[...]
        sz = _sigmoid(zs)
        silz = zs * sz
        yc, ys = yc_ref[...], y2 * silz
        mix = jnp.concatenate([yc, ys], axis=1).astype(MXU_DTYPE)
        mixt_ref[0:d_conv, :] = yc.T.astype(MXU_DTYPE)
        mixt_ref[d_conv:, :] = ys.T.astype(MXU_DTYPE)
        o = _dot(mix, wout[...])
        r2 = lax.rsqrt(jnp.mean(o * o, axis=-1, keepdims=True) + EPS)
        on = o * r2
        gpost = gpost_ref[...]
        err = (x_ref[...] + on * gpost) - _interleave(t_ref[...])
        loss_ref[...] += 0.5 * jnp.sum(jnp.mean(err * err, axis=-1, keepdims=True), axis=0, keepdims=True)
        dy = err * (1.0 / d_model)
        dy_ref[...] = dy
        dgpost_ref[...] += jnp.sum(dy * on, axis=0, keepdims=True)
        d_on = dy * gpost
        d_o = r2 * (d_on - on * jnp.mean(d_on * on, axis=-1, keepdims=True))
        do_mx = d_o.astype(MXU_DTYPE)
        do_ref[...] = do_mx
        d_mix_ = _dot_nt(do_mx, wout[...])
        dyc_ref[...] = d_mix_[:, :d_conv]
        d_yssm = d_mix_[:, d_conv:]
        d_y2 = d_yssm * silz
        dzs_ref[...] = (d_yssm * y2 * (sz * (1.0 + zs * (1.0 - sz)))).astype(dzs_ref.dtype)
        d_q = d_y2 * yg * (sq * (1.0 - sq))
        dq_mx = d_q.astype(MXU_DTYPE)
        dq_ref[...] = dq_mx
        ygt_ref[...] = yg.T.astype(MXU_DTYPE)
        dbglu_ref[...] += jnp.sum(d_q, axis=0, keepdims=True)
        d_yg = d_y2 * sq + _dot_nt(dq_mx, wglu[...])
        dyp_ref[...] = d_yg * dgelu

    def rows(width, col=0):
        return pl.BlockSpec((tm, width), lambda i: (i, col))

    def fixed(width):
        return pl.BlockSpec((1, width), lambda i: (0, 0))

    def cols(height):
        return pl.BlockSpec((height, tm), lambda i: (0, i))

    any_ = pl.BlockSpec(memory_space=pl.ANY)
    return pl.pallas_call(
        body, name="tail", grid=(seq // tm,),
        in_specs=[rows(d_model), _chunk_block(tm, d_model), rows(d_ssm, zs_col0 // d_ssm), rows(d_conv), rows(d_ssm),
                  any_, fixed(d_ssm), any_, fixed(d_model)],
        out_specs=[rows(d_model), rows(d_model), cols(d_mix), rows(d_conv), rows(d_ssm), rows(d_ssm), cols(d_ssm),
                   rows(d_ssm), fixed(LANES), fixed(d_model), fixed(d_ssm)],
        out_shape=[jax.ShapeDtypeStruct((seq, d_model), F32), jax.ShapeDtypeStruct((seq, d_model), MXU_DTYPE),
                   jax.ShapeDtypeStruct((d_mix, seq), MXU_DTYPE), jax.ShapeDtypeStruct((seq, d_conv), F32),
                   jax.ShapeDtypeStruct((seq, d_ssm), F32), jax.ShapeDtypeStruct((seq, d_ssm), MXU_DTYPE),
                   jax.ShapeDtypeStruct((d_ssm, seq), MXU_DTYPE), jax.ShapeDtypeStruct((seq, d_ssm), MXU_DTYPE),
                   jax.ShapeDtypeStruct((1, LANES), F32), jax.ShapeDtypeStruct((1, d_model), F32),
                   jax.ShapeDtypeStruct((1, d_ssm), F32)],
        scratch_shapes=[pltpu.VMEM(w_glu.shape, MXU_DTYPE), pltpu.VMEM(w_out.shape, MXU_DTYPE)],
        compiler_params=_params(("arbitrary",)),
    )(xp, t3, proj, yconv, yp, w_glu, b_glu, w_out, g_post)


def _bwd_in(d4, du, dzs, gr, win_g, xp, dy, g_pre, comm, tm):
    seq, d_model = xp.shape
    nb, _, nc = win_g.shape
    per = d4.shape[2] // gr

    def body(d4_ref, du_ref, dzs_ref, w_hbm, x_ref, dy_ref, g_ref, gx_ref, dg_ref, w_all, w_sems):
        def granule(g):
            p, cols = g // per, slice(g % per * gr, (g % per + 1) * gr)
            if p < 4:
                return d4_ref[p, :, cols]
            return du_ref[:, cols] if p == 4 else dzs_ref[:, cols]

        i = pl.program_id(0)
        loads = [pltpu.make_async_copy(w_hbm.at[k], w_all.at[k], w_sems.at[k]) for k in range(nb)]

        @pl.when(i == 0)
        def _():
            dg_ref[...] = jnp.zeros_like(dg_ref)
            for cp in loads:
                cp.start()

        dh = None
        for k in range(nb):
            @pl.when(i == 0)
            def _():
                loads[k].wait()

            dp = jnp.concatenate([granule(g) for g in range(k * nc // gr, (k + 1) * nc // gr)], axis=1)
            part = _dot_nt(dp, w_all[k])
            dh = part if dh is None else dh + part

        x = x_ref[...]
        r = lax.rsqrt(jnp.mean(x * x, axis=-1, keepdims=True) + EPS)
        xn = x * r
        dg_ref[...] += jnp.sum(dh * xn, axis=0, keepdims=True)
        dxn = dh * g_ref[...]
        gx_ref[...] = r * (dxn - xn * jnp.mean(dxn * xn, axis=-1, keepdims=True)) + dy_ref[...]

    row = pl.BlockSpec((tm, d_model), lambda i: (i, 0))
    vec = pl.BlockSpec((1, d_model), lambda i: (0, 0))
    return _call(
        body, comm, name="bwd_in", grid=(seq // tm,),
        in_specs=[pl.BlockSpec((4, tm, d4.shape[2]), lambda i: (0, i, 0)),
                  pl.BlockSpec((tm, du.shape[1]), lambda i: (i, 0)), pl.BlockSpec((tm, dzs.shape[1]), lambda i: (i, 0)),
                  pl.BlockSpec(memory_space=pl.ANY), row, row, vec],
        out_specs=[row, vec],
        out_shape=[jax.ShapeDtypeStruct((seq, d_model), F32), jax.ShapeDtypeStruct((1, d_model), F32)],
        scratch_shapes=[pltpu.VMEM(win_g.shape, win_g.dtype), pltpu.SemaphoreType.DMA((nb,))],
        operands=[d4, du, dzs, win_g, xp, dy, g_pre])


def _lookup(g, table):
    out = jnp.int32(table[0])
    for gi in range(1, len(table)):
        if table[gi] != table[gi - 1]:
            out = jnp.where(g >= gi, jnp.int32(table[gi]), out)
    return out


def _held(values, used):
    cur = next(v for v, u in zip(values, used) if u)
    out = []
    for v, u in zip(values, used):
        cur = v if u else cur
        out.append(cur)
    return out


def _dw_in(name, ht, d4, du, dzs, granules, gr, nc, tm, comm=None):
    d_model, seq = ht.shape
    per = d4.shape[2] // gr
    piece, col = [g // per for g in granules], [g % per for g in granules]
    sources = [(d4, [p < 4 for p in piece]), (du, [p == 4 for p in piece]), (dzs, [p == 5 for p in piece])]
    sources = [(a, used) for a, used in sources if any(used)]
    select = [next(s for s, (_, used) in enumerate(sources) if used[q]) for q in range(len(granules))]
    owner, place = [g * gr // nc for g in granules], [g * gr % nc // gr for g in granules]

    def body(a_ref, *refs):
        src_refs, o_ref = refs[:-1], refs[-1]
        j = pl.program_id(0)
        for s, ref in enumerate(src_refs):
            @pl.when(_lookup(j, select) == s)
            def _():
                o_ref[...] = _dot(a_ref[...], ref[...]).astype(o_ref.dtype)

    in_specs = [pl.BlockSpec((tm, seq), lambda j, i: (i, 0))]
    for a, used in sources:
        cols = _held(col, used)
        if a.ndim == 3:
            rows = _held(piece, used)
            in_specs.append(pl.BlockSpec((None, seq, gr), functools.partial(
                lambda j, i, rows, cols: (_lookup(j, rows), 0, _lookup(j, cols)), rows=rows, cols=cols)))
        else:
            in_specs.append(pl.BlockSpec((seq, gr), functools.partial(
                lambda j, i, cols: (0, _lookup(j, cols)), cols=cols)))
    return _call(
        body, comm, name=name, grid=(len(granules), d_model // tm), in_specs=in_specs,
        out_specs=[pl.BlockSpec((None, tm, gr), lambda j, i: (_lookup(j, owner), i, _lookup(j, place)))],
        out_shape=[jax.ShapeDtypeStruct((N_DEV, d_model, nc), MXU_DTYPE)],
        operands=[ht] + [a for a, _ in sources])


def _wgrad(name, at, b, tm, tn, out_shape, out_block, out_index, comm=None):
    m, seq = at.shape
    n = b.shape[1]

    def body(a_ref, b_ref, o_ref):
        o_ref[...] = _dot(a_ref[...], b_ref[...]).astype(o_ref.dtype)

    return _call(
        body, comm, name=name, grid=(n // tn, m // tm),
        in_specs=[pl.BlockSpec((tm, seq), lambda j, i: (i, 0)), pl.BlockSpec((seq, tn), lambda j, i: (0, j))],
        out_specs=[pl.BlockSpec(out_block, lambda j, i: out_index(i, j))],
        out_shape=[jax.ShapeDtypeStruct(out_shape, MXU_DTYPE)],
        operands=[at, b])


def _eye_g():
    return jnp.eye(HALF_G, dtype=F32)


def _bb_blockdiag(bbt_r, bbt_i):
    n_half = bbt_r.shape[0] // HALF_G

    def one(t):
        t = t.reshape(n_half, HALF_G, SSM_GROUP, SSM_STATE)
        t = t[:, :, :, None, :] * _eye_g()[None, :, None, :, None]
        return t.reshape(n_half, HALF_CH, HALF_W)

    return jnp.concatenate([one(bbt_r), one(bbt_i)], axis=-1)


def _cc_blockdiag(c_re, c_im):
    n_half = c_re.shape[0] // HALF_G

    def one(t):
        t = t.reshape(n_half, HALF_G, SSM_GROUP, SSM_STATE)
        t = jnp.transpose(t, (0, 3, 1, 2))
        t = t[:, None, :, :, :] * _eye_g()[None, :, None, :, None]
        return t.reshape(n_half, HALF_W, HALF_CH)

    return jnp.concatenate([one(c_re), one(-c_im)], axis=1)


def _bb_diag(dbb):
    n_half = dbb.shape[0]
    t = dbb.reshape(n_half, HALF_G, SSM_GROUP, 2, HALF_G, SSM_STATE)
    t = jnp.sum(t * _eye_g()[None, :, None, None, :, None], axis=4)
    t = jnp.transpose(t, (3, 0, 1, 2, 4))
    return t.reshape(2, n_half * HALF_G, SSM_GROUP, SSM_STATE)


def _cc_diag(dcc):
    n_half = dcc.shape[0]
    t = dcc.reshape(n_half, 2, HALF_G, SSM_STATE, HALF_G, SSM_GROUP)
    t = jnp.sum(t * _eye_g()[None, None, :, None, :, None], axis=2)
    t = jnp.transpose(t, (1, 0, 3, 4, 2))
    return t.reshape(2, n_half * HALF_G, SSM_GROUP, SSM_STATE)


def _permute_rows(a):
    seq, d = a.shape
    return a.reshape(N_CHUNK, seq // N_CHUNK, d).transpose(1, 0, 2).reshape(seq, d)


def _unpermute_rows(a):
    seq, d = a.shape
    return a.reshape(seq // N_CHUNK, N_CHUNK, d).transpose(1, 0, 2).reshape(seq, d)


def _pack_rows(shape):
    return -(-math.prod(shape) // (8 * LANES)) * 8


def _pack(parts, dtype=F32):
    rows = []
    for p in parts:
        flat = p.reshape(-1).astype(dtype)
        rows.append(jnp.pad(flat, (0, _pack_rows(p.shape) * LANES - flat.shape[0])).reshape(-1, LANES))
    return jnp.concatenate(rows, axis=0)


def _unpack(packed, shapes):
    out, o = [], 0
    for s in shapes:
        n = _pack_rows(s)
        out.append(packed[o:o + n].reshape(-1)[:math.prod(s)].reshape(s))
        o += n
    return out


def kernel(x, norm_pre_g, w_in, conv_w, conv_b, ssm_a_re, ssm_a_im, ssm_log_dt, ssm_b_re, ssm_b_im, ssm_c_re, ssm_c_im, ssm_d, w_glu, b_glu, w_out, norm_post_g, loss_target, m_norm_pre_g, m_w_in, m_conv_w, m_conv_b, m_ssm_a_re, m_ssm_a_im, m_ssm_log_dt, m_ssm_b_re, m_ssm_b_im, m_ssm_c_re, m_ssm_c_im, m_ssm_d, m_w_glu, m_b_glu, m_w_out, m_norm_post_g, v_norm_pre_g, v_w_in, v_conv_w, v_conv_b, v_ssm_a_re, v_ssm_a_im, v_ssm_log_dt, v_ssm_b_re, v_ssm_b_im, v_ssm_c_re, v_ssm_c_im, v_ssm_d, v_w_glu, v_b_glu, v_w_out, v_norm_post_g):
    seq, d_model = x.shape[1], x.shape[2]
    d_conv, d_ssm = conv_b.shape[0], ssm_d.shape[0]
    groups, states = ssm_a_re.shape
    assert x.shape[0] == 1 and seq % (8 * N_CHUNK) == 0 and d_conv == d_ssm
    assert (groups, states) == (d_ssm // SSM_GROUP, SSM_STATE) and d_ssm % LANES == 0
    me = 4 * lax.axis_index("x") + 2 * lax.axis_index("y") + lax.axis_index("c")
    tm = min(512, seq)

    x3 = x[0].reshape(N_CHUNK, seq // N_CHUNK, d_model)
    t3 = loss_target[0].reshape(N_CHUNK, seq // N_CHUNK, d_model)
    row = lambda a: a.reshape(1, -1)
    conv_w8 = jnp.pad(conv_w, ((0, 8 - conv_w.shape[0]), (0, 0)))

    g3 = lambda a: a.reshape(groups, 1, -1)
    bt_re, bt_im = jnp.transpose(ssm_b_re, (0, 2, 1)), jnp.transpose(ssm_b_im, (0, 2, 1))
    lbr, lbi, qr, qi, bbt_r, bbt_i = _ssm_prep(g3(ssm_a_re), g3(ssm_a_im), g3(ssm_log_dt), bt_re, bt_im)
    n_half = groups // HALF_G
    lam = jnp.stack([lbr.reshape(n_half, HALF_W), lbi.reshape(n_half, HALF_W)], axis=1)
    bbcat = _bb_blockdiag(bbt_r, bbt_i).astype(MXU_DTYPE)
    cccat = _cc_blockdiag(ssm_c_re, ssm_c_im).astype(MXU_DTYPE)

    xp, h, ht = _norm_in(x3, row(norm_pre_g), tm)
    order = jnp.stack([jnp.bitwise_xor(me, r) for r in CONSUME_ORDER]).astype(jnp.int32)
    proj, win_g, (convw_g,) = _fwd_in(h, w_in.astype(MXU_DTYPE), order, _Comm([conv_w8]), tm)
    conv_w_full = jnp.transpose(convw_g, (1, 0, 2)).reshape(8, d_conv)
    u_col0, zs_col0 = 4 * d_conv, 4 * d_conv + d_ssm
    yconv = _conv_fwd(proj, conv_w_full, row(conv_b), d_conv)
    (yp,), (wout_g, wglu_g) = _ssm_fwd(proj, lam, bbcat, cccat, row(ssm_d), d_ssm, u_col0,
                                       _Comm([w_out.astype(MXU_DTYPE), w_glu.astype(MXU_DTYPE)]))
    w_out_full = wout_g.reshape(-1, d_model)
    w_glu_full = wglu_g.reshape(-1, d_ssm)
    (dy, d_o, mixt, dyc, dyp, dzs, ygt, dq, loss_part, dgpost, dbglu) = _tail(
        xp, t3, proj, yconv, yp, w_glu_full, row(b_glu), w_out_full, row(norm_post_g), zs_col0, min(256, seq))

    r_out, r_glu, nc = w_out.shape[0], w_glu.shape[0], w_in.shape[1]
    (dwout_p,), _ = _wgrad("dw_out", mixt, d_o, r_out, min(1024, d_model), (N_DEV, r_out, d_model),
                           (None, r_out, min(1024, d_model)), lambda i, j: (i, 0, j))
    (dwglu_p,), _ = _wgrad("dw_glu", ygt, dq, r_glu, d_ssm, (N_DEV, r_glu, d_ssm),
                           (None, r_glu, d_ssm), lambda i, j: (i, 0, 0))
    (d4, dconvb, dconvw), (recv_glu,) = _conv_bwd(proj, dyc, conv_w_full, row(conv_b), d_conv,
                                                  _Comm([], [dwglu_p]))
    late = [k for k in range(N_DEV) if k * nc < u_col0 + d_ssm and (k + 1) * nc > u_col0]
    early = [k for k in range(N_DEV) if k not in late]
    gr = math.gcd(nc, d_conv)
    granules = lambda blocks: [g for k in blocks for g in range(k * nc // gr, (k + 1) * nc // gr)]
    tmw = min(1024, d_model)
    (dwin_e,), (recv_out,) = _dw_in("dw_in_early", ht, d4, None, dzs, granules(early), gr, nc, tmw,
                                    _Comm([], [dwout_p]))
    (du, dbb, dcc, da, dd), (recv_in,) = _ssm_bwd(
        proj, dyp, lam, bbcat, cccat, row(ssm_d), d_ssm, u_col0, _Comm([], [dwin_e], dests={0: early}))
    parts_mx = [_bb_diag(dbb), _cc_diag(dcc)]
    (dwin_l,), (pack_mx_g,) = _dw_in("dw_in_late", ht, d4, du, dzs, granules(late), gr, nc, tmw,
                                     _Comm([_pack(parts_mx, MXU_DTYPE)]))
    da_n = jnp.transpose(da.reshape(n_half, 2, HALF_G, SSM_STATE), (1, 0, 2, 3)).reshape(2, groups, 1, states)
    parts = [dgpost, dconvb, dd, dbglu, dconvw[:3], da_n, loss_part]
    shapes, shapes_mx = [p.shape for p in parts], [p.shape for p in parts_mx]
    (gx_p, dgpre), (pack_g, recv_in) = _bwd_in(
        d4, du, dzs, gr, win_g, xp, dy, row(norm_pre_g),
        _Comm([_pack(parts)], [dwin_l], dests={1: late}, into={1: recv_in}), min(256, seq))
    (last_g,) = _exchange("reduce_last", [_pack([dgpre])], [])
    (g_gpost, g_convb, g_d, g_bglu, g_convw, g_da, loss_sum) = _unpack(_sum_slots("sum_pack", pack_g), shapes)
    (g_dbb, g_dcc) = _unpack(_sum_slots("sum_pack_mx", pack_mx_g), shapes_mx)
    (g_gpre,) = _unpack(_sum_slots("sum_last", last_g), [dgpre.shape])
    g_convw = lax.dynamic_slice(g_convw, (0, me * conv_w.shape[1]), conv_w.shape)

    tr = lambda a: jnp.transpose(a, (0, 2, 1))
    direct = [(g_gpre, row(norm_pre_g), row(m_norm_pre_g), row(v_norm_pre_g)),
              (g_convb, row(conv_b), row(m_conv_b), row(v_conv_b)),
              (g_d, row(ssm_d), row(m_ssm_d), row(v_ssm_d)),
              (g_bglu, row(b_glu), row(m_b_glu), row(v_b_glu)),
              (g_gpost, row(norm_post_g), row(m_norm_post_g), row(v_norm_post_g)),
              (g_convw, conv_w, m_conv_w, v_conv_w),
              (g_dcc[0], ssm_c_re, m_ssm_c_re, v_ssm_c_re),
              (-g_dcc[1], ssm_c_im, m_ssm_c_im, v_ssm_c_im)]
    ssm = dict(da_r=g_da[0], da_i=g_da[1], dbb_r=g_dbb[0], dbb_i=g_dbb[1], lr=g3(ssm_a_re), li=g3(ssm_a_im),
               ldt=g3(ssm_log_dt), bt_r=bt_re, bt_i=bt_im, lbr=lbr, lbi=lbi, qr=qr, qi=qi,
               w_a_re=g3(ssm_a_re), m_a_re=g3(m_ssm_a_re), v_a_re=g3(v_ssm_a_re),
               w_a_im=g3(ssm_a_im), m_a_im=g3(m_ssm_a_im), v_a_im=g3(v_ssm_a_im),
               w_log_dt=g3(ssm_log_dt), m_log_dt=g3(m_ssm_log_dt), v_log_dt=g3(v_ssm_log_dt),
               w_bt_re=bt_re, m_bt_re=tr(m_ssm_b_re), v_bt_re=tr(v_ssm_b_re),
               w_bt_im=bt_im, m_bt_im=tr(m_ssm_b_im), v_bt_im=tr(v_ssm_b_im))
    small = _small_update(direct, ssm)
    res = {}
    for name, quad, shape in zip(["norm_pre_g", "conv_b", "ssm_d", "b_glu", "norm_post_g", "conv_w", "ssm_c_re", "ssm_c_im"],
                                 small[:8], [norm_pre_g.shape, conv_b.shape, ssm_d.shape, b_glu.shape,
                                             norm_post_g.shape, conv_w.shape, ssm_c_re.shape, ssm_c_im.shape]):
        res[name] = tuple(a.reshape(shape) for a in quad)
    res["ssm_a_re"] = tuple(a.reshape(ssm_a_re.shape) for a in small[8])
    res["ssm_a_im"] = tuple(a.reshape(ssm_a_im.shape) for a in small[9])
    res["ssm_log_dt"] = tuple(a.reshape(ssm_log_dt.shape) for a in small[10])
    res["ssm_b_re"] = tuple(tr(a) for a in small[11])
    res["ssm_b_im"] = tuple(tr(a) for a in small[12])
    res["w_in"] = tuple(_adam_big("adam_w_in", recv_in, w_in, m_w_in, v_w_in, min(256, d_model)))
    res["w_out"] = tuple(_adam_big("adam_w_out", recv_out, w_out, m_w_out, v_w_out, min(128, r_out)))
    res["w_glu"] = tuple(_adam_big("adam_w_glu", recv_glu, w_glu, m_w_glu, v_w_glu, r_glu))

    order = ["norm_pre_g", "w_in", "conv_w", "conv_b", "ssm_a_re", "ssm_a_im", "ssm_log_dt", "ssm_b_re", "ssm_b_im",
             "ssm_c_re", "ssm_c_im", "ssm_d", "w_glu", "b_glu", "w_out", "norm_post_g"]
    loss = loss_sum[0, 0]
    grad_x = _unpermute_rows(gx_p)[None]
    return (loss, grad_x, *[res[n][0] for n in order], *[res[n][1] for n in order],
            *[res[n][2] for n in order], *[res[n][3] for n in order])
```

```python
import functools
import math

import jax
import jax.numpy as jnp
from jax import lax
from jax.experimental import pallas as pl
from jax.experimental.pallas import tpu as pltpu

F32 = jnp.float32
MXU_DTYPE = jnp.bfloat16
AXES = ("x", "y", "c")
N_DEV = 8
N_CHUNK = 8
LANES = 128
SSM_GROUP = 16
SSM_STATE = 64
HALF_CH = 64
HALF_G = HALF_CH // SSM_GROUP
HALF_W = HALF_G * SSM_STATE
EPS = 1e-6
ADAM_LR, ADAM_B1, ADAM_B2, ADAM_EPS, ADAM_WD, ADAM_STEP = 0.001, 0.9, 0.999, 1e-08, 0.01, 10
GELU_C = math.sqrt(2.0 / math.pi)
GELU_K = 0.044715
VMEM_LIMIT = 56 * 1024 * 1024


def _params(sem=None):
    return pltpu.CompilerParams(dimension_semantics=sem, vmem_limit_bytes=VMEM_LIMIT)


def _dot(a, b):
    return jnp.dot(a, b, preferred_element_type=F32)


def _dot_nt(a, b):
    return lax.dot_general(a, b, (((1,), (1,)), ((), ())), preferred_element_type=F32)


def _dot_tn(a, b):
    return lax.dot_general(a, b, (((0,), (0,)), ((), ())), preferred_element_type=F32)


def _sigmoid(z):
    return 1.0 / (1.0 + jnp.exp(-z))


def _flip(v, bit):
    return 1 - v if bit else v


def _peers():
    x, y, c = (lax.axis_index(a) for a in AXES)
    out = []
    for m in range(1, N_DEV):
        px, py, pc = _flip(x, (m >> 2) & 1), _flip(y, (m >> 1) & 1), _flip(c, m & 1)
        out.append((px, py, pc, 4 * px + 2 * py + pc))
    return out


class _Comm:
    def __init__(self, gathers=(), scatters=(), dests=None, into=None):
        self.n_g = len(gathers)
        self.operands = list(gathers) + list(scatters)
        self.n = len(self.operands)
        self.dests = dests or {}
        self.into = into or {}

    def out_shape(self):
        return [jax.ShapeDtypeStruct((N_DEV,) + a.shape if t < self.n_g else a.shape, a.dtype)
                for t, a in enumerate(self.operands)]

    def scratch(self):
        if not self.n:
            return []
        return [pltpu.SemaphoreType.DMA((self.n, N_DEV - 1)), pltpu.SemaphoreType.DMA((self.n, N_DEV - 1)),
                pltpu.SemaphoreType.DMA((self.n,))]

    def _copies(self, in_refs, out_refs, sems, arrivals):
        send_sems, recv_sems, local_sems = sems
        x, y, c = (lax.axis_index(a) for a in AXES)
        me = 4 * x + 2 * y + c

        def src(t, dev):
            return in_refs[t] if t < self.n_g else in_refs[t].at[dev]

        def member(t, dev):
            if t not in self.dests:
                return None
            return functools.reduce(jnp.logical_or, [dev == d for d in self.dests[t]])

        local = [(member(t, me), pltpu.make_async_copy(src(t, me), out_refs[t].at[me], local_sems.at[t]))
                 for t in range(self.n)]
        sends, recvs = [], []
        for t in range(self.n):
            for m, (px, py, pc, peer) in enumerate(_peers()):
                kw = dict(send_sem=send_sems.at[t, m], recv_sem=recv_sems.at[t, m],
                          device_id=(px, py, pc), device_id_type=pl.DeviceIdType.MESH)
                sends.append((member(t, peer), pltpu.make_async_remote_copy(
                    src_ref=src(t, peer), dst_ref=out_refs[t].at[me], **kw)))
                if arrivals:
                    recvs.append((member(t, me), pltpu.make_async_remote_copy(
                        src_ref=src(t, peer), dst_ref=out_refs[t].at[peer], **kw)))
        return local, sends, recvs

    @staticmethod
    def _do(cond, action):
        if cond is None:
            action()
        else:
            pl.when(cond)(action)

    def start(self, in_refs, out_refs, sems):
        local, sends, _ = self._copies(in_refs, out_refs, sems, arrivals=False)
        for cond, cp in local + sends:
            self._do(cond, cp.start)

    def finish(self, in_refs, out_refs, sems):
        local, sends, recvs = self._copies(in_refs, out_refs, sems, arrivals=True)
        for cond, cp in recvs:
            self._do(cond, cp.wait_recv)
        for cond, cp in sends:
            self._do(cond, cp.wait_send)
        for cond, cp in local:
            self._do(cond, cp.wait)


def _call(body, comm, *, name, grid, in_specs, out_specs, out_shape, operands, scratch_shapes=()):
    comm = comm or _Comm()
    n_in, n_out, n_scr, cn = len(in_specs), len(out_specs), len(scratch_shapes), comm.n
    landing = sorted(comm.into)
    aliases = {n_in + cn + q: n_out + t for q, t in enumerate(landing)}

    def wrapped(*refs):
        parts, o = [], 0
        for k in (n_in, cn, len(landing), n_out, cn, n_scr):
            parts.append(refs[o:o + k])
            o += k
        h_in, c_in, _, h_out, c_out, h_scr = parts
        sems = refs[o:]
        if cn:
            first = functools.reduce(jnp.logical_and, [pl.program_id(d) == 0 for d in range(len(grid))])

            @pl.when(first)
            def _():
                comm.start(c_in, c_out, sems)

        body(*h_in, *h_out, *h_scr)
        if cn:
            last = functools.reduce(jnp.logical_and, [pl.program_id(d) == grid[d] - 1 for d in range(len(grid))])

            @pl.when(last)
            def _():
                comm.finish(c_in, c_out, sems)

    any_ = pl.BlockSpec(memory_space=pl.ANY)
    res = pl.pallas_call(
        wrapped, name=name, grid=grid, in_specs=list(in_specs) + [any_] * (cn + len(landing)),
        out_specs=list(out_specs) + [any_] * cn,
        out_shape=list(out_shape) + comm.out_shape(), scratch_shapes=list(scratch_shapes) + comm.scratch(),
        input_output_aliases=aliases, compiler_params=_params(("arbitrary",) * len(grid)),
    )(*operands, *comm.operands, *[comm.into[t] for t in landing])
    return list(res[:n_out]), list(res[n_out:])


def _exchange(name, gathers, scatters):
    def body(tok_ref):
        tok_ref[...] = jnp.zeros_like(tok_ref)

    return _call(body, _Comm(gathers, scatters), name=name, grid=(1,), in_specs=[],
                 out_specs=[pl.BlockSpec((8, LANES), lambda i: (0, 0))],
                 out_shape=[jax.ShapeDtypeStruct((8, LANES), F32)], operands=[])[1]


def _ssm_prep(a_re, a_im, log_dt, bt_re, bt_im):
    def body(lr_ref, li_ref, ldt_ref, br_ref, bi_ref, lbr_ref, lbi_ref, qr_ref, qi_ref, bbr_ref, bbi_ref):
        lr, li = lr_ref[...], li_ref[...]
        dt = jnp.exp(ldt_ref[...])
        mag = jnp.exp(lr * dt)
        lbr, lbi = mag * jnp.cos(li * dt), mag * jnp.sin(li * dt)
        nr, ni = lbr - 1.0, lbi
        den = lr * lr + li * li
        qr = (nr * lr + ni * li) / den
        qi = (ni * lr - nr * li) / den
        br, bi = br_ref[...], bi_ref[...]
        lbr_ref[...], lbi_ref[...], qr_ref[...], qi_ref[...] = lbr, lbi, qr, qi
        bbr_ref[...] = qr * br - qi * bi
        bbi_ref[...] = qr * bi + qi * br

    s2 = jax.ShapeDtypeStruct(a_re.shape, F32)
    s3 = jax.ShapeDtypeStruct(bt_re.shape, F32)
    return pl.pallas_call(body, name="ssm_prep", out_shape=[s2, s2, s2, s2, s3, s3],
                          compiler_params=_params())(a_re, a_im, log_dt, bt_re, bt_im)


def _adam(w, g, m, v):
    m2 = ADAM_B1 * m + (1.0 - ADAM_B1) * g
    v2 = ADAM_B2 * v + (1.0 - ADAM_B2) * (g * g)
    m_hat = m2 / (1.0 - ADAM_B1 ** ADAM_STEP)
    v_hat = v2 / (1.0 - ADAM_B2 ** ADAM_STEP)
    delta = -ADAM_LR * (m_hat / (jnp.sqrt(v_hat) + ADAM_EPS) + ADAM_WD * w)
    return delta, m2, v2


def _small_update(direct, ssm):
    n_direct = len(direct)
    flat = [a for quad in direct for a in quad]
    names = ["da_r", "da_i", "dbb_r", "dbb_i", "lr", "li", "ldt", "bt_r", "bt_i", "lbr", "lbi", "qr", "qi"]
    flat += [ssm[k] for k in names]
    chain = ["a_re", "a_im", "log_dt", "bt_re", "bt_im"]
    for k in chain:
        flat += [ssm["w_" + k], ssm["m_" + k], ssm["v_" + k]]
    n_in = len(flat)

    def body(*refs):
        ins, outs = refs[:n_in], refs[n_in:]
        for p in range(n_direct):
            g, w, m, v = (r[...] for r in ins[4 * p:4 * p + 4])
            d, m2, v2 = _adam(w, g, m, v)
            outs[4 * p][...], outs[4 * p + 1][...], outs[4 * p + 2][...], outs[4 * p + 3][...] = g, d, m2, v2
        o = 4 * n_direct
        da_r, da_i, dbb_r, dbb_i, lr, li, ldt, bt_r, bt_i, lbr, lbi, qr, qi = (r[...] for r in ins[o:o + 13])
        dt = jnp.exp(ldt)
        g_br = qr * dbb_r + qi * dbb_i
        g_bi = qr * dbb_i - qi * dbb_r
        dq_r = jnp.sum(bt_r * dbb_r + bt_i * dbb_i, axis=1, keepdims=True)
        dq_i = jnp.sum(bt_r * dbb_i - bt_i * dbb_r, axis=1, keepdims=True)
        den = lr * lr + li * li
        cr, ci = lr / den, li / den
        gl_r = da_r + (cr * dq_r - ci * dq_i)
        gl_i = da_i + (cr * dq_i + ci * dq_r)
        w_r = qr * cr + qi * ci
        w_i = qi * cr - qr * ci
        g_lr = dt * (lbr * gl_r + lbi * gl_i) + (-w_r * dq_r - w_i * dq_i)
        g_li = dt * (lbr * gl_i - lbi * gl_r) + (-w_r * dq_i + w_i * dq_r)
        m_r = lr * lbr - li * lbi
        m_i = lr * lbi + li * lbr
        g_ldt = jnp.sum(m_r * gl_r + m_i * gl_i, axis=2, keepdims=True) * dt
        grads = [g_lr, g_li, g_ldt, g_br, g_bi]
        base_in, base_out = o + 13, 4 * n_direct
        for p, g in enumerate(grads):
            w, m, v = (r[...] for r in ins[base_in + 3 * p:base_in + 3 * p + 3])
            d, m2, v2 = _adam(w, g, m, v)
            q = base_out + 4 * p
            outs[q][...], outs[q + 1][...], outs[q + 2][...], outs[q + 3][...] = g, d, m2, v2

    out_shape = []
    for quad in direct:
        out_shape += [jax.ShapeDtypeStruct(quad[1].shape, F32)] * 4
    for k in chain:
        out_shape += [jax.ShapeDtypeStruct(ssm["w_" + k].shape, F32)] * 4
    res = pl.pallas_call(body, name="small_update", out_shape=out_shape, compiler_params=_params())(*flat)
    return [tuple(res[4 * p:4 * p + 4]) for p in range(n_direct + len(chain))]


def _sum_slots(name, pack):
    def body(p_ref, o_ref):
        acc = p_ref[0].astype(F32)
        for k in range(1, N_DEV):
            acc = acc + p_ref[k].astype(F32)
        o_ref[...] = acc

    return pl.pallas_call(body, name=name, out_shape=jax.ShapeDtypeStruct(pack.shape[1:], F32),
                          compiler_params=_params())(pack)


def _adam_big(name, recv, w, m, v, tr):
    _, rows, cols = recv.shape

    def body(r_ref, w_ref, m_ref, v_ref, g_ref, d_ref, m2_ref, v2_ref):
        g = r_ref[0].astype(F32)
        for k in range(1, N_DEV):
            g = g + r_ref[k].astype(F32)
        d, m2, v2 = _adam(w_ref[...], g, m_ref[...], v_ref[...])
        g_ref[...], d_ref[...], m2_ref[...], v2_ref[...] = g, d, m2, v2

    blk = pl.BlockSpec((tr, cols), lambda i: (i, 0))
    shp = jax.ShapeDtypeStruct((rows, cols), F32)
    return pl.pallas_call(
        body, name=name, grid=(rows // tr,),
        in_specs=[pl.BlockSpec((N_DEV, tr, cols), lambda i: (0, i, 0)), blk, blk, blk],
        out_specs=[blk] * 4, out_shape=[shp] * 4, compiler_params=_params(("parallel",)),
    )(recv, w, m, v)


def _chunk_block(tm, d):
    return pl.BlockSpec((N_CHUNK, tm // N_CHUNK, d), lambda i: (0, i, 0))


def _interleave(block):
    c, n, d = block.shape
    return pltpu.einshape("cjd->jcd", block).reshape(n * c, d)


def _norm_in(x3, g_pre, tm):
    _, steps, d_model = x3.shape
    seq = steps * N_CHUNK

    def body(x_ref, g_ref, xp_ref, h_ref, ht_ref):
        x = _interleave(x_ref[...])
        xp_ref[...] = x
        r = lax.rsqrt(jnp.mean(x * x, axis=-1, keepdims=True) + EPS)
        h = x * r * g_ref[...]
        h_ref[...] = h.astype(h_ref.dtype)
        ht_ref[...] = h.T.astype(ht_ref.dtype)

    rows = pl.BlockSpec((tm, d_model), lambda i: (i, 0))
    return pl.pallas_call(
        body, name="norm_in", grid=(seq // tm,),
        in_specs=[_chunk_block(tm, d_model), pl.BlockSpec((1, d_model), lambda i: (0, 0))],
        out_specs=[rows, rows, pl.BlockSpec((d_model, tm), lambda i: (0, i))],
        out_shape=[jax.ShapeDtypeStruct((seq, d_model), F32), jax.ShapeDtypeStruct((seq, d_model), MXU_DTYPE),
                   jax.ShapeDtypeStruct((d_model, seq), MXU_DTYPE)],
        compiler_params=_params(("parallel",)),
    )(x3, g_pre)


GATHER_ORDER = (0, 1, 4, 2, 6, 5, 3, 7)
CONSUME_ORDER = (0, 1, 4, 2, 5, 3, 6, 7)


def _fwd_in(h, w_shard, order, comm, tm):
    seq, d_model = h.shape
    nc = w_shard.shape[1]
    n_i = seq // tm
    cn = comm.n

    def body(order_ref, h_ref, w_hbm, *rest):
        c_in, rest = rest[:cn], rest[cn:]
        proj_ref, wing = rest[0], rest[1]
        c_out, rest = rest[2:2 + cn], rest[2 + cn:]
        wbuf, send_sems, recv_sems, own_sem, load_sems = rest[:5]
        c_sems = rest[5:]
        k, i = pl.program_id(0), pl.program_id(1)
        x, y, c = (lax.axis_index(a) for a in AXES)
        me = 4 * x + 2 * y + c

        def dev(rel):
            return _flip(x, (rel >> 2) & 1), _flip(y, (rel >> 1) & 1), _flip(c, rel & 1)

        def slot(rel):
            px, py, pc = dev(rel)
            return 4 * px + 2 * py + pc

        def remote(src, block, to_rel, sem):
            return pltpu.make_async_remote_copy(
                src_ref=src, dst_ref=wing.at[block], send_sem=send_sems.at[sem], recv_sem=recv_sems.at[sem],
                device_id=dev(to_rel), device_id_type=pl.DeviceIdType.MESH)

        own = pltpu.make_async_copy(w_hbm, wing.at[me], own_sem)
        first_hand = [remote(w_hbm, me, GATHER_ORDER[p], p - 1) for p in range(1, 4)]
        relay = {2: (c == 0, remote(wing.at[slot(4)], slot(4), 2, 3)),
                 3: (c == 1, remote(wing.at[slot(2)], slot(2), 4, 3))}
        passed_on = [remote(wing.at[slot(GATHER_ORDER[p])], slot(GATHER_ORDER[p]), 1, p + 2) for p in range(2, 5)]

        def load(q):
            return pltpu.make_async_copy(wing.at[slot(CONSUME_ORDER[q])], wbuf.at[q % 2], load_sems.at[q % 2])

        def take(q):
            p = GATHER_ORDER.index(CONSUME_ORDER[q])
            if p == 0:
                own.wait()
            else:
                remote(w_hbm, slot(GATHER_ORDER[p]), GATHER_ORDER[p], p - 1).wait_recv()
            if p in relay:
                pl.when(relay[p][0])(relay[p][1].start)
            if 2 <= p <= 4:
                passed_on[p - 2].start()
            load(q).start()

        @pl.when((k == 0) & (i == 0))
        def _():
            own.start()
            for cp in first_hand:
                cp.start()
            comm.start(c_in, c_out, c_sems)
            take(0)

        for q in range(N_DEV):
            @pl.when((k == q) & (i == 0))
            def _():
                load(q).wait()

            if q + 1 < N_DEV:
                @pl.when((k == q) & (i == n_i - 1))
                def _():
                    take(q + 1)

        proj_ref[...] = _dot(h_ref[...], wbuf[k % 2])

        @pl.when((k == N_DEV - 1) & (i == n_i - 1))
        def _():
            for cp in first_hand + passed_on:
                cp.wait_send()
            for cond, cp in relay.values():
                pl.when(cond)(cp.wait_send)
            comm.finish(c_in, c_out, c_sems)

    any_ = pl.BlockSpec(memory_space=pl.ANY)
    grid_spec = pltpu.PrefetchScalarGridSpec(
        num_scalar_prefetch=1, grid=(N_DEV, n_i),
        in_specs=[pl.BlockSpec((tm, d_model), lambda k, i, o: (i, 0)), any_] + [any_] * cn,
        out_specs=[pl.BlockSpec((tm, nc), lambda k, i, o: (i, o[k])), any_] + [any_] * cn,
        scratch_shapes=[pltpu.VMEM((2, d_model, nc), w_shard.dtype), pltpu.SemaphoreType.DMA((N_DEV - 1,)),
                        pltpu.SemaphoreType.DMA((N_DEV - 1,)), pltpu.SemaphoreType.DMA, pltpu.SemaphoreType.DMA((2,))]
        + comm.scratch())
    res = pl.pallas_call(
        body, name="fwd_in", grid_spec=grid_spec,
        out_shape=[jax.ShapeDtypeStruct((seq, N_DEV * nc), F32),
                   jax.ShapeDtypeStruct((N_DEV, d_model, nc), w_shard.dtype)] + comm.out_shape(),
        compiler_params=_params(("arbitrary", "arbitrary")),
    )(order, h, w_shard, *comm.operands)
    return res[0], res[1], list(res[2:])


def _shift_prev(a):
    n = a.shape[0]
    last = a[n - N_CHUNK:, :]
    row = lax.broadcasted_iota(jnp.int32, last.shape, 0)
    wrap = jnp.where(row == 0, 0.0, pltpu.roll(last, 1, axis=0))
    return jnp.concatenate([wrap, a[:n - N_CHUNK, :]], axis=0)


def _shift_next(a):
    first = a[:N_CHUNK, :]
    row = lax.broadcasted_iota(jnp.int32, first.shape, 0)
    wrap = jnp.where(row == N_CHUNK - 1, 0.0, pltpu.roll(first, N_CHUNK - 1, axis=0))
    return jnp.concatenate([a[N_CHUNK:, :], wrap], axis=0)


def _conv_specs(seq, d_conv):
    nblk = d_conv // LANES
    return [pl.BlockSpec((seq, LANES), functools.partial(lambda i, o: (0, o + i), o=q * nblk)) for q in range(4)]


def _conv_fwd(proj, conv_w8, conv_b, d_conv):
    seq = proj.shape[0]

    def body(bg_ref, cg_ref, v_ref, zc_ref, w_ref, b_ref, y_ref):
        cv = cg_ref[...] * v_ref[...]
        s1 = _shift_prev(cv)
        s2 = _shift_prev(s1)
        conv = b_ref[...] + w_ref[0:1, :] * s2 + w_ref[1:2, :] * s1 + w_ref[2:3, :] * cv
        z = zc_ref[...]
        y_ref[...] = (bg_ref[...] * conv * (z * _sigmoid(z))).astype(y_ref.dtype)

    col = pl.BlockSpec((seq, LANES), lambda i: (0, i))
    return pl.pallas_call(
        body, name="conv_fwd", grid=(d_conv // LANES,),
        in_specs=_conv_specs(seq, d_conv) + [pl.BlockSpec((8, LANES), lambda i: (0, i)), pl.BlockSpec((1, LANES), lambda i: (0, i))],
        out_specs=col, out_shape=jax.ShapeDtypeStruct((seq, d_conv), MXU_DTYPE),
        compiler_params=_params(("parallel",)),
    )(proj, proj, proj, proj, conv_w8, conv_b)


def _conv_bwd(proj, dyc, conv_w8, conv_b, d_conv, comm=None):
    seq = proj.shape[0]

    def body(bg_ref, cg_ref, v_ref, zc_ref, dy_ref, w_ref, b_ref, d4_ref, dcb_ref, dcw_ref):
        bg, cg, v, z = bg_ref[...], cg_ref[...], v_ref[...], zc_ref[...]
        w0, w1, w2 = w_ref[0:1, :], w_ref[1:2, :], w_ref[2:3, :]
        cv = cg * v
        s1 = _shift_prev(cv)
        s2 = _shift_prev(s1)
        conv = b_ref[...] + w0 * s2 + w1 * s1 + w2 * cv
        sig = _sigmoid(z)
        dy = dy_ref[...].astype(F32)
        g1 = dy * (z * sig)
        d_conv_ = g1 * bg
        d4_ref[0] = (g1 * conv).astype(d4_ref.dtype)
        d4_ref[3] = (dy * bg * conv * (sig * (1.0 + z * (1.0 - sig)))).astype(d4_ref.dtype)
        n1 = _shift_next(d_conv_)
        n2 = _shift_next(n1)
        d_cv = w2 * d_conv_ + w1 * n1 + w0 * n2
        d4_ref[1] = (d_cv * v).astype(d4_ref.dtype)
        d4_ref[2] = (d_cv * cg).astype(d4_ref.dtype)
        dcb_ref[...] = jnp.sum(d_conv_, axis=0, keepdims=True)
        rows = [jnp.sum(d_conv_ * s, axis=0, keepdims=True) for s in (s2, s1, cv)]
        dcw_ref[...] = jnp.concatenate(rows + [jnp.zeros((5, LANES), F32)], axis=0)

    col = pl.BlockSpec((seq, LANES), lambda i: (0, i))
    return _call(
        body, comm, name="conv_bwd", grid=(d_conv // LANES,),
        in_specs=_conv_specs(seq, d_conv) + [col, pl.BlockSpec((8, LANES), lambda i: (0, i)), pl.BlockSpec((1, LANES), lambda i: (0, i))],
        out_specs=[pl.BlockSpec((4, seq, LANES), lambda i: (0, 0, i)), pl.BlockSpec((1, LANES), lambda i: (0, i)),
                   pl.BlockSpec((8, LANES), lambda i: (0, i))],
        out_shape=[jax.ShapeDtypeStruct((4, seq, d_conv), MXU_DTYPE), jax.ShapeDtypeStruct((1, d_conv), F32),
                   jax.ShapeDtypeStruct((8, d_conv), F32)],
        operands=[proj, proj, proj, proj, dyc, conv_w8, conv_b])


def _cmul(ar, ai, br, bi):
    return ar * br - ai * bi, ar * bi + ai * br


def _cpow(ar, ai, n):
    rr, ri = jnp.ones_like(ar), jnp.zeros_like(ai)
    while n:
        if n & 1:
            rr, ri = _cmul(rr, ri, ar, ai)
        n >>= 1
        if n:
            ar, ai = _cmul(ar, ai, ar, ai)
    return rr, ri


def _down(v, k):
    row = lax.broadcasted_iota(jnp.int32, v.shape, 0)
    return jnp.where(row >= k, pltpu.roll(v, k, axis=0), 0.0)


def _up(v, k):
    row = lax.broadcasted_iota(jnp.int32, v.shape, 0)
    return jnp.where(row < N_CHUNK - k, pltpu.roll(v, N_CHUNK - k, axis=0), 0.0)


def _chunk_carry(fr, fi, mr, mi, shift):
    vr, vi = shift(fr, 1), shift(fi, 1)
    for k in (1, 2, 4):
        pr, pi = _cmul(mr, mi, shift(vr, k), shift(vi, k))
        vr, vi = vr + pr, vi + pi
        mr, mi = _cmul(mr, mi, mr, mi)
    return vr, vi


def _tile(ref, j, width, part):
    return ref.at[pl.ds(pl.multiple_of(j * N_CHUNK, N_CHUNK), N_CHUNK), pl.ds(part * width, width)]


def _row(t, k):
    return jnp.broadcast_to(t[k:k + 1, :], t.shape)


def _power_table(tab_ref, ar, ai, steps, width):
    e = lax.broadcasted_iota(jnp.int32, ar.shape, 0) + 1
    rr, ri = jnp.ones_like(ar), jnp.zeros_like(ai)
    br, bi = ar, ai
    for bit in range(4):
        mr, mi = _cmul(rr, ri, br, bi)
        take = ((e >> bit) & 1) == 1
        rr, ri = jnp.where(take, mr, rr), jnp.where(take, mi, ri)
        if bit < 3:
            br, bi = _cmul(br, bi, br, bi)
    _tile(tab_ref, 0, width, 0)[...] = rr
    _tile(tab_ref, 0, width, 1)[...] = ri

    def step(m, carry):
        tr, ti = _cmul(carry[0], carry[1], br, bi)
        _tile(tab_ref, m, width, 0)[...] = tr
        _tile(tab_ref, m, width, 1)[...] = ti
        return tr, ti

    lax.fori_loop(1, steps // N_CHUNK, step, (rr, ri))


def _last_power(tab_ref, steps, width):
    shape = (N_CHUNK, width)
    return (jnp.broadcast_to(tab_ref[steps - 1:steps, 0:width], shape),
            jnp.broadcast_to(tab_ref[steps - 1:steps, width:2 * width], shape))


def _scan_fwd(s_ref, ar, ai, steps, width):
    def step(j, carry):
        sr, si = carry
        nr = ar * sr - ai * si + _tile(s_ref, j, width, 0)[...]
        ni = ar * si + ai * sr + _tile(s_ref, j, width, 1)[...]
        _tile(s_ref, j, width, 0)[...] = nr
        _tile(s_ref, j, width, 1)[...] = ni
        return nr, ni

    z = jnp.zeros((N_CHUNK, width), F32)
    return lax.fori_loop(0, steps, step, (z, z), unroll=4)


def _scan_both(s_ref, g_ref, ar, ai, steps, width):
    def step(q, carry):
        sr, si, gr, gi = carry
        j, jb = q, steps - 1 - q
        nsr = ar * sr - ai * si + _tile(s_ref, j, width, 0)[...]
        nsi = ar * si + ai * sr + _tile(s_ref, j, width, 1)[...]
        ngr = ar * gr + ai * gi + _tile(g_ref, jb, width, 0)[...]
        ngi = ar * gi - ai * gr + _tile(g_ref, jb, width, 1)[...]
        _tile(s_ref, j, width, 0)[...] = nsr
        _tile(s_ref, j, width, 1)[...] = nsi
        _tile(g_ref, jb, width, 0)[...] = ngr
        _tile(g_ref, jb, width, 1)[...] = ngi
        return nsr, nsi, ngr, ngi

    z = jnp.zeros((N_CHUNK, width), F32)
    return lax.fori_loop(0, steps, step, (z, z, z, z), unroll=2)


def _patch_fwd(s_ref, tab_ref, cr, ci, steps, width):
    def tile(m, _):
        tr, ti = _tile(tab_ref, m, width, 0)[...], _tile(tab_ref, m, width, 1)[...]
        for k in range(N_CHUNK):
            fr, fi = _cmul(_row(tr, k), _row(ti, k), cr, ci)
            j = m * N_CHUNK + k
            _tile(s_ref, j, width, 0)[...] += fr
            _tile(s_ref, j, width, 1)[...] += fi
        return 0

    lax.fori_loop(0, steps // N_CHUNK, tile, 0)


def _lam_rows(lam_ref, hh, width):
    return (jnp.broadcast_to(lam_ref[hh, 0:1, :], (N_CHUNK, width)),
            jnp.broadcast_to(lam_ref[hh, 1:2, :], (N_CHUNK, width)))


def _ssm_specs(seq, col0):
    return dict(
        col=pl.BlockSpec((seq, LANES), lambda i: (0, col0 + i)),
        lam=pl.BlockSpec((2, 2, HALF_W), lambda i: (i, 0, 0)),
        bb=pl.BlockSpec((2, HALF_CH, 2 * HALF_W), lambda i: (i, 0, 0)),
        cc=pl.BlockSpec((2, 2 * HALF_W, HALF_CH), lambda i: (i, 0, 0)),
        vec=pl.BlockSpec((1, LANES), lambda i: (0, i)),
        out=pl.BlockSpec((seq, LANES), lambda i: (0, i)),
    )


def _ssm_fwd(proj, lam, bbcat, cccat, d_skip, d_ssm, u_col0, comm=None):
    seq = proj.shape[0]
    steps = seq // N_CHUNK

    def body(u_ref, lam_ref, bb_ref, cc_ref, d_ref, yp_ref, s_ref, tab_ref):
        for hh in range(2):
            lanes = slice(HALF_CH * hh, HALF_CH * (hh + 1))
            u_half = u_ref[:, lanes]
            ar, ai = _lam_rows(lam_ref, hh, HALF_W)
            _power_table(tab_ref, ar, ai, steps, HALF_W)
            s_ref[...] = _dot(u_half.astype(MXU_DTYPE), bb_ref[hh])
            fr, fi = _scan_fwd(s_ref, ar, ai, steps, HALF_W)
            pr, pi = _last_power(tab_ref, steps, HALF_W)
            cr, ci = _chunk_carry(fr, fi, pr, pi, _down)
            _patch_fwd(s_ref, tab_ref, cr, ci, steps, HALF_W)
            y = _dot(s_ref[...].astype(MXU_DTYPE), cc_ref[hh])
            yp_ref[:, lanes] = y + d_ref[:, lanes] * u_half

    sp = _ssm_specs(seq, u_col0 // LANES)
    return _call(
        body, comm, name="ssm_fwd", grid=(d_ssm // LANES,),
        in_specs=[sp["col"], sp["lam"], sp["bb"], sp["cc"], sp["vec"]], out_specs=[sp["out"]],
        out_shape=[jax.ShapeDtypeStruct((seq, d_ssm), F32)],
        scratch_shapes=[pltpu.VMEM((seq, 2 * HALF_W), F32), pltpu.VMEM((steps, 2 * HALF_W), F32)],
        operands=[proj, lam, bbcat, cccat, d_skip])


def _ssm_bwd(proj, dyp, lam, bbcat, cccat, d_skip, d_ssm, u_col0, comm=None):
    seq = proj.shape[0]
    steps = seq // N_CHUNK
    n_half = 2 * d_ssm // LANES
    width = HALF_W

    def body(u_ref, dyp_ref, lam_ref, bb_ref, cc_ref, d_ref, du_ref, dbb_ref, dcc_ref, da_ref, dd_ref,
             s_ref, g_ref, tab_ref):
        n_tiles = steps // N_CHUNK
        for hh in range(2):
            lanes = slice(HALF_CH * hh, HALF_CH * (hh + 1))
            u_half, dy_half = u_ref[:, lanes], dyp_ref[:, lanes].astype(F32)
            dy_mx = dy_half.astype(MXU_DTYPE)
            ar, ai = _lam_rows(lam_ref, hh, width)
            _power_table(tab_ref, ar, ai, steps, width)
            s_ref[...] = _dot(u_half.astype(MXU_DTYPE), bb_ref[hh])
            g_ref[...] = _dot_nt(dy_mx, cc_ref[hh])
            fr, fi, lr_, li_ = _scan_both(s_ref, g_ref, ar, ai, steps, width)
            pr, pi = _last_power(tab_ref, steps, width)
            cr, ci = _chunk_carry(fr, fi, pr, pi, _down)
            gr, gi = _chunk_carry(lr_, li_, pr, -pi, _up)

            def tile(m, carry):
                sr, si, accr, acci = carry
                t1r, t1i = _tile(tab_ref, m, width, 0)[...], _tile(tab_ref, m, width, 1)[...]
                mb = n_tiles - 1 - m
                t2r, t2i = _tile(tab_ref, mb, width, 0)[...], _tile(tab_ref, mb, width, 1)[...]
                for k in range(N_CHUNK):
                    j = m * N_CHUNK + k
                    xr, xi = _cmul(_row(t1r, k), _row(t1i, k), cr, ci)
                    nsr = _tile(s_ref, j, width, 0)[...] + xr
                    nsi = _tile(s_ref, j, width, 1)[...] + xi
                    _tile(s_ref, j, width, 0)[...] = nsr
                    _tile(s_ref, j, width, 1)[...] = nsi
                    qr, qi = _row(t2r, N_CHUNK - 1 - k), _row(t2i, N_CHUNK - 1 - k)
                    ngr = _tile(g_ref, j, width, 0)[...] + (qr * gr + qi * gi)
                    ngi = _tile(g_ref, j, width, 1)[...] + (qr * gi - qi * gr)
                    _tile(g_ref, j, width, 0)[...] = ngr
                    _tile(g_ref, j, width, 1)[...] = ngi
                    accr = accr + (sr * ngr + si * ngi)
                    acci = acci + (sr * ngi - si * ngr)
                    sr, si = nsr, nsi
                return sr, si, accr, acci

            z = jnp.zeros((N_CHUNK, width), F32)
            _, _, accr, acci = lax.fori_loop(0, n_tiles, tile, (cr, ci, z, z))
            da_ref[hh, :, 0:width] = jnp.sum(accr, axis=0, keepdims=True)
            da_ref[hh, :, width:2 * width] = jnp.sum(acci, axis=0, keepdims=True)

            g_mx = g_ref[...].astype(MXU_DTYPE)
            dcc_ref[hh] = _dot_tn(dy_mx, s_ref[...].astype(MXU_DTYPE)).T
            dbb_ref[hh] = _dot_tn(u_half.astype(MXU_DTYPE), g_mx)
            du = _dot_nt(g_mx, bb_ref[hh]) + d_ref[:, lanes] * dy_half
            du_ref[:, lanes] = du.astype(du_ref.dtype)
            dd_ref[:, lanes] = jnp.sum(dy_half * u_half, axis=0, keepdims=True)

    sp = _ssm_specs(seq, u_col0 // LANES)
    return _call(
        body, comm, name="ssm_bwd", grid=(d_ssm // LANES,),
        in_specs=[sp["col"], sp["out"], sp["lam"], sp["bb"], sp["cc"], sp["vec"]],
        out_specs=[sp["out"], sp["bb"], sp["cc"], pl.BlockSpec((2, 1, 2 * width), lambda i: (i, 0, 0)), sp["vec"]],
        out_shape=[jax.ShapeDtypeStruct((seq, d_ssm), MXU_DTYPE),
                   jax.ShapeDtypeStruct((n_half, HALF_CH, 2 * width), F32),
                   jax.ShapeDtypeStruct((n_half, 2 * width, HALF_CH), F32),
                   jax.ShapeDtypeStruct((n_half, 1, 2 * width), F32),
                   jax.ShapeDtypeStruct((1, d_ssm), F32)],
        scratch_shapes=[pltpu.VMEM((seq, 2 * width), F32), pltpu.VMEM((seq, 2 * width), F32),
                        pltpu.VMEM((steps, 2 * width), F32)],
        operands=[proj, dyp, lam, bbcat, cccat, d_skip])


def _tail(xp, t3, proj, yconv, yp, w_glu, b_glu, w_out, g_post, zs_col0, tm):
    seq, d_model = xp.shape
    d_conv, d_ssm = yconv.shape[1], yp.shape[1]
    d_mix = d_conv + d_ssm
    assert zs_col0 % d_ssm == 0

    def body(x_ref, t_ref, zs_ref, yc_ref, yp_ref, wglu_hbm, bglu_ref, wout_hbm, gpost_ref,
             dy_ref, do_ref, mixt_ref, dyc_ref, dyp_ref, dzs_ref, ygt_ref, dq_ref, loss_ref, dgpost_ref, dbglu_ref,
             wglu, wout):
        @pl.when(pl.program_id(0) == 0)
        def _():
            pltpu.sync_copy(wglu_hbm, wglu)
            pltpu.sync_copy(wout_hbm, wout)
            loss_ref[...] = jnp.zeros_like(loss_ref)
            dgpost_ref[...] = jnp.zeros_like(dgpost_ref)
            dbglu_ref[...] = jnp.zeros_like(dbglu_ref)

        a = yp_ref[...]
        th = jnp.tanh(GELU_C * (a + GELU_K * (a * a * a)))
        yg = a * (0.5 * (1.0 + th))
        dgelu = 0.5 * (1.0 + th) + 0.5 * a * (1.0 - th * th) * (GELU_C * (1.0 + 3.0 * GELU_K * a * a))
        yg_mx = yg.astype(MXU_DTYPE)
        sq = _sigmoid(_dot(yg_mx, wglu[...]) + bglu_ref[...])
        y2 = yg * sq
        zs = zs_ref[...]
        sz = _sigmoid(zs)
        silz = zs * sz
        yc, ys = yc_ref[...].astype(F32), y2 * silz
        mix = jnp.concatenate([yc, ys], axis=1).astype(MXU_DTYPE)
        mixt_ref[0:d_conv, :] = yc.T.astype(MXU_DTYPE)
        mixt_ref[d_conv:, :] = ys.T.astype(MXU_DTYPE)
        o = _dot(mix, wout[...])
        r2 = lax.rsqrt(jnp.mean(o * o, axis=-1, keepdims=True) + EPS)
        on = o * r2
        gpost = gpost_ref[...]
        err = (x_ref[...] + on * gpost) - _interleave(t_ref[...])
        loss_ref[...] += 0.5 * jnp.sum(jnp.mean(err * err, axis=-1, keepdims=True), axis=0, keepdims=True)
        dy = err * (1.0 / d_model)
        dy_ref[...] = dy
        dgpost_ref[...] += jnp.sum(dy * on, axis=0, keepdims=True)
        d_on = dy * gpost
        d_o = r2 * (d_on - on * jnp.mean(d_on * on, axis=-1, keepdims=True))
        do_mx = d_o.astype(MXU_DTYPE)
        do_ref[...] = do_mx
        d_mix_ = _dot_nt(do_mx, wout[...])
        dyc_ref[...] = d_mix_[:, :d_conv].astype(dyc_ref.dtype)
        d_yssm = d_mix_[:, d_conv:]
        d_y2 = d_yssm * silz
        dzs_ref[...] = (d_yssm * y2 * (sz * (1.0 + zs * (1.0 - sz)))).astype(dzs_ref.dtype)
        d_q = d_y2 * yg * (sq * (1.0 - sq))
        dq_mx = d_q.astype(MXU_DTYPE)
        dq_ref[...] = dq_mx
        ygt_ref[...] = yg.T.astype(MXU_DTYPE)
        dbglu_ref[...] += jnp.sum(d_q, axis=0, keepdims=True)
        d_yg = d_y2 * sq + _dot_nt(dq_mx, wglu[...])
        dyp_ref[...] = (d_yg * dgelu).astype(dyp_ref.dtype)

    def rows(width, col=0):
        return pl.BlockSpec((tm, width), lambda i: (i, col))

    def fixed(width):
        return pl.BlockSpec((1, width), lambda i: (0, 0))

    def cols(height):
        return pl.BlockSpec((height, tm), lambda i: (0, i))

    any_ = pl.BlockSpec(memory_space=pl.ANY)
    return pl.pallas_call(
        body, name="tail", grid=(seq // tm,),
        in_specs=[rows(d_model), _chunk_block(tm, d_model), rows(d_ssm, zs_col0 // d_ssm), rows(d_conv), rows(d_ssm),
                  any_, fixed(d_ssm), any_, fixed(d_model)],
        out_specs=[rows(d_model), rows(d_model), cols(d_mix), rows(d_conv), rows(d_ssm), rows(d_ssm), cols(d_ssm),
                   rows(d_ssm), fixed(LANES), fixed(d_model), fixed(d_ssm)],
        out_shape=[jax.ShapeDtypeStruct((seq, d_model), F32), jax.ShapeDtypeStruct((seq, d_model), MXU_DTYPE),
                   jax.ShapeDtypeStruct((d_mix, seq), MXU_DTYPE), jax.ShapeDtypeStruct((seq, d_conv), MXU_DTYPE),
                   jax.ShapeDtypeStruct((seq, d_ssm), MXU_DTYPE), jax.ShapeDtypeStruct((seq, d_ssm), MXU_DTYPE),
                   jax.ShapeDtypeStruct((d_ssm, seq), MXU_DTYPE), jax.ShapeDtypeStruct((seq, d_ssm), MXU_DTYPE),
                   jax.ShapeDtypeStruct((1, LANES), F32), jax.ShapeDtypeStruct((1, d_model), F32),
                   jax.ShapeDtypeStruct((1, d_ssm), F32)],
        scratch_shapes=[pltpu.VMEM(w_glu.shape, MXU_DTYPE), pltpu.VMEM(w_out.shape, MXU_DTYPE)],
        compiler_params=_params(("arbitrary",)),
    )(xp, t3, proj, yconv, yp, w_glu, b_glu, w_out, g_post)


def _bwd_in(d4, du, dzs, gr, win_g, xp, dy, g_pre, comm, tm):
    seq, d_model = xp.shape
    nb, _, nc = win_g.shape
    per = d4.shape[2] // gr

    def body(d4_ref, du_ref, dzs_ref, w_hbm, x_ref, dy_ref, g_ref, gx_ref, dg_ref, w_all, w_sems):
        def granule(g):
            p, cols = g // per, slice(g % per * gr, (g % per + 1) * gr)
            if p < 4:
                return d4_ref[p, :, cols]
            return du_ref[:, cols] if p == 4 else dzs_ref[:, cols]

        i = pl.program_id(0)
        loads = [pltpu.make_async_copy(w_hbm.at[k], w_all.at[k], w_sems.at[k]) for k in range(nb)]

        @pl.when(i == 0)
        def _():
            dg_ref[...] = jnp.zeros_like(dg_ref)
            for cp in loads:
                cp.start()

        dh = None
        for k in range(nb):
            @pl.when(i == 0)
            def _():
                loads[k].wait()

            dp = jnp.concatenate([granule(g) for g in range(k * nc // gr, (k + 1) * nc // gr)], axis=1)
            part = _dot_nt(dp, w_all[k])
            dh = part if dh is None else dh + part

        x = x_ref[...]
        r = lax.rsqrt(jnp.mean(x * x, axis=-1, keepdims=True) + EPS)
        xn = x * r
        dg_ref[...] += jnp.sum(dh * xn, axis=0, keepdims=True)
        dxn = dh * g_ref[...]
        gx_ref[...] = r * (dxn - xn * jnp.mean(dxn * xn, axis=-1, keepdims=True)) + dy_ref[...]

    row = pl.BlockSpec((tm, d_model), lambda i: (i, 0))
    vec = pl.BlockSpec((1, d_model), lambda i: (0, 0))
    return _call(
        body, comm, name="bwd_in", grid=(seq // tm,),
        in_specs=[pl.BlockSpec((4, tm, d4.shape[2]), lambda i: (0, i, 0)),
                  pl.BlockSpec((tm, du.shape[1]), lambda i: (i, 0)), pl.BlockSpec((tm, dzs.shape[1]), lambda i: (i, 0)),
                  pl.BlockSpec(memory_space=pl.ANY), row, row, vec],
        out_specs=[row, vec],
        out_shape=[jax.ShapeDtypeStruct((seq, d_model), F32), jax.ShapeDtypeStruct((1, d_model), F32)],
        scratch_shapes=[pltpu.VMEM(win_g.shape, win_g.dtype), pltpu.SemaphoreType.DMA((nb,))],
        operands=[d4, du, dzs, win_g, xp, dy, g_pre])


def _lookup(g, table):
    out = jnp.int32(table[0])
    for gi in range(1, len(table)):
        if table[gi] != table[gi - 1]:
            out = jnp.where(g >= gi, jnp.int32(table[gi]), out)
    return out


def _held(values, used):
    cur = next(v for v, u in zip(values, used) if u)
    out = []
    for v, u in zip(values, used):
        cur = v if u else cur
        out.append(cur)
    return out


def _dw_in(name, ht, d4, du, dzs, granules, gr, nc, tm, comm=None):
    d_model, seq = ht.shape
    per = d4.shape[2] // gr
    piece, col = [g // per for g in granules], [g % per for g in granules]
    sources = [(d4, [p < 4 for p in piece]), (du, [p == 4 for p in piece]), (dzs, [p == 5 for p in piece])]
    sources = [(a, used) for a, used in sources if any(used)]
    select = [next(s for s, (_, used) in enumerate(sources) if used[q]) for q in range(len(granules))]
    owner, place = [g * gr // nc for g in granules], [g * gr % nc // gr for g in granules]

    def body(a_ref, *refs):
        src_refs, o_ref = refs[:-1], refs[-1]
        j = pl.program_id(1)
        for s, ref in enumerate(src_refs):
            @pl.when(_lookup(j, select) == s)
            def _():
                o_ref[...] = _dot(a_ref[...], ref[...]).astype(o_ref.dtype)

    in_specs = [pl.BlockSpec((tm, seq), lambda i, j: (i, 0))]
    for a, used in sources:
        cols = _held(col, used)
        if a.ndim == 3:
            rows = _held(piece, used)
            in_specs.append(pl.BlockSpec((None, seq, gr), functools.partial(
                lambda i, j, rows, cols: (_lookup(j, rows), 0, _lookup(j, cols)), rows=rows, cols=cols)))
        else:
            in_specs.append(pl.BlockSpec((seq, gr), functools.partial(
                lambda i, j, cols: (0, _lookup(j, cols)), cols=cols)))
    return _call(
        body, comm, name=name, grid=(d_model // tm, len(granules)), in_specs=in_specs,
        out_specs=[pl.BlockSpec((None, tm, gr), lambda i, j: (_lookup(j, owner), i, _lookup(j, place)))],
        out_shape=[jax.ShapeDtypeStruct((N_DEV, d_model, nc), MXU_DTYPE)],
        operands=[ht] + [a for a, _ in sources])


def _wgrad(name, at, b, tm, tn, out_shape, out_block, out_index, comm=None):
    m, seq = at.shape
    n = b.shape[1]

    def body(a_ref, b_ref, o_ref):
        o_ref[...] = _dot(a_ref[...], b_ref[...]).astype(o_ref.dtype)

    return _call(
        body, comm, name=name, grid=(n // tn, m // tm),
        in_specs=[pl.BlockSpec((tm, seq), lambda j, i: (i, 0)), pl.BlockSpec((seq, tn), lambda j, i: (0, j))],
        out_specs=[pl.BlockSpec(out_block, lambda j, i: out_index(i, j))],
        out_shape=[jax.ShapeDtypeStruct(out_shape, MXU_DTYPE)],
        operands=[at, b])


def _eye_g():
    return jnp.eye(HALF_G, dtype=F32)


def _bb_blockdiag(bbt_r, bbt_i):
    n_half = bbt_r.shape[0] // HALF_G

    def one(t):
        t = t.reshape(n_half, HALF_G, SSM_GROUP, SSM_STATE)
        t = t[:, :, :, None, :] * _eye_g()[None, :, None, :, None]
        return t.reshape(n_half, HALF_CH, HALF_W)

    return jnp.concatenate([one(bbt_r), one(bbt_i)], axis=-1)


def _cc_blockdiag(c_re, c_im):
    n_half = c_re.shape[0] // HALF_G

    def one(t):
        t = t.reshape(n_half, HALF_G, SSM_GROUP, SSM_STATE)
        t = jnp.transpose(t, (0, 3, 1, 2))
        t = t[:, None, :, :, :] * _eye_g()[None, :, None, :, None]
        return t.reshape(n_half, HALF_W, HALF_CH)

    return jnp.concatenate([one(c_re), one(-c_im)], axis=1)


def _bb_diag(dbb):
    n_half = dbb.shape[0]
    t = dbb.reshape(n_half, HALF_G, SSM_GROUP, 2, HALF_G, SSM_STATE)
    t = jnp.sum(t * _eye_g()[None, :, None, None, :, None], axis=4)
    t = jnp.transpose(t, (3, 0, 1, 2, 4))
    return t.reshape(2, n_half * HALF_G, SSM_GROUP, SSM_STATE)


def _cc_diag(dcc):
    n_half = dcc.shape[0]
    t = dcc.reshape(n_half, 2, HALF_G, SSM_STATE, HALF_G, SSM_GROUP)
    t = jnp.sum(t * _eye_g()[None, None, :, None, :, None], axis=2)
    t = jnp.transpose(t, (1, 0, 3, 4, 2))
    return t.reshape(2, n_half * HALF_G, SSM_GROUP, SSM_STATE)


def _permute_rows(a):
    seq, d = a.shape
    return a.reshape(N_CHUNK, seq // N_CHUNK, d).transpose(1, 0, 2).reshape(seq, d)


def _unpermute_rows(a):
    seq, d = a.shape
    return a.reshape(seq // N_CHUNK, N_CHUNK, d).transpose(1, 0, 2).reshape(seq, d)


def _pack_rows(shape):
    return -(-math.prod(shape) // (8 * LANES)) * 8


def _pack(parts, dtype=F32):
    rows = []
    for p in parts:
        flat = p.reshape(-1).astype(dtype)
        rows.append(jnp.pad(flat, (0, _pack_rows(p.shape) * LANES - flat.shape[0])).reshape(-1, LANES))
    return jnp.concatenate(rows, axis=0)


def _unpack(packed, shapes):
    out, o = [], 0
    for s in shapes:
        n = _pack_rows(s)
        out.append(packed[o:o + n].reshape(-1)[:math.prod(s)].reshape(s))
        o += n
    return out


def kernel(x, norm_pre_g, w_in, conv_w, conv_b, ssm_a_re, ssm_a_im, ssm_log_dt, ssm_b_re, ssm_b_im, ssm_c_re, ssm_c_im, ssm_d, w_glu, b_glu, w_out, norm_post_g, loss_target, m_norm_pre_g, m_w_in, m_conv_w, m_conv_b, m_ssm_a_re, m_ssm_a_im, m_ssm_log_dt, m_ssm_b_re, m_ssm_b_im, m_ssm_c_re, m_ssm_c_im, m_ssm_d, m_w_glu, m_b_glu, m_w_out, m_norm_post_g, v_norm_pre_g, v_w_in, v_conv_w, v_conv_b, v_ssm_a_re, v_ssm_a_im, v_ssm_log_dt, v_ssm_b_re, v_ssm_b_im, v_ssm_c_re, v_ssm_c_im, v_ssm_d, v_w_glu, v_b_glu, v_w_out, v_norm_post_g):
    seq, d_model = x.shape[1], x.shape[2]
    d_conv, d_ssm = conv_b.shape[0], ssm_d.shape[0]
    groups, states = ssm_a_re.shape
    assert x.shape[0] == 1 and seq % (8 * N_CHUNK) == 0 and d_conv == d_ssm
    assert (groups, states) == (d_ssm // SSM_GROUP, SSM_STATE) and d_ssm % LANES == 0
    me = 4 * lax.axis_index("x") + 2 * lax.axis_index("y") + lax.axis_index("c")
    tm = min(512, seq)

    x3 = x[0].reshape(N_CHUNK, seq // N_CHUNK, d_model)
    t3 = loss_target[0].reshape(N_CHUNK, seq // N_CHUNK, d_model)
    row = lambda a: a.reshape(1, -1)
    conv_w8 = jnp.pad(conv_w, ((0, 8 - conv_w.shape[0]), (0, 0)))

    g3 = lambda a: a.reshape(groups, 1, -1)
    bt_re, bt_im = jnp.transpose(ssm_b_re, (0, 2, 1)), jnp.transpose(ssm_b_im, (0, 2, 1))
    lbr, lbi, qr, qi, bbt_r, bbt_i = _ssm_prep(g3(ssm_a_re), g3(ssm_a_im), g3(ssm_log_dt), bt_re, bt_im)
    n_half = groups // HALF_G
    lam = jnp.stack([lbr.reshape(n_half, HALF_W), lbi.reshape(n_half, HALF_W)], axis=1)
    bbcat = _bb_blockdiag(bbt_r, bbt_i).astype(MXU_DTYPE)
    cccat = _cc_blockdiag(ssm_c_re, ssm_c_im).astype(MXU_DTYPE)

    xp, h, ht = _norm_in(x3, row(norm_pre_g), tm)
    order = jnp.stack([jnp.bitwise_xor(me, r) for r in CONSUME_ORDER]).astype(jnp.int32)
    proj, win_g, (convw_g,) = _fwd_in(h, w_in.astype(MXU_DTYPE), order, _Comm([conv_w8]), tm)
    conv_w_full = jnp.transpose(convw_g, (1, 0, 2)).reshape(8, d_conv)
    u_col0, zs_col0 = 4 * d_conv, 4 * d_conv + d_ssm
    yconv = _conv_fwd(proj, conv_w_full, row(conv_b), d_conv)
    (yp,), (wout_g, wglu_g) = _ssm_fwd(proj, lam, bbcat, cccat, row(ssm_d), d_ssm, u_col0,
                                       _Comm([w_out.astype(MXU_DTYPE), w_glu.astype(MXU_DTYPE)]))
    w_out_full = wout_g.reshape(-1, d_model)
    w_glu_full = wglu_g.reshape(-1, d_ssm)
    (dy, d_o, mixt, dyc, dyp, dzs, ygt, dq, loss_part, dgpost, dbglu) = _tail(
        xp, t3, proj, yconv, yp, w_glu_full, row(b_glu), w_out_full, row(norm_post_g), zs_col0, min(256, seq))

    r_out, r_glu, nc = w_out.shape[0], w_glu.shape[0], w_in.shape[1]
    (dwout_p,), _ = _wgrad("dw_out", mixt, d_o, r_out, min(1024, d_model), (N_DEV, r_out, d_model),
                           (None, r_out, min(1024, d_model)), lambda i, j: (i, 0, j))
    (dwglu_p,), _ = _wgrad("dw_glu", ygt, dq, r_glu, d_ssm, (N_DEV, r_glu, d_ssm),
                           (None, r_glu, d_ssm), lambda i, j: (i, 0, 0))
    (d4, dconvb, dconvw), (recv_glu,) = _conv_bwd(proj, dyc, conv_w_full, row(conv_b), d_conv,
                                                  _Comm([], [dwglu_p]))
    late = [k for k in range(N_DEV) if k * nc < u_col0 + d_ssm and (k + 1) * nc > u_col0]
    early = [k for k in range(N_DEV) if k not in late]
    gr = math.gcd(nc, d_conv)
    granules = lambda blocks: [g for k in blocks for g in range(k * nc // gr, (k + 1) * nc // gr)]
    tmw = min(1024, d_model)
    (dwin_e,), (recv_out,) = _dw_in("dw_in_early", ht, d4, None, dzs, granules(early), gr, nc, tmw,
                                    _Comm([], [dwout_p]))
    (du, dbb, dcc, da, dd), (recv_in,) = _ssm_bwd(
        proj, dyp, lam, bbcat, cccat, row(ssm_d), d_ssm, u_col0, _Comm([], [dwin_e], dests={0: early}))
    parts_mx = [_bb_diag(dbb), _cc_diag(dcc)]
    (dwin_l,), (pack_mx_g,) = _dw_in("dw_in_late", ht, d4, du, dzs, granules(late), gr, nc, tmw,
                                     _Comm([_pack(parts_mx, MXU_DTYPE)]))
    da_n = jnp.transpose(da.reshape(n_half, 2, HALF_G, SSM_STATE), (1, 0, 2, 3)).reshape(2, groups, 1, states)
    parts = [dgpost, dconvb, dd, dbglu, dconvw[:3], da_n, loss_part]
    shapes, shapes_mx = [p.shape for p in parts], [p.shape for p in parts_mx]
    (gx_p, dgpre), (pack_g, recv_in) = _bwd_in(
        d4, du, dzs, gr, win_g, xp, dy, row(norm_pre_g),
        _Comm([_pack(parts)], [dwin_l], dests={1: late}, into={1: recv_in}), min(256, seq))
    (last_g,) = _exchange("reduce_last", [_pack([dgpre])], [])
    (g_gpost, g_convb, g_d, g_bglu, g_convw, g_da, loss_sum) = _unpack(_sum_slots("sum_pack", pack_g), shapes)
    (g_dbb, g_dcc) = _unpack(_sum_slots("sum_pack_mx", pack_mx_g), shapes_mx)
    (g_gpre,) = _unpack(_sum_slots("sum_last", last_g), [dgpre.shape])
    g_convw = lax.dynamic_slice(g_convw, (0, me * conv_w.shape[1]), conv_w.shape)

    tr = lambda a: jnp.transpose(a, (0, 2, 1))
    direct = [(g_gpre, row(norm_pre_g), row(m_norm_pre_g), row(v_norm_pre_g)),
              (g_convb, row(conv_b), row(m_conv_b), row(v_conv_b)),
              (g_d, row(ssm_d), row(m_ssm_d), row(v_ssm_d)),
              (g_bglu, row(b_glu), row(m_b_glu), row(v_b_glu)),
              (g_gpost, row(norm_post_g), row(m_norm_post_g), row(v_norm_post_g)),
              (g_convw, conv_w, m_conv_w, v_conv_w),
              (g_dcc[0], ssm_c_re, m_ssm_c_re, v_ssm_c_re),
              (-g_dcc[1], ssm_c_im, m_ssm_c_im, v_ssm_c_im)]
    ssm = dict(da_r=g_da[0], da_i=g_da[1], dbb_r=g_dbb[0], dbb_i=g_dbb[1], lr=g3(ssm_a_re), li=g3(ssm_a_im),
               ldt=g3(ssm_log_dt), bt_r=bt_re, bt_i=bt_im, lbr=lbr, lbi=lbi, qr=qr, qi=qi,
               w_a_re=g3(ssm_a_re), m_a_re=g3(m_ssm_a_re), v_a_re=g3(v_ssm_a_re),
               w_a_im=g3(ssm_a_im), m_a_im=g3(m_ssm_a_im), v_a_im=g3(v_ssm_a_im),
               w_log_dt=g3(ssm_log_dt), m_log_dt=g3(m_ssm_log_dt), v_log_dt=g3(v_ssm_log_dt),
               w_bt_re=bt_re, m_bt_re=tr(m_ssm_b_re), v_bt_re=tr(v_ssm_b_re),
               w_bt_im=bt_im, m_bt_im=tr(m_ssm_b_im), v_bt_im=tr(v_ssm_b_im))
    small = _small_update(direct, ssm)
    res = {}
    for name, quad, shape in zip(["norm_pre_g", "conv_b", "ssm_d", "b_glu", "norm_post_g", "conv_w", "ssm_c_re", "ssm_c_im"],
                                 small[:8], [norm_pre_g.shape, conv_b.shape, ssm_d.shape, b_glu.shape,
                                             norm_post_g.shape, conv_w.shape, ssm_c_re.shape, ssm_c_im.shape]):
        res[name] = tuple(a.reshape(shape) for a in quad)
    res["ssm_a_re"] = tuple(a.reshape(ssm_a_re.shape) for a in small[8])
    res["ssm_a_im"] = tuple(a.reshape(ssm_a_im.shape) for a in small[9])
    res["ssm_log_dt"] = tuple(a.reshape(ssm_log_dt.shape) for a in small[10])
    res["ssm_b_re"] = tuple(tr(a) for a in small[11])
    res["ssm_b_im"] = tuple(tr(a) for a in small[12])
    res["w_in"] = tuple(_adam_big("adam_w_in", recv_in, w_in, m_w_in, v_w_in, min(256, d_model)))
    res["w_out"] = tuple(_adam_big("adam_w_out", recv_out, w_out, m_w_out, v_w_out, min(128, r_out)))
    res["w_glu"] = tuple(_adam_big("adam_w_glu", recv_glu, w_glu, m_w_glu, v_w_glu, r_glu))

    order = ["norm_pre_g", "w_in", "conv_w", "conv_b", "ssm_a_re", "ssm_a_im", "ssm_log_dt", "ssm_b_re", "ssm_b_im",
             "ssm_c_re", "ssm_c_im", "ssm_d", "w_glu", "b_glu", "w_out", "norm_post_g"]
    loss = loss_sum[0, 0]
    grad_x = _unpermute_rows(gx_p)[None]
    return (loss, grad_x, *[res[n][0] for n in order], *[res[n][1] for n in order],
            *[res[n][2] for n in order], *[res[n][3] for n in order])
```

```python
import functools
import math

import jax
import jax.numpy as jnp
from jax import lax
from jax.experimental import pallas as pl
from jax.experimental.pallas import tpu as pltpu

F32 = jnp.float32
MXU_DTYPE = jnp.bfloat16
AXES = ("x", "y", "c")
N_DEV = 8
N_CHUNK = 8
LANES = 128
SSM_GROUP = 16
SSM_STATE = 64
HALF_CH = 64
HALF_G = HALF_CH // SSM_GROUP
HALF_W = HALF_G * SSM_STATE
EPS = 1e-6
ADAM_LR, ADAM_B1, ADAM_B2, ADAM_EPS, ADAM_WD, ADAM_STEP = 0.001, 0.9, 0.999, 1e-08, 0.01, 10
GELU_C = math.sqrt(2.0 / math.pi)
GELU_K = 0.044715
VMEM_LIMIT = 56 * 1024 * 1024


def _params(sem=None):
    return pltpu.CompilerParams(dimension_semantics=sem, vmem_limit_bytes=VMEM_LIMIT)


def _dot(a, b):
    return jnp.dot(a, b, preferred_element_type=F32)


def _dot_nt(a, b):
    return lax.dot_general(a, b, (((1,), (1,)), ((), ())), preferred_element_type=F32)


def _dot_tn(a, b):
    return lax.dot_general(a, b, (((0,), (0,)), ((), ())), preferred_element_type=F32)


def _sigmoid(z):
    return 1.0 / (1.0 + jnp.exp(-z))


def _flip(v, bit):
    return 1 - v if bit else v


def _peers():
    x, y, c = (lax.axis_index(a) for a in AXES)
    out = []
    for m in range(1, N_DEV):
        px, py, pc = _flip(x, (m >> 2) & 1), _flip(y, (m >> 1) & 1), _flip(c, m & 1)
        out.append((px, py, pc, 4 * px + 2 * py + pc))
    return out


class _Comm:
    def __init__(self, gathers=(), scatters=(), dests=None, into=None):
        self.n_g = len(gathers)
        self.operands = list(gathers) + list(scatters)
        self.n = len(self.operands)
        self.dests = dests or {}
        self.into = into or {}

    def out_shape(self):
        return [jax.ShapeDtypeStruct((N_DEV,) + a.shape if t < self.n_g else a.shape, a.dtype)
                for t, a in enumerate(self.operands)]

    def scratch(self):
        if not self.n:
            return []
        return [pltpu.SemaphoreType.DMA((self.n, N_DEV - 1)), pltpu.SemaphoreType.DMA((self.n, N_DEV - 1)),
                pltpu.SemaphoreType.DMA((self.n,))]

    def _copies(self, in_refs, out_refs, sems, arrivals):
        send_sems, recv_sems, local_sems = sems
        x, y, c = (lax.axis_index(a) for a in AXES)
        me = 4 * x + 2 * y + c

        def src(t, dev):
            return in_refs[t] if t < self.n_g else in_refs[t].at[dev]

        def member(t, dev):
            if t not in self.dests:
                return None
            return functools.reduce(jnp.logical_or, [dev == d for d in self.dests[t]])

        local = [(member(t, me), pltpu.make_async_copy(src(t, me), out_refs[t].at[me], local_sems.at[t]))
                 for t in range(self.n)]
        sends, recvs = [], []
        for t in range(self.n):
            for m, (px, py, pc, peer) in enumerate(_peers()):
                kw = dict(send_sem=send_sems.at[t, m], recv_sem=recv_sems.at[t, m],
                          device_id=(px, py, pc), device_id_type=pl.DeviceIdType.MESH)
                sends.append((member(t, peer), pltpu.make_async_remote_copy(
                    src_ref=src(t, peer), dst_ref=out_refs[t].at[me], **kw)))
                if arrivals:
                    recvs.append((member(t, me), pltpu.make_async_remote_copy(
                        src_ref=src(t, peer), dst_ref=out_refs[t].at[peer], **kw)))
        return local, sends, recvs

    @staticmethod
    def _do(cond, action):
        if cond is None:
            action()
        else:
            pl.when(cond)(action)

    def start(self, in_refs, out_refs, sems):
        local, sends, _ = self._copies(in_refs, out_refs, sems, arrivals=False)
        for cond, cp in local + sends:
            self._do(cond, cp.start)

    def finish(self, in_refs, out_refs, sems):
        local, sends, recvs = self._copies(in_refs, out_refs, sems, arrivals=True)
        for cond, cp in recvs:
            self._do(cond, cp.wait_recv)
        for cond, cp in sends:
            self._do(cond, cp.wait_send)
        for cond, cp in local:
            self._do(cond, cp.wait)


def _call(body, comm, *, name, grid, in_specs, out_specs, out_shape, operands, scratch_shapes=()):
    comm = comm or _Comm()
    n_in, n_out, n_scr, cn = len(in_specs), len(out_specs), len(scratch_shapes), comm.n
    landing = sorted(comm.into)
    aliases = {n_in + cn + q: n_out + t for q, t in enumerate(landing)}

    def wrapped(*refs):
        parts, o = [], 0
        for k in (n_in, cn, len(landing), n_out, cn, n_scr):
            parts.append(refs[o:o + k])
            o += k
        h_in, c_in, _, h_out, c_out, h_scr = parts
        sems = refs[o:]
        if cn:
            first = functools.reduce(jnp.logical_and, [pl.program_id(d) == 0 for d in range(len(grid))])

            @pl.when(first)
            def _():
                comm.start(c_in, c_out, sems)

        body(*h_in, *h_out, *h_scr)
        if cn:
            last = functools.reduce(jnp.logical_and, [pl.program_id(d) == grid[d] - 1 for d in range(len(grid))])

            @pl.when(last)
            def _():
                comm.finish(c_in, c_out, sems)

    any_ = pl.BlockSpec(memory_space=pl.ANY)
    res = pl.pallas_call(
        wrapped, name=name, grid=grid, in_specs=list(in_specs) + [any_] * (cn + len(landing)),
        out_specs=list(out_specs) + [any_] * cn,
        out_shape=list(out_shape) + comm.out_shape(), scratch_shapes=list(scratch_shapes) + comm.scratch(),
        input_output_aliases=aliases, compiler_params=_params(("arbitrary",) * len(grid)),
    )(*operands, *comm.operands, *[comm.into[t] for t in landing])
    return list(res[:n_out]), list(res[n_out:])


def _exchange(name, gathers, scatters):
    def body(tok_ref):
        tok_ref[...] = jnp.zeros_like(tok_ref)

    return _call(body, _Comm(gathers, scatters), name=name, grid=(1,), in_specs=[],
                 out_specs=[pl.BlockSpec((8, LANES), lambda i: (0, 0))],
                 out_shape=[jax.ShapeDtypeStruct((8, LANES), F32)], operands=[])[1]


def _ssm_prep(a_re, a_im, log_dt, bt_re, bt_im):
    def body(lr_ref, li_ref, ldt_ref, br_ref, bi_ref, lbr_ref, lbi_ref, qr_ref, qi_ref, bbr_ref, bbi_ref):
        lr, li = lr_ref[...], li_ref[...]
        dt = jnp.exp(ldt_ref[...])
        mag = jnp.exp(lr * dt)
        lbr, lbi = mag * jnp.cos(li * dt), mag * jnp.sin(li * dt)
        nr, ni = lbr - 1.0, lbi
        den = lr * lr + li * li
        qr = (nr * lr + ni * li) / den
        qi = (ni * lr - nr * li) / den
        br, bi = br_ref[...], bi_ref[...]
        lbr_ref[...], lbi_ref[...], qr_ref[...], qi_ref[...] = lbr, lbi, qr, qi
        bbr_ref[...] = qr * br - qi * bi
        bbi_ref[...] = qr * bi + qi * br

    s2 = jax.ShapeDtypeStruct(a_re.shape, F32)
    s3 = jax.ShapeDtypeStruct(bt_re.shape, F32)
    return pl.pallas_call(body, name="ssm_prep", out_shape=[s2, s2, s2, s2, s3, s3],
                          compiler_params=_params())(a_re, a_im, log_dt, bt_re, bt_im)


def _adam(w, g, m, v):
    m2 = ADAM_B1 * m + (1.0 - ADAM_B1) * g
    v2 = ADAM_B2 * v + (1.0 - ADAM_B2) * (g * g)
    m_hat = m2 / (1.0 - ADAM_B1 ** ADAM_STEP)
    v_hat = v2 / (1.0 - ADAM_B2 ** ADAM_STEP)
    delta = -ADAM_LR * (m_hat / (jnp.sqrt(v_hat) + ADAM_EPS) + ADAM_WD * w)
    return delta, m2, v2


def _small_update(direct, ssm):
    n_direct = len(direct)
    flat = [a for quad in direct for a in quad]
    names = ["da_r", "da_i", "dbb_r", "dbb_i", "lr", "li", "ldt", "bt_r", "bt_i", "lbr", "lbi", "qr", "qi"]
    flat += [ssm[k] for k in names]
    chain = ["a_re", "a_im", "log_dt", "bt_re", "bt_im"]
    for k in chain:
        flat += [ssm["w_" + k], ssm["m_" + k], ssm["v_" + k]]
    n_in = len(flat)

    def body(*refs):
        ins, outs = refs[:n_in], refs[n_in:]
        for p in range(n_direct):
            g, w, m, v = (r[...] for r in ins[4 * p:4 * p + 4])
            d, m2, v2 = _adam(w, g, m, v)
            outs[4 * p][...], outs[4 * p + 1][...], outs[4 * p + 2][...], outs[4 * p + 3][...] = g, d, m2, v2
        o = 4 * n_direct
        da_r, da_i, dbb_r, dbb_i, lr, li, ldt, bt_r, bt_i, lbr, lbi, qr, qi = (r[...] for r in ins[o:o + 13])
        dt = jnp.exp(ldt)
        g_br = qr * dbb_r + qi * dbb_i
        g_bi = qr * dbb_i - qi * dbb_r
        dq_r = jnp.sum(bt_r * dbb_r + bt_i * dbb_i, axis=1, keepdims=True)
        dq_i = jnp.sum(bt_r * dbb_i - bt_i * dbb_r, axis=1, keepdims=True)
        den = lr * lr + li * li
        cr, ci = lr / den, li / den
        gl_r = da_r + (cr * dq_r - ci * dq_i)
        gl_i = da_i + (cr * dq_i + ci * dq_r)
        w_r = qr * cr + qi * ci
        w_i = qi * cr - qr * ci
        g_lr = dt * (lbr * gl_r + lbi * gl_i) + (-w_r * dq_r - w_i * dq_i)
        g_li = dt * (lbr * gl_i - lbi * gl_r) + (-w_r * dq_i + w_i * dq_r)
        m_r = lr * lbr - li * lbi
        m_i = lr * lbi + li * lbr
        g_ldt = jnp.sum(m_r * gl_r + m_i * gl_i, axis=2, keepdims=True) * dt
        grads = [g_lr, g_li, g_ldt, g_br, g_bi]
        base_in, base_out = o + 13, 4 * n_direct
        for p, g in enumerate(grads):
            w, m, v = (r[...] for r in ins[base_in + 3 * p:base_in + 3 * p + 3])
            d, m2, v2 = _adam(w, g, m, v)
            q = base_out + 4 * p
            outs[q][...], outs[q + 1][...], outs[q + 2][...], outs[q + 3][...] = g, d, m2, v2

    out_shape = []
    for quad in direct:
        out_shape += [jax.ShapeDtypeStruct(quad[1].shape, F32)] * 4
    for k in chain:
        out_shape += [jax.ShapeDtypeStruct(ssm["w_" + k].shape, F32)] * 4
    res = pl.pallas_call(body, name="small_update", out_shape=out_shape, compiler_params=_params())(*flat)
    return [tuple(res[4 * p:4 * p + 4]) for p in range(n_direct + len(chain))]


def _sum_slots(name, pack):
    def body(p_ref, o_ref):
        acc = p_ref[0].astype(F32)
        for k in range(1, N_DEV):
            acc = acc + p_ref[k].astype(F32)
        o_ref[...] = acc

    return pl.pallas_call(body, name=name, out_shape=jax.ShapeDtypeStruct(pack.shape[1:], F32),
                          compiler_params=_params())(pack)


def _adam_big(name, recv, w, m, v, tr):
    _, rows, cols = recv.shape

    def body(r_ref, w_ref, m_ref, v_ref, g_ref, d_ref, m2_ref, v2_ref):
        g = r_ref[0].astype(F32)
        for k in range(1, N_DEV):
            g = g + r_ref[k].astype(F32)
        d, m2, v2 = _adam(w_ref[...], g, m_ref[...], v_ref[...])
        g_ref[...], d_ref[...], m2_ref[...], v2_ref[...] = g, d, m2, v2

    blk = pl.BlockSpec((tr, cols), lambda i: (i, 0))
    shp = jax.ShapeDtypeStruct((rows, cols), F32)
    return pl.pallas_call(
        body, name=name, grid=(rows // tr,),
        in_specs=[pl.BlockSpec((N_DEV, tr, cols), lambda i: (0, i, 0)), blk, blk, blk],
        out_specs=[blk] * 4, out_shape=[shp] * 4, compiler_params=_params(("parallel",)),
    )(recv, w, m, v)


def _chunk_block(tm, d):
    return pl.BlockSpec((N_CHUNK, tm // N_CHUNK, d), lambda i: (0, i, 0))


def _interleave(block):
    c, n, d = block.shape
    return pltpu.einshape("cjd->jcd", block).reshape(n * c, d)


def _norm_in(x3, g_pre, tm):
    _, steps, d_model = x3.shape
    seq = steps * N_CHUNK

    def body(x_ref, g_ref, xp_ref, h_ref, ht_ref):
        x = _interleave(x_ref[...])
        xp_ref[...] = x
        r = lax.rsqrt(jnp.mean(x * x, axis=-1, keepdims=True) + EPS)
        h = x * r * g_ref[...]
        h_ref[...] = h.astype(h_ref.dtype)
        ht_ref[...] = h.T.astype(ht_ref.dtype)

    rows = pl.BlockSpec((tm, d_model), lambda i: (i, 0))
    return pl.pallas_call(
        body, name="norm_in", grid=(seq // tm,),
        in_specs=[_chunk_block(tm, d_model), pl.BlockSpec((1, d_model), lambda i: (0, 0))],
        out_specs=[rows, rows, pl.BlockSpec((d_model, tm), lambda i: (0, i))],
        out_shape=[jax.ShapeDtypeStruct((seq, d_model), F32), jax.ShapeDtypeStruct((seq, d_model), MXU_DTYPE),
                   jax.ShapeDtypeStruct((d_model, seq), MXU_DTYPE)],
        compiler_params=_params(("parallel",)),
    )(x3, g_pre)


GATHER_ORDER = (0, 1, 4, 2, 6, 5, 3, 7)
CONSUME_ORDER = (0, 1, 4, 2, 5, 3, 6, 7)


def _fwd_in(h, w_shard, order, comm, tm):
    seq, d_model = h.shape
    nc = w_shard.shape[1]
    n_i = seq // tm
    cn = comm.n

    def body(order_ref, h_hbm, w_hbm, *rest):
        c_in, rest = rest[:cn], rest[cn:]
        proj_ref, wing = rest[0], rest[1]
        c_out, rest = rest[2:2 + cn], rest[2 + cn:]
        wbuf, send_sems, recv_sems, own_sem, load_sems, h_all, h_sems = rest[:7]
        c_sems = rest[7:]
        k, i = pl.program_id(0), pl.program_id(1)
        h_loads = [pltpu.make_async_copy(h_hbm.at[pl.ds(t * tm, tm)], h_all.at[pl.ds(t * tm, tm)], h_sems.at[t])
                   for t in range(n_i)]
        x, y, c = (lax.axis_index(a) for a in AXES)
        me = 4 * x + 2 * y + c

        def dev(rel):
            return _flip(x, (rel >> 2) & 1), _flip(y, (rel >> 1) & 1), _flip(c, rel & 1)

        def slot(rel):
            px, py, pc = dev(rel)
            return 4 * px + 2 * py + pc

        def remote(src, block, to_rel, sem):
            return pltpu.make_async_remote_copy(
                src_ref=src, dst_ref=wing.at[block], send_sem=send_sems.at[sem], recv_sem=recv_sems.at[sem],
                device_id=dev(to_rel), device_id_type=pl.DeviceIdType.MESH)

        own = pltpu.make_async_copy(w_hbm, wing.at[me], own_sem)
        first_hand = [remote(w_hbm, me, GATHER_ORDER[p], p - 1) for p in range(1, 4)]
        relay = {2: (c == 0, remote(wing.at[slot(4)], slot(4), 2, 3)),
                 3: (c == 1, remote(wing.at[slot(2)], slot(2), 4, 3))}
        passed_on = [remote(wing.at[slot(GATHER_ORDER[p])], slot(GATHER_ORDER[p]), 1, p + 2) for p in range(2, 5)]

        def load(q):
            return pltpu.make_async_copy(wing.at[slot(CONSUME_ORDER[q])], wbuf.at[q % 2], load_sems.at[q % 2])

        def take(q):
            p = GATHER_ORDER.index(CONSUME_ORDER[q])
            if p == 0:
                own.wait()
            else:
                remote(w_hbm, slot(GATHER_ORDER[p]), GATHER_ORDER[p], p - 1).wait_recv()
            if p in relay:
                pl.when(relay[p][0])(relay[p][1].start)
            if 2 <= p <= 4:
                passed_on[p - 2].start()
            load(q).start()

        @pl.when((k == 0) & (i == 0))
        def _():
            own.start()
            for cp in first_hand:
                cp.start()
            for cp in h_loads:
                cp.start()
            comm.start(c_in, c_out, c_sems)
            take(0)

        for t in range(n_i):
            pl.when((k == 0) & (i == t))(h_loads[t].wait)

        for q in range(N_DEV):
            @pl.when((k == q) & (i == 0))
            def _():
                load(q).wait()

            if q + 1 < N_DEV:
                @pl.when((k == q) & (i == n_i - 1))
                def _():
                    take(q + 1)

        proj_ref[...] = _dot(h_all[pl.ds(pl.multiple_of(i * tm, tm), tm), :], wbuf[k % 2])

        @pl.when((k == N_DEV - 1) & (i == n_i - 1))
        def _():
            for cp in first_hand + passed_on:
                cp.wait_send()
            for cond, cp in relay.values():
                pl.when(cond)(cp.wait_send)
            comm.finish(c_in, c_out, c_sems)

    any_ = pl.BlockSpec(memory_space=pl.ANY)
    grid_spec = pltpu.PrefetchScalarGridSpec(
        num_scalar_prefetch=1, grid=(N_DEV, n_i),
        in_specs=[any_, any_] + [any_] * cn,
        out_specs=[pl.BlockSpec((tm, nc), lambda k, i, o: (i, o[k])), any_] + [any_] * cn,
        scratch_shapes=[pltpu.VMEM((2, d_model, nc), w_shard.dtype), pltpu.SemaphoreType.DMA((N_DEV - 1,)),
                        pltpu.SemaphoreType.DMA((N_DEV - 1,)), pltpu.SemaphoreType.DMA, pltpu.SemaphoreType.DMA((2,)),
                        pltpu.VMEM((seq, d_model), h.dtype), pltpu.SemaphoreType.DMA((n_i,))]
        + comm.scratch())
    res = pl.pallas_call(
        body, name="fwd_in", grid_spec=grid_spec,
        out_shape=[jax.ShapeDtypeStruct((seq, N_DEV * nc), F32),
                   jax.ShapeDtypeStruct((N_DEV, d_model, nc), w_shard.dtype)] + comm.out_shape(),
        compiler_params=_params(("arbitrary", "arbitrary")),
    )(order, h, w_shard, *comm.operands)
    return res[0], res[1], list(res[2:])


def _shift_prev(a):
    n = a.shape[0]
    last = a[n - N_CHUNK:, :]
    row = lax.broadcasted_iota(jnp.int32, last.shape, 0)
    wrap = jnp.where(row == 0, 0.0, pltpu.roll(last, 1, axis=0))
    return jnp.concatenate([wrap, a[:n - N_CHUNK, :]], axis=0)


def _shift_next(a):
    first = a[:N_CHUNK, :]
    row = lax.broadcasted_iota(jnp.int32, first.shape, 0)
    wrap = jnp.where(row == N_CHUNK - 1, 0.0, pltpu.roll(first, N_CHUNK - 1, axis=0))
    return jnp.concatenate([a[N_CHUNK:, :], wrap], axis=0)


def _conv_specs(seq, d_conv):
    nblk = d_conv // LANES
    return [pl.BlockSpec((seq, LANES), functools.partial(lambda i, o: (0, o + i), o=q * nblk)) for q in range(4)]


def _conv_fwd(proj, conv_w8, conv_b, d_conv):
    seq = proj.shape[0]

    def body(bg_ref, cg_ref, v_ref, zc_ref, w_ref, b_ref, y_ref):
        cv = cg_ref[...] * v_ref[...]
        s1 = _shift_prev(cv)
        s2 = _shift_prev(s1)
        conv = b_ref[...] + w_ref[0:1, :] * s2 + w_ref[1:2, :] * s1 + w_ref[2:3, :] * cv
        z = zc_ref[...]
        y_ref[...] = (bg_ref[...] * conv * (z * _sigmoid(z))).astype(y_ref.dtype)

    col = pl.BlockSpec((seq, LANES), lambda i: (0, i))
    return pl.pallas_call(
        body, name="conv_fwd", grid=(d_conv // LANES,),
        in_specs=_conv_specs(seq, d_conv) + [pl.BlockSpec((8, LANES), lambda i: (0, i)), pl.BlockSpec((1, LANES), lambda i: (0, i))],
        out_specs=col, out_shape=jax.ShapeDtypeStruct((seq, d_conv), MXU_DTYPE),
        compiler_params=_params(("parallel",)),
    )(proj, proj, proj, proj, conv_w8, conv_b)


def _conv_bwd(proj, dyc, conv_w8, conv_b, d_conv, comm=None):
    seq = proj.shape[0]

    def body(bg_ref, cg_ref, v_ref, zc_ref, dy_ref, w_ref, b_ref, d4_ref, dcb_ref, dcw_ref):
        bg, cg, v, z = bg_ref[...], cg_ref[...], v_ref[...], zc_ref[...]
        w0, w1, w2 = w_ref[0:1, :], w_ref[1:2, :], w_ref[2:3, :]
        cv = cg * v
        s1 = _shift_prev(cv)
        s2 = _shift_prev(s1)
        conv = b_ref[...] + w0 * s2 + w1 * s1 + w2 * cv
        sig = _sigmoid(z)
        dy = dy_ref[...].astype(F32)
        g1 = dy * (z * sig)
        d_conv_ = g1 * bg
        d4_ref[0] = (g1 * conv).astype(d4_ref.dtype)
        d4_ref[3] = (dy * bg * conv * (sig * (1.0 + z * (1.0 - sig)))).astype(d4_ref.dtype)
        n1 = _shift_next(d_conv_)
        n2 = _shift_next(n1)
        d_cv = w2 * d_conv_ + w1 * n1 + w0 * n2
        d4_ref[1] = (d_cv * v).astype(d4_ref.dtype)
        d4_ref[2] = (d_cv * cg).astype(d4_ref.dtype)
        dcb_ref[...] = jnp.sum(d_conv_, axis=0, keepdims=True)
        rows = [jnp.sum(d_conv_ * s, axis=0, keepdims=True) for s in (s2, s1, cv)]
        dcw_ref[...] = jnp.concatenate(rows + [jnp.zeros((5, LANES), F32)], axis=0)

    col = pl.BlockSpec((seq, LANES), lambda i: (0, i))
    return _call(
        body, comm, name="conv_bwd", grid=(d_conv // LANES,),
        in_specs=_conv_specs(seq, d_conv) + [col, pl.BlockSpec((8, LANES), lambda i: (0, i)), pl.BlockSpec((1, LANES), lambda i: (0, i))],
        out_specs=[pl.BlockSpec((4, seq, LANES), lambda i: (0, 0, i)), pl.BlockSpec((1, LANES), lambda i: (0, i)),
                   pl.BlockSpec((8, LANES), lambda i: (0, i))],
        out_shape=[jax.ShapeDtypeStruct((4, seq, d_conv), MXU_DTYPE), jax.ShapeDtypeStruct((1, d_conv), F32),
                   jax.ShapeDtypeStruct((8, d_conv), F32)],
        operands=[proj, proj, proj, proj, dyc, conv_w8, conv_b])


def _cmul(ar, ai, br, bi):
    return ar * br - ai * bi, ar * bi + ai * br


def _cpow(ar, ai, n):
    rr, ri = jnp.ones_like(ar), jnp.zeros_like(ai)
    while n:
        if n & 1:
            rr, ri = _cmul(rr, ri, ar, ai)
        n >>= 1
        if n:
            ar, ai = _cmul(ar, ai, ar, ai)
    return rr, ri


def _down(v, k):
    row = lax.broadcasted_iota(jnp.int32, v.shape, 0)
    return jnp.where(row >= k, pltpu.roll(v, k, axis=0), 0.0)


def _up(v, k):
    row = lax.broadcasted_iota(jnp.int32, v.shape, 0)
    return jnp.where(row < N_CHUNK - k, pltpu.roll(v, N_CHUNK - k, axis=0), 0.0)


def _chunk_carry(fr, fi, mr, mi, shift):
    vr, vi = shift(fr, 1), shift(fi, 1)
    for k in (1, 2, 4):
        pr, pi = _cmul(mr, mi, shift(vr, k), shift(vi, k))
        vr, vi = vr + pr, vi + pi
        mr, mi = _cmul(mr, mi, mr, mi)
    return vr, vi


def _tile(ref, j, width, part):
    return ref.at[pl.ds(pl.multiple_of(j * N_CHUNK, N_CHUNK), N_CHUNK), pl.ds(part * width, width)]


def _row(t, k):
    return jnp.broadcast_to(t[k:k + 1, :], t.shape)


def _power_table(tab_ref, ar, ai, steps, width):
    e = lax.broadcasted_iota(jnp.int32, ar.shape, 0) + 1
    rr, ri = jnp.ones_like(ar), jnp.zeros_like(ai)
    br, bi = ar, ai
    for bit in range(4):
        mr, mi = _cmul(rr, ri, br, bi)
        take = ((e >> bit) & 1) == 1
        rr, ri = jnp.where(take, mr, rr), jnp.where(take, mi, ri)
        if bit < 3:
            br, bi = _cmul(br, bi, br, bi)
    _tile(tab_ref, 0, width, 0)[...] = rr
    _tile(tab_ref, 0, width, 1)[...] = ri

    def step(m, carry):
        tr, ti = _cmul(carry[0], carry[1], br, bi)
        _tile(tab_ref, m, width, 0)[...] = tr
        _tile(tab_ref, m, width, 1)[...] = ti
        return tr, ti

    lax.fori_loop(1, steps // N_CHUNK, step, (rr, ri))


def _last_power(tab_ref, steps, width):
    shape = (N_CHUNK, width)
    return (jnp.broadcast_to(tab_ref[steps - 1:steps, 0:width], shape),
            jnp.broadcast_to(tab_ref[steps - 1:steps, width:2 * width], shape))


def _scan_fwd(s_ref, ar, ai, steps, width):
    def step(j, carry):
        sr, si = carry
        nr = ar * sr - ai * si + _tile(s_ref, j, width, 0)[...]
        ni = ar * si + ai * sr + _tile(s_ref, j, width, 1)[...]
        _tile(s_ref, j, width, 0)[...] = nr
        _tile(s_ref, j, width, 1)[...] = ni
        return nr, ni

    z = jnp.zeros((N_CHUNK, width), F32)
    return lax.fori_loop(0, steps, step, (z, z), unroll=4)


def _scan_both(s_ref, g_ref, ar, ai, steps, width):
    def step(q, carry):
        sr, si, gr, gi = carry
        j, jb = q, steps - 1 - q
        nsr = ar * sr - ai * si + _tile(s_ref, j, width, 0)[...]
        nsi = ar * si + ai * sr + _tile(s_ref, j, width, 1)[...]
        ngr = ar * gr + ai * gi + _tile(g_ref, jb, width, 0)[...]
        ngi = ar * gi - ai * gr + _tile(g_ref, jb, width, 1)[...]
        _tile(s_ref, j, width, 0)[...] = nsr
        _tile(s_ref, j, width, 1)[...] = nsi
        _tile(g_ref, jb, width, 0)[...] = ngr
        _tile(g_ref, jb, width, 1)[...] = ngi
        return nsr, nsi, ngr, ngi

    z = jnp.zeros((N_CHUNK, width), F32)
    return lax.fori_loop(0, steps, step, (z, z, z, z), unroll=2)


def _patch_fwd(s_ref, tab_ref, cr, ci, steps, width):
    def tile(m, _):
        tr, ti = _tile(tab_ref, m, width, 0)[...], _tile(tab_ref, m, width, 1)[...]
        for k in range(N_CHUNK):
            fr, fi = _cmul(_row(tr, k), _row(ti, k), cr, ci)
            j = m * N_CHUNK + k
            _tile(s_ref, j, width, 0)[...] += fr
            _tile(s_ref, j, width, 1)[...] += fi
        return 0

    lax.fori_loop(0, steps // N_CHUNK, tile, 0)


def _lam_rows(lam_ref, hh, width):
    return (jnp.broadcast_to(lam_ref[hh, 0:1, :], (N_CHUNK, width)),
            jnp.broadcast_to(lam_ref[hh, 1:2, :], (N_CHUNK, width)))


def _ssm_specs(seq, col0):
    return dict(
        col=pl.BlockSpec((seq, LANES), lambda i: (0, col0 + i)),
        lam=pl.BlockSpec((2, 2, HALF_W), lambda i: (i, 0, 0)),
        bb=pl.BlockSpec((2, HALF_CH, 2 * HALF_W), lambda i: (i, 0, 0)),
        cc=pl.BlockSpec((2, 2 * HALF_W, HALF_CH), lambda i: (i, 0, 0)),
        vec=pl.BlockSpec((1, LANES), lambda i: (0, i)),
        out=pl.BlockSpec((seq, LANES), lambda i: (0, i)),
    )


def _ssm_fwd(proj, lam, bbcat, cccat, d_skip, d_ssm, u_col0, comm=None):
    seq = proj.shape[0]
    steps = seq // N_CHUNK

    def body(u_ref, lam_ref, bb_ref, cc_ref, d_ref, yp_ref, s_ref, tab_ref):
        for hh in range(2):
            lanes = slice(HALF_CH * hh, HALF_CH * (hh + 1))
            u_half = u_ref[:, lanes]
            ar, ai = _lam_rows(lam_ref, hh, HALF_W)
            _power_table(tab_ref, ar, ai, steps, HALF_W)
            s_ref[...] = _dot(u_half.astype(MXU_DTYPE), bb_ref[hh])
            fr, fi = _scan_fwd(s_ref, ar, ai, steps, HALF_W)
            pr, pi = _last_power(tab_ref, steps, HALF_W)
            cr, ci = _chunk_carry(fr, fi, pr, pi, _down)
            _patch_fwd(s_ref, tab_ref, cr, ci, steps, HALF_W)
            y = _dot(s_ref[...].astype(MXU_DTYPE), cc_ref[hh])
            yp_ref[:, lanes] = y + d_ref[:, lanes] * u_half

    sp = _ssm_specs(seq, u_col0 // LANES)
    return _call(
        body, comm, name="ssm_fwd", grid=(d_ssm // LANES,),
        in_specs=[sp["col"], sp["lam"], sp["bb"], sp["cc"], sp["vec"]], out_specs=[sp["out"]],
        out_shape=[jax.ShapeDtypeStruct((seq, d_ssm), F32)],
        scratch_shapes=[pltpu.VMEM((seq, 2 * HALF_W), F32), pltpu.VMEM((steps, 2 * HALF_W), F32)],
        operands=[proj, lam, bbcat, cccat, d_skip])


def _ssm_bwd(proj, dyp, lam, bbcat, cccat, d_skip, d_ssm, u_col0, comm=None):
    seq = proj.shape[0]
    steps = seq // N_CHUNK
    n_half = 2 * d_ssm // LANES
    width = HALF_W

    def body(u_ref, dyp_ref, lam_ref, bb_ref, cc_ref, d_ref, du_ref, dbb_ref, dcc_ref, da_ref, dd_ref,
             s_ref, g_ref, tab_ref):
        n_tiles = steps // N_CHUNK
        for hh in range(2):
            lanes = slice(HALF_CH * hh, HALF_CH * (hh + 1))
            u_half, dy_half = u_ref[:, lanes], dyp_ref[:, lanes].astype(F32)
            dy_mx = dy_half.astype(MXU_DTYPE)
            ar, ai = _lam_rows(lam_ref, hh, width)
            _power_table(tab_ref, ar, ai, steps, width)
            s_ref[...] = _dot(u_half.astype(MXU_DTYPE), bb_ref[hh])
            g_ref[...] = _dot_nt(dy_mx, cc_ref[hh])
            fr, fi, lr_, li_ = _scan_both(s_ref, g_ref, ar, ai, steps, width)
            pr, pi = _last_power(tab_ref, steps, width)
            cr, ci = _chunk_carry(fr, fi, pr, pi, _down)
            gr, gi = _chunk_carry(lr_, li_, pr, -pi, _up)

            def tile(m, carry):
                sr, si, accr, acci = carry
                t1r, t1i = _tile(tab_ref, m, width, 0)[...], _tile(tab_ref, m, width, 1)[...]
                mb = n_tiles - 1 - m
                t2r, t2i = _tile(tab_ref, mb, width, 0)[...], _tile(tab_ref, mb, width, 1)[...]
                for k in range(N_CHUNK):
                    j = m * N_CHUNK + k
                    xr, xi = _cmul(_row(t1r, k), _row(t1i, k), cr, ci)
                    nsr = _tile(s_ref, j, width, 0)[...] + xr
                    nsi = _tile(s_ref, j, width, 1)[...] + xi
                    _tile(s_ref, j, width, 0)[...] = nsr
                    _tile(s_ref, j, width, 1)[...] = nsi
                    qr, qi = _row(t2r, N_CHUNK - 1 - k), _row(t2i, N_CHUNK - 1 - k)
                    ngr = _tile(g_ref, j, width, 0)[...] + (qr * gr + qi * gi)
                    ngi = _tile(g_ref, j, width, 1)[...] + (qr * gi - qi * gr)
                    _tile(g_ref, j, width, 0)[...] = ngr
                    _tile(g_ref, j, width, 1)[...] = ngi
                    accr = accr + (sr * ngr + si * ngi)
                    acci = acci + (sr * ngi - si * ngr)
                    sr, si = nsr, nsi
                return sr, si, accr, acci

            z = jnp.zeros((N_CHUNK, width), F32)
            _, _, accr, acci = lax.fori_loop(0, n_tiles, tile, (cr, ci, z, z))
            da_ref[hh, :, 0:width] = jnp.sum(accr, axis=0, keepdims=True)
            da_ref[hh, :, width:2 * width] = jnp.sum(acci, axis=0, keepdims=True)

            g_mx = g_ref[...].astype(MXU_DTYPE)
            dcc_ref[hh] = _dot_tn(dy_mx, s_ref[...].astype(MXU_DTYPE)).T
            dbb_ref[hh] = _dot_tn(u_half.astype(MXU_DTYPE), g_mx)
            du = _dot_nt(g_mx, bb_ref[hh]) + d_ref[:, lanes] * dy_half
            du_ref[:, lanes] = du.astype(du_ref.dtype)
            dd_ref[:, lanes] = jnp.sum(dy_half * u_half, axis=0, keepdims=True)

    sp = _ssm_specs(seq, u_col0 // LANES)
    return _call(
        body, comm, name="ssm_bwd", grid=(d_ssm // LANES,),
        in_specs=[sp["col"], sp["out"], sp["lam"], sp["bb"], sp["cc"], sp["vec"]],
        out_specs=[sp["out"], sp["bb"], sp["cc"], pl.BlockSpec((2, 1, 2 * width), lambda i: (i, 0, 0)), sp["vec"]],
        out_shape=[jax.ShapeDtypeStruct((seq, d_ssm), MXU_DTYPE),
                   jax.ShapeDtypeStruct((n_half, HALF_CH, 2 * width), F32),
                   jax.ShapeDtypeStruct((n_half, 2 * width, HALF_CH), F32),
                   jax.ShapeDtypeStruct((n_half, 1, 2 * width), F32),
                   jax.ShapeDtypeStruct((1, d_ssm), F32)],
        scratch_shapes=[pltpu.VMEM((seq, 2 * width), F32), pltpu.VMEM((seq, 2 * width), F32),
                        pltpu.VMEM((steps, 2 * width), F32)],
        operands=[proj, dyp, lam, bbcat, cccat, d_skip])


def _tail(xp, t3, proj, yconv, yp, w_glu, b_glu, w_out, g_post, zs_col0, tm):
    seq, d_model = xp.shape
    d_conv, d_ssm = yconv.shape[1], yp.shape[1]
    d_mix = d_conv + d_ssm
    assert zs_col0 % d_ssm == 0

    def body(x_ref, t_ref, zs_ref, yc_ref, yp_ref, wglu_hbm, bglu_ref, wout_hbm, gpost_ref,
             dy_ref, do_ref, mixt_ref, dyc_ref, dyp_ref, dzs_ref, ygt_ref, dq_ref, loss_ref, dgpost_ref, dbglu_ref,
             wglu, wout):
        @pl.when(pl.program_id(0) == 0)
        def _():
            pltpu.sync_copy(wglu_hbm, wglu)
            pltpu.sync_copy(wout_hbm, wout)
            loss_ref[...] = jnp.zeros_like(loss_ref)
            dgpost_ref[...] = jnp.zeros_like(dgpost_ref)
            dbglu_ref[...] = jnp.zeros_like(dbglu_ref)

        a = yp_ref[...]
        th = jnp.tanh(GELU_C * (a + GELU_K * (a * a * a)))
        yg = a * (0.5 * (1.0 + th))
        dgelu = 0.5 * (1.0 + th) + 0.5 * a * (1.0 - th * th) * (GELU_C * (1.0 + 3.0 * GELU_K * a * a))
        yg_mx = yg.astype(MXU_DTYPE)
        sq = _sigmoid(_dot(yg_mx, wglu[...]) + bglu_ref[...])
        y2 = yg * sq
        zs = zs_ref[...]
        sz = _sigmoid(zs)
        silz = zs * sz
        yc, ys = yc_ref[...].astype(F32), y2 * silz
        mix = jnp.concatenate([yc, ys], axis=1).astype(MXU_DTYPE)
        mixt_ref[0:d_conv, :] = yc.T.astype(MXU_DTYPE)
        mixt_ref[d_conv:, :] = ys.T.astype(MXU_DTYPE)
        o = _dot(mix, wout[...])
        r2 = lax.rsqrt(jnp.mean(o * o, axis=-1, keepdims=True) + EPS)
        on = o * r2
        gpost = gpost_ref[...]
        err = (x_ref[...] + on * gpost) - _interleave(t_ref[...])
        loss_ref[...] += 0.5 * jnp.sum(jnp.mean(err * err, axis=-1, keepdims=True), axis=0, keepdims=True)
        dy = err * (1.0 / d_model)
        dy_ref[...] = dy
        dgpost_ref[...] += jnp.sum(dy * on, axis=0, keepdims=True)
        d_on = dy * gpost
        d_o = r2 * (d_on - on * jnp.mean(d_on * on, axis=-1, keepdims=True))
        do_mx = d_o.astype(MXU_DTYPE)
        do_ref[...] = do_mx
        d_mix_ = _dot_nt(do_mx, wout[...])
        dyc_ref[...] = d_mix_[:, :d_conv].astype(dyc_ref.dtype)
        d_yssm = d_mix_[:, d_conv:]
        d_y2 = d_yssm * silz
        dzs_ref[...] = (d_yssm * y2 * (sz * (1.0 + zs * (1.0 - sz)))).astype(dzs_ref.dtype)
        d_q = d_y2 * yg * (sq * (1.0 - sq))
        dq_mx = d_q.astype(MXU_DTYPE)
        dq_ref[...] = dq_mx
        ygt_ref[...] = yg.T.astype(MXU_DTYPE)
        dbglu_ref[...] += jnp.sum(d_q, axis=0, keepdims=True)
        d_yg = d_y2 * sq + _dot_nt(dq_mx, wglu[...])
        dyp_ref[...] = (d_yg * dgelu).astype(dyp_ref.dtype)

    def rows(width, col=0):
        return pl.BlockSpec((tm, width), lambda i: (i, col))

    def fixed(width):
        return pl.BlockSpec((1, width), lambda i: (0, 0))

    def cols(height):
        return pl.BlockSpec((height, tm), lambda i: (0, i))

    any_ = pl.BlockSpec(memory_space=pl.ANY)
    return pl.pallas_call(
        body, name="tail", grid=(seq // tm,),
        in_specs=[rows(d_model), _chunk_block(tm, d_model), rows(d_ssm, zs_col0 // d_ssm), rows(d_conv), rows(d_ssm),
                  any_, fixed(d_ssm), any_, fixed(d_model)],
        out_specs=[rows(d_model), rows(d_model), cols(d_mix), rows(d_conv), rows(d_ssm), rows(d_ssm), cols(d_ssm),
                   rows(d_ssm), fixed(LANES), fixed(d_model), fixed(d_ssm)],
        out_shape=[jax.ShapeDtypeStruct((seq, d_model), F32), jax.ShapeDtypeStruct((seq, d_model), MXU_DTYPE),
                   jax.ShapeDtypeStruct((d_mix, seq), MXU_DTYPE), jax.ShapeDtypeStruct((seq, d_conv), MXU_DTYPE),
                   jax.ShapeDtypeStruct((seq, d_ssm), MXU_DTYPE), jax.ShapeDtypeStruct((seq, d_ssm), MXU_DTYPE),
                   jax.ShapeDtypeStruct((d_ssm, seq), MXU_DTYPE), jax.ShapeDtypeStruct((seq, d_ssm), MXU_DTYPE),
                   jax.ShapeDtypeStruct((1, LANES), F32), jax.ShapeDtypeStruct((1, d_model), F32),
                   jax.ShapeDtypeStruct((1, d_ssm), F32)],
        scratch_shapes=[pltpu.VMEM(w_glu.shape, MXU_DTYPE), pltpu.VMEM(w_out.shape, MXU_DTYPE)],
        compiler_params=_params(("arbitrary",)),
    )(xp, t3, proj, yconv, yp, w_glu, b_glu, w_out, g_post)


def _bwd_in(d4, du, dzs, gr, win_g, xp, dy, g_pre, comm, tm):
    seq, d_model = xp.shape
    nb, _, nc = win_g.shape
    per = d4.shape[2] // gr

    def body(d4_ref, du_ref, dzs_ref, w_hbm, x_ref, dy_ref, g_ref, gx_ref, dg_ref, w_all, w_sems):
        def granule(g):
            p, cols = g // per, slice(g % per * gr, (g % per + 1) * gr)
            if p < 4:
                return d4_ref[p, :, cols]
            return du_ref[:, cols] if p == 4 else dzs_ref[:, cols]

        i = pl.program_id(0)
        loads = [pltpu.make_async_copy(w_hbm.at[k], w_all.at[k], w_sems.at[k]) for k in range(nb)]

        @pl.when(i == 0)
        def _():
            dg_ref[...] = jnp.zeros_like(dg_ref)
            for cp in loads:
                cp.start()

        dh = None
        for k in range(nb):
            @pl.when(i == 0)
            def _():
                loads[k].wait()

            dp = jnp.concatenate([granule(g) for g in range(k * nc // gr, (k + 1) * nc // gr)], axis=1)
            part = _dot_nt(dp, w_all[k])
            dh = part if dh is None else dh + part

        x = x_ref[...]
        r = lax.rsqrt(jnp.mean(x * x, axis=-1, keepdims=True) + EPS)
        xn = x * r
        dg_ref[...] += jnp.sum(dh * xn, axis=0, keepdims=True)
        dxn = dh * g_ref[...]
        gx_ref[...] = r * (dxn - xn * jnp.mean(dxn * xn, axis=-1, keepdims=True)) + dy_ref[...]

    row = pl.BlockSpec((tm, d_model), lambda i: (i, 0))
    vec = pl.BlockSpec((1, d_model), lambda i: (0, 0))
    return _call(
        body, comm, name="bwd_in", grid=(seq // tm,),
        in_specs=[pl.BlockSpec((4, tm, d4.shape[2]), lambda i: (0, i, 0)),
                  pl.BlockSpec((tm, du.shape[1]), lambda i: (i, 0)), pl.BlockSpec((tm, dzs.shape[1]), lambda i: (i, 0)),
                  pl.BlockSpec(memory_space=pl.ANY), row, row, vec],
        out_specs=[row, vec],
        out_shape=[jax.ShapeDtypeStruct((seq, d_model), F32), jax.ShapeDtypeStruct((1, d_model), F32)],
        scratch_shapes=[pltpu.VMEM(win_g.shape, win_g.dtype), pltpu.SemaphoreType.DMA((nb,))],
        operands=[d4, du, dzs, win_g, xp, dy, g_pre])


def _lookup(g, table):
    out = jnp.int32(table[0])
    for gi in range(1, len(table)):
        if table[gi] != table[gi - 1]:
            out = jnp.where(g >= gi, jnp.int32(table[gi]), out)
    return out


def _held(values, used):
    cur = next(v for v, u in zip(values, used) if u)
    out = []
    for v, u in zip(values, used):
        cur = v if u else cur
        out.append(cur)
    return out


def _dw_in(name, ht, d4, du, dzs, granules, gr, nc, tm, comm=None):
    d_model, seq = ht.shape
    per = d4.shape[2] // gr
    piece, col = [g // per for g in granules], [g % per for g in granules]
    sources = [(d4, [p < 4 for p in piece]), (du, [p == 4 for p in piece]), (dzs, [p == 5 for p in piece])]
    sources = [(a, used) for a, used in sources if any(used)]
    select = [next(s for s, (_, used) in enumerate(sources) if used[q]) for q in range(len(granules))]
    owner, place = [g * gr // nc for g in granules], [g * gr % nc // gr for g in granules]

    def body(a_ref, *refs):
        src_refs, o_ref = refs[:-1], refs[-1]
        j = pl.program_id(1)
        for s, ref in enumerate(src_refs):
            @pl.when(_lookup(j, select) == s)
            def _():
                o_ref[...] = _dot(a_ref[...], ref[...]).astype(o_ref.dtype)

    in_specs = [pl.BlockSpec((tm, seq), lambda i, j: (i, 0))]
    for a, used in sources:
        cols = _held(col, used)
        if a.ndim == 3:
            rows = _held(piece, used)
            in_specs.append(pl.BlockSpec((None, seq, gr), functools.partial(
                lambda i, j, rows, cols: (_lookup(j, rows), 0, _lookup(j, cols)), rows=rows, cols=cols)))
        else:
            in_specs.append(pl.BlockSpec((seq, gr), functools.partial(
                lambda i, j, cols: (0, _lookup(j, cols)), cols=cols)))
    return _call(
        body, comm, name=name, grid=(d_model // tm, len(granules)), in_specs=in_specs,
        out_specs=[pl.BlockSpec((None, tm, gr), lambda i, j: (_lookup(j, owner), i, _lookup(j, place)))],
        out_shape=[jax.ShapeDtypeStruct((N_DEV, d_model, nc), MXU_DTYPE)],
        operands=[ht] + [a for a, _ in sources])


def _wgrad(name, at, b, tm, tn, out_shape, out_block, out_index, comm=None):
    m, seq = at.shape
    n = b.shape[1]

    def body(a_ref, b_ref, o_ref):
        o_ref[...] = _dot(a_ref[...], b_ref[...]).astype(o_ref.dtype)

    return _call(
        body, comm, name=name, grid=(n // tn, m // tm),
        in_specs=[pl.BlockSpec((tm, seq), lambda j, i: (i, 0)), pl.BlockSpec((seq, tn), lambda j, i: (0, j))],
        out_specs=[pl.BlockSpec(out_block, lambda j, i: out_index(i, j))],
        out_shape=[jax.ShapeDtypeStruct(out_shape, MXU_DTYPE)],
        operands=[at, b])


def _eye_g():
    return jnp.eye(HALF_G, dtype=F32)


def _bb_blockdiag(bbt_r, bbt_i):
    n_half = bbt_r.shape[0] // HALF_G

    def one(t):
        t = t.reshape(n_half, HALF_G, SSM_GROUP, SSM_STATE)
        t = t[:, :, :, None, :] * _eye_g()[None, :, None, :, None]
        return t.reshape(n_half, HALF_CH, HALF_W)

    return jnp.concatenate([one(bbt_r), one(bbt_i)], axis=-1)


def _cc_blockdiag(c_re, c_im):
    n_half = c_re.shape[0] // HALF_G

    def one(t):
        t = t.reshape(n_half, HALF_G, SSM_GROUP, SSM_STATE)
        t = jnp.transpose(t, (0, 3, 1, 2))
        t = t[:, None, :, :, :] * _eye_g()[None, :, None, :, None]
        return t.reshape(n_half, HALF_W, HALF_CH)

    return jnp.concatenate([one(c_re), one(-c_im)], axis=1)


def _bb_diag(dbb):
    n_half = dbb.shape[0]
    t = dbb.reshape(n_half, HALF_G, SSM_GROUP, 2, HALF_G, SSM_STATE)
    t = jnp.sum(t * _eye_g()[None, :, None, None, :, None], axis=4)
    t = jnp.transpose(t, (3, 0, 1, 2, 4))
    return t.reshape(2, n_half * HALF_G, SSM_GROUP, SSM_STATE)


def _cc_diag(dcc):
    n_half = dcc.shape[0]
    t = dcc.reshape(n_half, 2, HALF_G, SSM_STATE, HALF_G, SSM_GROUP)
    t = jnp.sum(t * _eye_g()[None, None, :, None, :, None], axis=2)
    t = jnp.transpose(t, (1, 0, 3, 4, 2))
    return t.reshape(2, n_half * HALF_G, SSM_GROUP, SSM_STATE)


def _permute_rows(a):
    seq, d = a.shape
    return a.reshape(N_CHUNK, seq // N_CHUNK, d).transpose(1, 0, 2).reshape(seq, d)


def _unpermute_rows(a):
    seq, d = a.shape
    return a.reshape(seq // N_CHUNK, N_CHUNK, d).transpose(1, 0, 2).reshape(seq, d)


def _pack_rows(shape):
    return -(-math.prod(shape) // (8 * LANES)) * 8


def _pack(parts, dtype=F32):
    rows = []
    for p in parts:
        flat = p.reshape(-1).astype(dtype)
        rows.append(jnp.pad(flat, (0, _pack_rows(p.shape) * LANES - flat.shape[0])).reshape(-1, LANES))
    return jnp.concatenate(rows, axis=0)


def _unpack(packed, shapes):
    out, o = [], 0
    for s in shapes:
        n = _pack_rows(s)
        out.append(packed[o:o + n].reshape(-1)[:math.prod(s)].reshape(s))
        o += n
    return out


def kernel(x, norm_pre_g, w_in, conv_w, conv_b, ssm_a_re, ssm_a_im, ssm_log_dt, ssm_b_re, ssm_b_im, ssm_c_re, ssm_c_im, ssm_d, w_glu, b_glu, w_out, norm_post_g, loss_target, m_norm_pre_g, m_w_in, m_conv_w, m_conv_b, m_ssm_a_re, m_ssm_a_im, m_ssm_log_dt, m_ssm_b_re, m_ssm_b_im, m_ssm_c_re, m_ssm_c_im, m_ssm_d, m_w_glu, m_b_glu, m_w_out, m_norm_post_g, v_norm_pre_g, v_w_in, v_conv_w, v_conv_b, v_ssm_a_re, v_ssm_a_im, v_ssm_log_dt, v_ssm_b_re, v_ssm_b_im, v_ssm_c_re, v_ssm_c_im, v_ssm_d, v_w_glu, v_b_glu, v_w_out, v_norm_post_g):
    seq, d_model = x.shape[1], x.shape[2]
    d_conv, d_ssm = conv_b.shape[0], ssm_d.shape[0]
    groups, states = ssm_a_re.shape
    assert x.shape[0] == 1 and seq % (8 * N_CHUNK) == 0 and d_conv == d_ssm
    assert (groups, states) == (d_ssm // SSM_GROUP, SSM_STATE) and d_ssm % LANES == 0
    me = 4 * lax.axis_index("x") + 2 * lax.axis_index("y") + lax.axis_index("c")
    tm = min(512, seq)

    x3 = x[0].reshape(N_CHUNK, seq // N_CHUNK, d_model)
    t3 = loss_target[0].reshape(N_CHUNK, seq // N_CHUNK, d_model)
    row = lambda a: a.reshape(1, -1)
    conv_w8 = jnp.pad(conv_w, ((0, 8 - conv_w.shape[0]), (0, 0)))

    g3 = lambda a: a.reshape(groups, 1, -1)
    bt_re, bt_im = jnp.transpose(ssm_b_re, (0, 2, 1)), jnp.transpose(ssm_b_im, (0, 2, 1))
    lbr, lbi, qr, qi, bbt_r, bbt_i = _ssm_prep(g3(ssm_a_re), g3(ssm_a_im), g3(ssm_log_dt), bt_re, bt_im)
    n_half = groups // HALF_G
    lam = jnp.stack([lbr.reshape(n_half, HALF_W), lbi.reshape(n_half, HALF_W)], axis=1)
    bbcat = _bb_blockdiag(bbt_r, bbt_i).astype(MXU_DTYPE)
    cccat = _cc_blockdiag(ssm_c_re, ssm_c_im).astype(MXU_DTYPE)

    xp, h, ht = _norm_in(x3, row(norm_pre_g), tm)
    order = jnp.stack([jnp.bitwise_xor(me, r) for r in CONSUME_ORDER]).astype(jnp.int32)
    proj, win_g, (convw_g,) = _fwd_in(h, w_in.astype(MXU_DTYPE), order, _Comm([conv_w8]), tm)
    conv_w_full = jnp.transpose(convw_g, (1, 0, 2)).reshape(8, d_conv)
    u_col0, zs_col0 = 4 * d_conv, 4 * d_conv + d_ssm
    yconv = _conv_fwd(proj, conv_w_full, row(conv_b), d_conv)
    (yp,), (wout_g, wglu_g) = _ssm_fwd(proj, lam, bbcat, cccat, row(ssm_d), d_ssm, u_col0,
                                       _Comm([w_out.astype(MXU_DTYPE), w_glu.astype(MXU_DTYPE)]))
    w_out_full = wout_g.reshape(-1, d_model)
    w_glu_full = wglu_g.reshape(-1, d_ssm)
    (dy, d_o, mixt, dyc, dyp, dzs, ygt, dq, loss_part, dgpost, dbglu) = _tail(
        xp, t3, proj, yconv, yp, w_glu_full, row(b_glu), w_out_full, row(norm_post_g), zs_col0, min(256, seq))

    r_out, r_glu, nc = w_out.shape[0], w_glu.shape[0], w_in.shape[1]
    (dwout_p,), _ = _wgrad("dw_out", mixt, d_o, r_out, min(1024, d_model), (N_DEV, r_out, d_model),
                           (None, r_out, min(1024, d_model)), lambda i, j: (i, 0, j))
    (dwglu_p,), _ = _wgrad("dw_glu", ygt, dq, r_glu, d_ssm, (N_DEV, r_glu, d_ssm),
                           (None, r_glu, d_ssm), lambda i, j: (i, 0, 0))
    (d4, dconvb, dconvw), (recv_glu,) = _conv_bwd(proj, dyc, conv_w_full, row(conv_b), d_conv,
                                                  _Comm([], [dwglu_p]))
    late = [k for k in range(N_DEV) if k * nc < u_col0 + d_ssm and (k + 1) * nc > u_col0]
    early = [k for k in range(N_DEV) if k not in late]
    gr = math.gcd(nc, d_conv)
    granules = lambda blocks: [g for k in blocks for g in range(k * nc // gr, (k + 1) * nc // gr)]
    tmw = min(1024, d_model)
    (dwin_e,), (recv_out,) = _dw_in("dw_in_early", ht, d4, None, dzs, granules(early), gr, nc, tmw,
                                    _Comm([], [dwout_p]))
    (du, dbb, dcc, da, dd), (recv_in,) = _ssm_bwd(
        proj, dyp, lam, bbcat, cccat, row(ssm_d), d_ssm, u_col0, _Comm([], [dwin_e], dests={0: early}))
    parts_mx = [_bb_diag(dbb), _cc_diag(dcc)]
    (dwin_l,), (pack_mx_g,) = _dw_in("dw_in_late", ht, d4, du, dzs, granules(late), gr, nc, tmw,
                                     _Comm([_pack(parts_mx, MXU_DTYPE)]))
    da_n = jnp.transpose(da.reshape(n_half, 2, HALF_G, SSM_STATE), (1, 0, 2, 3)).reshape(2, groups, 1, states)
    parts = [dgpost, dconvb, dd, dbglu, dconvw[:3], da_n, loss_part]
    shapes, shapes_mx = [p.shape for p in parts], [p.shape for p in parts_mx]
    (gx_p, dgpre), (pack_g, recv_in) = _bwd_in(
        d4, du, dzs, gr, win_g, xp, dy, row(norm_pre_g),
        _Comm([_pack(parts)], [dwin_l], dests={1: late}, into={1: recv_in}), min(256, seq))
    (last_g,) = _exchange("reduce_last", [_pack([dgpre])], [])
    (g_gpost, g_convb, g_d, g_bglu, g_convw, g_da, loss_sum) = _unpack(_sum_slots("sum_pack", pack_g), shapes)
    (g_dbb, g_dcc) = _unpack(_sum_slots("sum_pack_mx", pack_mx_g), shapes_mx)
    (g_gpre,) = _unpack(_sum_slots("sum_last", last_g), [dgpre.shape])
    g_convw = lax.dynamic_slice(g_convw, (0, me * conv_w.shape[1]), conv_w.shape)

    tr = lambda a: jnp.transpose(a, (0, 2, 1))
    direct = [(g_gpre, row(norm_pre_g), row(m_norm_pre_g), row(v_norm_pre_g)),
              (g_convb, row(conv_b), row(m_conv_b), row(v_conv_b)),
              (g_d, row(ssm_d), row(m_ssm_d), row(v_ssm_d)),
              (g_bglu, row(b_glu), row(m_b_glu), row(v_b_glu)),
              (g_gpost, row(norm_post_g), row(m_norm_post_g), row(v_norm_post_g)),
              (g_convw, conv_w, m_conv_w, v_conv_w),
              (g_dcc[0], ssm_c_re, m_ssm_c_re, v_ssm_c_re),
              (-g_dcc[1], ssm_c_im, m_ssm_c_im, v_ssm_c_im)]
    ssm = dict(da_r=g_da[0], da_i=g_da[1], dbb_r=g_dbb[0], dbb_i=g_dbb[1], lr=g3(ssm_a_re), li=g3(ssm_a_im),
               ldt=g3(ssm_log_dt), bt_r=bt_re, bt_i=bt_im, lbr=lbr, lbi=lbi, qr=qr, qi=qi,
               w_a_re=g3(ssm_a_re), m_a_re=g3(m_ssm_a_re), v_a_re=g3(v_ssm_a_re),
               w_a_im=g3(ssm_a_im), m_a_im=g3(m_ssm_a_im), v_a_im=g3(v_ssm_a_im),
               w_log_dt=g3(ssm_log_dt), m_log_dt=g3(m_ssm_log_dt), v_log_dt=g3(v_ssm_log_dt),
               w_bt_re=bt_re, m_bt_re=tr(m_ssm_b_re), v_bt_re=tr(v_ssm_b_re),
               w_bt_im=bt_im, m_bt_im=tr(m_ssm_b_im), v_bt_im=tr(v_ssm_b_im))
    small = _small_update(direct, ssm)
    res = {}
    for name, quad, shape in zip(["norm_pre_g", "conv_b", "ssm_d", "b_glu", "norm_post_g", "conv_w", "ssm_c_re", "ssm_c_im"],
                                 small[:8], [norm_pre_g.shape, conv_b.shape, ssm_d.shape, b_glu.shape,
                                             norm_post_g.shape, conv_w.shape, ssm_c_re.shape, ssm_c_im.shape]):
        res[name] = tuple(a.reshape(shape) for a in quad)
    res["ssm_a_re"] = tuple(a.reshape(ssm_a_re.shape) for a in small[8])
    res["ssm_a_im"] = tuple(a.reshape(ssm_a_im.shape) for a in small[9])
    res["ssm_log_dt"] = tuple(a.reshape(ssm_log_dt.shape) for a in small[10])
    res["ssm_b_re"] = tuple(tr(a) for a in small[11])
    res["ssm_b_im"] = tuple(tr(a) for a in small[12])
    res["w_in"] = tuple(_adam_big("adam_w_in", recv_in, w_in, m_w_in, v_w_in, min(256, d_model)))
    res["w_out"] = tuple(_adam_big("adam_w_out", recv_out, w_out, m_w_out, v_w_out, min(128, r_out)))
    res["w_glu"] = tuple(_adam_big("adam_w_glu", recv_glu, w_glu, m_w_glu, v_w_glu, r_glu))

    order = ["norm_pre_g", "w_in", "conv_w", "conv_b", "ssm_a_re", "ssm_a_im", "ssm_log_dt", "ssm_b_re", "ssm_b_im",
             "ssm_c_re", "ssm_c_im", "ssm_d", "w_glu", "b_glu", "w_out", "norm_post_g"]
    loss = loss_sum[0, 0]
    grad_x = _unpermute_rows(gx_p)[None]
    return (loss, grad_x, *[res[n][0] for n in order], *[res[n][1] for n in order],
            *[res[n][2] for n in order], *[res[n][3] for n in order])
```

```python
import functools
import math

import jax
import jax.numpy as jnp
from jax import lax
from jax.experimental import pallas as pl
from jax.experimental.pallas import tpu as pltpu

F32 = jnp.float32
MXU_DTYPE = jnp.bfloat16
PROJ_DTYPE = jnp.bfloat16
AXES = ("x", "y", "c")
N_DEV = 8
N_CHUNK = 8
LANES = 128
SSM_GROUP = 16
SSM_STATE = 64
HALF_CH = 64
HALF_G = HALF_CH // SSM_GROUP
HALF_W = HALF_G * SSM_STATE
EPS = 1e-6
ADAM_LR, ADAM_B1, ADAM_B2, ADAM_EPS, ADAM_WD, ADAM_STEP = 0.001, 0.9, 0.999, 1e-08, 0.01, 10
GELU_C = math.sqrt(2.0 / math.pi)
GELU_K = 0.044715
VMEM_LIMIT = 56 * 1024 * 1024


def _params(sem=None):
    return pltpu.CompilerParams(dimension_semantics=sem, vmem_limit_bytes=VMEM_LIMIT)


def _dot(a, b):
    return jnp.dot(a, b, preferred_element_type=F32)


def _dot_nt(a, b):
    return lax.dot_general(a, b, (((1,), (1,)), ((), ())), preferred_element_type=F32)


def _dot_tn(a, b):
    return lax.dot_general(a, b, (((0,), (0,)), ((), ())), preferred_element_type=F32)


def _sigmoid(z):
    return 1.0 / (1.0 + jnp.exp(-z))


def _flip(v, bit):
    return 1 - v if bit else v


def _peers():
    x, y, c = (lax.axis_index(a) for a in AXES)
    out = []
    for m in range(1, N_DEV):
        px, py, pc = _flip(x, (m >> 2) & 1), _flip(y, (m >> 1) & 1), _flip(c, m & 1)
        out.append((px, py, pc, 4 * px + 2 * py + pc))
    return out


class _Comm:
    def __init__(self, gathers=(), scatters=(), dests=None, into=None):
        self.n_g = len(gathers)
        self.operands = list(gathers) + list(scatters)
        self.n = len(self.operands)
        self.dests = dests or {}
        self.into = into or {}

    def out_shape(self):
        return [jax.ShapeDtypeStruct((N_DEV,) + a.shape if t < self.n_g else a.shape, a.dtype)
                for t, a in enumerate(self.operands)]

    def scratch(self):
        if not self.n:
            return []
        return [pltpu.SemaphoreType.DMA((self.n, N_DEV - 1)), pltpu.SemaphoreType.DMA((self.n, N_DEV - 1)),
                pltpu.SemaphoreType.DMA((self.n,))]

    def _copies(self, in_refs, out_refs, sems, arrivals):
        send_sems, recv_sems, local_sems = sems
        x, y, c = (lax.axis_index(a) for a in AXES)
        me = 4 * x + 2 * y + c

        def src(t, dev):
            return in_refs[t] if t < self.n_g else in_refs[t].at[dev]

        def member(t, dev):
            if t not in self.dests:
                return None
            return functools.reduce(jnp.logical_or, [dev == d for d in self.dests[t]])

        local = [(member(t, me), pltpu.make_async_copy(src(t, me), out_refs[t].at[me], local_sems.at[t]))
                 for t in range(self.n)]
        sends, recvs = [], []
        for t in range(self.n):
            for m, (px, py, pc, peer) in enumerate(_peers()):
                kw = dict(send_sem=send_sems.at[t, m], recv_sem=recv_sems.at[t, m],
                          device_id=(px, py, pc), device_id_type=pl.DeviceIdType.MESH)
                sends.append((member(t, peer), pltpu.make_async_remote_copy(
                    src_ref=src(t, peer), dst_ref=out_refs[t].at[me], **kw)))
                if arrivals:
                    recvs.append((member(t, me), pltpu.make_async_remote_copy(
                        src_ref=src(t, peer), dst_ref=out_refs[t].at[peer], **kw)))
        return local, sends, recvs

    @staticmethod
    def _do(cond, action):
        if cond is None:
            action()
        else:
            pl.when(cond)(action)

    def start(self, in_refs, out_refs, sems):
        local, sends, _ = self._copies(in_refs, out_refs, sems, arrivals=False)
        for cond, cp in local + sends:
            self._do(cond, cp.start)

    def finish(self, in_refs, out_refs, sems):
        local, sends, recvs = self._copies(in_refs, out_refs, sems, arrivals=True)
        for cond, cp in recvs:
            self._do(cond, cp.wait_recv)
        for cond, cp in sends:
            self._do(cond, cp.wait_send)
        for cond, cp in local:
            self._do(cond, cp.wait)


def _call(body, comm, *, name, grid, in_specs, out_specs, out_shape, operands, scratch_shapes=()):
    comm = comm or _Comm()
    n_in, n_out, n_scr, cn = len(in_specs), len(out_specs), len(scratch_shapes), comm.n
    landing = sorted(comm.into)
    aliases = {n_in + cn + q: n_out + t for q, t in enumerate(landing)}

    def wrapped(*refs):
        parts, o = [], 0
        for k in (n_in, cn, len(landing), n_out, cn, n_scr):
            parts.append(refs[o:o + k])
            o += k
        h_in, c_in, _, h_out, c_out, h_scr = parts
        sems = refs[o:]
        if cn:
            first = functools.reduce(jnp.logical_and, [pl.program_id(d) == 0 for d in range(len(grid))])

            @pl.when(first)
            def _():
                comm.start(c_in, c_out, sems)

        body(*h_in, *h_out, *h_scr)
        if cn:
            last = functools.reduce(jnp.logical_and, [pl.program_id(d) == grid[d] - 1 for d in range(len(grid))])

            @pl.when(last)
            def _():
                comm.finish(c_in, c_out, sems)

    any_ = pl.BlockSpec(memory_space=pl.ANY)
    res = pl.pallas_call(
        wrapped, name=name, grid=grid, in_specs=list(in_specs) + [any_] * (cn + len(landing)),
        out_specs=list(out_specs) + [any_] * cn,
        out_shape=list(out_shape) + comm.out_shape(), scratch_shapes=list(scratch_shapes) + comm.scratch(),
        input_output_aliases=aliases, compiler_params=_params(("arbitrary",) * len(grid)),
    )(*operands, *comm.operands, *[comm.into[t] for t in landing])
    return list(res[:n_out]), list(res[n_out:])


def _exchange(name, gathers, scatters):
    def body(tok_ref):
        tok_ref[...] = jnp.zeros_like(tok_ref)

    return _call(body, _Comm(gathers, scatters), name=name, grid=(1,), in_specs=[],
                 out_specs=[pl.BlockSpec((8, LANES), lambda i: (0, 0))],
                 out_shape=[jax.ShapeDtypeStruct((8, LANES), F32)], operands=[])[1]


def _ssm_prep(a_re, a_im, log_dt, bt_re, bt_im):
    def body(lr_ref, li_ref, ldt_ref, br_ref, bi_ref, lbr_ref, lbi_ref, qr_ref, qi_ref, bbr_ref, bbi_ref):
        lr, li = lr_ref[...], li_ref[...]
        dt = jnp.exp(ldt_ref[...])
        mag = jnp.exp(lr * dt)
        lbr, lbi = mag * jnp.cos(li * dt), mag * jnp.sin(li * dt)
        nr, ni = lbr - 1.0, lbi
        den = lr * lr + li * li
        qr = (nr * lr + ni * li) / den
        qi = (ni * lr - nr * li) / den
        br, bi = br_ref[...], bi_ref[...]
        lbr_ref[...], lbi_ref[...], qr_ref[...], qi_ref[...] = lbr, lbi, qr, qi
        bbr_ref[...] = qr * br - qi * bi
        bbi_ref[...] = qr * bi + qi * br

    s2 = jax.ShapeDtypeStruct(a_re.shape, F32)
    s3 = jax.ShapeDtypeStruct(bt_re.shape, F32)
    return pl.pallas_call(body, name="ssm_prep", out_shape=[s2, s2, s2, s2, s3, s3],
                          compiler_params=_params())(a_re, a_im, log_dt, bt_re, bt_im)


def _adam(w, g, m, v):
    m2 = ADAM_B1 * m + (1.0 - ADAM_B1) * g
    v2 = ADAM_B2 * v + (1.0 - ADAM_B2) * (g * g)
    m_hat = m2 / (1.0 - ADAM_B1 ** ADAM_STEP)
    v_hat = v2 / (1.0 - ADAM_B2 ** ADAM_STEP)
    delta = -ADAM_LR * (m_hat / (jnp.sqrt(v_hat) + ADAM_EPS) + ADAM_WD * w)
    return delta, m2, v2


def _small_update(direct, ssm):
    n_direct = len(direct)
    flat = [a for quad in direct for a in quad]
    names = ["da_r", "da_i", "dbb_r", "dbb_i", "lr", "li", "ldt", "bt_r", "bt_i", "lbr", "lbi", "qr", "qi"]
    flat += [ssm[k] for k in names]
    chain = ["a_re", "a_im", "log_dt", "bt_re", "bt_im"]
    for k in chain:
        flat += [ssm["w_" + k], ssm["m_" + k], ssm["v_" + k]]
    n_in = len(flat)

    def body(*refs):
        ins, outs = refs[:n_in], refs[n_in:]
        for p in range(n_direct):
            g, w, m, v = (r[...] for r in ins[4 * p:4 * p + 4])
            d, m2, v2 = _adam(w, g, m, v)
            outs[4 * p][...], outs[4 * p + 1][...], outs[4 * p + 2][...], outs[4 * p + 3][...] = g, d, m2, v2
        o = 4 * n_direct
        da_r, da_i, dbb_r, dbb_i, lr, li, ldt, bt_r, bt_i, lbr, lbi, qr, qi = (r[...] for r in ins[o:o + 13])
        dt = jnp.exp(ldt)
        g_br = qr * dbb_r + qi * dbb_i
        g_bi = qr * dbb_i - qi * dbb_r
        dq_r = jnp.sum(bt_r * dbb_r + bt_i * dbb_i, axis=1, keepdims=True)
        dq_i = jnp.sum(bt_r * dbb_i - bt_i * dbb_r, axis=1, keepdims=True)
        den = lr * lr + li * li
        cr, ci = lr / den, li / den
        gl_r = da_r + (cr * dq_r - ci * dq_i)
        gl_i = da_i + (cr * dq_i + ci * dq_r)
        w_r = qr * cr + qi * ci
        w_i = qi * cr - qr * ci
        g_lr = dt * (lbr * gl_r + lbi * gl_i) + (-w_r * dq_r - w_i * dq_i)
        g_li = dt * (lbr * gl_i - lbi * gl_r) + (-w_r * dq_i + w_i * dq_r)
        m_r = lr * lbr - li * lbi
        m_i = lr * lbi + li * lbr
        g_ldt = jnp.sum(m_r * gl_r + m_i * gl_i, axis=2, keepdims=True) * dt
        grads = [g_lr, g_li, g_ldt, g_br, g_bi]
        base_in, base_out = o + 13, 4 * n_direct
        for p, g in enumerate(grads):
            w, m, v = (r[...] for r in ins[base_in + 3 * p:base_in + 3 * p + 3])
            d, m2, v2 = _adam(w, g, m, v)
            q = base_out + 4 * p
            outs[q][...], outs[q + 1][...], outs[q + 2][...], outs[q + 3][...] = g, d, m2, v2

    out_shape = []
    for quad in direct:
        out_shape += [jax.ShapeDtypeStruct(quad[1].shape, F32)] * 4
    for k in chain:
        out_shape += [jax.ShapeDtypeStruct(ssm["w_" + k].shape, F32)] * 4
    res = pl.pallas_call(body, name="small_update", out_shape=out_shape, compiler_params=_params())(*flat)
    return [tuple(res[4 * p:4 * p + 4]) for p in range(n_direct + len(chain))]


def _sum_slots(name, pack):
    def body(p_ref, o_ref):
        acc = p_ref[0].astype(F32)
        for k in range(1, N_DEV):
            acc = acc + p_ref[k].astype(F32)
        o_ref[...] = acc

    return pl.pallas_call(body, name=name, out_shape=jax.ShapeDtypeStruct(pack.shape[1:], F32),
                          compiler_params=_params())(pack)


def _adam_big(name, recv, w, m, v, tr):
    _, rows, cols = recv.shape

    def body(r_ref, w_ref, m_ref, v_ref, g_ref, d_ref, m2_ref, v2_ref):
        g = r_ref[0].astype(F32)
        for k in range(1, N_DEV):
            g = g + r_ref[k].astype(F32)
        d, m2, v2 = _adam(w_ref[...], g, m_ref[...], v_ref[...])
        g_ref[...], d_ref[...], m2_ref[...], v2_ref[...] = g, d, m2, v2

    blk = pl.BlockSpec((tr, cols), lambda i: (i, 0))
    shp = jax.ShapeDtypeStruct((rows, cols), F32)
    return pl.pallas_call(
        body, name=name, grid=(rows // tr,),
        in_specs=[pl.BlockSpec((N_DEV, tr, cols), lambda i: (0, i, 0)), blk, blk, blk],
        out_specs=[blk] * 4, out_shape=[shp] * 4, compiler_params=_params(("parallel",)),
    )(recv, w, m, v)


def _chunk_block(tm, d):
    return pl.BlockSpec((N_CHUNK, tm // N_CHUNK, d), lambda i: (0, i, 0))


def _interleave(block):
    c, n, d = block.shape
    return pltpu.einshape("cjd->jcd", block).reshape(n * c, d)


def _norm_in(x3, g_pre, tm):
    _, steps, d_model = x3.shape
    seq = steps * N_CHUNK

    def body(x_ref, g_ref, xp_ref, h_ref, ht_ref):
        x = _interleave(x_ref[...])
        xp_ref[...] = x
        r = lax.rsqrt(jnp.mean(x * x, axis=-1, keepdims=True) + EPS)
        h = x * r * g_ref[...]
        h_ref[...] = h.astype(h_ref.dtype)
        ht_ref[...] = h.T.astype(ht_ref.dtype)

    rows = pl.BlockSpec((tm, d_model), lambda i: (i, 0))
    return pl.pallas_call(
        body, name="norm_in", grid=(seq // tm,),
        in_specs=[_chunk_block(tm, d_model), pl.BlockSpec((1, d_model), lambda i: (0, 0))],
        out_specs=[rows, rows, pl.BlockSpec((d_model, tm), lambda i: (0, i))],
        out_shape=[jax.ShapeDtypeStruct((seq, d_model), F32), jax.ShapeDtypeStruct((seq, d_model), MXU_DTYPE),
                   jax.ShapeDtypeStruct((d_model, seq), MXU_DTYPE)],
        compiler_params=_params(("parallel",)),
    )(x3, g_pre)


GATHER_ORDER = (0, 1, 4, 2, 6, 5, 3, 7)
CONSUME_ORDER = (0, 1, 4, 2, 5, 3, 6, 7)


def _fwd_in(h, w_shard, order, comm, tm):
    seq, d_model = h.shape
    nc = w_shard.shape[1]
    n_i = seq // tm
    cn = comm.n

    def body(order_ref, h_hbm, w_hbm, *rest):
        c_in, rest = rest[:cn], rest[cn:]
        proj_ref, wing = rest[0], rest[1]
        c_out, rest = rest[2:2 + cn], rest[2 + cn:]
        wbuf, send_sems, recv_sems, own_sem, load_sems, h_all, h_sems = rest[:7]
        c_sems = rest[7:]
        k, i = pl.program_id(0), pl.program_id(1)
        h_loads = [pltpu.make_async_copy(h_hbm.at[pl.ds(t * tm, tm)], h_all.at[pl.ds(t * tm, tm)], h_sems.at[t])
                   for t in range(n_i)]
        x, y, c = (lax.axis_index(a) for a in AXES)
        me = 4 * x + 2 * y + c

        def dev(rel):
            return _flip(x, (rel >> 2) & 1), _flip(y, (rel >> 1) & 1), _flip(c, rel & 1)

        def slot(rel):
            px, py, pc = dev(rel)
            return 4 * px + 2 * py + pc

        def remote(src, block, to_rel, sem):
            return pltpu.make_async_remote_copy(
                src_ref=src, dst_ref=wing.at[block], send_sem=send_sems.at[sem], recv_sem=recv_sems.at[sem],
                device_id=dev(to_rel), device_id_type=pl.DeviceIdType.MESH)

        own = pltpu.make_async_copy(w_hbm, wing.at[me], own_sem)
        first_hand = [remote(w_hbm, me, GATHER_ORDER[p], p - 1) for p in range(1, 4)]
        relay = {2: (c == 0, remote(wing.at[slot(4)], slot(4), 2, 3)),
                 3: (c == 1, remote(wing.at[slot(2)], slot(2), 4, 3))}
        passed_on = [remote(wing.at[slot(GATHER_ORDER[p])], slot(GATHER_ORDER[p]), 1, p + 2) for p in range(2, 5)]

        def load(q):
            return pltpu.make_async_copy(wing.at[slot(CONSUME_ORDER[q])], wbuf.at[q % 2], load_sems.at[q % 2])

        def take(q):
            p = GATHER_ORDER.index(CONSUME_ORDER[q])
            if p == 0:
                own.wait()
            else:
                remote(w_hbm, slot(GATHER_ORDER[p]), GATHER_ORDER[p], p - 1).wait_recv()
            if p in relay:
                pl.when(relay[p][0])(relay[p][1].start)
            if 2 <= p <= 4:
                passed_on[p - 2].start()
            load(q).start()

        @pl.when((k == 0) & (i == 0))
        def _():
            own.start()
            for cp in first_hand:
                cp.start()
            for cp in h_loads:
                cp.start()
            comm.start(c_in, c_out, c_sems)
            take(0)

        for t in range(n_i):
            pl.when((k == 0) & (i == t))(h_loads[t].wait)

        for q in range(N_DEV):
            @pl.when((k == q) & (i == 0))
            def _():
                load(q).wait()

            if q + 1 < N_DEV:
                @pl.when((k == q) & (i == n_i - 1))
                def _():
                    take(q + 1)

        proj_ref[...] = _dot(h_all[pl.ds(pl.multiple_of(i * tm, tm), tm), :], wbuf[k % 2]).astype(proj_ref.dtype)

        @pl.when((k == N_DEV - 1) & (i == n_i - 1))
        def _():
            for cp in first_hand + passed_on:
                cp.wait_send()
            for cond, cp in relay.values():
                pl.when(cond)(cp.wait_send)
            comm.finish(c_in, c_out, c_sems)

    any_ = pl.BlockSpec(memory_space=pl.ANY)
    grid_spec = pltpu.PrefetchScalarGridSpec(
        num_scalar_prefetch=1, grid=(N_DEV, n_i),
        in_specs=[any_, any_] + [any_] * cn,
        out_specs=[pl.BlockSpec((tm, nc), lambda k, i, o: (i, o[k])), any_] + [any_] * cn,
        scratch_shapes=[pltpu.VMEM((2, d_model, nc), w_shard.dtype), pltpu.SemaphoreType.DMA((N_DEV - 1,)),
                        pltpu.SemaphoreType.DMA((N_DEV - 1,)), pltpu.SemaphoreType.DMA, pltpu.SemaphoreType.DMA((2,)),
                        pltpu.VMEM((seq, d_model), h.dtype), pltpu.SemaphoreType.DMA((n_i,))]
        + comm.scratch())
    res = pl.pallas_call(
        body, name="fwd_in", grid_spec=grid_spec,
        out_shape=[jax.ShapeDtypeStruct((seq, N_DEV * nc), PROJ_DTYPE),
                   jax.ShapeDtypeStruct((N_DEV, d_model, nc), w_shard.dtype)] + comm.out_shape(),
        compiler_params=_params(("arbitrary", "arbitrary")),
    )(order, h, w_shard, *comm.operands)
    return res[0], res[1], list(res[2:])


def _shift_prev(a):
    n = a.shape[0]
    last = a[n - N_CHUNK:, :]
    row = lax.broadcasted_iota(jnp.int32, last.shape, 0)
    wrap = jnp.where(row == 0, 0.0, pltpu.roll(last, 1, axis=0))
    return jnp.concatenate([wrap, a[:n - N_CHUNK, :]], axis=0)


def _shift_next(a):
    first = a[:N_CHUNK, :]
    row = lax.broadcasted_iota(jnp.int32, first.shape, 0)
    wrap = jnp.where(row == N_CHUNK - 1, 0.0, pltpu.roll(first, N_CHUNK - 1, axis=0))
    return jnp.concatenate([a[N_CHUNK:, :], wrap], axis=0)


def _conv_specs(seq, d_conv):
    nblk = d_conv // LANES
    return [pl.BlockSpec((seq, LANES), functools.partial(lambda i, o: (0, o + i), o=q * nblk)) for q in range(4)]


def _conv_fwd(proj, conv_w8, conv_b, d_conv):
    seq = proj.shape[0]

    def body(bg_ref, cg_ref, v_ref, zc_ref, w_ref, b_ref, y_ref):
        cv = cg_ref[...].astype(F32) * v_ref[...].astype(F32)
        s1 = _shift_prev(cv)
        s2 = _shift_prev(s1)
        conv = b_ref[...] + w_ref[0:1, :] * s2 + w_ref[1:2, :] * s1 + w_ref[2:3, :] * cv
        z = zc_ref[...].astype(F32)
        y_ref[...] = (bg_ref[...].astype(F32) * conv * (z * _sigmoid(z))).astype(y_ref.dtype)

    col = pl.BlockSpec((seq, LANES), lambda i: (0, i))
    return pl.pallas_call(
        body, name="conv_fwd", grid=(d_conv // LANES,),
        in_specs=_conv_specs(seq, d_conv) + [pl.BlockSpec((8, LANES), lambda i: (0, i)), pl.BlockSpec((1, LANES), lambda i: (0, i))],
        out_specs=col, out_shape=jax.ShapeDtypeStruct((seq, d_conv), MXU_DTYPE),
        compiler_params=_params(("parallel",)),
    )(proj, proj, proj, proj, conv_w8, conv_b)


def _conv_bwd(proj, dyc, conv_w8, conv_b, d_conv, comm=None):
    seq = proj.shape[0]

    def body(bg_ref, cg_ref, v_ref, zc_ref, dy_ref, w_ref, b_ref, d4_ref, dcb_ref, dcw_ref):
        bg, cg, v, z = (r[...].astype(F32) for r in (bg_ref, cg_ref, v_ref, zc_ref))
        w0, w1, w2 = w_ref[0:1, :], w_ref[1:2, :], w_ref[2:3, :]
        cv = cg * v
        s1 = _shift_prev(cv)
        s2 = _shift_prev(s1)
        conv = b_ref[...] + w0 * s2 + w1 * s1 + w2 * cv
        sig = _sigmoid(z)
        dy = dy_ref[...].astype(F32)
        g1 = dy * (z * sig)
        d_conv_ = g1 * bg
        d4_ref[0] = (g1 * conv).astype(d4_ref.dtype)
        d4_ref[3] = (dy * bg * conv * (sig * (1.0 + z * (1.0 - sig)))).astype(d4_ref.dtype)
        n1 = _shift_next(d_conv_)
        n2 = _shift_next(n1)
        d_cv = w2 * d_conv_ + w1 * n1 + w0 * n2
        d4_ref[1] = (d_cv * v).astype(d4_ref.dtype)
        d4_ref[2] = (d_cv * cg).astype(d4_ref.dtype)
        dcb_ref[...] = jnp.sum(d_conv_, axis=0, keepdims=True)
        rows = [jnp.sum(d_conv_ * s, axis=0, keepdims=True) for s in (s2, s1, cv)]
        dcw_ref[...] = jnp.concatenate(rows + [jnp.zeros((5, LANES), F32)], axis=0)

    col = pl.BlockSpec((seq, LANES), lambda i: (0, i))
    return _call(
        body, comm, name="conv_bwd", grid=(d_conv // LANES,),
        in_specs=_conv_specs(seq, d_conv) + [col, pl.BlockSpec((8, LANES), lambda i: (0, i)), pl.BlockSpec((1, LANES), lambda i: (0, i))],
        out_specs=[pl.BlockSpec((4, seq, LANES), lambda i: (0, 0, i)), pl.BlockSpec((1, LANES), lambda i: (0, i)),
                   pl.BlockSpec((8, LANES), lambda i: (0, i))],
        out_shape=[jax.ShapeDtypeStruct((4, seq, d_conv), MXU_DTYPE), jax.ShapeDtypeStruct((1, d_conv), F32),
                   jax.ShapeDtypeStruct((8, d_conv), F32)],
        operands=[proj, proj, proj, proj, dyc, conv_w8, conv_b])


def _cmul(ar, ai, br, bi):
    return ar * br - ai * bi, ar * bi + ai * br


def _cpow(ar, ai, n):
    rr, ri = jnp.ones_like(ar), jnp.zeros_like(ai)
    while n:
        if n & 1:
            rr, ri = _cmul(rr, ri, ar, ai)
        n >>= 1
        if n:
            ar, ai = _cmul(ar, ai, ar, ai)
    return rr, ri


def _down(v, k):
    row = lax.broadcasted_iota(jnp.int32, v.shape, 0)
    return jnp.where(row >= k, pltpu.roll(v, k, axis=0), 0.0)


def _up(v, k):
    row = lax.broadcasted_iota(jnp.int32, v.shape, 0)
    return jnp.where(row < N_CHUNK - k, pltpu.roll(v, N_CHUNK - k, axis=0), 0.0)


def _chunk_carry(fr, fi, mr, mi, shift):
    vr, vi = shift(fr, 1), shift(fi, 1)
    for k in (1, 2, 4):
        pr, pi = _cmul(mr, mi, shift(vr, k), shift(vi, k))
        vr, vi = vr + pr, vi + pi
        mr, mi = _cmul(mr, mi, mr, mi)
    return vr, vi


def _tile(ref, j, width, part):
    return ref.at[pl.ds(pl.multiple_of(j * N_CHUNK, N_CHUNK), N_CHUNK), pl.ds(part * width, width)]


def _row(t, k):
    return jnp.broadcast_to(t[k:k + 1, :], t.shape)


def _power_table(tab_ref, ar, ai, steps, width):
    e = lax.broadcasted_iota(jnp.int32, ar.shape, 0) + 1
    rr, ri = jnp.ones_like(ar), jnp.zeros_like(ai)
    br, bi = ar, ai
    for bit in range(4):
        mr, mi = _cmul(rr, ri, br, bi)
        take = ((e >> bit) & 1) == 1
        rr, ri = jnp.where(take, mr, rr), jnp.where(take, mi, ri)
        if bit < 3:
            br, bi = _cmul(br, bi, br, bi)
    _tile(tab_ref, 0, width, 0)[...] = rr
    _tile(tab_ref, 0, width, 1)[...] = ri

    def step(m, carry):
        tr, ti = _cmul(carry[0], carry[1], br, bi)
        _tile(tab_ref, m, width, 0)[...] = tr
        _tile(tab_ref, m, width, 1)[...] = ti
        return tr, ti

    lax.fori_loop(1, steps // N_CHUNK, step, (rr, ri))


def _last_power(tab_ref, steps, width):
    shape = (N_CHUNK, width)
    return (jnp.broadcast_to(tab_ref[steps - 1:steps, 0:width], shape),
            jnp.broadcast_to(tab_ref[steps - 1:steps, width:2 * width], shape))


def _scan_fwd(s_ref, ar, ai, steps, width):
    def step(j, carry):
        sr, si = carry
        nr = ar * sr - ai * si + _tile(s_ref, j, width, 0)[...]
        ni = ar * si + ai * sr + _tile(s_ref, j, width, 1)[...]
        _tile(s_ref, j, width, 0)[...] = nr
        _tile(s_ref, j, width, 1)[...] = ni
        return nr, ni

    z = jnp.zeros((N_CHUNK, width), F32)
    return lax.fori_loop(0, steps, step, (z, z), unroll=4)


def _scan_both(s_ref, g_ref, ar, ai, steps, width):
    def step(q, carry):
        sr, si, gr, gi = carry
        j, jb = q, steps - 1 - q
        nsr = ar * sr - ai * si + _tile(s_ref, j, width, 0)[...]
        nsi = ar * si + ai * sr + _tile(s_ref, j, width, 1)[...]
        ngr = ar * gr + ai * gi + _tile(g_ref, jb, width, 0)[...]
        ngi = ar * gi - ai * gr + _tile(g_ref, jb, width, 1)[...]
        _tile(s_ref, j, width, 0)[...] = nsr
        _tile(s_ref, j, width, 1)[...] = nsi
        _tile(g_ref, jb, width, 0)[...] = ngr
        _tile(g_ref, jb, width, 1)[...] = ngi
        return nsr, nsi, ngr, ngi

    z = jnp.zeros((N_CHUNK, width), F32)
    return lax.fori_loop(0, steps, step, (z, z, z, z), unroll=2)


def _patch_fwd(s_ref, tab_ref, cr, ci, steps, width):
    def tile(m, _):
        tr, ti = _tile(tab_ref, m, width, 0)[...], _tile(tab_ref, m, width, 1)[...]
        for k in range(N_CHUNK):
            fr, fi = _cmul(_row(tr, k), _row(ti, k), cr, ci)
            j = m * N_CHUNK + k
            _tile(s_ref, j, width, 0)[...] += fr
            _tile(s_ref, j, width, 1)[...] += fi
        return 0

    lax.fori_loop(0, steps // N_CHUNK, tile, 0)


def _lam_rows(lam_ref, hh, width):
    return (jnp.broadcast_to(lam_ref[hh, 0:1, :], (N_CHUNK, width)),
            jnp.broadcast_to(lam_ref[hh, 1:2, :], (N_CHUNK, width)))


def _ssm_specs(seq, col0):
    return dict(
        col=pl.BlockSpec((seq, LANES), lambda i: (0, col0 + i)),
        lam=pl.BlockSpec((2, 2, HALF_W), lambda i: (i, 0, 0)),
        bb=pl.BlockSpec((2, HALF_CH, 2 * HALF_W), lambda i: (i, 0, 0)),
        cc=pl.BlockSpec((2, 2 * HALF_W, HALF_CH), lambda i: (i, 0, 0)),
        vec=pl.BlockSpec((1, LANES), lambda i: (0, i)),
        out=pl.BlockSpec((seq, LANES), lambda i: (0, i)),
    )


def _ssm_fwd(proj, lam, bbcat, cccat, d_skip, d_ssm, u_col0, comm=None):
    seq = proj.shape[0]
    steps = seq // N_CHUNK

    def body(u_ref, lam_ref, bb_ref, cc_ref, d_ref, yp_ref, s_ref, tab_ref):
        for hh in range(2):
            lanes = slice(HALF_CH * hh, HALF_CH * (hh + 1))
            u_half = u_ref[:, lanes].astype(F32)
            ar, ai = _lam_rows(lam_ref, hh, HALF_W)
            _power_table(tab_ref, ar, ai, steps, HALF_W)
            s_ref[...] = _dot(u_half.astype(MXU_DTYPE), bb_ref[hh])
            fr, fi = _scan_fwd(s_ref, ar, ai, steps, HALF_W)
            pr, pi = _last_power(tab_ref, steps, HALF_W)
            cr, ci = _chunk_carry(fr, fi, pr, pi, _down)
            _patch_fwd(s_ref, tab_ref, cr, ci, steps, HALF_W)
            y = _dot(s_ref[...].astype(MXU_DTYPE), cc_ref[hh])
            yp_ref[:, lanes] = y + d_ref[:, lanes] * u_half

    sp = _ssm_specs(seq, u_col0 // LANES)
    return _call(
        body, comm, name="ssm_fwd", grid=(d_ssm // LANES,),
        in_specs=[sp["col"], sp["lam"], sp["bb"], sp["cc"], sp["vec"]], out_specs=[sp["out"]],
        out_shape=[jax.ShapeDtypeStruct((seq, d_ssm), F32)],
        scratch_shapes=[pltpu.VMEM((seq, 2 * HALF_W), F32), pltpu.VMEM((steps, 2 * HALF_W), F32)],
        operands=[proj, lam, bbcat, cccat, d_skip])


def _ssm_bwd(proj, dyp, lam, bbcat, cccat, d_skip, d_ssm, u_col0, comm=None):
    seq = proj.shape[0]
    steps = seq // N_CHUNK
    n_half = 2 * d_ssm // LANES
    width = HALF_W

    def body(u_ref, dyp_ref, lam_ref, bb_ref, cc_ref, d_ref, du_ref, dbb_ref, dcc_ref, da_ref, dd_ref,
             s_ref, g_ref, tab_ref):
        n_tiles = steps // N_CHUNK
        for hh in range(2):
            lanes = slice(HALF_CH * hh, HALF_CH * (hh + 1))
            u_half, dy_half = u_ref[:, lanes].astype(F32), dyp_ref[:, lanes].astype(F32)
            dy_mx = dy_half.astype(MXU_DTYPE)
            ar, ai = _lam_rows(lam_ref, hh, width)
            _power_table(tab_ref, ar, ai, steps, width)
            s_ref[...] = _dot(u_half.astype(MXU_DTYPE), bb_ref[hh])
            g_ref[...] = _dot_nt(dy_mx, cc_ref[hh])
            fr, fi, lr_, li_ = _scan_both(s_ref, g_ref, ar, ai, steps, width)
            pr, pi = _last_power(tab_ref, steps, width)
            cr, ci = _chunk_carry(fr, fi, pr, pi, _down)
            gr, gi = _chunk_carry(lr_, li_, pr, -pi, _up)

            def tile(m, carry):
                sr, si, accr, acci = carry
                t1r, t1i = _tile(tab_ref, m, width, 0)[...], _tile(tab_ref, m, width, 1)[...]
                mb = n_tiles - 1 - m
                t2r, t2i = _tile(tab_ref, mb, width, 0)[...], _tile(tab_ref, mb, width, 1)[...]
                for k in range(N_CHUNK):
                    j = m * N_CHUNK + k
                    xr, xi = _cmul(_row(t1r, k), _row(t1i, k), cr, ci)
                    nsr = _tile(s_ref, j, width, 0)[...] + xr
                    nsi = _tile(s_ref, j, width, 1)[...] + xi
                    _tile(s_ref, j, width, 0)[...] = nsr
                    _tile(s_ref, j, width, 1)[...] = nsi
                    qr, qi = _row(t2r, N_CHUNK - 1 - k), _row(t2i, N_CHUNK - 1 - k)
                    ngr = _tile(g_ref, j, width, 0)[...] + (qr * gr + qi * gi)
                    ngi = _tile(g_ref, j, width, 1)[...] + (qr * gi - qi * gr)
                    _tile(g_ref, j, width, 0)[...] = ngr
                    _tile(g_ref, j, width, 1)[...] = ngi
                    accr = accr + (sr * ngr + si * ngi)
                    acci = acci + (sr * ngi - si * ngr)
                    sr, si = nsr, nsi
                return sr, si, accr, acci

            z = jnp.zeros((N_CHUNK, width), F32)
            _, _, accr, acci = lax.fori_loop(0, n_tiles, tile, (cr, ci, z, z))
            da_ref[hh, :, 0:width] = jnp.sum(accr, axis=0, keepdims=True)
            da_ref[hh, :, width:2 * width] = jnp.sum(acci, axis=0, keepdims=True)

            g_mx = g_ref[...].astype(MXU_DTYPE)
            dcc_ref[hh] = _dot_tn(dy_mx, s_ref[...].astype(MXU_DTYPE)).T
            dbb_ref[hh] = _dot_tn(u_half.astype(MXU_DTYPE), g_mx)
            du = _dot_nt(g_mx, bb_ref[hh]) + d_ref[:, lanes] * dy_half
            du_ref[:, lanes] = du.astype(du_ref.dtype)
            dd_ref[:, lanes] = jnp.sum(dy_half * u_half, axis=0, keepdims=True)

    sp = _ssm_specs(seq, u_col0 // LANES)
    return _call(
        body, comm, name="ssm_bwd", grid=(d_ssm // LANES,),
        in_specs=[sp["col"], sp["out"], sp["lam"], sp["bb"], sp["cc"], sp["vec"]],
        out_specs=[sp["out"], sp["bb"], sp["cc"], pl.BlockSpec((2, 1, 2 * width), lambda i: (i, 0, 0)), sp["vec"]],
        out_shape=[jax.ShapeDtypeStruct((seq, d_ssm), MXU_DTYPE),
                   jax.ShapeDtypeStruct((n_half, HALF_CH, 2 * width), F32),
                   jax.ShapeDtypeStruct((n_half, 2 * width, HALF_CH), F32),
                   jax.ShapeDtypeStruct((n_half, 1, 2 * width), F32),
                   jax.ShapeDtypeStruct((1, d_ssm), F32)],
        scratch_shapes=[pltpu.VMEM((seq, 2 * width), F32), pltpu.VMEM((seq, 2 * width), F32),
                        pltpu.VMEM((steps, 2 * width), F32)],
        operands=[proj, dyp, lam, bbcat, cccat, d_skip])


def _tail(xp, t3, proj, yconv, yp, w_glu, b_glu, w_out, g_post, zs_col0, tm):
    seq, d_model = xp.shape
    d_conv, d_ssm = yconv.shape[1], yp.shape[1]
    d_mix = d_conv + d_ssm
    assert zs_col0 % d_ssm == 0

    def body(x_ref, t_ref, zs_ref, yc_ref, yp_ref, wglu_hbm, bglu_ref, wout_hbm, gpost_ref,
             dy_ref, do_ref, mixt_ref, dyc_ref, dyp_ref, dzs_ref, ygt_ref, dq_ref, loss_ref, dgpost_ref, dbglu_ref,
             wglu, wout):
        @pl.when(pl.program_id(0) == 0)
        def _():
            pltpu.sync_copy(wglu_hbm, wglu)
            pltpu.sync_copy(wout_hbm, wout)
            loss_ref[...] = jnp.zeros_like(loss_ref)
            dgpost_ref[...] = jnp.zeros_like(dgpost_ref)
            dbglu_ref[...] = jnp.zeros_like(dbglu_ref)

        a = yp_ref[...]
        th = jnp.tanh(GELU_C * (a + GELU_K * (a * a * a)))
        yg = a * (0.5 * (1.0 + th))
        dgelu = 0.5 * (1.0 + th) + 0.5 * a * (1.0 - th * th) * (GELU_C * (1.0 + 3.0 * GELU_K * a * a))
        yg_mx = yg.astype(MXU_DTYPE)
        sq = _sigmoid(_dot(yg_mx, wglu[...]) + bglu_ref[...])
        y2 = yg * sq
        zs = zs_ref[...].astype(F32)
        sz = _sigmoid(zs)
        silz = zs * sz
        yc, ys = yc_ref[...].astype(F32), y2 * silz
        mix = jnp.concatenate([yc, ys], axis=1).astype(MXU_DTYPE)
        mixt_ref[0:d_conv, :] = yc.T.astype(MXU_DTYPE)
        mixt_ref[d_conv:, :] = ys.T.astype(MXU_DTYPE)
        o = _dot(mix, wout[...])
        r2 = lax.rsqrt(jnp.mean(o * o, axis=-1, keepdims=True) + EPS)
        on = o * r2
        gpost = gpost_ref[...]
        err = (x_ref[...] + on * gpost) - _interleave(t_ref[...])
        loss_ref[...] += 0.5 * jnp.sum(jnp.mean(err * err, axis=-1, keepdims=True), axis=0, keepdims=True)
        dy = err * (1.0 / d_model)
        dy_ref[...] = dy
        dgpost_ref[...] += jnp.sum(dy * on, axis=0, keepdims=True)
        d_on = dy * gpost
        d_o = r2 * (d_on - on * jnp.mean(d_on * on, axis=-1, keepdims=True))
        do_mx = d_o.astype(MXU_DTYPE)
        do_ref[...] = do_mx
        d_mix_ = _dot_nt(do_mx, wout[...])
        dyc_ref[...] = d_mix_[:, :d_conv].astype(dyc_ref.dtype)
        d_yssm = d_mix_[:, d_conv:]
        d_y2 = d_yssm * silz
        dzs_ref[...] = (d_yssm * y2 * (sz * (1.0 + zs * (1.0 - sz)))).astype(dzs_ref.dtype)
        d_q = d_y2 * yg * (sq * (1.0 - sq))
        dq_mx = d_q.astype(MXU_DTYPE)
        dq_ref[...] = dq_mx
        ygt_ref[...] = yg.T.astype(MXU_DTYPE)
        dbglu_ref[...] += jnp.sum(d_q, axis=0, keepdims=True)
        d_yg = d_y2 * sq + _dot_nt(dq_mx, wglu[...])
        dyp_ref[...] = (d_yg * dgelu).astype(dyp_ref.dtype)

    def rows(width, col=0):
        return pl.BlockSpec((tm, width), lambda i: (i, col))

    def fixed(width):
        return pl.BlockSpec((1, width), lambda i: (0, 0))

    def cols(height):
        return pl.BlockSpec((height, tm), lambda i: (0, i))

    any_ = pl.BlockSpec(memory_space=pl.ANY)
    return pl.pallas_call(
        body, name="tail", grid=(seq // tm,),
        in_specs=[rows(d_model), _chunk_block(tm, d_model), rows(d_ssm, zs_col0 // d_ssm), rows(d_conv), rows(d_ssm),
                  any_, fixed(d_ssm), any_, fixed(d_model)],
        out_specs=[rows(d_model), rows(d_model), cols(d_mix), rows(d_conv), rows(d_ssm), rows(d_ssm), cols(d_ssm),
                   rows(d_ssm), fixed(LANES), fixed(d_model), fixed(d_ssm)],
        out_shape=[jax.ShapeDtypeStruct((seq, d_model), F32), jax.ShapeDtypeStruct((seq, d_model), MXU_DTYPE),
                   jax.ShapeDtypeStruct((d_mix, seq), MXU_DTYPE), jax.ShapeDtypeStruct((seq, d_conv), MXU_DTYPE),
                   jax.ShapeDtypeStruct((seq, d_ssm), MXU_DTYPE), jax.ShapeDtypeStruct((seq, d_ssm), MXU_DTYPE),
                   jax.ShapeDtypeStruct((d_ssm, seq), MXU_DTYPE), jax.ShapeDtypeStruct((seq, d_ssm), MXU_DTYPE),
                   jax.ShapeDtypeStruct((1, LANES), F32), jax.ShapeDtypeStruct((1, d_model), F32),
                   jax.ShapeDtypeStruct((1, d_ssm), F32)],
        scratch_shapes=[pltpu.VMEM(w_glu.shape, MXU_DTYPE), pltpu.VMEM(w_out.shape, MXU_DTYPE)],
        compiler_params=_params(("arbitrary",)),
    )(xp, t3, proj, yconv, yp, w_glu, b_glu, w_out, g_post)


def _bwd_in(d4, du, dzs, gr, win_g, xp, dy, g_pre, comm, tm):
    seq, d_model = xp.shape
    nb, _, nc = win_g.shape
    per = d4.shape[2] // gr

    def body(d4_ref, du_ref, dzs_ref, w_hbm, x_ref, dy_ref, g_ref, gx_ref, dg_ref, w_all, w_sems):
        def granule(g):
            p, cols = g // per, slice(g % per * gr, (g % per + 1) * gr)
            if p < 4:
                return d4_ref[p, :, cols]
            return du_ref[:, cols] if p == 4 else dzs_ref[:, cols]

        i = pl.program_id(0)
        loads = [pltpu.make_async_copy(w_hbm.at[k], w_all.at[k], w_sems.at[k]) for k in range(nb)]

        @pl.when(i == 0)
        def _():
            dg_ref[...] = jnp.zeros_like(dg_ref)
            for cp in loads:
                cp.start()

        dh = None
        for k in range(nb):
            @pl.when(i == 0)
            def _():
                loads[k].wait()

            dp = jnp.concatenate([granule(g) for g in range(k * nc // gr, (k + 1) * nc // gr)], axis=1)
            part = _dot_nt(dp, w_all[k])
            dh = part if dh is None else dh + part

        x = x_ref[...]
        r = lax.rsqrt(jnp.mean(x * x, axis=-1, keepdims=True) + EPS)
        xn = x * r
        dg_ref[...] += jnp.sum(dh * xn, axis=0, keepdims=True)
        dxn = dh * g_ref[...]
        gx_ref[...] = r * (dxn - xn * jnp.mean(dxn * xn, axis=-1, keepdims=True)) + dy_ref[...]

    row = pl.BlockSpec((tm, d_model), lambda i: (i, 0))
    vec = pl.BlockSpec((1, d_model), lambda i: (0, 0))
    return _call(
        body, comm, name="bwd_in", grid=(seq // tm,),
        in_specs=[pl.BlockSpec((4, tm, d4.shape[2]), lambda i: (0, i, 0)),
                  pl.BlockSpec((tm, du.shape[1]), lambda i: (i, 0)), pl.BlockSpec((tm, dzs.shape[1]), lambda i: (i, 0)),
                  pl.BlockSpec(memory_space=pl.ANY), row, row, vec],
        out_specs=[row, vec],
        out_shape=[jax.ShapeDtypeStruct((seq, d_model), F32), jax.ShapeDtypeStruct((1, d_model), F32)],
        scratch_shapes=[pltpu.VMEM(win_g.shape, win_g.dtype), pltpu.SemaphoreType.DMA((nb,))],
        operands=[d4, du, dzs, win_g, xp, dy, g_pre])


def _lookup(g, table):
    out = jnp.int32(table[0])
    for gi in range(1, len(table)):
        if table[gi] != table[gi - 1]:
            out = jnp.where(g >= gi, jnp.int32(table[gi]), out)
    return out


def _held(values, used):
    cur = next(v for v, u in zip(values, used) if u)
    out = []
    for v, u in zip(values, used):
        cur = v if u else cur
        out.append(cur)
    return out


def _dw_in(name, ht, d4, du, dzs, granules, gr, nc, tm, comm=None):
    d_model, seq = ht.shape
    per = d4.shape[2] // gr
    piece, col = [g // per for g in granules], [g % per for g in granules]
    sources = [(d4, [p < 4 for p in piece]), (du, [p == 4 for p in piece]), (dzs, [p == 5 for p in piece])]
    sources = [(a, used) for a, used in sources if any(used)]
    select = [next(s for s, (_, used) in enumerate(sources) if used[q]) for q in range(len(granules))]
    owner, place = [g * gr // nc for g in granules], [g * gr % nc // gr for g in granules]

    def body(a_ref, *refs):
        src_refs, o_ref = refs[:-1], refs[-1]
        j = pl.program_id(1)
        for s, ref in enumerate(src_refs):
            @pl.when(_lookup(j, select) == s)
            def _():
                o_ref[...] = _dot(a_ref[...], ref[...]).astype(o_ref.dtype)

    in_specs = [pl.BlockSpec((tm, seq), lambda i, j: (i, 0))]
    for a, used in sources:
        cols = _held(col, used)
        if a.ndim == 3:
            rows = _held(piece, used)
            in_specs.append(pl.BlockSpec((None, seq, gr), functools.partial(
                lambda i, j, rows, cols: (_lookup(j, rows), 0, _lookup(j, cols)), rows=rows, cols=cols)))
        else:
            in_specs.append(pl.BlockSpec((seq, gr), functools.partial(
                lambda i, j, cols: (0, _lookup(j, cols)), cols=cols)))
    return _call(
        body, comm, name=name, grid=(d_model // tm, len(granules)), in_specs=in_specs,
        out_specs=[pl.BlockSpec((None, tm, gr), lambda i, j: (_lookup(j, owner), i, _lookup(j, place)))],
        out_shape=[jax.ShapeDtypeStruct((N_DEV, d_model, nc), MXU_DTYPE)],
        operands=[ht] + [a for a, _ in sources])


def _wgrad(name, at, b, tm, tn, out_shape, out_block, out_index, comm=None):
    m, seq = at.shape
    n = b.shape[1]

    def body(a_ref, b_ref, o_ref):
        o_ref[...] = _dot(a_ref[...], b_ref[...]).astype(o_ref.dtype)

    return _call(
        body, comm, name=name, grid=(n // tn, m // tm),
        in_specs=[pl.BlockSpec((tm, seq), lambda j, i: (i, 0)), pl.BlockSpec((seq, tn), lambda j, i: (0, j))],
        out_specs=[pl.BlockSpec(out_block, lambda j, i: out_index(i, j))],
        out_shape=[jax.ShapeDtypeStruct(out_shape, MXU_DTYPE)],
        operands=[at, b])


def _eye_g():
    return jnp.eye(HALF_G, dtype=F32)


def _bb_blockdiag(bbt_r, bbt_i):
    n_half = bbt_r.shape[0] // HALF_G

    def one(t):
        t = t.reshape(n_half, HALF_G, SSM_GROUP, SSM_STATE)
        t = t[:, :, :, None, :] * _eye_g()[None, :, None, :, None]
        return t.reshape(n_half, HALF_CH, HALF_W)

    return jnp.concatenate([one(bbt_r), one(bbt_i)], axis=-1)


def _cc_blockdiag(c_re, c_im):
    n_half = c_re.shape[0] // HALF_G

    def one(t):
        t = t.reshape(n_half, HALF_G, SSM_GROUP, SSM_STATE)
        t = jnp.transpose(t, (0, 3, 1, 2))
        t = t[:, None, :, :, :] * _eye_g()[None, :, None, :, None]
        return t.reshape(n_half, HALF_W, HALF_CH)

    return jnp.concatenate([one(c_re), one(-c_im)], axis=1)


def _bb_diag(dbb):
    n_half = dbb.shape[0]
    t = dbb.reshape(n_half, HALF_G, SSM_GROUP, 2, HALF_G, SSM_STATE)
    t = jnp.sum(t * _eye_g()[None, :, None, None, :, None], axis=4)
    t = jnp.transpose(t, (3, 0, 1, 2, 4))
    return t.reshape(2, n_half * HALF_G, SSM_GROUP, SSM_STATE)


def _cc_diag(dcc):
    n_half = dcc.shape[0]
    t = dcc.reshape(n_half, 2, HALF_G, SSM_STATE, HALF_G, SSM_GROUP)
    t = jnp.sum(t * _eye_g()[None, None, :, None, :, None], axis=2)
    t = jnp.transpose(t, (1, 0, 3, 4, 2))
    return t.reshape(2, n_half * HALF_G, SSM_GROUP, SSM_STATE)


def _permute_rows(a):
    seq, d = a.shape
    return a.reshape(N_CHUNK, seq // N_CHUNK, d).transpose(1, 0, 2).reshape(seq, d)


def _unpermute_rows(a):
    seq, d = a.shape
    return a.reshape(seq // N_CHUNK, N_CHUNK, d).transpose(1, 0, 2).reshape(seq, d)


def _pack_rows(shape):
    return -(-math.prod(shape) // (8 * LANES)) * 8


def _pack(parts, dtype=F32):
    rows = []
    for p in parts:
        flat = p.reshape(-1).astype(dtype)
        rows.append(jnp.pad(flat, (0, _pack_rows(p.shape) * LANES - flat.shape[0])).reshape(-1, LANES))
    return jnp.concatenate(rows, axis=0)


def _unpack(packed, shapes):
    out, o = [], 0
    for s in shapes:
        n = _pack_rows(s)
        out.append(packed[o:o + n].reshape(-1)[:math.prod(s)].reshape(s))
        o += n
    return out


def kernel(x, norm_pre_g, w_in, conv_w, conv_b, ssm_a_re, ssm_a_im, ssm_log_dt, ssm_b_re, ssm_b_im, ssm_c_re, ssm_c_im, ssm_d, w_glu, b_glu, w_out, norm_post_g, loss_target, m_norm_pre_g, m_w_in, m_conv_w, m_conv_b, m_ssm_a_re, m_ssm_a_im, m_ssm_log_dt, m_ssm_b_re, m_ssm_b_im, m_ssm_c_re, m_ssm_c_im, m_ssm_d, m_w_glu, m_b_glu, m_w_out, m_norm_post_g, v_norm_pre_g, v_w_in, v_conv_w, v_conv_b, v_ssm_a_re, v_ssm_a_im, v_ssm_log_dt, v_ssm_b_re, v_ssm_b_im, v_ssm_c_re, v_ssm_c_im, v_ssm_d, v_w_glu, v_b_glu, v_w_out, v_norm_post_g):
    seq, d_model = x.shape[1], x.shape[2]
    d_conv, d_ssm = conv_b.shape[0], ssm_d.shape[0]
    groups, states = ssm_a_re.shape
    assert x.shape[0] == 1 and seq % (8 * N_CHUNK) == 0 and d_conv == d_ssm
    assert (groups, states) == (d_ssm // SSM_GROUP, SSM_STATE) and d_ssm % LANES == 0
    me = 4 * lax.axis_index("x") + 2 * lax.axis_index("y") + lax.axis_index("c")
    tm = min(512, seq)

    x3 = x[0].reshape(N_CHUNK, seq // N_CHUNK, d_model)
    t3 = loss_target[0].reshape(N_CHUNK, seq // N_CHUNK, d_model)
    row = lambda a: a.reshape(1, -1)
    conv_w8 = jnp.pad(conv_w, ((0, 8 - conv_w.shape[0]), (0, 0)))

    g3 = lambda a: a.reshape(groups, 1, -1)
    bt_re, bt_im = jnp.transpose(ssm_b_re, (0, 2, 1)), jnp.transpose(ssm_b_im, (0, 2, 1))
    lbr, lbi, qr, qi, bbt_r, bbt_i = _ssm_prep(g3(ssm_a_re), g3(ssm_a_im), g3(ssm_log_dt), bt_re, bt_im)
    n_half = groups // HALF_G
    lam = jnp.stack([lbr.reshape(n_half, HALF_W), lbi.reshape(n_half, HALF_W)], axis=1)
    bbcat = _bb_blockdiag(bbt_r, bbt_i).astype(MXU_DTYPE)
    cccat = _cc_blockdiag(ssm_c_re, ssm_c_im).astype(MXU_DTYPE)

    xp, h, ht = _norm_in(x3, row(norm_pre_g), tm)
    order = jnp.stack([jnp.bitwise_xor(me, r) for r in CONSUME_ORDER]).astype(jnp.int32)
    proj, win_g, (convw_g,) = _fwd_in(h, w_in.astype(MXU_DTYPE), order, _Comm([conv_w8]), tm)
    conv_w_full = jnp.transpose(convw_g, (1, 0, 2)).reshape(8, d_conv)
    u_col0, zs_col0 = 4 * d_conv, 4 * d_conv + d_ssm
    yconv = _conv_fwd(proj, conv_w_full, row(conv_b), d_conv)
    (yp,), (wout_g, wglu_g) = _ssm_fwd(proj, lam, bbcat, cccat, row(ssm_d), d_ssm, u_col0,
                                       _Comm([w_out.astype(MXU_DTYPE), w_glu.astype(MXU_DTYPE)]))
    w_out_full = wout_g.reshape(-1, d_model)
    w_glu_full = wglu_g.reshape(-1, d_ssm)
    (dy, d_o, mixt, dyc, dyp, dzs, ygt, dq, loss_part, dgpost, dbglu) = _tail(
        xp, t3, proj, yconv, yp, w_glu_full, row(b_glu), w_out_full, row(norm_post_g), zs_col0, min(256, seq))

    r_out, r_glu, nc = w_out.shape[0], w_glu.shape[0], w_in.shape[1]
    (dwout_p,), _ = _wgrad("dw_out", mixt, d_o, r_out, min(1024, d_model), (N_DEV, r_out, d_model),
                           (None, r_out, min(1024, d_model)), lambda i, j: (i, 0, j))
    (dwglu_p,), _ = _wgrad("dw_glu", ygt, dq, r_glu, d_ssm, (N_DEV, r_glu, d_ssm),
                           (None, r_glu, d_ssm), lambda i, j: (i, 0, 0))
    (d4, dconvb, dconvw), (recv_glu,) = _conv_bwd(proj, dyc, conv_w_full, row(conv_b), d_conv,
                                                  _Comm([], [dwglu_p]))
    late = [k for k in range(N_DEV) if k * nc < u_col0 + d_ssm and (k + 1) * nc > u_col0]
    early = [k for k in range(N_DEV) if k not in late]
    gr = math.gcd(nc, d_conv)
    granules = lambda blocks: [g for k in blocks for g in range(k * nc // gr, (k + 1) * nc // gr)]
    tmw = min(1024, d_model)
    (dwin_e,), (recv_out,) = _dw_in("dw_in_early", ht, d4, None, dzs, granules(early), gr, nc, tmw,
                                    _Comm([], [dwout_p]))
    (du, dbb, dcc, da, dd), (recv_in,) = _ssm_bwd(
        proj, dyp, lam, bbcat, cccat, row(ssm_d), d_ssm, u_col0, _Comm([], [dwin_e], dests={0: early}))
    parts_mx = [_bb_diag(dbb), _cc_diag(dcc)]
    (dwin_l,), (pack_mx_g,) = _dw_in("dw_in_late", ht, d4, du, dzs, granules(late), gr, nc, tmw,
                                     _Comm([_pack(parts_mx, MXU_DTYPE)]))
    da_n = jnp.transpose(da.reshape(n_half, 2, HALF_G, SSM_STATE), (1, 0, 2, 3)).reshape(2, groups, 1, states)
    parts = [dgpost, dconvb, dd, dbglu, dconvw[:3], da_n, loss_part]
    shapes, shapes_mx = [p.shape for p in parts], [p.shape for p in parts_mx]
    (gx_p, dgpre), (pack_g, recv_in) = _bwd_in(
        d4, du, dzs, gr, win_g, xp, dy, row(norm_pre_g),
        _Comm([_pack(parts)], [dwin_l], dests={1: late}, into={1: recv_in}), min(256, seq))
    (last_g,) = _exchange("reduce_last", [_pack([dgpre])], [])
    (g_gpost, g_convb, g_d, g_bglu, g_convw, g_da, loss_sum) = _unpack(_sum_slots("sum_pack", pack_g), shapes)
    (g_dbb, g_dcc) = _unpack(_sum_slots("sum_pack_mx", pack_mx_g), shapes_mx)
    (g_gpre,) = _unpack(_sum_slots("sum_last", last_g), [dgpre.shape])
    g_convw = lax.dynamic_slice(g_convw, (0, me * conv_w.shape[1]), conv_w.shape)

    tr = lambda a: jnp.transpose(a, (0, 2, 1))
    direct = [(g_gpre, row(norm_pre_g), row(m_norm_pre_g), row(v_norm_pre_g)),
              (g_convb, row(conv_b), row(m_conv_b), row(v_conv_b)),
              (g_d, row(ssm_d), row(m_ssm_d), row(v_ssm_d)),
              (g_bglu, row(b_glu), row(m_b_glu), row(v_b_glu)),
              (g_gpost, row(norm_post_g), row(m_norm_post_g), row(v_norm_post_g)),
              (g_convw, conv_w, m_conv_w, v_conv_w),
              (g_dcc[0], ssm_c_re, m_ssm_c_re, v_ssm_c_re),
              (-g_dcc[1], ssm_c_im, m_ssm_c_im, v_ssm_c_im)]
    ssm = dict(da_r=g_da[0], da_i=g_da[1], dbb_r=g_dbb[0], dbb_i=g_dbb[1], lr=g3(ssm_a_re), li=g3(ssm_a_im),
               ldt=g3(ssm_log_dt), bt_r=bt_re, bt_i=bt_im, lbr=lbr, lbi=lbi, qr=qr, qi=qi,
               w_a_re=g3(ssm_a_re), m_a_re=g3(m_ssm_a_re), v_a_re=g3(v_ssm_a_re),
               w_a_im=g3(ssm_a_im), m_a_im=g3(m_ssm_a_im), v_a_im=g3(v_ssm_a_im),
               w_log_dt=g3(ssm_log_dt), m_log_dt=g3(m_ssm_log_dt), v_log_dt=g3(v_ssm_log_dt),
               w_bt_re=bt_re, m_bt_re=tr(m_ssm_b_re), v_bt_re=tr(v_ssm_b_re),
               w_bt_im=bt_im, m_bt_im=tr(m_ssm_b_im), v_bt_im=tr(v_ssm_b_im))
    small = _small_update(direct, ssm)
    res = {}
    for name, quad, shape in zip(["norm_pre_g", "conv_b", "ssm_d", "b_glu", "norm_post_g", "conv_w", "ssm_c_re", "ssm_c_im"],
                                 small[:8], [norm_pre_g.shape, conv_b.shape, ssm_d.shape, b_glu.shape,
                                             norm_post_g.shape, conv_w.shape, ssm_c_re.shape, ssm_c_im.shape]):
        res[name] = tuple(a.reshape(shape) for a in quad)
    res["ssm_a_re"] = tuple(a.reshape(ssm_a_re.shape) for a in small[8])
    res["ssm_a_im"] = tuple(a.reshape(ssm_a_im.shape) for a in small[9])
    res["ssm_log_dt"] = tuple(a.reshape(ssm_log_dt.shape) for a in small[10])
    res["ssm_b_re"] = tuple(tr(a) for a in small[11])
    res["ssm_b_im"] = tuple(tr(a) for a in small[12])
    res["w_in"] = tuple(_adam_big("adam_w_in", recv_in, w_in, m_w_in, v_w_in, min(256, d_model)))
    res["w_out"] = tuple(_adam_big("adam_w_out", recv_out, w_out, m_w_out, v_w_out, min(128, r_out)))
    res["w_glu"] = tuple(_adam_big("adam_w_glu", recv_glu, w_glu, m_w_glu, v_w_glu, r_glu))

    order = ["norm_pre_g", "w_in", "conv_w", "conv_b", "ssm_a_re", "ssm_a_im", "ssm_log_dt", "ssm_b_re", "ssm_b_im",
             "ssm_c_re", "ssm_c_im", "ssm_d", "w_glu", "b_glu", "w_out", "norm_post_g"]
    loss = loss_sum[0, 0]
    grad_x = _unpermute_rows(gx_p)[None]
    return (loss, grad_x, *[res[n][0] for n in order], *[res[n][1] for n in order],
            *[res[n][2] for n in order], *[res[n][3] for n in order])
```

```python
import functools
import math

import jax
import jax.numpy as jnp
from jax import lax
from jax.experimental import pallas as pl
from jax.experimental.pallas import tpu as pltpu

F32 = jnp.float32
MXU_DTYPE = jnp.bfloat16
PROJ_DTYPE = jnp.bfloat16
AXES = ("x", "y", "c")
N_DEV = 8
N_CHUNK = 8
LANES = 128
SSM_GROUP = 16
SSM_STATE = 64
HALF_CH = 64
HALF_G = HALF_CH // SSM_GROUP
HALF_W = HALF_G * SSM_STATE
EPS = 1e-6
ADAM_LR, ADAM_B1, ADAM_B2, ADAM_EPS, ADAM_WD, ADAM_STEP = 0.001, 0.9, 0.999, 1e-08, 0.01, 10
GELU_C = math.sqrt(2.0 / math.pi)
GELU_K = 0.044715
VMEM_LIMIT = 56 * 1024 * 1024


def _params(sem=None):
    return pltpu.CompilerParams(dimension_semantics=sem, vmem_limit_bytes=VMEM_LIMIT)


def _dot(a, b):
    return jnp.dot(a, b, preferred_element_type=F32)


def _dot_nt(a, b):
    return lax.dot_general(a, b, (((1,), (1,)), ((), ())), preferred_element_type=F32)


def _dot_tn(a, b):
    return lax.dot_general(a, b, (((0,), (0,)), ((), ())), preferred_element_type=F32)


def _sigmoid(z):
    return 1.0 / (1.0 + jnp.exp(-z))


def _flip(v, bit):
    return 1 - v if bit else v


def _peers():
    x, y, c = (lax.axis_index(a) for a in AXES)
    out = []
    for m in range(1, N_DEV):
        px, py, pc = _flip(x, (m >> 2) & 1), _flip(y, (m >> 1) & 1), _flip(c, m & 1)
        out.append((px, py, pc, 4 * px + 2 * py + pc))
    return out


class _Comm:
    def __init__(self, gathers=(), scatters=(), dests=None, into=None):
        self.n_g = len(gathers)
        self.operands = list(gathers) + list(scatters)
        self.n = len(self.operands)
        self.dests = dests or {}
        self.into = into or {}

    def out_shape(self):
        return [jax.ShapeDtypeStruct((N_DEV,) + a.shape if t < self.n_g else a.shape, a.dtype)
                for t, a in enumerate(self.operands)]

    def scratch(self):
        if not self.n:
            return []
        return [pltpu.SemaphoreType.DMA((self.n, N_DEV - 1)), pltpu.SemaphoreType.DMA((self.n, N_DEV - 1)),
                pltpu.SemaphoreType.DMA((self.n,))]

    def _copies(self, in_refs, out_refs, sems, arrivals):
        send_sems, recv_sems, local_sems = sems
        x, y, c = (lax.axis_index(a) for a in AXES)
        me = 4 * x + 2 * y + c

        def src(t, dev):
            return in_refs[t] if t < self.n_g else in_refs[t].at[dev]

        def member(t, dev):
            if t not in self.dests:
                return None
            return functools.reduce(jnp.logical_or, [dev == d for d in self.dests[t]])

        local = [(member(t, me), pltpu.make_async_copy(src(t, me), out_refs[t].at[me], local_sems.at[t]))
                 for t in range(self.n)]
        sends, recvs = [], []
        for t in range(self.n):
            for m, (px, py, pc, peer) in enumerate(_peers()):
                kw = dict(send_sem=send_sems.at[t, m], recv_sem=recv_sems.at[t, m],
                          device_id=(px, py, pc), device_id_type=pl.DeviceIdType.MESH)
                sends.append((member(t, peer), pltpu.make_async_remote_copy(
                    src_ref=src(t, peer), dst_ref=out_refs[t].at[me], **kw)))
                if arrivals:
                    recvs.append((member(t, me), pltpu.make_async_remote_copy(
                        src_ref=src(t, peer), dst_ref=out_refs[t].at[peer], **kw)))
        return local, sends, recvs

    @staticmethod
    def _do(cond, action):
        if cond is None:
            action()
        else:
            pl.when(cond)(action)

    def start(self, in_refs, out_refs, sems):
        local, sends, _ = self._copies(in_refs, out_refs, sems, arrivals=False)
        for cond, cp in local + sends:
            self._do(cond, cp.start)

    def finish(self, in_refs, out_refs, sems):
        local, sends, recvs = self._copies(in_refs, out_refs, sems, arrivals=True)
        for cond, cp in recvs:
            self._do(cond, cp.wait_recv)
        for cond, cp in sends:
            self._do(cond, cp.wait_send)
        for cond, cp in local:
            self._do(cond, cp.wait)


def _call(body, comm, *, name, grid, in_specs, out_specs, out_shape, operands, scratch_shapes=()):
    comm = comm or _Comm()
    n_in, n_out, n_scr, cn = len(in_specs), len(out_specs), len(scratch_shapes), comm.n
    landing = sorted(comm.into)
    aliases = {n_in + cn + q: n_out + t for q, t in enumerate(landing)}

    def wrapped(*refs):
        parts, o = [], 0
        for k in (n_in, cn, len(landing), n_out, cn, n_scr):
            parts.append(refs[o:o + k])
            o += k
        h_in, c_in, _, h_out, c_out, h_scr = parts
        sems = refs[o:]
        if cn:
            first = functools.reduce(jnp.logical_and, [pl.program_id(d) == 0 for d in range(len(grid))])

            @pl.when(first)
            def _():
                comm.start(c_in, c_out, sems)

        body(*h_in, *h_out, *h_scr)
        if cn:
            last = functools.reduce(jnp.logical_and, [pl.program_id(d) == grid[d] - 1 for d in range(len(grid))])

            @pl.when(last)
            def _():
                comm.finish(c_in, c_out, sems)

    any_ = pl.BlockSpec(memory_space=pl.ANY)
    res = pl.pallas_call(
        wrapped, name=name, grid=grid, in_specs=list(in_specs) + [any_] * (cn + len(landing)),
        out_specs=list(out_specs) + [any_] * cn,
        out_shape=list(out_shape) + comm.out_shape(), scratch_shapes=list(scratch_shapes) + comm.scratch(),
        input_output_aliases=aliases, compiler_params=_params(("arbitrary",) * len(grid)),
    )(*operands, *comm.operands, *[comm.into[t] for t in landing])
    return list(res[:n_out]), list(res[n_out:])


def _exchange(name, gathers, scatters):
    def body(tok_ref):
        tok_ref[...] = jnp.zeros_like(tok_ref)

    return _call(body, _Comm(gathers, scatters), name=name, grid=(1,), in_specs=[],
                 out_specs=[pl.BlockSpec((8, LANES), lambda i: (0, 0))],
                 out_shape=[jax.ShapeDtypeStruct((8, LANES), F32)], operands=[])[1]


def _ssm_prep(a_re, a_im, log_dt, bt_re, bt_im):
    def body(lr_ref, li_ref, ldt_ref, br_ref, bi_ref, lbr_ref, lbi_ref, qr_ref, qi_ref, bbr_ref, bbi_ref):
        lr, li = lr_ref[...], li_ref[...]
        dt = jnp.exp(ldt_ref[...])
        mag = jnp.exp(lr * dt)
        lbr, lbi = mag * jnp.cos(li * dt), mag * jnp.sin(li * dt)
        nr, ni = lbr - 1.0, lbi
        den = lr * lr + li * li
        qr = (nr * lr + ni * li) / den
        qi = (ni * lr - nr * li) / den
        br, bi = br_ref[...], bi_ref[...]
        lbr_ref[...], lbi_ref[...], qr_ref[...], qi_ref[...] = lbr, lbi, qr, qi
        bbr_ref[...] = qr * br - qi * bi
        bbi_ref[...] = qr * bi + qi * br

    s2 = jax.ShapeDtypeStruct(a_re.shape, F32)
    s3 = jax.ShapeDtypeStruct(bt_re.shape, F32)
    return pl.pallas_call(body, name="ssm_prep", out_shape=[s2, s2, s2, s2, s3, s3],
                          compiler_params=_params())(a_re, a_im, log_dt, bt_re, bt_im)


def _adam(w, g, m, v):
    m2 = ADAM_B1 * m + (1.0 - ADAM_B1) * g
    v2 = ADAM_B2 * v + (1.0 - ADAM_B2) * (g * g)
    m_hat = m2 / (1.0 - ADAM_B1 ** ADAM_STEP)
    v_hat = v2 / (1.0 - ADAM_B2 ** ADAM_STEP)
    delta = -ADAM_LR * (m_hat / (jnp.sqrt(v_hat) + ADAM_EPS) + ADAM_WD * w)
    return delta, m2, v2


def _small_update(direct, ssm):
    n_direct = len(direct)
    flat = [a for quad in direct for a in quad]
    names = ["da_r", "da_i", "dbb_r", "dbb_i", "lr", "li", "ldt", "bt_r", "bt_i", "lbr", "lbi", "qr", "qi"]
    flat += [ssm[k] for k in names]
    chain = ["a_re", "a_im", "log_dt", "bt_re", "bt_im"]
    for k in chain:
        flat += [ssm["w_" + k], ssm["m_" + k], ssm["v_" + k]]
    n_in = len(flat)

    def body(*refs):
        ins, outs = refs[:n_in], refs[n_in:]
        for p in range(n_direct):
            g, w, m, v = (r[...] for r in ins[4 * p:4 * p + 4])
            d, m2, v2 = _adam(w, g, m, v)
            outs[4 * p][...], outs[4 * p + 1][...], outs[4 * p + 2][...], outs[4 * p + 3][...] = g, d, m2, v2
        o = 4 * n_direct
        da_r, da_i, dbb_r, dbb_i, lr, li, ldt, bt_r, bt_i, lbr, lbi, qr, qi = (r[...] for r in ins[o:o + 13])
        dt = jnp.exp(ldt)
        g_br = qr * dbb_r + qi * dbb_i
        g_bi = qr * dbb_i - qi * dbb_r
        dq_r = jnp.sum(bt_r * dbb_r + bt_i * dbb_i, axis=1, keepdims=True)
        dq_i = jnp.sum(bt_r * dbb_i - bt_i * dbb_r, axis=1, keepdims=True)
        den = lr * lr + li * li
        cr, ci = lr / den, li / den
        gl_r = da_r + (cr * dq_r - ci * dq_i)
        gl_i = da_i + (cr * dq_i + ci * dq_r)
        w_r = qr * cr + qi * ci
        w_i = qi * cr - qr * ci
        g_lr = dt * (lbr * gl_r + lbi * gl_i) + (-w_r * dq_r - w_i * dq_i)
        g_li = dt * (lbr * gl_i - lbi * gl_r) + (-w_r * dq_i + w_i * dq_r)
        m_r = lr * lbr - li * lbi
        m_i = lr * lbi + li * lbr
        g_ldt = jnp.sum(m_r * gl_r + m_i * gl_i, axis=2, keepdims=True) * dt
        grads = [g_lr, g_li, g_ldt, g_br, g_bi]
        base_in, base_out = o + 13, 4 * n_direct
        for p, g in enumerate(grads):
            w, m, v = (r[...] for r in ins[base_in + 3 * p:base_in + 3 * p + 3])
            d, m2, v2 = _adam(w, g, m, v)
            q = base_out + 4 * p
            outs[q][...], outs[q + 1][...], outs[q + 2][...], outs[q + 3][...] = g, d, m2, v2

    out_shape = []
    for quad in direct:
        out_shape += [jax.ShapeDtypeStruct(quad[1].shape, F32)] * 4
    for k in chain:
        out_shape += [jax.ShapeDtypeStruct(ssm["w_" + k].shape, F32)] * 4
    res = pl.pallas_call(body, name="small_update", out_shape=out_shape, compiler_params=_params())(*flat)
    return [tuple(res[4 * p:4 * p + 4]) for p in range(n_direct + len(chain))]


def _sum_slots(name, pack):
    def body(p_ref, o_ref):
        acc = p_ref[0].astype(F32)
        for k in range(1, N_DEV):
            acc = acc + p_ref[k].astype(F32)
        o_ref[...] = acc

    return pl.pallas_call(body, name=name, out_shape=jax.ShapeDtypeStruct(pack.shape[1:], F32),
                          compiler_params=_params())(pack)


def _adam_big(name, recv, w, m, v, tr):
    _, rows, cols = recv.shape

    def body(r_ref, w_ref, m_ref, v_ref, g_ref, d_ref, m2_ref, v2_ref):
        g = r_ref[0].astype(F32)
        for k in range(1, N_DEV):
            g = g + r_ref[k].astype(F32)
        d, m2, v2 = _adam(w_ref[...], g, m_ref[...], v_ref[...])
        g_ref[...], d_ref[...], m2_ref[...], v2_ref[...] = g, d, m2, v2

    blk = pl.BlockSpec((tr, cols), lambda i: (i, 0))
    shp = jax.ShapeDtypeStruct((rows, cols), F32)
    return pl.pallas_call(
        body, name=name, grid=(rows // tr,),
        in_specs=[pl.BlockSpec((N_DEV, tr, cols), lambda i: (0, i, 0)), blk, blk, blk],
        out_specs=[blk] * 4, out_shape=[shp] * 4, compiler_params=_params(("parallel",)),
    )(recv, w, m, v)


def _chunk_block(tm, d):
    return pl.BlockSpec((N_CHUNK, tm // N_CHUNK, d), lambda i: (0, i, 0))


def _interleave(block):
    c, n, d = block.shape
    return pltpu.einshape("cjd->jcd", block).reshape(n * c, d)


def _norm_in(x3, g_pre, tm):
    _, steps, d_model = x3.shape
    seq = steps * N_CHUNK

    def body(x_ref, g_ref, xp_ref, h_ref, ht_ref):
        x = _interleave(x_ref[...])
        xp_ref[...] = x
        r = lax.rsqrt(jnp.mean(x * x, axis=-1, keepdims=True) + EPS)
        h = x * r * g_ref[...]
        h_ref[...] = h.astype(h_ref.dtype)
        ht_ref[...] = h.T.astype(ht_ref.dtype)

    rows = pl.BlockSpec((tm, d_model), lambda i: (i, 0))
    return pl.pallas_call(
        body, name="norm_in", grid=(seq // tm,),
        in_specs=[_chunk_block(tm, d_model), pl.BlockSpec((1, d_model), lambda i: (0, 0))],
        out_specs=[rows, rows, pl.BlockSpec((d_model, tm), lambda i: (0, i))],
        out_shape=[jax.ShapeDtypeStruct((seq, d_model), F32), jax.ShapeDtypeStruct((seq, d_model), MXU_DTYPE),
                   jax.ShapeDtypeStruct((d_model, seq), MXU_DTYPE)],
        compiler_params=_params(("parallel",)),
    )(x3, g_pre)


GATHER_ORDER = (0, 1, 4, 2, 6, 5, 3, 7)
PIECE_COLS = 256


def _fwd_in(h, w_pieces, me, comm, tm):
    seq, d_model = h.shape
    n_p, _, gw = w_pieces.shape
    n_i = seq // tm
    cn = comm.n
    consume = [(rel, r) for rel in (0, 1) for r in range(n_p)]
    consume += [(rel, r) for r in range(n_p) for rel in (4, 2, 5, 3)]
    consume += [(rel, r) for r in range(n_p) for rel in (6, 7)]
    n_q = len(consume)
    order = jnp.stack([jnp.bitwise_xor(me, rel) * n_p + r for rel, r in consume]).astype(jnp.int32)

    def body(order_ref, h_hbm, w_hbm, *rest):
        c_in, rest = rest[:cn], rest[cn:]
        proj_ref, wing = rest[0], rest[1]
        c_out, rest = rest[2:2 + cn], rest[2 + cn:]
        wbuf, send_sems, recv_sems, own_sems, load_sems, h_all, h_sems = rest[:7]
        c_sems = rest[7:]
        k, i = pl.program_id(0), pl.program_id(1)
        h_loads = [pltpu.make_async_copy(h_hbm.at[pl.ds(t * tm, tm)], h_all.at[pl.ds(t * tm, tm)], h_sems.at[t])
                   for t in range(n_i)]
        x, y, c = (lax.axis_index(a) for a in AXES)
        me_ = 4 * x + 2 * y + c

        def dev(rel):
            return _flip(x, (rel >> 2) & 1), _flip(y, (rel >> 1) & 1), _flip(c, rel & 1)

        def slot(rel):
            px, py, pc = dev(rel)
            return 4 * px + 2 * py + pc

        def remote(src, block, r, to_rel, sem):
            return pltpu.make_async_remote_copy(
                src_ref=src, dst_ref=wing.at[block, r], send_sem=send_sems.at[sem, r], recv_sem=recv_sems.at[sem, r],
                device_id=dev(to_rel), device_id_type=pl.DeviceIdType.MESH)

        pieces = range(n_p)
        own = [pltpu.make_async_copy(w_hbm.at[r], wing.at[me_, r], own_sems.at[r]) for r in pieces]
        first_hand = {p: [remote(w_hbm.at[r], me_, r, GATHER_ORDER[p], p - 1) for r in pieces] for p in (1, 2, 3)}
        relay = {2: (c == 0, [remote(wing.at[slot(4), r], slot(4), r, 2, 3) for r in pieces]),
                 3: (c == 1, [remote(wing.at[slot(2), r], slot(2), r, 4, 3) for r in pieces])}
        passed_on = {p: [remote(wing.at[slot(GATHER_ORDER[p]), r], slot(GATHER_ORDER[p]), r, 1, p + 2) for r in pieces]
                     for p in (2, 3, 4)}

        def load(q):
            rel, r = consume[q]
            return pltpu.make_async_copy(wing.at[slot(rel), r], wbuf.at[q % 2], load_sems.at[q % 2])

        def take(q):
            rel, r = consume[q]
            p = GATHER_ORDER.index(rel)
            if p == 0:
                own[r].wait()
            else:
                remote(w_hbm.at[r], slot(rel), r, rel, p - 1).wait_recv()
            if p in relay:
                pl.when(relay[p][0])(relay[p][1][r].start)
            if p in passed_on:
                passed_on[p][r].start()
            load(q).start()

        @pl.when((k == 0) & (i == 0))
        def _():
            for r in pieces:
                own[r].start()
            for r in pieces:
                for p in (1, 2, 3):
                    first_hand[p][r].start()
            for cp in h_loads:
                cp.start()
            comm.start(c_in, c_out, c_sems)
            take(0)

        for t in range(n_i):
            pl.when((k == 0) & (i == t))(h_loads[t].wait)

        for q in range(n_q):
            @pl.when((k == q) & (i == 0))
            def _():
                load(q).wait()

            if q + 1 < n_q:
                @pl.when((k == q) & (i == n_i - 1))
                def _():
                    take(q + 1)

        proj_ref[...] = _dot(h_all[pl.ds(pl.multiple_of(i * tm, tm), tm), :], wbuf[k % 2]).astype(proj_ref.dtype)

        @pl.when((k == n_q - 1) & (i == n_i - 1))
        def _():
            for p in first_hand:
                for cp in first_hand[p]:
                    cp.wait_send()
            for p in passed_on:
                for cp in passed_on[p]:
                    cp.wait_send()
            for cond, cps in relay.values():
                for cp in cps:
                    pl.when(cond)(cp.wait_send)
            comm.finish(c_in, c_out, c_sems)

    any_ = pl.BlockSpec(memory_space=pl.ANY)
    grid_spec = pltpu.PrefetchScalarGridSpec(
        num_scalar_prefetch=1, grid=(n_q, n_i),
        in_specs=[any_, any_] + [any_] * cn,
        out_specs=[pl.BlockSpec((tm, gw), lambda k, i, o: (i, o[k])), any_] + [any_] * cn,
        scratch_shapes=[pltpu.VMEM((2, d_model, gw), w_pieces.dtype), pltpu.SemaphoreType.DMA((N_DEV - 1, n_p)),
                        pltpu.SemaphoreType.DMA((N_DEV - 1, n_p)), pltpu.SemaphoreType.DMA((n_p,)),
                        pltpu.SemaphoreType.DMA((2,)), pltpu.VMEM((seq, d_model), h.dtype),
                        pltpu.SemaphoreType.DMA((n_i,))]
        + comm.scratch())
    res = pl.pallas_call(
        body, name="fwd_in", grid_spec=grid_spec,
        out_shape=[jax.ShapeDtypeStruct((seq, N_DEV * n_p * gw), PROJ_DTYPE),
                   jax.ShapeDtypeStruct((N_DEV, n_p, d_model, gw), w_pieces.dtype)] + comm.out_shape(),
        compiler_params=_params(("arbitrary", "arbitrary")),
    )(order, h, w_pieces, *comm.operands)
    return res[0], res[1], list(res[2:])


def _shift_prev(a):
    n = a.shape[0]
    last = a[n - N_CHUNK:, :]
    row = lax.broadcasted_iota(jnp.int32, last.shape, 0)
    wrap = jnp.where(row == 0, 0.0, pltpu.roll(last, 1, axis=0))
    return jnp.concatenate([wrap, a[:n - N_CHUNK, :]], axis=0)


def _shift_next(a):
    first = a[:N_CHUNK, :]
    row = lax.broadcasted_iota(jnp.int32, first.shape, 0)
    wrap = jnp.where(row == N_CHUNK - 1, 0.0, pltpu.roll(first, N_CHUNK - 1, axis=0))
    return jnp.concatenate([a[N_CHUNK:, :], wrap], axis=0)


def _conv_specs(seq, d_conv):
    nblk = d_conv // LANES
    return [pl.BlockSpec((seq, LANES), functools.partial(lambda i, o: (0, o + i), o=q * nblk)) for q in range(4)]


def _conv_fwd(proj, conv_w8, conv_b, d_conv):
    seq = proj.shape[0]

    def body(bg_ref, cg_ref, v_ref, zc_ref, w_ref, b_ref, y_ref):
        cv = cg_ref[...].astype(F32) * v_ref[...].astype(F32)
        s1 = _shift_prev(cv)
        s2 = _shift_prev(s1)
        conv = b_ref[...] + w_ref[0:1, :] * s2 + w_ref[1:2, :] * s1 + w_ref[2:3, :] * cv
        z = zc_ref[...].astype(F32)
        y_ref[...] = (bg_ref[...].astype(F32) * conv * (z * _sigmoid(z))).astype(y_ref.dtype)

    col = pl.BlockSpec((seq, LANES), lambda i: (0, i))
    return pl.pallas_call(
        body, name="conv_fwd", grid=(d_conv // LANES,),
        in_specs=_conv_specs(seq, d_conv) + [pl.BlockSpec((8, LANES), lambda i: (0, i)), pl.BlockSpec((1, LANES), lambda i: (0, i))],
        out_specs=col, out_shape=jax.ShapeDtypeStruct((seq, d_conv), MXU_DTYPE),
        compiler_params=_params(("parallel",)),
    )(proj, proj, proj, proj, conv_w8, conv_b)


def _conv_bwd(proj, dyc, conv_w8, conv_b, d_conv, comm=None):
    seq = proj.shape[0]

    def body(bg_ref, cg_ref, v_ref, zc_ref, dy_ref, w_ref, b_ref, d4_ref, dcb_ref, dcw_ref):
        bg, cg, v, z = (r[...].astype(F32) for r in (bg_ref, cg_ref, v_ref, zc_ref))
        w0, w1, w2 = w_ref[0:1, :], w_ref[1:2, :], w_ref[2:3, :]
        cv = cg * v
        s1 = _shift_prev(cv)
        s2 = _shift_prev(s1)
        conv = b_ref[...] + w0 * s2 + w1 * s1 + w2 * cv
        sig = _sigmoid(z)
        dy = dy_ref[...].astype(F32)
        g1 = dy * (z * sig)
        d_conv_ = g1 * bg
        d4_ref[0] = (g1 * conv).astype(d4_ref.dtype)
        d4_ref[3] = (dy * bg * conv * (sig * (1.0 + z * (1.0 - sig)))).astype(d4_ref.dtype)
        n1 = _shift_next(d_conv_)
        n2 = _shift_next(n1)
        d_cv = w2 * d_conv_ + w1 * n1 + w0 * n2
        d4_ref[1] = (d_cv * v).astype(d4_ref.dtype)
        d4_ref[2] = (d_cv * cg).astype(d4_ref.dtype)
        dcb_ref[...] = jnp.sum(d_conv_, axis=0, keepdims=True)
        rows = [jnp.sum(d_conv_ * s, axis=0, keepdims=True) for s in (s2, s1, cv)]
        dcw_ref[...] = jnp.concatenate(rows + [jnp.zeros((5, LANES), F32)], axis=0)

    col = pl.BlockSpec((seq, LANES), lambda i: (0, i))
    return _call(
        body, comm, name="conv_bwd", grid=(d_conv // LANES,),
        in_specs=_conv_specs(seq, d_conv) + [col, pl.BlockSpec((8, LANES), lambda i: (0, i)), pl.BlockSpec((1, LANES), lambda i: (0, i))],
        out_specs=[pl.BlockSpec((4, seq, LANES), lambda i: (0, 0, i)), pl.BlockSpec((1, LANES), lambda i: (0, i)),
                   pl.BlockSpec((8, LANES), lambda i: (0, i))],
        out_shape=[jax.ShapeDtypeStruct((4, seq, d_conv), MXU_DTYPE), jax.ShapeDtypeStruct((1, d_conv), F32),
                   jax.ShapeDtypeStruct((8, d_conv), F32)],
        operands=[proj, proj, proj, proj, dyc, conv_w8, conv_b])


def _cmul(ar, ai, br, bi):
    return ar * br - ai * bi, ar * bi + ai * br


def _cpow(ar, ai, n):
    rr, ri = jnp.ones_like(ar), jnp.zeros_like(ai)
    while n:
        if n & 1:
            rr, ri = _cmul(rr, ri, ar, ai)
        n >>= 1
        if n:
            ar, ai = _cmul(ar, ai, ar, ai)
    return rr, ri


def _down(v, k):
    row = lax.broadcasted_iota(jnp.int32, v.shape, 0)
    return jnp.where(row >= k, pltpu.roll(v, k, axis=0), 0.0)


def _up(v, k):
    row = lax.broadcasted_iota(jnp.int32, v.shape, 0)
    return jnp.where(row < N_CHUNK - k, pltpu.roll(v, N_CHUNK - k, axis=0), 0.0)


def _chunk_carry(fr, fi, mr, mi, shift):
    vr, vi = shift(fr, 1), shift(fi, 1)
    for k in (1, 2, 4):
        pr, pi = _cmul(mr, mi, shift(vr, k), shift(vi, k))
        vr, vi = vr + pr, vi + pi
        mr, mi = _cmul(mr, mi, mr, mi)
    return vr, vi


def _tile(ref, j, width, part):
    return ref.at[pl.ds(pl.multiple_of(j * N_CHUNK, N_CHUNK), N_CHUNK), pl.ds(part * width, width)]


def _row(t, k):
    return jnp.broadcast_to(t[k:k + 1, :], t.shape)


def _power_table(tab_ref, ar, ai, steps, width):
    e = lax.broadcasted_iota(jnp.int32, ar.shape, 0) + 1
    rr, ri = jnp.ones_like(ar), jnp.zeros_like(ai)
    br, bi = ar, ai
    for bit in range(4):
        mr, mi = _cmul(rr, ri, br, bi)
        take = ((e >> bit) & 1) == 1
        rr, ri = jnp.where(take, mr, rr), jnp.where(take, mi, ri)
        if bit < 3:
            br, bi = _cmul(br, bi, br, bi)
    _tile(tab_ref, 0, width, 0)[...] = rr
    _tile(tab_ref, 0, width, 1)[...] = ri

    def step(m, carry):
        tr, ti = _cmul(carry[0], carry[1], br, bi)
        _tile(tab_ref, m, width, 0)[...] = tr
        _tile(tab_ref, m, width, 1)[...] = ti
        return tr, ti

    lax.fori_loop(1, steps // N_CHUNK, step, (rr, ri))


def _last_power(tab_ref, steps, width):
    shape = (N_CHUNK, width)
    return (jnp.broadcast_to(tab_ref[steps - 1:steps, 0:width], shape),
            jnp.broadcast_to(tab_ref[steps - 1:steps, width:2 * width], shape))


def _scan_fwd(s_ref, ar, ai, steps, width):
    def step(j, carry):
        sr, si = carry
        nr = ar * sr - ai * si + _tile(s_ref, j, width, 0)[...]
        ni = ar * si + ai * sr + _tile(s_ref, j, width, 1)[...]
        _tile(s_ref, j, width, 0)[...] = nr
        _tile(s_ref, j, width, 1)[...] = ni
        return nr, ni

    z = jnp.zeros((N_CHUNK, width), F32)
    return lax.fori_loop(0, steps, step, (z, z), unroll=4)


def _scan_both(s_ref, g_ref, ar, ai, steps, width):
    def step(q, carry):
        sr, si, gr, gi = carry
        j, jb = q, steps - 1 - q
        nsr = ar * sr - ai * si + _tile(s_ref, j, width, 0)[...]
        nsi = ar * si + ai * sr + _tile(s_ref, j, width, 1)[...]
        ngr = ar * gr + ai * gi + _tile(g_ref, jb, width, 0)[...]
        ngi = ar * gi - ai * gr + _tile(g_ref, jb, width, 1)[...]
        _tile(s_ref, j, width, 0)[...] = nsr
        _tile(s_ref, j, width, 1)[...] = nsi
        _tile(g_ref, jb, width, 0)[...] = ngr
        _tile(g_ref, jb, width, 1)[...] = ngi
        return nsr, nsi, ngr, ngi

    z = jnp.zeros((N_CHUNK, width), F32)
    return lax.fori_loop(0, steps, step, (z, z, z, z), unroll=2)


def _patch_fwd(s_ref, tab_ref, cr, ci, steps, width):
    def tile(m, _):
        tr, ti = _tile(tab_ref, m, width, 0)[...], _tile(tab_ref, m, width, 1)[...]
        for k in range(N_CHUNK):
            fr, fi = _cmul(_row(tr, k), _row(ti, k), cr, ci)
            j = m * N_CHUNK + k
            _tile(s_ref, j, width, 0)[...] += fr
            _tile(s_ref, j, width, 1)[...] += fi
        return 0

    lax.fori_loop(0, steps // N_CHUNK, tile, 0)


def _lam_rows(lam_ref, hh, width):
    return (jnp.broadcast_to(lam_ref[hh, 0:1, :], (N_CHUNK, width)),
            jnp.broadcast_to(lam_ref[hh, 1:2, :], (N_CHUNK, width)))


def _ssm_specs(seq, col0):
    return dict(
        col=pl.BlockSpec((seq, LANES), lambda i: (0, col0 + i)),
        lam=pl.BlockSpec((2, 2, HALF_W), lambda i: (i, 0, 0)),
        bb=pl.BlockSpec((2, HALF_CH, 2 * HALF_W), lambda i: (i, 0, 0)),
        cc=pl.BlockSpec((2, 2 * HALF_W, HALF_CH), lambda i: (i, 0, 0)),
        vec=pl.BlockSpec((1, LANES), lambda i: (0, i)),
        out=pl.BlockSpec((seq, LANES), lambda i: (0, i)),
    )


def _ssm_fwd(proj, lam, bbcat, cccat, d_skip, d_ssm, u_col0, comm=None):
    seq = proj.shape[0]
    steps = seq // N_CHUNK

    def body(u_ref, lam_ref, bb_ref, cc_ref, d_ref, yp_ref, s_ref, tab_ref):
        for hh in range(2):
            lanes = slice(HALF_CH * hh, HALF_CH * (hh + 1))
            u_half = u_ref[:, lanes].astype(F32)
            ar, ai = _lam_rows(lam_ref, hh, HALF_W)
            _power_table(tab_ref, ar, ai, steps, HALF_W)
            s_ref[...] = _dot(u_half.astype(MXU_DTYPE), bb_ref[hh])
            fr, fi = _scan_fwd(s_ref, ar, ai, steps, HALF_W)
            pr, pi = _last_power(tab_ref, steps, HALF_W)
            cr, ci = _chunk_carry(fr, fi, pr, pi, _down)
            _patch_fwd(s_ref, tab_ref, cr, ci, steps, HALF_W)
            y = _dot(s_ref[...].astype(MXU_DTYPE), cc_ref[hh])
            yp_ref[:, lanes] = y + d_ref[:, lanes] * u_half

    sp = _ssm_specs(seq, u_col0 // LANES)
    return _call(
        body, comm, name="ssm_fwd", grid=(d_ssm // LANES,),
        in_specs=[sp["col"], sp["lam"], sp["bb"], sp["cc"], sp["vec"]], out_specs=[sp["out"]],
        out_shape=[jax.ShapeDtypeStruct((seq, d_ssm), F32)],
        scratch_shapes=[pltpu.VMEM((seq, 2 * HALF_W), F32), pltpu.VMEM((steps, 2 * HALF_W), F32)],
        operands=[proj, lam, bbcat, cccat, d_skip])


def _ssm_bwd(proj, dyp, lam, bbcat, cccat, d_skip, d_ssm, u_col0, comm=None):
    seq = proj.shape[0]
    steps = seq // N_CHUNK
    n_half = 2 * d_ssm // LANES
    width = HALF_W

    def body(u_ref, dyp_ref, lam_ref, bb_ref, cc_ref, d_ref, du_ref, dbb_ref, dcc_ref, da_ref, dd_ref,
             s_ref, g_ref, tab_ref):
        n_tiles = steps // N_CHUNK
        for hh in range(2):
            lanes = slice(HALF_CH * hh, HALF_CH * (hh + 1))
            u_half, dy_half = u_ref[:, lanes].astype(F32), dyp_ref[:, lanes].astype(F32)
            dy_mx = dy_half.astype(MXU_DTYPE)
            ar, ai = _lam_rows(lam_ref, hh, width)
            _power_table(tab_ref, ar, ai, steps, width)
            s_ref[...] = _dot(u_half.astype(MXU_DTYPE), bb_ref[hh])
            g_ref[...] = _dot_nt(dy_mx, cc_ref[hh])
            fr, fi, lr_, li_ = _scan_both(s_ref, g_ref, ar, ai, steps, width)
            pr, pi = _last_power(tab_ref, steps, width)
            cr, ci = _chunk_carry(fr, fi, pr, pi, _down)
            gr, gi = _chunk_carry(lr_, li_, pr, -pi, _up)

            def tile(m, carry):
                sr, si, accr, acci = carry
                t1r, t1i = _tile(tab_ref, m, width, 0)[...], _tile(tab_ref, m, width, 1)[...]
                mb = n_tiles - 1 - m
                t2r, t2i = _tile(tab_ref, mb, width, 0)[...], _tile(tab_ref, mb, width, 1)[...]
                for k in range(N_CHUNK):
                    j = m * N_CHUNK + k
                    xr, xi = _cmul(_row(t1r, k), _row(t1i, k), cr, ci)
                    nsr = _tile(s_ref, j, width, 0)[...] + xr
                    nsi = _tile(s_ref, j, width, 1)[...] + xi
                    _tile(s_ref, j, width, 0)[...] = nsr
                    _tile(s_ref, j, width, 1)[...] = nsi
                    qr, qi = _row(t2r, N_CHUNK - 1 - k), _row(t2i, N_CHUNK - 1 - k)
                    ngr = _tile(g_ref, j, width, 0)[...] + (qr * gr + qi * gi)
                    ngi = _tile(g_ref, j, width, 1)[...] + (qr * gi - qi * gr)
                    _tile(g_ref, j, width, 0)[...] = ngr
                    _tile(g_ref, j, width, 1)[...] = ngi
                    accr = accr + (sr * ngr + si * ngi)
                    acci = acci + (sr * ngi - si * ngr)
                    sr, si = nsr, nsi
                return sr, si, accr, acci

            z = jnp.zeros((N_CHUNK, width), F32)
            _, _, accr, acci = lax.fori_loop(0, n_tiles, tile, (cr, ci, z, z))
            da_ref[hh, :, 0:width] = jnp.sum(accr, axis=0, keepdims=True)
            da_ref[hh, :, width:2 * width] = jnp.sum(acci, axis=0, keepdims=True)

            g_mx = g_ref[...].astype(MXU_DTYPE)
            dcc_ref[hh] = _dot_tn(dy_mx, s_ref[...].astype(MXU_DTYPE)).T
            dbb_ref[hh] = _dot_tn(u_half.astype(MXU_DTYPE), g_mx)
            du = _dot_nt(g_mx, bb_ref[hh]) + d_ref[:, lanes] * dy_half
            du_ref[:, lanes] = du.astype(du_ref.dtype)
            dd_ref[:, lanes] = jnp.sum(dy_half * u_half, axis=0, keepdims=True)

    sp = _ssm_specs(seq, u_col0 // LANES)
    return _call(
        body, comm, name="ssm_bwd", grid=(d_ssm // LANES,),
        in_specs=[sp["col"], sp["out"], sp["lam"], sp["bb"], sp["cc"], sp["vec"]],
        out_specs=[sp["out"], sp["bb"], sp["cc"], pl.BlockSpec((2, 1, 2 * width), lambda i: (i, 0, 0)), sp["vec"]],
        out_shape=[jax.ShapeDtypeStruct((seq, d_ssm), MXU_DTYPE),
                   jax.ShapeDtypeStruct((n_half, HALF_CH, 2 * width), F32),
                   jax.ShapeDtypeStruct((n_half, 2 * width, HALF_CH), F32),
                   jax.ShapeDtypeStruct((n_half, 1, 2 * width), F32),
                   jax.ShapeDtypeStruct((1, d_ssm), F32)],
        scratch_shapes=[pltpu.VMEM((seq, 2 * width), F32), pltpu.VMEM((seq, 2 * width), F32),
                        pltpu.VMEM((steps, 2 * width), F32)],
        operands=[proj, dyp, lam, bbcat, cccat, d_skip])


def _tail(xp, t3, proj, yconv, yp, w_glu, b_glu, w_out, g_post, zs_col0, tm):
    seq, d_model = xp.shape
    d_conv, d_ssm = yconv.shape[1], yp.shape[1]
    d_mix = d_conv + d_ssm
    assert zs_col0 % d_ssm == 0

    def body(x_ref, t_ref, zs_ref, yc_ref, yp_ref, wglu_hbm, bglu_ref, wout_hbm, gpost_ref,
             dy_ref, do_ref, mixt_ref, dyc_ref, dyp_ref, dzs_ref, ygt_ref, dq_ref, loss_ref, dgpost_ref, dbglu_ref,
             wglu, wout):
        @pl.when(pl.program_id(0) == 0)
        def _():
            pltpu.sync_copy(wglu_hbm, wglu)
            pltpu.sync_copy(wout_hbm, wout)
            loss_ref[...] = jnp.zeros_like(loss_ref)
            dgpost_ref[...] = jnp.zeros_like(dgpost_ref)
            dbglu_ref[...] = jnp.zeros_like(dbglu_ref)

        a = yp_ref[...]
        th = jnp.tanh(GELU_C * (a + GELU_K * (a * a * a)))
        yg = a * (0.5 * (1.0 + th))
        dgelu = 0.5 * (1.0 + th) + 0.5 * a * (1.0 - th * th) * (GELU_C * (1.0 + 3.0 * GELU_K * a * a))
        yg_mx = yg.astype(MXU_DTYPE)
        sq = _sigmoid(_dot(yg_mx, wglu[...]) + bglu_ref[...])
        y2 = yg * sq
        zs = zs_ref[...].astype(F32)
        sz = _sigmoid(zs)
        silz = zs * sz
        yc, ys = yc_ref[...].astype(F32), y2 * silz
        mix = jnp.concatenate([yc, ys], axis=1).astype(MXU_DTYPE)
        mixt_ref[0:d_conv, :] = yc.T.astype(MXU_DTYPE)
        mixt_ref[d_conv:, :] = ys.T.astype(MXU_DTYPE)
        o = _dot(mix, wout[...])
        r2 = lax.rsqrt(jnp.mean(o * o, axis=-1, keepdims=True) + EPS)
        on = o * r2
        gpost = gpost_ref[...]
        err = (x_ref[...] + on * gpost) - _interleave(t_ref[...])
        loss_ref[...] += 0.5 * jnp.sum(jnp.mean(err * err, axis=-1, keepdims=True), axis=0, keepdims=True)
        dy = err * (1.0 / d_model)
        dy_ref[...] = dy
        dgpost_ref[...] += jnp.sum(dy * on, axis=0, keepdims=True)
        d_on = dy * gpost
        d_o = r2 * (d_on - on * jnp.mean(d_on * on, axis=-1, keepdims=True))
        do_mx = d_o.astype(MXU_DTYPE)
        do_ref[...] = do_mx
        d_mix_ = _dot_nt(do_mx, wout[...])
        dyc_ref[...] = d_mix_[:, :d_conv].astype(dyc_ref.dtype)
        d_yssm = d_mix_[:, d_conv:]
        d_y2 = d_yssm * silz
        dzs_ref[...] = (d_yssm * y2 * (sz * (1.0 + zs * (1.0 - sz)))).astype(dzs_ref.dtype)
        d_q = d_y2 * yg * (sq * (1.0 - sq))
        dq_mx = d_q.astype(MXU_DTYPE)
        dq_ref[...] = dq_mx
        ygt_ref[...] = yg.T.astype(MXU_DTYPE)
        dbglu_ref[...] += jnp.sum(d_q, axis=0, keepdims=True)
        d_yg = d_y2 * sq + _dot_nt(dq_mx, wglu[...])
        dyp_ref[...] = (d_yg * dgelu).astype(dyp_ref.dtype)

    def rows(width, col=0):
        return pl.BlockSpec((tm, width), lambda i: (i, col))

    def fixed(width):
        return pl.BlockSpec((1, width), lambda i: (0, 0))

    def cols(height):
        return pl.BlockSpec((height, tm), lambda i: (0, i))

    any_ = pl.BlockSpec(memory_space=pl.ANY)
    return pl.pallas_call(
        body, name="tail", grid=(seq // tm,),
        in_specs=[rows(d_model), _chunk_block(tm, d_model), rows(d_ssm, zs_col0 // d_ssm), rows(d_conv), rows(d_ssm),
                  any_, fixed(d_ssm), any_, fixed(d_model)],
        out_specs=[rows(d_model), rows(d_model), cols(d_mix), rows(d_conv), rows(d_ssm), rows(d_ssm), cols(d_ssm),
                   rows(d_ssm), fixed(LANES), fixed(d_model), fixed(d_ssm)],
        out_shape=[jax.ShapeDtypeStruct((seq, d_model), F32), jax.ShapeDtypeStruct((seq, d_model), MXU_DTYPE),
                   jax.ShapeDtypeStruct((d_mix, seq), MXU_DTYPE), jax.ShapeDtypeStruct((seq, d_conv), MXU_DTYPE),
                   jax.ShapeDtypeStruct((seq, d_ssm), MXU_DTYPE), jax.ShapeDtypeStruct((seq, d_ssm), MXU_DTYPE),
                   jax.ShapeDtypeStruct((d_ssm, seq), MXU_DTYPE), jax.ShapeDtypeStruct((seq, d_ssm), MXU_DTYPE),
                   jax.ShapeDtypeStruct((1, LANES), F32), jax.ShapeDtypeStruct((1, d_model), F32),
                   jax.ShapeDtypeStruct((1, d_ssm), F32)],
        scratch_shapes=[pltpu.VMEM(w_glu.shape, MXU_DTYPE), pltpu.VMEM(w_out.shape, MXU_DTYPE)],
        compiler_params=_params(("arbitrary",)),
    )(xp, t3, proj, yconv, yp, w_glu, b_glu, w_out, g_post)


def _bwd_in(d4, du, dzs, gr, win_g, xp, dy, g_pre, comm, tm):
    seq, d_model = xp.shape
    nb, n_p, _, gw = win_g.shape
    nc = n_p * gw
    per = d4.shape[2] // gr

    def body(d4_ref, du_ref, dzs_ref, w_hbm, x_ref, dy_ref, g_ref, gx_ref, dg_ref, w_all, w_sems):
        def granule(g):
            p, cols = g // per, slice(g % per * gr, (g % per + 1) * gr)
            if p < 4:
                return d4_ref[p, :, cols]
            return du_ref[:, cols] if p == 4 else dzs_ref[:, cols]

        i = pl.program_id(0)
        loads = [[pltpu.make_async_copy(w_hbm.at[k, r], w_all.at[k, :, pl.ds(r * gw, gw)], w_sems.at[k, r])
                  for r in range(n_p)] for k in range(nb)]

        @pl.when(i == 0)
        def _():
            dg_ref[...] = jnp.zeros_like(dg_ref)
            for row in loads:
                for cp in row:
                    cp.start()

        dh = None
        for k in range(nb):
            @pl.when(i == 0)
            def _():
                for cp in loads[k]:
                    cp.wait()

            dp = jnp.concatenate([granule(g) for g in range(k * nc // gr, (k + 1) * nc // gr)], axis=1)
            part = _dot_nt(dp, w_all[k])
            dh = part if dh is None else dh + part

        x = x_ref[...]
        r = lax.rsqrt(jnp.mean(x * x, axis=-1, keepdims=True) + EPS)
        xn = x * r
        dg_ref[...] += jnp.sum(dh * xn, axis=0, keepdims=True)
        dxn = dh * g_ref[...]
        gx_ref[...] = r * (dxn - xn * jnp.mean(dxn * xn, axis=-1, keepdims=True)) + dy_ref[...]

    row = pl.BlockSpec((tm, d_model), lambda i: (i, 0))
    vec = pl.BlockSpec((1, d_model), lambda i: (0, 0))
    return _call(
        body, comm, name="bwd_in", grid=(seq // tm,),
        in_specs=[pl.BlockSpec((4, tm, d4.shape[2]), lambda i: (0, i, 0)),
                  pl.BlockSpec((tm, du.shape[1]), lambda i: (i, 0)), pl.BlockSpec((tm, dzs.shape[1]), lambda i: (i, 0)),
                  pl.BlockSpec(memory_space=pl.ANY), row, row, vec],
        out_specs=[row, vec],
        out_shape=[jax.ShapeDtypeStruct((seq, d_model), F32), jax.ShapeDtypeStruct((1, d_model), F32)],
        scratch_shapes=[pltpu.VMEM((nb, d_model, nc), win_g.dtype), pltpu.SemaphoreType.DMA((nb, n_p))],
        operands=[d4, du, dzs, win_g, xp, dy, g_pre])


def _lookup(g, table):
    out = jnp.int32(table[0])
    for gi in range(1, len(table)):
        if table[gi] != table[gi - 1]:
            out = jnp.where(g >= gi, jnp.int32(table[gi]), out)
    return out


def _held(values, used):
    cur = next(v for v, u in zip(values, used) if u)
    out = []
    for v, u in zip(values, used):
        cur = v if u else cur
        out.append(cur)
    return out


def _dw_in(name, ht, d4, du, dzs, granules, gr, nc, tm, comm=None):
    d_model, seq = ht.shape
    per = d4.shape[2] // gr
    piece, col = [g // per for g in granules], [g % per for g in granules]
    sources = [(d4, [p < 4 for p in piece]), (du, [p == 4 for p in piece]), (dzs, [p == 5 for p in piece])]
    sources = [(a, used) for a, used in sources if any(used)]
    select = [next(s for s, (_, used) in enumerate(sources) if used[q]) for q in range(len(granules))]
    owner, place = [g * gr // nc for g in granules], [g * gr % nc // gr for g in granules]

    def body(a_ref, *refs):
        src_refs, o_ref = refs[:-1], refs[-1]
        j = pl.program_id(1)
        for s, ref in enumerate(src_refs):
            @pl.when(_lookup(j, select) == s)
            def _():
                o_ref[...] = _dot(a_ref[...], ref[...]).astype(o_ref.dtype)

    in_specs = [pl.BlockSpec((tm, seq), lambda i, j: (i, 0))]
    for a, used in sources:
        cols = _held(col, used)
        if a.ndim == 3:
            rows = _held(piece, used)
            in_specs.append(pl.BlockSpec((None, seq, gr), functools.partial(
                lambda i, j, rows, cols: (_lookup(j, rows), 0, _lookup(j, cols)), rows=rows, cols=cols)))
        else:
            in_specs.append(pl.BlockSpec((seq, gr), functools.partial(
                lambda i, j, cols: (0, _lookup(j, cols)), cols=cols)))
    return _call(
        body, comm, name=name, grid=(d_model // tm, len(granules)), in_specs=in_specs,
        out_specs=[pl.BlockSpec((None, tm, gr), lambda i, j: (_lookup(j, owner), i, _lookup(j, place)))],
        out_shape=[jax.ShapeDtypeStruct((N_DEV, d_model, nc), MXU_DTYPE)],
        operands=[ht] + [a for a, _ in sources])


def _wgrad(name, at, b, tm, tn, out_shape, out_block, out_index, comm=None):
    m, seq = at.shape
    n = b.shape[1]

    def body(a_ref, b_ref, o_ref):
        o_ref[...] = _dot(a_ref[...], b_ref[...]).astype(o_ref.dtype)

    return _call(
        body, comm, name=name, grid=(n // tn, m // tm),
        in_specs=[pl.BlockSpec((tm, seq), lambda j, i: (i, 0)), pl.BlockSpec((seq, tn), lambda j, i: (0, j))],
        out_specs=[pl.BlockSpec(out_block, lambda j, i: out_index(i, j))],
        out_shape=[jax.ShapeDtypeStruct(out_shape, MXU_DTYPE)],
        operands=[at, b])


def _eye_g():
    return jnp.eye(HALF_G, dtype=F32)


def _bb_blockdiag(bbt_r, bbt_i):
    n_half = bbt_r.shape[0] // HALF_G

    def one(t):
        t = t.reshape(n_half, HALF_G, SSM_GROUP, SSM_STATE)
        t = t[:, :, :, None, :] * _eye_g()[None, :, None, :, None]
        return t.reshape(n_half, HALF_CH, HALF_W)

    return jnp.concatenate([one(bbt_r), one(bbt_i)], axis=-1)


def _cc_blockdiag(c_re, c_im):
    n_half = c_re.shape[0] // HALF_G

    def one(t):
        t = t.reshape(n_half, HALF_G, SSM_GROUP, SSM_STATE)
        t = jnp.transpose(t, (0, 3, 1, 2))
        t = t[:, None, :, :, :] * _eye_g()[None, :, None, :, None]
        return t.reshape(n_half, HALF_W, HALF_CH)

    return jnp.concatenate([one(c_re), one(-c_im)], axis=1)


def _bb_diag(dbb):
    n_half = dbb.shape[0]
    t = dbb.reshape(n_half, HALF_G, SSM_GROUP, 2, HALF_G, SSM_STATE)
    t = jnp.sum(t * _eye_g()[None, :, None, None, :, None], axis=4)
    t = jnp.transpose(t, (3, 0, 1, 2, 4))
    return t.reshape(2, n_half * HALF_G, SSM_GROUP, SSM_STATE)


def _cc_diag(dcc):
    n_half = dcc.shape[0]
    t = dcc.reshape(n_half, 2, HALF_G, SSM_STATE, HALF_G, SSM_GROUP)
    t = jnp.sum(t * _eye_g()[None, None, :, None, :, None], axis=2)
    t = jnp.transpose(t, (1, 0, 3, 4, 2))
    return t.reshape(2, n_half * HALF_G, SSM_GROUP, SSM_STATE)


def _permute_rows(a):
    seq, d = a.shape
    return a.reshape(N_CHUNK, seq // N_CHUNK, d).transpose(1, 0, 2).reshape(seq, d)


def _unpermute_rows(a):
    seq, d = a.shape
    return a.reshape(seq // N_CHUNK, N_CHUNK, d).transpose(1, 0, 2).reshape(seq, d)


def _pack_rows(shape):
    return -(-math.prod(shape) // (8 * LANES)) * 8


def _pack(parts, dtype=F32):
    rows = []
    for p in parts:
        flat = p.reshape(-1).astype(dtype)
        rows.append(jnp.pad(flat, (0, _pack_rows(p.shape) * LANES - flat.shape[0])).reshape(-1, LANES))
    return jnp.concatenate(rows, axis=0)


def _unpack(packed, shapes):
    out, o = [], 0
    for s in shapes:
        n = _pack_rows(s)
        out.append(packed[o:o + n].reshape(-1)[:math.prod(s)].reshape(s))
        o += n
    return out


def kernel(x, norm_pre_g, w_in, conv_w, conv_b, ssm_a_re, ssm_a_im, ssm_log_dt, ssm_b_re, ssm_b_im, ssm_c_re, ssm_c_im, ssm_d, w_glu, b_glu, w_out, norm_post_g, loss_target, m_norm_pre_g, m_w_in, m_conv_w, m_conv_b, m_ssm_a_re, m_ssm_a_im, m_ssm_log_dt, m_ssm_b_re, m_ssm_b_im, m_ssm_c_re, m_ssm_c_im, m_ssm_d, m_w_glu, m_b_glu, m_w_out, m_norm_post_g, v_norm_pre_g, v_w_in, v_conv_w, v_conv_b, v_ssm_a_re, v_ssm_a_im, v_ssm_log_dt, v_ssm_b_re, v_ssm_b_im, v_ssm_c_re, v_ssm_c_im, v_ssm_d, v_w_glu, v_b_glu, v_w_out, v_norm_post_g):
    seq, d_model = x.shape[1], x.shape[2]
    d_conv, d_ssm = conv_b.shape[0], ssm_d.shape[0]
    groups, states = ssm_a_re.shape
    assert x.shape[0] == 1 and seq % (8 * N_CHUNK) == 0 and d_conv == d_ssm
    assert (groups, states) == (d_ssm // SSM_GROUP, SSM_STATE) and d_ssm % LANES == 0
    me = 4 * lax.axis_index("x") + 2 * lax.axis_index("y") + lax.axis_index("c")
    tm = min(512, seq)

    x3 = x[0].reshape(N_CHUNK, seq // N_CHUNK, d_model)
    t3 = loss_target[0].reshape(N_CHUNK, seq // N_CHUNK, d_model)
    row = lambda a: a.reshape(1, -1)
    conv_w8 = jnp.pad(conv_w, ((0, 8 - conv_w.shape[0]), (0, 0)))

    g3 = lambda a: a.reshape(groups, 1, -1)
    bt_re, bt_im = jnp.transpose(ssm_b_re, (0, 2, 1)), jnp.transpose(ssm_b_im, (0, 2, 1))
    lbr, lbi, qr, qi, bbt_r, bbt_i = _ssm_prep(g3(ssm_a_re), g3(ssm_a_im), g3(ssm_log_dt), bt_re, bt_im)
    n_half = groups // HALF_G
    lam = jnp.stack([lbr.reshape(n_half, HALF_W), lbi.reshape(n_half, HALF_W)], axis=1)
    bbcat = _bb_blockdiag(bbt_r, bbt_i).astype(MXU_DTYPE)
    cccat = _cc_blockdiag(ssm_c_re, ssm_c_im).astype(MXU_DTYPE)

    xp, h, ht = _norm_in(x3, row(norm_pre_g), tm)
    nc = w_in.shape[1]
    w_pieces = jnp.transpose(w_in.astype(MXU_DTYPE).reshape(d_model, nc // PIECE_COLS, PIECE_COLS), (1, 0, 2))
    proj, win_g, (convw_g,) = _fwd_in(h, w_pieces, me, _Comm([conv_w8]), min(2048, seq))
    conv_w_full = jnp.transpose(convw_g, (1, 0, 2)).reshape(8, d_conv)
    u_col0, zs_col0 = 4 * d_conv, 4 * d_conv + d_ssm
    yconv = _conv_fwd(proj, conv_w_full, row(conv_b), d_conv)
    (yp,), (wout_g, wglu_g) = _ssm_fwd(proj, lam, bbcat, cccat, row(ssm_d), d_ssm, u_col0,
                                       _Comm([w_out.astype(MXU_DTYPE), w_glu.astype(MXU_DTYPE)]))
    w_out_full = wout_g.reshape(-1, d_model)
    w_glu_full = wglu_g.reshape(-1, d_ssm)
    (dy, d_o, mixt, dyc, dyp, dzs, ygt, dq, loss_part, dgpost, dbglu) = _tail(
        xp, t3, proj, yconv, yp, w_glu_full, row(b_glu), w_out_full, row(norm_post_g), zs_col0, min(256, seq))

    r_out, r_glu, nc = w_out.shape[0], w_glu.shape[0], w_in.shape[1]
    (dwout_p,), _ = _wgrad("dw_out", mixt, d_o, r_out, min(1024, d_model), (N_DEV, r_out, d_model),
                           (None, r_out, min(1024, d_model)), lambda i, j: (i, 0, j))
    (dwglu_p,), _ = _wgrad("dw_glu", ygt, dq, r_glu, d_ssm, (N_DEV, r_glu, d_ssm),
                           (None, r_glu, d_ssm), lambda i, j: (i, 0, 0))
    (d4, dconvb, dconvw), (recv_glu,) = _conv_bwd(proj, dyc, conv_w_full, row(conv_b), d_conv,
                                                  _Comm([], [dwglu_p]))
    late = [k for k in range(N_DEV) if k * nc < u_col0 + d_ssm and (k + 1) * nc > u_col0]
    early = [k for k in range(N_DEV) if k not in late]
    gr = math.gcd(nc, d_conv)
    granules = lambda blocks: [g for k in blocks for g in range(k * nc // gr, (k + 1) * nc // gr)]
    tmw = min(1024, d_model)
    (dwin_e,), (recv_out,) = _dw_in("dw_in_early", ht, d4, None, dzs, granules(early), gr, nc, tmw,
                                    _Comm([], [dwout_p]))
    (du, dbb, dcc, da, dd), (recv_in,) = _ssm_bwd(
        proj, dyp, lam, bbcat, cccat, row(ssm_d), d_ssm, u_col0, _Comm([], [dwin_e], dests={0: early}))
    parts_mx = [_bb_diag(dbb), _cc_diag(dcc)]
    (dwin_l,), (pack_mx_g,) = _dw_in("dw_in_late", ht, d4, du, dzs, granules(late), gr, nc, tmw,
                                     _Comm([_pack(parts_mx, MXU_DTYPE)]))
    da_n = jnp.transpose(da.reshape(n_half, 2, HALF_G, SSM_STATE), (1, 0, 2, 3)).reshape(2, groups, 1, states)
    parts = [dgpost, dconvb, dd, dbglu, dconvw[:3], da_n, loss_part]
    shapes, shapes_mx = [p.shape for p in parts], [p.shape for p in parts_mx]
    (gx_p, dgpre), (pack_g, recv_in) = _bwd_in(
        d4, du, dzs, gr, win_g, xp, dy, row(norm_pre_g),
        _Comm([_pack(parts)], [dwin_l], dests={1: late}, into={1: recv_in}), min(256, seq))
    (last_g,) = _exchange("reduce_last", [_pack([dgpre])], [])
    (g_gpost, g_convb, g_d, g_bglu, g_convw, g_da, loss_sum) = _unpack(_sum_slots("sum_pack", pack_g), shapes)
    (g_dbb, g_dcc) = _unpack(_sum_slots("sum_pack_mx", pack_mx_g), shapes_mx)
    (g_gpre,) = _unpack(_sum_slots("sum_last", last_g), [dgpre.shape])
    g_convw = lax.dynamic_slice(g_convw, (0, me * conv_w.shape[1]), conv_w.shape)

    tr = lambda a: jnp.transpose(a, (0, 2, 1))
    direct = [(g_gpre, row(norm_pre_g), row(m_norm_pre_g), row(v_norm_pre_g)),
              (g_convb, row(conv_b), row(m_conv_b), row(v_conv_b)),
              (g_d, row(ssm_d), row(m_ssm_d), row(v_ssm_d)),
              (g_bglu, row(b_glu), row(m_b_glu), row(v_b_glu)),
              (g_gpost, row(norm_post_g), row(m_norm_post_g), row(v_norm_post_g)),
              (g_convw, conv_w, m_conv_w, v_conv_w),
              (g_dcc[0], ssm_c_re, m_ssm_c_re, v_ssm_c_re),
              (-g_dcc[1], ssm_c_im, m_ssm_c_im, v_ssm_c_im)]
    ssm = dict(da_r=g_da[0], da_i=g_da[1], dbb_r=g_dbb[0], dbb_i=g_dbb[1], lr=g3(ssm_a_re), li=g3(ssm_a_im),
               ldt=g3(ssm_log_dt), bt_r=bt_re, bt_i=bt_im, lbr=lbr, lbi=lbi, qr=qr, qi=qi,
               w_a_re=g3(ssm_a_re), m_a_re=g3(m_ssm_a_re), v_a_re=g3(v_ssm_a_re),
               w_a_im=g3(ssm_a_im), m_a_im=g3(m_ssm_a_im), v_a_im=g3(v_ssm_a_im),
               w_log_dt=g3(ssm_log_dt), m_log_dt=g3(m_ssm_log_dt), v_log_dt=g3(v_ssm_log_dt),
               w_bt_re=bt_re, m_bt_re=tr(m_ssm_b_re), v_bt_re=tr(v_ssm_b_re),
               w_bt_im=bt_im, m_bt_im=tr(m_ssm_b_im), v_bt_im=tr(v_ssm_b_im))
    small = _small_update(direct, ssm)
    res = {}
    for name, quad, shape in zip(["norm_pre_g", "conv_b", "ssm_d", "b_glu", "norm_post_g", "conv_w", "ssm_c_re", "ssm_c_im"],
                                 small[:8], [norm_pre_g.shape, conv_b.shape, ssm_d.shape, b_glu.shape,
                                             norm_post_g.shape, conv_w.shape, ssm_c_re.shape, ssm_c_im.shape]):
        res[name] = tuple(a.reshape(shape) for a in quad)
    res["ssm_a_re"] = tuple(a.reshape(ssm_a_re.shape) for a in small[8])
    res["ssm_a_im"] = tuple(a.reshape(ssm_a_im.shape) for a in small[9])
    res["ssm_log_dt"] = tuple(a.reshape(ssm_log_dt.shape) for a in small[10])
    res["ssm_b_re"] = tuple(tr(a) for a in small[11])
    res["ssm_b_im"] = tuple(tr(a) for a in small[12])
    res["w_in"] = tuple(_adam_big("adam_w_in", recv_in, w_in, m_w_in, v_w_in, min(256, d_model)))
    res["w_out"] = tuple(_adam_big("adam_w_out", recv_out, w_out, m_w_out, v_w_out, min(128, r_out)))
    res["w_glu"] = tuple(_adam_big("adam_w_glu", recv_glu, w_glu, m_w_glu, v_w_glu, r_glu))

    order = ["norm_pre_g", "w_in", "conv_w", "conv_b", "ssm_a_re", "ssm_a_im", "ssm_log_dt", "ssm_b_re", "ssm_b_im",
             "ssm_c_re", "ssm_c_im", "ssm_d", "w_glu", "b_glu", "w_out", "norm_post_g"]
    loss = loss_sum[0, 0]
    grad_x = _unpermute_rows(gx_p)[None]
    return (loss, grad_x, *[res[n][0] for n in order], *[res[n][1] for n in order],
            *[res[n][2] for n in order], *[res[n][3] for n in order])
```

```python
import functools
import math

import jax
import jax.numpy as jnp
from jax import lax
from jax.experimental import pallas as pl
from jax.experimental.pallas import tpu as pltpu

F32 = jnp.float32
MXU_DTYPE = jnp.bfloat16
PROJ_DTYPE = jnp.bfloat16
AXES = ("x", "y", "c")
N_DEV = 8
N_CHUNK = 8
LANES = 128
SSM_GROUP = 16
SSM_STATE = 64
HALF_CH = 64
HALF_G = HALF_CH // SSM_GROUP
HALF_W = HALF_G * SSM_STATE
EPS = 1e-6
ADAM_LR, ADAM_B1, ADAM_B2, ADAM_EPS, ADAM_WD, ADAM_STEP = 0.001, 0.9, 0.999, 1e-08, 0.01, 10
GELU_C = math.sqrt(2.0 / math.pi)
GELU_K = 0.044715
VMEM_LIMIT = 56 * 1024 * 1024


def _params(sem=None):
    return pltpu.CompilerParams(dimension_semantics=sem, vmem_limit_bytes=VMEM_LIMIT)


def _dot(a, b):
    return jnp.dot(a, b, preferred_element_type=F32)


def _dot_nt(a, b):
    return lax.dot_general(a, b, (((1,), (1,)), ((), ())), preferred_element_type=F32)


def _dot_tn(a, b):
    return lax.dot_general(a, b, (((0,), (0,)), ((), ())), preferred_element_type=F32)


def _sigmoid(z):
    return 1.0 / (1.0 + jnp.exp(-z))


def _flip(v, bit):
    return 1 - v if bit else v


def _peers():
    x, y, c = (lax.axis_index(a) for a in AXES)
    out = []
    for m in range(1, N_DEV):
        px, py, pc = _flip(x, (m >> 2) & 1), _flip(y, (m >> 1) & 1), _flip(c, m & 1)
        out.append((px, py, pc, 4 * px + 2 * py + pc))
    return out


class _Comm:
    def __init__(self, gathers=(), scatters=(), dests=None, into=None):
        self.n_g = len(gathers)
        self.operands = list(gathers) + list(scatters)
        self.n = len(self.operands)
        self.dests = dests or {}
        self.into = into or {}

    def out_shape(self):
        return [jax.ShapeDtypeStruct((N_DEV,) + a.shape if t < self.n_g else a.shape, a.dtype)
                for t, a in enumerate(self.operands)]

    def scratch(self):
        if not self.n:
            return []
        return [pltpu.SemaphoreType.DMA((self.n, N_DEV - 1)), pltpu.SemaphoreType.DMA((self.n, N_DEV - 1)),
                pltpu.SemaphoreType.DMA((self.n,))]

    def _copies(self, in_refs, out_refs, sems, arrivals):
        send_sems, recv_sems, local_sems = sems
        x, y, c = (lax.axis_index(a) for a in AXES)
        me = 4 * x + 2 * y + c

        def src(t, dev):
            return in_refs[t] if t < self.n_g else in_refs[t].at[dev]

        def member(t, dev):
            if t not in self.dests:
                return None
            return functools.reduce(jnp.logical_or, [dev == d for d in self.dests[t]])

        local = [(member(t, me), pltpu.make_async_copy(src(t, me), out_refs[t].at[me], local_sems.at[t]))
                 for t in range(self.n)]
        sends, recvs = [], []
        for t in range(self.n):
            for m, (px, py, pc, peer) in enumerate(_peers()):
                kw = dict(send_sem=send_sems.at[t, m], recv_sem=recv_sems.at[t, m],
                          device_id=(px, py, pc), device_id_type=pl.DeviceIdType.MESH)
                sends.append((member(t, peer), pltpu.make_async_remote_copy(
                    src_ref=src(t, peer), dst_ref=out_refs[t].at[me], **kw)))
                if arrivals:
                    recvs.append((member(t, me), pltpu.make_async_remote_copy(
                        src_ref=src(t, peer), dst_ref=out_refs[t].at[peer], **kw)))
        return local, sends, recvs

    @staticmethod
    def _do(cond, action):
        if cond is None:
            action()
        else:
            pl.when(cond)(action)

    def start(self, in_refs, out_refs, sems):
        local, sends, _ = self._copies(in_refs, out_refs, sems, arrivals=False)
        for cond, cp in local + sends:
            self._do(cond, cp.start)

    def finish(self, in_refs, out_refs, sems):
        local, sends, recvs = self._copies(in_refs, out_refs, sems, arrivals=True)
        for cond, cp in recvs:
            self._do(cond, cp.wait_recv)
        for cond, cp in sends:
            self._do(cond, cp.wait_send)
        for cond, cp in local:
            self._do(cond, cp.wait)


def _call(body, comm, *, name, grid, in_specs, out_specs, out_shape, operands, scratch_shapes=()):
    comm = comm or _Comm()
    n_in, n_out, n_scr, cn = len(in_specs), len(out_specs), len(scratch_shapes), comm.n
    landing = sorted(comm.into)
    aliases = {n_in + cn + q: n_out + t for q, t in enumerate(landing)}

    def wrapped(*refs):
        parts, o = [], 0
        for k in (n_in, cn, len(landing), n_out, cn, n_scr):
            parts.append(refs[o:o + k])
            o += k
        h_in, c_in, _, h_out, c_out, h_scr = parts
        sems = refs[o:]
        if cn:
            first = functools.reduce(jnp.logical_and, [pl.program_id(d) == 0 for d in range(len(grid))])

            @pl.when(first)
            def _():
                comm.start(c_in, c_out, sems)

        body(*h_in, *h_out, *h_scr)
        if cn:
            last = functools.reduce(jnp.logical_and, [pl.program_id(d) == grid[d] - 1 for d in range(len(grid))])

            @pl.when(last)
            def _():
                comm.finish(c_in, c_out, sems)

    any_ = pl.BlockSpec(memory_space=pl.ANY)
    res = pl.pallas_call(
        wrapped, name=name, grid=grid, in_specs=list(in_specs) + [any_] * (cn + len(landing)),
        out_specs=list(out_specs) + [any_] * cn,
        out_shape=list(out_shape) + comm.out_shape(), scratch_shapes=list(scratch_shapes) + comm.scratch(),
        input_output_aliases=aliases, compiler_params=_params(("arbitrary",) * len(grid)),
    )(*operands, *comm.operands, *[comm.into[t] for t in landing])
    return list(res[:n_out]), list(res[n_out:])


def _exchange(name, gathers, scatters):
    def body(tok_ref):
        tok_ref[...] = jnp.zeros_like(tok_ref)

    return _call(body, _Comm(gathers, scatters), name=name, grid=(1,), in_specs=[],
                 out_specs=[pl.BlockSpec((8, LANES), lambda i: (0, 0))],
                 out_shape=[jax.ShapeDtypeStruct((8, LANES), F32)], operands=[])[1]


def _ssm_prep(a_re, a_im, log_dt, bt_re, bt_im):
    def body(lr_ref, li_ref, ldt_ref, br_ref, bi_ref, lbr_ref, lbi_ref, qr_ref, qi_ref, bbr_ref, bbi_ref):
        lr, li = lr_ref[...], li_ref[...]
        dt = jnp.exp(ldt_ref[...])
        mag = jnp.exp(lr * dt)
        lbr, lbi = mag * jnp.cos(li * dt), mag * jnp.sin(li * dt)
        nr, ni = lbr - 1.0, lbi
        den = lr * lr + li * li
        qr = (nr * lr + ni * li) / den
        qi = (ni * lr - nr * li) / den
        br, bi = br_ref[...], bi_ref[...]
        lbr_ref[...], lbi_ref[...], qr_ref[...], qi_ref[...] = lbr, lbi, qr, qi
        bbr_ref[...] = qr * br - qi * bi
        bbi_ref[...] = qr * bi + qi * br

    s2 = jax.ShapeDtypeStruct(a_re.shape, F32)
    s3 = jax.ShapeDtypeStruct(bt_re.shape, F32)
    return pl.pallas_call(body, name="ssm_prep", out_shape=[s2, s2, s2, s2, s3, s3],
                          compiler_params=_params())(a_re, a_im, log_dt, bt_re, bt_im)


def _adam(w, g, m, v):
    m2 = ADAM_B1 * m + (1.0 - ADAM_B1) * g
    v2 = ADAM_B2 * v + (1.0 - ADAM_B2) * (g * g)
    m_hat = m2 / (1.0 - ADAM_B1 ** ADAM_STEP)
    v_hat = v2 / (1.0 - ADAM_B2 ** ADAM_STEP)
    delta = -ADAM_LR * (m_hat / (jnp.sqrt(v_hat) + ADAM_EPS) + ADAM_WD * w)
    return delta, m2, v2


def _small_update(direct, ssm):
    n_direct = len(direct)
    flat = [a for quad in direct for a in quad]
    names = ["da_r", "da_i", "dbb_r", "dbb_i", "lr", "li", "ldt", "bt_r", "bt_i", "lbr", "lbi", "qr", "qi"]
    flat += [ssm[k] for k in names]
    chain = ["a_re", "a_im", "log_dt", "bt_re", "bt_im"]
    for k in chain:
        flat += [ssm["w_" + k], ssm["m_" + k], ssm["v_" + k]]
    n_in = len(flat)

    def body(*refs):
        ins, outs = refs[:n_in], refs[n_in:]
        for p in range(n_direct):
            g, w, m, v = (r[...] for r in ins[4 * p:4 * p + 4])
            d, m2, v2 = _adam(w, g, m, v)
            outs[4 * p][...], outs[4 * p + 1][...], outs[4 * p + 2][...], outs[4 * p + 3][...] = g, d, m2, v2
        o = 4 * n_direct
        da_r, da_i, dbb_r, dbb_i, lr, li, ldt, bt_r, bt_i, lbr, lbi, qr, qi = (r[...] for r in ins[o:o + 13])
        dt = jnp.exp(ldt)
        g_br = qr * dbb_r + qi * dbb_i
        g_bi = qr * dbb_i - qi * dbb_r
        dq_r = jnp.sum(bt_r * dbb_r + bt_i * dbb_i, axis=1, keepdims=True)
        dq_i = jnp.sum(bt_r * dbb_i - bt_i * dbb_r, axis=1, keepdims=True)
        den = lr * lr + li * li
        cr, ci = lr / den, li / den
        gl_r = da_r + (cr * dq_r - ci * dq_i)
        gl_i = da_i + (cr * dq_i + ci * dq_r)
        w_r = qr * cr + qi * ci
        w_i = qi * cr - qr * ci
        g_lr = dt * (lbr * gl_r + lbi * gl_i) + (-w_r * dq_r - w_i * dq_i)
        g_li = dt * (lbr * gl_i - lbi * gl_r) + (-w_r * dq_i + w_i * dq_r)
        m_r = lr * lbr - li * lbi
        m_i = lr * lbi + li * lbr
        g_ldt = jnp.sum(m_r * gl_r + m_i * gl_i, axis=2, keepdims=True) * dt
        grads = [g_lr, g_li, g_ldt, g_br, g_bi]
        base_in, base_out = o + 13, 4 * n_direct
        for p, g in enumerate(grads):
            w, m, v = (r[...] for r in ins[base_in + 3 * p:base_in + 3 * p + 3])
            d, m2, v2 = _adam(w, g, m, v)
            q = base_out + 4 * p
            outs[q][...], outs[q + 1][...], outs[q + 2][...], outs[q + 3][...] = g, d, m2, v2

    out_shape = []
    for quad in direct:
        out_shape += [jax.ShapeDtypeStruct(quad[1].shape, F32)] * 4
    for k in chain:
        out_shape += [jax.ShapeDtypeStruct(ssm["w_" + k].shape, F32)] * 4
    res = pl.pallas_call(body, name="small_update", out_shape=out_shape, compiler_params=_params())(*flat)
    return [tuple(res[4 * p:4 * p + 4]) for p in range(n_direct + len(chain))]


def _sum_slots(name, pack):
    def body(p_ref, o_ref):
        acc = p_ref[0].astype(F32)
        for k in range(1, N_DEV):
            acc = acc + p_ref[k].astype(F32)
        o_ref[...] = acc

    return pl.pallas_call(body, name=name, out_shape=jax.ShapeDtypeStruct(pack.shape[1:], F32),
                          compiler_params=_params())(pack)


def _adam_big(name, recv, w, m, v, tr, comm=None):
    _, rows, cols = recv.shape

    def body(r_ref, w_ref, m_ref, v_ref, g_ref, d_ref, m2_ref, v2_ref):
        g = r_ref[0].astype(F32)
        for k in range(1, N_DEV):
            g = g + r_ref[k].astype(F32)
        d, m2, v2 = _adam(w_ref[...], g, m_ref[...], v_ref[...])
        g_ref[...], d_ref[...], m2_ref[...], v2_ref[...] = g, d, m2, v2

    blk = pl.BlockSpec((tr, cols), lambda i: (i, 0))
    shp = jax.ShapeDtypeStruct((rows, cols), F32)
    return _call(
        body, comm, name=name, grid=(rows // tr,),
        in_specs=[pl.BlockSpec((N_DEV, tr, cols), lambda i: (0, i, 0)), blk, blk, blk],
        out_specs=[blk] * 4, out_shape=[shp] * 4, operands=[recv, w, m, v])


def _chunk_block(tm, d):
    return pl.BlockSpec((N_CHUNK, tm // N_CHUNK, d), lambda i: (0, i, 0))


def _interleave(block):
    c, n, d = block.shape
    return pltpu.einshape("cjd->jcd", block).reshape(n * c, d)


def _norm_in(x3, g_pre, tm):
    _, steps, d_model = x3.shape
    seq = steps * N_CHUNK

    def body(x_ref, g_ref, xp_ref, h_ref, ht_ref):
        x = _interleave(x_ref[...])
        xp_ref[...] = x
        r = lax.rsqrt(jnp.mean(x * x, axis=-1, keepdims=True) + EPS)
        h = x * r * g_ref[...]
        h_ref[...] = h.astype(h_ref.dtype)
        ht_ref[...] = h.T.astype(ht_ref.dtype)

    rows = pl.BlockSpec((tm, d_model), lambda i: (i, 0))
    return pl.pallas_call(
        body, name="norm_in", grid=(seq // tm,),
        in_specs=[_chunk_block(tm, d_model), pl.BlockSpec((1, d_model), lambda i: (0, 0))],
        out_specs=[rows, rows, pl.BlockSpec((d_model, tm), lambda i: (0, i))],
        out_shape=[jax.ShapeDtypeStruct((seq, d_model), F32), jax.ShapeDtypeStruct((seq, d_model), MXU_DTYPE),
                   jax.ShapeDtypeStruct((d_model, seq), MXU_DTYPE)],
        compiler_params=_params(("parallel",)),
    )(x3, g_pre)


GATHER_ORDER = (0, 1, 4, 2, 6, 5, 3, 7)
PIECE_COLS = 256


def _fwd_in(h, w_pieces, me, comm, tm):
    seq, d_model = h.shape
    n_p, _, gw = w_pieces.shape
    n_i = seq // tm
    cn = comm.n
    consume = [(rel, r) for rel in (0, 1) for r in range(n_p)]
    consume += [(rel, r) for r in range(n_p) for rel in (4, 2, 5, 3)]
    consume += [(rel, r) for r in range(n_p) for rel in (6, 7)]
    n_q = len(consume)
    order = jnp.stack([jnp.bitwise_xor(me, rel) * n_p + r for rel, r in consume]).astype(jnp.int32)

    def body(order_ref, h_hbm, w_hbm, *rest):
        c_in, rest = rest[:cn], rest[cn:]
        proj_ref, wing = rest[0], rest[1]
        c_out, rest = rest[2:2 + cn], rest[2 + cn:]
        wbuf, send_sems, recv_sems, own_sems, load_sems, h_all, h_sems = rest[:7]
        c_sems = rest[7:]
        k, i = pl.program_id(0), pl.program_id(1)
        h_loads = [pltpu.make_async_copy(h_hbm.at[pl.ds(t * tm, tm)], h_all.at[pl.ds(t * tm, tm)], h_sems.at[t])
                   for t in range(n_i)]
        x, y, c = (lax.axis_index(a) for a in AXES)
        me_ = 4 * x + 2 * y + c

        def dev(rel):
            return _flip(x, (rel >> 2) & 1), _flip(y, (rel >> 1) & 1), _flip(c, rel & 1)

        def slot(rel):
            px, py, pc = dev(rel)
            return 4 * px + 2 * py + pc

        def remote(src, block, r, to_rel, sem):
            return pltpu.make_async_remote_copy(
                src_ref=src, dst_ref=wing.at[block, r], send_sem=send_sems.at[sem, r], recv_sem=recv_sems.at[sem, r],
                device_id=dev(to_rel), device_id_type=pl.DeviceIdType.MESH)

        pieces = range(n_p)
        own = [pltpu.make_async_copy(w_hbm.at[r], wing.at[me_, r], own_sems.at[r]) for r in pieces]
        first_hand = {p: [remote(w_hbm.at[r], me_, r, GATHER_ORDER[p], p - 1) for r in pieces] for p in (1, 2, 3)}
        relay = {2: (c == 0, [remote(wing.at[slot(4), r], slot(4), r, 2, 3) for r in pieces]),
                 3: (c == 1, [remote(wing.at[slot(2), r], slot(2), r, 4, 3) for r in pieces])}
        passed_on = {p: [remote(wing.at[slot(GATHER_ORDER[p]), r], slot(GATHER_ORDER[p]), r, 1, p + 2) for r in pieces]
                     for p in (2, 3, 4)}

        def load(q):
            rel, r = consume[q]
            return pltpu.make_async_copy(wing.at[slot(rel), r], wbuf.at[q % 2], load_sems.at[q % 2])

        def take(q):
            rel, r = consume[q]
            p = GATHER_ORDER.index(rel)
            if p == 0:
                own[r].wait()
            else:
                remote(w_hbm.at[r], slot(rel), r, rel, p - 1).wait_recv()
            if p in relay:
                pl.when(relay[p][0])(relay[p][1][r].start)
            if p in passed_on:
                passed_on[p][r].start()
            load(q).start()

        @pl.when((k == 0) & (i == 0))
        def _():
            for r in pieces:
                own[r].start()
            for r in pieces:
                for p in (1, 2, 3):
                    first_hand[p][r].start()
            for cp in h_loads:
                cp.start()
            comm.start(c_in, c_out, c_sems)
            take(0)

        for t in range(n_i):
            pl.when((k == 0) & (i == t))(h_loads[t].wait)

        for q in range(n_q):
            @pl.when((k == q) & (i == 0))
            def _():
                load(q).wait()

            if q + 1 < n_q:
                @pl.when((k == q) & (i == n_i - 1))
                def _():
                    take(q + 1)

        proj_ref[...] = _dot(h_all[pl.ds(pl.multiple_of(i * tm, tm), tm), :], wbuf[k % 2]).astype(proj_ref.dtype)

        @pl.when((k == n_q - 1) & (i == n_i - 1))
        def _():
            for p in first_hand:
                for cp in first_hand[p]:
                    cp.wait_send()
            for p in passed_on:
                for cp in passed_on[p]:
                    cp.wait_send()
            for cond, cps in relay.values():
                for cp in cps:
                    pl.when(cond)(cp.wait_send)
            comm.finish(c_in, c_out, c_sems)

    any_ = pl.BlockSpec(memory_space=pl.ANY)
    grid_spec = pltpu.PrefetchScalarGridSpec(
        num_scalar_prefetch=1, grid=(n_q, n_i),
        in_specs=[any_, any_] + [any_] * cn,
        out_specs=[pl.BlockSpec((tm, gw), lambda k, i, o: (i, o[k])), any_] + [any_] * cn,
        scratch_shapes=[pltpu.VMEM((2, d_model, gw), w_pieces.dtype), pltpu.SemaphoreType.DMA((N_DEV - 1, n_p)),
                        pltpu.SemaphoreType.DMA((N_DEV - 1, n_p)), pltpu.SemaphoreType.DMA((n_p,)),
                        pltpu.SemaphoreType.DMA((2,)), pltpu.VMEM((seq, d_model), h.dtype),
                        pltpu.SemaphoreType.DMA((n_i,))]
        + comm.scratch())
    res = pl.pallas_call(
        body, name="fwd_in", grid_spec=grid_spec,
        out_shape=[jax.ShapeDtypeStruct((seq, N_DEV * n_p * gw), PROJ_DTYPE),
                   jax.ShapeDtypeStruct((N_DEV, n_p, d_model, gw), w_pieces.dtype)] + comm.out_shape(),
        compiler_params=_params(("arbitrary", "arbitrary")),
    )(order, h, w_pieces, *comm.operands)
    return res[0], res[1], list(res[2:])


def _shift_prev(a):
    n = a.shape[0]
    last = a[n - N_CHUNK:, :]
    row = lax.broadcasted_iota(jnp.int32, last.shape, 0)
    wrap = jnp.where(row == 0, 0.0, pltpu.roll(last, 1, axis=0))
    return jnp.concatenate([wrap, a[:n - N_CHUNK, :]], axis=0)


def _shift_next(a):
    first = a[:N_CHUNK, :]
    row = lax.broadcasted_iota(jnp.int32, first.shape, 0)
    wrap = jnp.where(row == N_CHUNK - 1, 0.0, pltpu.roll(first, N_CHUNK - 1, axis=0))
    return jnp.concatenate([a[N_CHUNK:, :], wrap], axis=0)


def _conv_specs(seq, d_conv):
    nblk = d_conv // LANES
    return [pl.BlockSpec((seq, LANES), functools.partial(lambda i, o: (0, o + i), o=q * nblk)) for q in range(4)]


def _conv_fwd(proj, conv_w8, conv_b, d_conv):
    seq = proj.shape[0]

    def body(bg_ref, cg_ref, v_ref, zc_ref, w_ref, b_ref, y_ref):
        cv = cg_ref[...].astype(F32) * v_ref[...].astype(F32)
        s1 = _shift_prev(cv)
        s2 = _shift_prev(s1)
        conv = b_ref[...] + w_ref[0:1, :] * s2 + w_ref[1:2, :] * s1 + w_ref[2:3, :] * cv
        z = zc_ref[...].astype(F32)
        y_ref[...] = (bg_ref[...].astype(F32) * conv * (z * _sigmoid(z))).astype(y_ref.dtype)

    col = pl.BlockSpec((seq, LANES), lambda i: (0, i))
    return pl.pallas_call(
        body, name="conv_fwd", grid=(d_conv // LANES,),
        in_specs=_conv_specs(seq, d_conv) + [pl.BlockSpec((8, LANES), lambda i: (0, i)), pl.BlockSpec((1, LANES), lambda i: (0, i))],
        out_specs=col, out_shape=jax.ShapeDtypeStruct((seq, d_conv), MXU_DTYPE),
        compiler_params=_params(("parallel",)),
    )(proj, proj, proj, proj, conv_w8, conv_b)


def _conv_bwd(proj, dyc, conv_w8, conv_b, d_conv, comm=None):
    seq = proj.shape[0]

    def body(bg_ref, cg_ref, v_ref, zc_ref, dy_ref, w_ref, b_ref, d4_ref, dcb_ref, dcw_ref):
        bg, cg, v, z = (r[...].astype(F32) for r in (bg_ref, cg_ref, v_ref, zc_ref))
        w0, w1, w2 = w_ref[0:1, :], w_ref[1:2, :], w_ref[2:3, :]
        cv = cg * v
        s1 = _shift_prev(cv)
        s2 = _shift_prev(s1)
        conv = b_ref[...] + w0 * s2 + w1 * s1 + w2 * cv
        sig = _sigmoid(z)
        dy = dy_ref[...].astype(F32)
        g1 = dy * (z * sig)
        d_conv_ = g1 * bg
        d4_ref[0] = (g1 * conv).astype(d4_ref.dtype)
        d4_ref[3] = (dy * bg * conv * (sig * (1.0 + z * (1.0 - sig)))).astype(d4_ref.dtype)
        n1 = _shift_next(d_conv_)
        n2 = _shift_next(n1)
        d_cv = w2 * d_conv_ + w1 * n1 + w0 * n2
        d4_ref[1] = (d_cv * v).astype(d4_ref.dtype)
        d4_ref[2] = (d_cv * cg).astype(d4_ref.dtype)
        dcb_ref[...] = jnp.sum(d_conv_, axis=0, keepdims=True)
        rows = [jnp.sum(d_conv_ * s, axis=0, keepdims=True) for s in (s2, s1, cv)]
        dcw_ref[...] = jnp.concatenate(rows + [jnp.zeros((5, LANES), F32)], axis=0)

    col = pl.BlockSpec((seq, LANES), lambda i: (0, i))
    return _call(
        body, comm, name="conv_bwd", grid=(d_conv // LANES,),
        in_specs=_conv_specs(seq, d_conv) + [col, pl.BlockSpec((8, LANES), lambda i: (0, i)), pl.BlockSpec((1, LANES), lambda i: (0, i))],
        out_specs=[pl.BlockSpec((4, seq, LANES), lambda i: (0, 0, i)), pl.BlockSpec((1, LANES), lambda i: (0, i)),
                   pl.BlockSpec((8, LANES), lambda i: (0, i))],
        out_shape=[jax.ShapeDtypeStruct((4, seq, d_conv), MXU_DTYPE), jax.ShapeDtypeStruct((1, d_conv), F32),
                   jax.ShapeDtypeStruct((8, d_conv), F32)],
        operands=[proj, proj, proj, proj, dyc, conv_w8, conv_b])


def _cmul(ar, ai, br, bi):
    return ar * br - ai * bi, ar * bi + ai * br


def _cpow(ar, ai, n):
    rr, ri = jnp.ones_like(ar), jnp.zeros_like(ai)
    while n:
        if n & 1:
            rr, ri = _cmul(rr, ri, ar, ai)
        n >>= 1
        if n:
            ar, ai = _cmul(ar, ai, ar, ai)
    return rr, ri


def _down(v, k):
    row = lax.broadcasted_iota(jnp.int32, v.shape, 0)
    return jnp.where(row >= k, pltpu.roll(v, k, axis=0), 0.0)


def _up(v, k):
    row = lax.broadcasted_iota(jnp.int32, v.shape, 0)
    return jnp.where(row < N_CHUNK - k, pltpu.roll(v, N_CHUNK - k, axis=0), 0.0)


def _chunk_carry(fr, fi, mr, mi, shift):
    vr, vi = shift(fr, 1), shift(fi, 1)
    for k in (1, 2, 4):
        pr, pi = _cmul(mr, mi, shift(vr, k), shift(vi, k))
        vr, vi = vr + pr, vi + pi
        mr, mi = _cmul(mr, mi, mr, mi)
    return vr, vi


def _tile(ref, j, width, part):
    return ref.at[pl.ds(pl.multiple_of(j * N_CHUNK, N_CHUNK), N_CHUNK), pl.ds(part * width, width)]


def _row(t, k):
    return jnp.broadcast_to(t[k:k + 1, :], t.shape)


def _power_table(tab_ref, ar, ai, steps, width):
    e = lax.broadcasted_iota(jnp.int32, ar.shape, 0) + 1
    rr, ri = jnp.ones_like(ar), jnp.zeros_like(ai)
    br, bi = ar, ai
    for bit in range(4):
        mr, mi = _cmul(rr, ri, br, bi)
        take = ((e >> bit) & 1) == 1
        rr, ri = jnp.where(take, mr, rr), jnp.where(take, mi, ri)
        if bit < 3:
            br, bi = _cmul(br, bi, br, bi)
    _tile(tab_ref, 0, width, 0)[...] = rr
    _tile(tab_ref, 0, width, 1)[...] = ri

    def step(m, carry):
        tr, ti = _cmul(carry[0], carry[1], br, bi)
        _tile(tab_ref, m, width, 0)[...] = tr
        _tile(tab_ref, m, width, 1)[...] = ti
        return tr, ti

    lax.fori_loop(1, steps // N_CHUNK, step, (rr, ri))


def _last_power(tab_ref, steps, width):
    shape = (N_CHUNK, width)
    return (jnp.broadcast_to(tab_ref[steps - 1:steps, 0:width], shape),
            jnp.broadcast_to(tab_ref[steps - 1:steps, width:2 * width], shape))


def _scan_fwd(s_ref, ar, ai, steps, width):
    def step(j, carry):
        sr, si = carry
        nr = ar * sr - ai * si + _tile(s_ref, j, width, 0)[...]
        ni = ar * si + ai * sr + _tile(s_ref, j, width, 1)[...]
        _tile(s_ref, j, width, 0)[...] = nr
        _tile(s_ref, j, width, 1)[...] = ni
        return nr, ni

    z = jnp.zeros((N_CHUNK, width), F32)
    return lax.fori_loop(0, steps, step, (z, z), unroll=4)


def _scan_both(s_ref, g_ref, ar, ai, steps, width):
    def step(q, carry):
        sr, si, gr, gi = carry
        j, jb = q, steps - 1 - q
        nsr = ar * sr - ai * si + _tile(s_ref, j, width, 0)[...]
        nsi = ar * si + ai * sr + _tile(s_ref, j, width, 1)[...]
        ngr = ar * gr + ai * gi + _tile(g_ref, jb, width, 0)[...]
        ngi = ar * gi - ai * gr + _tile(g_ref, jb, width, 1)[...]
        _tile(s_ref, j, width, 0)[...] = nsr
        _tile(s_ref, j, width, 1)[...] = nsi
        _tile(g_ref, jb, width, 0)[...] = ngr
        _tile(g_ref, jb, width, 1)[...] = ngi
        return nsr, nsi, ngr, ngi

    z = jnp.zeros((N_CHUNK, width), F32)
    return lax.fori_loop(0, steps, step, (z, z, z, z), unroll=2)


def _patch_fwd(s_ref, tab_ref, cr, ci, steps, width):
    def tile(m, _):
        tr, ti = _tile(tab_ref, m, width, 0)[...], _tile(tab_ref, m, width, 1)[...]
        for k in range(N_CHUNK):
            fr, fi = _cmul(_row(tr, k), _row(ti, k), cr, ci)
            j = m * N_CHUNK + k
            _tile(s_ref, j, width, 0)[...] += fr
            _tile(s_ref, j, width, 1)[...] += fi
        return 0

    lax.fori_loop(0, steps // N_CHUNK, tile, 0)


def _lam_rows(lam_ref, hh, width):
    return (jnp.broadcast_to(lam_ref[hh, 0:1, :], (N_CHUNK, width)),
            jnp.broadcast_to(lam_ref[hh, 1:2, :], (N_CHUNK, width)))


def _ssm_specs(seq, col0):
    return dict(
        col=pl.BlockSpec((seq, LANES), lambda i: (0, col0 + i)),
        lam=pl.BlockSpec((2, 2, HALF_W), lambda i: (i, 0, 0)),
        bb=pl.BlockSpec((2, HALF_CH, 2 * HALF_W), lambda i: (i, 0, 0)),
        cc=pl.BlockSpec((2, 2 * HALF_W, HALF_CH), lambda i: (i, 0, 0)),
        vec=pl.BlockSpec((1, LANES), lambda i: (0, i)),
        out=pl.BlockSpec((seq, LANES), lambda i: (0, i)),
    )


def _ssm_fwd(proj, lam, bbcat, cccat, d_skip, d_ssm, u_col0, comm=None):
    seq = proj.shape[0]
    steps = seq // N_CHUNK

    def body(u_ref, lam_ref, bb_ref, cc_ref, d_ref, yp_ref, s_ref, tab_ref):
        for hh in range(2):
            lanes = slice(HALF_CH * hh, HALF_CH * (hh + 1))
            u_half = u_ref[:, lanes].astype(F32)
            ar, ai = _lam_rows(lam_ref, hh, HALF_W)
            _power_table(tab_ref, ar, ai, steps, HALF_W)
            s_ref[...] = _dot(u_half.astype(MXU_DTYPE), bb_ref[hh])
            fr, fi = _scan_fwd(s_ref, ar, ai, steps, HALF_W)
            pr, pi = _last_power(tab_ref, steps, HALF_W)
            cr, ci = _chunk_carry(fr, fi, pr, pi, _down)
            _patch_fwd(s_ref, tab_ref, cr, ci, steps, HALF_W)
            y = _dot(s_ref[...].astype(MXU_DTYPE), cc_ref[hh])
            yp_ref[:, lanes] = y + d_ref[:, lanes] * u_half

    sp = _ssm_specs(seq, u_col0 // LANES)
    return _call(
        body, comm, name="ssm_fwd", grid=(d_ssm // LANES,),
        in_specs=[sp["col"], sp["lam"], sp["bb"], sp["cc"], sp["vec"]], out_specs=[sp["out"]],
        out_shape=[jax.ShapeDtypeStruct((seq, d_ssm), F32)],
        scratch_shapes=[pltpu.VMEM((seq, 2 * HALF_W), F32), pltpu.VMEM((steps, 2 * HALF_W), F32)],
        operands=[proj, lam, bbcat, cccat, d_skip])


def _ssm_bwd(proj, dyp, lam, bbcat, cccat, d_skip, d_ssm, u_col0, comm=None):
    seq = proj.shape[0]
    steps = seq // N_CHUNK
    n_half = 2 * d_ssm // LANES
    width = HALF_W

    def body(u_ref, dyp_ref, lam_ref, bb_ref, cc_ref, d_ref, du_ref, dbb_ref, dcc_ref, da_ref, dd_ref,
             s_ref, g_ref, tab_ref):
        n_tiles = steps // N_CHUNK
        for hh in range(2):
            lanes = slice(HALF_CH * hh, HALF_CH * (hh + 1))
            u_half, dy_half = u_ref[:, lanes].astype(F32), dyp_ref[:, lanes].astype(F32)
            dy_mx = dy_half.astype(MXU_DTYPE)
            ar, ai = _lam_rows(lam_ref, hh, width)
            _power_table(tab_ref, ar, ai, steps, width)
            s_ref[...] = _dot(u_half.astype(MXU_DTYPE), bb_ref[hh])
            g_ref[...] = _dot_nt(dy_mx, cc_ref[hh])
            fr, fi, lr_, li_ = _scan_both(s_ref, g_ref, ar, ai, steps, width)
            pr, pi = _last_power(tab_ref, steps, width)
            cr, ci = _chunk_carry(fr, fi, pr, pi, _down)
            gr, gi = _chunk_carry(lr_, li_, pr, -pi, _up)

            def tile(m, carry):
                sr, si, accr, acci = carry
                t1r, t1i = _tile(tab_ref, m, width, 0)[...], _tile(tab_ref, m, width, 1)[...]
                mb = n_tiles - 1 - m
                t2r, t2i = _tile(tab_ref, mb, width, 0)[...], _tile(tab_ref, mb, width, 1)[...]
                for k in range(N_CHUNK):
                    j = m * N_CHUNK + k
                    xr, xi = _cmul(_row(t1r, k), _row(t1i, k), cr, ci)
                    nsr = _tile(s_ref, j, width, 0)[...] + xr
                    nsi = _tile(s_ref, j, width, 1)[...] + xi
                    _tile(s_ref, j, width, 0)[...] = nsr
                    _tile(s_ref, j, width, 1)[...] = nsi
                    qr, qi = _row(t2r, N_CHUNK - 1 - k), _row(t2i, N_CHUNK - 1 - k)
                    ngr = _tile(g_ref, j, width, 0)[...] + (qr * gr + qi * gi)
                    ngi = _tile(g_ref, j, width, 1)[...] + (qr * gi - qi * gr)
                    _tile(g_ref, j, width, 0)[...] = ngr
                    _tile(g_ref, j, width, 1)[...] = ngi
                    accr = accr + (sr * ngr + si * ngi)
                    acci = acci + (sr * ngi - si * ngr)
                    sr, si = nsr, nsi
                return sr, si, accr, acci

            z = jnp.zeros((N_CHUNK, width), F32)
            _, _, accr, acci = lax.fori_loop(0, n_tiles, tile, (cr, ci, z, z))
            da_ref[hh, :, 0:width] = jnp.sum(accr, axis=0, keepdims=True)
            da_ref[hh, :, width:2 * width] = jnp.sum(acci, axis=0, keepdims=True)

            g_mx = g_ref[...].astype(MXU_DTYPE)
            dcc_ref[hh] = _dot_tn(dy_mx, s_ref[...].astype(MXU_DTYPE)).T
            dbb_ref[hh] = _dot_tn(u_half.astype(MXU_DTYPE), g_mx)
            du = _dot_nt(g_mx, bb_ref[hh]) + d_ref[:, lanes] * dy_half
            du_ref[:, lanes] = du.astype(du_ref.dtype)
            dd_ref[:, lanes] = jnp.sum(dy_half * u_half, axis=0, keepdims=True)

    sp = _ssm_specs(seq, u_col0 // LANES)
    return _call(
        body, comm, name="ssm_bwd", grid=(d_ssm // LANES,),
        in_specs=[sp["col"], sp["out"], sp["lam"], sp["bb"], sp["cc"], sp["vec"]],
        out_specs=[sp["out"], sp["bb"], sp["cc"], pl.BlockSpec((2, 1, 2 * width), lambda i: (i, 0, 0)), sp["vec"]],
        out_shape=[jax.ShapeDtypeStruct((seq, d_ssm), MXU_DTYPE),
                   jax.ShapeDtypeStruct((n_half, HALF_CH, 2 * width), F32),
                   jax.ShapeDtypeStruct((n_half, 2 * width, HALF_CH), F32),
                   jax.ShapeDtypeStruct((n_half, 1, 2 * width), F32),
                   jax.ShapeDtypeStruct((1, d_ssm), F32)],
        scratch_shapes=[pltpu.VMEM((seq, 2 * width), F32), pltpu.VMEM((seq, 2 * width), F32),
                        pltpu.VMEM((steps, 2 * width), F32)],
        operands=[proj, dyp, lam, bbcat, cccat, d_skip])


def _tail(xp, t3, proj, yconv, yp, w_glu, b_glu, w_out, g_post, zs_col0, tm):
    seq, d_model = xp.shape
    d_conv, d_ssm = yconv.shape[1], yp.shape[1]
    d_mix = d_conv + d_ssm
    assert zs_col0 % d_ssm == 0

    def body(x_ref, t_ref, zs_ref, yc_ref, yp_ref, wglu_hbm, bglu_ref, wout_hbm, gpost_ref,
             dy_ref, do_ref, mixt_ref, dyc_ref, dyp_ref, dzs_ref, ygt_ref, dq_ref, loss_ref, dgpost_ref, dbglu_ref,
             wglu, wout):
        @pl.when(pl.program_id(0) == 0)
        def _():
            pltpu.sync_copy(wglu_hbm, wglu)
            pltpu.sync_copy(wout_hbm, wout)
            loss_ref[...] = jnp.zeros_like(loss_ref)
            dgpost_ref[...] = jnp.zeros_like(dgpost_ref)
            dbglu_ref[...] = jnp.zeros_like(dbglu_ref)

        a = yp_ref[...]
        th = jnp.tanh(GELU_C * (a + GELU_K * (a * a * a)))
        yg = a * (0.5 * (1.0 + th))
        dgelu = 0.5 * (1.0 + th) + 0.5 * a * (1.0 - th * th) * (GELU_C * (1.0 + 3.0 * GELU_K * a * a))
        yg_mx = yg.astype(MXU_DTYPE)
        sq = _sigmoid(_dot(yg_mx, wglu[...]) + bglu_ref[...])
        y2 = yg * sq
        zs = zs_ref[...].astype(F32)
        sz = _sigmoid(zs)
        silz = zs * sz
        yc, ys = yc_ref[...].astype(F32), y2 * silz
        mix = jnp.concatenate([yc, ys], axis=1).astype(MXU_DTYPE)
        mixt_ref[0:d_conv, :] = yc.T.astype(MXU_DTYPE)
        mixt_ref[d_conv:, :] = ys.T.astype(MXU_DTYPE)
        o = _dot(mix, wout[...])
        r2 = lax.rsqrt(jnp.mean(o * o, axis=-1, keepdims=True) + EPS)
        on = o * r2
        gpost = gpost_ref[...]
        err = (x_ref[...] + on * gpost) - _interleave(t_ref[...])
        loss_ref[...] += 0.5 * jnp.sum(jnp.mean(err * err, axis=-1, keepdims=True), axis=0, keepdims=True)
        dy = err * (1.0 / d_model)
        dy_ref[...] = dy
        dgpost_ref[...] += jnp.sum(dy * on, axis=0, keepdims=True)
        d_on = dy * gpost
        d_o = r2 * (d_on - on * jnp.mean(d_on * on, axis=-1, keepdims=True))
        do_mx = d_o.astype(MXU_DTYPE)
        do_ref[...] = do_mx
        d_mix_ = _dot_nt(do_mx, wout[...])
        dyc_ref[...] = d_mix_[:, :d_conv].astype(dyc_ref.dtype)
        d_yssm = d_mix_[:, d_conv:]
        d_y2 = d_yssm * silz
        dzs_ref[...] = (d_yssm * y2 * (sz * (1.0 + zs * (1.0 - sz)))).astype(dzs_ref.dtype)
        d_q = d_y2 * yg * (sq * (1.0 - sq))
        dq_mx = d_q.astype(MXU_DTYPE)
        dq_ref[...] = dq_mx
        ygt_ref[...] = yg.T.astype(MXU_DTYPE)
        dbglu_ref[...] += jnp.sum(d_q, axis=0, keepdims=True)
        d_yg = d_y2 * sq + _dot_nt(dq_mx, wglu[...])
        dyp_ref[...] = (d_yg * dgelu).astype(dyp_ref.dtype)

    def rows(width, col=0):
        return pl.BlockSpec((tm, width), lambda i: (i, col))

    def fixed(width):
        return pl.BlockSpec((1, width), lambda i: (0, 0))

    def cols(height):
        return pl.BlockSpec((height, tm), lambda i: (0, i))

    any_ = pl.BlockSpec(memory_space=pl.ANY)
    return pl.pallas_call(
        body, name="tail", grid=(seq // tm,),
        in_specs=[rows(d_model), _chunk_block(tm, d_model), rows(d_ssm, zs_col0 // d_ssm), rows(d_conv), rows(d_ssm),
                  any_, fixed(d_ssm), any_, fixed(d_model)],
        out_specs=[rows(d_model), rows(d_model), cols(d_mix), rows(d_conv), rows(d_ssm), rows(d_ssm), cols(d_ssm),
                   rows(d_ssm), fixed(LANES), fixed(d_model), fixed(d_ssm)],
        out_shape=[jax.ShapeDtypeStruct((seq, d_model), F32), jax.ShapeDtypeStruct((seq, d_model), MXU_DTYPE),
                   jax.ShapeDtypeStruct((d_mix, seq), MXU_DTYPE), jax.ShapeDtypeStruct((seq, d_conv), MXU_DTYPE),
                   jax.ShapeDtypeStruct((seq, d_ssm), MXU_DTYPE), jax.ShapeDtypeStruct((seq, d_ssm), MXU_DTYPE),
                   jax.ShapeDtypeStruct((d_ssm, seq), MXU_DTYPE), jax.ShapeDtypeStruct((seq, d_ssm), MXU_DTYPE),
                   jax.ShapeDtypeStruct((1, LANES), F32), jax.ShapeDtypeStruct((1, d_model), F32),
                   jax.ShapeDtypeStruct((1, d_ssm), F32)],
        scratch_shapes=[pltpu.VMEM(w_glu.shape, MXU_DTYPE), pltpu.VMEM(w_out.shape, MXU_DTYPE)],
        compiler_params=_params(("arbitrary",)),
    )(xp, t3, proj, yconv, yp, w_glu, b_glu, w_out, g_post)


def _bwd_in(d4, du, dzs, gr, win_g, xp, dy, g_pre, comm, tm):
    seq, d_model = xp.shape
    nb, n_p, _, gw = win_g.shape
    nc = n_p * gw
    per = d4.shape[2] // gr

    def body(d4_ref, du_ref, dzs_ref, w_hbm, x_ref, dy_ref, g_ref, gx_ref, dg_ref, w_all, w_sems):
        def granule(g):
            p, cols = g // per, slice(g % per * gr, (g % per + 1) * gr)
            if p < 4:
                return d4_ref[p, :, cols]
            return du_ref[:, cols] if p == 4 else dzs_ref[:, cols]

        i = pl.program_id(0)
        loads = [[pltpu.make_async_copy(w_hbm.at[k, r], w_all.at[k, :, pl.ds(r * gw, gw)], w_sems.at[k, r])
                  for r in range(n_p)] for k in range(nb)]

        @pl.when(i == 0)
        def _():
            dg_ref[...] = jnp.zeros_like(dg_ref)
            for row in loads:
                for cp in row:
                    cp.start()

        dh = None
        for k in range(nb):
            @pl.when(i == 0)
            def _():
                for cp in loads[k]:
                    cp.wait()

            dp = jnp.concatenate([granule(g) for g in range(k * nc // gr, (k + 1) * nc // gr)], axis=1)
            part = _dot_nt(dp, w_all[k])
            dh = part if dh is None else dh + part

        x = x_ref[...]
        r = lax.rsqrt(jnp.mean(x * x, axis=-1, keepdims=True) + EPS)
        xn = x * r
        dg_ref[...] += jnp.sum(dh * xn, axis=0, keepdims=True)
        dxn = dh * g_ref[...]
        gx_ref[...] = r * (dxn - xn * jnp.mean(dxn * xn, axis=-1, keepdims=True)) + dy_ref[...]

    row = pl.BlockSpec((tm, d_model), lambda i: (i, 0))
    vec = pl.BlockSpec((1, d_model), lambda i: (0, 0))
    return _call(
        body, comm, name="bwd_in", grid=(seq // tm,),
        in_specs=[pl.BlockSpec((4, tm, d4.shape[2]), lambda i: (0, i, 0)),
                  pl.BlockSpec((tm, du.shape[1]), lambda i: (i, 0)), pl.BlockSpec((tm, dzs.shape[1]), lambda i: (i, 0)),
                  pl.BlockSpec(memory_space=pl.ANY), row, row, vec],
        out_specs=[row, vec],
        out_shape=[jax.ShapeDtypeStruct((seq, d_model), F32), jax.ShapeDtypeStruct((1, d_model), F32)],
        scratch_shapes=[pltpu.VMEM((nb, d_model, nc), win_g.dtype), pltpu.SemaphoreType.DMA((nb, n_p))],
        operands=[d4, du, dzs, win_g, xp, dy, g_pre])


def _lookup(g, table):
    out = jnp.int32(table[0])
    for gi in range(1, len(table)):
        if table[gi] != table[gi - 1]:
            out = jnp.where(g >= gi, jnp.int32(table[gi]), out)
    return out


def _held(values, used):
    cur = next(v for v, u in zip(values, used) if u)
    out = []
    for v, u in zip(values, used):
        cur = v if u else cur
        out.append(cur)
    return out


def _dw_in(name, ht, d4, du, dzs, granules, gr, nc, tm, comm=None):
    d_model, seq = ht.shape
    per = d4.shape[2] // gr
    piece, col = [g // per for g in granules], [g % per for g in granules]
    sources = [(d4, [p < 4 for p in piece]), (du, [p == 4 for p in piece]), (dzs, [p == 5 for p in piece])]
    sources = [(a, used) for a, used in sources if any(used)]
    select = [next(s for s, (_, used) in enumerate(sources) if used[q]) for q in range(len(granules))]
    owner, place = [g * gr // nc for g in granules], [g * gr % nc // gr for g in granules]

    def body(a_ref, *refs):
        src_refs, o_ref = refs[:-1], refs[-1]
        j = pl.program_id(1)
        for s, ref in enumerate(src_refs):
            @pl.when(_lookup(j, select) == s)
            def _():
                o_ref[...] = _dot(a_ref[...], ref[...]).astype(o_ref.dtype)

    in_specs = [pl.BlockSpec((tm, seq), lambda i, j: (i, 0))]
    for a, used in sources:
        cols = _held(col, used)
        if a.ndim == 3:
            rows = _held(piece, used)
            in_specs.append(pl.BlockSpec((None, seq, gr), functools.partial(
                lambda i, j, rows, cols: (_lookup(j, rows), 0, _lookup(j, cols)), rows=rows, cols=cols)))
        else:
            in_specs.append(pl.BlockSpec((seq, gr), functools.partial(
                lambda i, j, cols: (0, _lookup(j, cols)), cols=cols)))
    return _call(
        body, comm, name=name, grid=(d_model // tm, len(granules)), in_specs=in_specs,
        out_specs=[pl.BlockSpec((None, tm, gr), lambda i, j: (_lookup(j, owner), i, _lookup(j, place)))],
        out_shape=[jax.ShapeDtypeStruct((N_DEV, d_model, nc), MXU_DTYPE)],
        operands=[ht] + [a for a, _ in sources])


def _wgrad(name, at, b, tm, tn, out_shape, out_block, out_index, comm=None):
    m, seq = at.shape
    n = b.shape[1]

    def body(a_ref, b_ref, o_ref):
        o_ref[...] = _dot(a_ref[...], b_ref[...]).astype(o_ref.dtype)

    return _call(
        body, comm, name=name, grid=(n // tn, m // tm),
        in_specs=[pl.BlockSpec((tm, seq), lambda j, i: (i, 0)), pl.BlockSpec((seq, tn), lambda j, i: (0, j))],
        out_specs=[pl.BlockSpec(out_block, lambda j, i: out_index(i, j))],
        out_shape=[jax.ShapeDtypeStruct(out_shape, MXU_DTYPE)],
        operands=[at, b])


def _eye_g():
    return jnp.eye(HALF_G, dtype=F32)


def _bb_blockdiag(bbt_r, bbt_i):
    n_half = bbt_r.shape[0] // HALF_G

    def one(t):
        t = t.reshape(n_half, HALF_G, SSM_GROUP, SSM_STATE)
        t = t[:, :, :, None, :] * _eye_g()[None, :, None, :, None]
        return t.reshape(n_half, HALF_CH, HALF_W)

    return jnp.concatenate([one(bbt_r), one(bbt_i)], axis=-1)


def _cc_blockdiag(c_re, c_im):
    n_half = c_re.shape[0] // HALF_G

    def one(t):
        t = t.reshape(n_half, HALF_G, SSM_GROUP, SSM_STATE)
        t = jnp.transpose(t, (0, 3, 1, 2))
        t = t[:, None, :, :, :] * _eye_g()[None, :, None, :, None]
        return t.reshape(n_half, HALF_W, HALF_CH)

    return jnp.concatenate([one(c_re), one(-c_im)], axis=1)


def _bb_diag(dbb):
    n_half = dbb.shape[0]
    t = dbb.reshape(n_half, HALF_G, SSM_GROUP, 2, HALF_G, SSM_STATE)
    t = jnp.sum(t * _eye_g()[None, :, None, None, :, None], axis=4)
    t = jnp.transpose(t, (3, 0, 1, 2, 4))
    return t.reshape(2, n_half * HALF_G, SSM_GROUP, SSM_STATE)


def _cc_diag(dcc):
    n_half = dcc.shape[0]
    t = dcc.reshape(n_half, 2, HALF_G, SSM_STATE, HALF_G, SSM_GROUP)
    t = jnp.sum(t * _eye_g()[None, None, :, None, :, None], axis=2)
    t = jnp.transpose(t, (1, 0, 3, 4, 2))
    return t.reshape(2, n_half * HALF_G, SSM_GROUP, SSM_STATE)


def _permute_rows(a):
    seq, d = a.shape
    return a.reshape(N_CHUNK, seq // N_CHUNK, d).transpose(1, 0, 2).reshape(seq, d)


def _unpermute_rows(a):
    seq, d = a.shape
    return a.reshape(seq // N_CHUNK, N_CHUNK, d).transpose(1, 0, 2).reshape(seq, d)


def _pack_rows(shape):
    return -(-math.prod(shape) // (8 * LANES)) * 8


def _pack(parts, dtype=F32):
    rows = []
    for p in parts:
        flat = p.reshape(-1).astype(dtype)
        rows.append(jnp.pad(flat, (0, _pack_rows(p.shape) * LANES - flat.shape[0])).reshape(-1, LANES))
    return jnp.concatenate(rows, axis=0)


def _unpack(packed, shapes):
    out, o = [], 0
    for s in shapes:
        n = _pack_rows(s)
        out.append(packed[o:o + n].reshape(-1)[:math.prod(s)].reshape(s))
        o += n
    return out


def kernel(x, norm_pre_g, w_in, conv_w, conv_b, ssm_a_re, ssm_a_im, ssm_log_dt, ssm_b_re, ssm_b_im, ssm_c_re, ssm_c_im, ssm_d, w_glu, b_glu, w_out, norm_post_g, loss_target, m_norm_pre_g, m_w_in, m_conv_w, m_conv_b, m_ssm_a_re, m_ssm_a_im, m_ssm_log_dt, m_ssm_b_re, m_ssm_b_im, m_ssm_c_re, m_ssm_c_im, m_ssm_d, m_w_glu, m_b_glu, m_w_out, m_norm_post_g, v_norm_pre_g, v_w_in, v_conv_w, v_conv_b, v_ssm_a_re, v_ssm_a_im, v_ssm_log_dt, v_ssm_b_re, v_ssm_b_im, v_ssm_c_re, v_ssm_c_im, v_ssm_d, v_w_glu, v_b_glu, v_w_out, v_norm_post_g):
    seq, d_model = x.shape[1], x.shape[2]
    d_conv, d_ssm = conv_b.shape[0], ssm_d.shape[0]
    groups, states = ssm_a_re.shape
    assert x.shape[0] == 1 and seq % (8 * N_CHUNK) == 0 and d_conv == d_ssm
    assert (groups, states) == (d_ssm // SSM_GROUP, SSM_STATE) and d_ssm % LANES == 0
    me = 4 * lax.axis_index("x") + 2 * lax.axis_index("y") + lax.axis_index("c")
    tm = min(512, seq)

    x3 = x[0].reshape(N_CHUNK, seq // N_CHUNK, d_model)
    t3 = loss_target[0].reshape(N_CHUNK, seq // N_CHUNK, d_model)
    row = lambda a: a.reshape(1, -1)
    conv_w8 = jnp.pad(conv_w, ((0, 8 - conv_w.shape[0]), (0, 0)))

    g3 = lambda a: a.reshape(groups, 1, -1)
    bt_re, bt_im = jnp.transpose(ssm_b_re, (0, 2, 1)), jnp.transpose(ssm_b_im, (0, 2, 1))
    lbr, lbi, qr, qi, bbt_r, bbt_i = _ssm_prep(g3(ssm_a_re), g3(ssm_a_im), g3(ssm_log_dt), bt_re, bt_im)
    n_half = groups // HALF_G
    lam = jnp.stack([lbr.reshape(n_half, HALF_W), lbi.reshape(n_half, HALF_W)], axis=1)
    bbcat = _bb_blockdiag(bbt_r, bbt_i).astype(MXU_DTYPE)
    cccat = _cc_blockdiag(ssm_c_re, ssm_c_im).astype(MXU_DTYPE)

    xp, h, ht = _norm_in(x3, row(norm_pre_g), tm)
    nc = w_in.shape[1]
    w_pieces = jnp.transpose(w_in.astype(MXU_DTYPE).reshape(d_model, nc // PIECE_COLS, PIECE_COLS), (1, 0, 2))
    proj, win_g, (convw_g,) = _fwd_in(h, w_pieces, me, _Comm([conv_w8]), min(2048, seq))
    conv_w_full = jnp.transpose(convw_g, (1, 0, 2)).reshape(8, d_conv)
    u_col0, zs_col0 = 4 * d_conv, 4 * d_conv + d_ssm
    yconv = _conv_fwd(proj, conv_w_full, row(conv_b), d_conv)
    (yp,), (wout_g, wglu_g) = _ssm_fwd(proj, lam, bbcat, cccat, row(ssm_d), d_ssm, u_col0,
                                       _Comm([w_out.astype(MXU_DTYPE), w_glu.astype(MXU_DTYPE)]))
    w_out_full = wout_g.reshape(-1, d_model)
    w_glu_full = wglu_g.reshape(-1, d_ssm)
    (dy, d_o, mixt, dyc, dyp, dzs, ygt, dq, loss_part, dgpost, dbglu) = _tail(
        xp, t3, proj, yconv, yp, w_glu_full, row(b_glu), w_out_full, row(norm_post_g), zs_col0, min(256, seq))

    r_out, r_glu, nc = w_out.shape[0], w_glu.shape[0], w_in.shape[1]
    (dwout_p,), _ = _wgrad("dw_out", mixt, d_o, r_out, min(1024, d_model), (N_DEV, r_out, d_model),
                           (None, r_out, min(1024, d_model)), lambda i, j: (i, 0, j))
    (dwglu_p,), _ = _wgrad("dw_glu", ygt, dq, r_glu, d_ssm, (N_DEV, r_glu, d_ssm),
                           (None, r_glu, d_ssm), lambda i, j: (i, 0, 0))
    (d4, dconvb, dconvw), (recv_glu,) = _conv_bwd(proj, dyc, conv_w_full, row(conv_b), d_conv,
                                                  _Comm([], [dwglu_p]))
    late = [k for k in range(N_DEV) if k * nc < u_col0 + d_ssm and (k + 1) * nc > u_col0]
    early = [k for k in range(N_DEV) if k not in late]
    gr = math.gcd(nc, d_conv)
    granules = lambda blocks: [g for k in blocks for g in range(k * nc // gr, (k + 1) * nc // gr)]
    tmw = min(1024, d_model)
    (dwin_e,), (recv_out,) = _dw_in("dw_in_early", ht, d4, None, dzs, granules(early), gr, nc, tmw,
                                    _Comm([], [dwout_p]))
    (du, dbb, dcc, da, dd), (recv_in,) = _ssm_bwd(
        proj, dyp, lam, bbcat, cccat, row(ssm_d), d_ssm, u_col0, _Comm([], [dwin_e], dests={0: early}))
    parts_mx = [_bb_diag(dbb), _cc_diag(dcc)]
    pack_mx = _pack(parts_mx, MXU_DTYPE)
    assert pack_mx.shape[0] % (16 * N_DEV) == 0
    (dwin_l,), (mx_recv,) = _dw_in("dw_in_late", ht, d4, du, dzs, granules(late), gr, nc, tmw,
                                   _Comm([], [pack_mx.reshape(N_DEV, -1, LANES)]))
    mx_mine = _sum_slots("sum_pack_mx", mx_recv)
    da_n = jnp.transpose(da.reshape(n_half, 2, HALF_G, SSM_STATE), (1, 0, 2, 3)).reshape(2, groups, 1, states)
    parts = [dgpost, dconvb, dd, dbglu, dconvw[:3], da_n, loss_part]
    shapes, shapes_mx = [p.shape for p in parts], [p.shape for p in parts_mx]
    (gx_p, dgpre), (pack_g, mx_all, recv_in) = _bwd_in(
        d4, du, dzs, gr, win_g, xp, dy, row(norm_pre_g),
        _Comm([_pack(parts), mx_mine], [dwin_l], dests={2: late}, into={2: recv_in}), min(256, seq))
    res = {}
    quad, (last_g,) = _adam_big("adam_w_in", recv_in, w_in, m_w_in, v_w_in, min(256, d_model),
                                _Comm([_pack([dgpre])]))
    res["w_in"] = tuple(quad)
    (g_gpost, g_convb, g_d, g_bglu, g_convw, g_da, loss_sum) = _unpack(_sum_slots("sum_pack", pack_g), shapes)
    (g_dbb, g_dcc) = _unpack(mx_all.reshape(-1, LANES), shapes_mx)
    (g_gpre,) = _unpack(_sum_slots("sum_last", last_g), [dgpre.shape])
    g_convw = lax.dynamic_slice(g_convw, (0, me * conv_w.shape[1]), conv_w.shape)

    tr = lambda a: jnp.transpose(a, (0, 2, 1))
    direct = [(g_gpre, row(norm_pre_g), row(m_norm_pre_g), row(v_norm_pre_g)),
              (g_convb, row(conv_b), row(m_conv_b), row(v_conv_b)),
              (g_d, row(ssm_d), row(m_ssm_d), row(v_ssm_d)),
              (g_bglu, row(b_glu), row(m_b_glu), row(v_b_glu)),
              (g_gpost, row(norm_post_g), row(m_norm_post_g), row(v_norm_post_g)),
              (g_convw, conv_w, m_conv_w, v_conv_w),
              (g_dcc[0], ssm_c_re, m_ssm_c_re, v_ssm_c_re),
              (-g_dcc[1], ssm_c_im, m_ssm_c_im, v_ssm_c_im)]
    ssm = dict(da_r=g_da[0], da_i=g_da[1], dbb_r=g_dbb[0], dbb_i=g_dbb[1], lr=g3(ssm_a_re), li=g3(ssm_a_im),
               ldt=g3(ssm_log_dt), bt_r=bt_re, bt_i=bt_im, lbr=lbr, lbi=lbi, qr=qr, qi=qi,
               w_a_re=g3(ssm_a_re), m_a_re=g3(m_ssm_a_re), v_a_re=g3(v_ssm_a_re),
               w_a_im=g3(ssm_a_im), m_a_im=g3(m_ssm_a_im), v_a_im=g3(v_ssm_a_im),
               w_log_dt=g3(ssm_log_dt), m_log_dt=g3(m_ssm_log_dt), v_log_dt=g3(v_ssm_log_dt),
               w_bt_re=bt_re, m_bt_re=tr(m_ssm_b_re), v_bt_re=tr(v_ssm_b_re),
               w_bt_im=bt_im, m_bt_im=tr(m_ssm_b_im), v_bt_im=tr(v_ssm_b_im))
    small = _small_update(direct, ssm)
    for name, quad, shape in zip(["norm_pre_g", "conv_b", "ssm_d", "b_glu", "norm_post_g", "conv_w", "ssm_c_re", "ssm_c_im"],
                                 small[:8], [norm_pre_g.shape, conv_b.shape, ssm_d.shape, b_glu.shape,
                                             norm_post_g.shape, conv_w.shape, ssm_c_re.shape, ssm_c_im.shape]):
        res[name] = tuple(a.reshape(shape) for a in quad)
    res["ssm_a_re"] = tuple(a.reshape(ssm_a_re.shape) for a in small[8])
    res["ssm_a_im"] = tuple(a.reshape(ssm_a_im.shape) for a in small[9])
    res["ssm_log_dt"] = tuple(a.reshape(ssm_log_dt.shape) for a in small[10])
    res["ssm_b_re"] = tuple(tr(a) for a in small[11])
    res["ssm_b_im"] = tuple(tr(a) for a in small[12])
    res["w_out"] = tuple(_adam_big("adam_w_out", recv_out, w_out, m_w_out, v_w_out, min(128, r_out))[0])
    res["w_glu"] = tuple(_adam_big("adam_w_glu", recv_glu, w_glu, m_w_glu, v_w_glu, r_glu)[0])

    order = ["norm_pre_g", "w_in", "conv_w", "conv_b", "ssm_a_re", "ssm_a_im", "ssm_log_dt", "ssm_b_re", "ssm_b_im",
             "ssm_c_re", "ssm_c_im", "ssm_d", "w_glu", "b_glu", "w_out", "norm_post_g"]
    loss = loss_sum[0, 0]
    grad_x = _unpermute_rows(gx_p)[None]
    return (loss, grad_x, *[res[n][0] for n in order], *[res[n][1] for n in order],
            *[res[n][2] for n in order], *[res[n][3] for n in order])
```

```python
import functools
import math

import jax
import jax.numpy as jnp
from jax import lax
from jax.experimental import pallas as pl
from jax.experimental.pallas import tpu as pltpu

F32 = jnp.float32
MXU_DTYPE = jnp.bfloat16
PROJ_DTYPE = jnp.bfloat16
AXES = ("x", "y", "c")
N_DEV = 8
N_CHUNK = 8
LANES = 128
SSM_GROUP = 16
SSM_STATE = 64
HALF_CH = 64
HALF_G = HALF_CH // SSM_GROUP
HALF_W = HALF_G * SSM_STATE
EPS = 1e-6
ADAM_LR, ADAM_B1, ADAM_B2, ADAM_EPS, ADAM_WD, ADAM_STEP = 0.001, 0.9, 0.999, 1e-08, 0.01, 10
GELU_C = math.sqrt(2.0 / math.pi)
GELU_K = 0.044715
VMEM_LIMIT = 56 * 1024 * 1024


def _params(sem=None):
    return pltpu.CompilerParams(dimension_semantics=sem, vmem_limit_bytes=VMEM_LIMIT)


def _dot(a, b):
    return jnp.dot(a, b, preferred_element_type=F32)


def _dot_nt(a, b):
    return lax.dot_general(a, b, (((1,), (1,)), ((), ())), preferred_element_type=F32)


def _dot_tn(a, b):
    return lax.dot_general(a, b, (((0,), (0,)), ((), ())), preferred_element_type=F32)


def _sigmoid(z):
    return 1.0 / (1.0 + jnp.exp(-z))


def _flip(v, bit):
    return 1 - v if bit else v


def _peers():
    x, y, c = (lax.axis_index(a) for a in AXES)
    out = []
    for m in range(1, N_DEV):
        px, py, pc = _flip(x, (m >> 2) & 1), _flip(y, (m >> 1) & 1), _flip(c, m & 1)
        out.append((px, py, pc, 4 * px + 2 * py + pc))
    return out


class _Comm:
    def __init__(self, gathers=(), scatters=(), dests=None, into=None):
        self.n_g = len(gathers)
        self.operands = list(gathers) + list(scatters)
        self.n = len(self.operands)
        self.dests = dests or {}
        self.into = into or {}

    def out_shape(self):
        return [jax.ShapeDtypeStruct((N_DEV,) + a.shape if t < self.n_g else a.shape, a.dtype)
                for t, a in enumerate(self.operands)]

    def scratch(self):
        if not self.n:
            return []
        return [pltpu.SemaphoreType.DMA((self.n, N_DEV - 1)), pltpu.SemaphoreType.DMA((self.n, N_DEV - 1)),
                pltpu.SemaphoreType.DMA((self.n,))]

    def _copies(self, in_refs, out_refs, sems, arrivals):
        send_sems, recv_sems, local_sems = sems
        x, y, c = (lax.axis_index(a) for a in AXES)
        me = 4 * x + 2 * y + c

        def src(t, dev):
            return in_refs[t] if t < self.n_g else in_refs[t].at[dev]

        def member(t, dev):
            if t not in self.dests:
                return None
            return functools.reduce(jnp.logical_or, [dev == d for d in self.dests[t]])

        local = [(member(t, me), pltpu.make_async_copy(src(t, me), out_refs[t].at[me], local_sems.at[t]))
                 for t in range(self.n)]
        sends, recvs = [], []
        for t in range(self.n):
            for m, (px, py, pc, peer) in enumerate(_peers()):
                kw = dict(send_sem=send_sems.at[t, m], recv_sem=recv_sems.at[t, m],
                          device_id=(px, py, pc), device_id_type=pl.DeviceIdType.MESH)
                sends.append((member(t, peer), pltpu.make_async_remote_copy(
                    src_ref=src(t, peer), dst_ref=out_refs[t].at[me], **kw)))
                if arrivals:
                    recvs.append((member(t, me), pltpu.make_async_remote_copy(
                        src_ref=src(t, peer), dst_ref=out_refs[t].at[peer], **kw)))
        return local, sends, recvs

    @staticmethod
    def _do(cond, action):
        if cond is None:
            action()
        else:
            pl.when(cond)(action)

    def start(self, in_refs, out_refs, sems):
        local, sends, _ = self._copies(in_refs, out_refs, sems, arrivals=False)
        for cond, cp in local + sends:
            self._do(cond, cp.start)

    def finish(self, in_refs, out_refs, sems):
        local, sends, recvs = self._copies(in_refs, out_refs, sems, arrivals=True)
        for cond, cp in recvs:
            self._do(cond, cp.wait_recv)
        for cond, cp in sends:
            self._do(cond, cp.wait_send)
        for cond, cp in local:
            self._do(cond, cp.wait)


def _call(body, comm, *, name, grid, in_specs, out_specs, out_shape, operands, scratch_shapes=()):
    comm = comm or _Comm()
    n_in, n_out, n_scr, cn = len(in_specs), len(out_specs), len(scratch_shapes), comm.n
    landing = sorted(comm.into)
    aliases = {n_in + cn + q: n_out + t for q, t in enumerate(landing)}

    def wrapped(*refs):
        parts, o = [], 0
        for k in (n_in, cn, len(landing), n_out, cn, n_scr):
            parts.append(refs[o:o + k])
            o += k
        h_in, c_in, _, h_out, c_out, h_scr = parts
        sems = refs[o:]
        if cn:
            first = functools.reduce(jnp.logical_and, [pl.program_id(d) == 0 for d in range(len(grid))])

            @pl.when(first)
            def _():
                comm.start(c_in, c_out, sems)

        body(*h_in, *h_out, *h_scr)
        if cn:
            last = functools.reduce(jnp.logical_and, [pl.program_id(d) == grid[d] - 1 for d in range(len(grid))])

            @pl.when(last)
            def _():
                comm.finish(c_in, c_out, sems)

    any_ = pl.BlockSpec(memory_space=pl.ANY)
    res = pl.pallas_call(
        wrapped, name=name, grid=grid, in_specs=list(in_specs) + [any_] * (cn + len(landing)),
        out_specs=list(out_specs) + [any_] * cn,
        out_shape=list(out_shape) + comm.out_shape(), scratch_shapes=list(scratch_shapes) + comm.scratch(),
        input_output_aliases=aliases, compiler_params=_params(("arbitrary",) * len(grid)),
    )(*operands, *comm.operands, *[comm.into[t] for t in landing])
    return list(res[:n_out]), list(res[n_out:])


def _exchange(name, gathers, scatters):
    def body(tok_ref):
        tok_ref[...] = jnp.zeros_like(tok_ref)

    return _call(body, _Comm(gathers, scatters), name=name, grid=(1,), in_specs=[],
                 out_specs=[pl.BlockSpec((8, LANES), lambda i: (0, 0))],
                 out_shape=[jax.ShapeDtypeStruct((8, LANES), F32)], operands=[])[1]


def _ssm_prep(a_re, a_im, log_dt, bt_re, bt_im):
    def body(lr_ref, li_ref, ldt_ref, br_ref, bi_ref, lbr_ref, lbi_ref, qr_ref, qi_ref, bbr_ref, bbi_ref):
        lr, li = lr_ref[...], li_ref[...]
        dt = jnp.exp(ldt_ref[...])
        mag = jnp.exp(lr * dt)
        lbr, lbi = mag * jnp.cos(li * dt), mag * jnp.sin(li * dt)
        nr, ni = lbr - 1.0, lbi
        den = lr * lr + li * li
        qr = (nr * lr + ni * li) / den
        qi = (ni * lr - nr * li) / den
        br, bi = br_ref[...], bi_ref[...]
        lbr_ref[...], lbi_ref[...], qr_ref[...], qi_ref[...] = lbr, lbi, qr, qi
        bbr_ref[...] = qr * br - qi * bi
        bbi_ref[...] = qr * bi + qi * br

    s2 = jax.ShapeDtypeStruct(a_re.shape, F32)
    s3 = jax.ShapeDtypeStruct(bt_re.shape, F32)
    return pl.pallas_call(body, name="ssm_prep", out_shape=[s2, s2, s2, s2, s3, s3],
                          compiler_params=_params())(a_re, a_im, log_dt, bt_re, bt_im)


def _adam(w, g, m, v):
    m2 = ADAM_B1 * m + (1.0 - ADAM_B1) * g
    v2 = ADAM_B2 * v + (1.0 - ADAM_B2) * (g * g)
    m_hat = m2 / (1.0 - ADAM_B1 ** ADAM_STEP)
    v_hat = v2 / (1.0 - ADAM_B2 ** ADAM_STEP)
    delta = -ADAM_LR * (m_hat / (jnp.sqrt(v_hat) + ADAM_EPS) + ADAM_WD * w)
    return delta, m2, v2


def _small_update(direct, ssm):
    n_direct = len(direct)
    flat = [a for quad in direct for a in quad]
    names = ["da_r", "da_i", "dbb_r", "dbb_i", "lr", "li", "ldt", "bt_r", "bt_i", "lbr", "lbi", "qr", "qi"]
    flat += [ssm[k] for k in names]
    chain = ["a_re", "a_im", "log_dt", "bt_re", "bt_im"]
    for k in chain:
        flat += [ssm["w_" + k], ssm["m_" + k], ssm["v_" + k]]
    n_in = len(flat)

    def body(*refs):
        ins, outs = refs[:n_in], refs[n_in:]
        for p in range(n_direct):
            g, w, m, v = (r[...] for r in ins[4 * p:4 * p + 4])
            d, m2, v2 = _adam(w, g, m, v)
            outs[4 * p][...], outs[4 * p + 1][...], outs[4 * p + 2][...], outs[4 * p + 3][...] = g, d, m2, v2
        o = 4 * n_direct
        da_r, da_i, dbb_r, dbb_i, lr, li, ldt, bt_r, bt_i, lbr, lbi, qr, qi = (r[...] for r in ins[o:o + 13])
        dt = jnp.exp(ldt)
        g_br = qr * dbb_r + qi * dbb_i
        g_bi = qr * dbb_i - qi * dbb_r
        dq_r = jnp.sum(bt_r * dbb_r + bt_i * dbb_i, axis=1, keepdims=True)
        dq_i = jnp.sum(bt_r * dbb_i - bt_i * dbb_r, axis=1, keepdims=True)
        den = lr * lr + li * li
        cr, ci = lr / den, li / den
        gl_r = da_r + (cr * dq_r - ci * dq_i)
        gl_i = da_i + (cr * dq_i + ci * dq_r)
        w_r = qr * cr + qi * ci
        w_i = qi * cr - qr * ci
        g_lr = dt * (lbr * gl_r + lbi * gl_i) + (-w_r * dq_r - w_i * dq_i)
        g_li = dt * (lbr * gl_i - lbi * gl_r) + (-w_r * dq_i + w_i * dq_r)
        m_r = lr * lbr - li * lbi
        m_i = lr * lbi + li * lbr
        g_ldt = jnp.sum(m_r * gl_r + m_i * gl_i, axis=2, keepdims=True) * dt
        grads = [g_lr, g_li, g_ldt, g_br, g_bi]
        base_in, base_out = o + 13, 4 * n_direct
        for p, g in enumerate(grads):
            w, m, v = (r[...] for r in ins[base_in + 3 * p:base_in + 3 * p + 3])
            d, m2, v2 = _adam(w, g, m, v)
            q = base_out + 4 * p
            outs[q][...], outs[q + 1][...], outs[q + 2][...], outs[q + 3][...] = g, d, m2, v2

    out_shape = []
    for quad in direct:
        out_shape += [jax.ShapeDtypeStruct(quad[1].shape, F32)] * 4
    for k in chain:
        out_shape += [jax.ShapeDtypeStruct(ssm["w_" + k].shape, F32)] * 4
    res = pl.pallas_call(body, name="small_update", out_shape=out_shape, compiler_params=_params())(*flat)
    return [tuple(res[4 * p:4 * p + 4]) for p in range(n_direct + len(chain))]


def _sum_slots(name, pack):
    def body(p_ref, o_ref):
        acc = p_ref[0].astype(F32)
        for k in range(1, N_DEV):
            acc = acc + p_ref[k].astype(F32)
        o_ref[...] = acc

    return pl.pallas_call(body, name=name, out_shape=jax.ShapeDtypeStruct(pack.shape[1:], F32),
                          compiler_params=_params())(pack)


def _adam_big(name, recv, w, m, v, tr):
    _, rows, cols = recv.shape

    def body(r_ref, w_ref, m_ref, v_ref, g_ref, d_ref, m2_ref, v2_ref):
        g = r_ref[0].astype(F32)
        for k in range(1, N_DEV):
            g = g + r_ref[k].astype(F32)
        d, m2, v2 = _adam(w_ref[...], g, m_ref[...], v_ref[...])
        g_ref[...], d_ref[...], m2_ref[...], v2_ref[...] = g, d, m2, v2

    blk = pl.BlockSpec((tr, cols), lambda i: (i, 0))
    shp = jax.ShapeDtypeStruct((rows, cols), F32)
    return pl.pallas_call(
        body, name=name, grid=(rows // tr,),
        in_specs=[pl.BlockSpec((N_DEV, tr, cols), lambda i: (0, i, 0)), blk, blk, blk],
        out_specs=[blk] * 4, out_shape=[shp] * 4, compiler_params=_params(("parallel",)),
    )(recv, w, m, v)


def _chunk_block(tm, d):
    return pl.BlockSpec((N_CHUNK, tm // N_CHUNK, d), lambda i: (0, i, 0))


def _interleave(block):
    c, n, d = block.shape
    return pltpu.einshape("cjd->jcd", block).reshape(n * c, d)


def _norm_in(x3, g_pre, tm):
    _, steps, d_model = x3.shape
    seq = steps * N_CHUNK

    def body(x_ref, g_ref, xp_ref, h_ref, ht_ref):
        x = _interleave(x_ref[...])
        xp_ref[...] = x
        r = lax.rsqrt(jnp.mean(x * x, axis=-1, keepdims=True) + EPS)
        h = x * r * g_ref[...]
        h_ref[...] = h.astype(h_ref.dtype)
        ht_ref[...] = h.T.astype(ht_ref.dtype)

    rows = pl.BlockSpec((tm, d_model), lambda i: (i, 0))
    return pl.pallas_call(
        body, name="norm_in", grid=(seq // tm,),
        in_specs=[_chunk_block(tm, d_model), pl.BlockSpec((1, d_model), lambda i: (0, 0))],
        out_specs=[rows, rows, pl.BlockSpec((d_model, tm), lambda i: (0, i))],
        out_shape=[jax.ShapeDtypeStruct((seq, d_model), F32), jax.ShapeDtypeStruct((seq, d_model), MXU_DTYPE),
                   jax.ShapeDtypeStruct((d_model, seq), MXU_DTYPE)],
        compiler_params=_params(("parallel",)),
    )(x3, g_pre)


GATHER_ORDER = (0, 1, 4, 2, 6, 5, 3, 7)
PIECE_COLS = 256


def _fwd_in(h, w_pieces, me, comm, tm):
    seq, d_model = h.shape
    n_p, _, gw = w_pieces.shape
    n_i = seq // tm
    cn = comm.n
    consume = [(rel, r) for rel in (0, 1) for r in range(n_p)]
    consume += [(rel, r) for r in range(n_p) for rel in (4, 2, 5, 3)]
    consume += [(rel, r) for r in range(n_p) for rel in (6, 7)]
    n_q = len(consume)
    order = jnp.stack([jnp.bitwise_xor(me, rel) * n_p + r for rel, r in consume]).astype(jnp.int32)

    def body(order_ref, h_hbm, w_hbm, *rest):
        c_in, rest = rest[:cn], rest[cn:]
        proj_ref, wing = rest[0], rest[1]
        c_out, rest = rest[2:2 + cn], rest[2 + cn:]
        wbuf, send_sems, recv_sems, own_sems, load_sems, h_all, h_sems = rest[:7]
        c_sems = rest[7:]
        k, i = pl.program_id(0), pl.program_id(1)
        h_loads = [pltpu.make_async_copy(h_hbm.at[pl.ds(t * tm, tm)], h_all.at[pl.ds(t * tm, tm)], h_sems.at[t])
                   for t in range(n_i)]
        x, y, c = (lax.axis_index(a) for a in AXES)
        me_ = 4 * x + 2 * y + c

        def dev(rel):
            return _flip(x, (rel >> 2) & 1), _flip(y, (rel >> 1) & 1), _flip(c, rel & 1)

        def slot(rel):
            px, py, pc = dev(rel)
            return 4 * px + 2 * py + pc

        def remote(src, block, r, to_rel, sem):
            return pltpu.make_async_remote_copy(
                src_ref=src, dst_ref=wing.at[block, r], send_sem=send_sems.at[sem, r], recv_sem=recv_sems.at[sem, r],
                device_id=dev(to_rel), device_id_type=pl.DeviceIdType.MESH)

        pieces = range(n_p)
        own = [pltpu.make_async_copy(w_hbm.at[r], wing.at[me_, r], own_sems.at[r]) for r in pieces]
        first_hand = {p: [remote(w_hbm.at[r], me_, r, GATHER_ORDER[p], p - 1) for r in pieces] for p in (1, 2, 3)}
        relay = {2: (c == 0, [remote(wing.at[slot(4), r], slot(4), r, 2, 3) for r in pieces]),
                 3: (c == 1, [remote(wing.at[slot(2), r], slot(2), r, 4, 3) for r in pieces])}
        passed_on = {p: [remote(wing.at[slot(GATHER_ORDER[p]), r], slot(GATHER_ORDER[p]), r, 1, p + 2) for r in pieces]
                     for p in (2, 3, 4)}

        def load(q):
            rel, r = consume[q]
            return pltpu.make_async_copy(wing.at[slot(rel), r], wbuf.at[q % 2], load_sems.at[q % 2])

        def take(q):
            rel, r = consume[q]
            p = GATHER_ORDER.index(rel)
            if p == 0:
                own[r].wait()
            else:
                remote(w_hbm.at[r], slot(rel), r, rel, p - 1).wait_recv()
            if p in relay:
                pl.when(relay[p][0])(relay[p][1][r].start)
            if p in passed_on:
                passed_on[p][r].start()
            load(q).start()

        @pl.when((k == 0) & (i == 0))
        def _():
            for r in pieces:
                own[r].start()
            for r in pieces:
                for p in (1, 2, 3):
                    first_hand[p][r].start()
            for cp in h_loads:
                cp.start()
            comm.start(c_in, c_out, c_sems)
            take(0)

        for t in range(n_i):
            pl.when((k == 0) & (i == t))(h_loads[t].wait)

        for q in range(n_q):
            @pl.when((k == q) & (i == 0))
            def _():
                load(q).wait()

            if q + 1 < n_q:
                @pl.when((k == q) & (i == n_i - 1))
                def _():
                    take(q + 1)

        proj_ref[...] = _dot(h_all[pl.ds(pl.multiple_of(i * tm, tm), tm), :], wbuf[k % 2]).astype(proj_ref.dtype)

        @pl.when((k == n_q - 1) & (i == n_i - 1))
        def _():
            for p in first_hand:
                for cp in first_hand[p]:
                    cp.wait_send()
            for p in passed_on:
                for cp in passed_on[p]:
                    cp.wait_send()
            for cond, cps in relay.values():
                for cp in cps:
                    pl.when(cond)(cp.wait_send)
            comm.finish(c_in, c_out, c_sems)

    any_ = pl.BlockSpec(memory_space=pl.ANY)
    grid_spec = pltpu.PrefetchScalarGridSpec(
        num_scalar_prefetch=1, grid=(n_q, n_i),
        in_specs=[any_, any_] + [any_] * cn,
        out_specs=[pl.BlockSpec((tm, gw), lambda k, i, o: (i, o[k])), any_] + [any_] * cn,
        scratch_shapes=[pltpu.VMEM((2, d_model, gw), w_pieces.dtype), pltpu.SemaphoreType.DMA((N_DEV - 1, n_p)),
                        pltpu.SemaphoreType.DMA((N_DEV - 1, n_p)), pltpu.SemaphoreType.DMA((n_p,)),
                        pltpu.SemaphoreType.DMA((2,)), pltpu.VMEM((seq, d_model), h.dtype),
                        pltpu.SemaphoreType.DMA((n_i,))]
        + comm.scratch())
    res = pl.pallas_call(
        body, name="fwd_in", grid_spec=grid_spec,
        out_shape=[jax.ShapeDtypeStruct((seq, N_DEV * n_p * gw), PROJ_DTYPE),
                   jax.ShapeDtypeStruct((N_DEV, n_p, d_model, gw), w_pieces.dtype)] + comm.out_shape(),
        compiler_params=_params(("arbitrary", "arbitrary")),
    )(order, h, w_pieces, *comm.operands)
    return res[0], res[1], list(res[2:])


def _shift_prev(a):
    n = a.shape[0]
    last = a[n - N_CHUNK:, :]
    row = lax.broadcasted_iota(jnp.int32, last.shape, 0)
    wrap = jnp.where(row == 0, 0.0, pltpu.roll(last, 1, axis=0))
    return jnp.concatenate([wrap, a[:n - N_CHUNK, :]], axis=0)


def _shift_next(a):
    first = a[:N_CHUNK, :]
    row = lax.broadcasted_iota(jnp.int32, first.shape, 0)
    wrap = jnp.where(row == N_CHUNK - 1, 0.0, pltpu.roll(first, N_CHUNK - 1, axis=0))
    return jnp.concatenate([a[N_CHUNK:, :], wrap], axis=0)


def _conv_specs(seq, d_conv):
    nblk = d_conv // LANES
    return [pl.BlockSpec((seq, LANES), functools.partial(lambda i, o: (0, o + i), o=q * nblk)) for q in range(4)]


def _conv_fwd(proj, conv_w8, conv_b, d_conv):
    seq = proj.shape[0]

    def body(bg_ref, cg_ref, v_ref, zc_ref, w_ref, b_ref, y_ref):
        cv = cg_ref[...].astype(F32) * v_ref[...].astype(F32)
        s1 = _shift_prev(cv)
        s2 = _shift_prev(s1)
        conv = b_ref[...] + w_ref[0:1, :] * s2 + w_ref[1:2, :] * s1 + w_ref[2:3, :] * cv
        z = zc_ref[...].astype(F32)
        y_ref[...] = (bg_ref[...].astype(F32) * conv * (z * _sigmoid(z))).astype(y_ref.dtype)

    col = pl.BlockSpec((seq, LANES), lambda i: (0, i))
    return pl.pallas_call(
        body, name="conv_fwd", grid=(d_conv // LANES,),
        in_specs=_conv_specs(seq, d_conv) + [pl.BlockSpec((8, LANES), lambda i: (0, i)), pl.BlockSpec((1, LANES), lambda i: (0, i))],
        out_specs=col, out_shape=jax.ShapeDtypeStruct((seq, d_conv), MXU_DTYPE),
        compiler_params=_params(("parallel",)),
    )(proj, proj, proj, proj, conv_w8, conv_b)


def _conv_bwd(proj, dyc, conv_w8, conv_b, d_conv, comm=None):
    seq = proj.shape[0]

    def body(bg_ref, cg_ref, v_ref, zc_ref, dy_ref, w_ref, b_ref, d4_ref, dcb_ref, dcw_ref):
        bg, cg, v, z = (r[...].astype(F32) for r in (bg_ref, cg_ref, v_ref, zc_ref))
        w0, w1, w2 = w_ref[0:1, :], w_ref[1:2, :], w_ref[2:3, :]
        cv = cg * v
        s1 = _shift_prev(cv)
        s2 = _shift_prev(s1)
        conv = b_ref[...] + w0 * s2 + w1 * s1 + w2 * cv
        sig = _sigmoid(z)
        dy = dy_ref[...].astype(F32)
        g1 = dy * (z * sig)
        d_conv_ = g1 * bg
        d4_ref[0] = (g1 * conv).astype(d4_ref.dtype)
        d4_ref[3] = (dy * bg * conv * (sig * (1.0 + z * (1.0 - sig)))).astype(d4_ref.dtype)
        n1 = _shift_next(d_conv_)
        n2 = _shift_next(n1)
        d_cv = w2 * d_conv_ + w1 * n1 + w0 * n2
        d4_ref[1] = (d_cv * v).astype(d4_ref.dtype)
        d4_ref[2] = (d_cv * cg).astype(d4_ref.dtype)
        dcb_ref[...] = jnp.sum(d_conv_, axis=0, keepdims=True)
        rows = [jnp.sum(d_conv_ * s, axis=0, keepdims=True) for s in (s2, s1, cv)]
        dcw_ref[...] = jnp.concatenate(rows + [jnp.zeros((5, LANES), F32)], axis=0)

    col = pl.BlockSpec((seq, LANES), lambda i: (0, i))
    return _call(
        body, comm, name="conv_bwd", grid=(d_conv // LANES,),
        in_specs=_conv_specs(seq, d_conv) + [col, pl.BlockSpec((8, LANES), lambda i: (0, i)), pl.BlockSpec((1, LANES), lambda i: (0, i))],
        out_specs=[pl.BlockSpec((4, seq, LANES), lambda i: (0, 0, i)), pl.BlockSpec((1, LANES), lambda i: (0, i)),
                   pl.BlockSpec((8, LANES), lambda i: (0, i))],
        out_shape=[jax.ShapeDtypeStruct((4, seq, d_conv), MXU_DTYPE), jax.ShapeDtypeStruct((1, d_conv), F32),
                   jax.ShapeDtypeStruct((8, d_conv), F32)],
        operands=[proj, proj, proj, proj, dyc, conv_w8, conv_b])


def _cmul(ar, ai, br, bi):
    return ar * br - ai * bi, ar * bi + ai * br


def _down(v, k):
    row = lax.broadcasted_iota(jnp.int32, v.shape, 0)
    return jnp.where(row >= k, pltpu.roll(v, k, axis=0), 0.0)


def _up(v, k):
    row = lax.broadcasted_iota(jnp.int32, v.shape, 0)
    return jnp.where(row < N_CHUNK - k, pltpu.roll(v, N_CHUNK - k, axis=0), 0.0)


def _chunk_carry(fr, fi, mr, mi, shift):
    vr, vi = shift(fr, 1), shift(fi, 1)
    for k in (1, 2, 4):
        pr, pi = _cmul(mr, mi, shift(vr, k), shift(vi, k))
        vr, vi = vr + pr, vi + pi
        mr, mi = _cmul(mr, mi, mr, mi)
    return vr, vi


def _tile(ref, j, width, part):
    return ref.at[pl.ds(pl.multiple_of(j * N_CHUNK, N_CHUNK), N_CHUNK), pl.ds(part * width, width)]


def _row(t, k):
    return jnp.broadcast_to(t[k:k + 1, :], t.shape)


def _power_table(tab_ref, ar, ai, steps, width):
    e = lax.broadcasted_iota(jnp.int32, ar.shape, 0) + 1
    rr, ri = jnp.ones_like(ar), jnp.zeros_like(ai)
    br, bi = ar, ai
    for bit in range(4):
        mr, mi = _cmul(rr, ri, br, bi)
        take = ((e >> bit) & 1) == 1
        rr, ri = jnp.where(take, mr, rr), jnp.where(take, mi, ri)
        if bit < 3:
            br, bi = _cmul(br, bi, br, bi)
    _tile(tab_ref, 0, width, 0)[...] = rr
    _tile(tab_ref, 0, width, 1)[...] = ri

    def step(m, carry):
        tr, ti = _cmul(carry[0], carry[1], br, bi)
        _tile(tab_ref, m, width, 0)[...] = tr
        _tile(tab_ref, m, width, 1)[...] = ti
        return tr, ti

    lax.fori_loop(1, steps // N_CHUNK, step, (rr, ri))


def _last_power(tab_ref, steps, width):
    shape = (N_CHUNK, width)
    return (jnp.broadcast_to(tab_ref[steps - 1:steps, 0:width], shape),
            jnp.broadcast_to(tab_ref[steps - 1:steps, width:2 * width], shape))


def _scan_fwd(s_ref, ar, ai, steps, width):
    def step(j, carry):
        sr, si = carry
        nr = ar * sr - ai * si + _tile(s_ref, j, width, 0)[...]
        ni = ar * si + ai * sr + _tile(s_ref, j, width, 1)[...]
        _tile(s_ref, j, width, 0)[...] = nr
        _tile(s_ref, j, width, 1)[...] = ni
        return nr, ni

    z = jnp.zeros((N_CHUNK, width), F32)
    return lax.fori_loop(0, steps, step, (z, z), unroll=4)


def _scan_both(s_ref, g_ref, ar, ai, steps, width):
    def step(q, carry):
        sr, si, gr, gi = carry
        j, jb = q, steps - 1 - q
        nsr = ar * sr - ai * si + _tile(s_ref, j, width, 0)[...]
        nsi = ar * si + ai * sr + _tile(s_ref, j, width, 1)[...]
        ngr = ar * gr + ai * gi + _tile(g_ref, jb, width, 0)[...]
        ngi = ar * gi - ai * gr + _tile(g_ref, jb, width, 1)[...]
        _tile(s_ref, j, width, 0)[...] = nsr
        _tile(s_ref, j, width, 1)[...] = nsi
        _tile(g_ref, jb, width, 0)[...] = ngr
        _tile(g_ref, jb, width, 1)[...] = ngi
        return nsr, nsi, ngr, ngi

    z = jnp.zeros((N_CHUNK, width), F32)
    return lax.fori_loop(0, steps, step, (z, z, z, z), unroll=2)


def _patch_fwd(s_ref, tab_ref, cr, ci, steps, width):
    def tile(m, _):
        tr, ti = _tile(tab_ref, m, width, 0)[...], _tile(tab_ref, m, width, 1)[...]
        for k in range(N_CHUNK):
            fr, fi = _cmul(_row(tr, k), _row(ti, k), cr, ci)
            j = m * N_CHUNK + k
            _tile(s_ref, j, width, 0)[...] += fr
            _tile(s_ref, j, width, 1)[...] += fi
        return 0

    lax.fori_loop(0, steps // N_CHUNK, tile, 0)


def _lam_rows(lam_ref, hh, width):
    return (jnp.broadcast_to(lam_ref[hh, 0:1, :], (N_CHUNK, width)),
            jnp.broadcast_to(lam_ref[hh, 1:2, :], (N_CHUNK, width)))


def _ssm_specs(seq, col0):
    return dict(
        col=pl.BlockSpec((seq, LANES), lambda i: (0, col0 + i)),
        lam=pl.BlockSpec((2, 2, HALF_W), lambda i: (i, 0, 0)),
        bb=pl.BlockSpec((2, HALF_CH, 2 * HALF_W), lambda i: (i, 0, 0)),
        cc=pl.BlockSpec((2, 2 * HALF_W, HALF_CH), lambda i: (i, 0, 0)),
        vec=pl.BlockSpec((1, LANES), lambda i: (0, i)),
        out=pl.BlockSpec((seq, LANES), lambda i: (0, i)),
    )


def _ssm_fwd(proj, lam, bbcat, cccat, d_skip, d_ssm, u_col0, comm=None):
    seq = proj.shape[0]
    steps = seq // N_CHUNK

    def body(u_ref, lam_ref, bb_ref, cc_ref, d_ref, yp_ref, s_ref, tab_ref):
        for hh in range(2):
            lanes = slice(HALF_CH * hh, HALF_CH * (hh + 1))
            u_half = u_ref[:, lanes].astype(F32)
            ar, ai = _lam_rows(lam_ref, hh, HALF_W)
            _power_table(tab_ref, ar, ai, steps, HALF_W)
            s_ref[...] = _dot(u_half.astype(MXU_DTYPE), bb_ref[hh])
            fr, fi = _scan_fwd(s_ref, ar, ai, steps, HALF_W)
            pr, pi = _last_power(tab_ref, steps, HALF_W)
            cr, ci = _chunk_carry(fr, fi, pr, pi, _down)
            _patch_fwd(s_ref, tab_ref, cr, ci, steps, HALF_W)
            y = _dot(s_ref[...].astype(MXU_DTYPE), cc_ref[hh])
            yp_ref[:, lanes] = y + d_ref[:, lanes] * u_half

    sp = _ssm_specs(seq, u_col0 // LANES)
    return _call(
        body, comm, name="ssm_fwd", grid=(d_ssm // LANES,),
        in_specs=[sp["col"], sp["lam"], sp["bb"], sp["cc"], sp["vec"]], out_specs=[sp["out"]],
        out_shape=[jax.ShapeDtypeStruct((seq, d_ssm), F32)],
        scratch_shapes=[pltpu.VMEM((seq, 2 * HALF_W), F32), pltpu.VMEM((steps, 2 * HALF_W), F32)],
        operands=[proj, lam, bbcat, cccat, d_skip])


def _ssm_bwd(proj, dyp, lam, bbcat, cccat, d_skip, d_ssm, u_col0, comm=None):
    seq = proj.shape[0]
    steps = seq // N_CHUNK
    n_half = 2 * d_ssm // LANES
    width = HALF_W

    def body(u_ref, dyp_ref, lam_ref, bb_ref, cc_ref, d_ref, du_ref, dbb_ref, dcc_ref, da_ref, dd_ref,
             s_ref, g_ref, tab_ref):
        n_tiles = steps // N_CHUNK
        for hh in range(2):
            lanes = slice(HALF_CH * hh, HALF_CH * (hh + 1))
            u_half, dy_half = u_ref[:, lanes].astype(F32), dyp_ref[:, lanes].astype(F32)
            dy_mx = dy_half.astype(MXU_DTYPE)
            ar, ai = _lam_rows(lam_ref, hh, width)
            _power_table(tab_ref, ar, ai, steps, width)
            s_ref[...] = _dot(u_half.astype(MXU_DTYPE), bb_ref[hh])
            g_ref[...] = _dot_nt(dy_mx, cc_ref[hh])
            fr, fi, lr_, li_ = _scan_both(s_ref, g_ref, ar, ai, steps, width)
            pr, pi = _last_power(tab_ref, steps, width)
            cr, ci = _chunk_carry(fr, fi, pr, pi, _down)
            gr, gi = _chunk_carry(lr_, li_, pr, -pi, _up)

            def tile(m, carry):
                sr, si, accr, acci = carry
                t1r, t1i = _tile(tab_ref, m, width, 0)[...], _tile(tab_ref, m, width, 1)[...]
                mb = n_tiles - 1 - m
                t2r, t2i = _tile(tab_ref, mb, width, 0)[...], _tile(tab_ref, mb, width, 1)[...]
                for k in range(N_CHUNK):
                    j = m * N_CHUNK + k
                    xr, xi = _cmul(_row(t1r, k), _row(t1i, k), cr, ci)
                    nsr = _tile(s_ref, j, width, 0)[...] + xr
                    nsi = _tile(s_ref, j, width, 1)[...] + xi
                    _tile(s_ref, j, width, 0)[...] = nsr
                    _tile(s_ref, j, width, 1)[...] = nsi
                    qr, qi = _row(t2r, N_CHUNK - 1 - k), _row(t2i, N_CHUNK - 1 - k)
                    ngr = _tile(g_ref, j, width, 0)[...] + (qr * gr + qi * gi)
                    ngi = _tile(g_ref, j, width, 1)[...] + (qr * gi - qi * gr)
                    _tile(g_ref, j, width, 0)[...] = ngr
                    _tile(g_ref, j, width, 1)[...] = ngi
                    accr = accr + (sr * ngr + si * ngi)
                    acci = acci + (sr * ngi - si * ngr)
                    sr, si = nsr, nsi
                return sr, si, accr, acci

            z = jnp.zeros((N_CHUNK, width), F32)
            _, _, accr, acci = lax.fori_loop(0, n_tiles, tile, (cr, ci, z, z))
            da_ref[hh, :, 0:width] = jnp.sum(accr, axis=0, keepdims=True)
            da_ref[hh, :, width:2 * width] = jnp.sum(acci, axis=0, keepdims=True)

            g_mx = g_ref[...].astype(MXU_DTYPE)
            dcc_ref[hh] = _dot_tn(dy_mx, s_ref[...].astype(MXU_DTYPE)).T
            dbb_ref[hh] = _dot_tn(u_half.astype(MXU_DTYPE), g_mx)
            du = _dot_nt(g_mx, bb_ref[hh]) + d_ref[:, lanes] * dy_half
            du_ref[:, lanes] = du.astype(du_ref.dtype)
            dd_ref[:, lanes] = jnp.sum(dy_half * u_half, axis=0, keepdims=True)

    sp = _ssm_specs(seq, u_col0 // LANES)
    return _call(
        body, comm, name="ssm_bwd", grid=(d_ssm // LANES,),
        in_specs=[sp["col"], sp["out"], sp["lam"], sp["bb"], sp["cc"], sp["vec"]],
        out_specs=[sp["out"], sp["bb"], sp["cc"], pl.BlockSpec((2, 1, 2 * width), lambda i: (i, 0, 0)), sp["vec"]],
        out_shape=[jax.ShapeDtypeStruct((seq, d_ssm), MXU_DTYPE),
                   jax.ShapeDtypeStruct((n_half, HALF_CH, 2 * width), F32),
                   jax.ShapeDtypeStruct((n_half, 2 * width, HALF_CH), F32),
                   jax.ShapeDtypeStruct((n_half, 1, 2 * width), F32),
                   jax.ShapeDtypeStruct((1, d_ssm), F32)],
        scratch_shapes=[pltpu.VMEM((seq, 2 * width), F32), pltpu.VMEM((seq, 2 * width), F32),
                        pltpu.VMEM((steps, 2 * width), F32)],
        operands=[proj, dyp, lam, bbcat, cccat, d_skip])


def _tail(xp, t3, proj, yconv, yp, w_glu, b_glu, w_out, g_post, zs_col0, tm):
    seq, d_model = xp.shape
    d_conv, d_ssm = yconv.shape[1], yp.shape[1]
    d_mix = d_conv + d_ssm
    assert zs_col0 % d_ssm == 0

    def body(x_ref, t_ref, zs_ref, yc_ref, yp_ref, wglu_hbm, bglu_ref, wout_hbm, gpost_ref,
             dy_ref, do_ref, mixt_ref, dyc_ref, dyp_ref, dzs_ref, ygt_ref, dq_ref, loss_ref, dgpost_ref, dbglu_ref,
             wglu, wout):
        @pl.when(pl.program_id(0) == 0)
        def _():
            pltpu.sync_copy(wglu_hbm, wglu)
            pltpu.sync_copy(wout_hbm, wout)
            loss_ref[...] = jnp.zeros_like(loss_ref)
            dgpost_ref[...] = jnp.zeros_like(dgpost_ref)
            dbglu_ref[...] = jnp.zeros_like(dbglu_ref)

        a = yp_ref[...]
        th = jnp.tanh(GELU_C * (a + GELU_K * (a * a * a)))
        yg = a * (0.5 * (1.0 + th))
        dgelu = 0.5 * (1.0 + th) + 0.5 * a * (1.0 - th * th) * (GELU_C * (1.0 + 3.0 * GELU_K * a * a))
        yg_mx = yg.astype(MXU_DTYPE)
        sq = _sigmoid(_dot(yg_mx, wglu[...]) + bglu_ref[...])
        y2 = yg * sq
        zs = zs_ref[...].astype(F32)
        sz = _sigmoid(zs)
        silz = zs * sz
        yc, ys = yc_ref[...].astype(F32), y2 * silz
        mix = jnp.concatenate([yc, ys], axis=1).astype(MXU_DTYPE)
        mixt_ref[0:d_conv, :] = yc.T.astype(MXU_DTYPE)
        mixt_ref[d_conv:, :] = ys.T.astype(MXU_DTYPE)
        o = _dot(mix, wout[...])
        r2 = lax.rsqrt(jnp.mean(o * o, axis=-1, keepdims=True) + EPS)
        on = o * r2
        gpost = gpost_ref[...]
        err = (x_ref[...] + on * gpost) - _interleave(t_ref[...])
        loss_ref[...] += 0.5 * jnp.sum(jnp.mean(err * err, axis=-1, keepdims=True), axis=0, keepdims=True)
        dy = err * (1.0 / d_model)
        dy_ref[...] = dy
        dgpost_ref[...] += jnp.sum(dy * on, axis=0, keepdims=True)
        d_on = dy * gpost
        d_o = r2 * (d_on - on * jnp.mean(d_on * on, axis=-1, keepdims=True))
        do_mx = d_o.astype(MXU_DTYPE)
        do_ref[...] = do_mx
        d_mix_ = _dot_nt(do_mx, wout[...])
        dyc_ref[...] = d_mix_[:, :d_conv].astype(dyc_ref.dtype)
        d_yssm = d_mix_[:, d_conv:]
        d_y2 = d_yssm * silz
        dzs_ref[...] = (d_yssm * y2 * (sz * (1.0 + zs * (1.0 - sz)))).astype(dzs_ref.dtype)
        d_q = d_y2 * yg * (sq * (1.0 - sq))
        dq_mx = d_q.astype(MXU_DTYPE)
        dq_ref[...] = dq_mx
        ygt_ref[...] = yg.T.astype(MXU_DTYPE)
        dbglu_ref[...] += jnp.sum(d_q, axis=0, keepdims=True)
        d_yg = d_y2 * sq + _dot_nt(dq_mx, wglu[...])
        dyp_ref[...] = (d_yg * dgelu).astype(dyp_ref.dtype)

    def rows(width, col=0):
        return pl.BlockSpec((tm, width), lambda i: (i, col))

    def fixed(width):
        return pl.BlockSpec((1, width), lambda i: (0, 0))

    def cols(height):
        return pl.BlockSpec((height, tm), lambda i: (0, i))

    any_ = pl.BlockSpec(memory_space=pl.ANY)
    return pl.pallas_call(
        body, name="tail", grid=(seq // tm,),
        in_specs=[rows(d_model), _chunk_block(tm, d_model), rows(d_ssm, zs_col0 // d_ssm), rows(d_conv), rows(d_ssm),
                  any_, fixed(d_ssm), any_, fixed(d_model)],
        out_specs=[rows(d_model), rows(d_model), cols(d_mix), rows(d_conv), rows(d_ssm), rows(d_ssm), cols(d_ssm),
                   rows(d_ssm), fixed(LANES), fixed(d_model), fixed(d_ssm)],
        out_shape=[jax.ShapeDtypeStruct((seq, d_model), F32), jax.ShapeDtypeStruct((seq, d_model), MXU_DTYPE),
                   jax.ShapeDtypeStruct((d_mix, seq), MXU_DTYPE), jax.ShapeDtypeStruct((seq, d_conv), MXU_DTYPE),
                   jax.ShapeDtypeStruct((seq, d_ssm), MXU_DTYPE), jax.ShapeDtypeStruct((seq, d_ssm), MXU_DTYPE),
                   jax.ShapeDtypeStruct((d_ssm, seq), MXU_DTYPE), jax.ShapeDtypeStruct((seq, d_ssm), MXU_DTYPE),
                   jax.ShapeDtypeStruct((1, LANES), F32), jax.ShapeDtypeStruct((1, d_model), F32),
                   jax.ShapeDtypeStruct((1, d_ssm), F32)],
        scratch_shapes=[pltpu.VMEM(w_glu.shape, MXU_DTYPE), pltpu.VMEM(w_out.shape, MXU_DTYPE)],
        compiler_params=_params(("arbitrary",)),
    )(xp, t3, proj, yconv, yp, w_glu, b_glu, w_out, g_post)


def _bwd_in(d4, du, dzs, gr, win_g, xp, dy, g_pre, comm, tm):
    seq, d_model = xp.shape
    nb, n_p, _, gw = win_g.shape
    nc = n_p * gw
    per = d4.shape[2] // gr

    def body(d4_ref, du_ref, dzs_ref, w_hbm, x_ref, dy_ref, g_ref, gx_ref, dg_ref, w_all, w_sems):
        def granule(g):
            p, cols = g // per, slice(g % per * gr, (g % per + 1) * gr)
            if p < 4:
                return d4_ref[p, :, cols]
            return du_ref[:, cols] if p == 4 else dzs_ref[:, cols]

        i = pl.program_id(0)
        loads = [[pltpu.make_async_copy(w_hbm.at[k, r], w_all.at[k, :, pl.ds(r * gw, gw)], w_sems.at[k, r])
                  for r in range(n_p)] for k in range(nb)]

        @pl.when(i == 0)
        def _():
            dg_ref[...] = jnp.zeros_like(dg_ref)
            for row in loads:
                for cp in row:
                    cp.start()

        dh = None
        for k in range(nb):
            @pl.when(i == 0)
            def _():
                for cp in loads[k]:
                    cp.wait()

            dp = jnp.concatenate([granule(g) for g in range(k * nc // gr, (k + 1) * nc // gr)], axis=1)
            part = _dot_nt(dp, w_all[k])
            dh = part if dh is None else dh + part

        x = x_ref[...]
        r = lax.rsqrt(jnp.mean(x * x, axis=-1, keepdims=True) + EPS)
        xn = x * r
        dg_ref[...] += jnp.sum(dh * xn, axis=0, keepdims=True)
        dxn = dh * g_ref[...]
        gx_ref[...] = r * (dxn - xn * jnp.mean(dxn * xn, axis=-1, keepdims=True)) + dy_ref[...]

    row = pl.BlockSpec((tm, d_model), lambda i: (i, 0))
    vec = pl.BlockSpec((1, d_model), lambda i: (0, 0))
    return _call(
        body, comm, name="bwd_in", grid=(seq // tm,),
        in_specs=[pl.BlockSpec((4, tm, d4.shape[2]), lambda i: (0, i, 0)),
                  pl.BlockSpec((tm, du.shape[1]), lambda i: (i, 0)), pl.BlockSpec((tm, dzs.shape[1]), lambda i: (i, 0)),
                  pl.BlockSpec(memory_space=pl.ANY), row, row, vec],
        out_specs=[row, vec],
        out_shape=[jax.ShapeDtypeStruct((seq, d_model), F32), jax.ShapeDtypeStruct((1, d_model), F32)],
        scratch_shapes=[pltpu.VMEM((nb, d_model, nc), win_g.dtype), pltpu.SemaphoreType.DMA((nb, n_p))],
        operands=[d4, du, dzs, win_g, xp, dy, g_pre])


def _lookup(g, table):
    out = jnp.int32(table[0])
    for gi in range(1, len(table)):
        if table[gi] != table[gi - 1]:
            out = jnp.where(g >= gi, jnp.int32(table[gi]), out)
    return out


def _held(values, used):
    cur = next(v for v, u in zip(values, used) if u)
    out = []
    for v, u in zip(values, used):
        cur = v if u else cur
        out.append(cur)
    return out


def _dw_in(name, ht, d4, du, dzs, granules, gr, nc, tm, comm=None):
    d_model, seq = ht.shape
    per = d4.shape[2] // gr
    piece, col = [g // per for g in granules], [g % per for g in granules]
    sources = [(d4, [p < 4 for p in piece]), (du, [p == 4 for p in piece]), (dzs, [p == 5 for p in piece])]
    sources = [(a, used) for a, used in sources if any(used)]
    select = [next(s for s, (_, used) in enumerate(sources) if used[q]) for q in range(len(granules))]
    owner, place = [g * gr // nc for g in granules], [g * gr % nc // gr for g in granules]

    def body(a_ref, *refs):
        src_refs, o_ref = refs[:-1], refs[-1]
        j = pl.program_id(1)
        for s, ref in enumerate(src_refs):
            @pl.when(_lookup(j, select) == s)
            def _():
                o_ref[...] = _dot(a_ref[...], ref[...]).astype(o_ref.dtype)

    in_specs = [pl.BlockSpec((tm, seq), lambda i, j: (i, 0))]
    for a, used in sources:
        cols = _held(col, used)
        if a.ndim == 3:
            rows = _held(piece, used)
            in_specs.append(pl.BlockSpec((None, seq, gr), functools.partial(
                lambda i, j, rows, cols: (_lookup(j, rows), 0, _lookup(j, cols)), rows=rows, cols=cols)))
        else:
            in_specs.append(pl.BlockSpec((seq, gr), functools.partial(
                lambda i, j, cols: (0, _lookup(j, cols)), cols=cols)))
    return _call(
        body, comm, name=name, grid=(d_model // tm, len(granules)), in_specs=in_specs,
        out_specs=[pl.BlockSpec((None, tm, gr), lambda i, j: (_lookup(j, owner), i, _lookup(j, place)))],
        out_shape=[jax.ShapeDtypeStruct((N_DEV, d_model, nc), MXU_DTYPE)],
        operands=[ht] + [a for a, _ in sources])


def _wgrad(name, at, b, tm, tn, out_shape, out_block, out_index, comm=None):
    m, seq = at.shape
    n = b.shape[1]

    def body(a_ref, b_ref, o_ref):
        o_ref[...] = _dot(a_ref[...], b_ref[...]).astype(o_ref.dtype)

    return _call(
        body, comm, name=name, grid=(n // tn, m // tm),
        in_specs=[pl.BlockSpec((tm, seq), lambda j, i: (i, 0)), pl.BlockSpec((seq, tn), lambda j, i: (0, j))],
        out_specs=[pl.BlockSpec(out_block, lambda j, i: out_index(i, j))],
        out_shape=[jax.ShapeDtypeStruct(out_shape, MXU_DTYPE)],
        operands=[at, b])


def _eye_g():
    return jnp.eye(HALF_G, dtype=F32)


def _bb_blockdiag(bbt_r, bbt_i):
    n_half = bbt_r.shape[0] // HALF_G

    def one(t):
        t = t.reshape(n_half, HALF_G, SSM_GROUP, SSM_STATE)
        t = t[:, :, :, None, :] * _eye_g()[None, :, None, :, None]
        return t.reshape(n_half, HALF_CH, HALF_W)

    return jnp.concatenate([one(bbt_r), one(bbt_i)], axis=-1)


def _cc_blockdiag(c_re, c_im):
    n_half = c_re.shape[0] // HALF_G

    def one(t):
        t = t.reshape(n_half, HALF_G, SSM_GROUP, SSM_STATE)
        t = jnp.transpose(t, (0, 3, 1, 2))
        t = t[:, None, :, :, :] * _eye_g()[None, :, None, :, None]
        return t.reshape(n_half, HALF_W, HALF_CH)

    return jnp.concatenate([one(c_re), one(-c_im)], axis=1)


def _bb_diag(dbb):
    n_half = dbb.shape[0]
    t = dbb.reshape(n_half, HALF_G, SSM_GROUP, 2, HALF_G, SSM_STATE)
    t = jnp.sum(t * _eye_g()[None, :, None, None, :, None], axis=4)
    t = jnp.transpose(t, (3, 0, 1, 2, 4))
    return t.reshape(2, n_half * HALF_G, SSM_GROUP, SSM_STATE)


def _cc_diag(dcc):
    n_half = dcc.shape[0]
    t = dcc.reshape(n_half, 2, HALF_G, SSM_STATE, HALF_G, SSM_GROUP)
    t = jnp.sum(t * _eye_g()[None, None, :, None, :, None], axis=2)
    t = jnp.transpose(t, (1, 0, 3, 4, 2))
    return t.reshape(2, n_half * HALF_G, SSM_GROUP, SSM_STATE)


def _unpermute_rows(a):
    seq, d = a.shape
    return a.reshape(seq // N_CHUNK, N_CHUNK, d).transpose(1, 0, 2).reshape(seq, d)


def _pack_rows(shape):
    return -(-math.prod(shape) // (8 * LANES)) * 8


def _pack(parts, dtype=F32):
    rows = []
    for p in parts:
        flat = p.reshape(-1).astype(dtype)
        rows.append(jnp.pad(flat, (0, _pack_rows(p.shape) * LANES - flat.shape[0])).reshape(-1, LANES))
    return jnp.concatenate(rows, axis=0)


def _unpack(packed, shapes):
    out, o = [], 0
    for s in shapes:
        n = _pack_rows(s)
        out.append(packed[o:o + n].reshape(-1)[:math.prod(s)].reshape(s))
        o += n
    return out


def kernel(x, norm_pre_g, w_in, conv_w, conv_b, ssm_a_re, ssm_a_im, ssm_log_dt, ssm_b_re, ssm_b_im, ssm_c_re, ssm_c_im, ssm_d, w_glu, b_glu, w_out, norm_post_g, loss_target, m_norm_pre_g, m_w_in, m_conv_w, m_conv_b, m_ssm_a_re, m_ssm_a_im, m_ssm_log_dt, m_ssm_b_re, m_ssm_b_im, m_ssm_c_re, m_ssm_c_im, m_ssm_d, m_w_glu, m_b_glu, m_w_out, m_norm_post_g, v_norm_pre_g, v_w_in, v_conv_w, v_conv_b, v_ssm_a_re, v_ssm_a_im, v_ssm_log_dt, v_ssm_b_re, v_ssm_b_im, v_ssm_c_re, v_ssm_c_im, v_ssm_d, v_w_glu, v_b_glu, v_w_out, v_norm_post_g):
    seq, d_model = x.shape[1], x.shape[2]
    d_conv, d_ssm = conv_b.shape[0], ssm_d.shape[0]
    groups, states = ssm_a_re.shape
    assert x.shape[0] == 1 and seq % (8 * N_CHUNK) == 0 and d_conv == d_ssm
    assert (groups, states) == (d_ssm // SSM_GROUP, SSM_STATE) and d_ssm % LANES == 0
    me = 4 * lax.axis_index("x") + 2 * lax.axis_index("y") + lax.axis_index("c")
    tm = min(512, seq)

    x3 = x[0].reshape(N_CHUNK, seq // N_CHUNK, d_model)
    t3 = loss_target[0].reshape(N_CHUNK, seq // N_CHUNK, d_model)
    row = lambda a: a.reshape(1, -1)
    conv_w8 = jnp.pad(conv_w, ((0, 8 - conv_w.shape[0]), (0, 0)))

    g3 = lambda a: a.reshape(groups, 1, -1)
    bt_re, bt_im = jnp.transpose(ssm_b_re, (0, 2, 1)), jnp.transpose(ssm_b_im, (0, 2, 1))
    lbr, lbi, qr, qi, bbt_r, bbt_i = _ssm_prep(g3(ssm_a_re), g3(ssm_a_im), g3(ssm_log_dt), bt_re, bt_im)
    n_half = groups // HALF_G
    lam = jnp.stack([lbr.reshape(n_half, HALF_W), lbi.reshape(n_half, HALF_W)], axis=1)
    bbcat = _bb_blockdiag(bbt_r, bbt_i).astype(MXU_DTYPE)
    cccat = _cc_blockdiag(ssm_c_re, ssm_c_im).astype(MXU_DTYPE)

    xp, h, ht = _norm_in(x3, row(norm_pre_g), tm)
    nc = w_in.shape[1]
    w_pieces = jnp.transpose(w_in.astype(MXU_DTYPE).reshape(d_model, nc // PIECE_COLS, PIECE_COLS), (1, 0, 2))
    proj, win_g, (convw_g, wglu_g) = _fwd_in(h, w_pieces, me, _Comm([conv_w8, w_glu.astype(MXU_DTYPE)]),
                                             min(2048, seq))
    conv_w_full = jnp.transpose(convw_g, (1, 0, 2)).reshape(8, d_conv)
    u_col0, zs_col0 = 4 * d_conv, 4 * d_conv + d_ssm
    yconv = _conv_fwd(proj, conv_w_full, row(conv_b), d_conv)
    (yp,), (wout_g,) = _ssm_fwd(proj, lam, bbcat, cccat, row(ssm_d), d_ssm, u_col0,
                                _Comm([w_out.astype(MXU_DTYPE)]))
    w_out_full = wout_g.reshape(-1, d_model)
    w_glu_full = wglu_g.reshape(-1, d_ssm)
    (dy, d_o, mixt, dyc, dyp, dzs, ygt, dq, loss_part, dgpost, dbglu) = _tail(
        xp, t3, proj, yconv, yp, w_glu_full, row(b_glu), w_out_full, row(norm_post_g), zs_col0, min(256, seq))

    r_out, r_glu, nc = w_out.shape[0], w_glu.shape[0], w_in.shape[1]
    (dwout_p,), _ = _wgrad("dw_out", mixt, d_o, r_out, min(1024, d_model), (N_DEV, r_out, d_model),
                           (None, r_out, min(1024, d_model)), lambda i, j: (i, 0, j))
    (dwglu_p,), _ = _wgrad("dw_glu", ygt, dq, r_glu, d_ssm, (N_DEV, r_glu, d_ssm),
                           (None, r_glu, d_ssm), lambda i, j: (i, 0, 0))
    (d4, dconvb, dconvw), (recv_glu,) = _conv_bwd(proj, dyc, conv_w_full, row(conv_b), d_conv,
                                                  _Comm([], [dwglu_p]))
    late = [k for k in range(N_DEV) if k * nc < u_col0 + d_ssm and (k + 1) * nc > u_col0]
    early = [k for k in range(N_DEV) if k not in late]
    gr = math.gcd(nc, d_conv)
    granules = lambda blocks: [g for k in blocks for g in range(k * nc // gr, (k + 1) * nc // gr)]
    tmw = min(1024, d_model)
    (dwin_e,), (recv_out,) = _dw_in("dw_in_early", ht, d4, None, dzs, granules(early), gr, nc, tmw,
                                    _Comm([], [dwout_p]))
    (du, dbb, dcc, da, dd), (recv_in,) = _ssm_bwd(
        proj, dyp, lam, bbcat, cccat, row(ssm_d), d_ssm, u_col0, _Comm([], [dwin_e], dests={0: early}))
    parts_mx = [_bb_diag(dbb), _cc_diag(dcc)]
    (dwin_l,), (pack_mx_g,) = _dw_in("dw_in_late", ht, d4, du, dzs, granules(late), gr, nc, tmw,
                                     _Comm([_pack(parts_mx, MXU_DTYPE)]))
    da_n = jnp.transpose(da.reshape(n_half, 2, HALF_G, SSM_STATE), (1, 0, 2, 3)).reshape(2, groups, 1, states)
    parts = [dgpost, dconvb, dd, dbglu, dconvw[:3], da_n, loss_part]
    shapes, shapes_mx = [p.shape for p in parts], [p.shape for p in parts_mx]
    (gx_p, dgpre), (pack_g, recv_in) = _bwd_in(
        d4, du, dzs, gr, win_g, xp, dy, row(norm_pre_g),
        _Comm([_pack(parts)], [dwin_l], dests={1: late}, into={1: recv_in}), min(256, seq))
    (last_g,) = _exchange("reduce_last", [_pack([dgpre])], [])
    (g_gpost, g_convb, g_d, g_bglu, g_convw, g_da, loss_sum) = _unpack(_sum_slots("sum_pack", pack_g), shapes)
    (g_dbb, g_dcc) = _unpack(_sum_slots("sum_pack_mx", pack_mx_g), shapes_mx)
    (g_gpre,) = _unpack(_sum_slots("sum_last", last_g), [dgpre.shape])
    g_convw = lax.dynamic_slice(g_convw, (0, me * conv_w.shape[1]), conv_w.shape)

    tr = lambda a: jnp.transpose(a, (0, 2, 1))
    direct = [(g_gpre, row(norm_pre_g), row(m_norm_pre_g), row(v_norm_pre_g)),
              (g_convb, row(conv_b), row(m_conv_b), row(v_conv_b)),
              (g_d, row(ssm_d), row(m_ssm_d), row(v_ssm_d)),
              (g_bglu, row(b_glu), row(m_b_glu), row(v_b_glu)),
              (g_gpost, row(norm_post_g), row(m_norm_post_g), row(v_norm_post_g)),
              (g_convw, conv_w, m_conv_w, v_conv_w),
              (g_dcc[0], ssm_c_re, m_ssm_c_re, v_ssm_c_re),
              (-g_dcc[1], ssm_c_im, m_ssm_c_im, v_ssm_c_im)]
    ssm = dict(da_r=g_da[0], da_i=g_da[1], dbb_r=g_dbb[0], dbb_i=g_dbb[1], lr=g3(ssm_a_re), li=g3(ssm_a_im),
               ldt=g3(ssm_log_dt), bt_r=bt_re, bt_i=bt_im, lbr=lbr, lbi=lbi, qr=qr, qi=qi,
               w_a_re=g3(ssm_a_re), m_a_re=g3(m_ssm_a_re), v_a_re=g3(v_ssm_a_re),
               w_a_im=g3(ssm_a_im), m_a_im=g3(m_ssm_a_im), v_a_im=g3(v_ssm_a_im),
               w_log_dt=g3(ssm_log_dt), m_log_dt=g3(m_ssm_log_dt), v_log_dt=g3(v_ssm_log_dt),
               w_bt_re=bt_re, m_bt_re=tr(m_ssm_b_re), v_bt_re=tr(v_ssm_b_re),
               w_bt_im=bt_im, m_bt_im=tr(m_ssm_b_im), v_bt_im=tr(v_ssm_b_im))
    small = _small_update(direct, ssm)
    res = {}
    for name, quad, shape in zip(["norm_pre_g", "conv_b", "ssm_d", "b_glu", "norm_post_g", "conv_w", "ssm_c_re", "ssm_c_im"],
                                 small[:8], [norm_pre_g.shape, conv_b.shape, ssm_d.shape, b_glu.shape,
                                             norm_post_g.shape, conv_w.shape, ssm_c_re.shape, ssm_c_im.shape]):
        res[name] = tuple(a.reshape(shape) for a in quad)
    res["ssm_a_re"] = tuple(a.reshape(ssm_a_re.shape) for a in small[8])
    res["ssm_a_im"] = tuple(a.reshape(ssm_a_im.shape) for a in small[9])
    res["ssm_log_dt"] = tuple(a.reshape(ssm_log_dt.shape) for a in small[10])
    res["ssm_b_re"] = tuple(tr(a) for a in small[11])
    res["ssm_b_im"] = tuple(tr(a) for a in small[12])
    res["w_in"] = tuple(_adam_big("adam_w_in", recv_in, w_in, m_w_in, v_w_in, min(256, d_model)))
    res["w_out"] = tuple(_adam_big("adam_w_out", recv_out, w_out, m_w_out, v_w_out, min(128, r_out)))
    res["w_glu"] = tuple(_adam_big("adam_w_glu", recv_glu, w_glu, m_w_glu, v_w_glu, r_glu))

    order = ["norm_pre_g", "w_in", "conv_w", "conv_b", "ssm_a_re", "ssm_a_im", "ssm_log_dt", "ssm_b_re", "ssm_b_im",
             "ssm_c_re", "ssm_c_im", "ssm_d", "w_glu", "b_glu", "w_out", "norm_post_g"]
    loss = loss_sum[0, 0]
    grad_x = _unpermute_rows(gx_p)[None]
    return (loss, grad_x, *[res[n][0] for n in order], *[res[n][1] for n in order],
            *[res[n][2] for n in order], *[res[n][3] for n in order])
```

```python
import functools
import math

import jax
import jax.numpy as jnp
from jax import lax
from jax.experimental import pallas as pl
from jax.experimental.pallas import tpu as pltpu

F32 = jnp.float32
MXU_DTYPE = jnp.bfloat16
PROJ_DTYPE = jnp.bfloat16
AXES = ("x", "y", "c")
N_DEV = 8
N_CHUNK = 8
LANES = 128
SSM_GROUP = 16
SSM_STATE = 64
HALF_CH = 64
HALF_G = HALF_CH // SSM_GROUP
HALF_W = HALF_G * SSM_STATE
EPS = 1e-6
ADAM_LR, ADAM_B1, ADAM_B2, ADAM_EPS, ADAM_WD, ADAM_STEP = 0.001, 0.9, 0.999, 1e-08, 0.01, 10
GELU_C = math.sqrt(2.0 / math.pi)
GELU_K = 0.044715
VMEM_LIMIT = 56 * 1024 * 1024


def _params(sem=None):
    return pltpu.CompilerParams(dimension_semantics=sem, vmem_limit_bytes=VMEM_LIMIT)


def _dot(a, b):
    return jnp.dot(a, b, preferred_element_type=F32)


def _dot_nt(a, b):
    return lax.dot_general(a, b, (((1,), (1,)), ((), ())), preferred_element_type=F32)


def _dot_tn(a, b):
    return lax.dot_general(a, b, (((0,), (0,)), ((), ())), preferred_element_type=F32)


def _sigmoid(z):
    return 1.0 / (1.0 + jnp.exp(-z))


def _flip(v, bit):
    return 1 - v if bit else v


def _peers():
    x, y, c = (lax.axis_index(a) for a in AXES)
    out = []
    for m in range(1, N_DEV):
        px, py, pc = _flip(x, (m >> 2) & 1), _flip(y, (m >> 1) & 1), _flip(c, m & 1)
        out.append((px, py, pc, 4 * px + 2 * py + pc))
    return out


class _Comm:
    def __init__(self, gathers=(), scatters=(), dests=None, into=None):
        self.n_g = len(gathers)
        self.operands = list(gathers) + list(scatters)
        self.n = len(self.operands)
        self.dests = dests or {}
        self.into = into or {}

    def out_shape(self):
        return [jax.ShapeDtypeStruct((N_DEV,) + a.shape if t < self.n_g else a.shape, a.dtype)
                for t, a in enumerate(self.operands)]

    def scratch(self):
        if not self.n:
            return []
        return [pltpu.SemaphoreType.DMA((self.n, N_DEV - 1)), pltpu.SemaphoreType.DMA((self.n, N_DEV - 1)),
                pltpu.SemaphoreType.DMA((self.n,))]

    def _copies(self, in_refs, out_refs, sems, arrivals):
        send_sems, recv_sems, local_sems = sems
        x, y, c = (lax.axis_index(a) for a in AXES)
        me = 4 * x + 2 * y + c

        def src(t, dev):
            return in_refs[t] if t < self.n_g else in_refs[t].at[dev]

        def member(t, dev):
            if t not in self.dests:
                return None
            return functools.reduce(jnp.logical_or, [dev == d for d in self.dests[t]])

        local = [(member(t, me), pltpu.make_async_copy(src(t, me), out_refs[t].at[me], local_sems.at[t]))
                 for t in range(self.n)]
        sends, recvs = [], []
        for t in range(self.n):
            for m, (px, py, pc, peer) in enumerate(_peers()):
                kw = dict(send_sem=send_sems.at[t, m], recv_sem=recv_sems.at[t, m],
                          device_id=(px, py, pc), device_id_type=pl.DeviceIdType.MESH)
                sends.append((member(t, peer), pltpu.make_async_remote_copy(
                    src_ref=src(t, peer), dst_ref=out_refs[t].at[me], **kw)))
                if arrivals:
                    recvs.append((member(t, me), pltpu.make_async_remote_copy(
                        src_ref=src(t, peer), dst_ref=out_refs[t].at[peer], **kw)))
        return local, sends, recvs

    @staticmethod
    def _do(cond, action):
        if cond is None:
            action()
        else:
            pl.when(cond)(action)

    def start(self, in_refs, out_refs, sems):
        local, sends, _ = self._copies(in_refs, out_refs, sems, arrivals=False)
        for cond, cp in local + sends:
            self._do(cond, cp.start)

    def finish(self, in_refs, out_refs, sems):
        local, sends, recvs = self._copies(in_refs, out_refs, sems, arrivals=True)
        for cond, cp in recvs:
            self._do(cond, cp.wait_recv)
        for cond, cp in sends:
            self._do(cond, cp.wait_send)
        for cond, cp in local:
            self._do(cond, cp.wait)


def _call(body, comm, *, name, grid, in_specs, out_specs, out_shape, operands, scratch_shapes=()):
    comm = comm or _Comm()
    n_in, n_out, n_scr, cn = len(in_specs), len(out_specs), len(scratch_shapes), comm.n
    landing = sorted(comm.into)
    aliases = {n_in + cn + q: n_out + t for q, t in enumerate(landing)}

    def wrapped(*refs):
        parts, o = [], 0
        for k in (n_in, cn, len(landing), n_out, cn, n_scr):
            parts.append(refs[o:o + k])
            o += k
        h_in, c_in, _, h_out, c_out, h_scr = parts
        sems = refs[o:]
        if cn:
            first = functools.reduce(jnp.logical_and, [pl.program_id(d) == 0 for d in range(len(grid))])

            @pl.when(first)
            def _():
                comm.start(c_in, c_out, sems)

        body(*h_in, *h_out, *h_scr)
        if cn:
            last = functools.reduce(jnp.logical_and, [pl.program_id(d) == grid[d] - 1 for d in range(len(grid))])

            @pl.when(last)
            def _():
                comm.finish(c_in, c_out, sems)

    any_ = pl.BlockSpec(memory_space=pl.ANY)
    res = pl.pallas_call(
        wrapped, name=name, grid=grid, in_specs=list(in_specs) + [any_] * (cn + len(landing)),
        out_specs=list(out_specs) + [any_] * cn,
        out_shape=list(out_shape) + comm.out_shape(), scratch_shapes=list(scratch_shapes) + comm.scratch(),
        input_output_aliases=aliases, compiler_params=_params(("arbitrary",) * len(grid)),
    )(*operands, *comm.operands, *[comm.into[t] for t in landing])
    return list(res[:n_out]), list(res[n_out:])


def _exchange(name, gathers, scatters):
    def body(tok_ref):
        tok_ref[...] = jnp.zeros_like(tok_ref)

    return _call(body, _Comm(gathers, scatters), name=name, grid=(1,), in_specs=[],
                 out_specs=[pl.BlockSpec((8, LANES), lambda i: (0, 0))],
                 out_shape=[jax.ShapeDtypeStruct((8, LANES), F32)], operands=[])[1]


def _ssm_prep(a_re, a_im, log_dt, bt_re, bt_im):
    def body(lr_ref, li_ref, ldt_ref, br_ref, bi_ref, lbr_ref, lbi_ref, qr_ref, qi_ref, bbr_ref, bbi_ref):
        lr, li = lr_ref[...], li_ref[...]
        dt = jnp.exp(ldt_ref[...])
        mag = jnp.exp(lr * dt)
        lbr, lbi = mag * jnp.cos(li * dt), mag * jnp.sin(li * dt)
        nr, ni = lbr - 1.0, lbi
        den = lr * lr + li * li
        qr = (nr * lr + ni * li) / den
        qi = (ni * lr - nr * li) / den
        br, bi = br_ref[...], bi_ref[...]
        lbr_ref[...], lbi_ref[...], qr_ref[...], qi_ref[...] = lbr, lbi, qr, qi
        bbr_ref[...] = qr * br - qi * bi
        bbi_ref[...] = qr * bi + qi * br

    s2 = jax.ShapeDtypeStruct(a_re.shape, F32)
    s3 = jax.ShapeDtypeStruct(bt_re.shape, F32)
    return pl.pallas_call(body, name="ssm_prep", out_shape=[s2, s2, s2, s2, s3, s3],
                          compiler_params=_params())(a_re, a_im, log_dt, bt_re, bt_im)


def _adam(w, g, m, v):
    m2 = ADAM_B1 * m + (1.0 - ADAM_B1) * g
    v2 = ADAM_B2 * v + (1.0 - ADAM_B2) * (g * g)
    m_hat = m2 / (1.0 - ADAM_B1 ** ADAM_STEP)
    v_hat = v2 / (1.0 - ADAM_B2 ** ADAM_STEP)
    delta = -ADAM_LR * (m_hat / (jnp.sqrt(v_hat) + ADAM_EPS) + ADAM_WD * w)
    return delta, m2, v2


def _small_update(direct, ssm):
    n_direct = len(direct)
    flat = [a for quad in direct for a in quad]
    names = ["da_r", "da_i", "dbb_r", "dbb_i", "lr", "li", "ldt", "bt_r", "bt_i", "lbr", "lbi", "qr", "qi"]
    flat += [ssm[k] for k in names]
    chain = ["a_re", "a_im", "log_dt", "bt_re", "bt_im"]
    for k in chain:
        flat += [ssm["w_" + k], ssm["m_" + k], ssm["v_" + k]]
    n_in = len(flat)

    def body(*refs):
        ins, outs = refs[:n_in], refs[n_in:]
        for p in range(n_direct):
            g, w, m, v = (r[...] for r in ins[4 * p:4 * p + 4])
            d, m2, v2 = _adam(w, g, m, v)
            outs[4 * p][...], outs[4 * p + 1][...], outs[4 * p + 2][...], outs[4 * p + 3][...] = g, d, m2, v2
        o = 4 * n_direct
        da_r, da_i, dbb_r, dbb_i, lr, li, ldt, bt_r, bt_i, lbr, lbi, qr, qi = (r[...] for r in ins[o:o + 13])
        dt = jnp.exp(ldt)
        g_br = qr * dbb_r + qi * dbb_i
        g_bi = qr * dbb_i - qi * dbb_r
        dq_r = jnp.sum(bt_r * dbb_r + bt_i * dbb_i, axis=1, keepdims=True)
        dq_i = jnp.sum(bt_r * dbb_i - bt_i * dbb_r, axis=1, keepdims=True)
        den = lr * lr + li * li
        cr, ci = lr / den, li / den
        gl_r = da_r + (cr * dq_r - ci * dq_i)
        gl_i = da_i + (cr * dq_i + ci * dq_r)
        w_r = qr * cr + qi * ci
        w_i = qi * cr - qr * ci
        g_lr = dt * (lbr * gl_r + lbi * gl_i) + (-w_r * dq_r - w_i * dq_i)
        g_li = dt * (lbr * gl_i - lbi * gl_r) + (-w_r * dq_i + w_i * dq_r)
        m_r = lr * lbr - li * lbi
        m_i = lr * lbi + li * lbr
        g_ldt = jnp.sum(m_r * gl_r + m_i * gl_i, axis=2, keepdims=True) * dt
        grads = [g_lr, g_li, g_ldt, g_br, g_bi]
        base_in, base_out = o + 13, 4 * n_direct
        for p, g in enumerate(grads):
            w, m, v = (r[...] for r in ins[base_in + 3 * p:base_in + 3 * p + 3])
            d, m2, v2 = _adam(w, g, m, v)
            q = base_out + 4 * p
            outs[q][...], outs[q + 1][...], outs[q + 2][...], outs[q + 3][...] = g, d, m2, v2

    out_shape = []
    for quad in direct:
        out_shape += [jax.ShapeDtypeStruct(quad[1].shape, F32)] * 4
    for k in chain:
        out_shape += [jax.ShapeDtypeStruct(ssm["w_" + k].shape, F32)] * 4
    res = pl.pallas_call(body, name="small_update", out_shape=out_shape, compiler_params=_params())(*flat)
    return [tuple(res[4 * p:4 * p + 4]) for p in range(n_direct + len(chain))]


def _sum_slots(name, pack):
    def body(p_ref, o_ref):
        acc = p_ref[0].astype(F32)
        for k in range(1, N_DEV):
            acc = acc + p_ref[k].astype(F32)
        o_ref[...] = acc

    return pl.pallas_call(body, name=name, out_shape=jax.ShapeDtypeStruct(pack.shape[1:], F32),
                          compiler_params=_params())(pack)


def _adam_big(name, recv, w, m, v, tr):
    _, rows, cols = recv.shape

    def body(r_ref, w_ref, m_ref, v_ref, g_ref, d_ref, m2_ref, v2_ref):
        g = r_ref[0].astype(F32)
        for k in range(1, N_DEV):
            g = g + r_ref[k].astype(F32)
        d, m2, v2 = _adam(w_ref[...], g, m_ref[...], v_ref[...])
        g_ref[...], d_ref[...], m2_ref[...], v2_ref[...] = g, d, m2, v2

    blk = pl.BlockSpec((tr, cols), lambda i: (i, 0))
    shp = jax.ShapeDtypeStruct((rows, cols), F32)
    return pl.pallas_call(
        body, name=name, grid=(rows // tr,),
        in_specs=[pl.BlockSpec((N_DEV, tr, cols), lambda i: (0, i, 0)), blk, blk, blk],
        out_specs=[blk] * 4, out_shape=[shp] * 4, compiler_params=_params(("parallel",)),
    )(recv, w, m, v)


def _chunk_block(tm, d):
    return pl.BlockSpec((N_CHUNK, tm // N_CHUNK, d), lambda i: (0, i, 0))


def _interleave(block):
    c, n, d = block.shape
    return pltpu.einshape("cjd->jcd", block).reshape(n * c, d)


def _norm_in(x3, g_pre, tm):
    _, steps, d_model = x3.shape
    seq = steps * N_CHUNK

    def body(x_ref, g_ref, xp_ref, h_ref, ht_ref):
        x = _interleave(x_ref[...])
        xp_ref[...] = x
        r = lax.rsqrt(jnp.mean(x * x, axis=-1, keepdims=True) + EPS)
        h = x * r * g_ref[...]
        h_ref[...] = h.astype(h_ref.dtype)
        ht_ref[...] = h.T.astype(ht_ref.dtype)

    rows = pl.BlockSpec((tm, d_model), lambda i: (i, 0))
    return pl.pallas_call(
        body, name="norm_in", grid=(seq // tm,),
        in_specs=[_chunk_block(tm, d_model), pl.BlockSpec((1, d_model), lambda i: (0, 0))],
        out_specs=[rows, rows, pl.BlockSpec((d_model, tm), lambda i: (0, i))],
        out_shape=[jax.ShapeDtypeStruct((seq, d_model), F32), jax.ShapeDtypeStruct((seq, d_model), MXU_DTYPE),
                   jax.ShapeDtypeStruct((d_model, seq), MXU_DTYPE)],
        compiler_params=_params(("parallel",)),
    )(x3, g_pre)


GATHER_ORDER = (0, 1, 4, 2, 6, 5, 3, 7)
PIECE_COLS = 256


def _fwd_in(h, w_pieces, me, comm, tm):
    seq, d_model = h.shape
    n_p, _, gw = w_pieces.shape
    n_i = seq // tm
    cn = comm.n
    consume = [(rel, r) for rel in (0, 1) for r in range(n_p)]
    consume += [(rel, r) for r in range(n_p) for rel in (4, 2, 5, 3)]
    consume += [(rel, r) for r in range(n_p) for rel in (6, 7)]
    n_q = len(consume)
    order = jnp.stack([jnp.bitwise_xor(me, rel) * n_p + r for rel, r in consume]).astype(jnp.int32)

    def body(order_ref, h_hbm, w_hbm, *rest):
        c_in, rest = rest[:cn], rest[cn:]
        proj_ref, wing = rest[0], rest[1]
        c_out, rest = rest[2:2 + cn], rest[2 + cn:]
        wbuf, send_sems, recv_sems, own_sems, load_sems, h_all, h_sems = rest[:7]
        c_sems = rest[7:]
        k, i = pl.program_id(0), pl.program_id(1)
        h_loads = [pltpu.make_async_copy(h_hbm.at[pl.ds(t * tm, tm)], h_all.at[pl.ds(t * tm, tm)], h_sems.at[t])
                   for t in range(n_i)]
        x, y, c = (lax.axis_index(a) for a in AXES)
        me_ = 4 * x + 2 * y + c

        def dev(rel):
            return _flip(x, (rel >> 2) & 1), _flip(y, (rel >> 1) & 1), _flip(c, rel & 1)

        def slot(rel):
            px, py, pc = dev(rel)
            return 4 * px + 2 * py + pc

        def remote(src, block, r, to_rel, sem):
            return pltpu.make_async_remote_copy(
                src_ref=src, dst_ref=wing.at[block, r], send_sem=send_sems.at[sem, r], recv_sem=recv_sems.at[sem, r],
                device_id=dev(to_rel), device_id_type=pl.DeviceIdType.MESH)

        pieces = range(n_p)
        own = [pltpu.make_async_copy(w_hbm.at[r], wing.at[me_, r], own_sems.at[r]) for r in pieces]
        first_hand = {p: [remote(w_hbm.at[r], me_, r, GATHER_ORDER[p], p - 1) for r in pieces] for p in (1, 2, 3)}
        relay = {2: (c == 0, [remote(wing.at[slot(4), r], slot(4), r, 2, 3) for r in pieces]),
                 3: (c == 1, [remote(wing.at[slot(2), r], slot(2), r, 4, 3) for r in pieces])}
        passed_on = {p: [remote(wing.at[slot(GATHER_ORDER[p]), r], slot(GATHER_ORDER[p]), r, 1, p + 2) for r in pieces]
                     for p in (2, 3, 4)}

        def load(q):
            rel, r = consume[q]
            return pltpu.make_async_copy(wing.at[slot(rel), r], wbuf.at[q % 2], load_sems.at[q % 2])

        def take(q):
            rel, r = consume[q]
            p = GATHER_ORDER.index(rel)
            if p == 0:
                own[r].wait()
            else:
                remote(w_hbm.at[r], slot(rel), r, rel, p - 1).wait_recv()
            if p in relay:
                pl.when(relay[p][0])(relay[p][1][r].start)
            if p in passed_on:
                passed_on[p][r].start()
            load(q).start()

        @pl.when((k == 0) & (i == 0))
        def _():
            for r in pieces:
                own[r].start()
            for r in pieces:
                for p in (1, 2, 3):
                    first_hand[p][r].start()
            for cp in h_loads:
                cp.start()
            comm.start(c_in, c_out, c_sems)
            take(0)

        for t in range(n_i):
            pl.when((k == 0) & (i == t))(h_loads[t].wait)

        for q in range(n_q):
            @pl.when((k == q) & (i == 0))
            def _():
                load(q).wait()

            if q + 1 < n_q:
                @pl.when((k == q) & (i == n_i - 1))
                def _():
                    take(q + 1)

        proj_ref[...] = _dot(h_all[pl.ds(pl.multiple_of(i * tm, tm), tm), :], wbuf[k % 2]).astype(proj_ref.dtype)

        @pl.when((k == n_q - 1) & (i == n_i - 1))
        def _():
            for p in first_hand:
                for cp in first_hand[p]:
                    cp.wait_send()
            for p in passed_on:
                for cp in passed_on[p]:
                    cp.wait_send()
            for cond, cps in relay.values():
                for cp in cps:
                    pl.when(cond)(cp.wait_send)
            comm.finish(c_in, c_out, c_sems)

    any_ = pl.BlockSpec(memory_space=pl.ANY)
    grid_spec = pltpu.PrefetchScalarGridSpec(
        num_scalar_prefetch=1, grid=(n_q, n_i),
        in_specs=[any_, any_] + [any_] * cn,
        out_specs=[pl.BlockSpec((tm, gw), lambda k, i, o: (i, o[k])), any_] + [any_] * cn,
        scratch_shapes=[pltpu.VMEM((2, d_model, gw), w_pieces.dtype), pltpu.SemaphoreType.DMA((N_DEV - 1, n_p)),
                        pltpu.SemaphoreType.DMA((N_DEV - 1, n_p)), pltpu.SemaphoreType.DMA((n_p,)),
                        pltpu.SemaphoreType.DMA((2,)), pltpu.VMEM((seq, d_model), h.dtype),
                        pltpu.SemaphoreType.DMA((n_i,))]
        + comm.scratch())
    res = pl.pallas_call(
        body, name="fwd_in", grid_spec=grid_spec,
        out_shape=[jax.ShapeDtypeStruct((seq, N_DEV * n_p * gw), PROJ_DTYPE),
                   jax.ShapeDtypeStruct((N_DEV, n_p, d_model, gw), w_pieces.dtype)] + comm.out_shape(),
        compiler_params=_params(("arbitrary", "arbitrary")),
    )(order, h, w_pieces, *comm.operands)
    return res[0], res[1], list(res[2:])


def _shift_prev(a):
    n = a.shape[0]
    last = a[n - N_CHUNK:, :]
    row = lax.broadcasted_iota(jnp.int32, last.shape, 0)
    wrap = jnp.where(row == 0, 0.0, pltpu.roll(last, 1, axis=0))
    return jnp.concatenate([wrap, a[:n - N_CHUNK, :]], axis=0)


def _shift_next(a):
    first = a[:N_CHUNK, :]
    row = lax.broadcasted_iota(jnp.int32, first.shape, 0)
    wrap = jnp.where(row == N_CHUNK - 1, 0.0, pltpu.roll(first, N_CHUNK - 1, axis=0))
    return jnp.concatenate([a[N_CHUNK:, :], wrap], axis=0)


def _conv_specs(seq, d_conv):
    nblk = d_conv // LANES
    return [pl.BlockSpec((seq, LANES), functools.partial(lambda i, o: (0, o + i), o=q * nblk)) for q in range(4)]


def _conv_fwd(proj, conv_w8, conv_b, d_conv):
    seq = proj.shape[0]

    def body(bg_ref, cg_ref, v_ref, zc_ref, w_ref, b_ref, y_ref):
        cv = cg_ref[...].astype(F32) * v_ref[...].astype(F32)
        s1 = _shift_prev(cv)
        s2 = _shift_prev(s1)
        conv = b_ref[...] + w_ref[0:1, :] * s2 + w_ref[1:2, :] * s1 + w_ref[2:3, :] * cv
        z = zc_ref[...].astype(F32)
        y_ref[...] = (bg_ref[...].astype(F32) * conv * (z * _sigmoid(z))).astype(y_ref.dtype)

    col = pl.BlockSpec((seq, LANES), lambda i: (0, i))
    return pl.pallas_call(
        body, name="conv_fwd", grid=(d_conv // LANES,),
        in_specs=_conv_specs(seq, d_conv) + [pl.BlockSpec((8, LANES), lambda i: (0, i)), pl.BlockSpec((1, LANES), lambda i: (0, i))],
        out_specs=col, out_shape=jax.ShapeDtypeStruct((seq, d_conv), MXU_DTYPE),
        compiler_params=_params(("parallel",)),
    )(proj, proj, proj, proj, conv_w8, conv_b)


def _conv_bwd(proj, dyc, conv_w8, conv_b, d_conv, comm=None):
    seq = proj.shape[0]

    def body(bg_ref, cg_ref, v_ref, zc_ref, dy_ref, w_ref, b_ref, d4_ref, dcb_ref, dcw_ref):
        bg, cg, v, z = (r[...].astype(F32) for r in (bg_ref, cg_ref, v_ref, zc_ref))
        w0, w1, w2 = w_ref[0:1, :], w_ref[1:2, :], w_ref[2:3, :]
        cv = cg * v
        s1 = _shift_prev(cv)
        s2 = _shift_prev(s1)
        conv = b_ref[...] + w0 * s2 + w1 * s1 + w2 * cv
        sig = _sigmoid(z)
        dy = dy_ref[...].astype(F32)
        g1 = dy * (z * sig)
        d_conv_ = g1 * bg
        d4_ref[0] = (g1 * conv).astype(d4_ref.dtype)
        d4_ref[3] = (dy * bg * conv * (sig * (1.0 + z * (1.0 - sig)))).astype(d4_ref.dtype)
        n1 = _shift_next(d_conv_)
        n2 = _shift_next(n1)
        d_cv = w2 * d_conv_ + w1 * n1 + w0 * n2
        d4_ref[1] = (d_cv * v).astype(d4_ref.dtype)
        d4_ref[2] = (d_cv * cg).astype(d4_ref.dtype)
        dcb_ref[...] = jnp.sum(d_conv_, axis=0, keepdims=True)
        rows = [jnp.sum(d_conv_ * s, axis=0, keepdims=True) for s in (s2, s1, cv)]
        dcw_ref[...] = jnp.concatenate(rows + [jnp.zeros((5, LANES), F32)], axis=0)

    col = pl.BlockSpec((seq, LANES), lambda i: (0, i))
    return _call(
        body, comm, name="conv_bwd", grid=(d_conv // LANES,),
        in_specs=_conv_specs(seq, d_conv) + [col, pl.BlockSpec((8, LANES), lambda i: (0, i)), pl.BlockSpec((1, LANES), lambda i: (0, i))],
        out_specs=[pl.BlockSpec((4, seq, LANES), lambda i: (0, 0, i)), pl.BlockSpec((1, LANES), lambda i: (0, i)),
                   pl.BlockSpec((8, LANES), lambda i: (0, i))],
        out_shape=[jax.ShapeDtypeStruct((4, seq, d_conv), MXU_DTYPE), jax.ShapeDtypeStruct((1, d_conv), F32),
                   jax.ShapeDtypeStruct((8, d_conv), F32)],
        operands=[proj, proj, proj, proj, dyc, conv_w8, conv_b])


def _cmul(ar, ai, br, bi):
    return ar * br - ai * bi, ar * bi + ai * br


def _down(v, k):
    row = lax.broadcasted_iota(jnp.int32, v.shape, 0)
    return jnp.where(row >= k, pltpu.roll(v, k, axis=0), 0.0)


def _up(v, k):
    row = lax.broadcasted_iota(jnp.int32, v.shape, 0)
    return jnp.where(row < N_CHUNK - k, pltpu.roll(v, N_CHUNK - k, axis=0), 0.0)


def _chunk_carry(fr, fi, mr, mi, shift):
    vr, vi = shift(fr, 1), shift(fi, 1)
    for k in (1, 2, 4):
        pr, pi = _cmul(mr, mi, shift(vr, k), shift(vi, k))
        vr, vi = vr + pr, vi + pi
        mr, mi = _cmul(mr, mi, mr, mi)
    return vr, vi


def _tile(ref, j, width, part):
    return ref.at[pl.ds(pl.multiple_of(j * N_CHUNK, N_CHUNK), N_CHUNK), pl.ds(part * width, width)]


def _row(t, k):
    return jnp.broadcast_to(t[k:k + 1, :], t.shape)


def _power_table(tab_ref, ar, ai, steps, width):
    e = lax.broadcasted_iota(jnp.int32, ar.shape, 0) + 1
    rr, ri = jnp.ones_like(ar), jnp.zeros_like(ai)
    br, bi = ar, ai
    for bit in range(4):
        mr, mi = _cmul(rr, ri, br, bi)
        take = ((e >> bit) & 1) == 1
        rr, ri = jnp.where(take, mr, rr), jnp.where(take, mi, ri)
        if bit < 3:
            br, bi = _cmul(br, bi, br, bi)
    _tile(tab_ref, 0, width, 0)[...] = rr
    _tile(tab_ref, 0, width, 1)[...] = ri

    def step(m, carry):
        tr, ti = _cmul(carry[0], carry[1], br, bi)
        _tile(tab_ref, m, width, 0)[...] = tr
        _tile(tab_ref, m, width, 1)[...] = ti
        return tr, ti

    lax.fori_loop(1, steps // N_CHUNK, step, (rr, ri))


def _last_power(tab_ref, steps, width):
    shape = (N_CHUNK, width)
    return (jnp.broadcast_to(tab_ref[steps - 1:steps, 0:width], shape),
            jnp.broadcast_to(tab_ref[steps - 1:steps, width:2 * width], shape))


def _scan_fwd(s_ref, ar, ai, steps, width):
    def step(j, carry):
        sr, si = carry
        nr = ar * sr - ai * si + _tile(s_ref, j, width, 0)[...]
        ni = ar * si + ai * sr + _tile(s_ref, j, width, 1)[...]
        _tile(s_ref, j, width, 0)[...] = nr
        _tile(s_ref, j, width, 1)[...] = ni
        return nr, ni

    z = jnp.zeros((N_CHUNK, width), F32)
    return lax.fori_loop(0, steps, step, (z, z), unroll=4)


def _scan_both(s_ref, g_ref, ar, ai, steps, width):
    def step(q, carry):
        sr, si, gr, gi = carry
        j, jb = q, steps - 1 - q
        nsr = ar * sr - ai * si + _tile(s_ref, j, width, 0)[...]
        nsi = ar * si + ai * sr + _tile(s_ref, j, width, 1)[...]
        ngr = ar * gr + ai * gi + _tile(g_ref, jb, width, 0)[...]
        ngi = ar * gi - ai * gr + _tile(g_ref, jb, width, 1)[...]
        _tile(s_ref, j, width, 0)[...] = nsr
        _tile(s_ref, j, width, 1)[...] = nsi
        _tile(g_ref, jb, width, 0)[...] = ngr
        _tile(g_ref, jb, width, 1)[...] = ngi
        return nsr, nsi, ngr, ngi

    z = jnp.zeros((N_CHUNK, width), F32)
    return lax.fori_loop(0, steps, step, (z, z, z, z), unroll=2)


def _patch_fwd(s_ref, tab_ref, cr, ci, steps, width):
    def tile(m, _):
        tr, ti = _tile(tab_ref, m, width, 0)[...], _tile(tab_ref, m, width, 1)[...]
        for k in range(N_CHUNK):
            fr, fi = _cmul(_row(tr, k), _row(ti, k), cr, ci)
            j = m * N_CHUNK + k
            _tile(s_ref, j, width, 0)[...] += fr
            _tile(s_ref, j, width, 1)[...] += fi
        return 0

    lax.fori_loop(0, steps // N_CHUNK, tile, 0)


def _lam_rows(lam_ref, hh, width):
    return (jnp.broadcast_to(lam_ref[hh, 0:1, :], (N_CHUNK, width)),
            jnp.broadcast_to(lam_ref[hh, 1:2, :], (N_CHUNK, width)))


def _ssm_specs(seq, col0):
    return dict(
        col=pl.BlockSpec((seq, LANES), lambda i: (0, col0 + i)),
        lam=pl.BlockSpec((2, 2, HALF_W), lambda i: (i, 0, 0)),
        bb=pl.BlockSpec((2, HALF_CH, 2 * HALF_W), lambda i: (i, 0, 0)),
        cc=pl.BlockSpec((2, 2 * HALF_W, HALF_CH), lambda i: (i, 0, 0)),
        vec=pl.BlockSpec((1, LANES), lambda i: (0, i)),
        out=pl.BlockSpec((seq, LANES), lambda i: (0, i)),
    )


def _ssm_fwd(proj, lam, bbcat, cccat, d_skip, d_ssm, u_col0, comm=None):
    seq = proj.shape[0]
    steps = seq // N_CHUNK

    def body(u_ref, lam_ref, bb_ref, cc_ref, d_ref, yp_ref, s_ref, tab_ref):
        for hh in range(2):
            lanes = slice(HALF_CH * hh, HALF_CH * (hh + 1))
            u_half = u_ref[:, lanes].astype(F32)
            ar, ai = _lam_rows(lam_ref, hh, HALF_W)
            _power_table(tab_ref, ar, ai, steps, HALF_W)
            s_ref[...] = _dot(u_half.astype(MXU_DTYPE), bb_ref[hh])
            fr, fi = _scan_fwd(s_ref, ar, ai, steps, HALF_W)
            pr, pi = _last_power(tab_ref, steps, HALF_W)
            cr, ci = _chunk_carry(fr, fi, pr, pi, _down)
            _patch_fwd(s_ref, tab_ref, cr, ci, steps, HALF_W)
            y = _dot(s_ref[...].astype(MXU_DTYPE), cc_ref[hh])
            yp_ref[:, lanes] = y + d_ref[:, lanes] * u_half

    sp = _ssm_specs(seq, u_col0 // LANES)
    return _call(
        body, comm, name="ssm_fwd", grid=(d_ssm // LANES,),
        in_specs=[sp["col"], sp["lam"], sp["bb"], sp["cc"], sp["vec"]], out_specs=[sp["out"]],
        out_shape=[jax.ShapeDtypeStruct((seq, d_ssm), F32)],
        scratch_shapes=[pltpu.VMEM((seq, 2 * HALF_W), F32), pltpu.VMEM((steps, 2 * HALF_W), F32)],
        operands=[proj, lam, bbcat, cccat, d_skip])


def _ssm_bwd(proj, dyp, lam, bbcat, cccat, d_skip, d_ssm, u_col0, comm=None):
    seq = proj.shape[0]
    steps = seq // N_CHUNK
    n_half = 2 * d_ssm // LANES
    width = HALF_W

    def body(u_ref, dyp_ref, lam_ref, bb_ref, cc_ref, d_ref, du_ref, dbb_ref, dcc_ref, da_ref, dd_ref,
             s_ref, g_ref, tab_ref):
        n_tiles = steps // N_CHUNK
        for hh in range(2):
            lanes = slice(HALF_CH * hh, HALF_CH * (hh + 1))
            u_half, dy_half = u_ref[:, lanes].astype(F32), dyp_ref[:, lanes].astype(F32)
            dy_mx = dy_half.astype(MXU_DTYPE)
            ar, ai = _lam_rows(lam_ref, hh, width)
            _power_table(tab_ref, ar, ai, steps, width)
            s_ref[...] = _dot(u_half.astype(MXU_DTYPE), bb_ref[hh])
            g_ref[...] = _dot_nt(dy_mx, cc_ref[hh])
            fr, fi, lr_, li_ = _scan_both(s_ref, g_ref, ar, ai, steps, width)
            pr, pi = _last_power(tab_ref, steps, width)
            cr, ci = _chunk_carry(fr, fi, pr, pi, _down)
            gr, gi = _chunk_carry(lr_, li_, pr, -pi, _up)

            def tile(m, carry):
                sr, si, accr, acci = carry
                t1r, t1i = _tile(tab_ref, m, width, 0)[...], _tile(tab_ref, m, width, 1)[...]
                mb = n_tiles - 1 - m
                t2r, t2i = _tile(tab_ref, mb, width, 0)[...], _tile(tab_ref, mb, width, 1)[...]
                for k in range(N_CHUNK):
                    j = m * N_CHUNK + k
                    xr, xi = _cmul(_row(t1r, k), _row(t1i, k), cr, ci)
                    nsr = _tile(s_ref, j, width, 0)[...] + xr
                    nsi = _tile(s_ref, j, width, 1)[...] + xi
                    _tile(s_ref, j, width, 0)[...] = nsr
                    _tile(s_ref, j, width, 1)[...] = nsi
                    qr, qi = _row(t2r, N_CHUNK - 1 - k), _row(t2i, N_CHUNK - 1 - k)
                    ngr = _tile(g_ref, j, width, 0)[...] + (qr * gr + qi * gi)
                    ngi = _tile(g_ref, j, width, 1)[...] + (qr * gi - qi * gr)
                    _tile(g_ref, j, width, 0)[...] = ngr
                    _tile(g_ref, j, width, 1)[...] = ngi
                    accr = accr + (sr * ngr + si * ngi)
                    acci = acci + (sr * ngi - si * ngr)
                    sr, si = nsr, nsi
                return sr, si, accr, acci

            z = jnp.zeros((N_CHUNK, width), F32)
            _, _, accr, acci = lax.fori_loop(0, n_tiles, tile, (cr, ci, z, z))
            da_ref[hh, :, 0:width] = jnp.sum(accr, axis=0, keepdims=True)
            da_ref[hh, :, width:2 * width] = jnp.sum(acci, axis=0, keepdims=True)

            g_mx = g_ref[...].astype(MXU_DTYPE)
            dcc_ref[hh] = _dot_tn(dy_mx, s_ref[...].astype(MXU_DTYPE)).T
            dbb_ref[hh] = _dot_tn(u_half.astype(MXU_DTYPE), g_mx)
            du = _dot_nt(g_mx, bb_ref[hh]) + d_ref[:, lanes] * dy_half
            du_ref[:, lanes] = du.astype(du_ref.dtype)
            dd_ref[:, lanes] = jnp.sum(dy_half * u_half, axis=0, keepdims=True)

    sp = _ssm_specs(seq, u_col0 // LANES)
    return _call(
        body, comm, name="ssm_bwd", grid=(d_ssm // LANES,),
        in_specs=[sp["col"], sp["out"], sp["lam"], sp["bb"], sp["cc"], sp["vec"]],
        out_specs=[sp["out"], sp["bb"], sp["cc"], pl.BlockSpec((2, 1, 2 * width), lambda i: (i, 0, 0)), sp["vec"]],
        out_shape=[jax.ShapeDtypeStruct((seq, d_ssm), MXU_DTYPE),
                   jax.ShapeDtypeStruct((n_half, HALF_CH, 2 * width), F32),
                   jax.ShapeDtypeStruct((n_half, 2 * width, HALF_CH), F32),
                   jax.ShapeDtypeStruct((n_half, 1, 2 * width), F32),
                   jax.ShapeDtypeStruct((1, d_ssm), F32)],
        scratch_shapes=[pltpu.VMEM((seq, 2 * width), F32), pltpu.VMEM((seq, 2 * width), F32),
                        pltpu.VMEM((steps, 2 * width), F32)],
        operands=[proj, dyp, lam, bbcat, cccat, d_skip])


def _tail(xp, t3, proj, yconv, yp, w_glu, b_glu, w_out, g_post, zs_col0, tm):
    seq, d_model = xp.shape
    d_conv, d_ssm = yconv.shape[1], yp.shape[1]
    d_mix = d_conv + d_ssm
    assert zs_col0 % d_ssm == 0

    def body(x_ref, t_ref, zs_ref, yc_ref, yp_ref, wglu_hbm, bglu_ref, wout_hbm, gpost_ref,
             dy_ref, do_ref, mixt_ref, dyc_ref, dyp_ref, dzs_ref, ygt_ref, dq_ref, loss_ref, dgpost_ref, dbglu_ref,
             wglu, wout):
        @pl.when(pl.program_id(0) == 0)
        def _():
            pltpu.sync_copy(wglu_hbm, wglu)
            pltpu.sync_copy(wout_hbm, wout)
            loss_ref[...] = jnp.zeros_like(loss_ref)
            dgpost_ref[...] = jnp.zeros_like(dgpost_ref)
            dbglu_ref[...] = jnp.zeros_like(dbglu_ref)

        a = yp_ref[...]
        th = jnp.tanh(GELU_C * (a + GELU_K * (a * a * a)))
        yg = a * (0.5 * (1.0 + th))
        dgelu = 0.5 * (1.0 + th) + 0.5 * a * (1.0 - th * th) * (GELU_C * (1.0 + 3.0 * GELU_K * a * a))
        yg_mx = yg.astype(MXU_DTYPE)
        sq = _sigmoid(_dot(yg_mx, wglu[...]) + bglu_ref[...])
        y2 = yg * sq
        zs = zs_ref[...].astype(F32)
        sz = _sigmoid(zs)
        silz = zs * sz
        yc, ys = yc_ref[...].astype(F32), y2 * silz
        mix = jnp.concatenate([yc, ys], axis=1).astype(MXU_DTYPE)
        mixt_ref[0:d_conv, :] = yc.T.astype(MXU_DTYPE)
        mixt_ref[d_conv:, :] = ys.T.astype(MXU_DTYPE)
        o = _dot(mix, wout[...])
        r2 = lax.rsqrt(jnp.mean(o * o, axis=-1, keepdims=True) + EPS)
        on = o * r2
        gpost = gpost_ref[...]
        err = (x_ref[...] + on * gpost) - _interleave(t_ref[...])
        loss_ref[...] += 0.5 * jnp.sum(jnp.mean(err * err, axis=-1, keepdims=True), axis=0, keepdims=True)
        dy = err * (1.0 / d_model)
        dy_ref[...] = dy
        dgpost_ref[...] += jnp.sum(dy * on, axis=0, keepdims=True)
        d_on = dy * gpost
        d_o = r2 * (d_on - on * jnp.mean(d_on * on, axis=-1, keepdims=True))
        do_mx = d_o.astype(MXU_DTYPE)
        do_ref[...] = do_mx
        d_mix_ = _dot_nt(do_mx, wout[...])
        dyc_ref[...] = d_mix_[:, :d_conv].astype(dyc_ref.dtype)
        d_yssm = d_mix_[:, d_conv:]
        d_y2 = d_yssm * silz
        dzs_ref[...] = (d_yssm * y2 * (sz * (1.0 + zs * (1.0 - sz)))).astype(dzs_ref.dtype)
        d_q = d_y2 * yg * (sq * (1.0 - sq))
        dq_mx = d_q.astype(MXU_DTYPE)
        dq_ref[...] = dq_mx
        ygt_ref[...] = yg.T.astype(MXU_DTYPE)
        dbglu_ref[...] += jnp.sum(d_q, axis=0, keepdims=True)
        d_yg = d_y2 * sq + _dot_nt(dq_mx, wglu[...])
        dyp_ref[...] = (d_yg * dgelu).astype(dyp_ref.dtype)

    def rows(width, col=0):
        return pl.BlockSpec((tm, width), lambda i: (i, col))

    def fixed(width):
        return pl.BlockSpec((1, width), lambda i: (0, 0))

    def cols(height):
        return pl.BlockSpec((height, tm), lambda i: (0, i))

    any_ = pl.BlockSpec(memory_space=pl.ANY)
    return pl.pallas_call(
        body, name="tail", grid=(seq // tm,),
        in_specs=[rows(d_model), _chunk_block(tm, d_model), rows(d_ssm, zs_col0 // d_ssm), rows(d_conv), rows(d_ssm),
                  any_, fixed(d_ssm), any_, fixed(d_model)],
        out_specs=[rows(d_model), rows(d_model), cols(d_mix), rows(d_conv), rows(d_ssm), rows(d_ssm), cols(d_ssm),
                   rows(d_ssm), fixed(LANES), fixed(d_model), fixed(d_ssm)],
        out_shape=[jax.ShapeDtypeStruct((seq, d_model), F32), jax.ShapeDtypeStruct((seq, d_model), MXU_DTYPE),
                   jax.ShapeDtypeStruct((d_mix, seq), MXU_DTYPE), jax.ShapeDtypeStruct((seq, d_conv), MXU_DTYPE),
                   jax.ShapeDtypeStruct((seq, d_ssm), MXU_DTYPE), jax.ShapeDtypeStruct((seq, d_ssm), MXU_DTYPE),
                   jax.ShapeDtypeStruct((d_ssm, seq), MXU_DTYPE), jax.ShapeDtypeStruct((seq, d_ssm), MXU_DTYPE),
                   jax.ShapeDtypeStruct((1, LANES), F32), jax.ShapeDtypeStruct((1, d_model), F32),
                   jax.ShapeDtypeStruct((1, d_ssm), F32)],
        scratch_shapes=[pltpu.VMEM(w_glu.shape, MXU_DTYPE), pltpu.VMEM(w_out.shape, MXU_DTYPE)],
        compiler_params=_params(("arbitrary",)),
    )(xp, t3, proj, yconv, yp, w_glu, b_glu, w_out, g_post)


def _bwd_in(d4, du, dzs, gr, win_g, xp, dy, g_pre, comm, tm):
    seq, d_model = xp.shape
    nb, n_p, _, gw = win_g.shape
    nc = n_p * gw
    per = d4.shape[2] // gr

    def body(d4_ref, du_ref, dzs_ref, w_hbm, x_ref, dy_ref, g_ref, gx_ref, dg_ref, w_all, w_sems):
        def granule(g):
            p, cols = g // per, slice(g % per * gr, (g % per + 1) * gr)
            if p < 4:
                return d4_ref[p, :, cols]
            return du_ref[:, cols] if p == 4 else dzs_ref[:, cols]

        i = pl.program_id(0)
        loads = [[pltpu.make_async_copy(w_hbm.at[k, r], w_all.at[k, :, pl.ds(r * gw, gw)], w_sems.at[k, r])
                  for r in range(n_p)] for k in range(nb)]

        @pl.when(i == 0)
        def _():
            dg_ref[...] = jnp.zeros_like(dg_ref)
            for row in loads:
                for cp in row:
                    cp.start()

        dh = None
        for k in range(nb):
            @pl.when(i == 0)
            def _():
                for cp in loads[k]:
                    cp.wait()

            dp = jnp.concatenate([granule(g) for g in range(k * nc // gr, (k + 1) * nc // gr)], axis=1)
            part = _dot_nt(dp, w_all[k])
            dh = part if dh is None else dh + part

        x = x_ref[...]
        r = lax.rsqrt(jnp.mean(x * x, axis=-1, keepdims=True) + EPS)
        xn = x * r
        dg_ref[...] += jnp.sum(dh * xn, axis=0, keepdims=True)
        dxn = dh * g_ref[...]
        gx_ref[...] = r * (dxn - xn * jnp.mean(dxn * xn, axis=-1, keepdims=True)) + dy_ref[...]

    row = pl.BlockSpec((tm, d_model), lambda i: (i, 0))
    vec = pl.BlockSpec((1, d_model), lambda i: (0, 0))
    return _call(
        body, comm, name="bwd_in", grid=(seq // tm,),
        in_specs=[pl.BlockSpec((4, tm, d4.shape[2]), lambda i: (0, i, 0)),
                  pl.BlockSpec((tm, du.shape[1]), lambda i: (i, 0)), pl.BlockSpec((tm, dzs.shape[1]), lambda i: (i, 0)),
                  pl.BlockSpec(memory_space=pl.ANY), row, row, vec],
        out_specs=[row, vec],
        out_shape=[jax.ShapeDtypeStruct((seq, d_model), F32), jax.ShapeDtypeStruct((1, d_model), F32)],
        scratch_shapes=[pltpu.VMEM((nb, d_model, nc), win_g.dtype), pltpu.SemaphoreType.DMA((nb, n_p))],
        operands=[d4, du, dzs, win_g, xp, dy, g_pre])


def _lookup(g, table):
    out = jnp.int32(table[0])
    for gi in range(1, len(table)):
        if table[gi] != table[gi - 1]:
            out = jnp.where(g >= gi, jnp.int32(table[gi]), out)
    return out


def _held(values, used):
    cur = next(v for v, u in zip(values, used) if u)
    out = []
    for v, u in zip(values, used):
        cur = v if u else cur
        out.append(cur)
    return out


def _dw_in(name, ht, d4, du, dzs, granules, gr, nc, tm, comm=None):
    d_model, seq = ht.shape
    per = d4.shape[2] // gr
    piece, col = [g // per for g in granules], [g % per for g in granules]
    sources = [(d4, [p < 4 for p in piece]), (du, [p == 4 for p in piece]), (dzs, [p == 5 for p in piece])]
    sources = [(a, used) for a, used in sources if any(used)]
    select = [next(s for s, (_, used) in enumerate(sources) if used[q]) for q in range(len(granules))]
    owner, place = [g * gr // nc for g in granules], [g * gr % nc // gr for g in granules]

    def body(a_ref, *refs):
        src_refs, o_ref = refs[:-1], refs[-1]
        j = pl.program_id(1)
        for s, ref in enumerate(src_refs):
            @pl.when(_lookup(j, select) == s)
            def _():
                o_ref[...] = _dot(a_ref[...], ref[...]).astype(o_ref.dtype)

    in_specs = [pl.BlockSpec((tm, seq), lambda i, j: (i, 0))]
    for a, used in sources:
        cols = _held(col, used)
        if a.ndim == 3:
            rows = _held(piece, used)
            in_specs.append(pl.BlockSpec((None, seq, gr), functools.partial(
                lambda i, j, rows, cols: (_lookup(j, rows), 0, _lookup(j, cols)), rows=rows, cols=cols)))
        else:
            in_specs.append(pl.BlockSpec((seq, gr), functools.partial(
                lambda i, j, cols: (0, _lookup(j, cols)), cols=cols)))
    return _call(
        body, comm, name=name, grid=(d_model // tm, len(granules)), in_specs=in_specs,
        out_specs=[pl.BlockSpec((None, tm, gr), lambda i, j: (_lookup(j, owner), i, _lookup(j, place)))],
        out_shape=[jax.ShapeDtypeStruct((N_DEV, d_model, nc), MXU_DTYPE)],
        operands=[ht] + [a for a, _ in sources])


def _wgrad(name, at, b, tm, tn, out_shape, out_block, out_index, comm=None):
    m, seq = at.shape
    n = b.shape[1]

    def body(a_ref, b_ref, o_ref):
        o_ref[...] = _dot(a_ref[...], b_ref[...]).astype(o_ref.dtype)

    return _call(
        body, comm, name=name, grid=(n // tn, m // tm),
        in_specs=[pl.BlockSpec((tm, seq), lambda j, i: (i, 0)), pl.BlockSpec((seq, tn), lambda j, i: (0, j))],
        out_specs=[pl.BlockSpec(out_block, lambda j, i: out_index(i, j))],
        out_shape=[jax.ShapeDtypeStruct(out_shape, MXU_DTYPE)],
        operands=[at, b])


def _eye_g():
    return jnp.eye(HALF_G, dtype=F32)


def _bb_blockdiag(bbt_r, bbt_i):
    n_half = bbt_r.shape[0] // HALF_G

    def one(t):
        t = t.reshape(n_half, HALF_G, SSM_GROUP, SSM_STATE)
        t = t[:, :, :, None, :] * _eye_g()[None, :, None, :, None]
        return t.reshape(n_half, HALF_CH, HALF_W)

    return jnp.concatenate([one(bbt_r), one(bbt_i)], axis=-1)


def _cc_blockdiag(c_re, c_im):
    n_half = c_re.shape[0] // HALF_G

    def one(t):
        t = t.reshape(n_half, HALF_G, SSM_GROUP, SSM_STATE)
        t = jnp.transpose(t, (0, 3, 1, 2))
        t = t[:, None, :, :, :] * _eye_g()[None, :, None, :, None]
        return t.reshape(n_half, HALF_W, HALF_CH)

    return jnp.concatenate([one(c_re), one(-c_im)], axis=1)


def _bb_diag(dbb):
    n_half = dbb.shape[0]
    t = dbb.reshape(n_half, HALF_G, SSM_GROUP, 2, HALF_G, SSM_STATE)
    t = jnp.sum(t * _eye_g()[None, :, None, None, :, None], axis=4)
    t = jnp.transpose(t, (3, 0, 1, 2, 4))
    return t.reshape(2, n_half * HALF_G, SSM_GROUP, SSM_STATE)


def _cc_diag(dcc):
    n_half = dcc.shape[0]
    t = dcc.reshape(n_half, 2, HALF_G, SSM_STATE, HALF_G, SSM_GROUP)
    t = jnp.sum(t * _eye_g()[None, None, :, None, :, None], axis=2)
    t = jnp.transpose(t, (1, 0, 3, 4, 2))
    return t.reshape(2, n_half * HALF_G, SSM_GROUP, SSM_STATE)


def _unpermute_rows(a):
    seq, d = a.shape
    return a.reshape(seq // N_CHUNK, N_CHUNK, d).transpose(1, 0, 2).reshape(seq, d)


def _pack_rows(shape):
    return -(-math.prod(shape) // (8 * LANES)) * 8


def _pack(parts, dtype=F32):
    rows = []
    for p in parts:
        flat = p.reshape(-1).astype(dtype)
        rows.append(jnp.pad(flat, (0, _pack_rows(p.shape) * LANES - flat.shape[0])).reshape(-1, LANES))
    return jnp.concatenate(rows, axis=0)


def _unpack(packed, shapes):
    out, o = [], 0
    for s in shapes:
        n = _pack_rows(s)
        out.append(packed[o:o + n].reshape(-1)[:math.prod(s)].reshape(s))
        o += n
    return out


def kernel(x, norm_pre_g, w_in, conv_w, conv_b, ssm_a_re, ssm_a_im, ssm_log_dt, ssm_b_re, ssm_b_im, ssm_c_re, ssm_c_im, ssm_d, w_glu, b_glu, w_out, norm_post_g, loss_target, m_norm_pre_g, m_w_in, m_conv_w, m_conv_b, m_ssm_a_re, m_ssm_a_im, m_ssm_log_dt, m_ssm_b_re, m_ssm_b_im, m_ssm_c_re, m_ssm_c_im, m_ssm_d, m_w_glu, m_b_glu, m_w_out, m_norm_post_g, v_norm_pre_g, v_w_in, v_conv_w, v_conv_b, v_ssm_a_re, v_ssm_a_im, v_ssm_log_dt, v_ssm_b_re, v_ssm_b_im, v_ssm_c_re, v_ssm_c_im, v_ssm_d, v_w_glu, v_b_glu, v_w_out, v_norm_post_g):
    seq, d_model = x.shape[1], x.shape[2]
    d_conv, d_ssm = conv_b.shape[0], ssm_d.shape[0]
    groups, states = ssm_a_re.shape
    assert x.shape[0] == 1 and seq % (8 * N_CHUNK) == 0 and d_conv == d_ssm
    assert (groups, states) == (d_ssm // SSM_GROUP, SSM_STATE) and d_ssm % LANES == 0
    me = 4 * lax.axis_index("x") + 2 * lax.axis_index("y") + lax.axis_index("c")
    tm = min(512, seq)

    x3 = x[0].reshape(N_CHUNK, seq // N_CHUNK, d_model)
    t3 = loss_target[0].reshape(N_CHUNK, seq // N_CHUNK, d_model)
    row = lambda a: a.reshape(1, -1)
    conv_w8 = jnp.pad(conv_w, ((0, 8 - conv_w.shape[0]), (0, 0)))

    g3 = lambda a: a.reshape(groups, 1, -1)
    bt_re, bt_im = jnp.transpose(ssm_b_re, (0, 2, 1)), jnp.transpose(ssm_b_im, (0, 2, 1))
    lbr, lbi, qr, qi, bbt_r, bbt_i = _ssm_prep(g3(ssm_a_re), g3(ssm_a_im), g3(ssm_log_dt), bt_re, bt_im)
    n_half = groups // HALF_G
    lam = jnp.stack([lbr.reshape(n_half, HALF_W), lbi.reshape(n_half, HALF_W)], axis=1)
    bbcat = _bb_blockdiag(bbt_r, bbt_i).astype(MXU_DTYPE)
    cccat = _cc_blockdiag(ssm_c_re, ssm_c_im).astype(MXU_DTYPE)

    xp, h, ht = _norm_in(x3, row(norm_pre_g), tm)
    nc = w_in.shape[1]
    w_pieces = jnp.transpose(w_in.astype(MXU_DTYPE).reshape(d_model, nc // PIECE_COLS, PIECE_COLS), (1, 0, 2))
    proj, win_g, (convw_g, wglu_g) = _fwd_in(h, w_pieces, me, _Comm([conv_w8, w_glu.astype(MXU_DTYPE)]),
                                             min(2048, seq))
    conv_w_full = jnp.transpose(convw_g, (1, 0, 2)).reshape(8, d_conv)
    u_col0, zs_col0 = 4 * d_conv, 4 * d_conv + d_ssm
    yconv = _conv_fwd(proj, conv_w_full, row(conv_b), d_conv)
    (yp,), (wout_g,) = _ssm_fwd(proj, lam, bbcat, cccat, row(ssm_d), d_ssm, u_col0,
                                _Comm([w_out.astype(MXU_DTYPE)]))
    w_out_full = wout_g.reshape(-1, d_model)
    w_glu_full = wglu_g.reshape(-1, d_ssm)
    (dy, d_o, mixt, dyc, dyp, dzs, ygt, dq, loss_part, dgpost, dbglu) = _tail(
        xp, t3, proj, yconv, yp, w_glu_full, row(b_glu), w_out_full, row(norm_post_g), zs_col0, min(256, seq))

    r_out, r_glu, nc = w_out.shape[0], w_glu.shape[0], w_in.shape[1]
    (dwout_p,), _ = _wgrad("dw_out", mixt, d_o, r_out, min(1024, d_model), (N_DEV, r_out, d_model),
                           (None, r_out, min(1024, d_model)), lambda i, j: (i, 0, j))
    (dwglu_p,), _ = _wgrad("dw_glu", ygt, dq, r_glu, d_ssm, (N_DEV, r_glu, d_ssm),
                           (None, r_glu, d_ssm), lambda i, j: (i, 0, 0))
    (d4, dconvb, dconvw), _ = _conv_bwd(proj, dyc, conv_w_full, row(conv_b), d_conv)
    late = [k for k in range(N_DEV) if k * nc < u_col0 + d_ssm and (k + 1) * nc > u_col0]
    early = [k for k in range(N_DEV) if k not in late]
    gr = math.gcd(nc, d_conv)
    granules = lambda blocks: [g for k in blocks for g in range(k * nc // gr, (k + 1) * nc // gr)]
    tmw = min(1024, d_model)
    (dwin_e,), (recv_out,) = _dw_in("dw_in_early", ht, d4, None, dzs, granules(early), gr, nc, tmw,
                                    _Comm([], [dwout_p]))
    (du, dbb, dcc, da, dd), (recv_in, recv_glu) = _ssm_bwd(
        proj, dyp, lam, bbcat, cccat, row(ssm_d), d_ssm, u_col0, _Comm([], [dwin_e, dwglu_p], dests={0: early}))
    parts_mx = [_bb_diag(dbb), _cc_diag(dcc)]
    (dwin_l,), (pack_mx_g,) = _dw_in("dw_in_late", ht, d4, du, dzs, granules(late), gr, nc, tmw,
                                     _Comm([_pack(parts_mx, MXU_DTYPE)]))
    da_n = jnp.transpose(da.reshape(n_half, 2, HALF_G, SSM_STATE), (1, 0, 2, 3)).reshape(2, groups, 1, states)
    parts = [dgpost, dconvb, dd, dbglu, dconvw[:3], da_n, loss_part]
    shapes, shapes_mx = [p.shape for p in parts], [p.shape for p in parts_mx]
    (gx_p, dgpre), (pack_g, recv_in) = _bwd_in(
        d4, du, dzs, gr, win_g, xp, dy, row(norm_pre_g),
        _Comm([_pack(parts)], [dwin_l], dests={1: late}, into={1: recv_in}), min(256, seq))
    (last_g,) = _exchange("reduce_last", [_pack([dgpre])], [])
    (g_gpost, g_convb, g_d, g_bglu, g_convw, g_da, loss_sum) = _unpack(_sum_slots("sum_pack", pack_g), shapes)
    (g_dbb, g_dcc) = _unpack(_sum_slots("sum_pack_mx", pack_mx_g), shapes_mx)
    (g_gpre,) = _unpack(_sum_slots("sum_last", last_g), [dgpre.shape])
    g_convw = lax.dynamic_slice(g_convw, (0, me * conv_w.shape[1]), conv_w.shape)

    tr = lambda a: jnp.transpose(a, (0, 2, 1))
    direct = [(g_gpre, row(norm_pre_g), row(m_norm_pre_g), row(v_norm_pre_g)),
              (g_convb, row(conv_b), row(m_conv_b), row(v_conv_b)),
              (g_d, row(ssm_d), row(m_ssm_d), row(v_ssm_d)),
              (g_bglu, row(b_glu), row(m_b_glu), row(v_b_glu)),
              (g_gpost, row(norm_post_g), row(m_norm_post_g), row(v_norm_post_g)),
              (g_convw, conv_w, m_conv_w, v_conv_w),
              (g_dcc[0], ssm_c_re, m_ssm_c_re, v_ssm_c_re),
              (-g_dcc[1], ssm_c_im, m_ssm_c_im, v_ssm_c_im)]
    ssm = dict(da_r=g_da[0], da_i=g_da[1], dbb_r=g_dbb[0], dbb_i=g_dbb[1], lr=g3(ssm_a_re), li=g3(ssm_a_im),
               ldt=g3(ssm_log_dt), bt_r=bt_re, bt_i=bt_im, lbr=lbr, lbi=lbi, qr=qr, qi=qi,
               w_a_re=g3(ssm_a_re), m_a_re=g3(m_ssm_a_re), v_a_re=g3(v_ssm_a_re),
               w_a_im=g3(ssm_a_im), m_a_im=g3(m_ssm_a_im), v_a_im=g3(v_ssm_a_im),
               w_log_dt=g3(ssm_log_dt), m_log_dt=g3(m_ssm_log_dt), v_log_dt=g3(v_ssm_log_dt),
               w_bt_re=bt_re, m_bt_re=tr(m_ssm_b_re), v_bt_re=tr(v_ssm_b_re),
               w_bt_im=bt_im, m_bt_im=tr(m_ssm_b_im), v_bt_im=tr(v_ssm_b_im))
    small = _small_update(direct, ssm)
    res = {}
    for name, quad, shape in zip(["norm_pre_g", "conv_b", "ssm_d", "b_glu", "norm_post_g", "conv_w", "ssm_c_re", "ssm_c_im"],
                                 small[:8], [norm_pre_g.shape, conv_b.shape, ssm_d.shape, b_glu.shape,
                                             norm_post_g.shape, conv_w.shape, ssm_c_re.shape, ssm_c_im.shape]):
        res[name] = tuple(a.reshape(shape) for a in quad)
    res["ssm_a_re"] = tuple(a.reshape(ssm_a_re.shape) for a in small[8])
    res["ssm_a_im"] = tuple(a.reshape(ssm_a_im.shape) for a in small[9])
    res["ssm_log_dt"] = tuple(a.reshape(ssm_log_dt.shape) for a in small[10])
    res["ssm_b_re"] = tuple(tr(a) for a in small[11])
    res["ssm_b_im"] = tuple(tr(a) for a in small[12])
    res["w_in"] = tuple(_adam_big("adam_w_in", recv_in, w_in, m_w_in, v_w_in, min(256, d_model)))
    res["w_out"] = tuple(_adam_big("adam_w_out", recv_out, w_out, m_w_out, v_w_out, min(128, r_out)))
    res["w_glu"] = tuple(_adam_big("adam_w_glu", recv_glu, w_glu, m_w_glu, v_w_glu, r_glu))

    order = ["norm_pre_g", "w_in", "conv_w", "conv_b", "ssm_a_re", "ssm_a_im", "ssm_log_dt", "ssm_b_re", "ssm_b_im",
             "ssm_c_re", "ssm_c_im", "ssm_d", "w_glu", "b_glu", "w_out", "norm_post_g"]
    loss = loss_sum[0, 0]
    grad_x = _unpermute_rows(gx_p)[None]
    return (loss, grad_x, *[res[n][0] for n in order], *[res[n][1] for n in order],
            *[res[n][2] for n in order], *[res[n][3] for n in order])
```

```python
import functools
import math

import jax
import jax.numpy as jnp
from jax import lax
from jax.experimental import pallas as pl
from jax.experimental.pallas import tpu as pltpu

F32 = jnp.float32
MXU_DTYPE = jnp.bfloat16
PROJ_DTYPE = jnp.bfloat16
AXES = ("x", "y", "c")
N_DEV = 8
N_CHUNK = 8
LANES = 128
SSM_GROUP = 16
SSM_STATE = 64
HALF_CH = 64
HALF_G = HALF_CH // SSM_GROUP
HALF_W = HALF_G * SSM_STATE
EPS = 1e-6
ADAM_LR, ADAM_B1, ADAM_B2, ADAM_EPS, ADAM_WD, ADAM_STEP = 0.001, 0.9, 0.999, 1e-08, 0.01, 10
GELU_C = math.sqrt(2.0 / math.pi)
GELU_K = 0.044715
VMEM_LIMIT = 56 * 1024 * 1024


def _params(sem=None):
    return pltpu.CompilerParams(dimension_semantics=sem, vmem_limit_bytes=VMEM_LIMIT)


def _dot(a, b):
    return jnp.dot(a, b, preferred_element_type=F32)


def _dot_nt(a, b):
    return lax.dot_general(a, b, (((1,), (1,)), ((), ())), preferred_element_type=F32)


def _dot_tn(a, b):
    return lax.dot_general(a, b, (((0,), (0,)), ((), ())), preferred_element_type=F32)


def _sigmoid(z):
    return 1.0 / (1.0 + jnp.exp(-z))


def _flip(v, bit):
    return 1 - v if bit else v


def _peers():
    x, y, c = (lax.axis_index(a) for a in AXES)
    out = []
    for m in range(1, N_DEV):
        px, py, pc = _flip(x, (m >> 2) & 1), _flip(y, (m >> 1) & 1), _flip(c, m & 1)
        out.append((px, py, pc, 4 * px + 2 * py + pc))
    return out


class _Comm:
    def __init__(self, gathers=(), scatters=(), dests=None, into=None):
        self.n_g = len(gathers)
        self.operands = list(gathers) + list(scatters)
        self.n = len(self.operands)
        self.dests = dests or {}
        self.into = into or {}

    def out_shape(self):
        return [jax.ShapeDtypeStruct((N_DEV,) + a.shape if t < self.n_g else a.shape, a.dtype)
                for t, a in enumerate(self.operands)]

    def scratch(self):
        if not self.n:
            return []
        return [pltpu.SemaphoreType.DMA((self.n, N_DEV - 1)), pltpu.SemaphoreType.DMA((self.n, N_DEV - 1)),
                pltpu.SemaphoreType.DMA((self.n,))]

    def _copies(self, in_refs, out_refs, sems, arrivals):
        send_sems, recv_sems, local_sems = sems
        x, y, c = (lax.axis_index(a) for a in AXES)
        me = 4 * x + 2 * y + c

        def src(t, dev):
            return in_refs[t] if t < self.n_g else in_refs[t].at[dev]

        def member(t, dev):
            if t not in self.dests:
                return None
            return functools.reduce(jnp.logical_or, [dev == d for d in self.dests[t]])

        local = [(member(t, me), pltpu.make_async_copy(src(t, me), out_refs[t].at[me], local_sems.at[t]))
                 for t in range(self.n)]
        sends, recvs = [], []
        for t in range(self.n):
            for m, (px, py, pc, peer) in enumerate(_peers()):
                kw = dict(send_sem=send_sems.at[t, m], recv_sem=recv_sems.at[t, m],
                          device_id=(px, py, pc), device_id_type=pl.DeviceIdType.MESH)
                sends.append((member(t, peer), pltpu.make_async_remote_copy(
                    src_ref=src(t, peer), dst_ref=out_refs[t].at[me], **kw)))
                if arrivals:
                    recvs.append((member(t, me), pltpu.make_async_remote_copy(
                        src_ref=src(t, peer), dst_ref=out_refs[t].at[peer], **kw)))
        return local, sends, recvs

    @staticmethod
    def _do(cond, action):
        if cond is None:
            action()
        else:
            pl.when(cond)(action)

    def start(self, in_refs, out_refs, sems):
        local, sends, _ = self._copies(in_refs, out_refs, sems, arrivals=False)
        for cond, cp in local + sends:
            self._do(cond, cp.start)

    def finish(self, in_refs, out_refs, sems):
        local, sends, recvs = self._copies(in_refs, out_refs, sems, arrivals=True)
        for cond, cp in recvs:
            self._do(cond, cp.wait_recv)
        for cond, cp in sends:
            self._do(cond, cp.wait_send)
        for cond, cp in local:
            self._do(cond, cp.wait)


def _call(body, comm, *, name, grid, in_specs, out_specs, out_shape, operands, scratch_shapes=()):
    comm = comm or _Comm()
    n_in, n_out, n_scr, cn = len(in_specs), len(out_specs), len(scratch_shapes), comm.n
    landing = sorted(comm.into)
    aliases = {n_in + cn + q: n_out + t for q, t in enumerate(landing)}

    def wrapped(*refs):
        parts, o = [], 0
        for k in (n_in, cn, len(landing), n_out, cn, n_scr):
            parts.append(refs[o:o + k])
            o += k
        h_in, c_in, _, h_out, c_out, h_scr = parts
        sems = refs[o:]
        if cn:
            first = functools.reduce(jnp.logical_and, [pl.program_id(d) == 0 for d in range(len(grid))])

            @pl.when(first)
            def _():
                comm.start(c_in, c_out, sems)

        body(*h_in, *h_out, *h_scr)
        if cn:
            last = functools.reduce(jnp.logical_and, [pl.program_id(d) == grid[d] - 1 for d in range(len(grid))])

            @pl.when(last)
            def _():
                comm.finish(c_in, c_out, sems)

    any_ = pl.BlockSpec(memory_space=pl.ANY)
    res = pl.pallas_call(
        wrapped, name=name, grid=grid, in_specs=list(in_specs) + [any_] * (cn + len(landing)),
        out_specs=list(out_specs) + [any_] * cn,
        out_shape=list(out_shape) + comm.out_shape(), scratch_shapes=list(scratch_shapes) + comm.scratch(),
        input_output_aliases=aliases, compiler_params=_params(("arbitrary",) * len(grid)),
    )(*operands, *comm.operands, *[comm.into[t] for t in landing])
    return list(res[:n_out]), list(res[n_out:])


def _exchange(name, gathers, scatters):
    def body(tok_ref):
        tok_ref[...] = jnp.zeros_like(tok_ref)

    return _call(body, _Comm(gathers, scatters), name=name, grid=(1,), in_specs=[],
                 out_specs=[pl.BlockSpec((8, LANES), lambda i: (0, 0))],
                 out_shape=[jax.ShapeDtypeStruct((8, LANES), F32)], operands=[])[1]


def _ssm_prep(a_re, a_im, log_dt, bt_re, bt_im):
    def body(lr_ref, li_ref, ldt_ref, br_ref, bi_ref, lbr_ref, lbi_ref, qr_ref, qi_ref, bbr_ref, bbi_ref):
        lr, li = lr_ref[...], li_ref[...]
        dt = jnp.exp(ldt_ref[...])
        mag = jnp.exp(lr * dt)
        lbr, lbi = mag * jnp.cos(li * dt), mag * jnp.sin(li * dt)
        nr, ni = lbr - 1.0, lbi
        den = lr * lr + li * li
        qr = (nr * lr + ni * li) / den
        qi = (ni * lr - nr * li) / den
        br, bi = br_ref[...], bi_ref[...]
        lbr_ref[...], lbi_ref[...], qr_ref[...], qi_ref[...] = lbr, lbi, qr, qi
        bbr_ref[...] = qr * br - qi * bi
        bbi_ref[...] = qr * bi + qi * br

    s2 = jax.ShapeDtypeStruct(a_re.shape, F32)
    s3 = jax.ShapeDtypeStruct(bt_re.shape, F32)
    return pl.pallas_call(body, name="ssm_prep", out_shape=[s2, s2, s2, s2, s3, s3],
                          compiler_params=_params())(a_re, a_im, log_dt, bt_re, bt_im)


def _adam(w, g, m, v):
    m2 = ADAM_B1 * m + (1.0 - ADAM_B1) * g
    v2 = ADAM_B2 * v + (1.0 - ADAM_B2) * (g * g)
    m_hat = m2 / (1.0 - ADAM_B1 ** ADAM_STEP)
    v_hat = v2 / (1.0 - ADAM_B2 ** ADAM_STEP)
    delta = -ADAM_LR * (m_hat / (jnp.sqrt(v_hat) + ADAM_EPS) + ADAM_WD * w)
    return delta, m2, v2


def _small_update(direct, ssm):
    n_direct = len(direct)
    flat = [a for quad in direct for a in quad]
    names = ["da_r", "da_i", "dbb_r", "dbb_i", "lr", "li", "ldt", "bt_r", "bt_i", "lbr", "lbi", "qr", "qi"]
    flat += [ssm[k] for k in names]
    chain = ["a_re", "a_im", "log_dt", "bt_re", "bt_im"]
    for k in chain:
        flat += [ssm["w_" + k], ssm["m_" + k], ssm["v_" + k]]
    n_in = len(flat)

    def body(*refs):
        ins, outs = refs[:n_in], refs[n_in:]
        for p in range(n_direct):
            g, w, m, v = (r[...] for r in ins[4 * p:4 * p + 4])
            d, m2, v2 = _adam(w, g, m, v)
            outs[4 * p][...], outs[4 * p + 1][...], outs[4 * p + 2][...], outs[4 * p + 3][...] = g, d, m2, v2
        o = 4 * n_direct
        da_r, da_i, dbb_r, dbb_i, lr, li, ldt, bt_r, bt_i, lbr, lbi, qr, qi = (r[...] for r in ins[o:o + 13])
        dt = jnp.exp(ldt)
        g_br = qr * dbb_r + qi * dbb_i
        g_bi = qr * dbb_i - qi * dbb_r
        dq_r = jnp.sum(bt_r * dbb_r + bt_i * dbb_i, axis=1, keepdims=True)
        dq_i = jnp.sum(bt_r * dbb_i - bt_i * dbb_r, axis=1, keepdims=True)
        den = lr * lr + li * li
        cr, ci = lr / den, li / den
        gl_r = da_r + (cr * dq_r - ci * dq_i)
        gl_i = da_i + (cr * dq_i + ci * dq_r)
        w_r = qr * cr + qi * ci
        w_i = qi * cr - qr * ci
        g_lr = dt * (lbr * gl_r + lbi * gl_i) + (-w_r * dq_r - w_i * dq_i)
        g_li = dt * (lbr * gl_i - lbi * gl_r) + (-w_r * dq_i + w_i * dq_r)
        m_r = lr * lbr - li * lbi
        m_i = lr * lbi + li * lbr
        g_ldt = jnp.sum(m_r * gl_r + m_i * gl_i, axis=2, keepdims=True) * dt
        grads = [g_lr, g_li, g_ldt, g_br, g_bi]
        base_in, base_out = o + 13, 4 * n_direct
        for p, g in enumerate(grads):
            w, m, v = (r[...] for r in ins[base_in + 3 * p:base_in + 3 * p + 3])
            d, m2, v2 = _adam(w, g, m, v)
            q = base_out + 4 * p
            outs[q][...], outs[q + 1][...], outs[q + 2][...], outs[q + 3][...] = g, d, m2, v2

    out_shape = []
    for quad in direct:
        out_shape += [jax.ShapeDtypeStruct(quad[1].shape, F32)] * 4
    for k in chain:
        out_shape += [jax.ShapeDtypeStruct(ssm["w_" + k].shape, F32)] * 4
    res = pl.pallas_call(body, name="small_update", out_shape=out_shape, compiler_params=_params())(*flat)
    return [tuple(res[4 * p:4 * p + 4]) for p in range(n_direct + len(chain))]


def _sum_slots(name, pack):
    def body(p_ref, o_ref):
        acc = p_ref[0].astype(F32)
        for k in range(1, N_DEV):
            acc = acc + p_ref[k].astype(F32)
        o_ref[...] = acc

    return pl.pallas_call(body, name=name, out_shape=jax.ShapeDtypeStruct(pack.shape[1:], F32),
                          compiler_params=_params())(pack)


def _adam_big(name, recv, w, m, v, tr):
    _, rows, cols = recv.shape

    def body(r_ref, w_ref, m_ref, v_ref, g_ref, d_ref, m2_ref, v2_ref):
        g = r_ref[0].astype(F32)
        for k in range(1, N_DEV):
            g = g + r_ref[k].astype(F32)
        d, m2, v2 = _adam(w_ref[...], g, m_ref[...], v_ref[...])
        g_ref[...], d_ref[...], m2_ref[...], v2_ref[...] = g, d, m2, v2

    blk = pl.BlockSpec((tr, cols), lambda i: (i, 0))
    shp = jax.ShapeDtypeStruct((rows, cols), F32)
    return pl.pallas_call(
        body, name=name, grid=(rows // tr,),
        in_specs=[pl.BlockSpec((N_DEV, tr, cols), lambda i: (0, i, 0)), blk, blk, blk],
        out_specs=[blk] * 4, out_shape=[shp] * 4, compiler_params=_params(("parallel",)),
    )(recv, w, m, v)


def _chunk_block(tm, d):
    return pl.BlockSpec((N_CHUNK, tm // N_CHUNK, d), lambda i: (0, i, 0))


def _interleave(block):
    c, n, d = block.shape
    return pltpu.einshape("cjd->jcd", block).reshape(n * c, d)


def _norm_in(x3, g_pre, tm):
    _, steps, d_model = x3.shape
    seq = steps * N_CHUNK

    def body(x_ref, g_ref, xp_ref, h_ref, ht_ref):
        x = _interleave(x_ref[...])
        xp_ref[...] = x
        r = lax.rsqrt(jnp.mean(x * x, axis=-1, keepdims=True) + EPS)
        h = x * r * g_ref[...]
        h_ref[...] = h.astype(h_ref.dtype)
        ht_ref[...] = h.T.astype(ht_ref.dtype)

    rows = pl.BlockSpec((tm, d_model), lambda i: (i, 0))
    return pl.pallas_call(
        body, name="norm_in", grid=(seq // tm,),
        in_specs=[_chunk_block(tm, d_model), pl.BlockSpec((1, d_model), lambda i: (0, 0))],
        out_specs=[rows, rows, pl.BlockSpec((d_model, tm), lambda i: (0, i))],
        out_shape=[jax.ShapeDtypeStruct((seq, d_model), F32), jax.ShapeDtypeStruct((seq, d_model), MXU_DTYPE),
                   jax.ShapeDtypeStruct((d_model, seq), MXU_DTYPE)],
        compiler_params=_params(("parallel",)),
    )(x3, g_pre)


GATHER_ORDER = (0, 1, 4, 2, 6, 5, 3, 7)
PIECE_COLS = 256


def _fwd_in(h, w_pieces, me, comm, tm):
    seq, d_model = h.shape
    n_p, _, gw = w_pieces.shape
    n_i = seq // tm
    cn = comm.n
    consume = [(rel, r) for rel in (0, 1) for r in range(n_p)]
    consume += [(rel, r) for r in range(n_p) for rel in (4, 2, 5, 3)]
    consume += [(rel, r) for r in range(n_p) for rel in (6, 7)]
    n_q = len(consume)
    order = jnp.stack([jnp.bitwise_xor(me, rel) * n_p + r for rel, r in consume]).astype(jnp.int32)

    def body(order_ref, h_hbm, w_hbm, *rest):
        c_in, rest = rest[:cn], rest[cn:]
        proj_hbm, wing = rest[0], rest[1]
        c_out, rest = rest[2:2 + cn], rest[2 + cn:]
        wbuf, send_sems, recv_sems, own_sems, load_sems, h_all, h_sems, out_buf, out_sems = rest[:9]
        c_sems = rest[9:]
        h_loads = [pltpu.make_async_copy(h_hbm.at[pl.ds(t * tm, tm)], h_all.at[pl.ds(t * tm, tm)], h_sems.at[t])
                   for t in range(n_i)]
        x, y, c = (lax.axis_index(a) for a in AXES)
        me_ = 4 * x + 2 * y + c

        def dev(rel):
            return _flip(x, (rel >> 2) & 1), _flip(y, (rel >> 1) & 1), _flip(c, rel & 1)

        def slot(rel):
            px, py, pc = dev(rel)
            return 4 * px + 2 * py + pc

        def remote(src, block, r, to_rel, sem):
            return pltpu.make_async_remote_copy(
                src_ref=src, dst_ref=wing.at[block, r], send_sem=send_sems.at[sem, r], recv_sem=recv_sems.at[sem, r],
                device_id=dev(to_rel), device_id_type=pl.DeviceIdType.MESH)

        pieces = range(n_p)
        own = [pltpu.make_async_copy(w_hbm.at[r], wing.at[me_, r], own_sems.at[r]) for r in pieces]
        first_hand = {p: [remote(w_hbm.at[r], me_, r, GATHER_ORDER[p], p - 1) for r in pieces] for p in (1, 2, 3)}
        relay = {2: (c == 0, [remote(wing.at[slot(4), r], slot(4), r, 2, 3) for r in pieces]),
                 3: (c == 1, [remote(wing.at[slot(2), r], slot(2), r, 4, 3) for r in pieces])}
        passed_on = {p: [remote(wing.at[slot(GATHER_ORDER[p]), r], slot(GATHER_ORDER[p]), r, 1, p + 2) for r in pieces]
                     for p in (2, 3, 4)}

        def load(q):
            rel, r = consume[q]
            return pltpu.make_async_copy(wing.at[slot(rel), r], wbuf.at[q % 2], load_sems.at[q % 2])

        def take(q):
            rel, r = consume[q]
            p = GATHER_ORDER.index(rel)
            if p == 0:
                own[r].wait()
            else:
                remote(w_hbm.at[r], slot(rel), r, rel, p - 1).wait_recv()
            if p in relay:
                pl.when(relay[p][0])(relay[p][1][r].start)
            if p in passed_on:
                passed_on[p][r].start()
            load(q).start()

        for r in pieces:
            own[r].start()
        for r in pieces:
            for p in (1, 2, 3):
                first_hand[p][r].start()
        for cp in h_loads:
            cp.start()
        comm.start(c_in, c_out, c_sems)
        take(0)

        def store(s, k, i):
            col = pl.multiple_of(order_ref[k] * gw, gw)
            return pltpu.make_async_copy(
                out_buf.at[s % 2], proj_hbm.at[pl.ds(pl.multiple_of(i * tm, tm), tm), pl.ds(col, gw)],
                out_sems.at[s % 2])

        def step(s, carry):
            k, i = s // n_i, s % n_i
            for t in range(n_i):
                pl.when((k == 0) & (i == t))(h_loads[t].wait)
            for q in range(n_q):
                @pl.when((k == q) & (i == 0))
                def _():
                    load(q).wait()

                if q + 1 < n_q:
                    @pl.when((k == q) & (i == n_i - 1))
                    def _():
                        take(q + 1)

            @pl.when(s >= 2)
            def _():
                store(s, k, i).wait()

            out_buf[s % 2] = _dot(h_all[pl.ds(pl.multiple_of(i * tm, tm), tm), :], wbuf[k % 2]).astype(out_buf.dtype)
            store(s, k, i).start()
            return carry

        n_s = n_q * n_i
        lax.fori_loop(0, n_s, step, 0)
        for s in range(max(n_s - 2, 0), n_s):
            store(s, s // n_i, s % n_i).wait()
        for p in first_hand:
            for cp in first_hand[p]:
                cp.wait_send()
        for p in passed_on:
            for cp in passed_on[p]:
                cp.wait_send()
        for cond, cps in relay.values():
            for cp in cps:
                pl.when(cond)(cp.wait_send)
        comm.finish(c_in, c_out, c_sems)

    any_ = pl.BlockSpec(memory_space=pl.ANY)
    grid_spec = pltpu.PrefetchScalarGridSpec(
        num_scalar_prefetch=1, grid=(1,),
        in_specs=[any_, any_] + [any_] * cn,
        out_specs=[any_, any_] + [any_] * cn,
        scratch_shapes=[pltpu.VMEM((2, d_model, gw), w_pieces.dtype), pltpu.SemaphoreType.DMA((N_DEV - 1, n_p)),
                        pltpu.SemaphoreType.DMA((N_DEV - 1, n_p)), pltpu.SemaphoreType.DMA((n_p,)),
                        pltpu.SemaphoreType.DMA((2,)), pltpu.VMEM((seq, d_model), h.dtype),
                        pltpu.SemaphoreType.DMA((n_i,)), pltpu.VMEM((2, tm, gw), PROJ_DTYPE),
                        pltpu.SemaphoreType.DMA((2,))]
        + comm.scratch())
    res = pl.pallas_call(
        body, name="fwd_in", grid_spec=grid_spec,
        out_shape=[jax.ShapeDtypeStruct((seq, N_DEV * n_p * gw), PROJ_DTYPE),
                   jax.ShapeDtypeStruct((N_DEV, n_p, d_model, gw), w_pieces.dtype)] + comm.out_shape(),
        compiler_params=_params(("arbitrary",)),
    )(order, h, w_pieces, *comm.operands)
    return res[0], res[1], list(res[2:])


def _shift_prev(a):
    n = a.shape[0]
    last = a[n - N_CHUNK:, :]
    row = lax.broadcasted_iota(jnp.int32, last.shape, 0)
    wrap = jnp.where(row == 0, 0.0, pltpu.roll(last, 1, axis=0))
    return jnp.concatenate([wrap, a[:n - N_CHUNK, :]], axis=0)


def _shift_next(a):
    first = a[:N_CHUNK, :]
    row = lax.broadcasted_iota(jnp.int32, first.shape, 0)
    wrap = jnp.where(row == N_CHUNK - 1, 0.0, pltpu.roll(first, N_CHUNK - 1, axis=0))
    return jnp.concatenate([a[N_CHUNK:, :], wrap], axis=0)


def _conv_specs(seq, d_conv):
    nblk = d_conv // LANES
    return [pl.BlockSpec((seq, LANES), functools.partial(lambda i, o: (0, o + i), o=q * nblk)) for q in range(4)]


def _conv_fwd(proj, conv_w8, conv_b, d_conv):
    seq = proj.shape[0]

    def body(bg_ref, cg_ref, v_ref, zc_ref, w_ref, b_ref, y_ref):
        cv = cg_ref[...].astype(F32) * v_ref[...].astype(F32)
        s1 = _shift_prev(cv)
        s2 = _shift_prev(s1)
        conv = b_ref[...] + w_ref[0:1, :] * s2 + w_ref[1:2, :] * s1 + w_ref[2:3, :] * cv
        z = zc_ref[...].astype(F32)
        y_ref[...] = (bg_ref[...].astype(F32) * conv * (z * _sigmoid(z))).astype(y_ref.dtype)

    col = pl.BlockSpec((seq, LANES), lambda i: (0, i))
    return pl.pallas_call(
        body, name="conv_fwd", grid=(d_conv // LANES,),
        in_specs=_conv_specs(seq, d_conv) + [pl.BlockSpec((8, LANES), lambda i: (0, i)), pl.BlockSpec((1, LANES), lambda i: (0, i))],
        out_specs=col, out_shape=jax.ShapeDtypeStruct((seq, d_conv), MXU_DTYPE),
        compiler_params=_params(("parallel",)),
    )(proj, proj, proj, proj, conv_w8, conv_b)


def _conv_bwd(proj, dyc, conv_w8, conv_b, d_conv, comm=None):
    seq = proj.shape[0]

    def body(bg_ref, cg_ref, v_ref, zc_ref, dy_ref, w_ref, b_ref, d4_ref, dcb_ref, dcw_ref):
        bg, cg, v, z = (r[...].astype(F32) for r in (bg_ref, cg_ref, v_ref, zc_ref))
        w0, w1, w2 = w_ref[0:1, :], w_ref[1:2, :], w_ref[2:3, :]
        cv = cg * v
        s1 = _shift_prev(cv)
        s2 = _shift_prev(s1)
        conv = b_ref[...] + w0 * s2 + w1 * s1 + w2 * cv
        sig = _sigmoid(z)
        dy = dy_ref[...].astype(F32)
        g1 = dy * (z * sig)
        d_conv_ = g1 * bg
        d4_ref[0] = (g1 * conv).astype(d4_ref.dtype)
        d4_ref[3] = (dy * bg * conv * (sig * (1.0 + z * (1.0 - sig)))).astype(d4_ref.dtype)
        n1 = _shift_next(d_conv_)
        n2 = _shift_next(n1)
        d_cv = w2 * d_conv_ + w1 * n1 + w0 * n2
        d4_ref[1] = (d_cv * v).astype(d4_ref.dtype)
        d4_ref[2] = (d_cv * cg).astype(d4_ref.dtype)
        dcb_ref[...] = jnp.sum(d_conv_, axis=0, keepdims=True)
        rows = [jnp.sum(d_conv_ * s, axis=0, keepdims=True) for s in (s2, s1, cv)]
        dcw_ref[...] = jnp.concatenate(rows + [jnp.zeros((5, LANES), F32)], axis=0)

    col = pl.BlockSpec((seq, LANES), lambda i: (0, i))
    return _call(
        body, comm, name="conv_bwd", grid=(d_conv // LANES,),
        in_specs=_conv_specs(seq, d_conv) + [col, pl.BlockSpec((8, LANES), lambda i: (0, i)), pl.BlockSpec((1, LANES), lambda i: (0, i))],
        out_specs=[pl.BlockSpec((4, seq, LANES), lambda i: (0, 0, i)), pl.BlockSpec((1, LANES), lambda i: (0, i)),
                   pl.BlockSpec((8, LANES), lambda i: (0, i))],
        out_shape=[jax.ShapeDtypeStruct((4, seq, d_conv), MXU_DTYPE), jax.ShapeDtypeStruct((1, d_conv), F32),
                   jax.ShapeDtypeStruct((8, d_conv), F32)],
        operands=[proj, proj, proj, proj, dyc, conv_w8, conv_b])


def _cmul(ar, ai, br, bi):
    return ar * br - ai * bi, ar * bi + ai * br


def _down(v, k):
    row = lax.broadcasted_iota(jnp.int32, v.shape, 0)
    return jnp.where(row >= k, pltpu.roll(v, k, axis=0), 0.0)


def _up(v, k):
    row = lax.broadcasted_iota(jnp.int32, v.shape, 0)
    return jnp.where(row < N_CHUNK - k, pltpu.roll(v, N_CHUNK - k, axis=0), 0.0)


def _chunk_carry(fr, fi, mr, mi, shift):
    vr, vi = shift(fr, 1), shift(fi, 1)
    for k in (1, 2, 4):
        pr, pi = _cmul(mr, mi, shift(vr, k), shift(vi, k))
        vr, vi = vr + pr, vi + pi
        mr, mi = _cmul(mr, mi, mr, mi)
    return vr, vi


def _tile(ref, j, width, part):
    return ref.at[pl.ds(pl.multiple_of(j * N_CHUNK, N_CHUNK), N_CHUNK), pl.ds(part * width, width)]


def _row(t, k):
    return jnp.broadcast_to(t[k:k + 1, :], t.shape)


def _power_table(tab_ref, ar, ai, steps, width):
    e = lax.broadcasted_iota(jnp.int32, ar.shape, 0) + 1
    rr, ri = jnp.ones_like(ar), jnp.zeros_like(ai)
    br, bi = ar, ai
    for bit in range(4):
        mr, mi = _cmul(rr, ri, br, bi)
        take = ((e >> bit) & 1) == 1
        rr, ri = jnp.where(take, mr, rr), jnp.where(take, mi, ri)
        if bit < 3:
            br, bi = _cmul(br, bi, br, bi)
    _tile(tab_ref, 0, width, 0)[...] = rr
    _tile(tab_ref, 0, width, 1)[...] = ri

    def step(m, carry):
        tr, ti = _cmul(carry[0], carry[1], br, bi)
        _tile(tab_ref, m, width, 0)[...] = tr
        _tile(tab_ref, m, width, 1)[...] = ti
        return tr, ti

    lax.fori_loop(1, steps // N_CHUNK, step, (rr, ri))


def _last_power(tab_ref, steps, width):
    shape = (N_CHUNK, width)
    return (jnp.broadcast_to(tab_ref[steps - 1:steps, 0:width], shape),
            jnp.broadcast_to(tab_ref[steps - 1:steps, width:2 * width], shape))


def _scan_fwd(s_ref, ar, ai, steps, width):
    def step(j, carry):
        sr, si = carry
        nr = ar * sr - ai * si + _tile(s_ref, j, width, 0)[...]
        ni = ar * si + ai * sr + _tile(s_ref, j, width, 1)[...]
        _tile(s_ref, j, width, 0)[...] = nr
        _tile(s_ref, j, width, 1)[...] = ni
        return nr, ni

    z = jnp.zeros((N_CHUNK, width), F32)
    return lax.fori_loop(0, steps, step, (z, z), unroll=4)


def _scan_both(s_ref, g_ref, ar, ai, steps, width):
    def step(q, carry):
        sr, si, gr, gi = carry
        j, jb = q, steps - 1 - q
        nsr = ar * sr - ai * si + _tile(s_ref, j, width, 0)[...]
        nsi = ar * si + ai * sr + _tile(s_ref, j, width, 1)[...]
        ngr = ar * gr + ai * gi + _tile(g_ref, jb, width, 0)[...]
        ngi = ar * gi - ai * gr + _tile(g_ref, jb, width, 1)[...]
        _tile(s_ref, j, width, 0)[...] = nsr
        _tile(s_ref, j, width, 1)[...] = nsi
        _tile(g_ref, jb, width, 0)[...] = ngr
        _tile(g_ref, jb, width, 1)[...] = ngi
        return nsr, nsi, ngr, ngi

    z = jnp.zeros((N_CHUNK, width), F32)
    return lax.fori_loop(0, steps, step, (z, z, z, z), unroll=2)


def _patch_fwd(s_ref, tab_ref, cr, ci, steps, width):
    def tile(m, _):
        tr, ti = _tile(tab_ref, m, width, 0)[...], _tile(tab_ref, m, width, 1)[...]
        for k in range(N_CHUNK):
            fr, fi = _cmul(_row(tr, k), _row(ti, k), cr, ci)
            j = m * N_CHUNK + k
            _tile(s_ref, j, width, 0)[...] += fr
            _tile(s_ref, j, width, 1)[...] += fi
        return 0

    lax.fori_loop(0, steps // N_CHUNK, tile, 0)


def _lam_rows(lam_ref, hh, width):
    return (jnp.broadcast_to(lam_ref[hh, 0:1, :], (N_CHUNK, width)),
            jnp.broadcast_to(lam_ref[hh, 1:2, :], (N_CHUNK, width)))


def _ssm_specs(seq, col0):
    return dict(
        col=pl.BlockSpec((seq, LANES), lambda i: (0, col0 + i)),
        lam=pl.BlockSpec((2, 2, HALF_W), lambda i: (i, 0, 0)),
        bb=pl.BlockSpec((2, HALF_CH, 2 * HALF_W), lambda i: (i, 0, 0)),
        cc=pl.BlockSpec((2, 2 * HALF_W, HALF_CH), lambda i: (i, 0, 0)),
        vec=pl.BlockSpec((1, LANES), lambda i: (0, i)),
        out=pl.BlockSpec((seq, LANES), lambda i: (0, i)),
    )


def _ssm_fwd(proj, lam, bbcat, cccat, d_skip, d_ssm, u_col0, comm=None):
    seq = proj.shape[0]
    steps = seq // N_CHUNK

    def body(u_ref, lam_ref, bb_ref, cc_ref, d_ref, yp_ref, s_ref, tab_ref):
        for hh in range(2):
            lanes = slice(HALF_CH * hh, HALF_CH * (hh + 1))
            u_half = u_ref[:, lanes].astype(F32)
            ar, ai = _lam_rows(lam_ref, hh, HALF_W)
            _power_table(tab_ref, ar, ai, steps, HALF_W)
            s_ref[...] = _dot(u_half.astype(MXU_DTYPE), bb_ref[hh])
            fr, fi = _scan_fwd(s_ref, ar, ai, steps, HALF_W)
            pr, pi = _last_power(tab_ref, steps, HALF_W)
            cr, ci = _chunk_carry(fr, fi, pr, pi, _down)
            _patch_fwd(s_ref, tab_ref, cr, ci, steps, HALF_W)
            y = _dot(s_ref[...].astype(MXU_DTYPE), cc_ref[hh])
            yp_ref[:, lanes] = y + d_ref[:, lanes] * u_half

    sp = _ssm_specs(seq, u_col0 // LANES)
    return _call(
        body, comm, name="ssm_fwd", grid=(d_ssm // LANES,),
        in_specs=[sp["col"], sp["lam"], sp["bb"], sp["cc"], sp["vec"]], out_specs=[sp["out"]],
        out_shape=[jax.ShapeDtypeStruct((seq, d_ssm), F32)],
        scratch_shapes=[pltpu.VMEM((seq, 2 * HALF_W), F32), pltpu.VMEM((steps, 2 * HALF_W), F32)],
        operands=[proj, lam, bbcat, cccat, d_skip])


def _ssm_bwd(proj, dyp, lam, bbcat, cccat, d_skip, d_ssm, u_col0, comm=None):
    seq = proj.shape[0]
    steps = seq // N_CHUNK
    n_half = 2 * d_ssm // LANES
    width = HALF_W

    def body(u_ref, dyp_ref, lam_ref, bb_ref, cc_ref, d_ref, du_ref, dbb_ref, dcc_ref, da_ref, dd_ref,
             s_ref, g_ref, tab_ref):
        n_tiles = steps // N_CHUNK
        for hh in range(2):
            lanes = slice(HALF_CH * hh, HALF_CH * (hh + 1))
            u_half, dy_half = u_ref[:, lanes].astype(F32), dyp_ref[:, lanes].astype(F32)
            dy_mx = dy_half.astype(MXU_DTYPE)
            ar, ai = _lam_rows(lam_ref, hh, width)
            _power_table(tab_ref, ar, ai, steps, width)
            s_ref[...] = _dot(u_half.astype(MXU_DTYPE), bb_ref[hh])
            g_ref[...] = _dot_nt(dy_mx, cc_ref[hh])
            fr, fi, lr_, li_ = _scan_both(s_ref, g_ref, ar, ai, steps, width)
            pr, pi = _last_power(tab_ref, steps, width)
            cr, ci = _chunk_carry(fr, fi, pr, pi, _down)
            gr, gi = _chunk_carry(lr_, li_, pr, -pi, _up)

            def tile(m, carry):
                sr, si, accr, acci = carry
                t1r, t1i = _tile(tab_ref, m, width, 0)[...], _tile(tab_ref, m, width, 1)[...]
                mb = n_tiles - 1 - m
                t2r, t2i = _tile(tab_ref, mb, width, 0)[...], _tile(tab_ref, mb, width, 1)[...]
                for k in range(N_CHUNK):
                    j = m * N_CHUNK + k
                    xr, xi = _cmul(_row(t1r, k), _row(t1i, k), cr, ci)
                    nsr = _tile(s_ref, j, width, 0)[...] + xr
                    nsi = _tile(s_ref, j, width, 1)[...] + xi
                    _tile(s_ref, j, width, 0)[...] = nsr
                    _tile(s_ref, j, width, 1)[...] = nsi
                    qr, qi = _row(t2r, N_CHUNK - 1 - k), _row(t2i, N_CHUNK - 1 - k)
                    ngr = _tile(g_ref, j, width, 0)[...] + (qr * gr + qi * gi)
                    ngi = _tile(g_ref, j, width, 1)[...] + (qr * gi - qi * gr)
                    _tile(g_ref, j, width, 0)[...] = ngr
                    _tile(g_ref, j, width, 1)[...] = ngi
                    accr = accr + (sr * ngr + si * ngi)
                    acci = acci + (sr * ngi - si * ngr)
                    sr, si = nsr, nsi
                return sr, si, accr, acci

            z = jnp.zeros((N_CHUNK, width), F32)
            _, _, accr, acci = lax.fori_loop(0, n_tiles, tile, (cr, ci, z, z))
            da_ref[hh, :, 0:width] = jnp.sum(accr, axis=0, keepdims=True)
            da_ref[hh, :, width:2 * width] = jnp.sum(acci, axis=0, keepdims=True)

            g_mx = g_ref[...].astype(MXU_DTYPE)
            dcc_ref[hh] = _dot_tn(dy_mx, s_ref[...].astype(MXU_DTYPE)).T
            dbb_ref[hh] = _dot_tn(u_half.astype(MXU_DTYPE), g_mx)
            du = _dot_nt(g_mx, bb_ref[hh]) + d_ref[:, lanes] * dy_half
            du_ref[:, lanes] = du.astype(du_ref.dtype)
            dd_ref[:, lanes] = jnp.sum(dy_half * u_half, axis=0, keepdims=True)

    sp = _ssm_specs(seq, u_col0 // LANES)
    return _call(
        body, comm, name="ssm_bwd", grid=(d_ssm // LANES,),
        in_specs=[sp["col"], sp["out"], sp["lam"], sp["bb"], sp["cc"], sp["vec"]],
        out_specs=[sp["out"], sp["bb"], sp["cc"], pl.BlockSpec((2, 1, 2 * width), lambda i: (i, 0, 0)), sp["vec"]],
        out_shape=[jax.ShapeDtypeStruct((seq, d_ssm), MXU_DTYPE),
                   jax.ShapeDtypeStruct((n_half, HALF_CH, 2 * width), F32),
                   jax.ShapeDtypeStruct((n_half, 2 * width, HALF_CH), F32),
                   jax.ShapeDtypeStruct((n_half, 1, 2 * width), F32),
                   jax.ShapeDtypeStruct((1, d_ssm), F32)],
        scratch_shapes=[pltpu.VMEM((seq, 2 * width), F32), pltpu.VMEM((seq, 2 * width), F32),
                        pltpu.VMEM((steps, 2 * width), F32)],
        operands=[proj, dyp, lam, bbcat, cccat, d_skip])


def _tail(xp, t3, proj, yconv, yp, w_glu, b_glu, w_out, g_post, zs_col0, tm):
    seq, d_model = xp.shape
    d_conv, d_ssm = yconv.shape[1], yp.shape[1]
    d_mix = d_conv + d_ssm
    assert zs_col0 % d_ssm == 0

    def body(x_ref, t_ref, zs_ref, yc_ref, yp_ref, wglu_hbm, bglu_ref, wout_hbm, gpost_ref,
             dy_ref, do_ref, mixt_ref, dyc_ref, dyp_ref, dzs_ref, ygt_ref, dq_ref, loss_ref, dgpost_ref, dbglu_ref,
             wglu, wout):
        @pl.when(pl.program_id(0) == 0)
        def _():
            pltpu.sync_copy(wglu_hbm, wglu)
            pltpu.sync_copy(wout_hbm, wout)
            loss_ref[...] = jnp.zeros_like(loss_ref)
            dgpost_ref[...] = jnp.zeros_like(dgpost_ref)
            dbglu_ref[...] = jnp.zeros_like(dbglu_ref)

        a = yp_ref[...]
        th = jnp.tanh(GELU_C * (a + GELU_K * (a * a * a)))
        yg = a * (0.5 * (1.0 + th))
        dgelu = 0.5 * (1.0 + th) + 0.5 * a * (1.0 - th * th) * (GELU_C * (1.0 + 3.0 * GELU_K * a * a))
        yg_mx = yg.astype(MXU_DTYPE)
        sq = _sigmoid(_dot(yg_mx, wglu[...]) + bglu_ref[...])
        y2 = yg * sq
        zs = zs_ref[...].astype(F32)
        sz = _sigmoid(zs)
        silz = zs * sz
        yc, ys = yc_ref[...].astype(F32), y2 * silz
        mix = jnp.concatenate([yc, ys], axis=1).astype(MXU_DTYPE)
        mixt_ref[0:d_conv, :] = yc.T.astype(MXU_DTYPE)
        mixt_ref[d_conv:, :] = ys.T.astype(MXU_DTYPE)
        o = _dot(mix, wout[...])
        r2 = lax.rsqrt(jnp.mean(o * o, axis=-1, keepdims=True) + EPS)
        on = o * r2
        gpost = gpost_ref[...]
        err = (x_ref[...] + on * gpost) - _interleave(t_ref[...])
        loss_ref[...] += 0.5 * jnp.sum(jnp.mean(err * err, axis=-1, keepdims=True), axis=0, keepdims=True)
        dy = err * (1.0 / d_model)
        dy_ref[...] = dy
        dgpost_ref[...] += jnp.sum(dy * on, axis=0, keepdims=True)
        d_on = dy * gpost
        d_o = r2 * (d_on - on * jnp.mean(d_on * on, axis=-1, keepdims=True))
        do_mx = d_o.astype(MXU_DTYPE)
        do_ref[...] = do_mx
        d_mix_ = _dot_nt(do_mx, wout[...])
        dyc_ref[...] = d_mix_[:, :d_conv].astype(dyc_ref.dtype)
        d_yssm = d_mix_[:, d_conv:]
        d_y2 = d_yssm * silz
        dzs_ref[...] = (d_yssm * y2 * (sz * (1.0 + zs * (1.0 - sz)))).astype(dzs_ref.dtype)
        d_q = d_y2 * yg * (sq * (1.0 - sq))
        dq_mx = d_q.astype(MXU_DTYPE)
        dq_ref[...] = dq_mx
        ygt_ref[...] = yg.T.astype(MXU_DTYPE)
        dbglu_ref[...] += jnp.sum(d_q, axis=0, keepdims=True)
        d_yg = d_y2 * sq + _dot_nt(dq_mx, wglu[...])
        dyp_ref[...] = (d_yg * dgelu).astype(dyp_ref.dtype)

    def rows(width, col=0):
        return pl.BlockSpec((tm, width), lambda i: (i, col))

    def fixed(width):
        return pl.BlockSpec((1, width), lambda i: (0, 0))

    def cols(height):
        return pl.BlockSpec((height, tm), lambda i: (0, i))

    any_ = pl.BlockSpec(memory_space=pl.ANY)
    return pl.pallas_call(
        body, name="tail", grid=(seq // tm,),
        in_specs=[rows(d_model), _chunk_block(tm, d_model), rows(d_ssm, zs_col0 // d_ssm), rows(d_conv), rows(d_ssm),
                  any_, fixed(d_ssm), any_, fixed(d_model)],
        out_specs=[rows(d_model), rows(d_model), cols(d_mix), rows(d_conv), rows(d_ssm), rows(d_ssm), cols(d_ssm),
                   rows(d_ssm), fixed(LANES), fixed(d_model), fixed(d_ssm)],
        out_shape=[jax.ShapeDtypeStruct((seq, d_model), F32), jax.ShapeDtypeStruct((seq, d_model), MXU_DTYPE),
                   jax.ShapeDtypeStruct((d_mix, seq), MXU_DTYPE), jax.ShapeDtypeStruct((seq, d_conv), MXU_DTYPE),
                   jax.ShapeDtypeStruct((seq, d_ssm), MXU_DTYPE), jax.ShapeDtypeStruct((seq, d_ssm), MXU_DTYPE),
                   jax.ShapeDtypeStruct((d_ssm, seq), MXU_DTYPE), jax.ShapeDtypeStruct((seq, d_ssm), MXU_DTYPE),
                   jax.ShapeDtypeStruct((1, LANES), F32), jax.ShapeDtypeStruct((1, d_model), F32),
                   jax.ShapeDtypeStruct((1, d_ssm), F32)],
        scratch_shapes=[pltpu.VMEM(w_glu.shape, MXU_DTYPE), pltpu.VMEM(w_out.shape, MXU_DTYPE)],
        compiler_params=_params(("arbitrary",)),
    )(xp, t3, proj, yconv, yp, w_glu, b_glu, w_out, g_post)


def _bwd_in(d4, du, dzs, gr, win_g, xp, dy, g_pre, comm, tm):
    seq, d_model = xp.shape
    nb, n_p, _, gw = win_g.shape
    nc = n_p * gw
    per = d4.shape[2] // gr

    def body(d4_ref, du_ref, dzs_ref, w_hbm, x_ref, dy_ref, g_ref, gx_ref, dg_ref, w_all, w_sems):
        def granule(g):
            p, cols = g // per, slice(g % per * gr, (g % per + 1) * gr)
            if p < 4:
                return d4_ref[p, :, cols]
            return du_ref[:, cols] if p == 4 else dzs_ref[:, cols]

        i = pl.program_id(0)
        loads = [[pltpu.make_async_copy(w_hbm.at[k, r], w_all.at[k, :, pl.ds(r * gw, gw)], w_sems.at[k, r])
                  for r in range(n_p)] for k in range(nb)]

        @pl.when(i == 0)
        def _():
            dg_ref[...] = jnp.zeros_like(dg_ref)
            for row in loads:
                for cp in row:
                    cp.start()

        dh = None
        for k in range(nb):
            @pl.when(i == 0)
            def _():
                for cp in loads[k]:
                    cp.wait()

            dp = jnp.concatenate([granule(g) for g in range(k * nc // gr, (k + 1) * nc // gr)], axis=1)
            part = _dot_nt(dp, w_all[k])
            dh = part if dh is None else dh + part

        x = x_ref[...]
        r = lax.rsqrt(jnp.mean(x * x, axis=-1, keepdims=True) + EPS)
        xn = x * r
        dg_ref[...] += jnp.sum(dh * xn, axis=0, keepdims=True)
        dxn = dh * g_ref[...]
        gx_ref[...] = r * (dxn - xn * jnp.mean(dxn * xn, axis=-1, keepdims=True)) + dy_ref[...]

    row = pl.BlockSpec((tm, d_model), lambda i: (i, 0))
    vec = pl.BlockSpec((1, d_model), lambda i: (0, 0))
    return _call(
        body, comm, name="bwd_in", grid=(seq // tm,),
        in_specs=[pl.BlockSpec((4, tm, d4.shape[2]), lambda i: (0, i, 0)),
                  pl.BlockSpec((tm, du.shape[1]), lambda i: (i, 0)), pl.BlockSpec((tm, dzs.shape[1]), lambda i: (i, 0)),
                  pl.BlockSpec(memory_space=pl.ANY), row, row, vec],
        out_specs=[row, vec],
        out_shape=[jax.ShapeDtypeStruct((seq, d_model), F32), jax.ShapeDtypeStruct((1, d_model), F32)],
        scratch_shapes=[pltpu.VMEM((nb, d_model, nc), win_g.dtype), pltpu.SemaphoreType.DMA((nb, n_p))],
        operands=[d4, du, dzs, win_g, xp, dy, g_pre])


def _lookup(g, table):
    out = jnp.int32(table[0])
    for gi in range(1, len(table)):
        if table[gi] != table[gi - 1]:
            out = jnp.where(g >= gi, jnp.int32(table[gi]), out)
    return out


def _held(values, used):
    cur = next(v for v, u in zip(values, used) if u)
    out = []
    for v, u in zip(values, used):
        cur = v if u else cur
        out.append(cur)
    return out


def _dw_in(name, ht, d4, du, dzs, granules, gr, nc, tm, comm=None):
    d_model, seq = ht.shape
    per = d4.shape[2] // gr
    piece, col = [g // per for g in granules], [g % per for g in granules]
    sources = [(d4, [p < 4 for p in piece]), (du, [p == 4 for p in piece]), (dzs, [p == 5 for p in piece])]
    sources = [(a, used) for a, used in sources if any(used)]
    select = [next(s for s, (_, used) in enumerate(sources) if used[q]) for q in range(len(granules))]
    owner, place = [g * gr // nc for g in granules], [g * gr % nc // gr for g in granules]

    def body(a_ref, *refs):
        src_refs, o_ref = refs[:-1], refs[-1]
        j = pl.program_id(1)
        for s, ref in enumerate(src_refs):
            @pl.when(_lookup(j, select) == s)
            def _():
                o_ref[...] = _dot(a_ref[...], ref[...]).astype(o_ref.dtype)

    in_specs = [pl.BlockSpec((tm, seq), lambda i, j: (i, 0))]
    for a, used in sources:
        cols = _held(col, used)
        if a.ndim == 3:
            rows = _held(piece, used)
            in_specs.append(pl.BlockSpec((None, seq, gr), functools.partial(
                lambda i, j, rows, cols: (_lookup(j, rows), 0, _lookup(j, cols)), rows=rows, cols=cols)))
        else:
            in_specs.append(pl.BlockSpec((seq, gr), functools.partial(
                lambda i, j, cols: (0, _lookup(j, cols)), cols=cols)))
    return _call(
        body, comm, name=name, grid=(d_model // tm, len(granules)), in_specs=in_specs,
        out_specs=[pl.BlockSpec((None, tm, gr), lambda i, j: (_lookup(j, owner), i, _lookup(j, place)))],
        out_shape=[jax.ShapeDtypeStruct((N_DEV, d_model, nc), MXU_DTYPE)],
        operands=[ht] + [a for a, _ in sources])


def _wgrad(name, at, b, tm, tn, out_shape, out_block, out_index, comm=None):
    m, seq = at.shape
    n = b.shape[1]

    def body(a_ref, b_ref, o_ref):
        o_ref[...] = _dot(a_ref[...], b_ref[...]).astype(o_ref.dtype)

    return _call(
        body, comm, name=name, grid=(n // tn, m // tm),
        in_specs=[pl.BlockSpec((tm, seq), lambda j, i: (i, 0)), pl.BlockSpec((seq, tn), lambda j, i: (0, j))],
        out_specs=[pl.BlockSpec(out_block, lambda j, i: out_index(i, j))],
        out_shape=[jax.ShapeDtypeStruct(out_shape, MXU_DTYPE)],
        operands=[at, b])


def _eye_g():
    return jnp.eye(HALF_G, dtype=F32)


def _bb_blockdiag(bbt_r, bbt_i):
    n_half = bbt_r.shape[0] // HALF_G

    def one(t):
        t = t.reshape(n_half, HALF_G, SSM_GROUP, SSM_STATE)
        t = t[:, :, :, None, :] * _eye_g()[None, :, None, :, None]
        return t.reshape(n_half, HALF_CH, HALF_W)

    return jnp.concatenate([one(bbt_r), one(bbt_i)], axis=-1)


def _cc_blockdiag(c_re, c_im):
    n_half = c_re.shape[0] // HALF_G

    def one(t):
        t = t.reshape(n_half, HALF_G, SSM_GROUP, SSM_STATE)
        t = jnp.transpose(t, (0, 3, 1, 2))
        t = t[:, None, :, :, :] * _eye_g()[None, :, None, :, None]
        return t.reshape(n_half, HALF_W, HALF_CH)

    return jnp.concatenate([one(c_re), one(-c_im)], axis=1)


def _bb_diag(dbb):
    n_half = dbb.shape[0]
    t = dbb.reshape(n_half, HALF_G, SSM_GROUP, 2, HALF_G, SSM_STATE)
    t = jnp.sum(t * _eye_g()[None, :, None, None, :, None], axis=4)
    t = jnp.transpose(t, (3, 0, 1, 2, 4))
    return t.reshape(2, n_half * HALF_G, SSM_GROUP, SSM_STATE)


def _cc_diag(dcc):
    n_half = dcc.shape[0]
    t = dcc.reshape(n_half, 2, HALF_G, SSM_STATE, HALF_G, SSM_GROUP)
    t = jnp.sum(t * _eye_g()[None, None, :, None, :, None], axis=2)
    t = jnp.transpose(t, (1, 0, 3, 4, 2))
    return t.reshape(2, n_half * HALF_G, SSM_GROUP, SSM_STATE)


def _unpermute_rows(a):
    seq, d = a.shape
    return a.reshape(seq // N_CHUNK, N_CHUNK, d).transpose(1, 0, 2).reshape(seq, d)


def _pack_rows(shape):
    return -(-math.prod(shape) // (8 * LANES)) * 8


def _pack(parts, dtype=F32):
    rows = []
    for p in parts:
        flat = p.reshape(-1).astype(dtype)
        rows.append(jnp.pad(flat, (0, _pack_rows(p.shape) * LANES - flat.shape[0])).reshape(-1, LANES))
    return jnp.concatenate(rows, axis=0)


def _unpack(packed, shapes):
    out, o = [], 0
    for s in shapes:
        n = _pack_rows(s)
        out.append(packed[o:o + n].reshape(-1)[:math.prod(s)].reshape(s))
        o += n
    return out


def kernel(x, norm_pre_g, w_in, conv_w, conv_b, ssm_a_re, ssm_a_im, ssm_log_dt, ssm_b_re, ssm_b_im, ssm_c_re, ssm_c_im, ssm_d, w_glu, b_glu, w_out, norm_post_g, loss_target, m_norm_pre_g, m_w_in, m_conv_w, m_conv_b, m_ssm_a_re, m_ssm_a_im, m_ssm_log_dt, m_ssm_b_re, m_ssm_b_im, m_ssm_c_re, m_ssm_c_im, m_ssm_d, m_w_glu, m_b_glu, m_w_out, m_norm_post_g, v_norm_pre_g, v_w_in, v_conv_w, v_conv_b, v_ssm_a_re, v_ssm_a_im, v_ssm_log_dt, v_ssm_b_re, v_ssm_b_im, v_ssm_c_re, v_ssm_c_im, v_ssm_d, v_w_glu, v_b_glu, v_w_out, v_norm_post_g):
    seq, d_model = x.shape[1], x.shape[2]
    d_conv, d_ssm = conv_b.shape[0], ssm_d.shape[0]
    groups, states = ssm_a_re.shape
    assert x.shape[0] == 1 and seq % (8 * N_CHUNK) == 0 and d_conv == d_ssm
    assert (groups, states) == (d_ssm // SSM_GROUP, SSM_STATE) and d_ssm % LANES == 0
    me = 4 * lax.axis_index("x") + 2 * lax.axis_index("y") + lax.axis_index("c")
    tm = min(512, seq)

    x3 = x[0].reshape(N_CHUNK, seq // N_CHUNK, d_model)
    t3 = loss_target[0].reshape(N_CHUNK, seq // N_CHUNK, d_model)
    row = lambda a: a.reshape(1, -1)
    conv_w8 = jnp.pad(conv_w, ((0, 8 - conv_w.shape[0]), (0, 0)))

    g3 = lambda a: a.reshape(groups, 1, -1)
    bt_re, bt_im = jnp.transpose(ssm_b_re, (0, 2, 1)), jnp.transpose(ssm_b_im, (0, 2, 1))
    lbr, lbi, qr, qi, bbt_r, bbt_i = _ssm_prep(g3(ssm_a_re), g3(ssm_a_im), g3(ssm_log_dt), bt_re, bt_im)
    n_half = groups // HALF_G
    lam = jnp.stack([lbr.reshape(n_half, HALF_W), lbi.reshape(n_half, HALF_W)], axis=1)
    bbcat = _bb_blockdiag(bbt_r, bbt_i).astype(MXU_DTYPE)
    cccat = _cc_blockdiag(ssm_c_re, ssm_c_im).astype(MXU_DTYPE)

    xp, h, ht = _norm_in(x3, row(norm_pre_g), tm)
    nc = w_in.shape[1]
    w_pieces = jnp.transpose(w_in.astype(MXU_DTYPE).reshape(d_model, nc // PIECE_COLS, PIECE_COLS), (1, 0, 2))
    proj, win_g, (convw_g, wglu_g) = _fwd_in(h, w_pieces, me, _Comm([conv_w8, w_glu.astype(MXU_DTYPE)]),
                                             min(2048, seq))
    conv_w_full = jnp.transpose(convw_g, (1, 0, 2)).reshape(8, d_conv)
    u_col0, zs_col0 = 4 * d_conv, 4 * d_conv + d_ssm
    yconv = _conv_fwd(proj, conv_w_full, row(conv_b), d_conv)
    (yp,), (wout_g,) = _ssm_fwd(proj, lam, bbcat, cccat, row(ssm_d), d_ssm, u_col0,
                                _Comm([w_out.astype(MXU_DTYPE)]))
    w_out_full = wout_g.reshape(-1, d_model)
    w_glu_full = wglu_g.reshape(-1, d_ssm)
    (dy, d_o, mixt, dyc, dyp, dzs, ygt, dq, loss_part, dgpost, dbglu) = _tail(
        xp, t3, proj, yconv, yp, w_glu_full, row(b_glu), w_out_full, row(norm_post_g), zs_col0, min(256, seq))

    r_out, r_glu, nc = w_out.shape[0], w_glu.shape[0], w_in.shape[1]
    (dwout_p,), _ = _wgrad("dw_out", mixt, d_o, r_out, min(1024, d_model), (N_DEV, r_out, d_model),
                           (None, r_out, min(1024, d_model)), lambda i, j: (i, 0, j))
    (dwglu_p,), _ = _wgrad("dw_glu", ygt, dq, r_glu, d_ssm, (N_DEV, r_glu, d_ssm),
                           (None, r_glu, d_ssm), lambda i, j: (i, 0, 0))
    (d4, dconvb, dconvw), (recv_glu,) = _conv_bwd(proj, dyc, conv_w_full, row(conv_b), d_conv,
                                                  _Comm([], [dwglu_p]))
    late = [k for k in range(N_DEV) if k * nc < u_col0 + d_ssm and (k + 1) * nc > u_col0]
    early = [k for k in range(N_DEV) if k not in late]
    gr = math.gcd(nc, d_conv)
    granules = lambda blocks: [g for k in blocks for g in range(k * nc // gr, (k + 1) * nc // gr)]
    tmw = min(1024, d_model)
    (dwin_e,), (recv_out,) = _dw_in("dw_in_early", ht, d4, None, dzs, granules(early), gr, nc, tmw,
                                    _Comm([], [dwout_p]))
    (du, dbb, dcc, da, dd), (recv_in,) = _ssm_bwd(
        proj, dyp, lam, bbcat, cccat, row(ssm_d), d_ssm, u_col0, _Comm([], [dwin_e], dests={0: early}))
    parts_mx = [_bb_diag(dbb), _cc_diag(dcc)]
    (dwin_l,), (pack_mx_g,) = _dw_in("dw_in_late", ht, d4, du, dzs, granules(late), gr, nc, tmw,
                                     _Comm([_pack(parts_mx, MXU_DTYPE)]))
    da_n = jnp.transpose(da.reshape(n_half, 2, HALF_G, SSM_STATE), (1, 0, 2, 3)).reshape(2, groups, 1, states)
    parts = [dgpost, dconvb, dd, dbglu, dconvw[:3], da_n, loss_part]
    shapes, shapes_mx = [p.shape for p in parts], [p.shape for p in parts_mx]
    (gx_p, dgpre), (pack_g, recv_in) = _bwd_in(
        d4, du, dzs, gr, win_g, xp, dy, row(norm_pre_g),
        _Comm([_pack(parts)], [dwin_l], dests={1: late}, into={1: recv_in}), min(256, seq))
    (last_g,) = _exchange("reduce_last", [_pack([dgpre])], [])
    (g_gpost, g_convb, g_d, g_bglu, g_convw, g_da, loss_sum) = _unpack(_sum_slots("sum_pack", pack_g), shapes)
    (g_dbb, g_dcc) = _unpack(_sum_slots("sum_pack_mx", pack_mx_g), shapes_mx)
    (g_gpre,) = _unpack(_sum_slots("sum_last", last_g), [dgpre.shape])
    g_convw = lax.dynamic_slice(g_convw, (0, me * conv_w.shape[1]), conv_w.shape)

    tr = lambda a: jnp.transpose(a, (0, 2, 1))
    direct = [(g_gpre, row(norm_pre_g), row(m_norm_pre_g), row(v_norm_pre_g)),
              (g_convb, row(conv_b), row(m_conv_b), row(v_conv_b)),
              (g_d, row(ssm_d), row(m_ssm_d), row(v_ssm_d)),
              (g_bglu, row(b_glu), row(m_b_glu), row(v_b_glu)),
              (g_gpost, row(norm_post_g), row(m_norm_post_g), row(v_norm_post_g)),
              (g_convw, conv_w, m_conv_w, v_conv_w),
              (g_dcc[0], ssm_c_re, m_ssm_c_re, v_ssm_c_re),
              (-g_dcc[1], ssm_c_im, m_ssm_c_im, v_ssm_c_im)]
    ssm = dict(da_r=g_da[0], da_i=g_da[1], dbb_r=g_dbb[0], dbb_i=g_dbb[1], lr=g3(ssm_a_re), li=g3(ssm_a_im),
               ldt=g3(ssm_log_dt), bt_r=bt_re, bt_i=bt_im, lbr=lbr, lbi=lbi, qr=qr, qi=qi,
               w_a_re=g3(ssm_a_re), m_a_re=g3(m_ssm_a_re), v_a_re=g3(v_ssm_a_re),
               w_a_im=g3(ssm_a_im), m_a_im=g3(m_ssm_a_im), v_a_im=g3(v_ssm_a_im),
               w_log_dt=g3(ssm_log_dt), m_log_dt=g3(m_ssm_log_dt), v_log_dt=g3(v_ssm_log_dt),
               w_bt_re=bt_re, m_bt_re=tr(m_ssm_b_re), v_bt_re=tr(v_ssm_b_re),
               w_bt_im=bt_im, m_bt_im=tr(m_ssm_b_im), v_bt_im=tr(v_ssm_b_im))
    small = _small_update(direct, ssm)
    res = {}
    for name, quad, shape in zip(["norm_pre_g", "conv_b", "ssm_d", "b_glu", "norm_post_g", "conv_w", "ssm_c_re", "ssm_c_im"],
                                 small[:8], [norm_pre_g.shape, conv_b.shape, ssm_d.shape, b_glu.shape,
                                             norm_post_g.shape, conv_w.shape, ssm_c_re.shape, ssm_c_im.shape]):
        res[name] = tuple(a.reshape(shape) for a in quad)
    res["ssm_a_re"] = tuple(a.reshape(ssm_a_re.shape) for a in small[8])
    res["ssm_a_im"] = tuple(a.reshape(ssm_a_im.shape) for a in small[9])
    res["ssm_log_dt"] = tuple(a.reshape(ssm_log_dt.shape) for a in small[10])
    res["ssm_b_re"] = tuple(tr(a) for a in small[11])
    res["ssm_b_im"] = tuple(tr(a) for a in small[12])
    res["w_in"] = tuple(_adam_big("adam_w_in", recv_in, w_in, m_w_in, v_w_in, min(256, d_model)))
    res["w_out"] = tuple(_adam_big("adam_w_out", recv_out, w_out, m_w_out, v_w_out, min(128, r_out)))
    res["w_glu"] = tuple(_adam_big("adam_w_glu", recv_glu, w_glu, m_w_glu, v_w_glu, r_glu))

    order = ["norm_pre_g", "w_in", "conv_w", "conv_b", "ssm_a_re", "ssm_a_im", "ssm_log_dt", "ssm_b_re", "ssm_b_im",
             "ssm_c_re", "ssm_c_im", "ssm_d", "w_glu", "b_glu", "w_out", "norm_post_g"]
    loss = loss_sum[0, 0]
    grad_x = _unpermute_rows(gx_p)[None]
    return (loss, grad_x, *[res[n][0] for n in order], *[res[n][1] for n in order],
            *[res[n][2] for n in order], *[res[n][3] for n in order])
```

```python
import functools
import math

import jax
import jax.numpy as jnp
from jax import lax
from jax.experimental import pallas as pl
from jax.experimental.pallas import tpu as pltpu

F32 = jnp.float32
MXU_DTYPE = jnp.bfloat16
PROJ_DTYPE = jnp.bfloat16
AXES = ("x", "y", "c")
N_DEV = 8
N_CHUNK = 8
LANES = 128
SSM_GROUP = 16
SSM_STATE = 64
HALF_CH = 64
HALF_G = HALF_CH // SSM_GROUP
HALF_W = HALF_G * SSM_STATE
EPS = 1e-6
ADAM_LR, ADAM_B1, ADAM_B2, ADAM_EPS, ADAM_WD, ADAM_STEP = 0.001, 0.9, 0.999, 1e-08, 0.01, 10
GELU_C = math.sqrt(2.0 / math.pi)
GELU_K = 0.044715
VMEM_LIMIT = 56 * 1024 * 1024


def _params(sem=None):
    return pltpu.CompilerParams(dimension_semantics=sem, vmem_limit_bytes=VMEM_LIMIT)


def _dot(a, b):
    return jnp.dot(a, b, preferred_element_type=F32)


def _dot_nt(a, b):
    return lax.dot_general(a, b, (((1,), (1,)), ((), ())), preferred_element_type=F32)


def _dot_tn(a, b):
    return lax.dot_general(a, b, (((0,), (0,)), ((), ())), preferred_element_type=F32)


def _sigmoid(z):
    return 1.0 / (1.0 + jnp.exp(-z))


def _flip(v, bit):
    return 1 - v if bit else v


def _peers():
    x, y, c = (lax.axis_index(a) for a in AXES)
    out = []
    for m in range(1, N_DEV):
        px, py, pc = _flip(x, (m >> 2) & 1), _flip(y, (m >> 1) & 1), _flip(c, m & 1)
        out.append((px, py, pc, 4 * px + 2 * py + pc))
    return out


class _Comm:
    def __init__(self, gathers=(), scatters=(), dests=None, into=None):
        self.n_g = len(gathers)
        self.operands = list(gathers) + list(scatters)
        self.n = len(self.operands)
        self.dests = dests or {}
        self.into = into or {}

    def out_shape(self):
        return [jax.ShapeDtypeStruct((N_DEV,) + a.shape if t < self.n_g else a.shape, a.dtype)
                for t, a in enumerate(self.operands)]

    def scratch(self):
        if not self.n:
            return []
        return [pltpu.SemaphoreType.DMA((self.n, N_DEV - 1)), pltpu.SemaphoreType.DMA((self.n, N_DEV - 1)),
                pltpu.SemaphoreType.DMA((self.n,))]

    def _copies(self, in_refs, out_refs, sems, arrivals):
        send_sems, recv_sems, local_sems = sems
        x, y, c = (lax.axis_index(a) for a in AXES)
        me = 4 * x + 2 * y + c

        def src(t, dev):
            return in_refs[t] if t < self.n_g else in_refs[t].at[dev]

        def member(t, dev):
            if t not in self.dests:
                return None
            return functools.reduce(jnp.logical_or, [dev == d for d in self.dests[t]])

        local = [(member(t, me), pltpu.make_async_copy(src(t, me), out_refs[t].at[me], local_sems.at[t]))
                 for t in range(self.n)]
        sends, recvs = [], []
        for t in range(self.n):
            for m, (px, py, pc, peer) in enumerate(_peers()):
                kw = dict(send_sem=send_sems.at[t, m], recv_sem=recv_sems.at[t, m],
                          device_id=(px, py, pc), device_id_type=pl.DeviceIdType.MESH)
                sends.append((member(t, peer), pltpu.make_async_remote_copy(
                    src_ref=src(t, peer), dst_ref=out_refs[t].at[me], **kw)))
                if arrivals:
                    recvs.append((member(t, me), pltpu.make_async_remote_copy(
                        src_ref=src(t, peer), dst_ref=out_refs[t].at[peer], **kw)))
        return local, sends, recvs

    @staticmethod
    def _do(cond, action):
        if cond is None:
            action()
        else:
            pl.when(cond)(action)

    def start(self, in_refs, out_refs, sems):
        local, sends, _ = self._copies(in_refs, out_refs, sems, arrivals=False)
        for cond, cp in local + sends:
            self._do(cond, cp.start)

    def finish(self, in_refs, out_refs, sems):
        local, sends, recvs = self._copies(in_refs, out_refs, sems, arrivals=True)
        for cond, cp in recvs:
            self._do(cond, cp.wait_recv)
        for cond, cp in sends:
            self._do(cond, cp.wait_send)
        for cond, cp in local:
            self._do(cond, cp.wait)


def _call(body, comm, *, name, grid, in_specs, out_specs, out_shape, operands, scratch_shapes=()):
    comm = comm or _Comm()
    n_in, n_out, n_scr, cn = len(in_specs), len(out_specs), len(scratch_shapes), comm.n
    landing = sorted(comm.into)
    aliases = {n_in + cn + q: n_out + t for q, t in enumerate(landing)}

    def wrapped(*refs):
        parts, o = [], 0
        for k in (n_in, cn, len(landing), n_out, cn, n_scr):
            parts.append(refs[o:o + k])
            o += k
        h_in, c_in, _, h_out, c_out, h_scr = parts
        sems = refs[o:]
        if cn:
            first = functools.reduce(jnp.logical_and, [pl.program_id(d) == 0 for d in range(len(grid))])

            @pl.when(first)
            def _():
                comm.start(c_in, c_out, sems)

        body(*h_in, *h_out, *h_scr)
        if cn:
            last = functools.reduce(jnp.logical_and, [pl.program_id(d) == grid[d] - 1 for d in range(len(grid))])

            @pl.when(last)
            def _():
                comm.finish(c_in, c_out, sems)

    any_ = pl.BlockSpec(memory_space=pl.ANY)
    res = pl.pallas_call(
        wrapped, name=name, grid=grid, in_specs=list(in_specs) + [any_] * (cn + len(landing)),
        out_specs=list(out_specs) + [any_] * cn,
        out_shape=list(out_shape) + comm.out_shape(), scratch_shapes=list(scratch_shapes) + comm.scratch(),
        input_output_aliases=aliases, compiler_params=_params(("arbitrary",) * len(grid)),
    )(*operands, *comm.operands, *[comm.into[t] for t in landing])
    return list(res[:n_out]), list(res[n_out:])


def _exchange(name, gathers, scatters):
    def body(tok_ref):
        tok_ref[...] = jnp.zeros_like(tok_ref)

    return _call(body, _Comm(gathers, scatters), name=name, grid=(1,), in_specs=[],
                 out_specs=[pl.BlockSpec((8, LANES), lambda i: (0, 0))],
                 out_shape=[jax.ShapeDtypeStruct((8, LANES), F32)], operands=[])[1]


def _ssm_prep(a_re, a_im, log_dt, bt_re, bt_im):
    def body(lr_ref, li_ref, ldt_ref, br_ref, bi_ref, lbr_ref, lbi_ref, qr_ref, qi_ref, bbr_ref, bbi_ref):
        lr, li = lr_ref[...], li_ref[...]
        dt = jnp.exp(ldt_ref[...])
        mag = jnp.exp(lr * dt)
        lbr, lbi = mag * jnp.cos(li * dt), mag * jnp.sin(li * dt)
        nr, ni = lbr - 1.0, lbi
        den = lr * lr + li * li
        qr = (nr * lr + ni * li) / den
        qi = (ni * lr - nr * li) / den
        br, bi = br_ref[...], bi_ref[...]
        lbr_ref[...], lbi_ref[...], qr_ref[...], qi_ref[...] = lbr, lbi, qr, qi
        bbr_ref[...] = qr * br - qi * bi
        bbi_ref[...] = qr * bi + qi * br

    s2 = jax.ShapeDtypeStruct(a_re.shape, F32)
    s3 = jax.ShapeDtypeStruct(bt_re.shape, F32)
    return pl.pallas_call(body, name="ssm_prep", out_shape=[s2, s2, s2, s2, s3, s3],
                          compiler_params=_params())(a_re, a_im, log_dt, bt_re, bt_im)


def _adam(w, g, m, v):
    m2 = ADAM_B1 * m + (1.0 - ADAM_B1) * g
    v2 = ADAM_B2 * v + (1.0 - ADAM_B2) * (g * g)
    m_hat = m2 / (1.0 - ADAM_B1 ** ADAM_STEP)
    v_hat = v2 / (1.0 - ADAM_B2 ** ADAM_STEP)
    delta = -ADAM_LR * (m_hat / (jnp.sqrt(v_hat) + ADAM_EPS) + ADAM_WD * w)
    return delta, m2, v2


def _small_update(direct, ssm):
    n_direct = len(direct)
    flat = [a for quad in direct for a in quad]
    names = ["da_r", "da_i", "dbb_r", "dbb_i", "lr", "li", "ldt", "bt_r", "bt_i", "lbr", "lbi", "qr", "qi"]
    flat += [ssm[k] for k in names]
    chain = ["a_re", "a_im", "log_dt", "bt_re", "bt_im"]
    for k in chain:
        flat += [ssm["w_" + k], ssm["m_" + k], ssm["v_" + k]]
    n_in = len(flat)

    def body(*refs):
        ins, outs = refs[:n_in], refs[n_in:]
        for p in range(n_direct):
            g, w, m, v = (r[...] for r in ins[4 * p:4 * p + 4])
            d, m2, v2 = _adam(w, g, m, v)
            outs[4 * p][...], outs[4 * p + 1][...], outs[4 * p + 2][...], outs[4 * p + 3][...] = g, d, m2, v2
        o = 4 * n_direct
        da_r, da_i, dbb_r, dbb_i, lr, li, ldt, bt_r, bt_i, lbr, lbi, qr, qi = (r[...] for r in ins[o:o + 13])
        dt = jnp.exp(ldt)
        g_br = qr * dbb_r + qi * dbb_i
        g_bi = qr * dbb_i - qi * dbb_r
        dq_r = jnp.sum(bt_r * dbb_r + bt_i * dbb_i, axis=1, keepdims=True)
        dq_i = jnp.sum(bt_r * dbb_i - bt_i * dbb_r, axis=1, keepdims=True)
        den = lr * lr + li * li
        cr, ci = lr / den, li / den
        gl_r = da_r + (cr * dq_r - ci * dq_i)
        gl_i = da_i + (cr * dq_i + ci * dq_r)
        w_r = qr * cr + qi * ci
        w_i = qi * cr - qr * ci
        g_lr = dt * (lbr * gl_r + lbi * gl_i) + (-w_r * dq_r - w_i * dq_i)
        g_li = dt * (lbr * gl_i - lbi * gl_r) + (-w_r * dq_i + w_i * dq_r)
        m_r = lr * lbr - li * lbi
        m_i = lr * lbi + li * lbr
        g_ldt = jnp.sum(m_r * gl_r + m_i * gl_i, axis=2, keepdims=True) * dt
        grads = [g_lr, g_li, g_ldt, g_br, g_bi]
        base_in, base_out = o + 13, 4 * n_direct
        for p, g in enumerate(grads):
            w, m, v = (r[...] for r in ins[base_in + 3 * p:base_in + 3 * p + 3])
            d, m2, v2 = _adam(w, g, m, v)
            q = base_out + 4 * p
            outs[q][...], outs[q + 1][...], outs[q + 2][...], outs[q + 3][...] = g, d, m2, v2

    out_shape = []
    for quad in direct:
        out_shape += [jax.ShapeDtypeStruct(quad[1].shape, F32)] * 4
    for k in chain:
        out_shape += [jax.ShapeDtypeStruct(ssm["w_" + k].shape, F32)] * 4
    res = pl.pallas_call(body, name="small_update", out_shape=out_shape, compiler_params=_params())(*flat)
    return [tuple(res[4 * p:4 * p + 4]) for p in range(n_direct + len(chain))]


def _sum_slots(name, pack):
    def body(p_ref, o_ref):
        acc = p_ref[0].astype(F32)
        for k in range(1, N_DEV):
            acc = acc + p_ref[k].astype(F32)
        o_ref[...] = acc

    return pl.pallas_call(body, name=name, out_shape=jax.ShapeDtypeStruct(pack.shape[1:], F32),
                          compiler_params=_params())(pack)


def _adam_big(name, recv, w, m, v, tr):
    _, rows, cols = recv.shape

    def body(r_ref, w_ref, m_ref, v_ref, g_ref, d_ref, m2_ref, v2_ref):
        g = r_ref[0].astype(F32)
        for k in range(1, N_DEV):
            g = g + r_ref[k].astype(F32)
        d, m2, v2 = _adam(w_ref[...], g, m_ref[...], v_ref[...])
        g_ref[...], d_ref[...], m2_ref[...], v2_ref[...] = g, d, m2, v2

    blk = pl.BlockSpec((tr, cols), lambda i: (i, 0))
    shp = jax.ShapeDtypeStruct((rows, cols), F32)
    return pl.pallas_call(
        body, name=name, grid=(rows // tr,),
        in_specs=[pl.BlockSpec((N_DEV, tr, cols), lambda i: (0, i, 0)), blk, blk, blk],
        out_specs=[blk] * 4, out_shape=[shp] * 4, compiler_params=_params(("parallel",)),
    )(recv, w, m, v)


def _chunk_block(tm, d):
    return pl.BlockSpec((N_CHUNK, tm // N_CHUNK, d), lambda i: (0, i, 0))


def _interleave(block):
    c, n, d = block.shape
    return pltpu.einshape("cjd->jcd", block).reshape(n * c, d)


def _norm_in(x3, g_pre, tm):
    _, steps, d_model = x3.shape
    seq = steps * N_CHUNK

    def body(x_ref, g_ref, xp_ref, h_ref, ht_ref):
        x = _interleave(x_ref[...])
        xp_ref[...] = x
        r = lax.rsqrt(jnp.mean(x * x, axis=-1, keepdims=True) + EPS)
        h = x * r * g_ref[...]
        h_ref[...] = h.astype(h_ref.dtype)
        ht_ref[...] = h.T.astype(ht_ref.dtype)

    rows = pl.BlockSpec((tm, d_model), lambda i: (i, 0))
    return pl.pallas_call(
        body, name="norm_in", grid=(seq // tm,),
        in_specs=[_chunk_block(tm, d_model), pl.BlockSpec((1, d_model), lambda i: (0, 0))],
        out_specs=[rows, rows, pl.BlockSpec((d_model, tm), lambda i: (0, i))],
        out_shape=[jax.ShapeDtypeStruct((seq, d_model), F32), jax.ShapeDtypeStruct((seq, d_model), MXU_DTYPE),
                   jax.ShapeDtypeStruct((d_model, seq), MXU_DTYPE)],
        compiler_params=_params(("parallel",)),
    )(x3, g_pre)


GATHER_ORDER = (0, 1, 4, 2, 6, 5, 3, 7)
PIECE_COLS = 256


def _fwd_in(h, w_pieces, me, comm, tm):
    seq, d_model = h.shape
    n_p, _, gw = w_pieces.shape
    n_i = seq // tm
    cn = comm.n
    consume = [(rel, r) for rel in (0, 1) for r in range(n_p)]
    consume += [(rel, r) for r in range(n_p) for rel in (4, 2, 5, 3)]
    consume += [(rel, r) for r in range(n_p) for rel in (6, 7)]
    n_q = len(consume)
    order = jnp.stack([jnp.bitwise_xor(me, rel) * n_p + r for rel, r in consume]).astype(jnp.int32)

    def body(order_ref, h_hbm, w_hbm, *rest):
        c_in, rest = rest[:cn], rest[cn:]
        proj_hbm, wing = rest[0], rest[1]
        c_out, rest = rest[2:2 + cn], rest[2 + cn:]
        wbuf, send_sems, recv_sems, own_sems, load_sems, h_all, h_sems, out_buf, out_sems = rest[:9]
        c_sems = rest[9:]
        h_loads = [pltpu.make_async_copy(h_hbm.at[pl.ds(t * tm, tm)], h_all.at[pl.ds(t * tm, tm)], h_sems.at[t])
                   for t in range(n_i)]
        x, y, c = (lax.axis_index(a) for a in AXES)
        me_ = 4 * x + 2 * y + c

        def dev(rel):
            return _flip(x, (rel >> 2) & 1), _flip(y, (rel >> 1) & 1), _flip(c, rel & 1)

        def slot(rel):
            px, py, pc = dev(rel)
            return 4 * px + 2 * py + pc

        def remote(src, block, r, to_rel, sem):
            return pltpu.make_async_remote_copy(
                src_ref=src, dst_ref=wing.at[block, r], send_sem=send_sems.at[sem, r], recv_sem=recv_sems.at[sem, r],
                device_id=dev(to_rel), device_id_type=pl.DeviceIdType.MESH)

        pieces = range(n_p)
        own = [pltpu.make_async_copy(w_hbm.at[r], wing.at[me_, r], own_sems.at[r]) for r in pieces]
        first_hand = {p: [remote(w_hbm.at[r], me_, r, GATHER_ORDER[p], p - 1) for r in pieces] for p in (1, 2, 3)}
        relay = {2: (c == 0, [remote(wing.at[slot(4), r], slot(4), r, 2, 3) for r in pieces]),
                 3: (c == 1, [remote(wing.at[slot(2), r], slot(2), r, 4, 3) for r in pieces])}
        passed_on = {p: [remote(wing.at[slot(GATHER_ORDER[p]), r], slot(GATHER_ORDER[p]), r, 1, p + 2) for r in pieces]
                     for p in (2, 3, 4)}

        def load(q):
            rel, r = consume[q]
            return pltpu.make_async_copy(wing.at[slot(rel), r], wbuf.at[q % 2], load_sems.at[q % 2])

        def take(q):
            rel, r = consume[q]
            p = GATHER_ORDER.index(rel)
            if p == 0:
                own[r].wait()
            else:
                remote(w_hbm.at[r], slot(rel), r, rel, p - 1).wait_recv()
            if p in relay:
                pl.when(relay[p][0])(relay[p][1][r].start)
            if p in passed_on:
                passed_on[p][r].start()
            load(q).start()

        for r in pieces:
            own[r].start()
        for r in pieces:
            for p in (1, 2, 3):
                first_hand[p][r].start()
        for cp in h_loads:
            cp.start()
        comm.start(c_in, c_out, c_sems)
        take(0)

        def store(s, k, i):
            col = pl.multiple_of(order_ref[k] * gw, gw)
            return pltpu.make_async_copy(
                out_buf.at[s % 2], proj_hbm.at[pl.ds(pl.multiple_of(i * tm, tm), tm), pl.ds(col, gw)],
                out_sems.at[s % 2])

        def step(s, carry):
            k, i = s // n_i, s % n_i
            for t in range(n_i):
                pl.when((k == 0) & (i == t))(h_loads[t].wait)
            for q in range(n_q):
                @pl.when((k == q) & (i == 0))
                def _():
                    load(q).wait()

            @pl.when(s >= 2)
            def _():
                store(s, k, i).wait()

            out_buf[s % 2] = _dot(h_all[pl.ds(pl.multiple_of(i * tm, tm), tm), :], wbuf[k % 2]).astype(out_buf.dtype)
            store(s, k, i).start()
            for q in range(n_q - 1):
                pl.when((k == q) & (i == n_i - 1))(functools.partial(take, q + 1))
            return carry

        n_s = n_q * n_i
        lax.fori_loop(0, n_s, step, 0)
        for s in range(max(n_s - 2, 0), n_s):
            store(s, s // n_i, s % n_i).wait()
        for p in first_hand:
            for cp in first_hand[p]:
                cp.wait_send()
        for p in passed_on:
            for cp in passed_on[p]:
                cp.wait_send()
        for cond, cps in relay.values():
            for cp in cps:
                pl.when(cond)(cp.wait_send)
        comm.finish(c_in, c_out, c_sems)

    any_ = pl.BlockSpec(memory_space=pl.ANY)
    grid_spec = pltpu.PrefetchScalarGridSpec(
        num_scalar_prefetch=1, grid=(1,),
        in_specs=[any_, any_] + [any_] * cn,
        out_specs=[any_, any_] + [any_] * cn,
        scratch_shapes=[pltpu.VMEM((2, d_model, gw), w_pieces.dtype), pltpu.SemaphoreType.DMA((N_DEV - 1, n_p)),
                        pltpu.SemaphoreType.DMA((N_DEV - 1, n_p)), pltpu.SemaphoreType.DMA((n_p,)),
                        pltpu.SemaphoreType.DMA((2,)), pltpu.VMEM((seq, d_model), h.dtype),
                        pltpu.SemaphoreType.DMA((n_i,)), pltpu.VMEM((2, tm, gw), PROJ_DTYPE),
                        pltpu.SemaphoreType.DMA((2,))]
        + comm.scratch())
    res = pl.pallas_call(
        body, name="fwd_in", grid_spec=grid_spec,
        out_shape=[jax.ShapeDtypeStruct((seq, N_DEV * n_p * gw), PROJ_DTYPE),
                   jax.ShapeDtypeStruct((N_DEV, n_p, d_model, gw), w_pieces.dtype)] + comm.out_shape(),
        compiler_params=_params(("arbitrary",)),
    )(order, h, w_pieces, *comm.operands)
    return res[0], res[1], list(res[2:])


def _shift_prev(a):
    n = a.shape[0]
    last = a[n - N_CHUNK:, :]
    row = lax.broadcasted_iota(jnp.int32, last.shape, 0)
    wrap = jnp.where(row == 0, 0.0, pltpu.roll(last, 1, axis=0))
    return jnp.concatenate([wrap, a[:n - N_CHUNK, :]], axis=0)


def _shift_next(a):
    first = a[:N_CHUNK, :]
    row = lax.broadcasted_iota(jnp.int32, first.shape, 0)
    wrap = jnp.where(row == N_CHUNK - 1, 0.0, pltpu.roll(first, N_CHUNK - 1, axis=0))
    return jnp.concatenate([a[N_CHUNK:, :], wrap], axis=0)


def _conv_specs(seq, d_conv):
    nblk = d_conv // LANES
    return [pl.BlockSpec((seq, LANES), functools.partial(lambda i, o: (0, o + i), o=q * nblk)) for q in range(4)]


def _conv_fwd(proj, conv_w8, conv_b, d_conv):
    seq = proj.shape[0]

    def body(bg_ref, cg_ref, v_ref, zc_ref, w_ref, b_ref, y_ref):
        cv = cg_ref[...].astype(F32) * v_ref[...].astype(F32)
        s1 = _shift_prev(cv)
        s2 = _shift_prev(s1)
        conv = b_ref[...] + w_ref[0:1, :] * s2 + w_ref[1:2, :] * s1 + w_ref[2:3, :] * cv
        z = zc_ref[...].astype(F32)
        y_ref[...] = (bg_ref[...].astype(F32) * conv * (z * _sigmoid(z))).astype(y_ref.dtype)

    col = pl.BlockSpec((seq, LANES), lambda i: (0, i))
    return pl.pallas_call(
        body, name="conv_fwd", grid=(d_conv // LANES,),
        in_specs=_conv_specs(seq, d_conv) + [pl.BlockSpec((8, LANES), lambda i: (0, i)), pl.BlockSpec((1, LANES), lambda i: (0, i))],
        out_specs=col, out_shape=jax.ShapeDtypeStruct((seq, d_conv), MXU_DTYPE),
        compiler_params=_params(("parallel",)),
    )(proj, proj, proj, proj, conv_w8, conv_b)


def _conv_bwd(proj, dyc, conv_w8, conv_b, d_conv, comm=None):
    seq = proj.shape[0]

    def body(bg_ref, cg_ref, v_ref, zc_ref, dy_ref, w_ref, b_ref, d4_ref, dcb_ref, dcw_ref):
        bg, cg, v, z = (r[...].astype(F32) for r in (bg_ref, cg_ref, v_ref, zc_ref))
        w0, w1, w2 = w_ref[0:1, :], w_ref[1:2, :], w_ref[2:3, :]
        cv = cg * v
        s1 = _shift_prev(cv)
        s2 = _shift_prev(s1)
        conv = b_ref[...] + w0 * s2 + w1 * s1 + w2 * cv
        sig = _sigmoid(z)
        dy = dy_ref[...].astype(F32)
        g1 = dy * (z * sig)
        d_conv_ = g1 * bg
        d4_ref[0] = (g1 * conv).astype(d4_ref.dtype)
        d4_ref[3] = (dy * bg * conv * (sig * (1.0 + z * (1.0 - sig)))).astype(d4_ref.dtype)
        n1 = _shift_next(d_conv_)
        n2 = _shift_next(n1)
        d_cv = w2 * d_conv_ + w1 * n1 + w0 * n2
        d4_ref[1] = (d_cv * v).astype(d4_ref.dtype)
        d4_ref[2] = (d_cv * cg).astype(d4_ref.dtype)
        dcb_ref[...] = jnp.sum(d_conv_, axis=0, keepdims=True)
        rows = [jnp.sum(d_conv_ * s, axis=0, keepdims=True) for s in (s2, s1, cv)]
        dcw_ref[...] = jnp.concatenate(rows + [jnp.zeros((5, LANES), F32)], axis=0)

    col = pl.BlockSpec((seq, LANES), lambda i: (0, i))
    return _call(
        body, comm, name="conv_bwd", grid=(d_conv // LANES,),
        in_specs=_conv_specs(seq, d_conv) + [col, pl.BlockSpec((8, LANES), lambda i: (0, i)), pl.BlockSpec((1, LANES), lambda i: (0, i))],
        out_specs=[pl.BlockSpec((4, seq, LANES), lambda i: (0, 0, i)), pl.BlockSpec((1, LANES), lambda i: (0, i)),
                   pl.BlockSpec((8, LANES), lambda i: (0, i))],
        out_shape=[jax.ShapeDtypeStruct((4, seq, d_conv), MXU_DTYPE), jax.ShapeDtypeStruct((1, d_conv), F32),
                   jax.ShapeDtypeStruct((8, d_conv), F32)],
        operands=[proj, proj, proj, proj, dyc, conv_w8, conv_b])


def _cmul(ar, ai, br, bi):
    return ar * br - ai * bi, ar * bi + ai * br


def _down(v, k):
    row = lax.broadcasted_iota(jnp.int32, v.shape, 0)
    return jnp.where(row >= k, pltpu.roll(v, k, axis=0), 0.0)


def _up(v, k):
    row = lax.broadcasted_iota(jnp.int32, v.shape, 0)
    return jnp.where(row < N_CHUNK - k, pltpu.roll(v, N_CHUNK - k, axis=0), 0.0)


def _chunk_carry(fr, fi, mr, mi, shift):
    vr, vi = shift(fr, 1), shift(fi, 1)
    for k in (1, 2, 4):
        pr, pi = _cmul(mr, mi, shift(vr, k), shift(vi, k))
        vr, vi = vr + pr, vi + pi
        mr, mi = _cmul(mr, mi, mr, mi)
    return vr, vi


def _tile(ref, j, width, part):
    return ref.at[pl.ds(pl.multiple_of(j * N_CHUNK, N_CHUNK), N_CHUNK), pl.ds(part * width, width)]


def _row(t, k):
    return jnp.broadcast_to(t[k:k + 1, :], t.shape)


def _power_table(tab_ref, ar, ai, steps, width):
    e = lax.broadcasted_iota(jnp.int32, ar.shape, 0) + 1
    rr, ri = jnp.ones_like(ar), jnp.zeros_like(ai)
    br, bi = ar, ai
    for bit in range(4):
        mr, mi = _cmul(rr, ri, br, bi)
        take = ((e >> bit) & 1) == 1
        rr, ri = jnp.where(take, mr, rr), jnp.where(take, mi, ri)
        if bit < 3:
            br, bi = _cmul(br, bi, br, bi)
    _tile(tab_ref, 0, width, 0)[...] = rr
    _tile(tab_ref, 0, width, 1)[...] = ri

    def step(m, carry):
        tr, ti = _cmul(carry[0], carry[1], br, bi)
        _tile(tab_ref, m, width, 0)[...] = tr
        _tile(tab_ref, m, width, 1)[...] = ti
        return tr, ti

    lax.fori_loop(1, steps // N_CHUNK, step, (rr, ri))


def _last_power(tab_ref, steps, width):
    shape = (N_CHUNK, width)
    return (jnp.broadcast_to(tab_ref[steps - 1:steps, 0:width], shape),
            jnp.broadcast_to(tab_ref[steps - 1:steps, width:2 * width], shape))


def _scan_fwd(s_ref, ar, ai, steps, width):
    def step(j, carry):
        sr, si = carry
        nr = ar * sr - ai * si + _tile(s_ref, j, width, 0)[...]
        ni = ar * si + ai * sr + _tile(s_ref, j, width, 1)[...]
        _tile(s_ref, j, width, 0)[...] = nr
        _tile(s_ref, j, width, 1)[...] = ni
        return nr, ni

    z = jnp.zeros((N_CHUNK, width), F32)
    return lax.fori_loop(0, steps, step, (z, z), unroll=4)


def _scan_both(s_ref, g_ref, ar, ai, steps, width):
    def step(q, carry):
        sr, si, gr, gi = carry
        j, jb = q, steps - 1 - q
        nsr = ar * sr - ai * si + _tile(s_ref, j, width, 0)[...]
        nsi = ar * si + ai * sr + _tile(s_ref, j, width, 1)[...]
        ngr = ar * gr + ai * gi + _tile(g_ref, jb, width, 0)[...]
        ngi = ar * gi - ai * gr + _tile(g_ref, jb, width, 1)[...]
        _tile(s_ref, j, width, 0)[...] = nsr
        _tile(s_ref, j, width, 1)[...] = nsi
        _tile(g_ref, jb, width, 0)[...] = ngr
        _tile(g_ref, jb, width, 1)[...] = ngi
        return nsr, nsi, ngr, ngi

    z = jnp.zeros((N_CHUNK, width), F32)
    return lax.fori_loop(0, steps, step, (z, z, z, z), unroll=2)


def _patch_fwd(s_ref, tab_ref, cr, ci, steps, width):
    def tile(m, _):
        tr, ti = _tile(tab_ref, m, width, 0)[...], _tile(tab_ref, m, width, 1)[...]
        for k in range(N_CHUNK):
            fr, fi = _cmul(_row(tr, k), _row(ti, k), cr, ci)
            j = m * N_CHUNK + k
            _tile(s_ref, j, width, 0)[...] += fr
            _tile(s_ref, j, width, 1)[...] += fi
        return 0

    lax.fori_loop(0, steps // N_CHUNK, tile, 0)


def _lam_rows(lam_ref, hh, width):
    return (jnp.broadcast_to(lam_ref[hh, 0:1, :], (N_CHUNK, width)),
            jnp.broadcast_to(lam_ref[hh, 1:2, :], (N_CHUNK, width)))


def _ssm_specs(seq, col0):
    return dict(
        col=pl.BlockSpec((seq, LANES), lambda i: (0, col0 + i)),
        lam=pl.BlockSpec((2, 2, HALF_W), lambda i: (i, 0, 0)),
        bb=pl.BlockSpec((2, HALF_CH, 2 * HALF_W), lambda i: (i, 0, 0)),
        cc=pl.BlockSpec((2, 2 * HALF_W, HALF_CH), lambda i: (i, 0, 0)),
        vec=pl.BlockSpec((1, LANES), lambda i: (0, i)),
        out=pl.BlockSpec((seq, LANES), lambda i: (0, i)),
    )


def _ssm_fwd(proj, lam, bbcat, cccat, d_skip, d_ssm, u_col0, comm=None):
    seq = proj.shape[0]
    steps = seq // N_CHUNK

    def body(u_ref, lam_ref, bb_ref, cc_ref, d_ref, yp_ref, s_ref, tab_ref):
        for hh in range(2):
            lanes = slice(HALF_CH * hh, HALF_CH * (hh + 1))
            u_half = u_ref[:, lanes].astype(F32)
            ar, ai = _lam_rows(lam_ref, hh, HALF_W)
            _power_table(tab_ref, ar, ai, steps, HALF_W)
            s_ref[...] = _dot(u_half.astype(MXU_DTYPE), bb_ref[hh])
            fr, fi = _scan_fwd(s_ref, ar, ai, steps, HALF_W)
            pr, pi = _last_power(tab_ref, steps, HALF_W)
            cr, ci = _chunk_carry(fr, fi, pr, pi, _down)
            _patch_fwd(s_ref, tab_ref, cr, ci, steps, HALF_W)
            y = _dot(s_ref[...].astype(MXU_DTYPE), cc_ref[hh])
            yp_ref[:, lanes] = y + d_ref[:, lanes] * u_half

    sp = _ssm_specs(seq, u_col0 // LANES)
    return _call(
        body, comm, name="ssm_fwd", grid=(d_ssm // LANES,),
        in_specs=[sp["col"], sp["lam"], sp["bb"], sp["cc"], sp["vec"]], out_specs=[sp["out"]],
        out_shape=[jax.ShapeDtypeStruct((seq, d_ssm), F32)],
        scratch_shapes=[pltpu.VMEM((seq, 2 * HALF_W), F32), pltpu.VMEM((steps, 2 * HALF_W), F32)],
        operands=[proj, lam, bbcat, cccat, d_skip])


def _ssm_bwd(proj, dyp, lam, bbcat, cccat, d_skip, d_ssm, u_col0, comm=None):
    seq = proj.shape[0]
    steps = seq // N_CHUNK
    n_half = 2 * d_ssm // LANES
    width = HALF_W

    def body(u_ref, dyp_ref, lam_ref, bb_ref, cc_ref, d_ref, du_ref, dbb_ref, dcc_ref, da_ref, dd_ref,
             s_ref, g_ref, tab_ref):
        n_tiles = steps // N_CHUNK
        for hh in range(2):
            lanes = slice(HALF_CH * hh, HALF_CH * (hh + 1))
            u_half, dy_half = u_ref[:, lanes].astype(F32), dyp_ref[:, lanes].astype(F32)
            dy_mx = dy_half.astype(MXU_DTYPE)
            ar, ai = _lam_rows(lam_ref, hh, width)
            _power_table(tab_ref, ar, ai, steps, width)
            s_ref[...] = _dot(u_half.astype(MXU_DTYPE), bb_ref[hh])
            g_ref[...] = _dot_nt(dy_mx, cc_ref[hh])
            fr, fi, lr_, li_ = _scan_both(s_ref, g_ref, ar, ai, steps, width)
            pr, pi = _last_power(tab_ref, steps, width)
            cr, ci = _chunk_carry(fr, fi, pr, pi, _down)
            gr, gi = _chunk_carry(lr_, li_, pr, -pi, _up)

            def tile(m, carry):
                sr, si, accr, acci = carry
                t1r, t1i = _tile(tab_ref, m, width, 0)[...], _tile(tab_ref, m, width, 1)[...]
                mb = n_tiles - 1 - m
                t2r, t2i = _tile(tab_ref, mb, width, 0)[...], _tile(tab_ref, mb, width, 1)[...]
                for k in range(N_CHUNK):
                    j = m * N_CHUNK + k
                    xr, xi = _cmul(_row(t1r, k), _row(t1i, k), cr, ci)
                    nsr = _tile(s_ref, j, width, 0)[...] + xr
                    nsi = _tile(s_ref, j, width, 1)[...] + xi
                    _tile(s_ref, j, width, 0)[...] = nsr
                    _tile(s_ref, j, width, 1)[...] = nsi
                    qr, qi = _row(t2r, N_CHUNK - 1 - k), _row(t2i, N_CHUNK - 1 - k)
                    ngr = _tile(g_ref, j, width, 0)[...] + (qr * gr + qi * gi)
                    ngi = _tile(g_ref, j, width, 1)[...] + (qr * gi - qi * gr)
                    _tile(g_ref, j, width, 0)[...] = ngr
                    _tile(g_ref, j, width, 1)[...] = ngi
                    accr = accr + (sr * ngr + si * ngi)
                    acci = acci + (sr * ngi - si * ngr)
                    sr, si = nsr, nsi
                return sr, si, accr, acci

            z = jnp.zeros((N_CHUNK, width), F32)
            _, _, accr, acci = lax.fori_loop(0, n_tiles, tile, (cr, ci, z, z))
            da_ref[hh, :, 0:width] = jnp.sum(accr, axis=0, keepdims=True)
            da_ref[hh, :, width:2 * width] = jnp.sum(acci, axis=0, keepdims=True)

            g_mx = g_ref[...].astype(MXU_DTYPE)
            dcc_ref[hh] = _dot_tn(dy_mx, s_ref[...].astype(MXU_DTYPE)).T
            dbb_ref[hh] = _dot_tn(u_half.astype(MXU_DTYPE), g_mx)
            du = _dot_nt(g_mx, bb_ref[hh]) + d_ref[:, lanes] * dy_half
            du_ref[:, lanes] = du.astype(du_ref.dtype)
            dd_ref[:, lanes] = jnp.sum(dy_half * u_half, axis=0, keepdims=True)

    sp = _ssm_specs(seq, u_col0 // LANES)
    return _call(
        body, comm, name="ssm_bwd", grid=(d_ssm // LANES,),
        in_specs=[sp["col"], sp["out"], sp["lam"], sp["bb"], sp["cc"], sp["vec"]],
        out_specs=[sp["out"], sp["bb"], sp["cc"], pl.BlockSpec((2, 1, 2 * width), lambda i: (i, 0, 0)), sp["vec"]],
        out_shape=[jax.ShapeDtypeStruct((seq, d_ssm), MXU_DTYPE),
                   jax.ShapeDtypeStruct((n_half, HALF_CH, 2 * width), F32),
                   jax.ShapeDtypeStruct((n_half, 2 * width, HALF_CH), F32),
                   jax.ShapeDtypeStruct((n_half, 1, 2 * width), F32),
                   jax.ShapeDtypeStruct((1, d_ssm), F32)],
        scratch_shapes=[pltpu.VMEM((seq, 2 * width), F32), pltpu.VMEM((seq, 2 * width), F32),
                        pltpu.VMEM((steps, 2 * width), F32)],
        operands=[proj, dyp, lam, bbcat, cccat, d_skip])


def _tail(xp, t3, proj, yconv, yp, w_glu, b_glu, w_out, g_post, zs_col0, tm):
    seq, d_model = xp.shape
    d_conv, d_ssm = yconv.shape[1], yp.shape[1]
    d_mix = d_conv + d_ssm
    assert zs_col0 % d_ssm == 0

    def body(x_ref, t_ref, zs_ref, yc_ref, yp_ref, wglu_hbm, bglu_ref, wout_hbm, gpost_ref,
             dy_ref, do_ref, mixt_ref, dyc_ref, dyp_ref, dzs_ref, ygt_ref, dq_ref, loss_ref, dgpost_ref, dbglu_ref,
             wglu, wout):
        @pl.when(pl.program_id(0) == 0)
        def _():
            pltpu.sync_copy(wglu_hbm, wglu)
            pltpu.sync_copy(wout_hbm, wout)
            loss_ref[...] = jnp.zeros_like(loss_ref)
            dgpost_ref[...] = jnp.zeros_like(dgpost_ref)
            dbglu_ref[...] = jnp.zeros_like(dbglu_ref)

        a = yp_ref[...]
        th = jnp.tanh(GELU_C * (a + GELU_K * (a * a * a)))
        yg = a * (0.5 * (1.0 + th))
        dgelu = 0.5 * (1.0 + th) + 0.5 * a * (1.0 - th * th) * (GELU_C * (1.0 + 3.0 * GELU_K * a * a))
        yg_mx = yg.astype(MXU_DTYPE)
        sq = _sigmoid(_dot(yg_mx, wglu[...]) + bglu_ref[...])
        y2 = yg * sq
        zs = zs_ref[...].astype(F32)
        sz = _sigmoid(zs)
        silz = zs * sz
        yc, ys = yc_ref[...].astype(F32), y2 * silz
        mix = jnp.concatenate([yc, ys], axis=1).astype(MXU_DTYPE)
        mixt_ref[0:d_conv, :] = yc.T.astype(MXU_DTYPE)
        mixt_ref[d_conv:, :] = ys.T.astype(MXU_DTYPE)
        o = _dot(mix, wout[...])
        r2 = lax.rsqrt(jnp.mean(o * o, axis=-1, keepdims=True) + EPS)
        on = o * r2
        gpost = gpost_ref[...]
        err = (x_ref[...] + on * gpost) - _interleave(t_ref[...])
        loss_ref[...] += 0.5 * jnp.sum(jnp.mean(err * err, axis=-1, keepdims=True), axis=0, keepdims=True)
        dy = err * (1.0 / d_model)
        dy_ref[...] = dy
        dgpost_ref[...] += jnp.sum(dy * on, axis=0, keepdims=True)
        d_on = dy * gpost
        d_o = r2 * (d_on - on * jnp.mean(d_on * on, axis=-1, keepdims=True))
        do_mx = d_o.astype(MXU_DTYPE)
        do_ref[...] = do_mx
        d_mix_ = _dot_nt(do_mx, wout[...])
        dyc_ref[...] = d_mix_[:, :d_conv].astype(dyc_ref.dtype)
        d_yssm = d_mix_[:, d_conv:]
        d_y2 = d_yssm * silz
        dzs_ref[...] = (d_yssm * y2 * (sz * (1.0 + zs * (1.0 - sz)))).astype(dzs_ref.dtype)
        d_q = d_y2 * yg * (sq * (1.0 - sq))
        dq_mx = d_q.astype(MXU_DTYPE)
        dq_ref[...] = dq_mx
        ygt_ref[...] = yg.T.astype(MXU_DTYPE)
        dbglu_ref[...] += jnp.sum(d_q, axis=0, keepdims=True)
        d_yg = d_y2 * sq + _dot_nt(dq_mx, wglu[...])
        dyp_ref[...] = (d_yg * dgelu).astype(dyp_ref.dtype)

    def rows(width, col=0):
        return pl.BlockSpec((tm, width), lambda i: (i, col))

    def fixed(width):
        return pl.BlockSpec((1, width), lambda i: (0, 0))

    def cols(height):
        return pl.BlockSpec((height, tm), lambda i: (0, i))

    any_ = pl.BlockSpec(memory_space=pl.ANY)
    return pl.pallas_call(
        body, name="tail", grid=(seq // tm,),
        in_specs=[rows(d_model), _chunk_block(tm, d_model), rows(d_ssm, zs_col0 // d_ssm), rows(d_conv), rows(d_ssm),
                  any_, fixed(d_ssm), any_, fixed(d_model)],
        out_specs=[rows(d_model), rows(d_model), cols(d_mix), rows(d_conv), rows(d_ssm), rows(d_ssm), cols(d_ssm),
                   rows(d_ssm), fixed(LANES), fixed(d_model), fixed(d_ssm)],
        out_shape=[jax.ShapeDtypeStruct((seq, d_model), F32), jax.ShapeDtypeStruct((seq, d_model), MXU_DTYPE),
                   jax.ShapeDtypeStruct((d_mix, seq), MXU_DTYPE), jax.ShapeDtypeStruct((seq, d_conv), MXU_DTYPE),
                   jax.ShapeDtypeStruct((seq, d_ssm), MXU_DTYPE), jax.ShapeDtypeStruct((seq, d_ssm), MXU_DTYPE),
                   jax.ShapeDtypeStruct((d_ssm, seq), MXU_DTYPE), jax.ShapeDtypeStruct((seq, d_ssm), MXU_DTYPE),
                   jax.ShapeDtypeStruct((1, LANES), F32), jax.ShapeDtypeStruct((1, d_model), F32),
                   jax.ShapeDtypeStruct((1, d_ssm), F32)],
        scratch_shapes=[pltpu.VMEM(w_glu.shape, MXU_DTYPE), pltpu.VMEM(w_out.shape, MXU_DTYPE)],
        compiler_params=_params(("arbitrary",)),
    )(xp, t3, proj, yconv, yp, w_glu, b_glu, w_out, g_post)


def _bwd_in(d4, du, dzs, gr, win_g, xp, dy, g_pre, comm, tm):
    seq, d_model = xp.shape
    nb, n_p, _, gw = win_g.shape
    nc = n_p * gw
    per = d4.shape[2] // gr

    def body(d4_ref, du_ref, dzs_ref, w_hbm, x_ref, dy_ref, g_ref, gx_ref, dg_ref, w_all, w_sems):
        def granule(g):
            p, cols = g // per, slice(g % per * gr, (g % per + 1) * gr)
            if p < 4:
                return d4_ref[p, :, cols]
            return du_ref[:, cols] if p == 4 else dzs_ref[:, cols]

        i = pl.program_id(0)
        loads = [[pltpu.make_async_copy(w_hbm.at[k, r], w_all.at[k, :, pl.ds(r * gw, gw)], w_sems.at[k, r])
                  for r in range(n_p)] for k in range(nb)]

        @pl.when(i == 0)
        def _():
            dg_ref[...] = jnp.zeros_like(dg_ref)
            for row in loads:
                for cp in row:
                    cp.start()

        dh = None
        for k in range(nb):
            @pl.when(i == 0)
            def _():
                for cp in loads[k]:
                    cp.wait()

            dp = jnp.concatenate([granule(g) for g in range(k * nc // gr, (k + 1) * nc // gr)], axis=1)
            part = _dot_nt(dp, w_all[k])
            dh = part if dh is None else dh + part

        x = x_ref[...]
        r = lax.rsqrt(jnp.mean(x * x, axis=-1, keepdims=True) + EPS)
        xn = x * r
        dg_ref[...] += jnp.sum(dh * xn, axis=0, keepdims=True)
        dxn = dh * g_ref[...]
        gx_ref[...] = r * (dxn - xn * jnp.mean(dxn * xn, axis=-1, keepdims=True)) + dy_ref[...]

    row = pl.BlockSpec((tm, d_model), lambda i: (i, 0))
    vec = pl.BlockSpec((1, d_model), lambda i: (0, 0))
    return _call(
        body, comm, name="bwd_in", grid=(seq // tm,),
        in_specs=[pl.BlockSpec((4, tm, d4.shape[2]), lambda i: (0, i, 0)),
                  pl.BlockSpec((tm, du.shape[1]), lambda i: (i, 0)), pl.BlockSpec((tm, dzs.shape[1]), lambda i: (i, 0)),
                  pl.BlockSpec(memory_space=pl.ANY), row, row, vec],
        out_specs=[row, vec],
        out_shape=[jax.ShapeDtypeStruct((seq, d_model), F32), jax.ShapeDtypeStruct((1, d_model), F32)],
        scratch_shapes=[pltpu.VMEM((nb, d_model, nc), win_g.dtype), pltpu.SemaphoreType.DMA((nb, n_p))],
        operands=[d4, du, dzs, win_g, xp, dy, g_pre])


def _lookup(g, table):
    out = jnp.int32(table[0])
    for gi in range(1, len(table)):
        if table[gi] != table[gi - 1]:
            out = jnp.where(g >= gi, jnp.int32(table[gi]), out)
    return out


def _held(values, used):
    cur = next(v for v, u in zip(values, used) if u)
    out = []
    for v, u in zip(values, used):
        cur = v if u else cur
        out.append(cur)
    return out


def _dw_in(name, ht, d4, du, dzs, granules, gr, nc, tm, comm=None):
    d_model, seq = ht.shape
    per = d4.shape[2] // gr
    piece, col = [g // per for g in granules], [g % per for g in granules]
    sources = [(d4, [p < 4 for p in piece]), (du, [p == 4 for p in piece]), (dzs, [p == 5 for p in piece])]
    sources = [(a, used) for a, used in sources if any(used)]
    select = [next(s for s, (_, used) in enumerate(sources) if used[q]) for q in range(len(granules))]
    owner, place = [g * gr // nc for g in granules], [g * gr % nc // gr for g in granules]

    def body(a_ref, *refs):
        src_refs, o_ref = refs[:-1], refs[-1]
        j = pl.program_id(1)
        for s, ref in enumerate(src_refs):
            @pl.when(_lookup(j, select) == s)
            def _():
                o_ref[...] = _dot(a_ref[...], ref[...]).astype(o_ref.dtype)

    in_specs = [pl.BlockSpec((tm, seq), lambda i, j: (i, 0))]
    for a, used in sources:
        cols = _held(col, used)
        if a.ndim == 3:
            rows = _held(piece, used)
            in_specs.append(pl.BlockSpec((None, seq, gr), functools.partial(
                lambda i, j, rows, cols: (_lookup(j, rows), 0, _lookup(j, cols)), rows=rows, cols=cols)))
        else:
            in_specs.append(pl.BlockSpec((seq, gr), functools.partial(
                lambda i, j, cols: (0, _lookup(j, cols)), cols=cols)))
    return _call(
        body, comm, name=name, grid=(d_model // tm, len(granules)), in_specs=in_specs,
        out_specs=[pl.BlockSpec((None, tm, gr), lambda i, j: (_lookup(j, owner), i, _lookup(j, place)))],
        out_shape=[jax.ShapeDtypeStruct((N_DEV, d_model, nc), MXU_DTYPE)],
        operands=[ht] + [a for a, _ in sources])


def _wgrad(name, at, b, tm, tn, out_shape, out_block, out_index, comm=None):
    m, seq = at.shape
    n = b.shape[1]

    def body(a_ref, b_ref, o_ref):
        o_ref[...] = _dot(a_ref[...], b_ref[...]).astype(o_ref.dtype)

    return _call(
        body, comm, name=name, grid=(n // tn, m // tm),
        in_specs=[pl.BlockSpec((tm, seq), lambda j, i: (i, 0)), pl.BlockSpec((seq, tn), lambda j, i: (0, j))],
        out_specs=[pl.BlockSpec(out_block, lambda j, i: out_index(i, j))],
        out_shape=[jax.ShapeDtypeStruct(out_shape, MXU_DTYPE)],
        operands=[at, b])


def _eye_g():
    return jnp.eye(HALF_G, dtype=F32)


def _bb_blockdiag(bbt_r, bbt_i):
    n_half = bbt_r.shape[0] // HALF_G

    def one(t):
        t = t.reshape(n_half, HALF_G, SSM_GROUP, SSM_STATE)
        t = t[:, :, :, None, :] * _eye_g()[None, :, None, :, None]
        return t.reshape(n_half, HALF_CH, HALF_W)

    return jnp.concatenate([one(bbt_r), one(bbt_i)], axis=-1)


def _cc_blockdiag(c_re, c_im):
    n_half = c_re.shape[0] // HALF_G

    def one(t):
        t = t.reshape(n_half, HALF_G, SSM_GROUP, SSM_STATE)
        t = jnp.transpose(t, (0, 3, 1, 2))
        t = t[:, None, :, :, :] * _eye_g()[None, :, None, :, None]
        return t.reshape(n_half, HALF_W, HALF_CH)

    return jnp.concatenate([one(c_re), one(-c_im)], axis=1)


def _bb_diag(dbb):
    n_half = dbb.shape[0]
    t = dbb.reshape(n_half, HALF_G, SSM_GROUP, 2, HALF_G, SSM_STATE)
    t = jnp.sum(t * _eye_g()[None, :, None, None, :, None], axis=4)
    t = jnp.transpose(t, (3, 0, 1, 2, 4))
    return t.reshape(2, n_half * HALF_G, SSM_GROUP, SSM_STATE)


def _cc_diag(dcc):
    n_half = dcc.shape[0]
    t = dcc.reshape(n_half, 2, HALF_G, SSM_STATE, HALF_G, SSM_GROUP)
    t = jnp.sum(t * _eye_g()[None, None, :, None, :, None], axis=2)
    t = jnp.transpose(t, (1, 0, 3, 4, 2))
    return t.reshape(2, n_half * HALF_G, SSM_GROUP, SSM_STATE)


def _unpermute_rows(a):
    seq, d = a.shape
    return a.reshape(seq // N_CHUNK, N_CHUNK, d).transpose(1, 0, 2).reshape(seq, d)


def _pack_rows(shape):
    return -(-math.prod(shape) // (8 * LANES)) * 8


def _pack(parts, dtype=F32):
    rows = []
    for p in parts:
        flat = p.reshape(-1).astype(dtype)
        rows.append(jnp.pad(flat, (0, _pack_rows(p.shape) * LANES - flat.shape[0])).reshape(-1, LANES))
    return jnp.concatenate(rows, axis=0)


def _unpack(packed, shapes):
    out, o = [], 0
    for s in shapes:
        n = _pack_rows(s)
        out.append(packed[o:o + n].reshape(-1)[:math.prod(s)].reshape(s))
        o += n
    return out


def kernel(x, norm_pre_g, w_in, conv_w, conv_b, ssm_a_re, ssm_a_im, ssm_log_dt, ssm_b_re, ssm_b_im, ssm_c_re, ssm_c_im, ssm_d, w_glu, b_glu, w_out, norm_post_g, loss_target, m_norm_pre_g, m_w_in, m_conv_w, m_conv_b, m_ssm_a_re, m_ssm_a_im, m_ssm_log_dt, m_ssm_b_re, m_ssm_b_im, m_ssm_c_re, m_ssm_c_im, m_ssm_d, m_w_glu, m_b_glu, m_w_out, m_norm_post_g, v_norm_pre_g, v_w_in, v_conv_w, v_conv_b, v_ssm_a_re, v_ssm_a_im, v_ssm_log_dt, v_ssm_b_re, v_ssm_b_im, v_ssm_c_re, v_ssm_c_im, v_ssm_d, v_w_glu, v_b_glu, v_w_out, v_norm_post_g):
    seq, d_model = x.shape[1], x.shape[2]
    d_conv, d_ssm = conv_b.shape[0], ssm_d.shape[0]
    groups, states = ssm_a_re.shape
    assert x.shape[0] == 1 and seq % (8 * N_CHUNK) == 0 and d_conv == d_ssm
    assert (groups, states) == (d_ssm // SSM_GROUP, SSM_STATE) and d_ssm % LANES == 0
    me = 4 * lax.axis_index("x") + 2 * lax.axis_index("y") + lax.axis_index("c")
    tm = min(512, seq)

    x3 = x[0].reshape(N_CHUNK, seq // N_CHUNK, d_model)
    t3 = loss_target[0].reshape(N_CHUNK, seq // N_CHUNK, d_model)
    row = lambda a: a.reshape(1, -1)
    conv_w8 = jnp.pad(conv_w, ((0, 8 - conv_w.shape[0]), (0, 0)))

    g3 = lambda a: a.reshape(groups, 1, -1)
    bt_re, bt_im = jnp.transpose(ssm_b_re, (0, 2, 1)), jnp.transpose(ssm_b_im, (0, 2, 1))
    lbr, lbi, qr, qi, bbt_r, bbt_i = _ssm_prep(g3(ssm_a_re), g3(ssm_a_im), g3(ssm_log_dt), bt_re, bt_im)
    n_half = groups // HALF_G
    lam = jnp.stack([lbr.reshape(n_half, HALF_W), lbi.reshape(n_half, HALF_W)], axis=1)
    bbcat = _bb_blockdiag(bbt_r, bbt_i).astype(MXU_DTYPE)
    cccat = _cc_blockdiag(ssm_c_re, ssm_c_im).astype(MXU_DTYPE)

    xp, h, ht = _norm_in(x3, row(norm_pre_g), tm)
    nc = w_in.shape[1]
    w_pieces = jnp.transpose(w_in.astype(MXU_DTYPE).reshape(d_model, nc // PIECE_COLS, PIECE_COLS), (1, 0, 2))
    proj, win_g, (convw_g, wglu_g) = _fwd_in(h, w_pieces, me, _Comm([conv_w8, w_glu.astype(MXU_DTYPE)]),
                                             min(2048, seq))
    conv_w_full = jnp.transpose(convw_g, (1, 0, 2)).reshape(8, d_conv)
    u_col0, zs_col0 = 4 * d_conv, 4 * d_conv + d_ssm
    yconv = _conv_fwd(proj, conv_w_full, row(conv_b), d_conv)
    (yp,), (wout_g,) = _ssm_fwd(proj, lam, bbcat, cccat, row(ssm_d), d_ssm, u_col0,
                                _Comm([w_out.astype(MXU_DTYPE)]))
    w_out_full = wout_g.reshape(-1, d_model)
    w_glu_full = wglu_g.reshape(-1, d_ssm)
    (dy, d_o, mixt, dyc, dyp, dzs, ygt, dq, loss_part, dgpost, dbglu) = _tail(
        xp, t3, proj, yconv, yp, w_glu_full, row(b_glu), w_out_full, row(norm_post_g), zs_col0, min(256, seq))

    r_out, r_glu, nc = w_out.shape[0], w_glu.shape[0], w_in.shape[1]
    (dwout_p,), _ = _wgrad("dw_out", mixt, d_o, r_out, min(1024, d_model), (N_DEV, r_out, d_model),
                           (None, r_out, min(1024, d_model)), lambda i, j: (i, 0, j))
    (dwglu_p,), _ = _wgrad("dw_glu", ygt, dq, r_glu, d_ssm, (N_DEV, r_glu, d_ssm),
                           (None, r_glu, d_ssm), lambda i, j: (i, 0, 0))
    (d4, dconvb, dconvw), (recv_glu,) = _conv_bwd(proj, dyc, conv_w_full, row(conv_b), d_conv,
                                                  _Comm([], [dwglu_p]))
    late = [k for k in range(N_DEV) if k * nc < u_col0 + d_ssm and (k + 1) * nc > u_col0]
    early = [k for k in range(N_DEV) if k not in late]
    gr = math.gcd(nc, d_conv)
    granules = lambda blocks: [g for k in blocks for g in range(k * nc // gr, (k + 1) * nc // gr)]
    tmw = min(1024, d_model)
    (dwin_e,), (recv_out,) = _dw_in("dw_in_early", ht, d4, None, dzs, granules(early), gr, nc, tmw,
                                    _Comm([], [dwout_p]))
    (du, dbb, dcc, da, dd), (recv_in,) = _ssm_bwd(
        proj, dyp, lam, bbcat, cccat, row(ssm_d), d_ssm, u_col0, _Comm([], [dwin_e], dests={0: early}))
    parts_mx = [_bb_diag(dbb), _cc_diag(dcc)]
    (dwin_l,), (pack_mx_g,) = _dw_in("dw_in_late", ht, d4, du, dzs, granules(late), gr, nc, tmw,
                                     _Comm([_pack(parts_mx, MXU_DTYPE)]))
    da_n = jnp.transpose(da.reshape(n_half, 2, HALF_G, SSM_STATE), (1, 0, 2, 3)).reshape(2, groups, 1, states)
    parts = [dgpost, dconvb, dd, dbglu, dconvw[:3], da_n, loss_part]
    shapes, shapes_mx = [p.shape for p in parts], [p.shape for p in parts_mx]
    (gx_p, dgpre), (pack_g, recv_in) = _bwd_in(
        d4, du, dzs, gr, win_g, xp, dy, row(norm_pre_g),
        _Comm([_pack(parts)], [dwin_l], dests={1: late}, into={1: recv_in}), min(256, seq))
    (last_g,) = _exchange("reduce_last", [_pack([dgpre])], [])
    (g_gpost, g_convb, g_d, g_bglu, g_convw, g_da, loss_sum) = _unpack(_sum_slots("sum_pack", pack_g), shapes)
    (g_dbb, g_dcc) = _unpack(_sum_slots("sum_pack_mx", pack_mx_g), shapes_mx)
    (g_gpre,) = _unpack(_sum_slots("sum_last", last_g), [dgpre.shape])
    g_convw = lax.dynamic_slice(g_convw, (0, me * conv_w.shape[1]), conv_w.shape)

    tr = lambda a: jnp.transpose(a, (0, 2, 1))
    direct = [(g_gpre, row(norm_pre_g), row(m_norm_pre_g), row(v_norm_pre_g)),
              (g_convb, row(conv_b), row(m_conv_b), row(v_conv_b)),
              (g_d, row(ssm_d), row(m_ssm_d), row(v_ssm_d)),
              (g_bglu, row(b_glu), row(m_b_glu), row(v_b_glu)),
              (g_gpost, row(norm_post_g), row(m_norm_post_g), row(v_norm_post_g)),
              (g_convw, conv_w, m_conv_w, v_conv_w),
              (g_dcc[0], ssm_c_re, m_ssm_c_re, v_ssm_c_re),
              (-g_dcc[1], ssm_c_im, m_ssm_c_im, v_ssm_c_im)]
    ssm = dict(da_r=g_da[0], da_i=g_da[1], dbb_r=g_dbb[0], dbb_i=g_dbb[1], lr=g3(ssm_a_re), li=g3(ssm_a_im),
               ldt=g3(ssm_log_dt), bt_r=bt_re, bt_i=bt_im, lbr=lbr, lbi=lbi, qr=qr, qi=qi,
               w_a_re=g3(ssm_a_re), m_a_re=g3(m_ssm_a_re), v_a_re=g3(v_ssm_a_re),
               w_a_im=g3(ssm_a_im), m_a_im=g3(m_ssm_a_im), v_a_im=g3(v_ssm_a_im),
               w_log_dt=g3(ssm_log_dt), m_log_dt=g3(m_ssm_log_dt), v_log_dt=g3(v_ssm_log_dt),
               w_bt_re=bt_re, m_bt_re=tr(m_ssm_b_re), v_bt_re=tr(v_ssm_b_re),
               w_bt_im=bt_im, m_bt_im=tr(m_ssm_b_im), v_bt_im=tr(v_ssm_b_im))
    small = _small_update(direct, ssm)
    res = {}
    for name, quad, shape in zip(["norm_pre_g", "conv_b", "ssm_d", "b_glu", "norm_post_g", "conv_w", "ssm_c_re", "ssm_c_im"],
                                 small[:8], [norm_pre_g.shape, conv_b.shape, ssm_d.shape, b_glu.shape,
                                             norm_post_g.shape, conv_w.shape, ssm_c_re.shape, ssm_c_im.shape]):
        res[name] = tuple(a.reshape(shape) for a in quad)
    res["ssm_a_re"] = tuple(a.reshape(ssm_a_re.shape) for a in small[8])
    res["ssm_a_im"] = tuple(a.reshape(ssm_a_im.shape) for a in small[9])
    res["ssm_log_dt"] = tuple(a.reshape(ssm_log_dt.shape) for a in small[10])
    res["ssm_b_re"] = tuple(tr(a) for a in small[11])
    res["ssm_b_im"] = tuple(tr(a) for a in small[12])
    res["w_in"] = tuple(_adam_big("adam_w_in", recv_in, w_in, m_w_in, v_w_in, min(256, d_model)))
    res["w_out"] = tuple(_adam_big("adam_w_out", recv_out, w_out, m_w_out, v_w_out, min(128, r_out)))
    res["w_glu"] = tuple(_adam_big("adam_w_glu", recv_glu, w_glu, m_w_glu, v_w_glu, r_glu))

    order = ["norm_pre_g", "w_in", "conv_w", "conv_b", "ssm_a_re", "ssm_a_im", "ssm_log_dt", "ssm_b_re", "ssm_b_im",
             "ssm_c_re", "ssm_c_im", "ssm_d", "w_glu", "b_glu", "w_out", "norm_post_g"]
    loss = loss_sum[0, 0]
    grad_x = _unpermute_rows(gx_p)[None]
    return (loss, grad_x, *[res[n][0] for n in order], *[res[n][1] for n in order],
            *[res[n][2] for n in order], *[res[n][3] for n in order])
```

```python
import functools
import math

import jax
import jax.numpy as jnp
from jax import lax
from jax.experimental import pallas as pl
from jax.experimental.pallas import tpu as pltpu

F32 = jnp.float32
MXU_DTYPE = jnp.bfloat16
PROJ_DTYPE = jnp.bfloat16
AXES = ("x", "y", "c")
N_DEV = 8
N_CHUNK = 8
LANES = 128
SSM_GROUP = 16
SSM_STATE = 64
HALF_CH = 64
HALF_G = HALF_CH // SSM_GROUP
HALF_W = HALF_G * SSM_STATE
EPS = 1e-6
ADAM_LR, ADAM_B1, ADAM_B2, ADAM_EPS, ADAM_WD, ADAM_STEP = 0.001, 0.9, 0.999, 1e-08, 0.01, 10
GELU_C = math.sqrt(2.0 / math.pi)
GELU_K = 0.044715
VMEM_LIMIT = 56 * 1024 * 1024


def _params(sem=None):
    return pltpu.CompilerParams(dimension_semantics=sem, vmem_limit_bytes=VMEM_LIMIT)


def _dot(a, b):
    return jnp.dot(a, b, preferred_element_type=F32)


def _dot_nt(a, b):
    return lax.dot_general(a, b, (((1,), (1,)), ((), ())), preferred_element_type=F32)


def _dot_tn(a, b):
    return lax.dot_general(a, b, (((0,), (0,)), ((), ())), preferred_element_type=F32)


def _sigmoid(z):
    return 1.0 / (1.0 + jnp.exp(-z))


def _flip(v, bit):
    return 1 - v if bit else v


def _peers():
    x, y, c = (lax.axis_index(a) for a in AXES)
    out = []
    for m in range(1, N_DEV):
        px, py, pc = _flip(x, (m >> 2) & 1), _flip(y, (m >> 1) & 1), _flip(c, m & 1)
        out.append((px, py, pc, 4 * px + 2 * py + pc))
    return out


class _Comm:
    def __init__(self, gathers=(), scatters=(), dests=None, into=None):
        self.n_g = len(gathers)
        self.operands = list(gathers) + list(scatters)
        self.n = len(self.operands)
        self.dests = dests or {}
        self.into = into or {}

    def out_shape(self):
        return [jax.ShapeDtypeStruct((N_DEV,) + a.shape if t < self.n_g else a.shape, a.dtype)
                for t, a in enumerate(self.operands)]

    def scratch(self):
        if not self.n:
            return []
        return [pltpu.SemaphoreType.DMA((self.n, N_DEV - 1)), pltpu.SemaphoreType.DMA((self.n, N_DEV - 1)),
                pltpu.SemaphoreType.DMA((self.n,))]

    def _copies(self, in_refs, out_refs, sems, arrivals):
        send_sems, recv_sems, local_sems = sems
        x, y, c = (lax.axis_index(a) for a in AXES)
        me = 4 * x + 2 * y + c

        def src(t, dev):
            return in_refs[t] if t < self.n_g else in_refs[t].at[dev]

        def member(t, dev):
            if t not in self.dests:
                return None
            return functools.reduce(jnp.logical_or, [dev == d for d in self.dests[t]])

        local = [(member(t, me), pltpu.make_async_copy(src(t, me), out_refs[t].at[me], local_sems.at[t]))
                 for t in range(self.n)]
        sends, recvs = [], []
        for t in range(self.n):
            for m, (px, py, pc, peer) in enumerate(_peers()):
                kw = dict(send_sem=send_sems.at[t, m], recv_sem=recv_sems.at[t, m],
                          device_id=(px, py, pc), device_id_type=pl.DeviceIdType.MESH)
                sends.append((member(t, peer), pltpu.make_async_remote_copy(
                    src_ref=src(t, peer), dst_ref=out_refs[t].at[me], **kw)))
                if arrivals:
                    recvs.append((member(t, me), pltpu.make_async_remote_copy(
                        src_ref=src(t, peer), dst_ref=out_refs[t].at[peer], **kw)))
        return local, sends, recvs

    @staticmethod
    def _do(cond, action):
        if cond is None:
            action()
        else:
            pl.when(cond)(action)

    def start(self, in_refs, out_refs, sems):
        local, sends, _ = self._copies(in_refs, out_refs, sems, arrivals=False)
        for cond, cp in local + sends:
            self._do(cond, cp.start)

    def finish(self, in_refs, out_refs, sems):
        local, sends, recvs = self._copies(in_refs, out_refs, sems, arrivals=True)
        for cond, cp in recvs:
            self._do(cond, cp.wait_recv)
        for cond, cp in sends:
            self._do(cond, cp.wait_send)
        for cond, cp in local:
            self._do(cond, cp.wait)


def _call(body, comm, *, name, grid, in_specs, out_specs, out_shape, operands, scratch_shapes=()):
    comm = comm or _Comm()
    n_in, n_out, n_scr, cn = len(in_specs), len(out_specs), len(scratch_shapes), comm.n
    landing = sorted(comm.into)
    aliases = {n_in + cn + q: n_out + t for q, t in enumerate(landing)}

    def wrapped(*refs):
        parts, o = [], 0
        for k in (n_in, cn, len(landing), n_out, cn, n_scr):
            parts.append(refs[o:o + k])
            o += k
        h_in, c_in, _, h_out, c_out, h_scr = parts
        sems = refs[o:]
        if cn:
            first = functools.reduce(jnp.logical_and, [pl.program_id(d) == 0 for d in range(len(grid))])

            @pl.when(first)
            def _():
                comm.start(c_in, c_out, sems)

        body(*h_in, *h_out, *h_scr)
        if cn:
            last = functools.reduce(jnp.logical_and, [pl.program_id(d) == grid[d] - 1 for d in range(len(grid))])

            @pl.when(last)
            def _():
                comm.finish(c_in, c_out, sems)

    any_ = pl.BlockSpec(memory_space=pl.ANY)
    res = pl.pallas_call(
        wrapped, name=name, grid=grid, in_specs=list(in_specs) + [any_] * (cn + len(landing)),
        out_specs=list(out_specs) + [any_] * cn,
        out_shape=list(out_shape) + comm.out_shape(), scratch_shapes=list(scratch_shapes) + comm.scratch(),
        input_output_aliases=aliases, compiler_params=_params(("arbitrary",) * len(grid)),
    )(*operands, *comm.operands, *[comm.into[t] for t in landing])
    return list(res[:n_out]), list(res[n_out:])


def _exchange(name, gathers, scatters):
    def body(tok_ref):
        tok_ref[...] = jnp.zeros_like(tok_ref)

    return _call(body, _Comm(gathers, scatters), name=name, grid=(1,), in_specs=[],
                 out_specs=[pl.BlockSpec((8, LANES), lambda i: (0, 0))],
                 out_shape=[jax.ShapeDtypeStruct((8, LANES), F32)], operands=[])[1]


def _ssm_prep(a_re, a_im, log_dt, bt_re, bt_im):
    def body(lr_ref, li_ref, ldt_ref, br_ref, bi_ref, lbr_ref, lbi_ref, qr_ref, qi_ref, bbr_ref, bbi_ref):
        lr, li = lr_ref[...], li_ref[...]
        dt = jnp.exp(ldt_ref[...])
        mag = jnp.exp(lr * dt)
        lbr, lbi = mag * jnp.cos(li * dt), mag * jnp.sin(li * dt)
        nr, ni = lbr - 1.0, lbi
        den = lr * lr + li * li
        qr = (nr * lr + ni * li) / den
        qi = (ni * lr - nr * li) / den
        br, bi = br_ref[...], bi_ref[...]
        lbr_ref[...], lbi_ref[...], qr_ref[...], qi_ref[...] = lbr, lbi, qr, qi
        bbr_ref[...] = qr * br - qi * bi
        bbi_ref[...] = qr * bi + qi * br

    s2 = jax.ShapeDtypeStruct(a_re.shape, F32)
    s3 = jax.ShapeDtypeStruct(bt_re.shape, F32)
    return pl.pallas_call(body, name="ssm_prep", out_shape=[s2, s2, s2, s2, s3, s3],
                          compiler_params=_params())(a_re, a_im, log_dt, bt_re, bt_im)


def _adam(w, g, m, v):
    m2 = ADAM_B1 * m + (1.0 - ADAM_B1) * g
    v2 = ADAM_B2 * v + (1.0 - ADAM_B2) * (g * g)
    m_hat = m2 / (1.0 - ADAM_B1 ** ADAM_STEP)
    v_hat = v2 / (1.0 - ADAM_B2 ** ADAM_STEP)
    delta = -ADAM_LR * (m_hat / (jnp.sqrt(v_hat) + ADAM_EPS) + ADAM_WD * w)
    return delta, m2, v2


def _small_update(direct, ssm):
    n_direct = len(direct)
    flat = [a for quad in direct for a in quad]
    names = ["da_r", "da_i", "dbb_r", "dbb_i", "lr", "li", "ldt", "bt_r", "bt_i", "lbr", "lbi", "qr", "qi"]
    flat += [ssm[k] for k in names]
    chain = ["a_re", "a_im", "log_dt", "bt_re", "bt_im"]
    for k in chain:
        flat += [ssm["w_" + k], ssm["m_" + k], ssm["v_" + k]]
    n_in = len(flat)

    def body(*refs):
        ins, outs = refs[:n_in], refs[n_in:]
        for p in range(n_direct):
            g, w, m, v = (r[...] for r in ins[4 * p:4 * p + 4])
            d, m2, v2 = _adam(w, g, m, v)
            outs[4 * p][...], outs[4 * p + 1][...], outs[4 * p + 2][...], outs[4 * p + 3][...] = g, d, m2, v2
        o = 4 * n_direct
        da_r, da_i, dbb_r, dbb_i, lr, li, ldt, bt_r, bt_i, lbr, lbi, qr, qi = (r[...] for r in ins[o:o + 13])
        dt = jnp.exp(ldt)
        g_br = qr * dbb_r + qi * dbb_i
        g_bi = qr * dbb_i - qi * dbb_r
        dq_r = jnp.sum(bt_r * dbb_r + bt_i * dbb_i, axis=1, keepdims=True)
        dq_i = jnp.sum(bt_r * dbb_i - bt_i * dbb_r, axis=1, keepdims=True)
        den = lr * lr + li * li
        cr, ci = lr / den, li / den
        gl_r = da_r + (cr * dq_r - ci * dq_i)
        gl_i = da_i + (cr * dq_i + ci * dq_r)
        w_r = qr * cr + qi * ci
        w_i = qi * cr - qr * ci
        g_lr = dt * (lbr * gl_r + lbi * gl_i) + (-w_r * dq_r - w_i * dq_i)
        g_li = dt * (lbr * gl_i - lbi * gl_r) + (-w_r * dq_i + w_i * dq_r)
        m_r = lr * lbr - li * lbi
        m_i = lr * lbi + li * lbr
        g_ldt = jnp.sum(m_r * gl_r + m_i * gl_i, axis=2, keepdims=True) * dt
        grads = [g_lr, g_li, g_ldt, g_br, g_bi]
        base_in, base_out = o + 13, 4 * n_direct
        for p, g in enumerate(grads):
            w, m, v = (r[...] for r in ins[base_in + 3 * p:base_in + 3 * p + 3])
            d, m2, v2 = _adam(w, g, m, v)
            q = base_out + 4 * p
            outs[q][...], outs[q + 1][...], outs[q + 2][...], outs[q + 3][...] = g, d, m2, v2

    out_shape = []
    for quad in direct:
        out_shape += [jax.ShapeDtypeStruct(quad[1].shape, F32)] * 4
    for k in chain:
        out_shape += [jax.ShapeDtypeStruct(ssm["w_" + k].shape, F32)] * 4
    res = pl.pallas_call(body, name="small_update", out_shape=out_shape, compiler_params=_params())(*flat)
    return [tuple(res[4 * p:4 * p + 4]) for p in range(n_direct + len(chain))]


def _sum_slots(name, pack):
    def body(p_ref, o_ref):
        acc = p_ref[0].astype(F32)
        for k in range(1, N_DEV):
            acc = acc + p_ref[k].astype(F32)
        o_ref[...] = acc

    return pl.pallas_call(body, name=name, out_shape=jax.ShapeDtypeStruct(pack.shape[1:], F32),
                          compiler_params=_params())(pack)


def _adam_big(name, recv, w, m, v, tr):
    _, rows, cols = recv.shape

    def body(r_ref, w_ref, m_ref, v_ref, g_ref, d_ref, m2_ref, v2_ref):
        g = r_ref[0].astype(F32)
        for k in range(1, N_DEV):
            g = g + r_ref[k].astype(F32)
        d, m2, v2 = _adam(w_ref[...], g, m_ref[...], v_ref[...])
        g_ref[...], d_ref[...], m2_ref[...], v2_ref[...] = g, d, m2, v2

    blk = pl.BlockSpec((tr, cols), lambda i: (i, 0))
    shp = jax.ShapeDtypeStruct((rows, cols), F32)
    return pl.pallas_call(
        body, name=name, grid=(rows // tr,),
        in_specs=[pl.BlockSpec((N_DEV, tr, cols), lambda i: (0, i, 0)), blk, blk, blk],
        out_specs=[blk] * 4, out_shape=[shp] * 4, compiler_params=_params(("parallel",)),
    )(recv, w, m, v)


def _chunk_block(tm, d):
    return pl.BlockSpec((N_CHUNK, tm // N_CHUNK, d), lambda i: (0, i, 0))


def _interleave(block):
    c, n, d = block.shape
    return pltpu.einshape("cjd->jcd", block).reshape(n * c, d)


def _norm_in(x3, g_pre, tm):
    _, steps, d_model = x3.shape
    seq = steps * N_CHUNK

    def body(x_ref, g_ref, xp_ref, h_ref, ht_ref):
        x = _interleave(x_ref[...])
        xp_ref[...] = x
        r = lax.rsqrt(jnp.mean(x * x, axis=-1, keepdims=True) + EPS)
        h = x * r * g_ref[...]
        h_ref[...] = h.astype(h_ref.dtype)
        ht_ref[...] = h.T.astype(ht_ref.dtype)

    rows = pl.BlockSpec((tm, d_model), lambda i: (i, 0))
    return pl.pallas_call(
        body, name="norm_in", grid=(seq // tm,),
        in_specs=[_chunk_block(tm, d_model), pl.BlockSpec((1, d_model), lambda i: (0, 0))],
        out_specs=[rows, rows, pl.BlockSpec((d_model, tm), lambda i: (0, i))],
        out_shape=[jax.ShapeDtypeStruct((seq, d_model), F32), jax.ShapeDtypeStruct((seq, d_model), MXU_DTYPE),
                   jax.ShapeDtypeStruct((d_model, seq), MXU_DTYPE)],
        compiler_params=_params(("parallel",)),
    )(x3, g_pre)


GATHER_ORDER = (0, 1, 4, 2, 6, 5, 3, 7)
PIECE_COLS = 256


def _fwd_in(h, w_pieces, me, comm, tm):
    seq, d_model = h.shape
    n_p, _, gw = w_pieces.shape
    n_i = seq // tm
    cn = comm.n
    consume = [(rel, r) for rel in (0, 1) for r in range(n_p)]
    consume += [(rel, r) for r in range(n_p) for rel in (4, 2, 5, 3)]
    consume += [(rel, r) for r in range(n_p) for rel in (6, 7)]
    n_q = len(consume)
    order = jnp.stack([jnp.bitwise_xor(me, rel) * n_p + r for rel, r in consume]).astype(jnp.int32)

    def body(order_ref, h_hbm, w_hbm, *rest):
        c_in, rest = rest[:cn], rest[cn:]
        proj_hbm, wing = rest[0], rest[1]
        c_out, rest = rest[2:2 + cn], rest[2 + cn:]
        wbuf, send_sems, recv_sems, own_sems, load_sems, h_all, h_sems, out_buf, out_sems = rest[:9]
        c_sems = rest[9:]
        h_loads = [pltpu.make_async_copy(h_hbm.at[pl.ds(t * tm, tm)], h_all.at[pl.ds(t * tm, tm)], h_sems.at[t])
                   for t in range(n_i)]
        x, y, c = (lax.axis_index(a) for a in AXES)
        me_ = 4 * x + 2 * y + c

        def dev(rel):
            return _flip(x, (rel >> 2) & 1), _flip(y, (rel >> 1) & 1), _flip(c, rel & 1)

        def slot(rel):
            px, py, pc = dev(rel)
            return 4 * px + 2 * py + pc

        def remote(src, block, r, to_rel, sem):
            return pltpu.make_async_remote_copy(
                src_ref=src, dst_ref=wing.at[block, r], send_sem=send_sems.at[sem, r], recv_sem=recv_sems.at[sem, r],
                device_id=dev(to_rel), device_id_type=pl.DeviceIdType.MESH)

        pieces = range(n_p)
        own = [pltpu.make_async_copy(w_hbm.at[r], wing.at[me_, r], own_sems.at[r]) for r in pieces]
        first_hand = {p: [remote(w_hbm.at[r], me_, r, GATHER_ORDER[p], p - 1) for r in pieces] for p in (1, 2, 3)}
        relay = {2: (c == 0, [remote(wing.at[slot(4), r], slot(4), r, 2, 3) for r in pieces]),
                 3: (c == 1, [remote(wing.at[slot(2), r], slot(2), r, 4, 3) for r in pieces])}
        passed_on = {p: [remote(wing.at[slot(GATHER_ORDER[p]), r], slot(GATHER_ORDER[p]), r, 1, p + 2) for r in pieces]
                     for p in (2, 3, 4)}

        def load(q):
            rel, r = consume[q]
            return pltpu.make_async_copy(wing.at[slot(rel), r], wbuf.at[q % 2], load_sems.at[q % 2])

        def take(q):
            rel, r = consume[q]
            p = GATHER_ORDER.index(rel)
            if p == 0:
                own[r].wait()
            else:
                remote(w_hbm.at[r], slot(rel), r, rel, p - 1).wait_recv()
            if p in relay:
                pl.when(relay[p][0])(relay[p][1][r].start)
            if p in passed_on:
                passed_on[p][r].start()
            load(q).start()

        for r in pieces:
            own[r].start()
        for r in pieces:
            for p in (1, 2, 3):
                first_hand[p][r].start()
        for cp in h_loads:
            cp.start()
        comm.start(c_in, c_out, c_sems)
        take(0)

        def store(s, k, i):
            col = pl.multiple_of(order_ref[k] * gw, gw)
            return pltpu.make_async_copy(
                out_buf.at[s % 2], proj_hbm.at[pl.ds(pl.multiple_of(i * tm, tm), tm), pl.ds(col, gw)],
                out_sems.at[s % 2])

        def step(s, carry):
            k, i = s // n_i, s % n_i
            for t in range(n_i):
                pl.when((k == 0) & (i == t))(h_loads[t].wait)
            for q in range(n_q):
                @pl.when((k == q) & (i == 0))
                def _():
                    load(q).wait()

                if q + 1 < n_q:
                    @pl.when((k == q) & (i == n_i - 1))
                    def _():
                        take(q + 1)

            @pl.when(s >= 2)
            def _():
                store(s, k, i).wait()

            out_buf[s % 2] = _dot(h_all[pl.ds(pl.multiple_of(i * tm, tm), tm), :], wbuf[k % 2]).astype(out_buf.dtype)
            store(s, k, i).start()
            return carry

        n_s = n_q * n_i
        lax.fori_loop(0, n_s, step, 0)
        for s in range(max(n_s - 2, 0), n_s):
            store(s, s // n_i, s % n_i).wait()
        for p in first_hand:
            for cp in first_hand[p]:
                cp.wait_send()
        for p in passed_on:
            for cp in passed_on[p]:
                cp.wait_send()
        for cond, cps in relay.values():
            for cp in cps:
                pl.when(cond)(cp.wait_send)
        comm.finish(c_in, c_out, c_sems)

    any_ = pl.BlockSpec(memory_space=pl.ANY)
    grid_spec = pltpu.PrefetchScalarGridSpec(
        num_scalar_prefetch=1, grid=(1,),
        in_specs=[any_, any_] + [any_] * cn,
        out_specs=[any_, any_] + [any_] * cn,
        scratch_shapes=[pltpu.VMEM((2, d_model, gw), w_pieces.dtype), pltpu.SemaphoreType.DMA((N_DEV - 1, n_p)),
                        pltpu.SemaphoreType.DMA((N_DEV - 1, n_p)), pltpu.SemaphoreType.DMA((n_p,)),
                        pltpu.SemaphoreType.DMA((2,)), pltpu.VMEM((seq, d_model), h.dtype),
                        pltpu.SemaphoreType.DMA((n_i,)), pltpu.VMEM((2, tm, gw), PROJ_DTYPE),
                        pltpu.SemaphoreType.DMA((2,))]
        + comm.scratch())
    res = pl.pallas_call(
        body, name="fwd_in", grid_spec=grid_spec,
        out_shape=[jax.ShapeDtypeStruct((seq, N_DEV * n_p * gw), PROJ_DTYPE),
                   jax.ShapeDtypeStruct((N_DEV, n_p, d_model, gw), w_pieces.dtype)] + comm.out_shape(),
        compiler_params=_params(("arbitrary",)),
    )(order, h, w_pieces, *comm.operands)
    return res[0], res[1], list(res[2:])


def _shift_prev(a):
    n = a.shape[0]
    last = a[n - N_CHUNK:, :]
    row = lax.broadcasted_iota(jnp.int32, last.shape, 0)
    wrap = jnp.where(row == 0, 0.0, pltpu.roll(last, 1, axis=0))
    return jnp.concatenate([wrap, a[:n - N_CHUNK, :]], axis=0)


def _shift_next(a):
    first = a[:N_CHUNK, :]
    row = lax.broadcasted_iota(jnp.int32, first.shape, 0)
    wrap = jnp.where(row == N_CHUNK - 1, 0.0, pltpu.roll(first, N_CHUNK - 1, axis=0))
    return jnp.concatenate([a[N_CHUNK:, :], wrap], axis=0)


def _conv_specs(seq, d_conv):
    nblk = d_conv // LANES
    return [pl.BlockSpec((seq, LANES), functools.partial(lambda i, o: (0, o + i), o=q * nblk)) for q in range(4)]


def _conv_fwd(proj, conv_w8, conv_b, d_conv):
    seq = proj.shape[0]

    def body(bg_ref, cg_ref, v_ref, zc_ref, w_ref, b_ref, y_ref):
        cv = cg_ref[...].astype(F32) * v_ref[...].astype(F32)
        s1 = _shift_prev(cv)
        s2 = _shift_prev(s1)
        conv = b_ref[...] + w_ref[0:1, :] * s2 + w_ref[1:2, :] * s1 + w_ref[2:3, :] * cv
        z = zc_ref[...].astype(F32)
        y_ref[...] = (bg_ref[...].astype(F32) * conv * (z * _sigmoid(z))).astype(y_ref.dtype)

    col = pl.BlockSpec((seq, LANES), lambda i: (0, i))
    return pl.pallas_call(
        body, name="conv_fwd", grid=(d_conv // LANES,),
        in_specs=_conv_specs(seq, d_conv) + [pl.BlockSpec((8, LANES), lambda i: (0, i)), pl.BlockSpec((1, LANES), lambda i: (0, i))],
        out_specs=col, out_shape=jax.ShapeDtypeStruct((seq, d_conv), MXU_DTYPE),
        compiler_params=_params(("parallel",)),
    )(proj, proj, proj, proj, conv_w8, conv_b)


def _conv_bwd(proj, dyc, conv_w8, conv_b, d_conv, comm=None):
    seq = proj.shape[0]

    def body(bg_ref, cg_ref, v_ref, zc_ref, dy_ref, w_ref, b_ref, d4_ref, dcb_ref, dcw_ref):
        bg, cg, v, z = (r[...].astype(F32) for r in (bg_ref, cg_ref, v_ref, zc_ref))
        w0, w1, w2 = w_ref[0:1, :], w_ref[1:2, :], w_ref[2:3, :]
        cv = cg * v
        s1 = _shift_prev(cv)
        s2 = _shift_prev(s1)
        conv = b_ref[...] + w0 * s2 + w1 * s1 + w2 * cv
        sig = _sigmoid(z)
        dy = dy_ref[...].astype(F32)
        g1 = dy * (z * sig)
        d_conv_ = g1 * bg
        d4_ref[0] = (g1 * conv).astype(d4_ref.dtype)
        d4_ref[3] = (dy * bg * conv * (sig * (1.0 + z * (1.0 - sig)))).astype(d4_ref.dtype)
        n1 = _shift_next(d_conv_)
        n2 = _shift_next(n1)
        d_cv = w2 * d_conv_ + w1 * n1 + w0 * n2
        d4_ref[1] = (d_cv * v).astype(d4_ref.dtype)
        d4_ref[2] = (d_cv * cg).astype(d4_ref.dtype)
        dcb_ref[...] = jnp.sum(d_conv_, axis=0, keepdims=True)
        rows = [jnp.sum(d_conv_ * s, axis=0, keepdims=True) for s in (s2, s1, cv)]
        dcw_ref[...] = jnp.concatenate(rows + [jnp.zeros((5, LANES), F32)], axis=0)

    col = pl.BlockSpec((seq, LANES), lambda i: (0, i))
    return _call(
        body, comm, name="conv_bwd", grid=(d_conv // LANES,),
        in_specs=_conv_specs(seq, d_conv) + [col, pl.BlockSpec((8, LANES), lambda i: (0, i)), pl.BlockSpec((1, LANES), lambda i: (0, i))],
        out_specs=[pl.BlockSpec((4, seq, LANES), lambda i: (0, 0, i)), pl.BlockSpec((1, LANES), lambda i: (0, i)),
                   pl.BlockSpec((8, LANES), lambda i: (0, i))],
        out_shape=[jax.ShapeDtypeStruct((4, seq, d_conv), MXU_DTYPE), jax.ShapeDtypeStruct((1, d_conv), F32),
                   jax.ShapeDtypeStruct((8, d_conv), F32)],
        operands=[proj, proj, proj, proj, dyc, conv_w8, conv_b])


def _cmul(ar, ai, br, bi):
    return ar * br - ai * bi, ar * bi + ai * br


def _down(v, k):
    row = lax.broadcasted_iota(jnp.int32, v.shape, 0)
    return jnp.where(row >= k, pltpu.roll(v, k, axis=0), 0.0)


def _up(v, k):
    row = lax.broadcasted_iota(jnp.int32, v.shape, 0)
    return jnp.where(row < N_CHUNK - k, pltpu.roll(v, N_CHUNK - k, axis=0), 0.0)


def _chunk_carry(fr, fi, mr, mi, shift):
    vr, vi = shift(fr, 1), shift(fi, 1)
    for k in (1, 2, 4):
        pr, pi = _cmul(mr, mi, shift(vr, k), shift(vi, k))
        vr, vi = vr + pr, vi + pi
        mr, mi = _cmul(mr, mi, mr, mi)
    return vr, vi


def _tile(ref, j, width, part):
    return ref.at[pl.ds(pl.multiple_of(j * N_CHUNK, N_CHUNK), N_CHUNK), pl.ds(part * width, width)]


def _row(t, k):
    return jnp.broadcast_to(t[k:k + 1, :], t.shape)


def _power_table(tab_ref, ar, ai, steps, width):
    e = lax.broadcasted_iota(jnp.int32, ar.shape, 0) + 1
    rr, ri = jnp.ones_like(ar), jnp.zeros_like(ai)
    br, bi = ar, ai
    for bit in range(4):
        mr, mi = _cmul(rr, ri, br, bi)
        take = ((e >> bit) & 1) == 1
        rr, ri = jnp.where(take, mr, rr), jnp.where(take, mi, ri)
        if bit < 3:
            br, bi = _cmul(br, bi, br, bi)
    _tile(tab_ref, 0, width, 0)[...] = rr
    _tile(tab_ref, 0, width, 1)[...] = ri

    def step(m, carry):
        tr, ti = _cmul(carry[0], carry[1], br, bi)
        _tile(tab_ref, m, width, 0)[...] = tr
        _tile(tab_ref, m, width, 1)[...] = ti
        return tr, ti

    lax.fori_loop(1, steps // N_CHUNK, step, (rr, ri))


def _last_power(tab_ref, steps, width):
    shape = (N_CHUNK, width)
    return (jnp.broadcast_to(tab_ref[steps - 1:steps, 0:width], shape),
            jnp.broadcast_to(tab_ref[steps - 1:steps, width:2 * width], shape))


def _scan_fwd(s_ref, ar, ai, steps, width):
    def step(j, carry):
        sr, si = carry
        nr = ar * sr - ai * si + _tile(s_ref, j, width, 0)[...]
        ni = ar * si + ai * sr + _tile(s_ref, j, width, 1)[...]
        _tile(s_ref, j, width, 0)[...] = nr
        _tile(s_ref, j, width, 1)[...] = ni
        return nr, ni

    z = jnp.zeros((N_CHUNK, width), F32)
    return lax.fori_loop(0, steps, step, (z, z), unroll=4)


def _scan_both(s_ref, g_ref, ar, ai, steps, width):
    def step(q, carry):
        sr, si, gr, gi = carry
        j, jb = q, steps - 1 - q
        nsr = ar * sr - ai * si + _tile(s_ref, j, width, 0)[...]
        nsi = ar * si + ai * sr + _tile(s_ref, j, width, 1)[...]
        ngr = ar * gr + ai * gi + _tile(g_ref, jb, width, 0)[...]
        ngi = ar * gi - ai * gr + _tile(g_ref, jb, width, 1)[...]
        _tile(s_ref, j, width, 0)[...] = nsr
        _tile(s_ref, j, width, 1)[...] = nsi
        _tile(g_ref, jb, width, 0)[...] = ngr
        _tile(g_ref, jb, width, 1)[...] = ngi
        return nsr, nsi, ngr, ngi

    z = jnp.zeros((N_CHUNK, width), F32)
    return lax.fori_loop(0, steps, step, (z, z, z, z), unroll=2)


def _patch_fwd(s_ref, tab_ref, cr, ci, steps, width):
    def tile(m, _):
        tr, ti = _tile(tab_ref, m, width, 0)[...], _tile(tab_ref, m, width, 1)[...]
        for k in range(N_CHUNK):
            fr, fi = _cmul(_row(tr, k), _row(ti, k), cr, ci)
            j = m * N_CHUNK + k
            _tile(s_ref, j, width, 0)[...] += fr
            _tile(s_ref, j, width, 1)[...] += fi
        return 0

    lax.fori_loop(0, steps // N_CHUNK, tile, 0)


def _lam_rows(lam_ref, hh, width):
    return (jnp.broadcast_to(lam_ref[hh, 0:1, :], (N_CHUNK, width)),
            jnp.broadcast_to(lam_ref[hh, 1:2, :], (N_CHUNK, width)))


def _ssm_specs(seq, col0):
    return dict(
        col=pl.BlockSpec((seq, LANES), lambda i: (0, col0 + i)),
        lam=pl.BlockSpec((2, 2, HALF_W), lambda i: (i, 0, 0)),
        bb=pl.BlockSpec((2, HALF_CH, 2 * HALF_W), lambda i: (i, 0, 0)),
        cc=pl.BlockSpec((2, 2 * HALF_W, HALF_CH), lambda i: (i, 0, 0)),
        vec=pl.BlockSpec((1, LANES), lambda i: (0, i)),
        out=pl.BlockSpec((seq, LANES), lambda i: (0, i)),
    )


def _ssm_fwd(proj, lam, bbcat, cccat, d_skip, d_ssm, u_col0, comm=None):
    seq = proj.shape[0]
    steps = seq // N_CHUNK

    def body(u_ref, lam_ref, bb_ref, cc_ref, d_ref, yp_ref, s_ref, tab_ref):
        for hh in range(2):
            lanes = slice(HALF_CH * hh, HALF_CH * (hh + 1))
            u_half = u_ref[:, lanes].astype(F32)
            ar, ai = _lam_rows(lam_ref, hh, HALF_W)
            _power_table(tab_ref, ar, ai, steps, HALF_W)
            s_ref[...] = _dot(u_half.astype(MXU_DTYPE), bb_ref[hh])
            fr, fi = _scan_fwd(s_ref, ar, ai, steps, HALF_W)
            pr, pi = _last_power(tab_ref, steps, HALF_W)
            cr, ci = _chunk_carry(fr, fi, pr, pi, _down)
            _patch_fwd(s_ref, tab_ref, cr, ci, steps, HALF_W)
            y = _dot(s_ref[...].astype(MXU_DTYPE), cc_ref[hh])
            yp_ref[:, lanes] = y + d_ref[:, lanes] * u_half

    sp = _ssm_specs(seq, u_col0 // LANES)
    return _call(
        body, comm, name="ssm_fwd", grid=(d_ssm // LANES,),
        in_specs=[sp["col"], sp["lam"], sp["bb"], sp["cc"], sp["vec"]], out_specs=[sp["out"]],
        out_shape=[jax.ShapeDtypeStruct((seq, d_ssm), F32)],
        scratch_shapes=[pltpu.VMEM((seq, 2 * HALF_W), F32), pltpu.VMEM((steps, 2 * HALF_W), F32)],
        operands=[proj, lam, bbcat, cccat, d_skip])


def _ssm_bwd(proj, dyp, lam, bbcat, cccat, d_skip, d_ssm, u_col0, comm=None):
    seq = proj.shape[0]
    steps = seq // N_CHUNK
    n_half = 2 * d_ssm // LANES
    width = HALF_W

    def body(u_ref, dyp_ref, lam_ref, bb_ref, cc_ref, d_ref, du_ref, dbb_ref, dcc_ref, da_ref, dd_ref,
             s_ref, g_ref, tab_ref):
        n_tiles = steps // N_CHUNK
        for hh in range(2):
            lanes = slice(HALF_CH * hh, HALF_CH * (hh + 1))
            u_half, dy_half = u_ref[:, lanes].astype(F32), dyp_ref[:, lanes].astype(F32)
            dy_mx = dy_half.astype(MXU_DTYPE)
            ar, ai = _lam_rows(lam_ref, hh, width)
            _power_table(tab_ref, ar, ai, steps, width)
            s_ref[...] = _dot(u_half.astype(MXU_DTYPE), bb_ref[hh])
            g_ref[...] = _dot_nt(dy_mx, cc_ref[hh])
            fr, fi, lr_, li_ = _scan_both(s_ref, g_ref, ar, ai, steps, width)
            pr, pi = _last_power(tab_ref, steps, width)
            cr, ci = _chunk_carry(fr, fi, pr, pi, _down)
            gr, gi = _chunk_carry(lr_, li_, pr, -pi, _up)

            def tile(m, carry):
                sr, si, accr, acci = carry
                t1r, t1i = _tile(tab_ref, m, width, 0)[...], _tile(tab_ref, m, width, 1)[...]
                mb = n_tiles - 1 - m
                t2r, t2i = _tile(tab_ref, mb, width, 0)[...], _tile(tab_ref, mb, width, 1)[...]
                for k in range(N_CHUNK):
                    j = m * N_CHUNK + k
                    xr, xi = _cmul(_row(t1r, k), _row(t1i, k), cr, ci)
                    nsr = _tile(s_ref, j, width, 0)[...] + xr
                    nsi = _tile(s_ref, j, width, 1)[...] + xi
                    _tile(s_ref, j, width, 0)[...] = nsr
                    _tile(s_ref, j, width, 1)[...] = nsi
                    qr, qi = _row(t2r, N_CHUNK - 1 - k), _row(t2i, N_CHUNK - 1 - k)
                    ngr = _tile(g_ref, j, width, 0)[...] + (qr * gr + qi * gi)
                    ngi = _tile(g_ref, j, width, 1)[...] + (qr * gi - qi * gr)
                    _tile(g_ref, j, width, 0)[...] = ngr
                    _tile(g_ref, j, width, 1)[...] = ngi
                    accr = accr + (sr * ngr + si * ngi)
                    acci = acci + (sr * ngi - si * ngr)
                    sr, si = nsr, nsi
                return sr, si, accr, acci

            z = jnp.zeros((N_CHUNK, width), F32)
            _, _, accr, acci = lax.fori_loop(0, n_tiles, tile, (cr, ci, z, z))
            da_ref[hh, :, 0:width] = jnp.sum(accr, axis=0, keepdims=True)
            da_ref[hh, :, width:2 * width] = jnp.sum(acci, axis=0, keepdims=True)

            g_mx = g_ref[...].astype(MXU_DTYPE)
            dcc_ref[hh] = _dot_tn(dy_mx, s_ref[...].astype(MXU_DTYPE)).T
            dbb_ref[hh] = _dot_tn(u_half.astype(MXU_DTYPE), g_mx)
            du = _dot_nt(g_mx, bb_ref[hh]) + d_ref[:, lanes] * dy_half
            du_ref[:, lanes] = du.astype(du_ref.dtype)
            dd_ref[:, lanes] = jnp.sum(dy_half * u_half, axis=0, keepdims=True)

    sp = _ssm_specs(seq, u_col0 // LANES)
    return _call(
        body, comm, name="ssm_bwd", grid=(d_ssm // LANES,),
        in_specs=[sp["col"], sp["out"], sp["lam"], sp["bb"], sp["cc"], sp["vec"]],
        out_specs=[sp["out"], sp["bb"], sp["cc"], pl.BlockSpec((2, 1, 2 * width), lambda i: (i, 0, 0)), sp["vec"]],
        out_shape=[jax.ShapeDtypeStruct((seq, d_ssm), MXU_DTYPE),
                   jax.ShapeDtypeStruct((n_half, HALF_CH, 2 * width), F32),
                   jax.ShapeDtypeStruct((n_half, 2 * width, HALF_CH), F32),
                   jax.ShapeDtypeStruct((n_half, 1, 2 * width), F32),
                   jax.ShapeDtypeStruct((1, d_ssm), F32)],
        scratch_shapes=[pltpu.VMEM((seq, 2 * width), F32), pltpu.VMEM((seq, 2 * width), F32),
                        pltpu.VMEM((steps, 2 * width), F32)],
        operands=[proj, dyp, lam, bbcat, cccat, d_skip])


def _tail(xp, t3, proj, yconv, yp, w_glu, b_glu, w_out, g_post, zs_col0, tm):
    seq, d_model = xp.shape
    d_conv, d_ssm = yconv.shape[1], yp.shape[1]
    d_mix = d_conv + d_ssm
    assert zs_col0 % d_ssm == 0

    def body(x_ref, t_ref, zs_ref, yc_ref, yp_ref, wglu_hbm, bglu_ref, wout_hbm, gpost_ref,
             dy_ref, do_ref, mixt_ref, dyc_ref, dyp_ref, dzs_ref, ygt_ref, dq_ref, loss_ref, dgpost_ref, dbglu_ref,
             wglu, wout):
        @pl.when(pl.program_id(0) == 0)
        def _():
            pltpu.sync_copy(wglu_hbm, wglu)
            pltpu.sync_copy(wout_hbm, wout)
            loss_ref[...] = jnp.zeros_like(loss_ref)
            dgpost_ref[...] = jnp.zeros_like(dgpost_ref)
            dbglu_ref[...] = jnp.zeros_like(dbglu_ref)

        a = yp_ref[...]
        th = jnp.tanh(GELU_C * (a + GELU_K * (a * a * a)))
        yg = a * (0.5 * (1.0 + th))
        dgelu = 0.5 * (1.0 + th) + 0.5 * a * (1.0 - th * th) * (GELU_C * (1.0 + 3.0 * GELU_K * a * a))
        yg_mx = yg.astype(MXU_DTYPE)
        sq = _sigmoid(_dot(yg_mx, wglu[...]) + bglu_ref[...])
        y2 = yg * sq
        zs = zs_ref[...].astype(F32)
        sz = _sigmoid(zs)
        silz = zs * sz
        yc, ys = yc_ref[...].astype(F32), y2 * silz
        mix = jnp.concatenate([yc, ys], axis=1).astype(MXU_DTYPE)
        mixt_ref[0:d_conv, :] = yc.T.astype(MXU_DTYPE)
        mixt_ref[d_conv:, :] = ys.T.astype(MXU_DTYPE)
        o = _dot(mix, wout[...])
        r2 = lax.rsqrt(jnp.mean(o * o, axis=-1, keepdims=True) + EPS)
        on = o * r2
        gpost = gpost_ref[...]
        err = (x_ref[...] + on * gpost) - _interleave(t_ref[...])
        loss_ref[...] += 0.5 * jnp.sum(jnp.mean(err * err, axis=-1, keepdims=True), axis=0, keepdims=True)
        dy = err * (1.0 / d_model)
        dy_ref[...] = dy
        dgpost_ref[...] += jnp.sum(dy * on, axis=0, keepdims=True)
        d_on = dy * gpost
        d_o = r2 * (d_on - on * jnp.mean(d_on * on, axis=-1, keepdims=True))
        do_mx = d_o.astype(MXU_DTYPE)
        do_ref[...] = do_mx
        d_mix_ = _dot_nt(do_mx, wout[...])
        dyc_ref[...] = d_mix_[:, :d_conv].astype(dyc_ref.dtype)
        d_yssm = d_mix_[:, d_conv:]
        d_y2 = d_yssm * silz
        dzs_ref[...] = (d_yssm * y2 * (sz * (1.0 + zs * (1.0 - sz)))).astype(dzs_ref.dtype)
        d_q = d_y2 * yg * (sq * (1.0 - sq))
        dq_mx = d_q.astype(MXU_DTYPE)
        dq_ref[...] = dq_mx
        ygt_ref[...] = yg.T.astype(MXU_DTYPE)
        dbglu_ref[...] += jnp.sum(d_q, axis=0, keepdims=True)
        d_yg = d_y2 * sq + _dot_nt(dq_mx, wglu[...])
        dyp_ref[...] = (d_yg * dgelu).astype(dyp_ref.dtype)

    def rows(width, col=0):
        return pl.BlockSpec((tm, width), lambda i: (i, col))

    def fixed(width):
        return pl.BlockSpec((1, width), lambda i: (0, 0))

    def cols(height):
        return pl.BlockSpec((height, tm), lambda i: (0, i))

    any_ = pl.BlockSpec(memory_space=pl.ANY)
    return pl.pallas_call(
        body, name="tail", grid=(seq // tm,),
        in_specs=[rows(d_model), _chunk_block(tm, d_model), rows(d_ssm, zs_col0 // d_ssm), rows(d_conv), rows(d_ssm),
                  any_, fixed(d_ssm), any_, fixed(d_model)],
        out_specs=[rows(d_model), rows(d_model), cols(d_mix), rows(d_conv), rows(d_ssm), rows(d_ssm), cols(d_ssm),
                   rows(d_ssm), fixed(LANES), fixed(d_model), fixed(d_ssm)],
        out_shape=[jax.ShapeDtypeStruct((seq, d_model), F32), jax.ShapeDtypeStruct((seq, d_model), MXU_DTYPE),
                   jax.ShapeDtypeStruct((d_mix, seq), MXU_DTYPE), jax.ShapeDtypeStruct((seq, d_conv), MXU_DTYPE),
                   jax.ShapeDtypeStruct((seq, d_ssm), MXU_DTYPE), jax.ShapeDtypeStruct((seq, d_ssm), MXU_DTYPE),
                   jax.ShapeDtypeStruct((d_ssm, seq), MXU_DTYPE), jax.ShapeDtypeStruct((seq, d_ssm), MXU_DTYPE),
                   jax.ShapeDtypeStruct((1, LANES), F32), jax.ShapeDtypeStruct((1, d_model), F32),
                   jax.ShapeDtypeStruct((1, d_ssm), F32)],
        scratch_shapes=[pltpu.VMEM(w_glu.shape, MXU_DTYPE), pltpu.VMEM(w_out.shape, MXU_DTYPE)],
        compiler_params=_params(("arbitrary",)),
    )(xp, t3, proj, yconv, yp, w_glu, b_glu, w_out, g_post)


def _bwd_in(d4, du, dzs, gr, win_g, xp, dy, g_pre, comm, tm):
    seq, d_model = xp.shape
    nb, n_p, _, gw = win_g.shape
    nc = n_p * gw
    per = d4.shape[2] // gr

    def body(d4_ref, du_ref, dzs_ref, w_hbm, x_ref, dy_ref, g_ref, gx_ref, dg_ref, w_all, w_sems):
        def granule(g):
            p, cols = g // per, slice(g % per * gr, (g % per + 1) * gr)
            if p < 4:
                return d4_ref[p, :, cols]
            return du_ref[:, cols] if p == 4 else dzs_ref[:, cols]

        i = pl.program_id(0)
        loads = [[pltpu.make_async_copy(w_hbm.at[k, r], w_all.at[k, :, pl.ds(r * gw, gw)], w_sems.at[k, r])
                  for r in range(n_p)] for k in range(nb)]

        @pl.when(i == 0)
        def _():
            dg_ref[...] = jnp.zeros_like(dg_ref)
            for row in loads:
                for cp in row:
                    cp.start()

        dh = None
        for k in range(nb):
            @pl.when(i == 0)
            def _():
                for cp in loads[k]:
                    cp.wait()

            dp = jnp.concatenate([granule(g) for g in range(k * nc // gr, (k + 1) * nc // gr)], axis=1)
            part = _dot_nt(dp, w_all[k])
            dh = part if dh is None else dh + part

        x = x_ref[...]
        r = lax.rsqrt(jnp.mean(x * x, axis=-1, keepdims=True) + EPS)
        xn = x * r
        dg_ref[...] += jnp.sum(dh * xn, axis=0, keepdims=True)
        dxn = dh * g_ref[...]
        gx_ref[...] = r * (dxn - xn * jnp.mean(dxn * xn, axis=-1, keepdims=True)) + dy_ref[...]

    row = pl.BlockSpec((tm, d_model), lambda i: (i, 0))
    vec = pl.BlockSpec((1, d_model), lambda i: (0, 0))
    return _call(
        body, comm, name="bwd_in", grid=(seq // tm,),
        in_specs=[pl.BlockSpec((4, tm, d4.shape[2]), lambda i: (0, i, 0)),
                  pl.BlockSpec((tm, du.shape[1]), lambda i: (i, 0)), pl.BlockSpec((tm, dzs.shape[1]), lambda i: (i, 0)),
                  pl.BlockSpec(memory_space=pl.ANY), row, row, vec],
        out_specs=[row, vec],
        out_shape=[jax.ShapeDtypeStruct((seq, d_model), F32), jax.ShapeDtypeStruct((1, d_model), F32)],
        scratch_shapes=[pltpu.VMEM((nb, d_model, nc), win_g.dtype), pltpu.SemaphoreType.DMA((nb, n_p))],
        operands=[d4, du, dzs, win_g, xp, dy, g_pre])


def _lookup(g, table):
    out = jnp.int32(table[0])
    for gi in range(1, len(table)):
        if table[gi] != table[gi - 1]:
            out = jnp.where(g >= gi, jnp.int32(table[gi]), out)
    return out


def _held(values, used):
    cur = next(v for v, u in zip(values, used) if u)
    out = []
    for v, u in zip(values, used):
        cur = v if u else cur
        out.append(cur)
    return out


def _dw_in(name, ht, d4, du, dzs, granules, gr, nc, tm, comm=None):
    d_model, seq = ht.shape
    per = d4.shape[2] // gr
    piece, col = [g // per for g in granules], [g % per for g in granules]
    sources = [(d4, [p < 4 for p in piece]), (du, [p == 4 for p in piece]), (dzs, [p == 5 for p in piece])]
    sources = [(a, used) for a, used in sources if any(used)]
    select = [next(s for s, (_, used) in enumerate(sources) if used[q]) for q in range(len(granules))]
    owner, place = [g * gr // nc for g in granules], [g * gr % nc // gr for g in granules]

    def body(a_ref, *refs):
        src_refs, o_ref = refs[:-1], refs[-1]
        j = pl.program_id(1)
        for s, ref in enumerate(src_refs):
            @pl.when(_lookup(j, select) == s)
            def _():
                o_ref[...] = _dot(a_ref[...], ref[...]).astype(o_ref.dtype)

    in_specs = [pl.BlockSpec((tm, seq), lambda i, j: (i, 0))]
    for a, used in sources:
        cols = _held(col, used)
        if a.ndim == 3:
            rows = _held(piece, used)
            in_specs.append(pl.BlockSpec((None, seq, gr), functools.partial(
                lambda i, j, rows, cols: (_lookup(j, rows), 0, _lookup(j, cols)), rows=rows, cols=cols)))
        else:
            in_specs.append(pl.BlockSpec((seq, gr), functools.partial(
                lambda i, j, cols: (0, _lookup(j, cols)), cols=cols)))
    return _call(
        body, comm, name=name, grid=(d_model // tm, len(granules)), in_specs=in_specs,
        out_specs=[pl.BlockSpec((None, tm, gr), lambda i, j: (_lookup(j, owner), i, _lookup(j, place)))],
        out_shape=[jax.ShapeDtypeStruct((N_DEV, d_model, nc), MXU_DTYPE)],
        operands=[ht] + [a for a, _ in sources])


def _wgrad(name, at, b, tm, tn, out_shape, out_block, out_index, comm=None):
    m, seq = at.shape
    n = b.shape[1]

    def body(a_ref, b_ref, o_ref):
        o_ref[...] = _dot(a_ref[...], b_ref[...]).astype(o_ref.dtype)

    return _call(
        body, comm, name=name, grid=(n // tn, m // tm),
        in_specs=[pl.BlockSpec((tm, seq), lambda j, i: (i, 0)), pl.BlockSpec((seq, tn), lambda j, i: (0, j))],
        out_specs=[pl.BlockSpec(out_block, lambda j, i: out_index(i, j))],
        out_shape=[jax.ShapeDtypeStruct(out_shape, MXU_DTYPE)],
        operands=[at, b])


def _eye_g():
    return jnp.eye(HALF_G, dtype=F32)


def _bb_blockdiag(bbt_r, bbt_i):
    n_half = bbt_r.shape[0] // HALF_G

    def one(t):
        t = t.reshape(n_half, HALF_G, SSM_GROUP, SSM_STATE)
        t = t[:, :, :, None, :] * _eye_g()[None, :, None, :, None]
        return t.reshape(n_half, HALF_CH, HALF_W)

    return jnp.concatenate([one(bbt_r), one(bbt_i)], axis=-1)


def _cc_blockdiag(c_re, c_im):
    n_half = c_re.shape[0] // HALF_G

    def one(t):
        t = t.reshape(n_half, HALF_G, SSM_GROUP, SSM_STATE)
        t = jnp.transpose(t, (0, 3, 1, 2))
        t = t[:, None, :, :, :] * _eye_g()[None, :, None, :, None]
        return t.reshape(n_half, HALF_W, HALF_CH)

    return jnp.concatenate([one(c_re), one(-c_im)], axis=1)


def _bb_diag(dbb):
    n_half = dbb.shape[0]
    t = dbb.reshape(n_half, HALF_G, SSM_GROUP, 2, HALF_G, SSM_STATE)
    t = jnp.sum(t * _eye_g()[None, :, None, None, :, None], axis=4)
    t = jnp.transpose(t, (3, 0, 1, 2, 4))
    return t.reshape(2, n_half * HALF_G, SSM_GROUP, SSM_STATE)


def _cc_diag(dcc):
    n_half = dcc.shape[0]
    t = dcc.reshape(n_half, 2, HALF_G, SSM_STATE, HALF_G, SSM_GROUP)
    t = jnp.sum(t * _eye_g()[None, None, :, None, :, None], axis=2)
    t = jnp.transpose(t, (1, 0, 3, 4, 2))
    return t.reshape(2, n_half * HALF_G, SSM_GROUP, SSM_STATE)


def _unpermute_rows(a):
    seq, d = a.shape
    return a.reshape(seq // N_CHUNK, N_CHUNK, d).transpose(1, 0, 2).reshape(seq, d)


def _pack_rows(shape):
    return -(-math.prod(shape) // (8 * LANES)) * 8


def _pack(parts, dtype=F32):
    rows = []
    for p in parts:
        flat = p.reshape(-1).astype(dtype)
        rows.append(jnp.pad(flat, (0, _pack_rows(p.shape) * LANES - flat.shape[0])).reshape(-1, LANES))
    return jnp.concatenate(rows, axis=0)


def _unpack(packed, shapes):
    out, o = [], 0
    for s in shapes:
        n = _pack_rows(s)
        out.append(packed[o:o + n].reshape(-1)[:math.prod(s)].reshape(s))
        o += n
    return out


def kernel(x, norm_pre_g, w_in, conv_w, conv_b, ssm_a_re, ssm_a_im, ssm_log_dt, ssm_b_re, ssm_b_im, ssm_c_re, ssm_c_im, ssm_d, w_glu, b_glu, w_out, norm_post_g, loss_target, m_norm_pre_g, m_w_in, m_conv_w, m_conv_b, m_ssm_a_re, m_ssm_a_im, m_ssm_log_dt, m_ssm_b_re, m_ssm_b_im, m_ssm_c_re, m_ssm_c_im, m_ssm_d, m_w_glu, m_b_glu, m_w_out, m_norm_post_g, v_norm_pre_g, v_w_in, v_conv_w, v_conv_b, v_ssm_a_re, v_ssm_a_im, v_ssm_log_dt, v_ssm_b_re, v_ssm_b_im, v_ssm_c_re, v_ssm_c_im, v_ssm_d, v_w_glu, v_b_glu, v_w_out, v_norm_post_g):
    seq, d_model = x.shape[1], x.shape[2]
    d_conv, d_ssm = conv_b.shape[0], ssm_d.shape[0]
    groups, states = ssm_a_re.shape
    assert x.shape[0] == 1 and seq % (8 * N_CHUNK) == 0 and d_conv == d_ssm
    assert (groups, states) == (d_ssm // SSM_GROUP, SSM_STATE) and d_ssm % LANES == 0
    me = 4 * lax.axis_index("x") + 2 * lax.axis_index("y") + lax.axis_index("c")
    tm = min(512, seq)

    x3 = x[0].reshape(N_CHUNK, seq // N_CHUNK, d_model)
    t3 = loss_target[0].reshape(N_CHUNK, seq // N_CHUNK, d_model)
    row = lambda a: a.reshape(1, -1)
    conv_w8 = jnp.pad(conv_w, ((0, 8 - conv_w.shape[0]), (0, 0)))

    g3 = lambda a: a.reshape(groups, 1, -1)
    bt_re, bt_im = jnp.transpose(ssm_b_re, (0, 2, 1)), jnp.transpose(ssm_b_im, (0, 2, 1))
    lbr, lbi, qr, qi, bbt_r, bbt_i = _ssm_prep(g3(ssm_a_re), g3(ssm_a_im), g3(ssm_log_dt), bt_re, bt_im)
    n_half = groups // HALF_G
    lam = jnp.stack([lbr.reshape(n_half, HALF_W), lbi.reshape(n_half, HALF_W)], axis=1)
    bbcat = _bb_blockdiag(bbt_r, bbt_i).astype(MXU_DTYPE)
    cccat = _cc_blockdiag(ssm_c_re, ssm_c_im).astype(MXU_DTYPE)

    xp, h, ht = _norm_in(x3, row(norm_pre_g), tm)
    nc = w_in.shape[1]
    w_pieces = jnp.transpose(w_in.astype(MXU_DTYPE).reshape(d_model, nc // PIECE_COLS, PIECE_COLS), (1, 0, 2))
    proj, win_g, (convw_g, wglu_g) = _fwd_in(h, w_pieces, me, _Comm([conv_w8, w_glu.astype(MXU_DTYPE)]),
                                             seq)
    conv_w_full = jnp.transpose(convw_g, (1, 0, 2)).reshape(8, d_conv)
    u_col0, zs_col0 = 4 * d_conv, 4 * d_conv + d_ssm
    yconv = _conv_fwd(proj, conv_w_full, row(conv_b), d_conv)
    (yp,), (wout_g,) = _ssm_fwd(proj, lam, bbcat, cccat, row(ssm_d), d_ssm, u_col0,
                                _Comm([w_out.astype(MXU_DTYPE)]))
    w_out_full = wout_g.reshape(-1, d_model)
    w_glu_full = wglu_g.reshape(-1, d_ssm)
    (dy, d_o, mixt, dyc, dyp, dzs, ygt, dq, loss_part, dgpost, dbglu) = _tail(
        xp, t3, proj, yconv, yp, w_glu_full, row(b_glu), w_out_full, row(norm_post_g), zs_col0, min(256, seq))

    r_out, r_glu, nc = w_out.shape[0], w_glu.shape[0], w_in.shape[1]
    (dwout_p,), _ = _wgrad("dw_out", mixt, d_o, r_out, min(1024, d_model), (N_DEV, r_out, d_model),
                           (None, r_out, min(1024, d_model)), lambda i, j: (i, 0, j))
    (dwglu_p,), _ = _wgrad("dw_glu", ygt, dq, r_glu, d_ssm, (N_DEV, r_glu, d_ssm),
                           (None, r_glu, d_ssm), lambda i, j: (i, 0, 0))
    (d4, dconvb, dconvw), (recv_glu,) = _conv_bwd(proj, dyc, conv_w_full, row(conv_b), d_conv,
                                                  _Comm([], [dwglu_p]))
    late = [k for k in range(N_DEV) if k * nc < u_col0 + d_ssm and (k + 1) * nc > u_col0]
    early = [k for k in range(N_DEV) if k not in late]
    gr = math.gcd(nc, d_conv)
    granules = lambda blocks: [g for k in blocks for g in range(k * nc // gr, (k + 1) * nc // gr)]
    tmw = min(1024, d_model)
    (dwin_e,), (recv_out,) = _dw_in("dw_in_early", ht, d4, None, dzs, granules(early), gr, nc, tmw,
                                    _Comm([], [dwout_p]))
    (du, dbb, dcc, da, dd), (recv_in,) = _ssm_bwd(
        proj, dyp, lam, bbcat, cccat, row(ssm_d), d_ssm, u_col0, _Comm([], [dwin_e], dests={0: early}))
    parts_mx = [_bb_diag(dbb), _cc_diag(dcc)]
    (dwin_l,), (pack_mx_g,) = _dw_in("dw_in_late", ht, d4, du, dzs, granules(late), gr, nc, tmw,
                                     _Comm([_pack(parts_mx, MXU_DTYPE)]))
    da_n = jnp.transpose(da.reshape(n_half, 2, HALF_G, SSM_STATE), (1, 0, 2, 3)).reshape(2, groups, 1, states)
    parts = [dgpost, dconvb, dd, dbglu, dconvw[:3], da_n, loss_part]
    shapes, shapes_mx = [p.shape for p in parts], [p.shape for p in parts_mx]
    (gx_p, dgpre), (pack_g, recv_in) = _bwd_in(
        d4, du, dzs, gr, win_g, xp, dy, row(norm_pre_g),
        _Comm([_pack(parts)], [dwin_l], dests={1: late}, into={1: recv_in}), min(256, seq))
    (last_g,) = _exchange("reduce_last", [_pack([dgpre])], [])
    (g_gpost, g_convb, g_d, g_bglu, g_convw, g_da, loss_sum) = _unpack(_sum_slots("sum_pack", pack_g), shapes)
    (g_dbb, g_dcc) = _unpack(_sum_slots("sum_pack_mx", pack_mx_g), shapes_mx)
    (g_gpre,) = _unpack(_sum_slots("sum_last", last_g), [dgpre.shape])
    g_convw = lax.dynamic_slice(g_convw, (0, me * conv_w.shape[1]), conv_w.shape)

    tr = lambda a: jnp.transpose(a, (0, 2, 1))
    direct = [(g_gpre, row(norm_pre_g), row(m_norm_pre_g), row(v_norm_pre_g)),
              (g_convb, row(conv_b), row(m_conv_b), row(v_conv_b)),
              (g_d, row(ssm_d), row(m_ssm_d), row(v_ssm_d)),
              (g_bglu, row(b_glu), row(m_b_glu), row(v_b_glu)),
              (g_gpost, row(norm_post_g), row(m_norm_post_g), row(v_norm_post_g)),
              (g_convw, conv_w, m_conv_w, v_conv_w),
              (g_dcc[0], ssm_c_re, m_ssm_c_re, v_ssm_c_re),
              (-g_dcc[1], ssm_c_im, m_ssm_c_im, v_ssm_c_im)]
    ssm = dict(da_r=g_da[0], da_i=g_da[1], dbb_r=g_dbb[0], dbb_i=g_dbb[1], lr=g3(ssm_a_re), li=g3(ssm_a_im),
               ldt=g3(ssm_log_dt), bt_r=bt_re, bt_i=bt_im, lbr=lbr, lbi=lbi, qr=qr, qi=qi,
               w_a_re=g3(ssm_a_re), m_a_re=g3(m_ssm_a_re), v_a_re=g3(v_ssm_a_re),
               w_a_im=g3(ssm_a_im), m_a_im=g3(m_ssm_a_im), v_a_im=g3(v_ssm_a_im),
               w_log_dt=g3(ssm_log_dt), m_log_dt=g3(m_ssm_log_dt), v_log_dt=g3(v_ssm_log_dt),
               w_bt_re=bt_re, m_bt_re=tr(m_ssm_b_re), v_bt_re=tr(v_ssm_b_re),
               w_bt_im=bt_im, m_bt_im=tr(m_ssm_b_im), v_bt_im=tr(v_ssm_b_im))
    small = _small_update(direct, ssm)
    res = {}
    for name, quad, shape in zip(["norm_pre_g", "conv_b", "ssm_d", "b_glu", "norm_post_g", "conv_w", "ssm_c_re", "ssm_c_im"],
                                 small[:8], [norm_pre_g.shape, conv_b.shape, ssm_d.shape, b_glu.shape,
                                             norm_post_g.shape, conv_w.shape, ssm_c_re.shape, ssm_c_im.shape]):
        res[name] = tuple(a.reshape(shape) for a in quad)
    res["ssm_a_re"] = tuple(a.reshape(ssm_a_re.shape) for a in small[8])
    res["ssm_a_im"] = tuple(a.reshape(ssm_a_im.shape) for a in small[9])
    res["ssm_log_dt"] = tuple(a.reshape(ssm_log_dt.shape) for a in small[10])
    res["ssm_b_re"] = tuple(tr(a) for a in small[11])
    res["ssm_b_im"] = tuple(tr(a) for a in small[12])
    res["w_in"] = tuple(_adam_big("adam_w_in", recv_in, w_in, m_w_in, v_w_in, min(256, d_model)))
    res["w_out"] = tuple(_adam_big("adam_w_out", recv_out, w_out, m_w_out, v_w_out, min(128, r_out)))
    res["w_glu"] = tuple(_adam_big("adam_w_glu", recv_glu, w_glu, m_w_glu, v_w_glu, r_glu))

    order = ["norm_pre_g", "w_in", "conv_w", "conv_b", "ssm_a_re", "ssm_a_im", "ssm_log_dt", "ssm_b_re", "ssm_b_im",
             "ssm_c_re", "ssm_c_im", "ssm_d", "w_glu", "b_glu", "w_out", "norm_post_g"]
    loss = loss_sum[0, 0]
    grad_x = _unpermute_rows(gx_p)[None]
    return (loss, grad_x, *[res[n][0] for n in order], *[res[n][1] for n in order],
            *[res[n][2] for n in order], *[res[n][3] for n in order])
```

```python
import functools
import math

import jax
import jax.numpy as jnp
from jax import lax
from jax.experimental import pallas as pl
from jax.experimental.pallas import tpu as pltpu

F32 = jnp.float32
MXU_DTYPE = jnp.bfloat16
PROJ_DTYPE = jnp.bfloat16
AXES = ("x", "y", "c")
N_DEV = 8
N_CHUNK = 8
LANES = 128
SSM_GROUP = 16
SSM_STATE = 64
HALF_CH = 64
HALF_G = HALF_CH // SSM_GROUP
HALF_W = HALF_G * SSM_STATE
EPS = 1e-6
ADAM_LR, ADAM_B1, ADAM_B2, ADAM_EPS, ADAM_WD, ADAM_STEP = 0.001, 0.9, 0.999, 1e-08, 0.01, 10
GELU_C = math.sqrt(2.0 / math.pi)
GELU_K = 0.044715
VMEM_LIMIT = 56 * 1024 * 1024


def _params(sem=None):
    return pltpu.CompilerParams(dimension_semantics=sem, vmem_limit_bytes=VMEM_LIMIT)


def _dot(a, b):
    return jnp.dot(a, b, preferred_element_type=F32)


def _dot_nt(a, b):
    return lax.dot_general(a, b, (((1,), (1,)), ((), ())), preferred_element_type=F32)


def _dot_tn(a, b):
    return lax.dot_general(a, b, (((0,), (0,)), ((), ())), preferred_element_type=F32)


def _sigmoid(z):
    return 1.0 / (1.0 + jnp.exp(-z))


def _flip(v, bit):
    return 1 - v if bit else v


def _peers():
    x, y, c = (lax.axis_index(a) for a in AXES)
    out = []
    for m in range(1, N_DEV):
        px, py, pc = _flip(x, (m >> 2) & 1), _flip(y, (m >> 1) & 1), _flip(c, m & 1)
        out.append((px, py, pc, 4 * px + 2 * py + pc))
    return out


class _Comm:
    def __init__(self, gathers=(), scatters=(), dests=None, into=None):
        self.n_g = len(gathers)
        self.operands = list(gathers) + list(scatters)
        self.n = len(self.operands)
        self.dests = dests or {}
        self.into = into or {}

    def out_shape(self):
        return [jax.ShapeDtypeStruct((N_DEV,) + a.shape if t < self.n_g else a.shape, a.dtype)
                for t, a in enumerate(self.operands)]

    def scratch(self):
        if not self.n:
            return []
        return [pltpu.SemaphoreType.DMA((self.n, N_DEV - 1)), pltpu.SemaphoreType.DMA((self.n, N_DEV - 1)),
                pltpu.SemaphoreType.DMA((self.n,))]

    def _copies(self, in_refs, out_refs, sems, arrivals):
        send_sems, recv_sems, local_sems = sems
        x, y, c = (lax.axis_index(a) for a in AXES)
        me = 4 * x + 2 * y + c

        def src(t, dev):
            return in_refs[t] if t < self.n_g else in_refs[t].at[dev]

        def member(t, dev):
            if t not in self.dests:
                return None
            return functools.reduce(jnp.logical_or, [dev == d for d in self.dests[t]])

        local = [(member(t, me), pltpu.make_async_copy(src(t, me), out_refs[t].at[me], local_sems.at[t]))
                 for t in range(self.n)]
        sends, recvs = [], []
        for t in range(self.n):
            for m, (px, py, pc, peer) in enumerate(_peers()):
                kw = dict(send_sem=send_sems.at[t, m], recv_sem=recv_sems.at[t, m],
                          device_id=(px, py, pc), device_id_type=pl.DeviceIdType.MESH)
                sends.append((member(t, peer), pltpu.make_async_remote_copy(
                    src_ref=src(t, peer), dst_ref=out_refs[t].at[me], **kw)))
                if arrivals:
                    recvs.append((member(t, me), pltpu.make_async_remote_copy(
                        src_ref=src(t, peer), dst_ref=out_refs[t].at[peer], **kw)))
        return local, sends, recvs

    @staticmethod
    def _do(cond, action):
        if cond is None:
            action()
        else:
            pl.when(cond)(action)

    def start(self, in_refs, out_refs, sems):
        local, sends, _ = self._copies(in_refs, out_refs, sems, arrivals=False)
        for cond, cp in local + sends:
            self._do(cond, cp.start)

    def finish(self, in_refs, out_refs, sems):
        local, sends, recvs = self._copies(in_refs, out_refs, sems, arrivals=True)
        for cond, cp in recvs:
            self._do(cond, cp.wait_recv)
        for cond, cp in sends:
            self._do(cond, cp.wait_send)
        for cond, cp in local:
            self._do(cond, cp.wait)


def _call(body, comm, *, name, grid, in_specs, out_specs, out_shape, operands, scratch_shapes=()):
    comm = comm or _Comm()
    n_in, n_out, n_scr, cn = len(in_specs), len(out_specs), len(scratch_shapes), comm.n
    landing = sorted(comm.into)
    aliases = {n_in + cn + q: n_out + t for q, t in enumerate(landing)}

    def wrapped(*refs):
        parts, o = [], 0
        for k in (n_in, cn, len(landing), n_out, cn, n_scr):
            parts.append(refs[o:o + k])
            o += k
        h_in, c_in, _, h_out, c_out, h_scr = parts
        sems = refs[o:]
        if cn:
            first = functools.reduce(jnp.logical_and, [pl.program_id(d) == 0 for d in range(len(grid))])

            @pl.when(first)
            def _():
                comm.start(c_in, c_out, sems)

        body(*h_in, *h_out, *h_scr)
        if cn:
            last = functools.reduce(jnp.logical_and, [pl.program_id(d) == grid[d] - 1 for d in range(len(grid))])

            @pl.when(last)
            def _():
                comm.finish(c_in, c_out, sems)

    any_ = pl.BlockSpec(memory_space=pl.ANY)
    res = pl.pallas_call(
        wrapped, name=name, grid=grid, in_specs=list(in_specs) + [any_] * (cn + len(landing)),
        out_specs=list(out_specs) + [any_] * cn,
        out_shape=list(out_shape) + comm.out_shape(), scratch_shapes=list(scratch_shapes) + comm.scratch(),
        input_output_aliases=aliases, compiler_params=_params(("arbitrary",) * len(grid)),
    )(*operands, *comm.operands, *[comm.into[t] for t in landing])
    return list(res[:n_out]), list(res[n_out:])


def _exchange(name, gathers, scatters):
    def body(tok_ref):
        tok_ref[...] = jnp.zeros_like(tok_ref)

    return _call(body, _Comm(gathers, scatters), name=name, grid=(1,), in_specs=[],
                 out_specs=[pl.BlockSpec((8, LANES), lambda i: (0, 0))],
                 out_shape=[jax.ShapeDtypeStruct((8, LANES), F32)], operands=[])[1]


def _ssm_prep(a_re, a_im, log_dt, bt_re, bt_im):
    def body(lr_ref, li_ref, ldt_ref, br_ref, bi_ref, lbr_ref, lbi_ref, qr_ref, qi_ref, bbr_ref, bbi_ref):
        lr, li = lr_ref[...], li_ref[...]
        dt = jnp.exp(ldt_ref[...])
        mag = jnp.exp(lr * dt)
        lbr, lbi = mag * jnp.cos(li * dt), mag * jnp.sin(li * dt)
        nr, ni = lbr - 1.0, lbi
        den = lr * lr + li * li
        qr = (nr * lr + ni * li) / den
        qi = (ni * lr - nr * li) / den
        br, bi = br_ref[...], bi_ref[...]
        lbr_ref[...], lbi_ref[...], qr_ref[...], qi_ref[...] = lbr, lbi, qr, qi
        bbr_ref[...] = qr * br - qi * bi
        bbi_ref[...] = qr * bi + qi * br

    s2 = jax.ShapeDtypeStruct(a_re.shape, F32)
    s3 = jax.ShapeDtypeStruct(bt_re.shape, F32)
    return pl.pallas_call(body, name="ssm_prep", out_shape=[s2, s2, s2, s2, s3, s3],
                          compiler_params=_params())(a_re, a_im, log_dt, bt_re, bt_im)


def _adam(w, g, m, v):
    m2 = ADAM_B1 * m + (1.0 - ADAM_B1) * g
    v2 = ADAM_B2 * v + (1.0 - ADAM_B2) * (g * g)
    m_hat = m2 / (1.0 - ADAM_B1 ** ADAM_STEP)
    v_hat = v2 / (1.0 - ADAM_B2 ** ADAM_STEP)
    delta = -ADAM_LR * (m_hat / (jnp.sqrt(v_hat) + ADAM_EPS) + ADAM_WD * w)
    return delta, m2, v2


def _small_update(direct, ssm):
    n_direct = len(direct)
    flat = [a for quad in direct for a in quad]
    names = ["da_r", "da_i", "dbb_r", "dbb_i", "lr", "li", "ldt", "bt_r", "bt_i", "lbr", "lbi", "qr", "qi"]
    flat += [ssm[k] for k in names]
    chain = ["a_re", "a_im", "log_dt", "bt_re", "bt_im"]
    for k in chain:
        flat += [ssm["w_" + k], ssm["m_" + k], ssm["v_" + k]]
    n_in = len(flat)

    def body(*refs):
        ins, outs = refs[:n_in], refs[n_in:]
        for p in range(n_direct):
            g, w, m, v = (r[...] for r in ins[4 * p:4 * p + 4])
            d, m2, v2 = _adam(w, g, m, v)
            outs[4 * p][...], outs[4 * p + 1][...], outs[4 * p + 2][...], outs[4 * p + 3][...] = g, d, m2, v2
        o = 4 * n_direct
        da_r, da_i, dbb_r, dbb_i, lr, li, ldt, bt_r, bt_i, lbr, lbi, qr, qi = (r[...] for r in ins[o:o + 13])
        dt = jnp.exp(ldt)
        g_br = qr * dbb_r + qi * dbb_i
        g_bi = qr * dbb_i - qi * dbb_r
        dq_r = jnp.sum(bt_r * dbb_r + bt_i * dbb_i, axis=1, keepdims=True)
        dq_i = jnp.sum(bt_r * dbb_i - bt_i * dbb_r, axis=1, keepdims=True)
        den = lr * lr + li * li
        cr, ci = lr / den, li / den
        gl_r = da_r + (cr * dq_r - ci * dq_i)
        gl_i = da_i + (cr * dq_i + ci * dq_r)
        w_r = qr * cr + qi * ci
        w_i = qi * cr - qr * ci
        g_lr = dt * (lbr * gl_r + lbi * gl_i) + (-w_r * dq_r - w_i * dq_i)
        g_li = dt * (lbr * gl_i - lbi * gl_r) + (-w_r * dq_i + w_i * dq_r)
        m_r = lr * lbr - li * lbi
        m_i = lr * lbi + li * lbr
        g_ldt = jnp.sum(m_r * gl_r + m_i * gl_i, axis=2, keepdims=True) * dt
        grads = [g_lr, g_li, g_ldt, g_br, g_bi]
        base_in, base_out = o + 13, 4 * n_direct
        for p, g in enumerate(grads):
            w, m, v = (r[...] for r in ins[base_in + 3 * p:base_in + 3 * p + 3])
            d, m2, v2 = _adam(w, g, m, v)
            q = base_out + 4 * p
            outs[q][...], outs[q + 1][...], outs[q + 2][...], outs[q + 3][...] = g, d, m2, v2

    out_shape = []
    for quad in direct:
        out_shape += [jax.ShapeDtypeStruct(quad[1].shape, F32)] * 4
    for k in chain:
        out_shape += [jax.ShapeDtypeStruct(ssm["w_" + k].shape, F32)] * 4
    res = pl.pallas_call(body, name="small_update", out_shape=out_shape, compiler_params=_params())(*flat)
    return [tuple(res[4 * p:4 * p + 4]) for p in range(n_direct + len(chain))]


def _sum_slots(name, pack):
    def body(p_ref, o_ref):
        acc = p_ref[0].astype(F32)
        for k in range(1, N_DEV):
            acc = acc + p_ref[k].astype(F32)
        o_ref[...] = acc

    return pl.pallas_call(body, name=name, out_shape=jax.ShapeDtypeStruct(pack.shape[1:], F32),
                          compiler_params=_params())(pack)


def _adam_big(name, recv, w, m, v, tr):
    _, rows, cols = recv.shape

    def body(r_ref, w_ref, m_ref, v_ref, g_ref, d_ref, m2_ref, v2_ref):
        g = r_ref[0].astype(F32)
        for k in range(1, N_DEV):
            g = g + r_ref[k].astype(F32)
        d, m2, v2 = _adam(w_ref[...], g, m_ref[...], v_ref[...])
        g_ref[...], d_ref[...], m2_ref[...], v2_ref[...] = g, d, m2, v2

    blk = pl.BlockSpec((tr, cols), lambda i: (i, 0))
    shp = jax.ShapeDtypeStruct((rows, cols), F32)
    return pl.pallas_call(
        body, name=name, grid=(rows // tr,),
        in_specs=[pl.BlockSpec((N_DEV, tr, cols), lambda i: (0, i, 0)), blk, blk, blk],
        out_specs=[blk] * 4, out_shape=[shp] * 4, compiler_params=_params(("parallel",)),
    )(recv, w, m, v)


def _chunk_block(tm, d):
    return pl.BlockSpec((N_CHUNK, tm // N_CHUNK, d), lambda i: (0, i, 0))


def _interleave(block):
    c, n, d = block.shape
    return pltpu.einshape("cjd->jcd", block).reshape(n * c, d)


def _norm_in(x3, g_pre, tm):
    _, steps, d_model = x3.shape
    seq = steps * N_CHUNK

    def body(x_ref, g_ref, xp_ref, h_ref, ht_ref):
        x = _interleave(x_ref[...])
        xp_ref[...] = x
        r = lax.rsqrt(jnp.mean(x * x, axis=-1, keepdims=True) + EPS)
        h = x * r * g_ref[...]
        h_ref[...] = h.astype(h_ref.dtype)
        ht_ref[...] = h.T.astype(ht_ref.dtype)

    rows = pl.BlockSpec((tm, d_model), lambda i: (i, 0))
    return pl.pallas_call(
        body, name="norm_in", grid=(seq // tm,),
        in_specs=[_chunk_block(tm, d_model), pl.BlockSpec((1, d_model), lambda i: (0, 0))],
        out_specs=[rows, rows, pl.BlockSpec((d_model, tm), lambda i: (0, i))],
        out_shape=[jax.ShapeDtypeStruct((seq, d_model), F32), jax.ShapeDtypeStruct((seq, d_model), MXU_DTYPE),
                   jax.ShapeDtypeStruct((d_model, seq), MXU_DTYPE)],
        compiler_params=_params(("parallel",)),
    )(x3, g_pre)


GATHER_ORDER = (0, 1, 4, 2, 6, 5, 3, 7)
PIECE_COLS = 384


def _fwd_in(h, w_pieces, me, comm, tm):
    seq, d_model = h.shape
    n_p, _, gw = w_pieces.shape
    n_i = seq // tm
    cn = comm.n
    consume = [(rel, r) for rel in (0, 1) for r in range(n_p)]
    consume += [(rel, r) for r in range(n_p) for rel in (4, 2, 5, 3)]
    consume += [(rel, r) for r in range(n_p) for rel in (6, 7)]
    n_q = len(consume)
    order = jnp.stack([jnp.bitwise_xor(me, rel) * n_p + r for rel, r in consume]).astype(jnp.int32)

    def body(order_ref, h_hbm, w_hbm, *rest):
        c_in, rest = rest[:cn], rest[cn:]
        proj_hbm, wing = rest[0], rest[1]
        c_out, rest = rest[2:2 + cn], rest[2 + cn:]
        wbuf, send_sems, recv_sems, own_sems, load_sems, h_all, h_sems, out_buf, out_sems = rest[:9]
        c_sems = rest[9:]
        h_loads = [pltpu.make_async_copy(h_hbm.at[pl.ds(t * tm, tm)], h_all.at[pl.ds(t * tm, tm)], h_sems.at[t])
                   for t in range(n_i)]
        x, y, c = (lax.axis_index(a) for a in AXES)
        me_ = 4 * x + 2 * y + c

        def dev(rel):
            return _flip(x, (rel >> 2) & 1), _flip(y, (rel >> 1) & 1), _flip(c, rel & 1)

        def slot(rel):
            px, py, pc = dev(rel)
            return 4 * px + 2 * py + pc

        def remote(src, block, r, to_rel, sem):
            return pltpu.make_async_remote_copy(
                src_ref=src, dst_ref=wing.at[block, r], send_sem=send_sems.at[sem, r], recv_sem=recv_sems.at[sem, r],
                device_id=dev(to_rel), device_id_type=pl.DeviceIdType.MESH)

        pieces = range(n_p)
        own = [pltpu.make_async_copy(w_hbm.at[r], wing.at[me_, r], own_sems.at[r]) for r in pieces]
        first_hand = {p: [remote(w_hbm.at[r], me_, r, GATHER_ORDER[p], p - 1) for r in pieces] for p in (1, 2, 3)}
        relay = {2: (c == 0, [remote(wing.at[slot(4), r], slot(4), r, 2, 3) for r in pieces]),
                 3: (c == 1, [remote(wing.at[slot(2), r], slot(2), r, 4, 3) for r in pieces])}
        passed_on = {p: [remote(wing.at[slot(GATHER_ORDER[p]), r], slot(GATHER_ORDER[p]), r, 1, p + 2) for r in pieces]
                     for p in (2, 3, 4)}

        def load(q):
            rel, r = consume[q]
            return pltpu.make_async_copy(wing.at[slot(rel), r], wbuf.at[q % 2], load_sems.at[q % 2])

        def take(q):
            rel, r = consume[q]
            p = GATHER_ORDER.index(rel)
            if p == 0:
                own[r].wait()
            else:
                remote(w_hbm.at[r], slot(rel), r, rel, p - 1).wait_recv()
            if p in relay:
                pl.when(relay[p][0])(relay[p][1][r].start)
            if p in passed_on:
                passed_on[p][r].start()
            load(q).start()

        for r in pieces:
            own[r].start()
        for r in pieces:
            for p in (1, 2, 3):
                first_hand[p][r].start()
        for cp in h_loads:
            cp.start()
        comm.start(c_in, c_out, c_sems)
        take(0)

        def store(s, k, i):
            col = pl.multiple_of(order_ref[k] * gw, gw)
            return pltpu.make_async_copy(
                out_buf.at[s % 2], proj_hbm.at[pl.ds(pl.multiple_of(i * tm, tm), tm), pl.ds(col, gw)],
                out_sems.at[s % 2])

        def step(s, carry):
            k, i = s // n_i, s % n_i
            for t in range(n_i):
                pl.when((k == 0) & (i == t))(h_loads[t].wait)
            for q in range(n_q):
                @pl.when((k == q) & (i == 0))
                def _():
                    load(q).wait()

                if q + 1 < n_q:
                    @pl.when((k == q) & (i == n_i - 1))
                    def _():
                        take(q + 1)

            @pl.when(s >= 2)
            def _():
                store(s, k, i).wait()

            out_buf[s % 2] = _dot(h_all[pl.ds(pl.multiple_of(i * tm, tm), tm), :], wbuf[k % 2]).astype(out_buf.dtype)
            store(s, k, i).start()
            return carry

        n_s = n_q * n_i
        lax.fori_loop(0, n_s, step, 0)
        for s in range(max(n_s - 2, 0), n_s):
            store(s, s // n_i, s % n_i).wait()
        for p in first_hand:
            for cp in first_hand[p]:
                cp.wait_send()
        for p in passed_on:
            for cp in passed_on[p]:
                cp.wait_send()
        for cond, cps in relay.values():
            for cp in cps:
                pl.when(cond)(cp.wait_send)
        comm.finish(c_in, c_out, c_sems)

    any_ = pl.BlockSpec(memory_space=pl.ANY)
    grid_spec = pltpu.PrefetchScalarGridSpec(
        num_scalar_prefetch=1, grid=(1,),
        in_specs=[any_, any_] + [any_] * cn,
        out_specs=[any_, any_] + [any_] * cn,
        scratch_shapes=[pltpu.VMEM((2, d_model, gw), w_pieces.dtype), pltpu.SemaphoreType.DMA((N_DEV - 1, n_p)),
                        pltpu.SemaphoreType.DMA((N_DEV - 1, n_p)), pltpu.SemaphoreType.DMA((n_p,)),
                        pltpu.SemaphoreType.DMA((2,)), pltpu.VMEM((seq, d_model), h.dtype),
                        pltpu.SemaphoreType.DMA((n_i,)), pltpu.VMEM((2, tm, gw), PROJ_DTYPE),
                        pltpu.SemaphoreType.DMA((2,))]
        + comm.scratch())
    res = pl.pallas_call(
        body, name="fwd_in", grid_spec=grid_spec,
        out_shape=[jax.ShapeDtypeStruct((seq, N_DEV * n_p * gw), PROJ_DTYPE),
                   jax.ShapeDtypeStruct((N_DEV, n_p, d_model, gw), w_pieces.dtype)] + comm.out_shape(),
        compiler_params=_params(("arbitrary",)),
    )(order, h, w_pieces, *comm.operands)
    return res[0], res[1], list(res[2:])


def _shift_prev(a):
    n = a.shape[0]
    last = a[n - N_CHUNK:, :]
    row = lax.broadcasted_iota(jnp.int32, last.shape, 0)
    wrap = jnp.where(row == 0, 0.0, pltpu.roll(last, 1, axis=0))
    return jnp.concatenate([wrap, a[:n - N_CHUNK, :]], axis=0)


def _shift_next(a):
    first = a[:N_CHUNK, :]
    row = lax.broadcasted_iota(jnp.int32, first.shape, 0)
    wrap = jnp.where(row == N_CHUNK - 1, 0.0, pltpu.roll(first, N_CHUNK - 1, axis=0))
    return jnp.concatenate([a[N_CHUNK:, :], wrap], axis=0)


def _conv_specs(seq, d_conv):
    nblk = d_conv // LANES
    return [pl.BlockSpec((seq, LANES), functools.partial(lambda i, o: (0, o + i), o=q * nblk)) for q in range(4)]


def _conv_fwd(proj, conv_w8, conv_b, d_conv):
    seq = proj.shape[0]

    def body(bg_ref, cg_ref, v_ref, zc_ref, w_ref, b_ref, y_ref):
        cv = cg_ref[...].astype(F32) * v_ref[...].astype(F32)
        s1 = _shift_prev(cv)
        s2 = _shift_prev(s1)
        conv = b_ref[...] + w_ref[0:1, :] * s2 + w_ref[1:2, :] * s1 + w_ref[2:3, :] * cv
        z = zc_ref[...].astype(F32)
        y_ref[...] = (bg_ref[...].astype(F32) * conv * (z * _sigmoid(z))).astype(y_ref.dtype)

    col = pl.BlockSpec((seq, LANES), lambda i: (0, i))
    return pl.pallas_call(
        body, name="conv_fwd", grid=(d_conv // LANES,),
        in_specs=_conv_specs(seq, d_conv) + [pl.BlockSpec((8, LANES), lambda i: (0, i)), pl.BlockSpec((1, LANES), lambda i: (0, i))],
        out_specs=col, out_shape=jax.ShapeDtypeStruct((seq, d_conv), MXU_DTYPE),
        compiler_params=_params(("parallel",)),
    )(proj, proj, proj, proj, conv_w8, conv_b)


def _conv_bwd(proj, dyc, conv_w8, conv_b, d_conv, comm=None):
    seq = proj.shape[0]

    def body(bg_ref, cg_ref, v_ref, zc_ref, dy_ref, w_ref, b_ref, d4_ref, dcb_ref, dcw_ref):
        bg, cg, v, z = (r[...].astype(F32) for r in (bg_ref, cg_ref, v_ref, zc_ref))
        w0, w1, w2 = w_ref[0:1, :], w_ref[1:2, :], w_ref[2:3, :]
        cv = cg * v
        s1 = _shift_prev(cv)
        s2 = _shift_prev(s1)
        conv = b_ref[...] + w0 * s2 + w1 * s1 + w2 * cv
        sig = _sigmoid(z)
        dy = dy_ref[...].astype(F32)
        g1 = dy * (z * sig)
        d_conv_ = g1 * bg
        d4_ref[0] = (g1 * conv).astype(d4_ref.dtype)
        d4_ref[3] = (dy * bg * conv * (sig * (1.0 + z * (1.0 - sig)))).astype(d4_ref.dtype)
        n1 = _shift_next(d_conv_)
        n2 = _shift_next(n1)
        d_cv = w2 * d_conv_ + w1 * n1 + w0 * n2
        d4_ref[1] = (d_cv * v).astype(d4_ref.dtype)
        d4_ref[2] = (d_cv * cg).astype(d4_ref.dtype)
        dcb_ref[...] = jnp.sum(d_conv_, axis=0, keepdims=True)
        rows = [jnp.sum(d_conv_ * s, axis=0, keepdims=True) for s in (s2, s1, cv)]
        dcw_ref[...] = jnp.concatenate(rows + [jnp.zeros((5, LANES), F32)], axis=0)

    col = pl.BlockSpec((seq, LANES), lambda i: (0, i))
    return _call(
        body, comm, name="conv_bwd", grid=(d_conv // LANES,),
        in_specs=_conv_specs(seq, d_conv) + [col, pl.BlockSpec((8, LANES), lambda i: (0, i)), pl.BlockSpec((1, LANES), lambda i: (0, i))],
        out_specs=[pl.BlockSpec((4, seq, LANES), lambda i: (0, 0, i)), pl.BlockSpec((1, LANES), lambda i: (0, i)),
                   pl.BlockSpec((8, LANES), lambda i: (0, i))],
        out_shape=[jax.ShapeDtypeStruct((4, seq, d_conv), MXU_DTYPE), jax.ShapeDtypeStruct((1, d_conv), F32),
                   jax.ShapeDtypeStruct((8, d_conv), F32)],
        operands=[proj, proj, proj, proj, dyc, conv_w8, conv_b])


def _cmul(ar, ai, br, bi):
    return ar * br - ai * bi, ar * bi + ai * br


def _down(v, k):
    row = lax.broadcasted_iota(jnp.int32, v.shape, 0)
    return jnp.where(row >= k, pltpu.roll(v, k, axis=0), 0.0)


def _up(v, k):
    row = lax.broadcasted_iota(jnp.int32, v.shape, 0)
    return jnp.where(row < N_CHUNK - k, pltpu.roll(v, N_CHUNK - k, axis=0), 0.0)


def _chunk_carry(fr, fi, mr, mi, shift):
    vr, vi = shift(fr, 1), shift(fi, 1)
    for k in (1, 2, 4):
        pr, pi = _cmul(mr, mi, shift(vr, k), shift(vi, k))
        vr, vi = vr + pr, vi + pi
        mr, mi = _cmul(mr, mi, mr, mi)
    return vr, vi


def _tile(ref, j, width, part):
    return ref.at[pl.ds(pl.multiple_of(j * N_CHUNK, N_CHUNK), N_CHUNK), pl.ds(part * width, width)]


def _row(t, k):
    return jnp.broadcast_to(t[k:k + 1, :], t.shape)


def _power_table(tab_ref, ar, ai, steps, width):
    e = lax.broadcasted_iota(jnp.int32, ar.shape, 0) + 1
    rr, ri = jnp.ones_like(ar), jnp.zeros_like(ai)
    br, bi = ar, ai
    for bit in range(4):
        mr, mi = _cmul(rr, ri, br, bi)
        take = ((e >> bit) & 1) == 1
        rr, ri = jnp.where(take, mr, rr), jnp.where(take, mi, ri)
        if bit < 3:
            br, bi = _cmul(br, bi, br, bi)
    _tile(tab_ref, 0, width, 0)[...] = rr
    _tile(tab_ref, 0, width, 1)[...] = ri

    def step(m, carry):
        tr, ti = _cmul(carry[0], carry[1], br, bi)
        _tile(tab_ref, m, width, 0)[...] = tr
        _tile(tab_ref, m, width, 1)[...] = ti
        return tr, ti

    lax.fori_loop(1, steps // N_CHUNK, step, (rr, ri))


def _last_power(tab_ref, steps, width):
    shape = (N_CHUNK, width)
    return (jnp.broadcast_to(tab_ref[steps - 1:steps, 0:width], shape),
            jnp.broadcast_to(tab_ref[steps - 1:steps, width:2 * width], shape))


def _scan_fwd(s_ref, ar, ai, steps, width):
    def step(j, carry):
        sr, si = carry
        nr = ar * sr - ai * si + _tile(s_ref, j, width, 0)[...]
        ni = ar * si + ai * sr + _tile(s_ref, j, width, 1)[...]
        _tile(s_ref, j, width, 0)[...] = nr
        _tile(s_ref, j, width, 1)[...] = ni
        return nr, ni

    z = jnp.zeros((N_CHUNK, width), F32)
    return lax.fori_loop(0, steps, step, (z, z), unroll=4)


def _scan_both(s_ref, g_ref, ar, ai, steps, width):
    def step(q, carry):
        sr, si, gr, gi = carry
        j, jb = q, steps - 1 - q
        nsr = ar * sr - ai * si + _tile(s_ref, j, width, 0)[...]
        nsi = ar * si + ai * sr + _tile(s_ref, j, width, 1)[...]
        ngr = ar * gr + ai * gi + _tile(g_ref, jb, width, 0)[...]
        ngi = ar * gi - ai * gr + _tile(g_ref, jb, width, 1)[...]
        _tile(s_ref, j, width, 0)[...] = nsr
        _tile(s_ref, j, width, 1)[...] = nsi
        _tile(g_ref, jb, width, 0)[...] = ngr
        _tile(g_ref, jb, width, 1)[...] = ngi
        return nsr, nsi, ngr, ngi

    z = jnp.zeros((N_CHUNK, width), F32)
    return lax.fori_loop(0, steps, step, (z, z, z, z), unroll=2)


def _patch_fwd(s_ref, tab_ref, cr, ci, steps, width):
    def tile(m, _):
        tr, ti = _tile(tab_ref, m, width, 0)[...], _tile(tab_ref, m, width, 1)[...]
        for k in range(N_CHUNK):
            fr, fi = _cmul(_row(tr, k), _row(ti, k), cr, ci)
            j = m * N_CHUNK + k
            _tile(s_ref, j, width, 0)[...] += fr
            _tile(s_ref, j, width, 1)[...] += fi
        return 0

    lax.fori_loop(0, steps // N_CHUNK, tile, 0)


def _lam_rows(lam_ref, hh, width):
    return (jnp.broadcast_to(lam_ref[hh, 0:1, :], (N_CHUNK, width)),
            jnp.broadcast_to(lam_ref[hh, 1:2, :], (N_CHUNK, width)))


def _ssm_specs(seq, col0):
    return dict(
        col=pl.BlockSpec((seq, LANES), lambda i: (0, col0 + i)),
        lam=pl.BlockSpec((2, 2, HALF_W), lambda i: (i, 0, 0)),
        bb=pl.BlockSpec((2, HALF_CH, 2 * HALF_W), lambda i: (i, 0, 0)),
        cc=pl.BlockSpec((2, 2 * HALF_W, HALF_CH), lambda i: (i, 0, 0)),
        vec=pl.BlockSpec((1, LANES), lambda i: (0, i)),
        out=pl.BlockSpec((seq, LANES), lambda i: (0, i)),
    )


def _ssm_fwd(proj, lam, bbcat, cccat, d_skip, d_ssm, u_col0, comm=None):
    seq = proj.shape[0]
    steps = seq // N_CHUNK

    def body(u_ref, lam_ref, bb_ref, cc_ref, d_ref, yp_ref, s_ref, tab_ref):
        for hh in range(2):
            lanes = slice(HALF_CH * hh, HALF_CH * (hh + 1))
            u_half = u_ref[:, lanes].astype(F32)
            ar, ai = _lam_rows(lam_ref, hh, HALF_W)
            _power_table(tab_ref, ar, ai, steps, HALF_W)
            s_ref[...] = _dot(u_half.astype(MXU_DTYPE), bb_ref[hh])
            fr, fi = _scan_fwd(s_ref, ar, ai, steps, HALF_W)
            pr, pi = _last_power(tab_ref, steps, HALF_W)
            cr, ci = _chunk_carry(fr, fi, pr, pi, _down)
            _patch_fwd(s_ref, tab_ref, cr, ci, steps, HALF_W)
            y = _dot(s_ref[...].astype(MXU_DTYPE), cc_ref[hh])
            yp_ref[:, lanes] = y + d_ref[:, lanes] * u_half

    sp = _ssm_specs(seq, u_col0 // LANES)
    return _call(
        body, comm, name="ssm_fwd", grid=(d_ssm // LANES,),
        in_specs=[sp["col"], sp["lam"], sp["bb"], sp["cc"], sp["vec"]], out_specs=[sp["out"]],
        out_shape=[jax.ShapeDtypeStruct((seq, d_ssm), F32)],
        scratch_shapes=[pltpu.VMEM((seq, 2 * HALF_W), F32), pltpu.VMEM((steps, 2 * HALF_W), F32)],
        operands=[proj, lam, bbcat, cccat, d_skip])


def _ssm_bwd(proj, dyp, lam, bbcat, cccat, d_skip, d_ssm, u_col0, comm=None):
    seq = proj.shape[0]
    steps = seq // N_CHUNK
    n_half = 2 * d_ssm // LANES
    width = HALF_W

    def body(u_ref, dyp_ref, lam_ref, bb_ref, cc_ref, d_ref, du_ref, dbb_ref, dcc_ref, da_ref, dd_ref,
             s_ref, g_ref, tab_ref):
        n_tiles = steps // N_CHUNK
        for hh in range(2):
            lanes = slice(HALF_CH * hh, HALF_CH * (hh + 1))
            u_half, dy_half = u_ref[:, lanes].astype(F32), dyp_ref[:, lanes].astype(F32)
            dy_mx = dy_half.astype(MXU_DTYPE)
            ar, ai = _lam_rows(lam_ref, hh, width)
            _power_table(tab_ref, ar, ai, steps, width)
            s_ref[...] = _dot(u_half.astype(MXU_DTYPE), bb_ref[hh])
            g_ref[...] = _dot_nt(dy_mx, cc_ref[hh])
            fr, fi, lr_, li_ = _scan_both(s_ref, g_ref, ar, ai, steps, width)
            pr, pi = _last_power(tab_ref, steps, width)
            cr, ci = _chunk_carry(fr, fi, pr, pi, _down)
            gr, gi = _chunk_carry(lr_, li_, pr, -pi, _up)

            def tile(m, carry):
                sr, si, accr, acci = carry
                t1r, t1i = _tile(tab_ref, m, width, 0)[...], _tile(tab_ref, m, width, 1)[...]
                mb = n_tiles - 1 - m
                t2r, t2i = _tile(tab_ref, mb, width, 0)[...], _tile(tab_ref, mb, width, 1)[...]
                for k in range(N_CHUNK):
                    j = m * N_CHUNK + k
                    xr, xi = _cmul(_row(t1r, k), _row(t1i, k), cr, ci)
                    nsr = _tile(s_ref, j, width, 0)[...] + xr
                    nsi = _tile(s_ref, j, width, 1)[...] + xi
                    _tile(s_ref, j, width, 0)[...] = nsr
                    _tile(s_ref, j, width, 1)[...] = nsi
                    qr, qi = _row(t2r, N_CHUNK - 1 - k), _row(t2i, N_CHUNK - 1 - k)
                    ngr = _tile(g_ref, j, width, 0)[...] + (qr * gr + qi * gi)
                    ngi = _tile(g_ref, j, width, 1)[...] + (qr * gi - qi * gr)
                    _tile(g_ref, j, width, 0)[...] = ngr
                    _tile(g_ref, j, width, 1)[...] = ngi
                    accr = accr + (sr * ngr + si * ngi)
                    acci = acci + (sr * ngi - si * ngr)
                    sr, si = nsr, nsi
                return sr, si, accr, acci

            z = jnp.zeros((N_CHUNK, width), F32)
            _, _, accr, acci = lax.fori_loop(0, n_tiles, tile, (cr, ci, z, z))
            da_ref[hh, :, 0:width] = jnp.sum(accr, axis=0, keepdims=True)
            da_ref[hh, :, width:2 * width] = jnp.sum(acci, axis=0, keepdims=True)

            g_mx = g_ref[...].astype(MXU_DTYPE)
            dcc_ref[hh] = _dot_tn(dy_mx, s_ref[...].astype(MXU_DTYPE)).T
            dbb_ref[hh] = _dot_tn(u_half.astype(MXU_DTYPE), g_mx)
            du = _dot_nt(g_mx, bb_ref[hh]) + d_ref[:, lanes] * dy_half
            du_ref[:, lanes] = du.astype(du_ref.dtype)
            dd_ref[:, lanes] = jnp.sum(dy_half * u_half, axis=0, keepdims=True)

    sp = _ssm_specs(seq, u_col0 // LANES)
    return _call(
        body, comm, name="ssm_bwd", grid=(d_ssm // LANES,),
        in_specs=[sp["col"], sp["out"], sp["lam"], sp["bb"], sp["cc"], sp["vec"]],
        out_specs=[sp["out"], sp["bb"], sp["cc"], pl.BlockSpec((2, 1, 2 * width), lambda i: (i, 0, 0)), sp["vec"]],
        out_shape=[jax.ShapeDtypeStruct((seq, d_ssm), MXU_DTYPE),
                   jax.ShapeDtypeStruct((n_half, HALF_CH, 2 * width), F32),
                   jax.ShapeDtypeStruct((n_half, 2 * width, HALF_CH), F32),
                   jax.ShapeDtypeStruct((n_half, 1, 2 * width), F32),
                   jax.ShapeDtypeStruct((1, d_ssm), F32)],
        scratch_shapes=[pltpu.VMEM((seq, 2 * width), F32), pltpu.VMEM((seq, 2 * width), F32),
                        pltpu.VMEM((steps, 2 * width), F32)],
        operands=[proj, dyp, lam, bbcat, cccat, d_skip])


def _tail(xp, t3, proj, yconv, yp, w_glu, b_glu, w_out, g_post, zs_col0, tm):
    seq, d_model = xp.shape
    d_conv, d_ssm = yconv.shape[1], yp.shape[1]
    d_mix = d_conv + d_ssm
    assert zs_col0 % d_ssm == 0

    def body(x_ref, t_ref, zs_ref, yc_ref, yp_ref, wglu_hbm, bglu_ref, wout_hbm, gpost_ref,
             dy_ref, do_ref, mixt_ref, dyc_ref, dyp_ref, dzs_ref, ygt_ref, dq_ref, loss_ref, dgpost_ref, dbglu_ref,
             wglu, wout):
        @pl.when(pl.program_id(0) == 0)
        def _():
            pltpu.sync_copy(wglu_hbm, wglu)
            pltpu.sync_copy(wout_hbm, wout)
            loss_ref[...] = jnp.zeros_like(loss_ref)
            dgpost_ref[...] = jnp.zeros_like(dgpost_ref)
            dbglu_ref[...] = jnp.zeros_like(dbglu_ref)

        a = yp_ref[...]
        th = jnp.tanh(GELU_C * (a + GELU_K * (a * a * a)))
        yg = a * (0.5 * (1.0 + th))
        dgelu = 0.5 * (1.0 + th) + 0.5 * a * (1.0 - th * th) * (GELU_C * (1.0 + 3.0 * GELU_K * a * a))
        yg_mx = yg.astype(MXU_DTYPE)
        sq = _sigmoid(_dot(yg_mx, wglu[...]) + bglu_ref[...])
        y2 = yg * sq
        zs = zs_ref[...].astype(F32)
        sz = _sigmoid(zs)
        silz = zs * sz
        yc, ys = yc_ref[...].astype(F32), y2 * silz
        mix = jnp.concatenate([yc, ys], axis=1).astype(MXU_DTYPE)
        mixt_ref[0:d_conv, :] = yc.T.astype(MXU_DTYPE)
        mixt_ref[d_conv:, :] = ys.T.astype(MXU_DTYPE)
        o = _dot(mix, wout[...])
        r2 = lax.rsqrt(jnp.mean(o * o, axis=-1, keepdims=True) + EPS)
        on = o * r2
        gpost = gpost_ref[...]
        err = (x_ref[...] + on * gpost) - _interleave(t_ref[...])
        loss_ref[...] += 0.5 * jnp.sum(jnp.mean(err * err, axis=-1, keepdims=True), axis=0, keepdims=True)
        dy = err * (1.0 / d_model)
        dy_ref[...] = dy
        dgpost_ref[...] += jnp.sum(dy * on, axis=0, keepdims=True)
        d_on = dy * gpost
        d_o = r2 * (d_on - on * jnp.mean(d_on * on, axis=-1, keepdims=True))
        do_mx = d_o.astype(MXU_DTYPE)
        do_ref[...] = do_mx
        d_mix_ = _dot_nt(do_mx, wout[...])
        dyc_ref[...] = d_mix_[:, :d_conv].astype(dyc_ref.dtype)
        d_yssm = d_mix_[:, d_conv:]
        d_y2 = d_yssm * silz
        dzs_ref[...] = (d_yssm * y2 * (sz * (1.0 + zs * (1.0 - sz)))).astype(dzs_ref.dtype)
        d_q = d_y2 * yg * (sq * (1.0 - sq))
        dq_mx = d_q.astype(MXU_DTYPE)
        dq_ref[...] = dq_mx
        ygt_ref[...] = yg.T.astype(MXU_DTYPE)
        dbglu_ref[...] += jnp.sum(d_q, axis=0, keepdims=True)
        d_yg = d_y2 * sq + _dot_nt(dq_mx, wglu[...])
        dyp_ref[...] = (d_yg * dgelu).astype(dyp_ref.dtype)

    def rows(width, col=0):
        return pl.BlockSpec((tm, width), lambda i: (i, col))

    def fixed(width):
        return pl.BlockSpec((1, width), lambda i: (0, 0))

    def cols(height):
        return pl.BlockSpec((height, tm), lambda i: (0, i))

    any_ = pl.BlockSpec(memory_space=pl.ANY)
    return pl.pallas_call(
        body, name="tail", grid=(seq // tm,),
        in_specs=[rows(d_model), _chunk_block(tm, d_model), rows(d_ssm, zs_col0 // d_ssm), rows(d_conv), rows(d_ssm),
                  any_, fixed(d_ssm), any_, fixed(d_model)],
        out_specs=[rows(d_model), rows(d_model), cols(d_mix), rows(d_conv), rows(d_ssm), rows(d_ssm), cols(d_ssm),
                   rows(d_ssm), fixed(LANES), fixed(d_model), fixed(d_ssm)],
        out_shape=[jax.ShapeDtypeStruct((seq, d_model), F32), jax.ShapeDtypeStruct((seq, d_model), MXU_DTYPE),
                   jax.ShapeDtypeStruct((d_mix, seq), MXU_DTYPE), jax.ShapeDtypeStruct((seq, d_conv), MXU_DTYPE),
                   jax.ShapeDtypeStruct((seq, d_ssm), MXU_DTYPE), jax.ShapeDtypeStruct((seq, d_ssm), MXU_DTYPE),
                   jax.ShapeDtypeStruct((d_ssm, seq), MXU_DTYPE), jax.ShapeDtypeStruct((seq, d_ssm), MXU_DTYPE),
                   jax.ShapeDtypeStruct((1, LANES), F32), jax.ShapeDtypeStruct((1, d_model), F32),
                   jax.ShapeDtypeStruct((1, d_ssm), F32)],
        scratch_shapes=[pltpu.VMEM(w_glu.shape, MXU_DTYPE), pltpu.VMEM(w_out.shape, MXU_DTYPE)],
        compiler_params=_params(("arbitrary",)),
    )(xp, t3, proj, yconv, yp, w_glu, b_glu, w_out, g_post)


def _bwd_in(d4, du, dzs, gr, win_g, xp, dy, g_pre, comm, tm):
    seq, d_model = xp.shape
    nb, n_p, _, gw = win_g.shape
    nc = n_p * gw
    per = d4.shape[2] // gr

    def body(d4_ref, du_ref, dzs_ref, w_hbm, x_ref, dy_ref, g_ref, gx_ref, dg_ref, w_all, w_sems):
        def granule(g):
            p, cols = g // per, slice(g % per * gr, (g % per + 1) * gr)
            if p < 4:
                return d4_ref[p, :, cols]
            return du_ref[:, cols] if p == 4 else dzs_ref[:, cols]

        i = pl.program_id(0)
        loads = [[pltpu.make_async_copy(w_hbm.at[k, r], w_all.at[k, :, pl.ds(r * gw, gw)], w_sems.at[k, r])
                  for r in range(n_p)] for k in range(nb)]

        @pl.when(i == 0)
        def _():
            dg_ref[...] = jnp.zeros_like(dg_ref)
            for row in loads:
                for cp in row:
                    cp.start()

        dh = None
        for k in range(nb):
            @pl.when(i == 0)
            def _():
                for cp in loads[k]:
                    cp.wait()

            dp = jnp.concatenate([granule(g) for g in range(k * nc // gr, (k + 1) * nc // gr)], axis=1)
            part = _dot_nt(dp, w_all[k])
            dh = part if dh is None else dh + part

        x = x_ref[...]
        r = lax.rsqrt(jnp.mean(x * x, axis=-1, keepdims=True) + EPS)
        xn = x * r
        dg_ref[...] += jnp.sum(dh * xn, axis=0, keepdims=True)
        dxn = dh * g_ref[...]
        gx_ref[...] = r * (dxn - xn * jnp.mean(dxn * xn, axis=-1, keepdims=True)) + dy_ref[...]

    row = pl.BlockSpec((tm, d_model), lambda i: (i, 0))
    vec = pl.BlockSpec((1, d_model), lambda i: (0, 0))
    return _call(
        body, comm, name="bwd_in", grid=(seq // tm,),
        in_specs=[pl.BlockSpec((4, tm, d4.shape[2]), lambda i: (0, i, 0)),
                  pl.BlockSpec((tm, du.shape[1]), lambda i: (i, 0)), pl.BlockSpec((tm, dzs.shape[1]), lambda i: (i, 0)),
                  pl.BlockSpec(memory_space=pl.ANY), row, row, vec],
        out_specs=[row, vec],
        out_shape=[jax.ShapeDtypeStruct((seq, d_model), F32), jax.ShapeDtypeStruct((1, d_model), F32)],
        scratch_shapes=[pltpu.VMEM((nb, d_model, nc), win_g.dtype), pltpu.SemaphoreType.DMA((nb, n_p))],
        operands=[d4, du, dzs, win_g, xp, dy, g_pre])


def _lookup(g, table):
    out = jnp.int32(table[0])
    for gi in range(1, len(table)):
        if table[gi] != table[gi - 1]:
            out = jnp.where(g >= gi, jnp.int32(table[gi]), out)
    return out


def _held(values, used):
    cur = next(v for v, u in zip(values, used) if u)
    out = []
    for v, u in zip(values, used):
        cur = v if u else cur
        out.append(cur)
    return out


def _dw_in(name, ht, d4, du, dzs, granules, gr, nc, tm, comm=None):
    d_model, seq = ht.shape
    per = d4.shape[2] // gr
    piece, col = [g // per for g in granules], [g % per for g in granules]
    sources = [(d4, [p < 4 for p in piece]), (du, [p == 4 for p in piece]), (dzs, [p == 5 for p in piece])]
    sources = [(a, used) for a, used in sources if any(used)]
    select = [next(s for s, (_, used) in enumerate(sources) if used[q]) for q in range(len(granules))]
    owner, place = [g * gr // nc for g in granules], [g * gr % nc // gr for g in granules]

    def body(a_ref, *refs):
        src_refs, o_ref = refs[:-1], refs[-1]
        j = pl.program_id(1)
        for s, ref in enumerate(src_refs):
            @pl.when(_lookup(j, select) == s)
            def _():
                o_ref[...] = _dot(a_ref[...], ref[...]).astype(o_ref.dtype)

    in_specs = [pl.BlockSpec((tm, seq), lambda i, j: (i, 0))]
    for a, used in sources:
        cols = _held(col, used)
        if a.ndim == 3:
            rows = _held(piece, used)
            in_specs.append(pl.BlockSpec((None, seq, gr), functools.partial(
                lambda i, j, rows, cols: (_lookup(j, rows), 0, _lookup(j, cols)), rows=rows, cols=cols)))
        else:
            in_specs.append(pl.BlockSpec((seq, gr), functools.partial(
                lambda i, j, cols: (0, _lookup(j, cols)), cols=cols)))
    return _call(
        body, comm, name=name, grid=(d_model // tm, len(granules)), in_specs=in_specs,
        out_specs=[pl.BlockSpec((None, tm, gr), lambda i, j: (_lookup(j, owner), i, _lookup(j, place)))],
        out_shape=[jax.ShapeDtypeStruct((N_DEV, d_model, nc), MXU_DTYPE)],
        operands=[ht] + [a for a, _ in sources])


def _wgrad(name, at, b, tm, tn, out_shape, out_block, out_index, comm=None):
    m, seq = at.shape
    n = b.shape[1]

    def body(a_ref, b_ref, o_ref):
        o_ref[...] = _dot(a_ref[...], b_ref[...]).astype(o_ref.dtype)

    return _call(
        body, comm, name=name, grid=(n // tn, m // tm),
        in_specs=[pl.BlockSpec((tm, seq), lambda j, i: (i, 0)), pl.BlockSpec((seq, tn), lambda j, i: (0, j))],
        out_specs=[pl.BlockSpec(out_block, lambda j, i: out_index(i, j))],
        out_shape=[jax.ShapeDtypeStruct(out_shape, MXU_DTYPE)],
        operands=[at, b])


def _eye_g():
    return jnp.eye(HALF_G, dtype=F32)


def _bb_blockdiag(bbt_r, bbt_i):
    n_half = bbt_r.shape[0] // HALF_G

    def one(t):
        t = t.reshape(n_half, HALF_G, SSM_GROUP, SSM_STATE)
        t = t[:, :, :, None, :] * _eye_g()[None, :, None, :, None]
        return t.reshape(n_half, HALF_CH, HALF_W)

    return jnp.concatenate([one(bbt_r), one(bbt_i)], axis=-1)


def _cc_blockdiag(c_re, c_im):
    n_half = c_re.shape[0] // HALF_G

    def one(t):
        t = t.reshape(n_half, HALF_G, SSM_GROUP, SSM_STATE)
        t = jnp.transpose(t, (0, 3, 1, 2))
        t = t[:, None, :, :, :] * _eye_g()[None, :, None, :, None]
        return t.reshape(n_half, HALF_W, HALF_CH)

    return jnp.concatenate([one(c_re), one(-c_im)], axis=1)


def _bb_diag(dbb):
    n_half = dbb.shape[0]
    t = dbb.reshape(n_half, HALF_G, SSM_GROUP, 2, HALF_G, SSM_STATE)
    t = jnp.sum(t * _eye_g()[None, :, None, None, :, None], axis=4)
    t = jnp.transpose(t, (3, 0, 1, 2, 4))
    return t.reshape(2, n_half * HALF_G, SSM_GROUP, SSM_STATE)


def _cc_diag(dcc):
    n_half = dcc.shape[0]
    t = dcc.reshape(n_half, 2, HALF_G, SSM_STATE, HALF_G, SSM_GROUP)
    t = jnp.sum(t * _eye_g()[None, None, :, None, :, None], axis=2)
    t = jnp.transpose(t, (1, 0, 3, 4, 2))
    return t.reshape(2, n_half * HALF_G, SSM_GROUP, SSM_STATE)


def _unpermute_rows(a):
    seq, d = a.shape
    return a.reshape(seq // N_CHUNK, N_CHUNK, d).transpose(1, 0, 2).reshape(seq, d)


def _pack_rows(shape):
    return -(-math.prod(shape) // (8 * LANES)) * 8


def _pack(parts, dtype=F32):
    rows = []
    for p in parts:
        flat = p.reshape(-1).astype(dtype)
        rows.append(jnp.pad(flat, (0, _pack_rows(p.shape) * LANES - flat.shape[0])).reshape(-1, LANES))
    return jnp.concatenate(rows, axis=0)


def _unpack(packed, shapes):
    out, o = [], 0
    for s in shapes:
        n = _pack_rows(s)
        out.append(packed[o:o + n].reshape(-1)[:math.prod(s)].reshape(s))
        o += n
    return out


def kernel(x, norm_pre_g, w_in, conv_w, conv_b, ssm_a_re, ssm_a_im, ssm_log_dt, ssm_b_re, ssm_b_im, ssm_c_re, ssm_c_im, ssm_d, w_glu, b_glu, w_out, norm_post_g, loss_target, m_norm_pre_g, m_w_in, m_conv_w, m_conv_b, m_ssm_a_re, m_ssm_a_im, m_ssm_log_dt, m_ssm_b_re, m_ssm_b_im, m_ssm_c_re, m_ssm_c_im, m_ssm_d, m_w_glu, m_b_glu, m_w_out, m_norm_post_g, v_norm_pre_g, v_w_in, v_conv_w, v_conv_b, v_ssm_a_re, v_ssm_a_im, v_ssm_log_dt, v_ssm_b_re, v_ssm_b_im, v_ssm_c_re, v_ssm_c_im, v_ssm_d, v_w_glu, v_b_glu, v_w_out, v_norm_post_g):
    seq, d_model = x.shape[1], x.shape[2]
    d_conv, d_ssm = conv_b.shape[0], ssm_d.shape[0]
    groups, states = ssm_a_re.shape
    assert x.shape[0] == 1 and seq % (8 * N_CHUNK) == 0 and d_conv == d_ssm
    assert (groups, states) == (d_ssm // SSM_GROUP, SSM_STATE) and d_ssm % LANES == 0
    me = 4 * lax.axis_index("x") + 2 * lax.axis_index("y") + lax.axis_index("c")
    tm = min(512, seq)

    x3 = x[0].reshape(N_CHUNK, seq // N_CHUNK, d_model)
    t3 = loss_target[0].reshape(N_CHUNK, seq // N_CHUNK, d_model)
    row = lambda a: a.reshape(1, -1)
    conv_w8 = jnp.pad(conv_w, ((0, 8 - conv_w.shape[0]), (0, 0)))

    g3 = lambda a: a.reshape(groups, 1, -1)
    bt_re, bt_im = jnp.transpose(ssm_b_re, (0, 2, 1)), jnp.transpose(ssm_b_im, (0, 2, 1))
    lbr, lbi, qr, qi, bbt_r, bbt_i = _ssm_prep(g3(ssm_a_re), g3(ssm_a_im), g3(ssm_log_dt), bt_re, bt_im)
    n_half = groups // HALF_G
    lam = jnp.stack([lbr.reshape(n_half, HALF_W), lbi.reshape(n_half, HALF_W)], axis=1)
    bbcat = _bb_blockdiag(bbt_r, bbt_i).astype(MXU_DTYPE)
    cccat = _cc_blockdiag(ssm_c_re, ssm_c_im).astype(MXU_DTYPE)

    xp, h, ht = _norm_in(x3, row(norm_pre_g), tm)
    nc = w_in.shape[1]
    w_pieces = jnp.transpose(w_in.astype(MXU_DTYPE).reshape(d_model, nc // PIECE_COLS, PIECE_COLS), (1, 0, 2))
    proj, win_g, (convw_g, wglu_g) = _fwd_in(h, w_pieces, me, _Comm([conv_w8, w_glu.astype(MXU_DTYPE)]),
                                             seq)
    conv_w_full = jnp.transpose(convw_g, (1, 0, 2)).reshape(8, d_conv)
    u_col0, zs_col0 = 4 * d_conv, 4 * d_conv + d_ssm
    yconv = _conv_fwd(proj, conv_w_full, row(conv_b), d_conv)
    (yp,), (wout_g,) = _ssm_fwd(proj, lam, bbcat, cccat, row(ssm_d), d_ssm, u_col0,
                                _Comm([w_out.astype(MXU_DTYPE)]))
    w_out_full = wout_g.reshape(-1, d_model)
    w_glu_full = wglu_g.reshape(-1, d_ssm)
    (dy, d_o, mixt, dyc, dyp, dzs, ygt, dq, loss_part, dgpost, dbglu) = _tail(
        xp, t3, proj, yconv, yp, w_glu_full, row(b_glu), w_out_full, row(norm_post_g), zs_col0, min(256, seq))

    r_out, r_glu, nc = w_out.shape[0], w_glu.shape[0], w_in.shape[1]
    (dwout_p,), _ = _wgrad("dw_out", mixt, d_o, r_out, min(1024, d_model), (N_DEV, r_out, d_model),
                           (None, r_out, min(1024, d_model)), lambda i, j: (i, 0, j))
    (dwglu_p,), _ = _wgrad("dw_glu", ygt, dq, r_glu, d_ssm, (N_DEV, r_glu, d_ssm),
                           (None, r_glu, d_ssm), lambda i, j: (i, 0, 0))
    (d4, dconvb, dconvw), (recv_glu,) = _conv_bwd(proj, dyc, conv_w_full, row(conv_b), d_conv,
                                                  _Comm([], [dwglu_p]))
    late = [k for k in range(N_DEV) if k * nc < u_col0 + d_ssm and (k + 1) * nc > u_col0]
    early = [k for k in range(N_DEV) if k not in late]
    gr = math.gcd(nc, d_conv)
    granules = lambda blocks: [g for k in blocks for g in range(k * nc // gr, (k + 1) * nc // gr)]
    tmw = min(1024, d_model)
    (dwin_e,), (recv_out,) = _dw_in("dw_in_early", ht, d4, None, dzs, granules(early), gr, nc, tmw,
                                    _Comm([], [dwout_p]))
    (du, dbb, dcc, da, dd), (recv_in,) = _ssm_bwd(
        proj, dyp, lam, bbcat, cccat, row(ssm_d), d_ssm, u_col0, _Comm([], [dwin_e], dests={0: early}))
    parts_mx = [_bb_diag(dbb), _cc_diag(dcc)]
    (dwin_l,), (pack_mx_g,) = _dw_in("dw_in_late", ht, d4, du, dzs, granules(late), gr, nc, tmw,
                                     _Comm([_pack(parts_mx, MXU_DTYPE)]))
    da_n = jnp.transpose(da.reshape(n_half, 2, HALF_G, SSM_STATE), (1, 0, 2, 3)).reshape(2, groups, 1, states)
    parts = [dgpost, dconvb, dd, dbglu, dconvw[:3], da_n, loss_part]
    shapes, shapes_mx = [p.shape for p in parts], [p.shape for p in parts_mx]
    (gx_p, dgpre), (pack_g, recv_in) = _bwd_in(
        d4, du, dzs, gr, win_g, xp, dy, row(norm_pre_g),
        _Comm([_pack(parts)], [dwin_l], dests={1: late}, into={1: recv_in}), min(256, seq))
    (last_g,) = _exchange("reduce_last", [_pack([dgpre])], [])
    (g_gpost, g_convb, g_d, g_bglu, g_convw, g_da, loss_sum) = _unpack(_sum_slots("sum_pack", pack_g), shapes)
    (g_dbb, g_dcc) = _unpack(_sum_slots("sum_pack_mx", pack_mx_g), shapes_mx)
    (g_gpre,) = _unpack(_sum_slots("sum_last", last_g), [dgpre.shape])
    g_convw = lax.dynamic_slice(g_convw, (0, me * conv_w.shape[1]), conv_w.shape)

    tr = lambda a: jnp.transpose(a, (0, 2, 1))
    direct = [(g_gpre, row(norm_pre_g), row(m_norm_pre_g), row(v_norm_pre_g)),
              (g_convb, row(conv_b), row(m_conv_b), row(v_conv_b)),
              (g_d, row(ssm_d), row(m_ssm_d), row(v_ssm_d)),
              (g_bglu, row(b_glu), row(m_b_glu), row(v_b_glu)),
              (g_gpost, row(norm_post_g), row(m_norm_post_g), row(v_norm_post_g)),
              (g_convw, conv_w, m_conv_w, v_conv_w),
              (g_dcc[0], ssm_c_re, m_ssm_c_re, v_ssm_c_re),
              (-g_dcc[1], ssm_c_im, m_ssm_c_im, v_ssm_c_im)]
    ssm = dict(da_r=g_da[0], da_i=g_da[1], dbb_r=g_dbb[0], dbb_i=g_dbb[1], lr=g3(ssm_a_re), li=g3(ssm_a_im),
               ldt=g3(ssm_log_dt), bt_r=bt_re, bt_i=bt_im, lbr=lbr, lbi=lbi, qr=qr, qi=qi,
               w_a_re=g3(ssm_a_re), m_a_re=g3(m_ssm_a_re), v_a_re=g3(v_ssm_a_re),
               w_a_im=g3(ssm_a_im), m_a_im=g3(m_ssm_a_im), v_a_im=g3(v_ssm_a_im),
               w_log_dt=g3(ssm_log_dt), m_log_dt=g3(m_ssm_log_dt), v_log_dt=g3(v_ssm_log_dt),
               w_bt_re=bt_re, m_bt_re=tr(m_ssm_b_re), v_bt_re=tr(v_ssm_b_re),
               w_bt_im=bt_im, m_bt_im=tr(m_ssm_b_im), v_bt_im=tr(v_ssm_b_im))
    small = _small_update(direct, ssm)
    res = {}
    for name, quad, shape in zip(["norm_pre_g", "conv_b", "ssm_d", "b_glu", "norm_post_g", "conv_w", "ssm_c_re", "ssm_c_im"],
                                 small[:8], [norm_pre_g.shape, conv_b.shape, ssm_d.shape, b_glu.shape,
                                             norm_post_g.shape, conv_w.shape, ssm_c_re.shape, ssm_c_im.shape]):
        res[name] = tuple(a.reshape(shape) for a in quad)
    res["ssm_a_re"] = tuple(a.reshape(ssm_a_re.shape) for a in small[8])
    res["ssm_a_im"] = tuple(a.reshape(ssm_a_im.shape) for a in small[9])
    res["ssm_log_dt"] = tuple(a.reshape(ssm_log_dt.shape) for a in small[10])
    res["ssm_b_re"] = tuple(tr(a) for a in small[11])
    res["ssm_b_im"] = tuple(tr(a) for a in small[12])
    res["w_in"] = tuple(_adam_big("adam_w_in", recv_in, w_in, m_w_in, v_w_in, min(256, d_model)))
    res["w_out"] = tuple(_adam_big("adam_w_out", recv_out, w_out, m_w_out, v_w_out, min(128, r_out)))
    res["w_glu"] = tuple(_adam_big("adam_w_glu", recv_glu, w_glu, m_w_glu, v_w_glu, r_glu))

    order = ["norm_pre_g", "w_in", "conv_w", "conv_b", "ssm_a_re", "ssm_a_im", "ssm_log_dt", "ssm_b_re", "ssm_b_im",
             "ssm_c_re", "ssm_c_im", "ssm_d", "w_glu", "b_glu", "w_out", "norm_post_g"]
    loss = loss_sum[0, 0]
    grad_x = _unpermute_rows(gx_p)[None]
    return (loss, grad_x, *[res[n][0] for n in order], *[res[n][1] for n in order],
            *[res[n][2] for n in order], *[res[n][3] for n in order])
```
